```python
import jax, jax.numpy as jnp
from jax import lax
import numpy as np

D_MODEL = 1024
BATCH = 8
SEQ = 8192
DEPTH = 4

CHUNK = 64
EPS = 1e-5
CONV_WIDTH = D_MODEL
CONV_GROUPS = 16
SHORT_K = 3
SSD_HEAD_DIM = 64
SSD_HEADS = D_MODEL // SSD_HEAD_DIM
SSD_INNER = SSD_HEADS * SSD_HEAD_DIM
SSD_GROUPS = 2
SSD_STATE = 128
SSD_CONV_K = 4
SSD_CONV_DIM = SSD_INNER + 2 * SSD_GROUPS * SSD_STATE
MIX_WIDTH = CONV_WIDTH + SSD_INNER
D_FF = 4 * D_MODEL
IN_COLS = 3 * CONV_WIDTH + SSD_INNER + SSD_CONV_DIM + SSD_HEADS

kernel_name = "hybrid_shortconv_ssd_trunk"


def rmsnorm(x, w):
    xf = x.astype(jnp.float32)
    y = xf * lax.rsqrt(jnp.mean(xf * xf, axis=-1, keepdims=True) + EPS)
    return (y * w.astype(jnp.float32)).astype(x.dtype)


def causal_dwconv(u, w, b=None):
    k, c = w.shape
    y = lax.conv_general_dilated(
        u, w[:, None, :], window_strides=(1,), padding=[(k - 1, 0)],
        dimension_numbers=("NWC", "WIO", "NWC"), feature_group_count=c)
    if b is not None:
        y = y + b
    return y


def short_conv_mixer(u_b, u_c, u_h, conv_w):
    return u_b * causal_dwconv(u_c * u_h, conv_w)


def ssd_scan(xs, dt, a_head, bm, cm):
    f32 = jnp.float32
    b, t, h, p = xs.shape
    g, n = bm.shape[2], bm.shape[3]
    r = h // g
    nc = t // CHUNK
    x_c = (xs.astype(f32) * dt[..., None]).reshape(b, nc, CHUNK, g, r, p)
    a_c = (dt * a_head).reshape(b, nc, CHUNK, g, r)
    b_c = bm.astype(f32).reshape(b, nc, CHUNK, g, n)
    c_c = cm.astype(f32).reshape(b, nc, CHUNK, g, n)
    a_cum = jnp.cumsum(a_c, axis=2)
    causal = jnp.tril(jnp.ones((CHUNK, CHUNK), dtype=bool))
    seg = a_cum[:, :, :, None] - a_cum[:, :, None, :]
    decay = jnp.exp(jnp.where(causal[None, None, :, :, None, None], seg, -jnp.inf))
    scores = jnp.einsum("bclgn,bcsgn->bclsg", c_c, b_c)
    y_diag = jnp.einsum("bclsgr,bcsgrp->bclgrp", scores[..., None] * decay, x_c)
    decay_end = jnp.exp(a_cum[:, :, -1:] - a_cum)
    states = jnp.einsum("bclgn,bclgrp->bcgrpn", b_c, x_c * decay_end[..., None])
    chunk_decay = jnp.exp(a_cum[:, :, -1])

    def step(hs, inp):
        s_c, d_c = inp
        return hs * d_c[..., None, None] + s_c, hs

    h0 = jnp.zeros((b, g, r, p, n), dtype=f32)
    _, prev = lax.scan(step, h0, (jnp.moveaxis(states, 1, 0), jnp.moveaxis(chunk_decay, 1, 0)))
    prev = jnp.moveaxis(prev, 0, 1)
    y_off = jnp.einsum("bclgn,bcgrpn->bclgrp", c_c, prev) * jnp.exp(a_cum)[..., None]
    return (y_diag + y_off).reshape(b, t, h, p)


def ssd_mixer(z, xbc, dt_raw, conv_w, conv_b, dt_bias, a_log, d_skip, norm_w):
    b, t, _ = z.shape
    xbc = jax.nn.silu(causal_dwconv(xbc, conv_w, conv_b))
    xs, bm, cm = jnp.split(xbc, [SSD_INNER, SSD_INNER + SSD_GROUPS * SSD_STATE], axis=-1)
    xs = xs.reshape(b, t, SSD_HEADS, SSD_HEAD_DIM)
    bm = bm.reshape(b, t, SSD_GROUPS, SSD_STATE)
    cm = cm.reshape(b, t, SSD_GROUPS, SSD_STATE)
    dt = jax.nn.softplus(dt_raw.astype(jnp.float32) + dt_bias.astype(jnp.float32))
    a_head = -jnp.exp(a_log.astype(jnp.float32))
    y = ssd_scan(xs, dt, a_head, bm, cm)
    y = y + d_skip.astype(jnp.float32)[:, None] * xs.astype(jnp.float32)
    y = y.reshape(b, t, SSD_INNER).astype(z.dtype)
    gated = (y * jax.nn.silu(z)).reshape(b, t, SSD_GROUPS, SSD_INNER // SSD_GROUPS)
    gated = rmsnorm(gated, norm_w.reshape(SSD_GROUPS, SSD_INNER // SSD_GROUPS))
    return gated.reshape(b, t, SSD_INNER)


SPLITS = list(np.cumsum([CONV_WIDTH, CONV_WIDTH, CONV_WIDTH, SSD_INNER, SSD_CONV_DIM]))


def hybrid_layer(x, norm_mix_w, w_in, short_conv_w, ssd_conv_w, ssd_conv_b, dt_bias,
                 a_log, d_skip, ssd_norm_w, w_out, norm_mlp_w, w_up, w_down):
    h = rmsnorm(x, norm_mix_w)
    proj = jnp.einsum("btd,dc->btc", h, w_in)
    u_b, u_c, u_h, z, xbc, dt_raw = jnp.split(proj, SPLITS, axis=-1)
    y_a = short_conv_mixer(u_b, u_c, u_h, short_conv_w)
    y_b = ssd_mixer(z, xbc, dt_raw, ssd_conv_w, ssd_conv_b, dt_bias, a_log, d_skip, ssd_norm_w)
    y = jnp.concatenate([y_a, y_b], axis=-1)
    x = x + jnp.einsum("btc,cd->btd", y, w_out)
    h = rmsnorm(x, norm_mlp_w)
    hid = jnp.square(jax.nn.relu(jnp.einsum("btd,df->btf", h, w_up)))
    return x + jnp.einsum("btf,fd->btd", hid, w_down)


def _fwd_setup_inputs(seed: int = 0) -> dict:
    key = jax.random.key(seed)
    ks = jax.random.split(key, 16)
    f32 = jnp.float32
    nrm = lambda k, shape, s: jax.random.normal(k, shape, f32) * s
    gain = lambda k, shape: 1.0 + 0.02 * jax.random.normal(k, shape, f32)
    dt0 = jnp.exp(jax.random.uniform(ks[6], (DEPTH, SSD_HEADS), f32, math_log(1e-3), math_log(1e-1)))
    dt_bias = dt0 + jnp.log(-jnp.expm1(-dt0))
    return {
        "x": jax.random.normal(ks[0], (BATCH, SEQ, D_MODEL), f32),
        "norm_mix_w": gain(ks[1], (DEPTH, D_MODEL)),
        "w_in": nrm(ks[2], (DEPTH, D_MODEL, IN_COLS), D_MODEL ** -0.5),
        "short_conv_w": nrm(ks[3], (DEPTH, SHORT_K, CONV_WIDTH), SHORT_K ** -0.5),
        "ssd_conv_w": nrm(ks[4], (DEPTH, SSD_CONV_K, SSD_CONV_DIM), SSD_CONV_K ** -0.5),
        "ssd_conv_b": nrm(ks[5], (DEPTH, SSD_CONV_DIM), 0.02),
        "dt_bias": dt_bias,
        "a_log": jnp.log(jax.random.uniform(ks[7], (DEPTH, SSD_HEADS), f32, 1.0, 16.0)),
        "d_skip": gain(ks[8], (DEPTH, SSD_HEADS)),
        "ssd_norm_w": gain(ks[9], (DEPTH, SSD_INNER)),
        "w_out": nrm(ks[10], (DEPTH, MIX_WIDTH, D_MODEL), MIX_WIDTH ** -0.5),
        "norm_mlp_w": gain(ks[11], (DEPTH, D_MODEL)),
        "w_up": nrm(ks[12], (DEPTH, D_MODEL, D_FF), D_MODEL ** -0.5),
        "w_down": nrm(ks[13], (DEPTH, D_FF, D_MODEL), D_FF ** -0.5),
        "final_norm_w": gain(ks[14], (D_MODEL,)),
    }


def math_log(v):
    return float(np.log(v))


def _fwd_reference(x, norm_mix_w, w_in, short_conv_w, ssd_conv_w, ssd_conv_b, dt_bias, a_log,
              d_skip, ssd_norm_w, w_out, norm_mlp_w, w_up, w_down, final_norm_w):
    for i in range(DEPTH):
        x = hybrid_layer(x, norm_mix_w[i], w_in[i], short_conv_w[i], ssd_conv_w[i],
                         ssd_conv_b[i], dt_bias[i], a_log[i], d_skip[i], ssd_norm_w[i],
                         w_out[i], norm_mlp_w[i], w_up[i], w_down[i])
    return rmsnorm(x, final_norm_w)


import jax as _jax
import jax.numpy as _jnp

TWIN_FORMAT = 'train_step'
FWD_PARAMS = ['x', 'norm_mix_w', 'w_in', 'short_conv_w', 'ssd_conv_w', 'ssd_conv_b', 'dt_bias', 'a_log', 'd_skip', 'ssd_norm_w', 'w_out', 'norm_mlp_w', 'w_up', 'w_down', 'final_norm_w']
TWIN_WEIGHTS = ['norm_mix_w', 'w_in', 'short_conv_w', 'ssd_conv_w', 'ssd_conv_b', 'dt_bias', 'a_log', 'd_skip', 'ssd_norm_w', 'w_out', 'norm_mlp_w', 'w_up', 'w_down', 'final_norm_w']
TWIN_DIFF_INPUT = 'x'
TWIN_INPUTS = ['x', 'norm_mix_w', 'w_in', 'short_conv_w', 'ssd_conv_w', 'ssd_conv_b', 'dt_bias', 'a_log', 'd_skip', 'ssd_norm_w', 'w_out', 'norm_mlp_w', 'w_up', 'w_down', 'final_norm_w', 'loss_target', 'm_norm_mix_w', 'm_w_in', 'm_short_conv_w', 'm_ssd_conv_w', 'm_ssd_conv_b', 'm_dt_bias', 'm_a_log', 'm_d_skip', 'm_ssd_norm_w', 'm_w_out', 'm_norm_mlp_w', 'm_w_up', 'm_w_down', 'm_final_norm_w', 'v_norm_mix_w', 'v_w_in', 'v_short_conv_w', 'v_ssd_conv_w', 'v_ssd_conv_b', 'v_dt_bias', 'v_a_log', 'v_d_skip', 'v_ssd_norm_w', 'v_w_out', 'v_norm_mlp_w', 'v_w_up', 'v_w_down', 'v_final_norm_w']
TWIN_OUTPUTS = ['loss', 'grad_x', 'grad_norm_mix_w', 'grad_w_in', 'grad_short_conv_w', 'grad_ssd_conv_w', 'grad_ssd_conv_b', 'grad_dt_bias', 'grad_a_log', 'grad_d_skip', 'grad_ssd_norm_w', 'grad_w_out', 'grad_norm_mlp_w', 'grad_w_up', 'grad_w_down', 'grad_final_norm_w', 'delta_norm_mix_w', 'delta_w_in', 'delta_short_conv_w', 'delta_ssd_conv_w', 'delta_ssd_conv_b', 'delta_dt_bias', 'delta_a_log', 'delta_d_skip', 'delta_ssd_norm_w', 'delta_w_out', 'delta_norm_mlp_w', 'delta_w_up', 'delta_w_down', 'delta_final_norm_w', 'new_m_norm_mix_w', 'new_m_w_in', 'new_m_short_conv_w', 'new_m_ssd_conv_w', 'new_m_ssd_conv_b', 'new_m_dt_bias', 'new_m_a_log', 'new_m_d_skip', 'new_m_ssd_norm_w', 'new_m_w_out', 'new_m_norm_mlp_w', 'new_m_w_up', 'new_m_w_down', 'new_m_final_norm_w', 'new_v_norm_mix_w', 'new_v_w_in', 'new_v_short_conv_w', 'new_v_ssd_conv_w', 'new_v_ssd_conv_b', 'new_v_dt_bias', 'new_v_a_log', 'new_v_d_skip', 'new_v_ssd_norm_w', 'new_v_w_out', 'new_v_norm_mlp_w', 'new_v_w_up', 'new_v_w_down', 'new_v_final_norm_w']
TWIN_LEAF_KINDS = {'loss': 'loss', 'grad_x': 'grad_x', 'grad_norm_mix_w': 'grad_w', 'grad_w_in': 'grad_w', 'grad_short_conv_w': 'grad_w', 'grad_ssd_conv_w': 'grad_w', 'grad_ssd_conv_b': 'grad_w', 'grad_dt_bias': 'grad_w', 'grad_a_log': 'grad_w', 'grad_d_skip': 'grad_w', 'grad_ssd_norm_w': 'grad_w', 'grad_w_out': 'grad_w', 'grad_norm_mlp_w': 'grad_w', 'grad_w_up': 'grad_w', 'grad_w_down': 'grad_w', 'grad_final_norm_w': 'grad_w', 'delta_norm_mix_w': 'delta_w', 'delta_w_in': 'delta_w', 'delta_short_conv_w': 'delta_w', 'delta_ssd_conv_w': 'delta_w', 'delta_ssd_conv_b': 'delta_w', 'delta_dt_bias': 'delta_w', 'delta_a_log': 'delta_w', 'delta_d_skip': 'delta_w', 'delta_ssd_norm_w': 'delta_w', 'delta_w_out': 'delta_w', 'delta_norm_mlp_w': 'delta_w', 'delta_w_up': 'delta_w', 'delta_w_down': 'delta_w', 'delta_final_norm_w': 'delta_w', 'new_m_norm_mix_w': 'new_m', 'new_m_w_in': 'new_m', 'new_m_short_conv_w': 'new_m', 'new_m_ssd_conv_w': 'new_m', 'new_m_ssd_conv_b': 'new_m', 'new_m_dt_bias': 'new_m', 'new_m_a_log': 'new_m', 'new_m_d_skip': 'new_m', 'new_m_ssd_norm_w': 'new_m', 'new_m_w_out': 'new_m', 'new_m_norm_mlp_w': 'new_m', 'new_m_w_up': 'new_m', 'new_m_w_down': 'new_m', 'new_m_final_norm_w': 'new_m', 'new_v_norm_mix_w': 'new_v', 'new_v_w_in': 'new_v', 'new_v_short_conv_w': 'new_v', 'new_v_ssd_conv_w': 'new_v', 'new_v_ssd_conv_b': 'new_v', 'new_v_dt_bias': 'new_v', 'new_v_a_log': 'new_v', 'new_v_d_skip': 'new_v', 'new_v_ssd_norm_w': 'new_v', 'new_v_w_out': 'new_v', 'new_v_norm_mlp_w': 'new_v', 'new_v_w_up': 'new_v', 'new_v_w_down': 'new_v', 'new_v_final_norm_w': 'new_v'}


def _forward(args):
    return _fwd_reference(*[args[k] for k in FWD_PARAMS])


def _output_shape():
    def fwd():
        inp = _fwd_setup_inputs(0)
        return _fwd_reference(*[inp[k] for k in FWD_PARAMS])
    out = _jax.eval_shape(fwd)
    return out.shape, out.dtype

N_MICROBATCH = 1
ADAM_LR = 0.001
ADAM_B1 = 0.9
ADAM_B2 = 0.999
ADAM_EPS = 1e-08
ADAM_WD = 0.01
ADAM_STEP = 10
PER_EXAMPLE_BATCH_AXIS = {'x': 0, 'loss_target': 0}
SHARED_INPUTS = []
_WEIGHT_DTYPES = {'norm_mix_w': _jnp.float32, 'w_in': _jnp.float32, 'short_conv_w': _jnp.float32, 'ssd_conv_w': _jnp.float32, 'ssd_conv_b': _jnp.float32, 'dt_bias': _jnp.float32, 'a_log': _jnp.float32, 'd_skip': _jnp.float32, 'ssd_norm_w': _jnp.float32, 'w_out': _jnp.float32, 'norm_mlp_w': _jnp.float32, 'w_up': _jnp.float32, 'w_down': _jnp.float32, 'final_norm_w': _jnp.float32}
MOMENT_SCALE = {'norm_mix_w': 2.995554e-01, 'w_in': 1.233117e-01, 'short_conv_w': 1.274932e-01, 'ssd_conv_w': 1.094421e-01, 'ssd_conv_b': 1.516415e-01, 'dt_bias': 4.029215e-01, 'a_log': 9.462588e-01, 'd_skip': 4.660455e-01, 'ssd_norm_w': 1.280568e-01, 'w_out': 1.783356e-01, 'norm_mlp_w': 1.910068e-01, 'w_up': 9.603974e-02, 'w_down': 1.882485e-01, 'final_norm_w': 6.497476e+01}


def _to_microbatches(a, axis):
    t = _jnp.moveaxis(a, axis, 0)
    t = t.reshape((N_MICROBATCH, t.shape[0] // N_MICROBATCH) + t.shape[1:])
    return _jnp.moveaxis(t, 1, axis + 1)


def setup_inputs(seed: int = 0) -> dict:
    inp = _fwd_setup_inputs(seed)
    key = _jax.random.fold_in(_jax.random.key(seed), 7919)
    shape, _ = _output_shape()
    out = dict(inp)
    out["loss_target"] = _jax.random.normal(_jax.random.fold_in(key, 0), shape, _jnp.float32)
    for i, name in enumerate(TWIN_WEIGHTS):
        w = inp[name].astype(_jnp.float32)
        if MOMENT_SCALE is None:
            s = _jnp.sqrt(_jnp.mean(_jnp.square(w)) + 1e-30)
        else:
            s = MOMENT_SCALE[name]
        km, kv = _jax.random.split(_jax.random.fold_in(key, i + 1))
        out[name] = w
        out["m_" + name] = s * _jax.random.normal(km, w.shape, _jnp.float32)
        out["v_" + name] = (s * s) * _jax.random.uniform(kv, w.shape, _jnp.float32, 0.5, 1.5)
    if N_MICROBATCH > 1:
        for name, axis in PER_EXAMPLE_BATCH_AXIS.items():
            out[name] = _to_microbatches(out[name], axis)
    return {'x': out['x'], 'norm_mix_w': out['norm_mix_w'], 'w_in': out['w_in'], 'short_conv_w': out['short_conv_w'], 'ssd_conv_w': out['ssd_conv_w'], 'ssd_conv_b': out['ssd_conv_b'], 'dt_bias': out['dt_bias'], 'a_log': out['a_log'], 'd_skip': out['d_skip'], 'ssd_norm_w': out['ssd_norm_w'], 'w_out': out['w_out'], 'norm_mlp_w': out['norm_mlp_w'], 'w_up': out['w_up'], 'w_down': out['w_down'], 'final_norm_w': out['final_norm_w'], 'loss_target': out['loss_target'], 'm_norm_mix_w': out['m_norm_mix_w'], 'm_w_in': out['m_w_in'], 'm_short_conv_w': out['m_short_conv_w'], 'm_ssd_conv_w': out['m_ssd_conv_w'], 'm_ssd_conv_b': out['m_ssd_conv_b'], 'm_dt_bias': out['m_dt_bias'], 'm_a_log': out['m_a_log'], 'm_d_skip': out['m_d_skip'], 'm_ssd_norm_w': out['m_ssd_norm_w'], 'm_w_out': out['m_w_out'], 'm_norm_mlp_w': out['m_norm_mlp_w'], 'm_w_up': out['m_w_up'], 'm_w_down': out['m_w_down'], 'm_final_norm_w': out['m_final_norm_w'], 'v_norm_mix_w': out['v_norm_mix_w'], 'v_w_in': out['v_w_in'], 'v_short_conv_w': out['v_short_conv_w'], 'v_ssd_conv_w': out['v_ssd_conv_w'], 'v_ssd_conv_b': out['v_ssd_conv_b'], 'v_dt_bias': out['v_dt_bias'], 'v_a_log': out['v_a_log'], 'v_d_skip': out['v_d_skip'], 'v_ssd_norm_w': out['v_ssd_norm_w'], 'v_w_out': out['v_w_out'], 'v_norm_mlp_w': out['v_norm_mlp_w'], 'v_w_up': out['v_w_up'], 'v_w_down': out['v_w_down'], 'v_final_norm_w': out['v_final_norm_w']}


def _loss(weights, diff, rest, loss_target):
    with _jax.named_scope("forward"):
        args = {**rest, TWIN_DIFF_INPUT: diff, **{k: w.astype(_WEIGHT_DTYPES[k]) for k, w in weights.items()}}
        y = _forward(args)
    with _jax.named_scope("loss_head"):
        err = _jnp.square(y.astype(_jnp.float32) - loss_target)
        return 0.5 * _jnp.sum(_jnp.mean(err, axis=-1)) if err.ndim else 0.5 * err


def _adamw(w, g, m, v):
    m = ADAM_B1 * m + (1.0 - ADAM_B1) * g
    v = ADAM_B2 * v + (1.0 - ADAM_B2) * _jnp.square(g)
    m_hat = m / (1.0 - ADAM_B1 ** ADAM_STEP)
    v_hat = v / (1.0 - ADAM_B2 ** ADAM_STEP)
    delta = -ADAM_LR * (m_hat / (_jnp.sqrt(v_hat) + ADAM_EPS) + ADAM_WD * w)
    return delta, m, v


def reference(x, norm_mix_w, w_in, short_conv_w, ssd_conv_w, ssd_conv_b, dt_bias, a_log, d_skip, ssd_norm_w, w_out, norm_mlp_w, w_up, w_down, final_norm_w, loss_target, m_norm_mix_w, m_w_in, m_short_conv_w, m_ssd_conv_w, m_ssd_conv_b, m_dt_bias, m_a_log, m_d_skip, m_ssd_norm_w, m_w_out, m_norm_mlp_w, m_w_up, m_w_down, m_final_norm_w, v_norm_mix_w, v_w_in, v_short_conv_w, v_ssd_conv_w, v_ssd_conv_b, v_dt_bias, v_a_log, v_d_skip, v_ssd_norm_w, v_w_out, v_norm_mlp_w, v_w_up, v_w_down, v_final_norm_w):
    given = dict(x=x, norm_mix_w=norm_mix_w, w_in=w_in, short_conv_w=short_conv_w, ssd_conv_w=ssd_conv_w, ssd_conv_b=ssd_conv_b, dt_bias=dt_bias, a_log=a_log, d_skip=d_skip, ssd_norm_w=ssd_norm_w, w_out=w_out, norm_mlp_w=norm_mlp_w, w_up=w_up, w_down=w_down, final_norm_w=final_norm_w, loss_target=loss_target, m_norm_mix_w=m_norm_mix_w, m_w_in=m_w_in, m_short_conv_w=m_short_conv_w, m_ssd_conv_w=m_ssd_conv_w, m_ssd_conv_b=m_ssd_conv_b, m_dt_bias=m_dt_bias, m_a_log=m_a_log, m_d_skip=m_d_skip, m_ssd_norm_w=m_ssd_norm_w, m_w_out=m_w_out, m_norm_mlp_w=m_norm_mlp_w, m_w_up=m_w_up, m_w_down=m_w_down, m_final_norm_w=m_final_norm_w, v_norm_mix_w=v_norm_mix_w, v_w_in=v_w_in, v_short_conv_w=v_short_conv_w, v_ssd_conv_w=v_ssd_conv_w, v_ssd_conv_b=v_ssd_conv_b, v_dt_bias=v_dt_bias, v_a_log=v_a_log, v_d_skip=v_d_skip, v_ssd_norm_w=v_ssd_norm_w, v_w_out=v_w_out, v_norm_mlp_w=v_norm_mlp_w, v_w_up=v_w_up, v_w_down=v_w_down, v_final_norm_w=v_final_norm_w)
    weights = {n: given[n] for n in TWIN_WEIGHTS}
    shared = {n: given[n] for n in SHARED_INPUTS}
    per_example = {n: given[n] for n in ['x']}
    grad_fn = _jax.value_and_grad(_loss, argnums=(0, 1))

    def one_microbatch(ex, loss_target):
        ex = dict(ex)
        diff = ex.pop(TWIN_DIFF_INPUT)
        return grad_fn(weights, diff, {**shared, **ex}, loss_target)

    if N_MICROBATCH == 1:
        loss, (grad_w, grad_x) = one_microbatch(per_example, given["loss_target"])
    else:
        def body(carry, xs):
            loss_sum, grad_sum = carry
            l_k, (gw_k, gx_k) = one_microbatch(xs[0], xs[1])
            with _jax.named_scope("update"):
                return (loss_sum + l_k, _jax.tree.map(_jnp.add, grad_sum, gw_k)), gx_k

        init = (_jnp.zeros((), _jnp.float32), _jax.tree.map(_jnp.zeros_like, weights))
        (loss, grad_w), grad_x = _jax.lax.scan(body, init, (per_example, given["loss_target"]))
    with _jax.named_scope("update"):
        delta_w, new_m, new_v = {}, {}, {}
        for n in TWIN_WEIGHTS:
            delta_w[n], new_m[n], new_v[n] = _adamw(weights[n], grad_w[n], given["m_" + n], given["v_" + n])
    return (loss, grad_x, *[grad_w[n] for n in TWIN_WEIGHTS], *[delta_w[n] for n in TWIN_WEIGHTS],
            *[new_m[n] for n in TWIN_WEIGHTS], *[new_v[n] for n in TWIN_WEIGHTS])
```

```python
import functools

import jax
import jax.numpy as jnp
from jax import lax
from jax.experimental import pallas as pl
from jax.experimental.pallas import tpu as pltpu

F32 = jnp.float32
BF16 = jnp.bfloat16

EPS = 1e-5
HEAD_DIM = 64
STATE = 128
GROUPS = 2
SHORT_K = 3
SSD_K = 4
LANES = 128
PAIR = LANES // HEAD_DIM
SCAN_CHUNK = 256
HALO = 16
VMEM_LIMIT = 56 * 1024 * 1024

ADAM_LR = 0.001
ADAM_B1 = 0.9
ADAM_B2 = 0.999
ADAM_EPS = 1e-08
ADAM_WD = 0.01
ADAM_STEP = 10

MESH = pl.DeviceIdType.MESH


def _params(sem):
    return pltpu.CompilerParams(dimension_semantics=sem, vmem_limit_bytes=VMEM_LIMIT)


def _tile(n, cap, quantum):
    if n <= cap:
        return n
    best = None
    for t in range(quantum, cap + 1, quantum):
        if n % t == 0:
            best = t
    assert best is not None, (n, cap, quantum)
    return best


def _dot(a, b):
    return jnp.dot(a, b, preferred_element_type=F32)


def _dot_nt(a, b):
    return lax.dot_general(a, b, (((1,), (1,)), ((), ())), preferred_element_type=F32)


def _dot_tn(a, b):
    return lax.dot_general(a, b, (((0,), (0,)), ((), ())), preferred_element_type=F32)


def _dot_exact(a, b):
    return jnp.dot(a, b, precision=lax.Precision.HIGHEST, preferred_element_type=F32)


def _sigmoid(x):
    return 1.0 / (1.0 + jnp.exp(-x))


def _softplus(x):
    return jnp.maximum(x, 0.0) + jnp.log(1.0 + jnp.exp(-jnp.abs(x)))


def norm_matmul(x, nw, w, out_dtype, name):
    t, d = x.shape
    n = w.shape[1]
    tm = _tile(t, 1024, 8)
    tn = _tile(n, 1536, LANES)

    def body(x_ref, nw_ref, w_ref, o_ref, h_ref):
        @pl.when(pl.program_id(1) == 0)
        def _():
            xf = x_ref[...]
            r = lax.rsqrt(jnp.mean(xf * xf, axis=-1, keepdims=True) + EPS)
            h_ref[...] = (xf * r * nw_ref[...]).astype(BF16)

        o_ref[...] = _dot(h_ref[...], w_ref[...]).astype(out_dtype)

    return pl.pallas_call(
        body, name=name, grid=(t // tm, n // tn),
        in_specs=[pl.BlockSpec((tm, d), lambda i, j: (i, 0)),
                  pl.BlockSpec((1, d), lambda i, j: (0, 0)),
                  pl.BlockSpec((d, tn), lambda i, j: (0, j))],
        out_specs=[pl.BlockSpec((tm, tn), lambda i, j: (i, j)),
                   pl.BlockSpec((tm, d), lambda i, j: (i, 0))],
        out_shape=[jax.ShapeDtypeStruct((t, n), out_dtype), jax.ShapeDtypeStruct((t, d), BF16)],
        compiler_params=_params(("parallel", "arbitrary")),
    )(x, nw, w)


def matmul(lhs, ws, out_dtype, name, *, lhs_fn=None, residual=None, relu_gate=None):
    t = lhs[0].shape[0]
    n = ws[0].shape[1]
    nl = len(lhs)
    ks = [a.shape[1] for a in lhs]
    tm = _tile(t, 1024 if sum(ks) <= 2048 else 512, 8)
    tn = _tile(n, 1024 if sum(ks) <= 2048 else 512, LANES)
    staged = [a.dtype != BF16 or lhs_fn is not None for a in lhs]
    fn = lhs_fn if lhs_fn is not None else (lambda v: v)

    def body(*refs):
        lrefs = refs[:nl]
        wrefs = refs[nl:2 * nl]
        pos = 2 * nl
        extra = None
        if residual is not None or relu_gate is not None:
            extra = refs[pos]
            pos += 1
        o_ref = refs[pos]
        srefs = refs[pos + 1:]

        si = 0
        ops = []
        for a_ref, st in zip(lrefs, staged):
            if st:
                s_ref = srefs[si]
                si += 1

                @pl.when(pl.program_id(1) == 0)
                def _(a_ref=a_ref, s_ref=s_ref):
                    s_ref[...] = fn(a_ref[...].astype(F32)).astype(BF16)

                ops.append(s_ref)
            else:
                ops.append(a_ref)
        acc = _dot(ops[0][...], wrefs[0][...])
        for a_ref, w_ref in zip(ops[1:], wrefs[1:]):
            acc = acc + _dot(a_ref[...], w_ref[...])
        if residual is not None:
            acc = acc + extra[...]
        if relu_gate is not None:
            acc = acc * (2.0 * jnp.maximum(extra[...].astype(F32), 0.0))
        o_ref[...] = acc.astype(out_dtype)

    in_specs = [pl.BlockSpec((tm, k), lambda i, j: (i, 0)) for k in ks]
    in_specs += [pl.BlockSpec((k, tn), lambda i, j: (0, j)) for k in ks]
    args = list(lhs) + list(ws)
    if residual is not None or relu_gate is not None:
        in_specs.append(pl.BlockSpec((tm, tn), lambda i, j: (i, j)))
        args.append(residual if residual is not None else relu_gate)
    return pl.pallas_call(
        body, name=name, grid=(t // tm, n // tn), in_specs=in_specs,
        out_specs=pl.BlockSpec((tm, tn), lambda i, j: (i, j)),
        out_shape=jax.ShapeDtypeStruct((t, n), out_dtype),
        scratch_shapes=[pltpu.VMEM((tm, k), BF16) for k, st in zip(ks, staged) if st],
        compiler_params=_params(("parallel", "arbitrary")),
    )(*args)


def matmul_normbwd(lhs, ws, x, nw, dres, name):
    t, d = x.shape
    nl = len(lhs)
    ks = [a.shape[1] for a in lhs]
    tm = _tile(t, 256, 8)

    def body(*refs):
        lrefs = refs[:nl]
        wrefs = refs[nl:2 * nl]
        x_ref, nw_ref, dres_ref, dx_ref, dnw_ref = refs[2 * nl:]
        dh = _dot(lrefs[0][...].astype(BF16), wrefs[0][...])
        for a_ref, w_ref in zip(lrefs[1:], wrefs[1:]):
            dh = dh + _dot(a_ref[...].astype(BF16), w_ref[...])
        xf = x_ref[...]
        r = lax.rsqrt(jnp.mean(xf * xf, axis=-1, keepdims=True) + EPS)
        nx = xf * r
        dn = dh * nw_ref[...]
        dx = r * (dn - nx * jnp.mean(dn * nx, axis=-1, keepdims=True))
        dx_ref[...] = dres_ref[...] + dx

        @pl.when(pl.program_id(0) == 0)
        def _():
            dnw_ref[...] = jnp.zeros_like(dnw_ref)

        dnw_ref[...] += jnp.sum(dh * nx, axis=0, keepdims=True)

    in_specs = [pl.BlockSpec((tm, k), lambda i: (i, 0)) for k in ks]
    in_specs += [pl.BlockSpec((k, d), lambda i: (0, 0)) for k in ks]
    in_specs += [pl.BlockSpec((tm, d), lambda i: (i, 0)), pl.BlockSpec((1, d), lambda i: (0, 0)),
                 pl.BlockSpec((tm, d), lambda i: (i, 0))]
    return pl.pallas_call(
        body, name=name, grid=(t // tm,), in_specs=in_specs,
        out_specs=[pl.BlockSpec((tm, d), lambda i: (i, 0)), pl.BlockSpec((1, d), lambda i: (0, 0))],
        out_shape=[jax.ShapeDtypeStruct((t, d), F32), jax.ShapeDtypeStruct((1, d), F32)],
        compiler_params=_params(("arbitrary",)),
    )(*lhs, *ws, x, nw, dres)


def matmul_tn(a, b, name, *, a_fn=None):
    t, k = a.shape
    n = b.shape[1]
    tk = _tile(k, 1024, LANES)
    tn = _tile(n, 1024, LANES)
    tt = _tile(t, 1024, 8)
    nt = t // tt
    fn = a_fn if a_fn is not None else (lambda v: v)

    def body(a_ref, b_ref, o_ref, acc_ref):
        @pl.when(pl.program_id(2) == 0)
        def _():
            acc_ref[...] = jnp.zeros_like(acc_ref)

        av = a_ref[...]
        if a_fn is not None:
            av = fn(av.astype(F32))
        acc_ref[...] += _dot_tn(av.astype(BF16), b_ref[...].astype(BF16))

        @pl.when(pl.program_id(2) == nt - 1)
        def _():
            o_ref[...] = acc_ref[...]

    return pl.pallas_call(
        body, name=name, grid=(k // tk, n // tn, nt),
        in_specs=[pl.BlockSpec((tt, tk), lambda i, j, s: (s, i)),
                  pl.BlockSpec((tt, tn), lambda i, j, s: (s, j))],
        out_specs=pl.BlockSpec((tk, tn), lambda i, j, s: (i, j)),
        out_shape=jax.ShapeDtypeStruct((k, n), F32),
        scratch_shapes=[pltpu.VMEM((tk, tn), F32)],
        compiler_params=_params(("parallel", "parallel", "arbitrary")),
    )(a, b)


def conv_mixer_fwd(proj, kw, cw, name):
    t = proj.shape[0]
    tm = _tile(t, 512, HALO)
    tc = _tile(cw, 512, LANES)
    nj = cw // tc
    hb = tm // HALO

    def body(ub_ref, uc_ref, uh_ref, ucp_ref, uhp_ref, kw_ref, y_ref, ext):
        i = pl.program_id(0)
        v = uc_ref[...].astype(F32) * uh_ref[...].astype(F32)
        vp = ucp_ref[...].astype(F32) * uhp_ref[...].astype(F32)
        ext[0:HALO, :] = jnp.where(i > 0, vp, 0.0)
        ext[HALO:HALO + tm, :] = v
        cv = kw_ref[pl.ds(SHORT_K - 1, 1), :] * v
        for k in range(SHORT_K - 1):
            cv = cv + kw_ref[pl.ds(k, 1), :] * ext[pl.ds(HALO - (SHORT_K - 1) + k, tm), :]
        y_ref[...] = (ub_ref[...].astype(F32) * cv).astype(BF16)

    prev = lambda off: (lambda i, j: (jnp.maximum(i * hb - 1, 0), off + j))
    return pl.pallas_call(
        body, name=name, grid=(t // tm, nj),
        in_specs=[pl.BlockSpec((tm, tc), lambda i, j: (i, j)),
                  pl.BlockSpec((tm, tc), lambda i, j: (i, nj + j)),
                  pl.BlockSpec((tm, tc), lambda i, j: (i, 2 * nj + j)),
                  pl.BlockSpec((HALO, tc), prev(nj)),
                  pl.BlockSpec((HALO, tc), prev(2 * nj)),
                  pl.BlockSpec((SHORT_K, tc), lambda i, j: (0, j))],
        out_specs=pl.BlockSpec((tm, tc), lambda i, j: (i, j)),
        out_shape=jax.ShapeDtypeStruct((t, cw), BF16),
        scratch_shapes=[pltpu.VMEM((tm + HALO, tc), F32)],
        compiler_params=_params(("parallel", "parallel")),
    )(proj, proj, proj, proj, proj, kw)


def conv_mixer_bwd(proj, dy, kw, cw, name):
    t = proj.shape[0]
    tm = _tile(t, 512, HALO)
    tc = _tile(cw, 512, LANES)
    nj = cw // tc
    hb = tm // HALO
    ni = t // tm
    last_hb = t // HALO - 1

    def body(ub_ref, uc_ref, uh_ref, dy_ref, ucp_ref, uhp_ref, ubn_ref, dyn_ref, kw_ref,
             dub_ref, duc_ref, duh_ref, dkw_ref, ext, extd):
        i = pl.program_id(1)
        ub = ub_ref[...].astype(F32)
        uc = uc_ref[...].astype(F32)
        uh = uh_ref[...].astype(F32)
        dyv = dy_ref[...].astype(F32)
        v = uc * uh
        vp = ucp_ref[...].astype(F32) * uhp_ref[...].astype(F32)
        ext[0:HALO, :] = jnp.where(i > 0, vp, 0.0)
        ext[HALO:HALO + tm, :] = v
        cv = kw_ref[pl.ds(SHORT_K - 1, 1), :] * v
        for k in range(SHORT_K - 1):
            cv = cv + kw_ref[pl.ds(k, 1), :] * ext[pl.ds(HALO - (SHORT_K - 1) + k, tm), :]
        dcv = dyv * ub
        dcvn = dyn_ref[...].astype(F32) * ubn_ref[...].astype(F32)
        extd[0:tm, :] = dcv
        extd[tm:tm + HALO, :] = jnp.where(i < ni - 1, dcvn, 0.0)
        dv = kw_ref[pl.ds(SHORT_K - 1, 1), :] * dcv
        for k in range(SHORT_K - 1):
            dv = dv + kw_ref[pl.ds(k, 1), :] * extd[pl.ds(SHORT_K - 1 - k, tm), :]
        dub_ref[...] = (dyv * cv).astype(BF16)
        duc_ref[...] = (dv * uh).astype(BF16)
        duh_ref[...] = (dv * uc).astype(BF16)

        @pl.when(i == 0)
        def _():
            dkw_ref[...] = jnp.zeros_like(dkw_ref)

        for k in range(SHORT_K):
            sh = ext[pl.ds(HALO - (SHORT_K - 1) + k, tm), :]
            dkw_ref[pl.ds(k, 1), :] += jnp.sum(dcv * sh, axis=0, keepdims=True)

    prev = lambda off: (lambda j, i: (jnp.maximum(i * hb - 1, 0), off + j))
    nxt = lambda off: (lambda j, i: (jnp.minimum((i + 1) * hb, last_hb), off + j))
    cur = lambda off: (lambda j, i: (i, off + j))
    return pl.pallas_call(
        body, name=name, grid=(nj, ni),
        in_specs=[pl.BlockSpec((tm, tc), cur(0)), pl.BlockSpec((tm, tc), cur(nj)),
                  pl.BlockSpec((tm, tc), cur(2 * nj)), pl.BlockSpec((tm, tc), cur(0)),
                  pl.BlockSpec((HALO, tc), prev(nj)), pl.BlockSpec((HALO, tc), prev(2 * nj)),
                  pl.BlockSpec((HALO, tc), nxt(0)), pl.BlockSpec((HALO, tc), nxt(0)),
                  pl.BlockSpec((SHORT_K, tc), lambda j, i: (0, j))],
        out_specs=[pl.BlockSpec((tm, tc), cur(0)), pl.BlockSpec((tm, tc), cur(0)),
                   pl.BlockSpec((tm, tc), cur(0)), pl.BlockSpec((SHORT_K, tc), lambda j, i: (0, j))],
        out_shape=[jax.ShapeDtypeStruct((t, cw), BF16)] * 3 + [jax.ShapeDtypeStruct((SHORT_K, cw), F32)],
        scratch_shapes=[pltpu.VMEM((tm + HALO, tc), F32), pltpu.VMEM((tm + HALO, tc), F32)],
        compiler_params=_params(("parallel", "arbitrary")),
    )(proj, proj, proj, dy, proj, proj, proj, dy, kw)


def _head_column(mat, lane, h):
    return jnp.sum(jnp.where(lane == h, mat, 0.0), axis=-1, keepdims=True)


def _ssd_common(dt_raw_ref, dtb_ref, aneg_ref, cum_s, cumt_s, chunk):
    dt = _softplus(dt_raw_ref[...] + dtb_ref[...])
    al = dt * aneg_ref[...]
    ri = lax.broadcasted_iota(jnp.int32, (chunk, chunk), 0)
    ci = lax.broadcasted_iota(jnp.int32, (chunk, chunk), 1)
    cum = _dot_exact((ri >= ci).astype(F32), al)
    cum_s[...] = cum
    cumt_s[...] = cum.T
    return dt, cum, ri >= ci


def _causal_conv(ext, kw_ref, b_ref, chunk, kk):
    acc = b_ref[...] + kw_ref[pl.ds(kk - 1, 1), :] * ext[pl.ds(8, chunk), :]
    for k in range(kk - 1):
        acc = acc + kw_ref[pl.ds(k, 1), :] * ext[pl.ds(8 - (kk - 1) + k, chunk), :]
    return acc


def ssd_fwd(proj, dt_raw, kw_xs, kw_bc, b_xs, b_bc, dtb, aneg, dskip, normw, cw, si, name):
    t = proj.shape[0]
    ch = min(SCAN_CHUNK, t)
    nc = t // ch
    npair = si // LANES
    ppg = npair // GROUPS
    gn = GROUPS * STATE
    gw = si // GROUPS
    assert cw == si and (3 * cw + 2 * si) % (2 * gn) == 0
    zblk = 3 * cw // si
    xsblk = zblk + 1
    bcblk = (3 * cw + 2 * si) // (2 * gn)

    def body(z_ref, xs_ref, bc_ref, dtr_ref, kwx_ref, kwb_ref, bx_ref, bb_ref, dtb_ref, aneg_ref, dsk_ref, nw_ref,
             yb_ref, ys_ref, hs_ref, extx, extb, xs_s, bc_s, h_s, gated_s, s_s, cum_s, cumt_s):
        c = pl.program_id(0)

        @pl.when(c == 0)
        def _():
            h_s[...] = jnp.zeros_like(h_s)
            extx[0:8, :] = jnp.zeros((8, si), F32)
            extb[0:8, :] = jnp.zeros((8, 2 * gn), F32)

        @pl.when(c > 0)
        def _():
            extx[0:8, :] = extx[ch:ch + 8, :]
            extb[0:8, :] = extb[ch:ch + 8, :]

        extx[8:8 + ch, :] = xs_ref[...].astype(F32)
        extb[8:8 + ch, :] = bc_ref[...].astype(F32)
        xc = _causal_conv(extx, kwx_ref, bx_ref, ch, SSD_K)
        xs_s[...] = xc * _sigmoid(xc)
        bcc = _causal_conv(extb, kwb_ref, bb_ref, ch, SSD_K)
        bc_s[...] = (bcc * _sigmoid(bcc)).astype(BF16)

        dt, cum, tril = _ssd_common(dtr_ref, dtb_ref, aneg_ref, cum_s, cumt_s, ch)
        lane = lax.broadcasted_iota(jnp.int32, (ch, LANES), 1)
        lane1 = lax.broadcasted_iota(jnp.int32, (1, LANES), 1)
        low = lane < HEAD_DIM
        clast = cum_s[pl.ds(ch - 1, 1), :]

        for p in range(npair):
            g = p // ppg
            col = slice(p * LANES, (p + 1) * LANES)
            bg = bc_s[:, g * STATE:(g + 1) * STATE]
            cg = bc_s[:, gn + g * STATE:gn + (g + 1) * STATE]
            if p % ppg == 0:
                s_s[...] = _dot_nt(cg, bg)
            heads = (PAIR * p, PAIR * p + 1)
            ccol = [_head_column(cum, lane, h) for h in heads]
            dcol = [_head_column(dt, lane, h) for h in heads]
            cl = [jnp.sum(jnp.where(lane1 == h, clast, 0.0), axis=-1, keepdims=True) for h in heads]
            cum_px = jnp.where(low, ccol[0], ccol[1])
            dt_px = jnp.where(low, dcol[0], dcol[1])
            cl_px = jnp.where(lane1 < HEAD_DIM, cl[0], cl[1])
            xs_p = xs_s[:, col]
            xdt = xs_p * dt_px
            y = dsk_ref[:, col] * xs_p
            for hi, h in enumerate(heads):
                dec = jnp.exp(jnp.where(tril, ccol[hi] - cumt_s[pl.ds(h, 1), :], -jnp.inf))
                wm = (s_s[...] * dec).astype(BF16)
                xm = jnp.where(low if hi == 0 else jnp.logical_not(low), xdt, 0.0).astype(BF16)
                y = y + _dot(wm, xm)
            hp = h_s[p]
            hs_ref[0, p] = hp
            y = y + _dot(cg, hp.astype(BF16)) * jnp.exp(cum_px)
            st = _dot_tn(bg, (xdt * jnp.exp(cl_px - cum_px)).astype(BF16))
            h_s[p] = jnp.exp(cl_px) * hp + st
            ys_ref[:, col] = y.astype(BF16)
            zp = z_ref[:, col].astype(F32)
            gated_s[:, col] = y * zp * _sigmoid(zp)

        for g in range(GROUPS):
            col = slice(g * gw, (g + 1) * gw)
            gg = gated_s[:, col]
            r = lax.rsqrt(jnp.mean(gg * gg, axis=-1, keepdims=True) + EPS)
            yb_ref[:, col] = (gg * r * nw_ref[:, col]).astype(BF16)

    full = lambda shape: pl.BlockSpec(shape, lambda c: tuple(0 for _ in shape))
    return pl.pallas_call(
        body, name=name, grid=(nc,),
        in_specs=[pl.BlockSpec((ch, si), lambda c: (c, zblk)),
                  pl.BlockSpec((ch, si), lambda c: (c, xsblk)),
                  pl.BlockSpec((ch, 2 * gn), lambda c: (c, bcblk)),
                  pl.BlockSpec((ch, LANES), lambda c: (c, 0)),
                  full((SSD_K, si)), full((SSD_K, 2 * gn)), full((1, si)), full((1, 2 * gn)),
                  full((1, LANES)), full((1, LANES)), full((1, si)), full((1, si))],
        out_specs=[pl.BlockSpec((ch, si), lambda c: (c, 0)),
                   pl.BlockSpec((ch, si), lambda c: (c, 0)),
                   pl.BlockSpec((1, npair, STATE, LANES), lambda c: (c, 0, 0, 0))],
        out_shape=[jax.ShapeDtypeStruct((t, si), BF16), jax.ShapeDtypeStruct((t, si), BF16),
                   jax.ShapeDtypeStruct((nc, npair, STATE, LANES), F32)],
        scratch_shapes=[pltpu.VMEM((ch + 8, si), F32), pltpu.VMEM((ch + 8, 2 * gn), F32),
                        pltpu.VMEM((ch, si), F32), pltpu.VMEM((ch, 2 * gn), BF16),
                        pltpu.VMEM((npair, STATE, LANES), F32), pltpu.VMEM((ch, si), F32),
                        pltpu.VMEM((ch, ch), F32), pltpu.VMEM((ch, LANES), F32), pltpu.VMEM((LANES, ch), F32)],
        compiler_params=_params(("arbitrary",)),
    )(proj, proj, proj, dt_raw, kw_xs, kw_bc, b_xs, b_bc, dtb, aneg, dskip, normw)


def ssd_bwd(proj, dt_raw, ys, hsave, dy, kw_xs, kw_bc, b_xs, b_bc, dtb, aneg, dskip, normw, cw, si, name):
    t = proj.shape[0]
    ch = min(SCAN_CHUNK, t)
    nc = t // ch
    npair = si // LANES
    ppg = npair // GROUPS
    gn = GROUPS * STATE
    gw = si // GROUPS
    zblk = 3 * cw // si
    xsblk = zblk + 1
    bcblk = (3 * cw + 2 * si) // (2 * gn)
    hb = ch // HALO

    def body(z_ref, xs_ref, bc_ref, xsp_ref, bcp_ref, dtr_ref, ys_ref, hs_ref, dyb_ref,
             kwx_ref, kwb_ref, bx_ref, bb_ref, dtb_ref, aneg_ref, dsk_ref, nw_ref,
             dz_ref, dxs_ref, dbc_ref, ddt_ref, dkwx_ref, dkwb_ref, dbx_ref, dbb_ref, ddtb_ref, da_ref, ddsk_ref,
             dnw_ref,
             extx, extb, extdx, extdb, xs_s, bc_s, dsx_s, dsb_s, dy_s, dxs_s, dbc_s, dh_s, s_s, ds_s,
             cum_s, cumt_s, dccol_s, dcrow_s, ddtcol_s, dcl_s):
        i = pl.program_id(0)
        rc = nc - 1 - i

        @pl.when(i == 0)
        def _():
            dh_s[...] = jnp.zeros_like(dh_s)
            extdx[ch:ch + 8, :] = jnp.zeros((8, si), F32)
            extdb[ch:ch + 8, :] = jnp.zeros((8, 2 * gn), F32)
            for r in (dkwx_ref, dkwb_ref, dbx_ref, dbb_ref, ddtb_ref, da_ref, ddsk_ref, dnw_ref):
                r[...] = jnp.zeros_like(r)

        extx[0:8, :] = jnp.where(rc > 0, xsp_ref[pl.ds(HALO - 8, 8), :].astype(F32), 0.0)
        extb[0:8, :] = jnp.where(rc > 0, bcp_ref[pl.ds(HALO - 8, 8), :].astype(F32), 0.0)
        extx[8:8 + ch, :] = xs_ref[...].astype(F32)
        extb[8:8 + ch, :] = bc_ref[...].astype(F32)
        xc = _causal_conv(extx, kwx_ref, bx_ref, ch, SSD_K)
        sg = _sigmoid(xc)
        xs_s[...] = xc * sg
        dsx_s[...] = sg * (1.0 + xc * (1.0 - sg))
        bcc = _causal_conv(extb, kwb_ref, bb_ref, ch, SSD_K)
        sgb = _sigmoid(bcc)
        bc_s[...] = (bcc * sgb).astype(BF16)
        dsb_s[...] = sgb * (1.0 + bcc * (1.0 - sgb))

        dt, cum, tril = _ssd_common(dtr_ref, dtb_ref, aneg_ref, cum_s, cumt_s, ch)
        lane = lax.broadcasted_iota(jnp.int32, (ch, LANES), 1)
        lane1 = lax.broadcasted_iota(jnp.int32, (1, LANES), 1)
        low = lane < HEAD_DIM
        low1 = lane1 < HEAD_DIM
        clast = cum_s[pl.ds(ch - 1, 1), :]

        for g in range(GROUPS):
            col = slice(g * gw, (g + 1) * gw)
            ysf = ys_ref[:, col].astype(F32)
            zf = z_ref[:, col].astype(F32)
            sz = _sigmoid(zf)
            silz = zf * sz
            gg = ysf * silz
            r = lax.rsqrt(jnp.mean(gg * gg, axis=-1, keepdims=True) + EPS)
            nrm = gg * r
            dyb = dyb_ref[:, col].astype(F32)
            dnw_ref[:, col] += jnp.sum(dyb * nrm, axis=0, keepdims=True)
            dn = dyb * nw_ref[:, col]
            dgg = r * (dn - nrm * jnp.mean(dn * nrm, axis=-1, keepdims=True))
            dy_s[:, col] = dgg * silz
            dz_ref[:, col] = (dgg * ysf * (sz * (1.0 + zf * (1.0 - sz)))).astype(BF16)

        dccol_s[...] = jnp.zeros_like(dccol_s)
        dcrow_s[...] = jnp.zeros_like(dcrow_s)
        ddtcol_s[...] = jnp.zeros_like(ddtcol_s)
        dcl_s[...] = jnp.zeros_like(dcl_s)
        dbc_s[...] = jnp.zeros_like(dbc_s)

        for p in range(npair):
            g = p // ppg
            col = slice(p * LANES, (p + 1) * LANES)
            bcol = slice(g * STATE, (g + 1) * STATE)
            ccolg = slice(gn + g * STATE, gn + (g + 1) * STATE)
            bg = bc_s[:, bcol]
            cg = bc_s[:, ccolg]
            if p % ppg == 0:
                s_s[...] = _dot_nt(cg, bg)
                ds_s[...] = jnp.zeros_like(ds_s)
            heads = (PAIR * p, PAIR * p + 1)
            masks = (low, jnp.logical_not(low))
            ccol = [_head_column(cum, lane, h) for h in heads]
            dcol = [_head_column(dt, lane, h) for h in heads]
            cl = [jnp.sum(jnp.where(lane1 == h, clast, 0.0), axis=-1, keepdims=True) for h in heads]
            cum_px = jnp.where(low, ccol[0], ccol[1])
            dt_px = jnp.where(low, dcol[0], dcol[1])
            cl_px = jnp.where(low1, cl[0], cl[1])
            e_px = jnp.exp(cum_px)
            dec_end = jnp.exp(cl_px - cum_px)
            gdec = jnp.exp(cl_px)
            xs_p = xs_s[:, col]
            xdt = xs_p * dt_px
            dyp = dy_s[:, col]
            hc = hs_ref[0, p]
            hcb = hc.astype(BF16)
            dhn = dh_s[p]
            dhnb = dhn.astype(BF16)

            ddsk_ref[:, col] += jnp.sum(dyp * xs_p, axis=0, keepdims=True)
            dxs_acc = dsk_ref[:, col] * dyp
            dye = dyp * e_px
            dyeb = dye.astype(BF16)
            dbc_s[:, ccolg] += _dot_nt(dyeb, hcb)
            dcum_lane = dye * _dot(cg, hcb)
            dh_from_y = _dot_tn(cg, dyeb)
            xd = xdt * dec_end
            dxd = _dot(bg, dhnb)
            dbc_s[:, bcol] += _dot_nt(xd.astype(BF16), dhnb)
            dxdt = dxd * dec_end
            t1 = dxd * xd
            dcum_lane = dcum_lane - t1
            dcl_lane = jnp.sum(t1, axis=0, keepdims=True) + jnp.sum(dhn * hc, axis=0, keepdims=True) * gdec
            dh_s[p] = gdec * dhn + dh_from_y
            xdtb = xdt.astype(BF16)
            for hi, h in enumerate(heads):
                dym = jnp.where(masks[hi], dyp, 0.0).astype(BF16)
                dw = _dot_nt(dym, xdtb)
                dec = jnp.exp(jnp.where(tril, ccol[hi] - cumt_s[pl.ds(h, 1), :], -jnp.inf))
                wm = s_s[...] * dec
                dxdt = dxdt + _dot_tn(wm.astype(BF16), dym)
                ds_s[...] += dw * dec
                gm = dw * wm
                rowsum = jnp.sum(gm, axis=-1, keepdims=True)
                lanesum = jnp.sum(jnp.where(masks[hi], dcum_lane, 0.0), axis=-1, keepdims=True)
                dccol_s[...] += jnp.where(lane == h, rowsum + lanesum, 0.0)
                dcrow_s[pl.ds(h, 1), :] = jnp.sum(gm, axis=0, keepdims=True)
                dcl_h = jnp.sum(jnp.where(masks[hi][0:1, :], dcl_lane, 0.0), axis=-1, keepdims=True)
                dcl_s[...] += jnp.where(lane1 == h, dcl_h, 0.0)
            ddt_lane = dxdt * xs_p
            for hi, h in enumerate(heads):
                s = jnp.sum(jnp.where(masks[hi], ddt_lane, 0.0), axis=-1, keepdims=True)
                ddtcol_s[...] += jnp.where(lane == h, s, 0.0)
            dxs_s[:, col] = dxs_acc + dxdt * dt_px
            if p % ppg == ppg - 1:
                dsb = ds_s[...].astype(BF16)
                dbc_s[:, ccolg] += _dot(dsb, bg)
                dbc_s[:, bcol] += _dot_tn(dsb, cg)

        rowi = lax.broadcasted_iota(jnp.int32, (ch, LANES), 0)
        dcum = dccol_s[...] - dcrow_s[...].T + jnp.where(rowi == ch - 1, dcl_s[...], 0.0)
        ri = lax.broadcasted_iota(jnp.int32, (ch, ch), 0)
        ci = lax.broadcasted_iota(jnp.int32, (ch, ch), 1)
        dal = _dot_exact((ri <= ci).astype(F32), dcum)
        ddt = dal * aneg_ref[...] + ddtcol_s[...]
        da_ref[...] += jnp.sum(dal * dt, axis=0, keepdims=True)
        ddtr = ddt * _sigmoid(dtr_ref[...] + dtb_ref[...])
        ddt_ref[...] = ddtr
        ddtb_ref[...] += jnp.sum(ddtr, axis=0, keepdims=True)

        for (dpost, dsl, extd, ext, kw_ref, dkw_ref, db_ref, out_ref) in (
                (dxs_s, dsx_s, extdx, extx, kwx_ref, dkwx_ref, dbx_ref, dxs_ref),
                (dbc_s, dsb_s, extdb, extb, kwb_ref, dkwb_ref, dbb_ref, dbc_ref)):
            dxc = dpost[...] * dsl[...]
            extd[0:ch, :] = dxc
            draw = kw_ref[pl.ds(SSD_K - 1, 1), :] * dxc
            for k in range(SSD_K - 1):
                draw = draw + kw_ref[pl.ds(k, 1), :] * extd[pl.ds(SSD_K - 1 - k, ch), :]
            out_ref[...] = draw.astype(BF16)
            db_ref[...] += jnp.sum(dxc, axis=0, keepdims=True)
            for k in range(SSD_K):
                sh = ext[pl.ds(8 - (SSD_K - 1) + k, ch), :]
                dkw_ref[pl.ds(k, 1), :] += jnp.sum(dxc * sh, axis=0, keepdims=True)
            extd[ch:ch + 8, :] = extd[0:8, :]

    full = lambda shape: pl.BlockSpec(shape, lambda i: tuple(0 for _ in shape))
    rev = lambda blk: (lambda i: (nc - 1 - i, blk))
    prev = lambda blk: (lambda i: (jnp.maximum((nc - 1 - i) * hb - 1, 0), blk))
    small = [(SSD_K, si), (SSD_K, 2 * gn), (1, si), (1, 2 * gn), (1, LANES), (1, LANES), (1, si), (1, si)]
    return pl.pallas_call(
        body, name=name, grid=(nc,),
        in_specs=[pl.BlockSpec((ch, si), rev(zblk)), pl.BlockSpec((ch, si), rev(xsblk)),
                  pl.BlockSpec((ch, 2 * gn), rev(bcblk)),
                  pl.BlockSpec((HALO, si), prev(xsblk)), pl.BlockSpec((HALO, 2 * gn), prev(bcblk)),
                  pl.BlockSpec((ch, LANES), rev(0)), pl.BlockSpec((ch, si), rev(0)),
                  pl.BlockSpec((1, npair, STATE, LANES), lambda i: (nc - 1 - i, 0, 0, 0)),
                  pl.BlockSpec((ch, si), rev(cw // si))] + [full(s) for s in small],
        out_specs=[pl.BlockSpec((ch, si), rev(0)), pl.BlockSpec((ch, si), rev(0)),
                   pl.BlockSpec((ch, 2 * gn), rev(0)), pl.BlockSpec((ch, LANES), rev(0))]
                  + [full(s) for s in small],
        out_shape=[jax.ShapeDtypeStruct((t, si), BF16), jax.ShapeDtypeStruct((t, si), BF16),
                   jax.ShapeDtypeStruct((t, 2 * gn), BF16), jax.ShapeDtypeStruct((t, LANES), F32)]
                  + [jax.ShapeDtypeStruct(s, F32) for s in small],
        scratch_shapes=[pltpu.VMEM((ch + 8, si), F32), pltpu.VMEM((ch + 8, 2 * gn), F32),
                        pltpu.VMEM((ch + 8, si), F32), pltpu.VMEM((ch + 8, 2 * gn), F32),
                        pltpu.VMEM((ch, si), F32), pltpu.VMEM((ch, 2 * gn), BF16),
                        pltpu.VMEM((ch, si), F32), pltpu.VMEM((ch, 2 * gn), F32),
                        pltpu.VMEM((ch, si), F32), pltpu.VMEM((ch, si), F32), pltpu.VMEM((ch, 2 * gn), F32),
                        pltpu.VMEM((npair, STATE, LANES), F32),
                        pltpu.VMEM((ch, ch), F32), pltpu.VMEM((ch, ch), F32),
                        pltpu.VMEM((ch, LANES), F32), pltpu.VMEM((LANES, ch), F32),
                        pltpu.VMEM((ch, LANES), F32), pltpu.VMEM((LANES, ch), F32),
                        pltpu.VMEM((ch, LANES), F32), pltpu.VMEM((1, LANES), F32)],
        compiler_params=_params(("arbitrary",)),
    )(proj, proj, proj, proj, proj, dt_raw, ys, hsave, dy,
      kw_xs, kw_bc, b_xs, b_bc, dtb, aneg, dskip, normw)


def final_loss(x, nw, tgt, name):
    t, d = x.shape
    tm = _tile(t, 512, 8)

    def body(x_ref, nw_ref, t_ref, dx_ref, dnw_ref, ls_ref):
        xf = x_ref[...]
        r = lax.rsqrt(jnp.mean(xf * xf, axis=-1, keepdims=True) + EPS)
        nx = xf * r
        e = nx * nw_ref[...] - t_ref[...]
        dyv = e * (1.0 / d)
        dn = dyv * nw_ref[...]
        dx_ref[...] = r * (dn - nx * jnp.mean(dn * nx, axis=-1, keepdims=True))

        @pl.when(pl.program_id(0) == 0)
        def _():
            dnw_ref[...] = jnp.zeros_like(dnw_ref)
            ls_ref[...] = jnp.zeros_like(ls_ref)

        dnw_ref[...] += jnp.sum(dyv * nx, axis=0, keepdims=True)
        ls_ref[...] += jnp.sum(e * e, axis=0, keepdims=True) * (0.5 / d)

    return pl.pallas_call(
        body, name=name, grid=(t // tm,),
        in_specs=[pl.BlockSpec((tm, d), lambda i: (i, 0)), pl.BlockSpec((1, d), lambda i: (0, 0)),
                  pl.BlockSpec((tm, d), lambda i: (i, 0))],
        out_specs=[pl.BlockSpec((tm, d), lambda i: (i, 0)), pl.BlockSpec((1, d), lambda i: (0, 0)),
                   pl.BlockSpec((1, d), lambda i: (0, 0))],
        out_shape=[jax.ShapeDtypeStruct((t, d), F32), jax.ShapeDtypeStruct((1, d), F32),
                   jax.ShapeDtypeStruct((1, d), F32)],
        compiler_params=_params(("arbitrary",)),
    )(x, nw, tgt)


def _rows3(a):
    if a.ndim == 1:
        return a.reshape(1, 1, a.shape[0])
    if a.ndim == 2:
        return a.reshape(1, *a.shape)
    return a.reshape(-1, a.shape[-2], a.shape[-1])


def _elementwise(fn, ins, n_out, out_dtypes, name):
    shape = ins[0].shape
    v = [_rows3(a) for a in ins]
    b, r, c = v[0].shape
    tr = _tile(r, 256, 16) if r % 16 == 0 else r
    n_in = len(ins)

    def body(*refs):
        outs = fn(*[rf[...] for rf in refs[:n_in]])
        for o_ref, o in zip(refs[n_in:], outs):
            o_ref[...] = o.astype(o_ref.dtype)

    spec = pl.BlockSpec((1, tr, c), lambda i, j: (i, j, 0))
    outs = pl.pallas_call(
        body, name=name, grid=(b, r // tr), in_specs=[spec] * n_in, out_specs=[spec] * n_out,
        out_shape=[jax.ShapeDtypeStruct((b, r, c), dt) for dt in out_dtypes],
        compiler_params=_params(("parallel", "parallel")),
    )(*v)
    return [o.reshape(shape) for o in outs]


def adamw(w, g, m, v, name):
    def fn(w, g, m, v):
        m = ADAM_B1 * m + (1.0 - ADAM_B1) * g
        v = ADAM_B2 * v + (1.0 - ADAM_B2) * (g * g)
        m_hat = m / (1.0 - ADAM_B1 ** ADAM_STEP)
        v_hat = v / (1.0 - ADAM_B2 ** ADAM_STEP)
        delta = -ADAM_LR * (m_hat / (jnp.sqrt(v_hat) + ADAM_EPS) + ADAM_WD * w)
        return delta, m, v

    return _elementwise(fn, [w, g, m, v], 3, [F32, F32, F32], name)


def _coords():
    return lax.axis_index("x"), lax.axis_index("y"), lax.axis_index("c")


def _ici_peers(x, y):
    chips = [(1 - x, y), (x, 1 - y), (1 - x, 1 - y)]
    return chips, [2 * cx + cy for cx, cy in chips]


def _any_specs(n):
    return [pl.BlockSpec(memory_space=pl.ANY)] * n


def gather_weights(shards, name):
    na = len(shards)
    for s in shards:
        assert s.shape[0] % 2 == 0

    def body(*refs):
        ins = refs[:na]
        outs = refs[na:2 * na]
        send_sems, recv_sems, local_sems = refs[2 * na:]
        x, y, c = _coords()
        me = 2 * x + y
        chips, chip_ids = _ici_peers(x, y)
        sibling = (x, y, 1 - c)
        copies = []
        local = []
        for a in range(na):
            hl = ins[a].shape[0] // 2
            mine = pl.ds(c * hl, hl)
            theirs = pl.ds((1 - c) * hl, hl)
            lc = pltpu.make_async_copy(ins[a], outs[a].at[me], local_sems.at[a])
            lc.start()
            local.append(lc)
            first = []
            for k in range(3):
                cp = pltpu.make_async_remote_copy(
                    src_ref=ins[a].at[mine], dst_ref=outs[a].at[me, mine],
                    send_sem=send_sems.at[6 * a + k], recv_sem=recv_sems.at[6 * a + k],
                    device_id=(*chips[k], c), device_id_type=MESH)
                cp.start()
                first.append(cp)
            copies.append((first, mine, theirs))
        passed = []
        for a in range(na):
            first, mine, theirs = copies[a]
            for k in range(3):
                landed = outs[a].at[chip_ids[k], mine]
                pltpu.make_async_remote_copy(
                    src_ref=landed, dst_ref=landed, send_sem=send_sems.at[6 * a + k],
                    recv_sem=recv_sems.at[6 * a + k], device_id=(*chips[k], c), device_id_type=MESH).wait_recv()
                fw = pltpu.make_async_remote_copy(
                    src_ref=landed, dst_ref=landed, send_sem=send_sems.at[6 * a + 3 + k],
                    recv_sem=recv_sems.at[6 * a + 3 + k], device_id=sibling, device_id_type=MESH)
                fw.start()
                passed.append(fw)
        for a in range(na):
            first, mine, theirs = copies[a]
            for k in range(3):
                got = outs[a].at[chip_ids[k], theirs]
                pltpu.make_async_remote_copy(
                    src_ref=got, dst_ref=got, send_sem=send_sems.at[6 * a + 3 + k],
                    recv_sem=recv_sems.at[6 * a + 3 + k], device_id=sibling, device_id_type=MESH).wait_recv()
            for cp in first:
                cp.wait_send()
        for fw in passed:
            fw.wait_send()
        for lc in local:
            lc.wait()

    return pl.pallas_call(
        body, name=name, in_specs=_any_specs(na), out_specs=_any_specs(na),
        out_shape=[jax.ShapeDtypeStruct((4, *s.shape), s.dtype) for s in shards],
        scratch_shapes=[pltpu.SemaphoreType.DMA((6 * na,)), pltpu.SemaphoreType.DMA((6 * na,)),
                        pltpu.SemaphoreType.DMA((na,))],
        compiler_params=pltpu.CompilerParams(has_side_effects=True),
    )(*shards)


def send_halves_to_sibling(arrs, name):
    na = len(arrs)

    def body(*refs):
        ins = refs[:na]
        outs = refs[na:2 * na]
        send_sems, recv_sems = refs[2 * na:]
        x, y, c = _coords()
        cps = []
        for a in range(na):
            cp = pltpu.make_async_remote_copy(
                src_ref=ins[a].at[1 - c], dst_ref=outs[a], send_sem=send_sems.at[a], recv_sem=recv_sems.at[a],
                device_id=(x, y, 1 - c), device_id_type=MESH)
            cp.start()
            cps.append(cp)
        for cp in cps:
            cp.wait()

    return pl.pallas_call(
        body, name=name, in_specs=_any_specs(na), out_specs=_any_specs(na),
        out_shape=[jax.ShapeDtypeStruct(a.shape[1:], a.dtype) for a in arrs],
        scratch_shapes=[pltpu.SemaphoreType.DMA((na,)), pltpu.SemaphoreType.DMA((na,))],
        compiler_params=pltpu.CompilerParams(has_side_effects=True),
    )(*arrs)


def scatter_to_chips(arrs, name):
    na = len(arrs)

    def body(*refs):
        ins = refs[:na]
        outs = refs[na:2 * na]
        send_sems, recv_sems = refs[2 * na:]
        x, y, c = _coords()
        chips, chip_ids = _ici_peers(x, y)
        cps = []
        for a in range(na):
            for k in range(3):
                cp = pltpu.make_async_remote_copy(
                    src_ref=ins[a].at[chip_ids[k]], dst_ref=outs[a].at[k],
                    send_sem=send_sems.at[3 * a + k], recv_sem=recv_sems.at[3 * a + k],
                    device_id=(*chips[k], c), device_id_type=MESH)
                cp.start()
                cps.append(cp)
        for cp in cps:
            cp.wait()

    return pl.pallas_call(
        body, name=name, in_specs=_any_specs(na), out_specs=_any_specs(na),
        out_shape=[jax.ShapeDtypeStruct((3, *a.shape[1:]), a.dtype) for a in arrs],
        scratch_shapes=[pltpu.SemaphoreType.DMA((3 * na,)), pltpu.SemaphoreType.DMA((3 * na,))],
        compiler_params=pltpu.CompilerParams(has_side_effects=True),
    )(*arrs)


def exchange_halves(arrs, name):
    na = len(arrs)

    def body(*refs):
        ins = refs[:na]
        outs = refs[na:2 * na]
        send_sems, recv_sems, local_sems = refs[2 * na:]
        x, y, c = _coords()
        cps = []
        for a in range(na):
            lc = pltpu.make_async_copy(ins[a], outs[a].at[c], local_sems.at[a])
            lc.start()
            cp = pltpu.make_async_remote_copy(
                src_ref=ins[a], dst_ref=outs[a].at[c], send_sem=send_sems.at[a], recv_sem=recv_sems.at[a],
                device_id=(x, y, 1 - c), device_id_type=MESH)
            cp.start()
            cps.append((lc, cp))
        for lc, cp in cps:
            cp.wait()
            lc.wait()

    return pl.pallas_call(
        body, name=name, in_specs=_any_specs(na), out_specs=_any_specs(na),
        out_shape=[jax.ShapeDtypeStruct((2, *a.shape), a.dtype) for a in arrs],
        scratch_shapes=[pltpu.SemaphoreType.DMA((na,)), pltpu.SemaphoreType.DMA((na,)),
                        pltpu.SemaphoreType.DMA((na,))],
        compiler_params=pltpu.CompilerParams(has_side_effects=True),
    )(*arrs)


def allreduce_small(buf, name):
    r, cdim = buf.shape

    def body(x_ref, o_ref, gath, send_sems, recv_sems):
        x, y, c = _coords()
        me, sibling = (x, y, c), (x, y, 1 - c)
        chips, _ = _ici_peers(x, y)

        def slot(px, py, pc):
            return gath.at[4 * px + 2 * py + pc]

        def copy(k, block, to, src=None):
            return pltpu.make_async_remote_copy(
                src_ref=slot(*block) if src is None else src, dst_ref=slot(*block),
                send_sem=send_sems.at[k], recv_sem=recv_sems.at[k], device_id=to, device_id_type=MESH)

        gath[4 * x + 2 * y + c] = x_ref[...]
        first = [copy(0, me, sibling, src=x_ref)]
        first += [copy(1 + j, me, (*chip, c), src=x_ref) for j, chip in enumerate(chips)]
        for cp in first:
            cp.start()
        passed = [copy(4 + j, (*chip, c), sibling) for j, chip in enumerate(chips)]
        for j, chip in enumerate(chips):
            copy(1 + j, (*chip, c), me).wait_recv()
            passed[j].start()
        copy(0, sibling, me).wait_recv()
        for j, chip in enumerate(chips):
            copy(4 + j, (*chip, 1 - c), me).wait_recv()
        for cp in first + passed:
            cp.wait_send()
        acc = gath[0]
        for d in range(1, 8):
            acc = acc + gath[d]
        o_ref[...] = acc

    return pl.pallas_call(
        body, name=name,
        in_specs=[pl.BlockSpec(memory_space=pltpu.VMEM)], out_specs=pl.BlockSpec(memory_space=pltpu.VMEM),
        out_shape=jax.ShapeDtypeStruct((r, cdim), F32),
        scratch_shapes=[pltpu.VMEM((8, r, cdim), F32), pltpu.SemaphoreType.DMA((7,)), pltpu.SemaphoreType.DMA((7,))],
        compiler_params=pltpu.CompilerParams(has_side_effects=True),
    )(buf)


def _expand_heads(v):
    return jnp.repeat(v.astype(F32), HEAD_DIM).reshape(1, -1)


def _pad_lanes(v):
    return jnp.pad(v.astype(F32), (0, LANES - v.shape[0])).reshape(1, LANES)


def local_step(x, tgt, p):
    nl = p["w_up"].shape[0]
    d = x.shape[1]
    cw = p["short_conv_w"].shape[2]
    si = p["ssd_norm_w"].shape[1]
    nh = si // HEAD_DIM
    gn = GROUPS * STATE
    cols = p["w_in"].shape[2]
    npad = -(-cols // LANES) * LANES
    dt_off = 3 * cw + si + si + 2 * gn
    assert cols == dt_off + nh and nh <= LANES

    w_in = jnp.pad(p["w_in"], ((0, 0), (0, 0), (0, npad - cols)))
    w_in_t = jnp.swapaxes(w_in, 1, 2)
    w_out_t = jnp.swapaxes(p["w_out"], 1, 2)
    w_up_t = jnp.swapaxes(p["w_up"], 1, 2)
    w_down_t = jnp.swapaxes(p["w_down"], 1, 2)
    pieces = [(0, cw), (cw, cw), (2 * cw, cw), (3 * cw, si), (3 * cw + si, si), (3 * cw + 2 * si, 2 * gn),
              (dt_off, LANES)]

    saved = []
    for l in range(nl):
        nw1 = p["norm_mix_w"][l].reshape(1, d)
        nw2 = p["norm_mlp_w"][l].reshape(1, d)
        kw3 = p["short_conv_w"][l]
        kwx, kwb = p["ssd_conv_w"][l][:, :si], p["ssd_conv_w"][l][:, si:]
        bx, bb = p["ssd_conv_b"][l][:si].reshape(1, si), p["ssd_conv_b"][l][si:].reshape(1, 2 * gn)
        dtb = _pad_lanes(p["dt_bias"][l])
        aneg = _pad_lanes(-jnp.exp(p["a_log"][l]))
        dsk = _expand_heads(p["d_skip"][l])
        snw = p["ssd_norm_w"][l].reshape(1, si)
        ssd_args = (kwx, kwb, bx, bb, dtb, aneg, dsk, snw)

        proj, h = norm_matmul(x, nw1, w_in[l], BF16, "in_proj")
        dt_raw, _ = norm_matmul(x, nw1, w_in[l][:, dt_off:dt_off + LANES], F32, "dt_proj")
        y_a = conv_mixer_fwd(proj, kw3, cw, "conv_mixer_fwd")
        y_b, ys, hsave = ssd_fwd(proj, dt_raw, *ssd_args, cw, si, "ssd_fwd")
        x2 = matmul([y_a, y_b], [p["w_out"][l][:cw], p["w_out"][l][cw:]], F32, "out_proj", residual=x)
        up, h2 = norm_matmul(x2, nw2, p["w_up"][l], BF16, "up_proj")
        x3 = matmul([up], [p["w_down"][l]], F32, "down_proj",
                    lhs_fn=lambda v: jnp.square(jnp.maximum(v, 0.0)), residual=x2)
        saved.append((x, h, proj, dt_raw, y_a, y_b, ys, hsave, x2, h2, up, nw1, nw2, kw3, ssd_args))
        x = x3

    dx, dwf, lvec = final_loss(x, p["final_norm_w"].reshape(1, d), tgt, "final_loss")
    loss = jnp.sum(lvec)

    grads = {k: [None] * nl for k in ("norm_mix_w", "w_in", "short_conv_w", "ssd_conv_w", "ssd_conv_b", "dt_bias",
                                      "a_log", "d_skip", "ssd_norm_w", "w_out", "norm_mlp_w", "w_up", "w_down")}
    relu2 = lambda v: jnp.square(jnp.maximum(v, 0.0))
    for l in reversed(range(nl)):
        x0, h, proj, dt_raw, y_a, y_b, ys, hsave, x2, h2, up, nw1, nw2, kw3, ssd_args = saved[l]
        dup = matmul([dx], [w_down_t[l]], BF16, "down_bwd", relu_gate=up)
        grads["w_down"][l] = matmul_tn(up, dx, "down_wgrad", a_fn=relu2)
        dx2, dnw2 = matmul_normbwd([dup], [w_up_t[l]], x2, nw2, dx, "up_bwd")
        grads["w_up"][l] = matmul_tn(h2, dup, "up_wgrad")
        grads["norm_mlp_w"][l] = dnw2.reshape(d)
        dy = matmul([dx2], [w_out_t[l]], BF16, "out_bwd")
        grads["w_out"][l] = jnp.concatenate(
            [matmul_tn(y_a, dx2, "out_wgrad_a"), matmul_tn(y_b, dx2, "out_wgrad_b")], axis=0)
        dub, duc, duh, dkw3 = conv_mixer_bwd(proj, dy, kw3, cw, "conv_mixer_bwd")
        (dz, dxs, dbc, ddt, dkwx, dkwb, dbx, dbb, ddtb, da, ddsk, dsnw) = ssd_bwd(
            proj, dt_raw, ys, hsave, dy, *ssd_args, cw, si, "ssd_bwd")
        dpieces = [dub, duc, duh, dz, dxs, dbc, ddt]
        wts = [w_in_t[l][o:o + n] for o, n in pieces]
        dxl, dnw1 = matmul_normbwd(dpieces, wts, x0, nw1, dx2, "in_bwd")
        dwin = jnp.concatenate([matmul_tn(h, dp, "in_wgrad_%d" % i) for i, dp in enumerate(dpieces)], axis=1)
        grads["w_in"][l] = dwin[:, :cols]
        grads["norm_mix_w"][l] = dnw1.reshape(d)
        grads["short_conv_w"][l] = dkw3
        grads["ssd_conv_w"][l] = jnp.concatenate([dkwx, dkwb], axis=1)
        grads["ssd_conv_b"][l] = jnp.concatenate([dbx, dbb], axis=1).reshape(-1)
        grads["dt_bias"][l] = ddtb[0, :nh]
        grads["a_log"][l] = da[0, :nh] * ssd_args[5][0, :nh]
        grads["d_skip"][l] = jnp.sum(ddsk.reshape(nh, HEAD_DIM), axis=1)
        grads["ssd_norm_w"][l] = dsnw.reshape(si)
        dx = dxl

    grads = {k: jnp.stack(v) for k, v in grads.items()}
    grads["final_norm_w"] = dwf.reshape(d)
    return loss, dx, grads


BIG = ("w_in", "w_out", "w_up", "w_down")
SHARD_AXIS = {"w_in": 2, "w_out": 1, "w_up": 2, "w_down": 1, "short_conv_w": 2, "ssd_conv_w": 2}
SMALL_SHARDED = ("short_conv_w", "ssd_conv_w")
SMALL_REPL = ("norm_mix_w", "ssd_conv_b", "dt_bias", "a_log", "d_skip", "ssd_norm_w", "norm_mlp_w", "final_norm_w")
WEIGHTS = ("norm_mix_w", "w_in", "short_conv_w", "ssd_conv_w", "ssd_conv_b", "dt_bias", "a_log", "d_skip",
           "ssd_norm_w", "w_out", "norm_mlp_w", "w_up", "w_down", "final_norm_w")
SMALL_COLS = 1024


def _unshard(g4, axis):
    return jnp.concatenate([g4[j] for j in range(4)], axis=axis)


def _to_chips(full, axis):
    parts = jnp.stack(jnp.split(full, 4, axis=axis))
    nl = full.shape[0]
    parts = parts.reshape(4, 2, nl // 2, *parts.shape[2:])
    return jnp.swapaxes(parts, 0, 1)


def _pack_small(named):
    flat = jnp.concatenate([v.reshape(-1).astype(F32) for v in named])
    n = flat.shape[0]
    rows = -(-n // SMALL_COLS)
    rows = -(-rows // 8) * 8
    return jnp.pad(flat, (0, rows * SMALL_COLS - n)).reshape(rows, SMALL_COLS)


def _unpack_small(buf, like):
    flat = buf.reshape(-1)
    out, off = [], 0
    for v in like:
        out.append(flat[off:off + v.size].reshape(v.shape))
        off += v.size
    return out


def kernel(x, norm_mix_w, w_in, short_conv_w, ssd_conv_w, ssd_conv_b, dt_bias, a_log, d_skip, ssd_norm_w, w_out, norm_mlp_w, w_up, w_down, final_norm_w, loss_target, m_norm_mix_w, m_w_in, m_short_conv_w, m_ssd_conv_w, m_ssd_conv_b, m_dt_bias, m_a_log, m_d_skip, m_ssd_norm_w, m_w_out, m_norm_mlp_w, m_w_up, m_w_down, m_final_norm_w, v_norm_mix_w, v_w_in, v_short_conv_w, v_ssd_conv_w, v_ssd_conv_b, v_dt_bias, v_a_log, v_d_skip, v_ssd_norm_w, v_w_out, v_norm_mlp_w, v_w_up, v_w_down, v_final_norm_w):
    w = dict(norm_mix_w=norm_mix_w, w_in=w_in, short_conv_w=short_conv_w, ssd_conv_w=ssd_conv_w,
             ssd_conv_b=ssd_conv_b, dt_bias=dt_bias, a_log=a_log, d_skip=d_skip, ssd_norm_w=ssd_norm_w, w_out=w_out,
             norm_mlp_w=norm_mlp_w, w_up=w_up, w_down=w_down, final_norm_w=final_norm_w)
    m = dict(norm_mix_w=m_norm_mix_w, w_in=m_w_in, short_conv_w=m_short_conv_w, ssd_conv_w=m_ssd_conv_w,
             ssd_conv_b=m_ssd_conv_b, dt_bias=m_dt_bias, a_log=m_a_log, d_skip=m_d_skip, ssd_norm_w=m_ssd_norm_w,
             w_out=m_w_out, norm_mlp_w=m_norm_mlp_w, w_up=m_w_up, w_down=m_w_down, final_norm_w=m_final_norm_w)
    v = dict(norm_mix_w=v_norm_mix_w, w_in=v_w_in, short_conv_w=v_short_conv_w, ssd_conv_w=v_ssd_conv_w,
             ssd_conv_b=v_ssd_conv_b, dt_bias=v_dt_bias, a_log=v_a_log, d_skip=v_d_skip, ssd_norm_w=v_ssd_norm_w,
             w_out=v_w_out, norm_mlp_w=v_norm_mlp_w, w_up=v_w_up, w_down=v_w_down, final_norm_w=v_final_norm_w)
    xi, yi, ci = lax.axis_index("x"), lax.axis_index("y"), lax.axis_index("c")
    chip = 2 * xi + yi

    shards = [w[k].astype(BF16) for k in BIG] + [w[k] for k in SMALL_SHARDED]
    gathered = gather_weights(shards, "gather_weights")
    full = dict(w)
    for k, g4 in zip(BIG + SMALL_SHARDED, gathered):
        full[k] = _unshard(g4, SHARD_AXIS[k])

    loss, grad_x, grads = local_step(x[0], loss_target[0], full)
    loss = lax.psum(loss, ("x", "y", "c"))

    by_chip = [_to_chips(grads[k], SHARD_AXIS[k]) for k in BIG]
    from_sibling = send_halves_to_sibling(by_chip, "grads_to_sibling")
    mine = [lax.dynamic_index_in_dim(a, ci, 0, keepdims=False) for a in by_chip]
    chip_sums = [_elementwise(lambda a, b: (a + b, a + b), [a, b], 2, [F32, BF16], "chip_sum_%s" % k)
                 for k, a, b in zip(BIG, mine, from_sibling)]
    from_chips = scatter_to_chips([s[1] for s in chip_sums], "grads_to_chips")
    reduced = []
    for k, s, r3 in zip(BIG, chip_sums, from_chips):
        own = lax.dynamic_index_in_dim(s[0], chip, 0, keepdims=False)
        red, = _elementwise(lambda a, b, c_, d_: (a + b.astype(F32) + c_.astype(F32) + d_.astype(F32),),
                            [own, r3[0], r3[1], r3[2]], 1, [F32], "grad_sum_%s" % k)
        reduced.append(red)
    both = exchange_halves(reduced, "grads_exchange")
    g_shard = {k: b.reshape(w[k].shape) for k, b in zip(BIG, both)}

    small_names = SMALL_REPL + SMALL_SHARDED
    small_sum = allreduce_small(_pack_small([grads[k] for k in small_names]), "allreduce_small")
    for k, g in zip(small_names, _unpack_small(small_sum, [grads[k] for k in small_names])):
        if k in SMALL_SHARDED:
            width = w[k].shape[2]
            g = lax.dynamic_slice_in_dim(g, chip * width, width, axis=2)
        g_shard[k] = g

    delta, new_m, new_v = {}, {}, {}
    for k in BIG:
        delta[k], new_m[k], new_v[k] = adamw(w[k], g_shard[k], m[k], v[k], "adamw_%s" % k)
    names = SMALL_REPL + SMALL_SHARDED
    packed = [_pack_small([d_[k] for k in names]) for d_ in (w, g_shard, m, v)]
    outs = adamw(*packed, "adamw_small")
    for d_, buf in zip((delta, new_m, new_v), outs):
        for k, val in zip(names, _unpack_small(buf, [w[k] for k in names])):
            d_[k] = val

    return (loss, grad_x[None], *[g_shard[k] for k in WEIGHTS], *[delta[k] for k in WEIGHTS],
            *[new_m[k] for k in WEIGHTS], *[new_v[k] for k in WEIGHTS])
```

```python
import functools

import jax
import jax.numpy as jnp
from jax import lax
from jax.experimental import pallas as pl
from jax.experimental.pallas import tpu as pltpu

F32 = jnp.float32
BF16 = jnp.bfloat16

EPS = 1e-5
HEAD_DIM = 64
STATE = 128
GROUPS = 2
SHORT_K = 3
SSD_K = 4
LANES = 128
PAIR = LANES // HEAD_DIM
SCAN_CHUNK = 256
HALO = 16
VMEM_LIMIT = 56 * 1024 * 1024

ADAM_LR = 0.001
ADAM_B1 = 0.9
ADAM_B2 = 0.999
ADAM_EPS = 1e-08
ADAM_WD = 0.01
ADAM_STEP = 10

MESH = pl.DeviceIdType.MESH


def _params(sem):
    return pltpu.CompilerParams(dimension_semantics=sem, vmem_limit_bytes=VMEM_LIMIT)


def _tile(n, cap, quantum):
    if n <= cap:
        return n
    best = None
    for t in range(quantum, cap + 1, quantum):
        if n % t == 0:
            best = t
    assert best is not None, (n, cap, quantum)
    return best


def _dot(a, b):
    return jnp.dot(a, b, preferred_element_type=F32)


def _dot_nt(a, b):
    return lax.dot_general(a, b, (((1,), (1,)), ((), ())), preferred_element_type=F32)


def _dot_tn(a, b):
    return lax.dot_general(a, b, (((0,), (0,)), ((), ())), preferred_element_type=F32)


def _dot_exact(a, b):
    return jnp.dot(a, b, precision=lax.Precision.HIGHEST, preferred_element_type=F32)


def _sigmoid(x):
    return 1.0 / (1.0 + jnp.exp(-x))


def _softplus(x):
    return jnp.maximum(x, 0.0) + jnp.log(1.0 + jnp.exp(-jnp.abs(x)))


def norm_matmul(x, nw, w, out_dtype, name, emit_h=True):
    t, d = x.shape
    n = w.shape[1]
    tm = _tile(t, 1024, 8)
    tn = _tile(n, 1536, LANES)

    def body(x_ref, nw_ref, w_ref, o_ref, h_ref):
        @pl.when(pl.program_id(1) == 0)
        def _():
            xf = x_ref[...]
            r = lax.rsqrt(jnp.mean(xf * xf, axis=-1, keepdims=True) + EPS)
            h_ref[...] = (xf * r * nw_ref[...]).astype(BF16)

        o_ref[...] = _dot(h_ref[...], w_ref[...]).astype(out_dtype)

    out_specs = [pl.BlockSpec((tm, tn), lambda i, j: (i, j))]
    out_shape = [jax.ShapeDtypeStruct((t, n), out_dtype)]
    if emit_h:
        out_specs.append(pl.BlockSpec((tm, d), lambda i, j: (i, 0)))
        out_shape.append(jax.ShapeDtypeStruct((t, d), BF16))
    outs = pl.pallas_call(
        body, name=name, grid=(t // tm, n // tn),
        in_specs=[pl.BlockSpec((tm, d), lambda i, j: (i, 0)),
                  pl.BlockSpec((1, d), lambda i, j: (0, 0)),
                  pl.BlockSpec((d, tn), lambda i, j: (0, j))],
        out_specs=out_specs, out_shape=out_shape,
        scratch_shapes=[] if emit_h else [pltpu.VMEM((tm, d), BF16)],
        compiler_params=_params(("parallel", "arbitrary")),
    )(x, nw, w)
    return outs if emit_h else outs[0]


def matmul(lhs, ws, out_dtype, name, *, lhs_fn=None, residual=None, relu_gate=None):
    t = lhs[0].shape[0]
    n = ws[0].shape[1]
    nl = len(lhs)
    ks = [a.shape[1] for a in lhs]
    tm = _tile(t, 1024 if sum(ks) <= 2048 else 512, 8)
    tn = _tile(n, 1024 if sum(ks) <= 2048 else 512, LANES)
    staged = [a.dtype != BF16 or lhs_fn is not None for a in lhs]
    fn = lhs_fn if lhs_fn is not None else (lambda v: v)

    def body(*refs):
        lrefs = refs[:nl]
        wrefs = refs[nl:2 * nl]
        pos = 2 * nl
        extra = None
        if residual is not None or relu_gate is not None:
            extra = refs[pos]
            pos += 1
        o_ref = refs[pos]
        srefs = refs[pos + 1:]

        si = 0
        ops = []
        for a_ref, st in zip(lrefs, staged):
            if st:
                s_ref = srefs[si]
                si += 1

                @pl.when(pl.program_id(1) == 0)
                def _(a_ref=a_ref, s_ref=s_ref):
                    s_ref[...] = fn(a_ref[...].astype(F32)).astype(BF16)

                ops.append(s_ref)
            else:
                ops.append(a_ref)
        acc = _dot(ops[0][...], wrefs[0][...])
        for a_ref, w_ref in zip(ops[1:], wrefs[1:]):
            acc = acc + _dot(a_ref[...], w_ref[...])
        if residual is not None:
            acc = acc + extra[...]
        if relu_gate is not None:
            acc = acc * (2.0 * jnp.maximum(extra[...].astype(F32), 0.0))
        o_ref[...] = acc.astype(out_dtype)

    in_specs = [pl.BlockSpec((tm, k), lambda i, j: (i, 0)) for k in ks]
    in_specs += [pl.BlockSpec((k, tn), lambda i, j: (0, j)) for k in ks]
    args = list(lhs) + list(ws)
    if residual is not None or relu_gate is not None:
        in_specs.append(pl.BlockSpec((tm, tn), lambda i, j: (i, j)))
        args.append(residual if residual is not None else relu_gate)
    return pl.pallas_call(
        body, name=name, grid=(t // tm, n // tn), in_specs=in_specs,
        out_specs=pl.BlockSpec((tm, tn), lambda i, j: (i, j)),
        out_shape=jax.ShapeDtypeStruct((t, n), out_dtype),
        scratch_shapes=[pltpu.VMEM((tm, k), BF16) for k, st in zip(ks, staged) if st],
        compiler_params=_params(("parallel", "arbitrary")),
    )(*args)


def matmul_normbwd(lhs, ws, x, nw, dres, name):
    t, d = x.shape
    nl = len(lhs)
    ks = [a.shape[1] for a in lhs]
    tm = _tile(t, 256, 8)

    def body(*refs):
        lrefs = refs[:nl]
        wrefs = refs[nl:2 * nl]
        x_ref, nw_ref, dres_ref, dx_ref, dnw_ref = refs[2 * nl:]
        dh = _dot(lrefs[0][...].astype(BF16), wrefs[0][...])
        for a_ref, w_ref in zip(lrefs[1:], wrefs[1:]):
            dh = dh + _dot(a_ref[...].astype(BF16), w_ref[...])
        xf = x_ref[...]
        r = lax.rsqrt(jnp.mean(xf * xf, axis=-1, keepdims=True) + EPS)
        nx = xf * r
        dn = dh * nw_ref[...]
        dx = r * (dn - nx * jnp.mean(dn * nx, axis=-1, keepdims=True))
        dx_ref[...] = dres_ref[...] + dx

        @pl.when(pl.program_id(0) == 0)
        def _():
            dnw_ref[...] = jnp.zeros_like(dnw_ref)

        dnw_ref[...] += jnp.sum(dh * nx, axis=0, keepdims=True)

    in_specs = [pl.BlockSpec((tm, k), lambda i: (i, 0)) for k in ks]
    in_specs += [pl.BlockSpec((k, d), lambda i: (0, 0)) for k in ks]
    in_specs += [pl.BlockSpec((tm, d), lambda i: (i, 0)), pl.BlockSpec((1, d), lambda i: (0, 0)),
                 pl.BlockSpec((tm, d), lambda i: (i, 0))]
    return pl.pallas_call(
        body, name=name, grid=(t // tm,), in_specs=in_specs,
        out_specs=[pl.BlockSpec((tm, d), lambda i: (i, 0)), pl.BlockSpec((1, d), lambda i: (0, 0))],
        out_shape=[jax.ShapeDtypeStruct((t, d), F32), jax.ShapeDtypeStruct((1, d), F32)],
        compiler_params=_params(("arbitrary",)),
    )(*lhs, *ws, x, nw, dres)


def matmul_tn(a, b, name, *, a_fn=None):
    t, k = a.shape
    n = b.shape[1]
    tk = _tile(k, 1024, LANES)
    tn = _tile(n, 1024, LANES)
    tt = _tile(t, 1024, 8)
    nt = t // tt
    fn = a_fn if a_fn is not None else (lambda v: v)

    def body(a_ref, b_ref, o_ref, acc_ref):
        @pl.when(pl.program_id(2) == 0)
        def _():
            acc_ref[...] = jnp.zeros_like(acc_ref)

        av = a_ref[...]
        if a_fn is not None:
            av = fn(av.astype(F32))
        acc_ref[...] += _dot_tn(av.astype(BF16), b_ref[...].astype(BF16))

        @pl.when(pl.program_id(2) == nt - 1)
        def _():
            o_ref[...] = acc_ref[...]

    return pl.pallas_call(
        body, name=name, grid=(k // tk, n // tn, nt),
        in_specs=[pl.BlockSpec((tt, tk), lambda i, j, s: (s, i)),
                  pl.BlockSpec((tt, tn), lambda i, j, s: (s, j))],
        out_specs=pl.BlockSpec((tk, tn), lambda i, j, s: (i, j)),
        out_shape=jax.ShapeDtypeStruct((k, n), F32),
        scratch_shapes=[pltpu.VMEM((tk, tn), F32)],
        compiler_params=_params(("parallel", "parallel", "arbitrary")),
    )(a, b)


def conv_mixer_fwd(proj, kw, cw, name):
    t = proj.shape[0]
    tm = _tile(t, 512, HALO)
    tc = _tile(cw, 512, LANES)
    nj = cw // tc
    hb = tm // HALO

    def body(ub_ref, uc_ref, uh_ref, ucp_ref, uhp_ref, kw_ref, y_ref, ext):
        i = pl.program_id(0)
        v = uc_ref[...].astype(F32) * uh_ref[...].astype(F32)
        vp = ucp_ref[...].astype(F32) * uhp_ref[...].astype(F32)
        ext[0:HALO, :] = jnp.where(i > 0, vp, 0.0)
        ext[HALO:HALO + tm, :] = v
        cv = kw_ref[pl.ds(SHORT_K - 1, 1), :] * v
        for k in range(SHORT_K - 1):
            cv = cv + kw_ref[pl.ds(k, 1), :] * ext[pl.ds(HALO - (SHORT_K - 1) + k, tm), :]
        y_ref[...] = (ub_ref[...].astype(F32) * cv).astype(BF16)

    prev = lambda off: (lambda i, j: (jnp.maximum(i * hb - 1, 0), off + j))
    return pl.pallas_call(
        body, name=name, grid=(t // tm, nj),
        in_specs=[pl.BlockSpec((tm, tc), lambda i, j: (i, j)),
                  pl.BlockSpec((tm, tc), lambda i, j: (i, nj + j)),
                  pl.BlockSpec((tm, tc), lambda i, j: (i, 2 * nj + j)),
                  pl.BlockSpec((HALO, tc), prev(nj)),
                  pl.BlockSpec((HALO, tc), prev(2 * nj)),
                  pl.BlockSpec((SHORT_K, tc), lambda i, j: (0, j))],
        out_specs=pl.BlockSpec((tm, tc), lambda i, j: (i, j)),
        out_shape=jax.ShapeDtypeStruct((t, cw), BF16),
        scratch_shapes=[pltpu.VMEM((tm + HALO, tc), F32)],
        compiler_params=_params(("parallel", "parallel")),
    )(proj, proj, proj, proj, proj, kw)


def conv_mixer_bwd(proj, dy, kw, cw, name):
    t = proj.shape[0]
    tm = _tile(t, 512, HALO)
    tc = _tile(cw, 512, LANES)
    nj = cw // tc
    hb = tm // HALO
    ni = t // tm
    last_hb = t // HALO - 1

    def body(ub_ref, uc_ref, uh_ref, dy_ref, ucp_ref, uhp_ref, ubn_ref, dyn_ref, kw_ref,
             dub_ref, duc_ref, duh_ref, dkw_ref, ext, extd):
        i = pl.program_id(1)
        ub = ub_ref[...].astype(F32)
        uc = uc_ref[...].astype(F32)
        uh = uh_ref[...].astype(F32)
        dyv = dy_ref[...].astype(F32)
        v = uc * uh
        vp = ucp_ref[...].astype(F32) * uhp_ref[...].astype(F32)
        ext[0:HALO, :] = jnp.where(i > 0, vp, 0.0)
        ext[HALO:HALO + tm, :] = v
        cv = kw_ref[pl.ds(SHORT_K - 1, 1), :] * v
        for k in range(SHORT_K - 1):
            cv = cv + kw_ref[pl.ds(k, 1), :] * ext[pl.ds(HALO - (SHORT_K - 1) + k, tm), :]
        dcv = dyv * ub
        dcvn = dyn_ref[...].astype(F32) * ubn_ref[...].astype(F32)
        extd[0:tm, :] = dcv
        extd[tm:tm + HALO, :] = jnp.where(i < ni - 1, dcvn, 0.0)
        dv = kw_ref[pl.ds(SHORT_K - 1, 1), :] * dcv
        for k in range(SHORT_K - 1):
            dv = dv + kw_ref[pl.ds(k, 1), :] * extd[pl.ds(SHORT_K - 1 - k, tm), :]
        dub_ref[...] = (dyv * cv).astype(BF16)
        duc_ref[...] = (dv * uh).astype(BF16)
        duh_ref[...] = (dv * uc).astype(BF16)

        @pl.when(i == 0)
        def _():
            dkw_ref[...] = jnp.zeros_like(dkw_ref)

        for k in range(SHORT_K):
            sh = ext[pl.ds(HALO - (SHORT_K - 1) + k, tm), :]
            dkw_ref[pl.ds(k, 1), :] += jnp.sum(dcv * sh, axis=0, keepdims=True)

    prev = lambda off: (lambda j, i: (jnp.maximum(i * hb - 1, 0), off + j))
    nxt = lambda off: (lambda j, i: (jnp.minimum((i + 1) * hb, last_hb), off + j))
    cur = lambda off: (lambda j, i: (i, off + j))
    return pl.pallas_call(
        body, name=name, grid=(nj, ni),
        in_specs=[pl.BlockSpec((tm, tc), cur(0)), pl.BlockSpec((tm, tc), cur(nj)),
                  pl.BlockSpec((tm, tc), cur(2 * nj)), pl.BlockSpec((tm, tc), cur(0)),
                  pl.BlockSpec((HALO, tc), prev(nj)), pl.BlockSpec((HALO, tc), prev(2 * nj)),
                  pl.BlockSpec((HALO, tc), nxt(0)), pl.BlockSpec((HALO, tc), nxt(0)),
                  pl.BlockSpec((SHORT_K, tc), lambda j, i: (0, j))],
        out_specs=[pl.BlockSpec((tm, tc), cur(0)), pl.BlockSpec((tm, tc), cur(0)),
                   pl.BlockSpec((tm, tc), cur(0)), pl.BlockSpec((SHORT_K, tc), lambda j, i: (0, j))],
        out_shape=[jax.ShapeDtypeStruct((t, cw), BF16)] * 3 + [jax.ShapeDtypeStruct((SHORT_K, cw), F32)],
        scratch_shapes=[pltpu.VMEM((tm + HALO, tc), F32), pltpu.VMEM((tm + HALO, tc), F32)],
        compiler_params=_params(("parallel", "arbitrary")),
    )(proj, proj, proj, dy, proj, proj, proj, dy, kw)


def _head_column(mat, lane, h):
    return jnp.sum(jnp.where(lane == h, mat, 0.0), axis=-1, keepdims=True)


def _ssd_common(dt_raw_ref, dtb_ref, aneg_ref, cum_s, cumt_s, chunk):
    dt = _softplus(dt_raw_ref[...] + dtb_ref[...])
    al = dt * aneg_ref[...]
    ri = lax.broadcasted_iota(jnp.int32, (chunk, chunk), 0)
    ci = lax.broadcasted_iota(jnp.int32, (chunk, chunk), 1)
    cum = _dot_exact((ri >= ci).astype(F32), al)
    cum_s[...] = cum
    cumt_s[...] = cum.T
    return dt, cum, ri >= ci


def _causal_conv(ext, kw_ref, b_ref, chunk, kk):
    acc = b_ref[...] + kw_ref[pl.ds(kk - 1, 1), :] * ext[pl.ds(8, chunk), :]
    for k in range(kk - 1):
        acc = acc + kw_ref[pl.ds(k, 1), :] * ext[pl.ds(8 - (kk - 1) + k, chunk), :]
    return acc


def ssd_fwd(proj, dt_raw, kw_xs, kw_bc, b_xs, b_bc, dtb, aneg, dskip, normw, cw, si, name):
    t = proj.shape[0]
    ch = min(SCAN_CHUNK, t)
    nc = t // ch
    npair = si // LANES
    ppg = npair // GROUPS
    gn = GROUPS * STATE
    gw = si // GROUPS
    assert cw == si and (3 * cw + 2 * si) % (2 * gn) == 0
    zblk = 3 * cw // si
    xsblk = zblk + 1
    bcblk = (3 * cw + 2 * si) // (2 * gn)

    def body(z_ref, xs_ref, bc_ref, dtr_ref, kwx_ref, kwb_ref, bx_ref, bb_ref, dtb_ref, aneg_ref, dsk_ref, nw_ref,
             yb_ref, ys_ref, hs_ref, extx, extb, xs_s, bc_s, h_s, gated_s, s_s, cum_s, cumt_s):
        c = pl.program_id(0)

        @pl.when(c == 0)
        def _():
            h_s[...] = jnp.zeros_like(h_s)
            extx[0:8, :] = jnp.zeros((8, si), F32)
            extb[0:8, :] = jnp.zeros((8, 2 * gn), F32)

        @pl.when(c > 0)
        def _():
            extx[0:8, :] = extx[ch:ch + 8, :]
            extb[0:8, :] = extb[ch:ch + 8, :]

        extx[8:8 + ch, :] = xs_ref[...].astype(F32)
        extb[8:8 + ch, :] = bc_ref[...].astype(F32)
        xc = _causal_conv(extx, kwx_ref, bx_ref, ch, SSD_K)
        xs_s[...] = xc * _sigmoid(xc)
        bcc = _causal_conv(extb, kwb_ref, bb_ref, ch, SSD_K)
        bc_s[...] = (bcc * _sigmoid(bcc)).astype(BF16)

        dt, cum, tril = _ssd_common(dtr_ref, dtb_ref, aneg_ref, cum_s, cumt_s, ch)
        lane = lax.broadcasted_iota(jnp.int32, (ch, LANES), 1)
        lane1 = lax.broadcasted_iota(jnp.int32, (1, LANES), 1)
        low = lane < HEAD_DIM
        clast = cum_s[pl.ds(ch - 1, 1), :]

        for p in range(npair):
            g = p // ppg
            col = slice(p * LANES, (p + 1) * LANES)
            bg = bc_s[:, g * STATE:(g + 1) * STATE]
            cg = bc_s[:, gn + g * STATE:gn + (g + 1) * STATE]
            if p % ppg == 0:
                s_s[...] = _dot_nt(cg, bg)
            heads = (PAIR * p, PAIR * p + 1)
            ccol = [_head_column(cum, lane, h) for h in heads]
            dcol = [_head_column(dt, lane, h) for h in heads]
            cl = [jnp.sum(jnp.where(lane1 == h, clast, 0.0), axis=-1, keepdims=True) for h in heads]
            cum_px = jnp.where(low, ccol[0], ccol[1])
            dt_px = jnp.where(low, dcol[0], dcol[1])
            cl_px = jnp.where(lane1 < HEAD_DIM, cl[0], cl[1])
            xs_p = xs_s[:, col]
            xdt = xs_p * dt_px
            y = dsk_ref[:, col] * xs_p
            for hi, h in enumerate(heads):
                dec = jnp.exp(jnp.where(tril, ccol[hi] - cumt_s[pl.ds(h, 1), :], -jnp.inf))
                wm = (s_s[...] * dec).astype(BF16)
                xm = jnp.where(low if hi == 0 else jnp.logical_not(low), xdt, 0.0).astype(BF16)
                y = y + _dot(wm, xm)
            hp = h_s[p]
            hs_ref[0, p] = hp
            y = y + _dot(cg, hp.astype(BF16)) * jnp.exp(cum_px)
            st = _dot_tn(bg, (xdt * jnp.exp(cl_px - cum_px)).astype(BF16))
            h_s[p] = jnp.exp(cl_px) * hp + st
            ys_ref[:, col] = y.astype(BF16)
            zp = z_ref[:, col].astype(F32)
            gated_s[:, col] = y * zp * _sigmoid(zp)

        for g in range(GROUPS):
            col = slice(g * gw, (g + 1) * gw)
            gg = gated_s[:, col]
            r = lax.rsqrt(jnp.mean(gg * gg, axis=-1, keepdims=True) + EPS)
            yb_ref[:, col] = (gg * r * nw_ref[:, col]).astype(BF16)

    full = lambda shape: pl.BlockSpec(shape, lambda c: tuple(0 for _ in shape))
    return pl.pallas_call(
        body, name=name, grid=(nc,),
        in_specs=[pl.BlockSpec((ch, si), lambda c: (c, zblk)),
                  pl.BlockSpec((ch, si), lambda c: (c, xsblk)),
                  pl.BlockSpec((ch, 2 * gn), lambda c: (c, bcblk)),
                  pl.BlockSpec((ch, LANES), lambda c: (c, 0)),
                  full((SSD_K, si)), full((SSD_K, 2 * gn)), full((1, si)), full((1, 2 * gn)),
                  full((1, LANES)), full((1, LANES)), full((1, si)), full((1, si))],
        out_specs=[pl.BlockSpec((ch, si), lambda c: (c, 0)),
                   pl.BlockSpec((ch, si), lambda c: (c, 0)),
                   pl.BlockSpec((1, npair, STATE, LANES), lambda c: (c, 0, 0, 0))],
        out_shape=[jax.ShapeDtypeStruct((t, si), BF16), jax.ShapeDtypeStruct((t, si), BF16),
                   jax.ShapeDtypeStruct((nc, npair, STATE, LANES), F32)],
        scratch_shapes=[pltpu.VMEM((ch + 8, si), F32), pltpu.VMEM((ch + 8, 2 * gn), F32),
                        pltpu.VMEM((ch, si), F32), pltpu.VMEM((ch, 2 * gn), BF16),
                        pltpu.VMEM((npair, STATE, LANES), F32), pltpu.VMEM((ch, si), F32),
                        pltpu.VMEM((ch, ch), F32), pltpu.VMEM((ch, LANES), F32), pltpu.VMEM((LANES, ch), F32)],
        compiler_params=_params(("arbitrary",)),
    )(proj, proj, proj, dt_raw, kw_xs, kw_bc, b_xs, b_bc, dtb, aneg, dskip, normw)


def ssd_bwd(proj, dt_raw, ys, hsave, dy, kw_xs, kw_bc, b_xs, b_bc, dtb, aneg, dskip, normw, cw, si, name):
    t = proj.shape[0]
    ch = min(SCAN_CHUNK, t)
    nc = t // ch
    npair = si // LANES
    ppg = npair // GROUPS
    gn = GROUPS * STATE
    gw = si // GROUPS
    zblk = 3 * cw // si
    xsblk = zblk + 1
    bcblk = (3 * cw + 2 * si) // (2 * gn)
    hb = ch // HALO

    def body(z_ref, xs_ref, bc_ref, xsp_ref, bcp_ref, dtr_ref, ys_ref, hs_ref, dyb_ref,
             kwx_ref, kwb_ref, bx_ref, bb_ref, dtb_ref, aneg_ref, dsk_ref, nw_ref,
             dz_ref, dxs_ref, dbc_ref, ddt_ref, dkwx_ref, dkwb_ref, dbx_ref, dbb_ref, ddtb_ref, da_ref, ddsk_ref,
             dnw_ref,
             extx, extb, extdx, extdb, xs_s, bc_s, dsx_s, dsb_s, dy_s, dxs_s, dbc_s, dh_s, s_s, ds_s,
             cum_s, cumt_s, dccol_s, dcrow_s, ddtcol_s, dcl_s):
        i = pl.program_id(0)
        rc = nc - 1 - i

        @pl.when(i == 0)
        def _():
            dh_s[...] = jnp.zeros_like(dh_s)
            extdx[ch:ch + 8, :] = jnp.zeros((8, si), F32)
            extdb[ch:ch + 8, :] = jnp.zeros((8, 2 * gn), F32)
            for r in (dkwx_ref, dkwb_ref, dbx_ref, dbb_ref, ddtb_ref, da_ref, ddsk_ref, dnw_ref):
                r[...] = jnp.zeros_like(r)

        extx[0:8, :] = jnp.where(rc > 0, xsp_ref[pl.ds(HALO - 8, 8), :].astype(F32), 0.0)
        extb[0:8, :] = jnp.where(rc > 0, bcp_ref[pl.ds(HALO - 8, 8), :].astype(F32), 0.0)
        extx[8:8 + ch, :] = xs_ref[...].astype(F32)
        extb[8:8 + ch, :] = bc_ref[...].astype(F32)
        xc = _causal_conv(extx, kwx_ref, bx_ref, ch, SSD_K)
        sg = _sigmoid(xc)
        xs_s[...] = xc * sg
        dsx_s[...] = sg * (1.0 + xc * (1.0 - sg))
        bcc = _causal_conv(extb, kwb_ref, bb_ref, ch, SSD_K)
        sgb = _sigmoid(bcc)
        bc_s[...] = (bcc * sgb).astype(BF16)
        dsb_s[...] = sgb * (1.0 + bcc * (1.0 - sgb))

        dt, cum, tril = _ssd_common(dtr_ref, dtb_ref, aneg_ref, cum_s, cumt_s, ch)
        lane = lax.broadcasted_iota(jnp.int32, (ch, LANES), 1)
        lane1 = lax.broadcasted_iota(jnp.int32, (1, LANES), 1)
        low = lane < HEAD_DIM
        low1 = lane1 < HEAD_DIM
        clast = cum_s[pl.ds(ch - 1, 1), :]

        for g in range(GROUPS):
            col = slice(g * gw, (g + 1) * gw)
            ysf = ys_ref[:, col].astype(F32)
            zf = z_ref[:, col].astype(F32)
            sz = _sigmoid(zf)
            silz = zf * sz
            gg = ysf * silz
            r = lax.rsqrt(jnp.mean(gg * gg, axis=-1, keepdims=True) + EPS)
            nrm = gg * r
            dyb = dyb_ref[:, col].astype(F32)
            dnw_ref[:, col] += jnp.sum(dyb * nrm, axis=0, keepdims=True)
            dn = dyb * nw_ref[:, col]
            dgg = r * (dn - nrm * jnp.mean(dn * nrm, axis=-1, keepdims=True))
            dy_s[:, col] = dgg * silz
            dz_ref[:, col] = (dgg * ysf * (sz * (1.0 + zf * (1.0 - sz)))).astype(BF16)

        dccol_s[...] = jnp.zeros_like(dccol_s)
        dcrow_s[...] = jnp.zeros_like(dcrow_s)
        ddtcol_s[...] = jnp.zeros_like(ddtcol_s)
        dcl_s[...] = jnp.zeros_like(dcl_s)
        dbc_s[...] = jnp.zeros_like(dbc_s)

        for p in range(npair):
            g = p // ppg
            col = slice(p * LANES, (p + 1) * LANES)
            bcol = slice(g * STATE, (g + 1) * STATE)
            ccolg = slice(gn + g * STATE, gn + (g + 1) * STATE)
            bg = bc_s[:, bcol]
            cg = bc_s[:, ccolg]
            if p % ppg == 0:
                s_s[...] = _dot_nt(cg, bg)
                ds_s[...] = jnp.zeros_like(ds_s)
            heads = (PAIR * p, PAIR * p + 1)
            masks = (low, jnp.logical_not(low))
            ccol = [_head_column(cum, lane, h) for h in heads]
            dcol = [_head_column(dt, lane, h) for h in heads]
            cl = [jnp.sum(jnp.where(lane1 == h, clast, 0.0), axis=-1, keepdims=True) for h in heads]
            cum_px = jnp.where(low, ccol[0], ccol[1])
            dt_px = jnp.where(low, dcol[0], dcol[1])
            cl_px = jnp.where(low1, cl[0], cl[1])
            e_px = jnp.exp(cum_px)
            dec_end = jnp.exp(cl_px - cum_px)
            gdec = jnp.exp(cl_px)
            xs_p = xs_s[:, col]
            xdt = xs_p * dt_px
            dyp = dy_s[:, col]
            hc = hs_ref[0, p]
            hcb = hc.astype(BF16)
            dhn = dh_s[p]
            dhnb = dhn.astype(BF16)

            ddsk_ref[:, col] += jnp.sum(dyp * xs_p, axis=0, keepdims=True)
            dxs_acc = dsk_ref[:, col] * dyp
            dye = dyp * e_px
            dyeb = dye.astype(BF16)
            dbc_s[:, ccolg] += _dot_nt(dyeb, hcb)
            dcum_lane = dye * _dot(cg, hcb)
            dh_from_y = _dot_tn(cg, dyeb)
            xd = xdt * dec_end
            dxd = _dot(bg, dhnb)
            dbc_s[:, bcol] += _dot_nt(xd.astype(BF16), dhnb)
            dxdt = dxd * dec_end
            t1 = dxd * xd
            dcum_lane = dcum_lane - t1
            dcl_lane = jnp.sum(t1, axis=0, keepdims=True) + jnp.sum(dhn * hc, axis=0, keepdims=True) * gdec
            dh_s[p] = gdec * dhn + dh_from_y
            xdtb = xdt.astype(BF16)
            for hi, h in enumerate(heads):
                dym = jnp.where(masks[hi], dyp, 0.0).astype(BF16)
                dw = _dot_nt(dym, xdtb)
                dec = jnp.exp(jnp.where(tril, ccol[hi] - cumt_s[pl.ds(h, 1), :], -jnp.inf))
                wm = s_s[...] * dec
                dxdt = dxdt + _dot_tn(wm.astype(BF16), dym)
                ds_s[...] += dw * dec
                gm = dw * wm
                rowsum = jnp.sum(gm, axis=-1, keepdims=True)
                lanesum = jnp.sum(jnp.where(masks[hi], dcum_lane, 0.0), axis=-1, keepdims=True)
                dccol_s[...] += jnp.where(lane == h, rowsum + lanesum, 0.0)
                dcrow_s[pl.ds(h, 1), :] = jnp.sum(gm, axis=0, keepdims=True)
                dcl_h = jnp.sum(jnp.where(masks[hi][0:1, :], dcl_lane, 0.0), axis=-1, keepdims=True)
                dcl_s[...] += jnp.where(lane1 == h, dcl_h, 0.0)
            ddt_lane = dxdt * xs_p
            for hi, h in enumerate(heads):
                s = jnp.sum(jnp.where(masks[hi], ddt_lane, 0.0), axis=-1, keepdims=True)
                ddtcol_s[...] += jnp.where(lane == h, s, 0.0)
            dxs_s[:, col] = dxs_acc + dxdt * dt_px
            if p % ppg == ppg - 1:
                dsb = ds_s[...].astype(BF16)
                dbc_s[:, ccolg] += _dot(dsb, bg)
                dbc_s[:, bcol] += _dot_tn(dsb, cg)

        rowi = lax.broadcasted_iota(jnp.int32, (ch, LANES), 0)
        dcum = dccol_s[...] - dcrow_s[...].T + jnp.where(rowi == ch - 1, dcl_s[...], 0.0)
        ri = lax.broadcasted_iota(jnp.int32, (ch, ch), 0)
        ci = lax.broadcasted_iota(jnp.int32, (ch, ch), 1)
        dal = _dot_exact((ri <= ci).astype(F32), dcum)
        ddt = dal * aneg_ref[...] + ddtcol_s[...]
        da_ref[...] += jnp.sum(dal * dt, axis=0, keepdims=True)
        ddtr = ddt * _sigmoid(dtr_ref[...] + dtb_ref[...])
        ddt_ref[...] = ddtr
        ddtb_ref[...] += jnp.sum(ddtr, axis=0, keepdims=True)

        for (dpost, dsl, extd, ext, kw_ref, dkw_ref, db_ref, out_ref) in (
                (dxs_s, dsx_s, extdx, extx, kwx_ref, dkwx_ref, dbx_ref, dxs_ref),
                (dbc_s, dsb_s, extdb, extb, kwb_ref, dkwb_ref, dbb_ref, dbc_ref)):
            dxc = dpost[...] * dsl[...]
            extd[0:ch, :] = dxc
            draw = kw_ref[pl.ds(SSD_K - 1, 1), :] * dxc
            for k in range(SSD_K - 1):
                draw = draw + kw_ref[pl.ds(k, 1), :] * extd[pl.ds(SSD_K - 1 - k, ch), :]
            out_ref[...] = draw.astype(BF16)
            db_ref[...] += jnp.sum(dxc, axis=0, keepdims=True)
            for k in range(SSD_K):
                sh = ext[pl.ds(8 - (SSD_K - 1) + k, ch), :]
                dkw_ref[pl.ds(k, 1), :] += jnp.sum(dxc * sh, axis=0, keepdims=True)
            extd[ch:ch + 8, :] = extd[0:8, :]

    full = lambda shape: pl.BlockSpec(shape, lambda i: tuple(0 for _ in shape))
    rev = lambda blk: (lambda i: (nc - 1 - i, blk))
    prev = lambda blk: (lambda i: (jnp.maximum((nc - 1 - i) * hb - 1, 0), blk))
    small = [(SSD_K, si), (SSD_K, 2 * gn), (1, si), (1, 2 * gn), (1, LANES), (1, LANES), (1, si), (1, si)]
    return pl.pallas_call(
        body, name=name, grid=(nc,),
        in_specs=[pl.BlockSpec((ch, si), rev(zblk)), pl.BlockSpec((ch, si), rev(xsblk)),
                  pl.BlockSpec((ch, 2 * gn), rev(bcblk)),
                  pl.BlockSpec((HALO, si), prev(xsblk)), pl.BlockSpec((HALO, 2 * gn), prev(bcblk)),
                  pl.BlockSpec((ch, LANES), rev(0)), pl.BlockSpec((ch, si), rev(0)),
                  pl.BlockSpec((1, npair, STATE, LANES), lambda i: (nc - 1 - i, 0, 0, 0)),
                  pl.BlockSpec((ch, si), rev(cw // si))] + [full(s) for s in small],
        out_specs=[pl.BlockSpec((ch, si), rev(0)), pl.BlockSpec((ch, si), rev(0)),
                   pl.BlockSpec((ch, 2 * gn), rev(0)), pl.BlockSpec((ch, LANES), rev(0))]
                  + [full(s) for s in small],
        out_shape=[jax.ShapeDtypeStruct((t, si), BF16), jax.ShapeDtypeStruct((t, si), BF16),
                   jax.ShapeDtypeStruct((t, 2 * gn), BF16), jax.ShapeDtypeStruct((t, LANES), F32)]
                  + [jax.ShapeDtypeStruct(s, F32) for s in small],
        scratch_shapes=[pltpu.VMEM((ch + 8, si), F32), pltpu.VMEM((ch + 8, 2 * gn), F32),
                        pltpu.VMEM((ch + 8, si), F32), pltpu.VMEM((ch + 8, 2 * gn), F32),
                        pltpu.VMEM((ch, si), F32), pltpu.VMEM((ch, 2 * gn), BF16),
                        pltpu.VMEM((ch, si), F32), pltpu.VMEM((ch, 2 * gn), F32),
                        pltpu.VMEM((ch, si), F32), pltpu.VMEM((ch, si), F32), pltpu.VMEM((ch, 2 * gn), F32),
                        pltpu.VMEM((npair, STATE, LANES), F32),
                        pltpu.VMEM((ch, ch), F32), pltpu.VMEM((ch, ch), F32),
                        pltpu.VMEM((ch, LANES), F32), pltpu.VMEM((LANES, ch), F32),
                        pltpu.VMEM((ch, LANES), F32), pltpu.VMEM((LANES, ch), F32),
                        pltpu.VMEM((ch, LANES), F32), pltpu.VMEM((1, LANES), F32)],
        compiler_params=_params(("arbitrary",)),
    )(proj, proj, proj, proj, proj, dt_raw, ys, hsave, dy,
      kw_xs, kw_bc, b_xs, b_bc, dtb, aneg, dskip, normw)


def final_loss(x, nw, tgt, name):
    t, d = x.shape
    tm = _tile(t, 512, 8)

    def body(x_ref, nw_ref, t_ref, dx_ref, dnw_ref, ls_ref):
        xf = x_ref[...]
        r = lax.rsqrt(jnp.mean(xf * xf, axis=-1, keepdims=True) + EPS)
        nx = xf * r
        e = nx * nw_ref[...] - t_ref[...]
        dyv = e * (1.0 / d)
        dn = dyv * nw_ref[...]
        dx_ref[...] = r * (dn - nx * jnp.mean(dn * nx, axis=-1, keepdims=True))

        @pl.when(pl.program_id(0) == 0)
        def _():
            dnw_ref[...] = jnp.zeros_like(dnw_ref)
            ls_ref[...] = jnp.zeros_like(ls_ref)

        dnw_ref[...] += jnp.sum(dyv * nx, axis=0, keepdims=True)
        ls_ref[...] += jnp.sum(e * e, axis=0, keepdims=True) * (0.5 / d)

    return pl.pallas_call(
        body, name=name, grid=(t // tm,),
        in_specs=[pl.BlockSpec((tm, d), lambda i: (i, 0)), pl.BlockSpec((1, d), lambda i: (0, 0)),
                  pl.BlockSpec((tm, d), lambda i: (i, 0))],
        out_specs=[pl.BlockSpec((tm, d), lambda i: (i, 0)), pl.BlockSpec((1, d), lambda i: (0, 0)),
                   pl.BlockSpec((1, d), lambda i: (0, 0))],
        out_shape=[jax.ShapeDtypeStruct((t, d), F32), jax.ShapeDtypeStruct((1, d), F32),
                   jax.ShapeDtypeStruct((1, d), F32)],
        compiler_params=_params(("arbitrary",)),
    )(x, nw, tgt)


def _rows3(a):
    if a.ndim == 1:
        return a.reshape(1, 1, a.shape[0])
    if a.ndim == 2:
        return a.reshape(1, *a.shape)
    return a.reshape(-1, a.shape[-2], a.shape[-1])


def _elementwise(fn, ins, n_out, out_dtypes, name):
    shape = ins[0].shape
    v = [_rows3(a) for a in ins]
    b, r, c = v[0].shape
    tr = _tile(r, 256, 16) if r % 16 == 0 else r
    n_in = len(ins)

    def body(*refs):
        outs = fn(*[rf[...] for rf in refs[:n_in]])
        for o_ref, o in zip(refs[n_in:], outs):
            o_ref[...] = o.astype(o_ref.dtype)

    spec = pl.BlockSpec((1, tr, c), lambda i, j: (i, j, 0))
    outs = pl.pallas_call(
        body, name=name, grid=(b, r // tr), in_specs=[spec] * n_in, out_specs=[spec] * n_out,
        out_shape=[jax.ShapeDtypeStruct((b, r, c), dt) for dt in out_dtypes],
        compiler_params=_params(("parallel", "parallel")),
    )(*v)
    return [o.reshape(shape) for o in outs]


def adamw(w, g, m, v, name):
    def fn(w, g, m, v):
        m = ADAM_B1 * m + (1.0 - ADAM_B1) * g
        v = ADAM_B2 * v + (1.0 - ADAM_B2) * (g * g)
        m_hat = m / (1.0 - ADAM_B1 ** ADAM_STEP)
        v_hat = v / (1.0 - ADAM_B2 ** ADAM_STEP)
        delta = -ADAM_LR * (m_hat / (jnp.sqrt(v_hat) + ADAM_EPS) + ADAM_WD * w)
        return delta, m, v

    return _elementwise(fn, [w, g, m, v], 3, [F32, F32, F32], name)


def _coords():
    return lax.axis_index("x"), lax.axis_index("y"), lax.axis_index("c")


def _ici_peers(x, y):
    chips = [(1 - x, y), (x, 1 - y), (1 - x, 1 - y)]
    return chips, [2 * cx + cy for cx, cy in chips]


def _any_specs(n):
    return [pl.BlockSpec(memory_space=pl.ANY)] * n


def gather_weights(shards, name):
    na = len(shards)
    for s in shards:
        assert s.shape[0] % 2 == 0

    def body(*refs):
        ins = refs[:na]
        outs = refs[na:2 * na]
        send_sems, recv_sems = refs[2 * na:]
        x, y, c = _coords()
        me = 2 * x + y
        chips, chip_ids = _ici_peers(x, y)
        sibling = (x, y, 1 - c)
        copies = []
        for a in range(na):
            hl = ins[a].shape[0] // 2
            mine = pl.ds(c * hl, hl)
            theirs = pl.ds((1 - c) * hl, hl)
            first = []
            for k in range(3):
                cp = pltpu.make_async_remote_copy(
                    src_ref=ins[a].at[mine], dst_ref=outs[a].at[me, mine],
                    send_sem=send_sems.at[6 * a + k], recv_sem=recv_sems.at[6 * a + k],
                    device_id=(*chips[k], c), device_id_type=MESH)
                cp.start()
                first.append(cp)
            copies.append((first, mine, theirs))
        passed = []
        for a in range(na):
            first, mine, theirs = copies[a]
            for k in range(3):
                landed = outs[a].at[chip_ids[k], mine]
                pltpu.make_async_remote_copy(
                    src_ref=landed, dst_ref=landed, send_sem=send_sems.at[6 * a + k],
                    recv_sem=recv_sems.at[6 * a + k], device_id=(*chips[k], c), device_id_type=MESH).wait_recv()
                fw = pltpu.make_async_remote_copy(
                    src_ref=landed, dst_ref=landed, send_sem=send_sems.at[6 * a + 3 + k],
                    recv_sem=recv_sems.at[6 * a + 3 + k], device_id=sibling, device_id_type=MESH)
                fw.start()
                passed.append(fw)
        for a in range(na):
            first, mine, theirs = copies[a]
            for k in range(3):
                got = outs[a].at[chip_ids[k], theirs]
                pltpu.make_async_remote_copy(
                    src_ref=got, dst_ref=got, send_sem=send_sems.at[6 * a + 3 + k],
                    recv_sem=recv_sems.at[6 * a + 3 + k], device_id=sibling, device_id_type=MESH).wait_recv()
            for cp in first:
                cp.wait_send()
        for fw in passed:
            fw.wait_send()

    return pl.pallas_call(
        body, name=name, in_specs=_any_specs(na), out_specs=_any_specs(na),
        out_shape=[jax.ShapeDtypeStruct((4, *s.shape), s.dtype) for s in shards],
        scratch_shapes=[pltpu.SemaphoreType.DMA((6 * na,)), pltpu.SemaphoreType.DMA((6 * na,))],
        compiler_params=pltpu.CompilerParams(has_side_effects=True),
    )(*shards)


def send_halves_to_sibling(arrs, name):
    na = len(arrs)

    def body(*refs):
        ins = refs[:na]
        outs = refs[na:2 * na]
        send_sems, recv_sems = refs[2 * na:]
        x, y, c = _coords()
        cps = []
        for a in range(na):
            cp = pltpu.make_async_remote_copy(
                src_ref=ins[a].at[1 - c], dst_ref=outs[a], send_sem=send_sems.at[a], recv_sem=recv_sems.at[a],
                device_id=(x, y, 1 - c), device_id_type=MESH)
            cp.start()
            cps.append(cp)
        for cp in cps:
            cp.wait()

    return pl.pallas_call(
        body, name=name, in_specs=_any_specs(na), out_specs=_any_specs(na),
        out_shape=[jax.ShapeDtypeStruct(a.shape[1:], a.dtype) for a in arrs],
        scratch_shapes=[pltpu.SemaphoreType.DMA((na,)), pltpu.SemaphoreType.DMA((na,))],
        compiler_params=pltpu.CompilerParams(has_side_effects=True),
    )(*arrs)


def scatter_to_chips(arrs, name):
    na = len(arrs)

    def body(*refs):
        ins = refs[:na]
        outs = refs[na:2 * na]
        send_sems, recv_sems = refs[2 * na:]
        x, y, c = _coords()
        chips, chip_ids = _ici_peers(x, y)
        cps = []
        for a in range(na):
            for k in range(3):
                cp = pltpu.make_async_remote_copy(
                    src_ref=ins[a].at[chip_ids[k]], dst_ref=outs[a].at[k],
                    send_sem=send_sems.at[3 * a + k], recv_sem=recv_sems.at[3 * a + k],
                    device_id=(*chips[k], c), device_id_type=MESH)
                cp.start()
                cps.append(cp)
        for cp in cps:
            cp.wait()

    return pl.pallas_call(
        body, name=name, in_specs=_any_specs(na), out_specs=_any_specs(na),
        out_shape=[jax.ShapeDtypeStruct((3, *a.shape[1:]), a.dtype) for a in arrs],
        scratch_shapes=[pltpu.SemaphoreType.DMA((3 * na,)), pltpu.SemaphoreType.DMA((3 * na,))],
        compiler_params=pltpu.CompilerParams(has_side_effects=True),
    )(*arrs)


def exchange_halves(arrs, name):
    na = len(arrs)

    def body(*refs):
        ins = refs[:na]
        outs = refs[na:2 * na]
        send_sems, recv_sems = refs[2 * na:]
        x, y, c = _coords()
        cps = []
        for a in range(na):
            cp = pltpu.make_async_remote_copy(
                src_ref=ins[a], dst_ref=outs[a], send_sem=send_sems.at[a], recv_sem=recv_sems.at[a],
                device_id=(x, y, 1 - c), device_id_type=MESH)
            cp.start()
            cps.append(cp)
        for cp in cps:
            cp.wait()

    return pl.pallas_call(
        body, name=name, in_specs=_any_specs(na), out_specs=_any_specs(na),
        out_shape=[jax.ShapeDtypeStruct(a.shape, a.dtype) for a in arrs],
        scratch_shapes=[pltpu.SemaphoreType.DMA((na,)), pltpu.SemaphoreType.DMA((na,))],
        compiler_params=pltpu.CompilerParams(has_side_effects=True),
    )(*arrs)


def allreduce_small(buf, name):
    r, cdim = buf.shape

    def body(x_ref, o_ref, gath, send_sems, recv_sems):
        x, y, c = _coords()
        me, sibling = (x, y, c), (x, y, 1 - c)
        chips, _ = _ici_peers(x, y)

        def slot(px, py, pc):
            return gath.at[4 * px + 2 * py + pc]

        def copy(k, block, to, src=None):
            return pltpu.make_async_remote_copy(
                src_ref=slot(*block) if src is None else src, dst_ref=slot(*block),
                send_sem=send_sems.at[k], recv_sem=recv_sems.at[k], device_id=to, device_id_type=MESH)

        gath[4 * x + 2 * y + c] = x_ref[...]
        first = [copy(0, me, sibling, src=x_ref)]
        first += [copy(1 + j, me, (*chip, c), src=x_ref) for j, chip in enumerate(chips)]
        for cp in first:
            cp.start()
        passed = [copy(4 + j, (*chip, c), sibling) for j, chip in enumerate(chips)]
        for j, chip in enumerate(chips):
            copy(1 + j, (*chip, c), me).wait_recv()
            passed[j].start()
        copy(0, sibling, me).wait_recv()
        for j, chip in enumerate(chips):
            copy(4 + j, (*chip, 1 - c), me).wait_recv()
        for cp in first + passed:
            cp.wait_send()
        acc = gath[0]
        for d in range(1, 8):
            acc = acc + gath[d]
        o_ref[...] = acc

    return pl.pallas_call(
        body, name=name,
        in_specs=[pl.BlockSpec(memory_space=pltpu.VMEM)], out_specs=pl.BlockSpec(memory_space=pltpu.VMEM),
        out_shape=jax.ShapeDtypeStruct((r, cdim), F32),
        scratch_shapes=[pltpu.VMEM((8, r, cdim), F32), pltpu.SemaphoreType.DMA((7,)), pltpu.SemaphoreType.DMA((7,))],
        compiler_params=pltpu.CompilerParams(has_side_effects=True),
    )(buf)


def _expand_heads(v):
    return jnp.repeat(v.astype(F32), HEAD_DIM).reshape(1, -1)


def _pad_lanes(v):
    return jnp.pad(v.astype(F32), (0, LANES - v.shape[0])).reshape(1, LANES)


def local_step(x, tgt, p):
    nl = p["w_up"].shape[0]
    d = x.shape[1]
    cw = p["short_conv_w"].shape[2]
    si = p["ssd_norm_w"].shape[1]
    nh = si // HEAD_DIM
    gn = GROUPS * STATE
    cols = p["w_in"].shape[2]
    npad = -(-cols // LANES) * LANES
    dt_off = 3 * cw + si + si + 2 * gn
    assert cols == dt_off + nh and nh <= LANES

    w_in = jnp.pad(p["w_in"], ((0, 0), (0, 0), (0, npad - cols)))
    w_in_t = jnp.swapaxes(w_in, 1, 2)
    w_out_t = jnp.swapaxes(p["w_out"], 1, 2)
    w_up_t = jnp.swapaxes(p["w_up"], 1, 2)
    w_down_t = jnp.swapaxes(p["w_down"], 1, 2)
    pieces = [(0, cw), (cw, cw), (2 * cw, cw), (3 * cw, si), (3 * cw + si, si), (3 * cw + 2 * si, 2 * gn),
              (dt_off, LANES)]

    saved = []
    for l in range(nl):
        nw1 = p["norm_mix_w"][l].reshape(1, d)
        nw2 = p["norm_mlp_w"][l].reshape(1, d)
        kw3 = p["short_conv_w"][l]
        kwx, kwb = p["ssd_conv_w"][l][:, :si], p["ssd_conv_w"][l][:, si:]
        bx, bb = p["ssd_conv_b"][l][:si].reshape(1, si), p["ssd_conv_b"][l][si:].reshape(1, 2 * gn)
        dtb = _pad_lanes(p["dt_bias"][l])
        aneg = _pad_lanes(-jnp.exp(p["a_log"][l]))
        dsk = _expand_heads(p["d_skip"][l])
        snw = p["ssd_norm_w"][l].reshape(1, si)
        ssd_args = (kwx, kwb, bx, bb, dtb, aneg, dsk, snw)

        proj, h = norm_matmul(x, nw1, w_in[l], BF16, "in_proj")
        dt_raw = norm_matmul(x, nw1, w_in[l][:, dt_off:dt_off + LANES], F32, "dt_proj", emit_h=False)
        y_a = conv_mixer_fwd(proj, kw3, cw, "conv_mixer_fwd")
        y_b, ys, hsave = ssd_fwd(proj, dt_raw, *ssd_args, cw, si, "ssd_fwd")
        x2 = matmul([y_a, y_b], [p["w_out"][l][:cw], p["w_out"][l][cw:]], F32, "out_proj", residual=x)
        up, h2 = norm_matmul(x2, nw2, p["w_up"][l], BF16, "up_proj")
        x3 = matmul([up], [p["w_down"][l]], F32, "down_proj",
                    lhs_fn=lambda v: jnp.square(jnp.maximum(v, 0.0)), residual=x2)
        saved.append((x, h, proj, dt_raw, y_a, y_b, ys, hsave, x2, h2, up, nw1, nw2, kw3, ssd_args))
        x = x3

    dx, dwf, lvec = final_loss(x, p["final_norm_w"].reshape(1, d), tgt, "final_loss")
    loss = jnp.sum(lvec)

    grads = {k: [None] * nl for k in ("norm_mix_w", "w_in", "short_conv_w", "ssd_conv_w", "ssd_conv_b", "dt_bias",
                                      "a_log", "d_skip", "ssd_norm_w", "w_out", "norm_mlp_w", "w_up", "w_down")}
    relu2 = lambda v: jnp.square(jnp.maximum(v, 0.0))
    for l in reversed(range(nl)):
        x0, h, proj, dt_raw, y_a, y_b, ys, hsave, x2, h2, up, nw1, nw2, kw3, ssd_args = saved[l]
        dup = matmul([dx], [w_down_t[l]], BF16, "down_bwd", relu_gate=up)
        grads["w_down"][l] = matmul_tn(up, dx, "down_wgrad", a_fn=relu2)
        dx2, dnw2 = matmul_normbwd([dup], [w_up_t[l]], x2, nw2, dx, "up_bwd")
        grads["w_up"][l] = matmul_tn(h2, dup, "up_wgrad")
        grads["norm_mlp_w"][l] = dnw2.reshape(d)
        dy = matmul([dx2], [w_out_t[l]], BF16, "out_bwd")
        grads["w_out"][l] = jnp.concatenate(
            [matmul_tn(y_a, dx2, "out_wgrad_a"), matmul_tn(y_b, dx2, "out_wgrad_b")], axis=0)
        dub, duc, duh, dkw3 = conv_mixer_bwd(proj, dy, kw3, cw, "conv_mixer_bwd")
        (dz, dxs, dbc, ddt, dkwx, dkwb, dbx, dbb, ddtb, da, ddsk, dsnw) = ssd_bwd(
            proj, dt_raw, ys, hsave, dy, *ssd_args, cw, si, "ssd_bwd")
        dpieces = [dub, duc, duh, dz, dxs, dbc, ddt]
        wts = [w_in_t[l][o:o + n] for o, n in pieces]
        dxl, dnw1 = matmul_normbwd(dpieces, wts, x0, nw1, dx2, "in_bwd")
        dwin = jnp.concatenate([matmul_tn(h, dp, "in_wgrad_%d" % i) for i, dp in enumerate(dpieces)], axis=1)
        grads["w_in"][l] = dwin[:, :cols]
        grads["norm_mix_w"][l] = dnw1.reshape(d)
        grads["short_conv_w"][l] = dkw3
        grads["ssd_conv_w"][l] = jnp.concatenate([dkwx, dkwb], axis=1)
        grads["ssd_conv_b"][l] = jnp.concatenate([dbx, dbb], axis=1).reshape(-1)
        grads["dt_bias"][l] = ddtb[0, :nh]
        grads["a_log"][l] = da[0, :nh] * ssd_args[5][0, :nh]
        grads["d_skip"][l] = jnp.sum(ddsk.reshape(nh, HEAD_DIM), axis=1)
        grads["ssd_norm_w"][l] = dsnw.reshape(si)
        dx = dxl

    grads = {k: jnp.stack(v) for k, v in grads.items()}
    grads["final_norm_w"] = dwf.reshape(d)
    return loss, dx, grads


BIG = ("w_in", "w_out", "w_up", "w_down")
SHARD_AXIS = {"w_in": 2, "w_out": 1, "w_up": 2, "w_down": 1, "short_conv_w": 2, "ssd_conv_w": 2}
SMALL_SHARDED = ("short_conv_w", "ssd_conv_w")
SMALL_REPL = ("norm_mix_w", "ssd_conv_b", "dt_bias", "a_log", "d_skip", "ssd_norm_w", "norm_mlp_w", "final_norm_w")
WEIGHTS = ("norm_mix_w", "w_in", "short_conv_w", "ssd_conv_w", "ssd_conv_b", "dt_bias", "a_log", "d_skip",
           "ssd_norm_w", "w_out", "norm_mlp_w", "w_up", "w_down", "final_norm_w")
SMALL_COLS = 1024


def _unshard(g4, own, chip, axis):
    return jnp.concatenate([jnp.where(chip == j, own, g4[j]) for j in range(4)], axis=axis)


def _to_chips(full, axis):
    parts = jnp.stack(jnp.split(full, 4, axis=axis))
    nl = full.shape[0]
    parts = parts.reshape(4, 2, nl // 2, *parts.shape[2:])
    return jnp.swapaxes(parts, 0, 1)


def _pack_small(named):
    flat = jnp.concatenate([v.reshape(-1).astype(F32) for v in named])
    n = flat.shape[0]
    rows = -(-n // SMALL_COLS)
    rows = -(-rows // 8) * 8
    return jnp.pad(flat, (0, rows * SMALL_COLS - n)).reshape(rows, SMALL_COLS)


def _unpack_small(buf, like):
    flat = buf.reshape(-1)
    out, off = [], 0
    for v in like:
        out.append(flat[off:off + v.size].reshape(v.shape))
        off += v.size
    return out


def kernel(x, norm_mix_w, w_in, short_conv_w, ssd_conv_w, ssd_conv_b, dt_bias, a_log, d_skip, ssd_norm_w, w_out, norm_mlp_w, w_up, w_down, final_norm_w, loss_target, m_norm_mix_w, m_w_in, m_short_conv_w, m_ssd_conv_w, m_ssd_conv_b, m_dt_bias, m_a_log, m_d_skip, m_ssd_norm_w, m_w_out, m_norm_mlp_w, m_w_up, m_w_down, m_final_norm_w, v_norm_mix_w, v_w_in, v_short_conv_w, v_ssd_conv_w, v_ssd_conv_b, v_dt_bias, v_a_log, v_d_skip, v_ssd_norm_w, v_w_out, v_norm_mlp_w, v_w_up, v_w_down, v_final_norm_w):
    w = dict(norm_mix_w=norm_mix_w, w_in=w_in, short_conv_w=short_conv_w, ssd_conv_w=ssd_conv_w,
             ssd_conv_b=ssd_conv_b, dt_bias=dt_bias, a_log=a_log, d_skip=d_skip, ssd_norm_w=ssd_norm_w, w_out=w_out,
             norm_mlp_w=norm_mlp_w, w_up=w_up, w_down=w_down, final_norm_w=final_norm_w)
    m = dict(norm_mix_w=m_norm_mix_w, w_in=m_w_in, short_conv_w=m_short_conv_w, ssd_conv_w=m_ssd_conv_w,
             ssd_conv_b=m_ssd_conv_b, dt_bias=m_dt_bias, a_log=m_a_log, d_skip=m_d_skip, ssd_norm_w=m_ssd_norm_w,
             w_out=m_w_out, norm_mlp_w=m_norm_mlp_w, w_up=m_w_up, w_down=m_w_down, final_norm_w=m_final_norm_w)
    v = dict(norm_mix_w=v_norm_mix_w, w_in=v_w_in, short_conv_w=v_short_conv_w, ssd_conv_w=v_ssd_conv_w,
             ssd_conv_b=v_ssd_conv_b, dt_bias=v_dt_bias, a_log=v_a_log, d_skip=v_d_skip, ssd_norm_w=v_ssd_norm_w,
             w_out=v_w_out, norm_mlp_w=v_norm_mlp_w, w_up=v_w_up, w_down=v_w_down, final_norm_w=v_final_norm_w)
    xi, yi, ci = lax.axis_index("x"), lax.axis_index("y"), lax.axis_index("c")
    chip = 2 * xi + yi

    shards = [w[k].astype(BF16) for k in BIG] + [w[k] for k in SMALL_SHARDED]
    gathered = gather_weights(shards, "gather_weights")
    full = dict(w)
    for k, own, g4 in zip(BIG + SMALL_SHARDED, shards, gathered):
        full[k] = _unshard(g4, own, chip, SHARD_AXIS[k])

    loss, grad_x, grads = local_step(x[0], loss_target[0], full)
    loss = lax.psum(loss, ("x", "y", "c"))

    by_chip = [_to_chips(grads[k], SHARD_AXIS[k]) for k in BIG]
    from_sibling = send_halves_to_sibling(by_chip, "grads_to_sibling")
    mine = [lax.dynamic_index_in_dim(a, ci, 0, keepdims=False) for a in by_chip]
    chip_sums = [_elementwise(lambda a, b: (a + b, a + b), [a, b], 2, [F32, BF16], "chip_sum_%s" % k)
                 for k, a, b in zip(BIG, mine, from_sibling)]
    from_chips = scatter_to_chips([s[1] for s in chip_sums], "grads_to_chips")
    reduced = []
    for k, s, r3 in zip(BIG, chip_sums, from_chips):
        own = lax.dynamic_index_in_dim(s[0], chip, 0, keepdims=False)
        red, = _elementwise(lambda a, b, c_, d_: (a + b.astype(F32) + c_.astype(F32) + d_.astype(F32),),
                            [own, r3[0], r3[1], r3[2]], 1, [F32], "grad_sum_%s" % k)
        reduced.append(red)
    theirs = exchange_halves(reduced, "grads_exchange")
    g_shard = {k: jnp.concatenate([jnp.where(ci == 0, a, b), jnp.where(ci == 0, b, a)], axis=0)
               for k, a, b in zip(BIG, reduced, theirs)}

    small_names = SMALL_REPL + SMALL_SHARDED
    small_sum = allreduce_small(_pack_small([grads[k] for k in small_names]), "allreduce_small")
    for k, g in zip(small_names, _unpack_small(small_sum, [grads[k] for k in small_names])):
        if k in SMALL_SHARDED:
            width = w[k].shape[2]
            g = lax.dynamic_slice_in_dim(g, chip * width, width, axis=2)
        g_shard[k] = g

    delta, new_m, new_v = {}, {}, {}
    for k in BIG:
        delta[k], new_m[k], new_v[k] = adamw(w[k], g_shard[k], m[k], v[k], "adamw_%s" % k)
    names = SMALL_REPL + SMALL_SHARDED
    packed = [_pack_small([d_[k] for k in names]) for d_ in (w, g_shard, m, v)]
    outs = adamw(*packed, "adamw_small")
    for d_, buf in zip((delta, new_m, new_v), outs):
        for k, val in zip(names, _unpack_small(buf, [w[k] for k in names])):
            d_[k] = val

    return (loss, grad_x[None], *[g_shard[k] for k in WEIGHTS], *[delta[k] for k in WEIGHTS],
            *[new_m[k] for k in WEIGHTS], *[new_v[k] for k in WEIGHTS])
```

```python
import jax
import jax.numpy as jnp
from jax import lax
from jax.experimental import pallas as pl
from jax.experimental.pallas import tpu as pltpu

F32 = jnp.float32
BF16 = jnp.bfloat16

EPS = 1e-5
HEAD_DIM = 64
STATE = 128
GROUPS = 2
SHORT_K = 3
SSD_K = 4
LANES = 128
PAIR = LANES // HEAD_DIM
SCAN_CHUNK = 256
HALO = 16
N_CHIPS = 4
VMEM_LIMIT = 56 * 1024 * 1024

ADAM_LR = 0.001
ADAM_B1 = 0.9
ADAM_B2 = 0.999
ADAM_EPS = 1e-08
ADAM_WD = 0.01
ADAM_STEP = 10

MESH = pl.DeviceIdType.MESH


def _params(sem):
    return pltpu.CompilerParams(dimension_semantics=sem, vmem_limit_bytes=VMEM_LIMIT)


def _tile(n, cap, quantum):
    if n <= cap:
        return n
    best = None
    for t in range(quantum, cap + 1, quantum):
        if n % t == 0:
            best = t
    assert best is not None, (n, cap, quantum)
    return best


def _dot(a, b):
    return jnp.dot(a, b, preferred_element_type=F32)


def _dot_nt(a, b):
    return lax.dot_general(a, b, (((1,), (1,)), ((), ())), preferred_element_type=F32)


def _dot_tn(a, b):
    return lax.dot_general(a, b, (((0,), (0,)), ((), ())), preferred_element_type=F32)


def _dot_exact(a, b):
    return jnp.dot(a, b, precision=lax.Precision.HIGHEST, preferred_element_type=F32)


def _sigmoid(x):
    return 1.0 / (1.0 + jnp.exp(-x))


def _softplus(x):
    return jnp.maximum(x, 0.0) + jnp.log(1.0 + jnp.exp(-jnp.abs(x)))


def _relu2(v):
    return jnp.square(jnp.maximum(v, 0.0))


def norm_matmul(x, nw, w, layer, n, col0, out_dtype, name, emit_h=True):
    t, d = x.shape
    tm = _tile(t, 1024, 8)
    tn = _tile(n, 1536, LANES)
    nj = n // tn

    def body(x_ref, nw_ref, w_ref, o_ref, h_ref):
        @pl.when(pl.program_id(1) == 0)
        def _():
            xf = x_ref[...]
            r = lax.rsqrt(jnp.mean(xf * xf, axis=-1, keepdims=True) + EPS)
            h_ref[...] = (xf * r * nw_ref[...]).astype(BF16)

        o_ref[...] = _dot(h_ref[...], w_ref[...]).astype(out_dtype)

    out_specs = [pl.BlockSpec((tm, tn), lambda i, j: (i, j))]
    out_shape = [jax.ShapeDtypeStruct((t, n), out_dtype)]
    if emit_h:
        out_specs.append(pl.BlockSpec((tm, d), lambda i, j: (i, 0)))
        out_shape.append(jax.ShapeDtypeStruct((t, d), BF16))
    outs = pl.pallas_call(
        body, name=name, grid=(t // tm, nj),
        in_specs=[pl.BlockSpec((tm, d), lambda i, j: (i, 0)),
                  pl.BlockSpec((1, d), lambda i, j: (0, 0)),
                  pl.BlockSpec((None, d, tn), lambda i, j: (layer, 0, col0 * nj + j))],
        out_specs=out_specs, out_shape=out_shape,
        scratch_shapes=[] if emit_h else [pltpu.VMEM((tm, d), BF16)],
        compiler_params=_params(("parallel", "arbitrary")),
    )(x, nw, w)
    return outs if emit_h else outs[0]


def matmul(lhs, w, layer, transposed, n, out_dtype, name, *, lhs_fn=None, residual=None, relu_gate=None):
    t, k = lhs.shape
    big = k > 2048
    tm = _tile(t, 512 if big else 1024, 8)
    tn = _tile(n, 512 if big else 1024, LANES)
    staged = lhs.dtype != BF16 or lhs_fn is not None
    fn = lhs_fn if lhs_fn is not None else (lambda v: v)
    has_extra = residual is not None or relu_gate is not None
    dot = _dot_nt if transposed else _dot

    def body(*refs):
        a_ref, w_ref = refs[:2]
        extra = refs[2] if has_extra else None
        o_ref = refs[3] if has_extra else refs[2]
        if staged:
            s_ref = refs[-1]

            @pl.when(pl.program_id(1) == 0)
            def _():
                s_ref[...] = fn(a_ref[...].astype(F32)).astype(BF16)

            a_ref = s_ref
        acc = dot(a_ref[...], w_ref[...])
        if residual is not None:
            acc = acc + extra[...]
        if relu_gate is not None:
            acc = acc * (2.0 * jnp.maximum(extra[...].astype(F32), 0.0))
        o_ref[...] = acc.astype(out_dtype)

    if transposed:
        w_spec = pl.BlockSpec((None, tn, k), lambda i, j: (layer, j, 0))
    else:
        w_spec = pl.BlockSpec((None, k, tn), lambda i, j: (layer, 0, j))
    in_specs = [pl.BlockSpec((tm, k), lambda i, j: (i, 0)), w_spec]
    args = [lhs, w]
    if has_extra:
        in_specs.append(pl.BlockSpec((tm, tn), lambda i, j: (i, j)))
        args.append(residual if residual is not None else relu_gate)
    return pl.pallas_call(
        body, name=name, grid=(t // tm, n // tn), in_specs=in_specs,
        out_specs=pl.BlockSpec((tm, tn), lambda i, j: (i, j)),
        out_shape=jax.ShapeDtypeStruct((t, n), out_dtype),
        scratch_shapes=[pltpu.VMEM((tm, k), BF16)] if staged else [],
        compiler_params=_params(("parallel", "arbitrary")),
    )(*args)


def matmul_normbwd(lhs, pieces, w, layer, x, nw, dres, name):
    t, d = x.shape
    nl = len(lhs)
    tm = _tile(t, 256, 8)
    for off, width in pieces:
        assert off % width == 0

    def body(*refs):
        lrefs = refs[:nl]
        wrefs = refs[nl:2 * nl]
        x_ref, nw_ref, dres_ref, dx_ref, dnw_ref = refs[2 * nl:]
        dh = _dot_nt(lrefs[0][...].astype(BF16), wrefs[0][...])
        for a_ref, w_ref in zip(lrefs[1:], wrefs[1:]):
            dh = dh + _dot_nt(a_ref[...].astype(BF16), w_ref[...])
        xf = x_ref[...]
        r = lax.rsqrt(jnp.mean(xf * xf, axis=-1, keepdims=True) + EPS)
        nx = xf * r
        dn = dh * nw_ref[...]
        dx = r * (dn - nx * jnp.mean(dn * nx, axis=-1, keepdims=True))
        dx_ref[...] = dres_ref[...] + dx

        @pl.when(pl.program_id(0) == 0)
        def _():
            dnw_ref[...] = jnp.zeros_like(dnw_ref)

        dnw_ref[...] += jnp.sum(dh * nx, axis=0, keepdims=True)

    in_specs = [pl.BlockSpec((tm, width), lambda i: (i, 0)) for _, width in pieces]
    in_specs += [pl.BlockSpec((None, d, width), (lambda blk: (lambda i: (layer, 0, blk)))(off // width))
                 for off, width in pieces]
    in_specs += [pl.BlockSpec((tm, d), lambda i: (i, 0)), pl.BlockSpec((1, d), lambda i: (0, 0)),
                 pl.BlockSpec((tm, d), lambda i: (i, 0))]
    return pl.pallas_call(
        body, name=name, grid=(t // tm,), in_specs=in_specs,
        out_specs=[pl.BlockSpec((tm, d), lambda i: (i, 0)), pl.BlockSpec((1, d), lambda i: (0, 0))],
        out_shape=[jax.ShapeDtypeStruct((t, d), F32), jax.ShapeDtypeStruct((1, d), F32)],
        compiler_params=_params(("arbitrary",)),
    )(*lhs, *([w] * nl), x, nw, dres)


def matmul_tn(a, b, name, *, a_fn=None, by_chip=False):
    t, k = a.shape
    n = b.shape[1]
    tk = _tile(k, 1024, LANES)
    tn = _tile(n // N_CHIPS if by_chip else n, 1024, LANES)
    tt = _tile(t, 1024, 8)
    nt = t // tt
    fn = a_fn if a_fn is not None else (lambda v: v)

    def body(a_ref, b_ref, o_ref, acc_ref):
        @pl.when(pl.program_id(2) == 0)
        def _():
            acc_ref[...] = jnp.zeros_like(acc_ref)

        av = a_ref[...]
        if a_fn is not None:
            av = fn(av.astype(F32))
        acc_ref[...] += _dot_tn(av.astype(BF16), b_ref[...].astype(BF16))

        @pl.when(pl.program_id(2) == nt - 1)
        def _():
            o_ref[...] = acc_ref[...]

    if by_chip:
        per = n // N_CHIPS // tn
        out_spec = pl.BlockSpec((None, tk, tn), lambda i, j, s: (j // per, i, j % per))
        out_shape = jax.ShapeDtypeStruct((N_CHIPS, k, n // N_CHIPS), F32)
    else:
        out_spec = pl.BlockSpec((tk, tn), lambda i, j, s: (i, j))
        out_shape = jax.ShapeDtypeStruct((k, n), F32)
    return pl.pallas_call(
        body, name=name, grid=(k // tk, n // tn, nt),
        in_specs=[pl.BlockSpec((tt, tk), lambda i, j, s: (s, i)),
                  pl.BlockSpec((tt, tn), lambda i, j, s: (s, j))],
        out_specs=out_spec, out_shape=out_shape,
        scratch_shapes=[pltpu.VMEM((tk, tn), F32)],
        compiler_params=_params(("parallel", "parallel", "arbitrary")),
    )(a, b)


def split_to_chips(pieces, cols, name):
    d = pieces[0].shape[0]
    widths = [p.shape[1] for p in pieces]
    w = cols // N_CHIPS
    tr = _tile(d, 256, 8)
    npc = len(pieces)

    def body(*refs):
        o_ref, row = refs[npc], refs[npc + 1]
        off = 0
        for r, n in zip(refs[:npc], widths):
            row[:, off:off + n] = r[...]
            off += n
        for j in range(N_CHIPS):
            o_ref[j] = row[:, j * w:(j + 1) * w]

    return pl.pallas_call(
        body, name=name, grid=(d // tr,),
        in_specs=[pl.BlockSpec((tr, n), lambda i: (i, 0)) for n in widths],
        out_specs=pl.BlockSpec((N_CHIPS, tr, w), lambda i: (0, i, 0)),
        out_shape=jax.ShapeDtypeStruct((N_CHIPS, d, w), F32),
        scratch_shapes=[pltpu.VMEM((tr, sum(widths)), F32)],
        compiler_params=_params(("parallel",)),
    )(*pieces)


def join_from_chips(g4, npad, name):
    _, nl, d, w = g4.shape
    tr = _tile(d, 256, HALO)

    def body(g_ref, o_ref):
        for j in range(N_CHIPS):
            o_ref[:, j * w:(j + 1) * w] = g_ref[j]
        o_ref[:, N_CHIPS * w:] = jnp.zeros((tr, npad - N_CHIPS * w), o_ref.dtype)

    return pl.pallas_call(
        body, name=name, grid=(nl, d // tr),
        in_specs=[pl.BlockSpec((N_CHIPS, None, tr, w), lambda l, i: (0, l, i, 0))],
        out_specs=pl.BlockSpec((None, tr, npad), lambda l, i: (l, i, 0)),
        out_shape=jax.ShapeDtypeStruct((nl, d, npad), g4.dtype),
        compiler_params=_params(("parallel", "parallel")),
    )(g4)


def conv_mixer_fwd(proj, kw, cw, out_cols, name):
    t = proj.shape[0]
    tm = _tile(t, 512, HALO)
    tc = _tile(cw, 512, LANES)
    nj = cw // tc
    hb = tm // HALO

    def body(ub_ref, uc_ref, uh_ref, ucp_ref, uhp_ref, kw_ref, y_ref, ext):
        i = pl.program_id(0)
        v = uc_ref[...].astype(F32) * uh_ref[...].astype(F32)
        vp = ucp_ref[...].astype(F32) * uhp_ref[...].astype(F32)
        ext[0:HALO, :] = jnp.where(i > 0, vp, 0.0)
        ext[HALO:HALO + tm, :] = v
        cv = kw_ref[pl.ds(SHORT_K - 1, 1), :] * v
        for k in range(SHORT_K - 1):
            cv = cv + kw_ref[pl.ds(k, 1), :] * ext[pl.ds(HALO - (SHORT_K - 1) + k, tm), :]
        y_ref[...] = (ub_ref[...].astype(F32) * cv).astype(BF16)

    prev = lambda off: (lambda i, j: (jnp.maximum(i * hb - 1, 0), off + j))
    return pl.pallas_call(
        body, name=name, grid=(t // tm, nj),
        in_specs=[pl.BlockSpec((tm, tc), lambda i, j: (i, j)),
                  pl.BlockSpec((tm, tc), lambda i, j: (i, nj + j)),
                  pl.BlockSpec((tm, tc), lambda i, j: (i, 2 * nj + j)),
                  pl.BlockSpec((HALO, tc), prev(nj)),
                  pl.BlockSpec((HALO, tc), prev(2 * nj)),
                  pl.BlockSpec((SHORT_K, tc), lambda i, j: (0, j))],
        out_specs=pl.BlockSpec((tm, tc), lambda i, j: (i, j)),
        out_shape=jax.ShapeDtypeStruct((t, out_cols), BF16),
        scratch_shapes=[pltpu.VMEM((tm + HALO, tc), F32)],
        compiler_params=_params(("parallel", "parallel")),
    )(proj, proj, proj, proj, proj, kw)


def conv_mixer_bwd(proj, dy, kw, cw, name):
    t = proj.shape[0]
    tm = _tile(t, 512, HALO)
    tc = _tile(cw, 512, LANES)
    nj = cw // tc
    hb = tm // HALO
    ni = t // tm
    last_hb = t // HALO - 1

    def body(ub_ref, uc_ref, uh_ref, dy_ref, ucp_ref, uhp_ref, ubn_ref, dyn_ref, kw_ref,
             dub_ref, duc_ref, duh_ref, dkw_ref, ext, extd):
        i = pl.program_id(1)
        ub = ub_ref[...].astype(F32)
        uc = uc_ref[...].astype(F32)
        uh = uh_ref[...].astype(F32)
        dyv = dy_ref[...].astype(F32)
        v = uc * uh
        vp = ucp_ref[...].astype(F32) * uhp_ref[...].astype(F32)
        ext[0:HALO, :] = jnp.where(i > 0, vp, 0.0)
        ext[HALO:HALO + tm, :] = v
        cv = kw_ref[pl.ds(SHORT_K - 1, 1), :] * v
        for k in range(SHORT_K - 1):
            cv = cv + kw_ref[pl.ds(k, 1), :] * ext[pl.ds(HALO - (SHORT_K - 1) + k, tm), :]
        dcv = dyv * ub
        dcvn = dyn_ref[...].astype(F32) * ubn_ref[...].astype(F32)
        extd[0:tm, :] = dcv
        extd[tm:tm + HALO, :] = jnp.where(i < ni - 1, dcvn, 0.0)
        dv = kw_ref[pl.ds(SHORT_K - 1, 1), :] * dcv
        for k in range(SHORT_K - 1):
            dv = dv + kw_ref[pl.ds(k, 1), :] * extd[pl.ds(SHORT_K - 1 - k, tm), :]
        dub_ref[...] = (dyv * cv).astype(BF16)
        duc_ref[...] = (dv * uh).astype(BF16)
        duh_ref[...] = (dv * uc).astype(BF16)

        @pl.when(i == 0)
        def _():
            dkw_ref[...] = jnp.zeros_like(dkw_ref)

        for k in range(SHORT_K):
            sh = ext[pl.ds(HALO - (SHORT_K - 1) + k, tm), :]
            dkw_ref[pl.ds(k, 1), :] += jnp.sum(dcv * sh, axis=0, keepdims=True)

    prev = lambda off: (lambda j, i: (jnp.maximum(i * hb - 1, 0), off + j))
    nxt = lambda off: (lambda j, i: (jnp.minimum((i + 1) * hb, last_hb), off + j))
    cur = lambda off: (lambda j, i: (i, off + j))
    return pl.pallas_call(
        body, name=name, grid=(nj, ni),
        in_specs=[pl.BlockSpec((tm, tc), cur(0)), pl.BlockSpec((tm, tc), cur(nj)),
                  pl.BlockSpec((tm, tc), cur(2 * nj)), pl.BlockSpec((tm, tc), cur(0)),
                  pl.BlockSpec((HALO, tc), prev(nj)), pl.BlockSpec((HALO, tc), prev(2 * nj)),
                  pl.BlockSpec((HALO, tc), nxt(0)), pl.BlockSpec((HALO, tc), nxt(0)),
                  pl.BlockSpec((SHORT_K, tc), lambda j, i: (0, j))],
        out_specs=[pl.BlockSpec((tm, tc), cur(0)), pl.BlockSpec((tm, tc), cur(0)),
                   pl.BlockSpec((tm, tc), cur(0)), pl.BlockSpec((SHORT_K, tc), lambda j, i: (0, j))],
        out_shape=[jax.ShapeDtypeStruct((t, cw), BF16)] * 3 + [jax.ShapeDtypeStruct((SHORT_K, cw), F32)],
        scratch_shapes=[pltpu.VMEM((tm + HALO, tc), F32), pltpu.VMEM((tm + HALO, tc), F32)],
        compiler_params=_params(("parallel", "arbitrary")),
    )(proj, proj, proj, dy, proj, proj, proj, dy, kw)


def _head_column(mat, lane, h):
    return jnp.sum(jnp.where(lane == h, mat, 0.0), axis=-1, keepdims=True)


def _ssd_common(dt_raw_ref, dtb_ref, aneg_ref, cum_s, cumt_s, chunk):
    dt = _softplus(dt_raw_ref[...] + dtb_ref[...])
    al = dt * aneg_ref[...]
    ri = lax.broadcasted_iota(jnp.int32, (chunk, chunk), 0)
    ci = lax.broadcasted_iota(jnp.int32, (chunk, chunk), 1)
    cum = _dot_exact((ri >= ci).astype(F32), al)
    cum_s[...] = cum
    cumt_s[...] = cum.T
    return dt, cum, ri >= ci


def _causal_conv(ext, kw_ref, b_ref, chunk, kk):
    acc = b_ref[...] + kw_ref[pl.ds(kk - 1, 1), :] * ext[pl.ds(8, chunk), :]
    for k in range(kk - 1):
        acc = acc + kw_ref[pl.ds(k, 1), :] * ext[pl.ds(8 - (kk - 1) + k, chunk), :]
    return acc


def ssd_fwd(proj, dt_raw, y_mix, kw_xs, kw_bc, b_xs, b_bc, dtb, aneg, dskip, normw, cw, si, name):
    t = proj.shape[0]
    ch = min(SCAN_CHUNK, t)
    nc = t // ch
    npair = si // LANES
    ppg = npair // GROUPS
    gn = GROUPS * STATE
    gw = si // GROUPS
    assert cw == si and (3 * cw + 2 * si) % (2 * gn) == 0
    zblk = 3 * cw // si
    xsblk = zblk + 1
    bcblk = (3 * cw + 2 * si) // (2 * gn)

    def body(z_ref, xs_ref, bc_ref, dtr_ref, ymix_ref, kwx_ref, kwb_ref, bx_ref, bb_ref, dtb_ref, aneg_ref, dsk_ref,
             nw_ref, yb_ref, ys_ref, hs_ref, extx, extb, xs_s, bc_s, h_s, gated_s, s_s, cum_s, cumt_s):
        del ymix_ref
        c = pl.program_id(0)

        @pl.when(c == 0)
        def _():
            h_s[...] = jnp.zeros_like(h_s)
            extx[0:8, :] = jnp.zeros((8, si), F32)
            extb[0:8, :] = jnp.zeros((8, 2 * gn), F32)

        @pl.when(c > 0)
        def _():
            extx[0:8, :] = extx[ch:ch + 8, :]
            extb[0:8, :] = extb[ch:ch + 8, :]

        extx[8:8 + ch, :] = xs_ref[...].astype(F32)
        extb[8:8 + ch, :] = bc_ref[...].astype(F32)
        xc = _causal_conv(extx, kwx_ref, bx_ref, ch, SSD_K)
        xs_s[...] = xc * _sigmoid(xc)
        bcc = _causal_conv(extb, kwb_ref, bb_ref, ch, SSD_K)
        bc_s[...] = (bcc * _sigmoid(bcc)).astype(BF16)

        dt, cum, tril = _ssd_common(dtr_ref, dtb_ref, aneg_ref, cum_s, cumt_s, ch)
        lane = lax.broadcasted_iota(jnp.int32, (ch, LANES), 1)
        lane1 = lax.broadcasted_iota(jnp.int32, (1, LANES), 1)
        low = lane < HEAD_DIM
        clast = cum_s[pl.ds(ch - 1, 1), :]

        for p in range(npair):
            g = p // ppg
            col = slice(p * LANES, (p + 1) * LANES)
            bg = bc_s[:, g * STATE:(g + 1) * STATE]
            cg = bc_s[:, gn + g * STATE:gn + (g + 1) * STATE]
            if p % ppg == 0:
                s_s[...] = _dot_nt(cg, bg)
            heads = (PAIR * p, PAIR * p + 1)
            ccol = [_head_column(cum, lane, h) for h in heads]
            dcol = [_head_column(dt, lane, h) for h in heads]
            cl = [jnp.sum(jnp.where(lane1 == h, clast, 0.0), axis=-1, keepdims=True) for h in heads]
            cum_px = jnp.where(low, ccol[0], ccol[1])
            dt_px = jnp.where(low, dcol[0], dcol[1])
            cl_px = jnp.where(lane1 < HEAD_DIM, cl[0], cl[1])
            xs_p = xs_s[:, col]
            xdt = xs_p * dt_px
            y = dsk_ref[:, col] * xs_p
            for hi, h in enumerate(heads):
                dec = jnp.exp(jnp.where(tril, ccol[hi] - cumt_s[pl.ds(h, 1), :], -jnp.inf))
                wm = (s_s[...] * dec).astype(BF16)
                xm = jnp.where(low if hi == 0 else jnp.logical_not(low), xdt, 0.0).astype(BF16)
                y = y + _dot(wm, xm)
            hp = h_s[p]
            hs_ref[0, p] = hp
            y = y + _dot(cg, hp.astype(BF16)) * jnp.exp(cum_px)
            st = _dot_tn(bg, (xdt * jnp.exp(cl_px - cum_px)).astype(BF16))
            h_s[p] = jnp.exp(cl_px) * hp + st
            ys_ref[:, col] = y.astype(BF16)
            zp = z_ref[:, col].astype(F32)
            gated_s[:, col] = y * zp * _sigmoid(zp)

        for g in range(GROUPS):
            col = slice(g * gw, (g + 1) * gw)
            gg = gated_s[:, col]
            r = lax.rsqrt(jnp.mean(gg * gg, axis=-1, keepdims=True) + EPS)
            yb_ref[:, col] = (gg * r * nw_ref[:, col]).astype(BF16)

    full = lambda shape: pl.BlockSpec(shape, lambda c: tuple(0 for _ in shape))
    outs = pl.pallas_call(
        body, name=name, grid=(nc,),
        in_specs=[pl.BlockSpec((ch, si), lambda c: (c, zblk)),
                  pl.BlockSpec((ch, si), lambda c: (c, xsblk)),
                  pl.BlockSpec((ch, 2 * gn), lambda c: (c, bcblk)),
                  pl.BlockSpec((ch, LANES), lambda c: (c, 0)),
                  pl.BlockSpec(memory_space=pl.ANY),
                  full((SSD_K, si)), full((SSD_K, 2 * gn)), full((1, si)), full((1, 2 * gn)),
                  full((1, LANES)), full((1, LANES)), full((1, si)), full((1, si))],
        out_specs=[pl.BlockSpec((ch, si), lambda c: (c, cw // si)),
                   pl.BlockSpec((ch, si), lambda c: (c, 0)),
                   pl.BlockSpec((1, npair, STATE, LANES), lambda c: (c, 0, 0, 0))],
        out_shape=[jax.ShapeDtypeStruct(y_mix.shape, BF16), jax.ShapeDtypeStruct((t, si), BF16),
                   jax.ShapeDtypeStruct((nc, npair, STATE, LANES), F32)],
        input_output_aliases={4: 0},
        scratch_shapes=[pltpu.VMEM((ch + 8, si), F32), pltpu.VMEM((ch + 8, 2 * gn), F32),
                        pltpu.VMEM((ch, si), F32), pltpu.VMEM((ch, 2 * gn), BF16),
                        pltpu.VMEM((npair, STATE, LANES), F32), pltpu.VMEM((ch, si), F32),
                        pltpu.VMEM((ch, ch), F32), pltpu.VMEM((ch, LANES), F32), pltpu.VMEM((LANES, ch), F32)],
        compiler_params=_params(("arbitrary",)),
    )(proj, proj, proj, dt_raw, y_mix, kw_xs, kw_bc, b_xs, b_bc, dtb, aneg, dskip, normw)
    return outs


def ssd_bwd(proj, dt_raw, ys, hsave, dy, kw_xs, kw_bc, b_xs, b_bc, dtb, aneg, dskip, normw, cw, si, name):
    t = proj.shape[0]
    ch = min(SCAN_CHUNK, t)
    nc = t // ch
    npair = si // LANES
    ppg = npair // GROUPS
    gn = GROUPS * STATE
    gw = si // GROUPS
    zblk = 3 * cw // si
    xsblk = zblk + 1
    bcblk = (3 * cw + 2 * si) // (2 * gn)
    hb = ch // HALO

    def body(z_ref, xs_ref, bc_ref, xsp_ref, bcp_ref, dtr_ref, ys_ref, hs_ref, dyb_ref,
             kwx_ref, kwb_ref, bx_ref, bb_ref, dtb_ref, aneg_ref, dsk_ref, nw_ref,
             dz_ref, dxs_ref, dbc_ref, ddt_ref, dkwx_ref, dkwb_ref, dbx_ref, dbb_ref, ddtb_ref, da_ref, ddsk_ref,
             dnw_ref,
             extx, extb, extdx, extdb, xs_s, bc_s, dsx_s, dsb_s, dy_s, dxs_s, dbc_s, dh_s, s_s, ds_s,
             cum_s, cumt_s, dccol_s, dcrow_s, ddtcol_s, dcl_s):
        i = pl.program_id(0)
        rc = nc - 1 - i

        @pl.when(i == 0)
        def _():
            dh_s[...] = jnp.zeros_like(dh_s)
            extdx[ch:ch + 8, :] = jnp.zeros((8, si), F32)
            extdb[ch:ch + 8, :] = jnp.zeros((8, 2 * gn), F32)
            for r in (dkwx_ref, dkwb_ref, dbx_ref, dbb_ref, ddtb_ref, da_ref, ddsk_ref, dnw_ref):
                r[...] = jnp.zeros_like(r)

        extx[0:8, :] = jnp.where(rc > 0, xsp_ref[pl.ds(HALO - 8, 8), :].astype(F32), 0.0)
        extb[0:8, :] = jnp.where(rc > 0, bcp_ref[pl.ds(HALO - 8, 8), :].astype(F32), 0.0)
        extx[8:8 + ch, :] = xs_ref[...].astype(F32)
        extb[8:8 + ch, :] = bc_ref[...].astype(F32)
        xc = _causal_conv(extx, kwx_ref, bx_ref, ch, SSD_K)
        sg = _sigmoid(xc)
        xs_s[...] = xc * sg
        dsx_s[...] = sg * (1.0 + xc * (1.0 - sg))
        bcc = _causal_conv(extb, kwb_ref, bb_ref, ch, SSD_K)
        sgb = _sigmoid(bcc)
        bc_s[...] = (bcc * sgb).astype(BF16)
        dsb_s[...] = sgb * (1.0 + bcc * (1.0 - sgb))

        dt, cum, tril = _ssd_common(dtr_ref, dtb_ref, aneg_ref, cum_s, cumt_s, ch)
        lane = lax.broadcasted_iota(jnp.int32, (ch, LANES), 1)
        lane1 = lax.broadcasted_iota(jnp.int32, (1, LANES), 1)
        low = lane < HEAD_DIM
        low1 = lane1 < HEAD_DIM
        clast = cum_s[pl.ds(ch - 1, 1), :]

        for g in range(GROUPS):
            col = slice(g * gw, (g + 1) * gw)
            ysf = ys_ref[:, col].astype(F32)
            zf = z_ref[:, col].astype(F32)
            sz = _sigmoid(zf)
            silz = zf * sz
            gg = ysf * silz
            r = lax.rsqrt(jnp.mean(gg * gg, axis=-1, keepdims=True) + EPS)
            nrm = gg * r
            dyb = dyb_ref[:, col].astype(F32)
            dnw_ref[:, col] += jnp.sum(dyb * nrm, axis=0, keepdims=True)
            dn = dyb * nw_ref[:, col]
            dgg = r * (dn - nrm * jnp.mean(dn * nrm, axis=-1, keepdims=True))
            dy_s[:, col] = dgg * silz
            dz_ref[:, col] = (dgg * ysf * (sz * (1.0 + zf * (1.0 - sz)))).astype(BF16)

        dccol_s[...] = jnp.zeros_like(dccol_s)
        dcrow_s[...] = jnp.zeros_like(dcrow_s)
        ddtcol_s[...] = jnp.zeros_like(ddtcol_s)
        dcl_s[...] = jnp.zeros_like(dcl_s)
        dbc_s[...] = jnp.zeros_like(dbc_s)

        for p in range(npair):
            g = p // ppg
            col = slice(p * LANES, (p + 1) * LANES)
            bcol = slice(g * STATE, (g + 1) * STATE)
            ccolg = slice(gn + g * STATE, gn + (g + 1) * STATE)
            bg = bc_s[:, bcol]
            cg = bc_s[:, ccolg]
            if p % ppg == 0:
                s_s[...] = _dot_nt(cg, bg)
                ds_s[...] = jnp.zeros_like(ds_s)
            heads = (PAIR * p, PAIR * p + 1)
            masks = (low, jnp.logical_not(low))
            masks1 = (low1, jnp.logical_not(low1))
            ccol = [_head_column(cum, lane, h) for h in heads]
            dcol = [_head_column(dt, lane, h) for h in heads]
            cl = [jnp.sum(jnp.where(lane1 == h, clast, 0.0), axis=-1, keepdims=True) for h in heads]
            cum_px = jnp.where(low, ccol[0], ccol[1])
            dt_px = jnp.where(low, dcol[0], dcol[1])
            cl_px = jnp.where(low1, cl[0], cl[1])
            e_px = jnp.exp(cum_px)
            dec_end = jnp.exp(cl_px - cum_px)
            gdec = jnp.exp(cl_px)
            xs_p = xs_s[:, col]
            xdt = xs_p * dt_px
            dyp = dy_s[:, col]
            hc = hs_ref[0, p]
            hcb = hc.astype(BF16)
            dhn = dh_s[p]
            dhnb = dhn.astype(BF16)

            ddsk_ref[:, col] += jnp.sum(dyp * xs_p, axis=0, keepdims=True)
            dxs_acc = dsk_ref[:, col] * dyp
            dye = dyp * e_px
            dyeb = dye.astype(BF16)
            dbc_s[:, ccolg] += _dot_nt(dyeb, hcb)
            dcum_lane = dye * _dot(cg, hcb)
            dh_from_y = _dot_tn(cg, dyeb)
            xd = xdt * dec_end
            dxd = _dot(bg, dhnb)
            dbc_s[:, bcol] += _dot_nt(xd.astype(BF16), dhnb)
            dxdt = dxd * dec_end
            t1 = dxd * xd
            dcum_lane = dcum_lane - t1
            dcl_lane = jnp.sum(t1, axis=0, keepdims=True) + jnp.sum(dhn * hc, axis=0, keepdims=True) * gdec
            dh_s[p] = gdec * dhn + dh_from_y
            xdtb = xdt.astype(BF16)
            for hi, h in enumerate(heads):
                dym = jnp.where(masks[hi], dyp, 0.0).astype(BF16)
                dw = _dot_nt(dym, xdtb)
                dec = jnp.exp(jnp.where(tril, ccol[hi] - cumt_s[pl.ds(h, 1), :], -jnp.inf))
                wm = s_s[...] * dec
                dxdt = dxdt + _dot_tn(wm.astype(BF16), dym)
                ds_s[...] += dw * dec
                gm = dw * wm
                rowsum = jnp.sum(gm, axis=-1, keepdims=True)
                lanesum = jnp.sum(jnp.where(masks[hi], dcum_lane, 0.0), axis=-1, keepdims=True)
                dccol_s[...] += jnp.where(lane == h, rowsum + lanesum, 0.0)
                dcrow_s[pl.ds(h, 1), :] = jnp.sum(gm, axis=0, keepdims=True)
                dcl_h = jnp.sum(jnp.where(masks1[hi], dcl_lane, 0.0), axis=-1, keepdims=True)
                dcl_s[...] += jnp.where(lane1 == h, dcl_h, 0.0)
            ddt_lane = dxdt * xs_p
            for hi, h in enumerate(heads):
                s = jnp.sum(jnp.where(masks[hi], ddt_lane, 0.0), axis=-1, keepdims=True)
                ddtcol_s[...] += jnp.where(lane == h, s, 0.0)
            dxs_s[:, col] = dxs_acc + dxdt * dt_px
            if p % ppg == ppg - 1:
                dsb = ds_s[...].astype(BF16)
                dbc_s[:, ccolg] += _dot(dsb, bg)
                dbc_s[:, bcol] += _dot_tn(dsb, cg)

        rowi = lax.broadcasted_iota(jnp.int32, (ch, LANES), 0)
        dcum = dccol_s[...] - dcrow_s[...].T + jnp.where(rowi == ch - 1, dcl_s[...], 0.0)
        ri = lax.broadcasted_iota(jnp.int32, (ch, ch), 0)
        ci = lax.broadcasted_iota(jnp.int32, (ch, ch), 1)
        dal = _dot_exact((ri <= ci).astype(F32), dcum)
        ddt = dal * aneg_ref[...] + ddtcol_s[...]
        da_ref[...] += jnp.sum(dal * dt, axis=0, keepdims=True)
        ddtr = ddt * _sigmoid(dtr_ref[...] + dtb_ref[...])
        ddt_ref[...] = ddtr
        ddtb_ref[...] += jnp.sum(ddtr, axis=0, keepdims=True)

        for (dpost, dsl, extd, ext, kw_ref, dkw_ref, db_ref, out_ref) in (
                (dxs_s, dsx_s, extdx, extx, kwx_ref, dkwx_ref, dbx_ref, dxs_ref),
                (dbc_s, dsb_s, extdb, extb, kwb_ref, dkwb_ref, dbb_ref, dbc_ref)):
            dxc = dpost[...] * dsl[...]
            extd[0:ch, :] = dxc
            draw = kw_ref[pl.ds(SSD_K - 1, 1), :] * dxc
            for k in range(SSD_K - 1):
                draw = draw + kw_ref[pl.ds(k, 1), :] * extd[pl.ds(SSD_K - 1 - k, ch), :]
            out_ref[...] = draw.astype(BF16)
            db_ref[...] += jnp.sum(dxc, axis=0, keepdims=True)
            for k in range(SSD_K):
                sh = ext[pl.ds(8 - (SSD_K - 1) + k, ch), :]
                dkw_ref[pl.ds(k, 1), :] += jnp.sum(dxc * sh, axis=0, keepdims=True)
            extd[ch:ch + 8, :] = extd[0:8, :]

    full = lambda shape: pl.BlockSpec(shape, lambda i: tuple(0 for _ in shape))
    rev = lambda blk: (lambda i: (nc - 1 - i, blk))
    prev = lambda blk: (lambda i: (jnp.maximum((nc - 1 - i) * hb - 1, 0), blk))
    small = [(SSD_K, si), (SSD_K, 2 * gn), (1, si), (1, 2 * gn), (1, LANES), (1, LANES), (1, si), (1, si)]
    return pl.pallas_call(
        body, name=name, grid=(nc,),
        in_specs=[pl.BlockSpec((ch, si), rev(zblk)), pl.BlockSpec((ch, si), rev(xsblk)),
                  pl.BlockSpec((ch, 2 * gn), rev(bcblk)),
                  pl.BlockSpec((HALO, si), prev(xsblk)), pl.BlockSpec((HALO, 2 * gn), prev(bcblk)),
                  pl.BlockSpec((ch, LANES), rev(0)), pl.BlockSpec((ch, si), rev(0)),
                  pl.BlockSpec((1, npair, STATE, LANES), lambda i: (nc - 1 - i, 0, 0, 0)),
                  pl.BlockSpec((ch, si), rev(cw // si))] + [full(s) for s in small],
        out_specs=[pl.BlockSpec((ch, si), rev(0)), pl.BlockSpec((ch, si), rev(0)),
                   pl.BlockSpec((ch, 2 * gn), rev(0)), pl.BlockSpec((ch, LANES), rev(0))]
                  + [full(s) for s in small],
        out_shape=[jax.ShapeDtypeStruct((t, si), BF16), jax.ShapeDtypeStruct((t, si), BF16),
                   jax.ShapeDtypeStruct((t, 2 * gn), BF16), jax.ShapeDtypeStruct((t, LANES), F32)]
                  + [jax.ShapeDtypeStruct(s, F32) for s in small],
        scratch_shapes=[pltpu.VMEM((ch + 8, si), F32), pltpu.VMEM((ch + 8, 2 * gn), F32),
                        pltpu.VMEM((ch + 8, si), F32), pltpu.VMEM((ch + 8, 2 * gn), F32),
                        pltpu.VMEM((ch, si), F32), pltpu.VMEM((ch, 2 * gn), BF16),
                        pltpu.VMEM((ch, si), F32), pltpu.VMEM((ch, 2 * gn), F32),
                        pltpu.VMEM((ch, si), F32), pltpu.VMEM((ch, si), F32), pltpu.VMEM((ch, 2 * gn), F32),
                        pltpu.VMEM((npair, STATE, LANES), F32),
                        pltpu.VMEM((ch, ch), F32), pltpu.VMEM((ch, ch), F32),
                        pltpu.VMEM((ch, LANES), F32), pltpu.VMEM((LANES, ch), F32),
                        pltpu.VMEM((ch, LANES), F32), pltpu.VMEM((LANES, ch), F32),
                        pltpu.VMEM((ch, LANES), F32), pltpu.VMEM((1, LANES), F32)],
        compiler_params=_params(("arbitrary",)),
    )(proj, proj, proj, proj, proj, dt_raw, ys, hsave, dy,
      kw_xs, kw_bc, b_xs, b_bc, dtb, aneg, dskip, normw)


def final_loss(x, nw, tgt, name):
    t, d = x.shape
    tm = _tile(t, 512, 8)

    def body(x_ref, nw_ref, t_ref, dx_ref, dnw_ref, ls_ref):
        xf = x_ref[...]
        r = lax.rsqrt(jnp.mean(xf * xf, axis=-1, keepdims=True) + EPS)
        nx = xf * r
        e = nx * nw_ref[...] - t_ref[...]
        dyv = e * (1.0 / d)
        dn = dyv * nw_ref[...]
        dx_ref[...] = r * (dn - nx * jnp.mean(dn * nx, axis=-1, keepdims=True))

        @pl.when(pl.program_id(0) == 0)
        def _():
            dnw_ref[...] = jnp.zeros_like(dnw_ref)
            ls_ref[...] = jnp.zeros_like(ls_ref)

        dnw_ref[...] += jnp.sum(dyv * nx, axis=0, keepdims=True)
        ls_ref[...] += jnp.sum(e * e, axis=0, keepdims=True) * (0.5 / d)

    return pl.pallas_call(
        body, name=name, grid=(t // tm,),
        in_specs=[pl.BlockSpec((tm, d), lambda i: (i, 0)), pl.BlockSpec((1, d), lambda i: (0, 0)),
                  pl.BlockSpec((tm, d), lambda i: (i, 0))],
        out_specs=[pl.BlockSpec((tm, d), lambda i: (i, 0)), pl.BlockSpec((1, d), lambda i: (0, 0)),
                   pl.BlockSpec((1, d), lambda i: (0, 0))],
        out_shape=[jax.ShapeDtypeStruct((t, d), F32), jax.ShapeDtypeStruct((1, d), F32),
                   jax.ShapeDtypeStruct((1, d), F32)],
        compiler_params=_params(("arbitrary",)),
    )(x, nw, tgt)


def _rows3(a):
    if a.ndim == 1:
        return a.reshape(1, 1, a.shape[0])
    if a.ndim == 2:
        return a.reshape(1, *a.shape)
    return a.reshape(-1, a.shape[-2], a.shape[-1])


def adamw(w, g, m, v, name):
    shape = w.shape
    views = [_rows3(a) for a in (w, g, m, v)]
    b, r, c = views[0].shape
    tr = _tile(r, 256, 16) if r % 16 == 0 else r

    def body(w_ref, g_ref, m_ref, v_ref, d_ref, nm_ref, nv_ref):
        g = g_ref[...]
        m = ADAM_B1 * m_ref[...] + (1.0 - ADAM_B1) * g
        v = ADAM_B2 * v_ref[...] + (1.0 - ADAM_B2) * (g * g)
        m_hat = m / (1.0 - ADAM_B1 ** ADAM_STEP)
        v_hat = v / (1.0 - ADAM_B2 ** ADAM_STEP)
        d_ref[...] = -ADAM_LR * (m_hat / (jnp.sqrt(v_hat) + ADAM_EPS) + ADAM_WD * w_ref[...])
        nm_ref[...] = m
        nv_ref[...] = v

    spec = pl.BlockSpec((1, tr, c), lambda i, j: (i, j, 0))
    outs = pl.pallas_call(
        body, name=name, grid=(b, r // tr), in_specs=[spec] * 4, out_specs=[spec] * 3,
        out_shape=[jax.ShapeDtypeStruct((b, r, c), F32)] * 3,
        compiler_params=_params(("parallel", "parallel")),
    )(*views)
    return [o.reshape(shape) for o in outs]


def _coords():
    return lax.axis_index("x"), lax.axis_index("y"), lax.axis_index("c")


def _ici_peers(x, y):
    chips = [(1 - x, y), (x, 1 - y), (1 - x, 1 - y)]
    return chips, [2 * cx + cy for cx, cy in chips]


def _any_specs(n):
    return [pl.BlockSpec(memory_space=pl.ANY)] * n


def _place(ref, how, chip, layers, per):
    if how == "lead":
        return ref.at[chip, layers]
    start = pl.multiple_of(chip * per, per)
    if how == "rows":
        return ref.at[layers, pl.ds(start, per), :]
    return ref.at[layers, :, pl.ds(start, per)]


def gather_weights(shards, hows, name):
    na = len(shards)
    out_shape = []
    for s, how in zip(shards, hows):
        assert s.shape[0] % 2 == 0
        if how == "lead":
            shp = (N_CHIPS, *s.shape)
        elif how == "rows":
            shp = (s.shape[0], N_CHIPS * s.shape[1], s.shape[2])
        else:
            shp = (s.shape[0], s.shape[1], N_CHIPS * s.shape[2])
        out_shape.append(jax.ShapeDtypeStruct(shp, s.dtype))

    def body(*refs):
        ins = refs[:na]
        outs = refs[na:2 * na]
        send_sems, recv_sems = refs[2 * na:]
        x, y, c = _coords()
        me = 2 * x + y
        chips, chip_ids = _ici_peers(x, y)
        sibling = (x, y, 1 - c)

        def dst(a, chip, layers):
            per = {"lead": 0, "rows": ins[a].shape[1], "cols": ins[a].shape[-1]}[hows[a]]
            return _place(outs[a], hows[a], chip, layers, per)

        def copy(a, k, src, dst_ref, to):
            return pltpu.make_async_remote_copy(
                src_ref=src, dst_ref=dst_ref, send_sem=send_sems.at[7 * a + k], recv_sem=recv_sems.at[7 * a + k],
                device_id=to, device_id_type=MESH)

        started = []
        halves = []
        for a in range(na):
            nl = ins[a].shape[0]
            hl = nl // 2
            mine = pl.ds(c * hl, hl)
            theirs = pl.ds((1 - c) * hl, hl)
            halves.append((mine, theirs))
            for k in range(3):
                cp = copy(a, k, ins[a].at[mine], dst(a, me, mine), (*chips[k], c))
                cp.start()
                started.append(cp)
            own = copy(a, 6, ins[a], dst(a, me, pl.ds(0, nl)), sibling)
            own.start()
            started.append(own)
        for a in range(na):
            mine, _ = halves[a]
            for k in range(3):
                landed = dst(a, chip_ids[k], mine)
                copy(a, k, landed, landed, (*chips[k], c)).wait_recv()
                fw = copy(a, 3 + k, landed, landed, sibling)
                fw.start()
                started.append(fw)
        for a in range(na):
            _, theirs = halves[a]
            for k in range(3):
                got = dst(a, chip_ids[k], theirs)
                copy(a, 3 + k, got, got, sibling).wait_recv()
            whole = dst(a, me, pl.ds(0, ins[a].shape[0]))
            copy(a, 6, whole, whole, sibling).wait_recv()
        for cp in started:
            cp.wait_send()

    return pl.pallas_call(
        body, name=name, in_specs=_any_specs(na), out_specs=_any_specs(na), out_shape=out_shape,
        scratch_shapes=[pltpu.SemaphoreType.DMA((7 * na,)), pltpu.SemaphoreType.DMA((7 * na,))],
        compiler_params=pltpu.CompilerParams(has_side_effects=True),
    )(*shards)


def grads_to_sibling(arrs, name):
    na = len(arrs)

    def body(*refs):
        ins = refs[:na]
        outs = refs[na:2 * na]
        send_sems, recv_sems = refs[2 * na:]
        x, y, c = _coords()
        cps = []
        for a in range(na):
            r2 = ins[a].shape[1] // 2
            src = ins[a].at[:, pl.ds(pl.multiple_of((1 - c) * r2, 8), r2), :]
            cp = pltpu.make_async_remote_copy(
                src_ref=src, dst_ref=outs[a], send_sem=send_sems.at[a], recv_sem=recv_sems.at[a],
                device_id=(x, y, 1 - c), device_id_type=MESH)
            cp.start()
            cps.append(cp)
        for cp in cps:
            cp.wait()

    return pl.pallas_call(
        body, name=name, in_specs=_any_specs(na), out_specs=_any_specs(na),
        out_shape=[jax.ShapeDtypeStruct((a.shape[0], a.shape[1] // 2, a.shape[2]), a.dtype) for a in arrs],
        scratch_shapes=[pltpu.SemaphoreType.DMA((na,)), pltpu.SemaphoreType.DMA((na,))],
        compiler_params=pltpu.CompilerParams(has_side_effects=True),
    )(*arrs)


def chip_sum(g, recv, core, name):
    nch, r, c = g.shape
    r2 = r // 2
    tr = _tile(r2, 256, 16)
    nb = r2 // tr

    def body(core_ref, g_ref, r_ref, o32_ref, o16_ref):
        del core_ref
        s = g_ref[...] + r_ref[...]
        o32_ref[...] = s
        o16_ref[...] = s.astype(BF16)

    here = pl.BlockSpec((1, tr, c), lambda i, j, core_ref: (i, j, 0))
    return pl.pallas_call(
        body, name=name,
        grid_spec=pltpu.PrefetchScalarGridSpec(
            num_scalar_prefetch=1, grid=(nch, nb),
            in_specs=[pl.BlockSpec((1, tr, c), lambda i, j, core_ref: (i, core_ref[0] * nb + j, 0)), here],
            out_specs=[here, here]),
        out_shape=[jax.ShapeDtypeStruct((nch, r2, c), F32), jax.ShapeDtypeStruct((nch, r2, c), BF16)],
        compiler_params=_params(("parallel", "parallel")),
    )(core, g, recv)


def grads_to_chips(arrs, name):
    na = len(arrs)

    def body(*refs):
        ins = refs[:na]
        outs = refs[na:2 * na]
        send_sems, recv_sems = refs[2 * na:]
        x, y, c = _coords()
        chips, chip_ids = _ici_peers(x, y)
        cps = []
        for a in range(na):
            for k in range(3):
                cp = pltpu.make_async_remote_copy(
                    src_ref=ins[a].at[chip_ids[k]], dst_ref=outs[a].at[k],
                    send_sem=send_sems.at[3 * a + k], recv_sem=recv_sems.at[3 * a + k],
                    device_id=(*chips[k], c), device_id_type=MESH)
                cp.start()
                cps.append(cp)
        for cp in cps:
            cp.wait()

    return pl.pallas_call(
        body, name=name, in_specs=_any_specs(na), out_specs=_any_specs(na),
        out_shape=[jax.ShapeDtypeStruct((3, *a.shape[1:]), a.dtype) for a in arrs],
        scratch_shapes=[pltpu.SemaphoreType.DMA((3 * na,)), pltpu.SemaphoreType.DMA((3 * na,))],
        compiler_params=pltpu.CompilerParams(has_side_effects=True),
    )(*arrs)


def grad_sum(p32, recv, where, layer, nl, buf, name):
    _, r2, c = p32.shape
    tr = _tile(r2, 256, 16)
    nb = r2 // tr

    def body(where_ref, p_ref, r0_ref, r1_ref, r2_ref, *rest):
        o_ref = rest[-1]
        o_ref[...] = (p_ref[...] + r0_ref[...].astype(F32) + r1_ref[...].astype(F32) + r2_ref[...].astype(F32))

    slot = lambda k: pl.BlockSpec((1, tr, c), lambda j, wr: (k, j, 0))
    in_specs = [pl.BlockSpec((1, tr, c), lambda j, wr: (wr[0], j, 0)), slot(0), slot(1), slot(2)]
    args = [where, p32, recv, recv, recv]
    aliases = {}
    if buf is not None:
        in_specs.append(pl.BlockSpec(memory_space=pl.ANY))
        args.append(buf)
        aliases = {5: 0}
    return pl.pallas_call(
        body, name=name,
        grid_spec=pltpu.PrefetchScalarGridSpec(
            num_scalar_prefetch=1, grid=(nb,), in_specs=in_specs,
            out_specs=pl.BlockSpec((1, tr, c), lambda j, wr: (layer, wr[1] * nb + j, 0))),
        out_shape=jax.ShapeDtypeStruct((nl, 2 * r2, c), F32),
        input_output_aliases=aliases,
        compiler_params=_params(("parallel",)),
    )(*args)


def grads_exchange(bufs, name):
    na = len(bufs)

    def body(*refs):
        ins = refs[:na]
        outs = refs[na:2 * na]
        send_sems, recv_sems = refs[2 * na:]
        x, y, c = _coords()
        cps = []
        for a in range(na):
            r2 = ins[a].shape[1] // 2
            rows = pl.ds(pl.multiple_of(c * r2, 8), r2)
            cp = pltpu.make_async_remote_copy(
                src_ref=ins[a].at[:, rows, :], dst_ref=outs[a].at[:, rows, :],
                send_sem=send_sems.at[a], recv_sem=recv_sems.at[a],
                device_id=(x, y, 1 - c), device_id_type=MESH)
            cp.start()
            cps.append(cp)
        for a, cp in enumerate(cps):
            cp.wait_send()
            r2 = ins[a].shape[1] // 2
            got = outs[a].at[:, pl.ds(pl.multiple_of((1 - c) * r2, 8), r2), :]
            pltpu.make_async_remote_copy(
                src_ref=got, dst_ref=got, send_sem=send_sems.at[a], recv_sem=recv_sems.at[a],
                device_id=(x, y, 1 - c), device_id_type=MESH).wait_recv()

    return pl.pallas_call(
        body, name=name, in_specs=_any_specs(na), out_specs=_any_specs(na),
        out_shape=[jax.ShapeDtypeStruct(a.shape, a.dtype) for a in bufs],
        input_output_aliases={a: a for a in range(na)},
        scratch_shapes=[pltpu.SemaphoreType.DMA((na,)), pltpu.SemaphoreType.DMA((na,))],
        compiler_params=pltpu.CompilerParams(has_side_effects=True),
    )(*bufs)


def allreduce_small(buf, name):
    r, cdim = buf.shape

    def body(x_ref, o_ref, gath, send_sems, recv_sems):
        x, y, c = _coords()
        me, sibling = (x, y, c), (x, y, 1 - c)
        chips, _ = _ici_peers(x, y)

        def slot(px, py, pc):
            return gath.at[4 * px + 2 * py + pc]

        def copy(k, block, to, src=None):
            return pltpu.make_async_remote_copy(
                src_ref=slot(*block) if src is None else src, dst_ref=slot(*block),
                send_sem=send_sems.at[k], recv_sem=recv_sems.at[k], device_id=to, device_id_type=MESH)

        gath[4 * x + 2 * y + c] = x_ref[...]
        first = [copy(0, me, sibling, src=x_ref)]
        first += [copy(1 + j, me, (*chip, c), src=x_ref) for j, chip in enumerate(chips)]
        for cp in first:
            cp.start()
        passed = [copy(4 + j, (*chip, c), sibling) for j, chip in enumerate(chips)]
        for j, chip in enumerate(chips):
            copy(1 + j, (*chip, c), me).wait_recv()
            passed[j].start()
        copy(0, sibling, me).wait_recv()
        for j, chip in enumerate(chips):
            copy(4 + j, (*chip, 1 - c), me).wait_recv()
        for cp in first + passed:
            cp.wait_send()
        acc = gath[0]
        for d in range(1, 8):
            acc = acc + gath[d]
        o_ref[...] = acc

    return pl.pallas_call(
        body, name=name,
        in_specs=[pl.BlockSpec(memory_space=pltpu.VMEM)], out_specs=pl.BlockSpec(memory_space=pltpu.VMEM),
        out_shape=jax.ShapeDtypeStruct((r, cdim), F32),
        scratch_shapes=[pltpu.VMEM((8, r, cdim), F32), pltpu.SemaphoreType.DMA((7,)), pltpu.SemaphoreType.DMA((7,))],
        compiler_params=pltpu.CompilerParams(has_side_effects=True),
    )(buf)


def _expand_heads(v):
    return jnp.repeat(v.astype(F32), HEAD_DIM).reshape(1, -1)


def _pad_lanes(v):
    return jnp.pad(v.astype(F32), (0, LANES - v.shape[0])).reshape(1, LANES)


def local_step(x, tgt, p, cols):
    nl = p["w_up"].shape[0]
    d = x.shape[1]
    cw = p["short_conv_w"].shape[2]
    si = p["ssd_norm_w"].shape[1]
    ff = p["w_up"].shape[2]
    nh = si // HEAD_DIM
    gn = GROUPS * STATE
    npad = p["w_in"].shape[2]
    dt_off = 3 * cw + si + si + 2 * gn
    assert cols == dt_off + nh and nh <= LANES and dt_off % LANES == 0 and npad == dt_off + LANES
    pieces = [(0, cw), (cw, cw), (2 * cw, cw), (3 * cw, si), (3 * cw + si, si), (3 * cw + 2 * si, 2 * gn),
              (dt_off, LANES)]

    saved = []
    for l in range(nl):
        nw1 = p["norm_mix_w"][l].reshape(1, d)
        nw2 = p["norm_mlp_w"][l].reshape(1, d)
        kw3 = p["short_conv_w"][l]
        kwx, kwb = p["ssd_conv_w"][l][:, :si], p["ssd_conv_w"][l][:, si:]
        bx, bb = p["ssd_conv_b"][l][:si].reshape(1, si), p["ssd_conv_b"][l][si:].reshape(1, 2 * gn)
        dtb = _pad_lanes(p["dt_bias"][l])
        aneg = _pad_lanes(-jnp.exp(p["a_log"][l]))
        dsk = _expand_heads(p["d_skip"][l])
        snw = p["ssd_norm_w"][l].reshape(1, si)
        ssd_args = (kwx, kwb, bx, bb, dtb, aneg, dsk, snw)

        proj, h = norm_matmul(x, nw1, p["w_in"], l, npad, 0, BF16, "in_proj")
        dt_raw = norm_matmul(x, nw1, p["w_in"], l, LANES, dt_off // LANES, F32, "dt_proj", emit_h=False)
        y_mix = conv_mixer_fwd(proj, kw3, cw, cw + si, "conv_mixer_fwd")
        y_mix, ys, hsave = ssd_fwd(proj, dt_raw, y_mix, *ssd_args, cw, si, "ssd_fwd")
        x2 = matmul(y_mix, p["w_out"], l, False, d, F32, "out_proj", residual=x)
        up, h2 = norm_matmul(x2, nw2, p["w_up"], l, ff, 0, BF16, "up_proj")
        x3 = matmul(up, p["w_down"], l, False, d, F32, "down_proj", lhs_fn=_relu2, residual=x2)
        saved.append((x, h, proj, dt_raw, y_mix, ys, hsave, x2, h2, up, nw1, nw2, kw3, ssd_args))
        x = x3

    dx, dwf, lvec = final_loss(x, p["final_norm_w"].reshape(1, d), tgt, "final_loss")
    loss = jnp.sum(lvec)

    names = ("norm_mix_w", "w_in", "short_conv_w", "ssd_conv_w", "ssd_conv_b", "dt_bias", "a_log", "d_skip",
             "ssd_norm_w", "w_out", "norm_mlp_w", "w_up", "w_down")
    grads = {k: [None] * nl for k in names}
    for l in reversed(range(nl)):
        x0, h, proj, dt_raw, y_mix, ys, hsave, x2, h2, up, nw1, nw2, kw3, ssd_args = saved[l]
        dup = matmul(dx, p["w_down"], l, True, ff, BF16, "down_bwd", relu_gate=up)
        grads["w_down"][l] = matmul_tn(up, dx, "down_wgrad", a_fn=_relu2)
        dx2, dnw2 = matmul_normbwd([dup], [(0, ff)], p["w_up"], l, x2, nw2, dx, "up_bwd")
        grads["w_up"][l] = matmul_tn(h2, dup, "up_wgrad", by_chip=True)
        grads["norm_mlp_w"][l] = dnw2.reshape(d)
        dy = matmul(dx2, p["w_out"], l, True, cw + si, BF16, "out_bwd")
        grads["w_out"][l] = matmul_tn(y_mix, dx2, "out_wgrad")
        dub, duc, duh, dkw3 = conv_mixer_bwd(proj, dy, kw3, cw, "conv_mixer_bwd")
        (dz, dxs, dbc, ddt, dkwx, dkwb, dbx, dbb, ddtb, da, ddsk, dsnw) = ssd_bwd(
            proj, dt_raw, ys, hsave, dy, *ssd_args, cw, si, "ssd_bwd")
        dpieces = [dub, duc, duh, dz, dxs, dbc, ddt]
        dxl, dnw1 = matmul_normbwd(dpieces, pieces, p["w_in"], l, x0, nw1, dx2, "in_bwd")
        grads["w_in"][l] = split_to_chips(
            [matmul_tn(h, dp, "in_wgrad_%d" % i) for i, dp in enumerate(dpieces)], cols, "in_wgrad_split")
        grads["norm_mix_w"][l] = dnw1.reshape(d)
        grads["short_conv_w"][l] = dkw3
        grads["ssd_conv_w"][l] = jnp.concatenate([dkwx, dkwb], axis=1)
        grads["ssd_conv_b"][l] = jnp.concatenate([dbx, dbb], axis=1).reshape(-1)
        grads["dt_bias"][l] = ddtb[0, :nh]
        grads["a_log"][l] = da[0, :nh] * ssd_args[5][0, :nh]
        grads["d_skip"][l] = jnp.sum(ddsk.reshape(nh, HEAD_DIM), axis=1)
        grads["ssd_norm_w"][l] = dsnw.reshape(si)
        dx = dxl

    grads = {k: (v if k in BIG else jnp.stack(v)) for k, v in grads.items()}
    grads["final_norm_w"] = dwf.reshape(d)
    return loss, dx, grads


BIG = ("w_in", "w_out", "w_up", "w_down")
GATHER_HOW = {"w_in": "lead", "w_out": "rows", "w_up": "cols", "w_down": "rows",
              "short_conv_w": "lead", "ssd_conv_w": "lead"}
SMALL_SHARDED = ("short_conv_w", "ssd_conv_w")
SMALL_REPL = ("norm_mix_w", "ssd_conv_b", "dt_bias", "a_log", "d_skip", "ssd_norm_w", "norm_mlp_w", "final_norm_w")
WEIGHTS = ("norm_mix_w", "w_in", "short_conv_w", "ssd_conv_w", "ssd_conv_b", "dt_bias", "a_log", "d_skip",
           "ssd_norm_w", "w_out", "norm_mlp_w", "w_up", "w_down", "final_norm_w")
SMALL_COLS = 1024


def _pack_small(named):
    flat = jnp.concatenate([v.reshape(-1).astype(F32) for v in named])
    n = flat.shape[0]
    rows = -(-n // SMALL_COLS)
    rows = -(-rows // 8) * 8
    return jnp.pad(flat, (0, rows * SMALL_COLS - n)).reshape(rows, SMALL_COLS)


def _unpack_small(buf, like):
    flat = buf.reshape(-1)
    out, off = [], 0
    for v in like:
        out.append(flat[off:off + v.size].reshape(v.shape))
        off += v.size
    return out


def kernel(x, norm_mix_w, w_in, short_conv_w, ssd_conv_w, ssd_conv_b, dt_bias, a_log, d_skip, ssd_norm_w, w_out, norm_mlp_w, w_up, w_down, final_norm_w, loss_target, m_norm_mix_w, m_w_in, m_short_conv_w, m_ssd_conv_w, m_ssd_conv_b, m_dt_bias, m_a_log, m_d_skip, m_ssd_norm_w, m_w_out, m_norm_mlp_w, m_w_up, m_w_down, m_final_norm_w, v_norm_mix_w, v_w_in, v_short_conv_w, v_ssd_conv_w, v_ssd_conv_b, v_dt_bias, v_a_log, v_d_skip, v_ssd_norm_w, v_w_out, v_norm_mlp_w, v_w_up, v_w_down, v_final_norm_w):
    w = dict(norm_mix_w=norm_mix_w, w_in=w_in, short_conv_w=short_conv_w, ssd_conv_w=ssd_conv_w,
             ssd_conv_b=ssd_conv_b, dt_bias=dt_bias, a_log=a_log, d_skip=d_skip, ssd_norm_w=ssd_norm_w, w_out=w_out,
             norm_mlp_w=norm_mlp_w, w_up=w_up, w_down=w_down, final_norm_w=final_norm_w)
    m = dict(norm_mix_w=m_norm_mix_w, w_in=m_w_in, short_conv_w=m_short_conv_w, ssd_conv_w=m_ssd_conv_w,
             ssd_conv_b=m_ssd_conv_b, dt_bias=m_dt_bias, a_log=m_a_log, d_skip=m_d_skip, ssd_norm_w=m_ssd_norm_w,
             w_out=m_w_out, norm_mlp_w=m_norm_mlp_w, w_up=m_w_up, w_down=m_w_down, final_norm_w=m_final_norm_w)
    v = dict(norm_mix_w=v_norm_mix_w, w_in=v_w_in, short_conv_w=v_short_conv_w, ssd_conv_w=v_ssd_conv_w,
             ssd_conv_b=v_ssd_conv_b, dt_bias=v_dt_bias, a_log=v_a_log, d_skip=v_d_skip, ssd_norm_w=v_ssd_norm_w,
             w_out=v_w_out, norm_mlp_w=v_norm_mlp_w, w_up=v_w_up, w_down=v_w_down, final_norm_w=v_final_norm_w)
    xi, yi, ci = lax.axis_index("x"), lax.axis_index("y"), lax.axis_index("c")
    chip = 2 * xi + yi
    nl = w_up.shape[0]
    cols = N_CHIPS * w_in.shape[2]
    npad = cols // LANES * LANES + LANES

    gathered_names = BIG + SMALL_SHARDED
    shards = [w[k].astype(BF16) for k in BIG] + [w[k] for k in SMALL_SHARDED]
    gathered = dict(zip(gathered_names, gather_weights(shards, [GATHER_HOW[k] for k in gathered_names],
                                                       "gather_weights")))
    full = dict(w)
    full["w_in"] = join_from_chips(gathered["w_in"], npad, "w_in_join")
    for k in ("w_out", "w_up", "w_down"):
        full[k] = gathered[k]
    for k in SMALL_SHARDED:
        full[k] = jnp.concatenate([gathered[k][j] for j in range(N_CHIPS)], axis=2)

    loss, grad_x, grads = local_step(x[0], loss_target[0], full, cols)
    loss = lax.psum(loss, ("x", "y", "c"))

    by_chip = {k: [g if g.ndim == 3 else g.reshape(N_CHIPS, g.shape[0] // N_CHIPS, g.shape[1]) for g in grads[k]]
               for k in BIG}
    flat = [by_chip[k][l] for k in BIG for l in range(nl)]
    from_sibling = grads_to_sibling(flat, "grads_to_sibling")
    core = ci.astype(jnp.int32).reshape(1)
    sums = [chip_sum(g, r, core, "chip_sum") for g, r in zip(flat, from_sibling)]
    from_chips = grads_to_chips([s[1] for s in sums], "grads_to_chips")
    where = jnp.stack([chip, ci]).astype(jnp.int32)
    bufs = []
    for ki, k in enumerate(BIG):
        buf = None
        for l in range(nl):
            i = ki * nl + l
            buf = grad_sum(sums[i][0], from_chips[i], where, l, nl, buf, "grad_sum")
        bufs.append(buf)
    g_shard = dict(zip(BIG, grads_exchange(bufs, "grads_exchange")))

    small_names = SMALL_REPL + SMALL_SHARDED
    small_sum = allreduce_small(_pack_small([grads[k] for k in small_names]), "allreduce_small")
    for k, g in zip(small_names, _unpack_small(small_sum, [grads[k] for k in small_names])):
        if k in SMALL_SHARDED:
            width = w[k].shape[2]
            g = lax.dynamic_slice_in_dim(g, chip * width, width, axis=2)
        g_shard[k] = g

    delta, new_m, new_v = {}, {}, {}
    for k in BIG:
        delta[k], new_m[k], new_v[k] = adamw(w[k], g_shard[k], m[k], v[k], "adamw_%s" % k)
    packed = [_pack_small([d_[k] for k in small_names]) for d_ in (w, g_shard, m, v)]
    outs = adamw(*packed, "adamw_small")
    for d_, buf in zip((delta, new_m, new_v), outs):
        for k, val in zip(small_names, _unpack_small(buf, [w[k] for k in small_names])):
            d_[k] = val

    return (loss, grad_x[None], *[g_shard[k] for k in WEIGHTS], *[delta[k] for k in WEIGHTS],
            *[new_m[k] for k in WEIGHTS], *[new_v[k] for k in WEIGHTS])
```

```python
import functools

import jax
import jax.numpy as jnp
from jax import lax
from jax.experimental import pallas as pl
from jax.experimental.pallas import tpu as pltpu

F32 = jnp.float32
BF16 = jnp.bfloat16

EPS = 1e-5
HEAD_DIM = 64
STATE = 128
GROUPS = 2
SHORT_K = 3
SSD_K = 4
LANES = 128
PAIR = LANES // HEAD_DIM
SCAN_CHUNK = 256
HALO = 16
N_CHIPS = 4
VMEM_LIMIT = 56 * 1024 * 1024

ADAM_LR = 0.001
ADAM_B1 = 0.9
ADAM_B2 = 0.999
ADAM_EPS = 1e-08
ADAM_WD = 0.01
ADAM_STEP = 10

MESH = pl.DeviceIdType.MESH


def _params(sem):
    return pltpu.CompilerParams(dimension_semantics=sem, vmem_limit_bytes=VMEM_LIMIT)


def _tile(n, cap, quantum):
    if n <= cap:
        return n
    best = None
    for t in range(quantum, cap + 1, quantum):
        if n % t == 0:
            best = t
    assert best is not None, (n, cap, quantum)
    return best


def _dot(a, b):
    return jnp.dot(a, b, preferred_element_type=F32)


def _dot_nt(a, b):
    return lax.dot_general(a, b, (((1,), (1,)), ((), ())), preferred_element_type=F32)


def _dot_tn(a, b):
    return lax.dot_general(a, b, (((0,), (0,)), ((), ())), preferred_element_type=F32)


def _dot_exact(a, b):
    return jnp.dot(a, b, precision=lax.Precision.HIGHEST, preferred_element_type=F32)


def _sigmoid(x):
    return 1.0 / (1.0 + jnp.exp(-x))


def _softplus(x):
    return jnp.maximum(x, 0.0) + jnp.log(1.0 + jnp.exp(-jnp.abs(x)))


def _relu2(v):
    return jnp.square(jnp.maximum(v, 0.0))


class Rider:
    def __init__(self, ins, out_shapes, aliases, n_sems, copies):
        self.ins, self.out_shapes, self.aliases, self.n_sems, self.copies = ins, out_shapes, aliases, n_sems, copies


def _any_specs(n):
    return [pl.BlockSpec(memory_space=pl.ANY)] * n


def _ride(body, grid, in_specs, out_specs, out_shape, scratch, args, aliases, rider, sem, name):
    n_in, n_out, n_scr = len(in_specs), len(out_specs), len(scratch)
    if rider is None:
        outs = pl.pallas_call(
            body, name=name, grid=grid, in_specs=in_specs, out_specs=out_specs, out_shape=out_shape,
            scratch_shapes=scratch, input_output_aliases=aliases, compiler_params=_params(sem))(*args)
        return list(outs), []
    ri, ro = len(rider.ins), len(rider.out_shapes)
    last = tuple(g - 1 for g in grid)

    def wrapped(*refs):
        ins = refs[:n_in]
        r_ins = refs[n_in:n_in + ri]
        outs = refs[n_in + ri:n_in + ri + n_out]
        r_outs = refs[n_in + ri + n_out:n_in + ri + n_out + ro]
        scr = refs[n_in + ri + n_out + ro:n_in + ri + n_out + ro + n_scr]
        send_sems, recv_sems = refs[-2:]
        ids = [pl.program_id(a) for a in range(len(grid))]
        at_first = functools.reduce(jnp.logical_and, [i == 0 for i in ids])
        at_last = functools.reduce(jnp.logical_and, [i == e for i, e in zip(ids, last)])

        @pl.when(at_first)
        def _():
            for cp, _ in rider.copies(r_ins, r_outs, send_sems, recv_sems):
                cp.start()

        body(*ins, *outs, *scr)

        @pl.when(at_last)
        def _():
            for cp, landed in rider.copies(r_ins, r_outs, send_sems, recv_sems):
                cp.wait_send()
                landed.wait_recv()

    all_aliases = dict(aliases)
    all_aliases.update({n_in + a: n_out + b for a, b in rider.aliases.items()})
    outs = pl.pallas_call(
        wrapped, name=name, grid=grid, in_specs=list(in_specs) + _any_specs(ri),
        out_specs=list(out_specs) + _any_specs(ro), out_shape=list(out_shape) + list(rider.out_shapes),
        scratch_shapes=list(scratch) + [pltpu.SemaphoreType.DMA((rider.n_sems,)),
                                        pltpu.SemaphoreType.DMA((rider.n_sems,))],
        input_output_aliases=all_aliases, compiler_params=_params(sem))(*args, *rider.ins)
    return list(outs[:n_out]), list(outs[n_out:])


def run_rider(rider, name):
    ri, ro = len(rider.ins), len(rider.out_shapes)

    def body(*refs):
        send_sems, recv_sems = refs[-2:]
        pairs = rider.copies(refs[:ri], refs[ri:ri + ro], send_sems, recv_sems)
        for cp, _ in pairs:
            cp.start()
        for cp, landed in pairs:
            cp.wait_send()
            landed.wait_recv()

    return list(pl.pallas_call(
        body, name=name, in_specs=_any_specs(ri), out_specs=_any_specs(ro), out_shape=list(rider.out_shapes),
        scratch_shapes=[pltpu.SemaphoreType.DMA((rider.n_sems,)), pltpu.SemaphoreType.DMA((rider.n_sems,))],
        input_output_aliases=dict(rider.aliases),
        compiler_params=pltpu.CompilerParams(has_side_effects=True))(*rider.ins))


def norm_matmul(x, nw, w, layer, n, col0, out_dtype, name, emit_h=True, rider=None):
    t, d = x.shape
    tm = _tile(t, 1024, 8)
    tn = _tile(n, 1536, LANES)
    nj = n // tn

    def body(x_ref, nw_ref, w_ref, o_ref, h_ref):
        @pl.when(pl.program_id(1) == 0)
        def _():
            xf = x_ref[...]
            r = lax.rsqrt(jnp.mean(xf * xf, axis=-1, keepdims=True) + EPS)
            h_ref[...] = (xf * r * nw_ref[...]).astype(BF16)

        o_ref[...] = _dot(h_ref[...], w_ref[...]).astype(out_dtype)

    out_specs = [pl.BlockSpec((tm, tn), lambda i, j: (i, j))]
    out_shape = [jax.ShapeDtypeStruct((t, n), out_dtype)]
    if emit_h:
        out_specs.append(pl.BlockSpec((tm, d), lambda i, j: (i, 0)))
        out_shape.append(jax.ShapeDtypeStruct((t, d), BF16))
    return _ride(
        body, (t // tm, nj),
        [pl.BlockSpec((tm, d), lambda i, j: (i, 0)), pl.BlockSpec((1, d), lambda i, j: (0, 0)),
         pl.BlockSpec((None, d, tn), lambda i, j: (layer, 0, col0 * nj + j))],
        out_specs, out_shape, [] if emit_h else [pltpu.VMEM((tm, d), BF16)], [x, nw, w], {}, rider,
        ("parallel", "arbitrary"), name)


def matmul(lhs, w, layer, transposed, n, out_dtype, name, *, lhs_fn=None, residual=None, relu_gate=None,
           rider=None):
    t, k = lhs.shape
    big = k > 2048
    tm = _tile(t, 512 if big else 1024, 8)
    tn = _tile(n, 512 if big else 1024, LANES)
    staged = lhs.dtype != BF16 or lhs_fn is not None
    fn = lhs_fn if lhs_fn is not None else (lambda v: v)
    has_extra = residual is not None or relu_gate is not None
    dot = _dot_nt if transposed else _dot

    def body(*refs):
        a_ref, w_ref = refs[:2]
        extra = refs[2] if has_extra else None
        o_ref = refs[3] if has_extra else refs[2]
        if staged:
            s_ref = refs[-1]

            @pl.when(pl.program_id(1) == 0)
            def _():
                s_ref[...] = fn(a_ref[...].astype(F32)).astype(BF16)

            a_ref = s_ref
        acc = dot(a_ref[...], w_ref[...])
        if residual is not None:
            acc = acc + extra[...]
        if relu_gate is not None:
            acc = acc * (2.0 * jnp.maximum(extra[...].astype(F32), 0.0))
        o_ref[...] = acc.astype(out_dtype)

    if transposed:
        w_spec = pl.BlockSpec((None, tn, k), lambda i, j: (layer, j, 0))
    else:
        w_spec = pl.BlockSpec((None, k, tn), lambda i, j: (layer, 0, j))
    in_specs = [pl.BlockSpec((tm, k), lambda i, j: (i, 0)), w_spec]
    args = [lhs, w]
    if has_extra:
        in_specs.append(pl.BlockSpec((tm, tn), lambda i, j: (i, j)))
        args.append(residual if residual is not None else relu_gate)
    outs, extra = _ride(
        body, (t // tm, n // tn), in_specs, [pl.BlockSpec((tm, tn), lambda i, j: (i, j))],
        [jax.ShapeDtypeStruct((t, n), out_dtype)], [pltpu.VMEM((tm, k), BF16)] if staged else [], args, {},
        rider, ("parallel", "arbitrary"), name)
    return outs[0], extra


def matmul_normbwd(lhs, pieces, w, layer, x, nw, dres, name):
    t, d = x.shape
    nl = len(lhs)
    tm = _tile(t, 256, 8)
    for off, width in pieces:
        assert off % width == 0

    def body(*refs):
        lrefs = refs[:nl]
        wrefs = refs[nl:2 * nl]
        x_ref, nw_ref, dres_ref, dx_ref, dnw_ref = refs[2 * nl:]
        dh = _dot_nt(lrefs[0][...].astype(BF16), wrefs[0][...])
        for a_ref, w_ref in zip(lrefs[1:], wrefs[1:]):
            dh = dh + _dot_nt(a_ref[...].astype(BF16), w_ref[...])
        xf = x_ref[...]
        r = lax.rsqrt(jnp.mean(xf * xf, axis=-1, keepdims=True) + EPS)
        nx = xf * r
        dn = dh * nw_ref[...]
        dx = r * (dn - nx * jnp.mean(dn * nx, axis=-1, keepdims=True))
        dx_ref[...] = dres_ref[...] + dx

        @pl.when(pl.program_id(0) == 0)
        def _():
            dnw_ref[...] = jnp.zeros_like(dnw_ref)

        dnw_ref[...] += jnp.sum(dh * nx, axis=0, keepdims=True)

    in_specs = [pl.BlockSpec((tm, width), lambda i: (i, 0)) for _, width in pieces]
    in_specs += [pl.BlockSpec((None, d, width), (lambda blk: (lambda i: (layer, 0, blk)))(off // width))
                 for off, width in pieces]
    in_specs += [pl.BlockSpec((tm, d), lambda i: (i, 0)), pl.BlockSpec((1, d), lambda i: (0, 0)),
                 pl.BlockSpec((tm, d), lambda i: (i, 0))]
    return pl.pallas_call(
        body, name=name, grid=(t // tm,), in_specs=in_specs,
        out_specs=[pl.BlockSpec((tm, d), lambda i: (i, 0)), pl.BlockSpec((1, d), lambda i: (0, 0))],
        out_shape=[jax.ShapeDtypeStruct((t, d), F32), jax.ShapeDtypeStruct((1, d), F32)],
        compiler_params=_params(("arbitrary",)),
    )(*lhs, *([w] * nl), x, nw, dres)


def matmul_tn(a, b, name, *, a_fn=None, by_chip=False):
    t, k = a.shape
    n = b.shape[1]
    tk = _tile(k, 1024, LANES)
    tn = _tile(n // N_CHIPS if by_chip else n, 1024, LANES)
    tt = _tile(t, 1024, 8)
    nt = t // tt
    fn = a_fn if a_fn is not None else (lambda v: v)

    def body(a_ref, b_ref, o_ref, acc_ref):
        @pl.when(pl.program_id(2) == 0)
        def _():
            acc_ref[...] = jnp.zeros_like(acc_ref)

        av = a_ref[...]
        if a_fn is not None:
            av = fn(av.astype(F32))
        acc_ref[...] += _dot_tn(av.astype(BF16), b_ref[...].astype(BF16))

        @pl.when(pl.program_id(2) == nt - 1)
        def _():
            o_ref[...] = acc_ref[...]

    if by_chip:
        per = n // N_CHIPS // tn
        out_spec = pl.BlockSpec((None, tk, tn), lambda i, j, s: (j // per, i, j % per))
        out_shape = jax.ShapeDtypeStruct((N_CHIPS, k, n // N_CHIPS), F32)
    else:
        out_spec = pl.BlockSpec((tk, tn), lambda i, j, s: (i, j))
        out_shape = jax.ShapeDtypeStruct((k, n), F32)
    return pl.pallas_call(
        body, name=name, grid=(k // tk, n // tn, nt),
        in_specs=[pl.BlockSpec((tt, tk), lambda i, j, s: (s, i)),
                  pl.BlockSpec((tt, tn), lambda i, j, s: (s, j))],
        out_specs=out_spec, out_shape=out_shape,
        scratch_shapes=[pltpu.VMEM((tk, tn), F32)],
        compiler_params=_params(("parallel", "parallel", "arbitrary")),
    )(a, b)


def split_to_chips(pieces, cols, name):
    d = pieces[0].shape[0]
    widths = [p.shape[1] for p in pieces]
    w = cols // N_CHIPS
    tr = _tile(d, 256, 8)
    npc = len(pieces)

    def body(*refs):
        o_ref, row = refs[npc], refs[npc + 1]
        off = 0
        for r, n in zip(refs[:npc], widths):
            row[:, off:off + n] = r[...]
            off += n
        for j in range(N_CHIPS):
            o_ref[j] = row[:, j * w:(j + 1) * w]

    return pl.pallas_call(
        body, name=name, grid=(d // tr,),
        in_specs=[pl.BlockSpec((tr, n), lambda i: (i, 0)) for n in widths],
        out_specs=pl.BlockSpec((N_CHIPS, tr, w), lambda i: (0, i, 0)),
        out_shape=jax.ShapeDtypeStruct((N_CHIPS, d, w), F32),
        scratch_shapes=[pltpu.VMEM((tr, sum(widths)), F32)],
        compiler_params=_params(("parallel",)),
    )(*pieces)


def join_from_chips(g4, npad, name):
    _, nl, d, w = g4.shape
    tr = _tile(d, 256, HALO)

    def body(g_ref, o_ref):
        for j in range(N_CHIPS):
            o_ref[:, j * w:(j + 1) * w] = g_ref[j]
        o_ref[:, N_CHIPS * w:] = jnp.zeros((tr, npad - N_CHIPS * w), o_ref.dtype)

    return pl.pallas_call(
        body, name=name, grid=(nl, d // tr),
        in_specs=[pl.BlockSpec((N_CHIPS, None, tr, w), lambda l, i: (0, l, i, 0))],
        out_specs=pl.BlockSpec((None, tr, npad), lambda l, i: (l, i, 0)),
        out_shape=jax.ShapeDtypeStruct((nl, d, npad), g4.dtype),
        compiler_params=_params(("parallel", "parallel")),
    )(g4)


def conv_mixer_fwd(proj, kw, cw, out_cols, name):
    t = proj.shape[0]
    tm = _tile(t, 512, HALO)
    tc = _tile(cw, 512, LANES)
    nj = cw // tc
    hb = tm // HALO

    def body(ub_ref, uc_ref, uh_ref, ucp_ref, uhp_ref, kw_ref, y_ref, ext):
        i = pl.program_id(0)
        v = uc_ref[...].astype(F32) * uh_ref[...].astype(F32)
        vp = ucp_ref[...].astype(F32) * uhp_ref[...].astype(F32)
        ext[0:HALO, :] = jnp.where(i > 0, vp, 0.0)
        ext[HALO:HALO + tm, :] = v
        cv = kw_ref[pl.ds(SHORT_K - 1, 1), :] * v
        for k in range(SHORT_K - 1):
            cv = cv + kw_ref[pl.ds(k, 1), :] * ext[pl.ds(HALO - (SHORT_K - 1) + k, tm), :]
        y_ref[...] = (ub_ref[...].astype(F32) * cv).astype(BF16)

    prev = lambda off: (lambda i, j: (jnp.maximum(i * hb - 1, 0), off + j))
    return pl.pallas_call(
        body, name=name, grid=(t // tm, nj),
        in_specs=[pl.BlockSpec((tm, tc), lambda i, j: (i, j)),
                  pl.BlockSpec((tm, tc), lambda i, j: (i, nj + j)),
                  pl.BlockSpec((tm, tc), lambda i, j: (i, 2 * nj + j)),
                  pl.BlockSpec((HALO, tc), prev(nj)),
                  pl.BlockSpec((HALO, tc), prev(2 * nj)),
                  pl.BlockSpec((SHORT_K, tc), lambda i, j: (0, j))],
        out_specs=pl.BlockSpec((tm, tc), lambda i, j: (i, j)),
        out_shape=jax.ShapeDtypeStruct((t, out_cols), BF16),
        scratch_shapes=[pltpu.VMEM((tm + HALO, tc), F32)],
        compiler_params=_params(("parallel", "parallel")),
    )(proj, proj, proj, proj, proj, kw)


def conv_mixer_bwd(proj, dy, kw, cw, name):
    t = proj.shape[0]
    tm = _tile(t, 512, HALO)
    tc = _tile(cw, 512, LANES)
    nj = cw // tc
    hb = tm // HALO
    ni = t // tm
    last_hb = t // HALO - 1

    def body(ub_ref, uc_ref, uh_ref, dy_ref, ucp_ref, uhp_ref, ubn_ref, dyn_ref, kw_ref,
             dub_ref, duc_ref, duh_ref, dkw_ref, ext, extd):
        i = pl.program_id(1)
        ub = ub_ref[...].astype(F32)
        uc = uc_ref[...].astype(F32)
        uh = uh_ref[...].astype(F32)
        dyv = dy_ref[...].astype(F32)
        v = uc * uh
        vp = ucp_ref[...].astype(F32) * uhp_ref[...].astype(F32)
        ext[0:HALO, :] = jnp.where(i > 0, vp, 0.0)
        ext[HALO:HALO + tm, :] = v
        cv = kw_ref[pl.ds(SHORT_K - 1, 1), :] * v
        for k in range(SHORT_K - 1):
            cv = cv + kw_ref[pl.ds(k, 1), :] * ext[pl.ds(HALO - (SHORT_K - 1) + k, tm), :]
        dcv = dyv * ub
        dcvn = dyn_ref[...].astype(F32) * ubn_ref[...].astype(F32)
        extd[0:tm, :] = dcv
        extd[tm:tm + HALO, :] = jnp.where(i < ni - 1, dcvn, 0.0)
        dv = kw_ref[pl.ds(SHORT_K - 1, 1), :] * dcv
        for k in range(SHORT_K - 1):
            dv = dv + kw_ref[pl.ds(k, 1), :] * extd[pl.ds(SHORT_K - 1 - k, tm), :]
        dub_ref[...] = (dyv * cv).astype(BF16)
        duc_ref[...] = (dv * uh).astype(BF16)
        duh_ref[...] = (dv * uc).astype(BF16)

        @pl.when(i == 0)
        def _():
            dkw_ref[...] = jnp.zeros_like(dkw_ref)

        for k in range(SHORT_K):
            sh = ext[pl.ds(HALO - (SHORT_K - 1) + k, tm), :]
            dkw_ref[pl.ds(k, 1), :] += jnp.sum(dcv * sh, axis=0, keepdims=True)

    prev = lambda off: (lambda j, i: (jnp.maximum(i * hb - 1, 0), off + j))
    nxt = lambda off: (lambda j, i: (jnp.minimum((i + 1) * hb, last_hb), off + j))
    cur = lambda off: (lambda j, i: (i, off + j))
    return pl.pallas_call(
        body, name=name, grid=(nj, ni),
        in_specs=[pl.BlockSpec((tm, tc), cur(0)), pl.BlockSpec((tm, tc), cur(nj)),
                  pl.BlockSpec((tm, tc), cur(2 * nj)), pl.BlockSpec((tm, tc), cur(0)),
                  pl.BlockSpec((HALO, tc), prev(nj)), pl.BlockSpec((HALO, tc), prev(2 * nj)),
                  pl.BlockSpec((HALO, tc), nxt(0)), pl.BlockSpec((HALO, tc), nxt(0)),
                  pl.BlockSpec((SHORT_K, tc), lambda j, i: (0, j))],
        out_specs=[pl.BlockSpec((tm, tc), cur(0)), pl.BlockSpec((tm, tc), cur(0)),
                   pl.BlockSpec((tm, tc), cur(0)), pl.BlockSpec((SHORT_K, tc), lambda j, i: (0, j))],
        out_shape=[jax.ShapeDtypeStruct((t, cw), BF16)] * 3 + [jax.ShapeDtypeStruct((SHORT_K, cw), F32)],
        scratch_shapes=[pltpu.VMEM((tm + HALO, tc), F32), pltpu.VMEM((tm + HALO, tc), F32)],
        compiler_params=_params(("parallel", "arbitrary")),
    )(proj, proj, proj, dy, proj, proj, proj, dy, kw)


def _head_column(mat, lane, h):
    return jnp.sum(jnp.where(lane == h, mat, 0.0), axis=-1, keepdims=True)


def _ssd_common(dt_raw_ref, dtb_ref, aneg_ref, cum_s, cumt_s, chunk):
    dt = _softplus(dt_raw_ref[...] + dtb_ref[...])
    al = dt * aneg_ref[...]
    ri = lax.broadcasted_iota(jnp.int32, (chunk, chunk), 0)
    ci = lax.broadcasted_iota(jnp.int32, (chunk, chunk), 1)
    cum = _dot_exact((ri >= ci).astype(F32), al)
    cum_s[...] = cum
    cumt_s[...] = cum.T
    return dt, cum, ri >= ci


def _causal_conv(ext, kw_ref, b_ref, chunk, kk):
    acc = b_ref[...] + kw_ref[pl.ds(kk - 1, 1), :] * ext[pl.ds(8, chunk), :]
    for k in range(kk - 1):
        acc = acc + kw_ref[pl.ds(k, 1), :] * ext[pl.ds(8 - (kk - 1) + k, chunk), :]
    return acc


def ssd_fwd(proj, dt_raw, y_mix, kw_xs, kw_bc, b_xs, b_bc, dtb, aneg, dskip, normw, cw, si, name, rider=None):
    t = proj.shape[0]
    ch = min(SCAN_CHUNK, t)
    nc = t // ch
    npair = si // LANES
    ppg = npair // GROUPS
    gn = GROUPS * STATE
    gw = si // GROUPS
    assert cw == si and (3 * cw + 2 * si) % (2 * gn) == 0
    zblk = 3 * cw // si
    xsblk = zblk + 1
    bcblk = (3 * cw + 2 * si) // (2 * gn)

    def body(z_ref, xs_ref, bc_ref, dtr_ref, ymix_ref, kwx_ref, kwb_ref, bx_ref, bb_ref, dtb_ref, aneg_ref, dsk_ref,
             nw_ref, yb_ref, ys_ref, hs_ref, extx, extb, xs_s, bc_s, h_s, gated_s, s_s, cum_s, cumt_s):
        del ymix_ref
        c = pl.program_id(0)

        @pl.when(c == 0)
        def _():
            h_s[...] = jnp.zeros_like(h_s)
            extx[0:8, :] = jnp.zeros((8, si), F32)
            extb[0:8, :] = jnp.zeros((8, 2 * gn), F32)

        @pl.when(c > 0)
        def _():
            extx[0:8, :] = extx[ch:ch + 8, :]
            extb[0:8, :] = extb[ch:ch + 8, :]

        extx[8:8 + ch, :] = xs_ref[...].astype(F32)
        extb[8:8 + ch, :] = bc_ref[...].astype(F32)
        xc = _causal_conv(extx, kwx_ref, bx_ref, ch, SSD_K)
        xs_s[...] = xc * _sigmoid(xc)
        bcc = _causal_conv(extb, kwb_ref, bb_ref, ch, SSD_K)
        bc_s[...] = (bcc * _sigmoid(bcc)).astype(BF16)

        dt, cum, tril = _ssd_common(dtr_ref, dtb_ref, aneg_ref, cum_s, cumt_s, ch)
        lane = lax.broadcasted_iota(jnp.int32, (ch, LANES), 1)
        lane1 = lax.broadcasted_iota(jnp.int32, (1, LANES), 1)
        low = lane < HEAD_DIM
        clast = cum_s[pl.ds(ch - 1, 1), :]

        for p in range(npair):
            g = p // ppg
            col = slice(p * LANES, (p + 1) * LANES)
            bg = bc_s[:, g * STATE:(g + 1) * STATE]
            cg = bc_s[:, gn + g * STATE:gn + (g + 1) * STATE]
            if p % ppg == 0:
                s_s[...] = _dot_nt(cg, bg)
            heads = (PAIR * p, PAIR * p + 1)
            ccol = [_head_column(cum, lane, h) for h in heads]
            dcol = [_head_column(dt, lane, h) for h in heads]
            cl = [jnp.sum(jnp.where(lane1 == h, clast, 0.0), axis=-1, keepdims=True) for h in heads]
            cum_px = jnp.where(low, ccol[0], ccol[1])
            dt_px = jnp.where(low, dcol[0], dcol[1])
            cl_px = jnp.where(lane1 < HEAD_DIM, cl[0], cl[1])
            xs_p = xs_s[:, col]
            xdt = xs_p * dt_px
            y = dsk_ref[:, col] * xs_p
            for hi, h in enumerate(heads):
                dec = jnp.exp(jnp.where(tril, ccol[hi] - cumt_s[pl.ds(h, 1), :], -jnp.inf))
                wm = (s_s[...] * dec).astype(BF16)
                xm = jnp.where(low if hi == 0 else jnp.logical_not(low), xdt, 0.0).astype(BF16)
                y = y + _dot(wm, xm)
            hp = h_s[p]
            hs_ref[0, p] = hp
            y = y + _dot(cg, hp.astype(BF16)) * jnp.exp(cum_px)
            st = _dot_tn(bg, (xdt * jnp.exp(cl_px - cum_px)).astype(BF16))
            h_s[p] = jnp.exp(cl_px) * hp + st
            ys_ref[:, col] = y.astype(BF16)
            zp = z_ref[:, col].astype(F32)
            gated_s[:, col] = y * zp * _sigmoid(zp)

        for g in range(GROUPS):
            col = slice(g * gw, (g + 1) * gw)
            gg = gated_s[:, col]
            r = lax.rsqrt(jnp.mean(gg * gg, axis=-1, keepdims=True) + EPS)
            yb_ref[:, col] = (gg * r * nw_ref[:, col]).astype(BF16)

    full = lambda shape: pl.BlockSpec(shape, lambda c: tuple(0 for _ in shape))
    return _ride(
        body, (nc,),
        [pl.BlockSpec((ch, si), lambda c: (c, zblk)),
         pl.BlockSpec((ch, si), lambda c: (c, xsblk)),
         pl.BlockSpec((ch, 2 * gn), lambda c: (c, bcblk)),
         pl.BlockSpec((ch, LANES), lambda c: (c, 0)),
         pl.BlockSpec(memory_space=pl.ANY),
         full((SSD_K, si)), full((SSD_K, 2 * gn)), full((1, si)), full((1, 2 * gn)),
         full((1, LANES)), full((1, LANES)), full((1, si)), full((1, si))],
        [pl.BlockSpec((ch, si), lambda c: (c, cw // si)),
         pl.BlockSpec((ch, si), lambda c: (c, 0)),
         pl.BlockSpec((1, npair, STATE, LANES), lambda c: (c, 0, 0, 0))],
        [jax.ShapeDtypeStruct(y_mix.shape, BF16), jax.ShapeDtypeStruct((t, si), BF16),
         jax.ShapeDtypeStruct((nc, npair, STATE, LANES), F32)],
        [pltpu.VMEM((ch + 8, si), F32), pltpu.VMEM((ch + 8, 2 * gn), F32),
         pltpu.VMEM((ch, si), F32), pltpu.VMEM((ch, 2 * gn), BF16),
         pltpu.VMEM((npair, STATE, LANES), F32), pltpu.VMEM((ch, si), F32),
         pltpu.VMEM((ch, ch), F32), pltpu.VMEM((ch, LANES), F32), pltpu.VMEM((LANES, ch), F32)],
        [proj, proj, proj, dt_raw, y_mix, kw_xs, kw_bc, b_xs, b_bc, dtb, aneg, dskip, normw], {4: 0}, rider,
        ("arbitrary",), name)


def ssd_bwd(proj, dt_raw, ys, hsave, dy, kw_xs, kw_bc, b_xs, b_bc, dtb, aneg, dskip, normw, cw, si, name,
            rider=None):
    t = proj.shape[0]
    ch = min(SCAN_CHUNK, t)
    nc = t // ch
    npair = si // LANES
    ppg = npair // GROUPS
    gn = GROUPS * STATE
    gw = si // GROUPS
    zblk = 3 * cw // si
    xsblk = zblk + 1
    bcblk = (3 * cw + 2 * si) // (2 * gn)
    hb = ch // HALO

    def body(z_ref, xs_ref, bc_ref, xsp_ref, bcp_ref, dtr_ref, ys_ref, hs_ref, dyb_ref,
             kwx_ref, kwb_ref, bx_ref, bb_ref, dtb_ref, aneg_ref, dsk_ref, nw_ref,
             dz_ref, dxs_ref, dbc_ref, ddt_ref, dkwx_ref, dkwb_ref, dbx_ref, dbb_ref, ddtb_ref, da_ref, ddsk_ref,
             dnw_ref,
             extx, extb, extdx, extdb, xs_s, bc_s, dsx_s, dsb_s, dy_s, dxs_s, dbc_s, dh_s, s_s, ds_s,
             cum_s, cumt_s, dccol_s, dcrow_s, ddtcol_s, dcl_s):
        i = pl.program_id(0)
        rc = nc - 1 - i

        @pl.when(i == 0)
        def _():
            dh_s[...] = jnp.zeros_like(dh_s)
            extdx[ch:ch + 8, :] = jnp.zeros((8, si), F32)
            extdb[ch:ch + 8, :] = jnp.zeros((8, 2 * gn), F32)
            for r in (dkwx_ref, dkwb_ref, dbx_ref, dbb_ref, ddtb_ref, da_ref, ddsk_ref, dnw_ref):
                r[...] = jnp.zeros_like(r)

        extx[0:8, :] = jnp.where(rc > 0, xsp_ref[pl.ds(HALO - 8, 8), :].astype(F32), 0.0)
        extb[0:8, :] = jnp.where(rc > 0, bcp_ref[pl.ds(HALO - 8, 8), :].astype(F32), 0.0)
        extx[8:8 + ch, :] = xs_ref[...].astype(F32)
        extb[8:8 + ch, :] = bc_ref[...].astype(F32)
        xc = _causal_conv(extx, kwx_ref, bx_ref, ch, SSD_K)
        sg = _sigmoid(xc)
        xs_s[...] = xc * sg
        dsx_s[...] = sg * (1.0 + xc * (1.0 - sg))
        bcc = _causal_conv(extb, kwb_ref, bb_ref, ch, SSD_K)
        sgb = _sigmoid(bcc)
        bc_s[...] = (bcc * sgb).astype(BF16)
        dsb_s[...] = sgb * (1.0 + bcc * (1.0 - sgb))

        dt, cum, tril = _ssd_common(dtr_ref, dtb_ref, aneg_ref, cum_s, cumt_s, ch)
        lane = lax.broadcasted_iota(jnp.int32, (ch, LANES), 1)
        lane1 = lax.broadcasted_iota(jnp.int32, (1, LANES), 1)
        low = lane < HEAD_DIM
        low1 = lane1 < HEAD_DIM
        clast = cum_s[pl.ds(ch - 1, 1), :]

        for g in range(GROUPS):
            col = slice(g * gw, (g + 1) * gw)
            ysf = ys_ref[:, col].astype(F32)
            zf = z_ref[:, col].astype(F32)
            sz = _sigmoid(zf)
            silz = zf * sz
            gg = ysf * silz
            r = lax.rsqrt(jnp.mean(gg * gg, axis=-1, keepdims=True) + EPS)
            nrm = gg * r
            dyb = dyb_ref[:, col].astype(F32)
            dnw_ref[:, col] += jnp.sum(dyb * nrm, axis=0, keepdims=True)
            dn = dyb * nw_ref[:, col]
            dgg = r * (dn - nrm * jnp.mean(dn * nrm, axis=-1, keepdims=True))
            dy_s[:, col] = dgg * silz
            dz_ref[:, col] = (dgg * ysf * (sz * (1.0 + zf * (1.0 - sz)))).astype(BF16)

        dccol_s[...] = jnp.zeros_like(dccol_s)
        dcrow_s[...] = jnp.zeros_like(dcrow_s)
        ddtcol_s[...] = jnp.zeros_like(ddtcol_s)
        dcl_s[...] = jnp.zeros_like(dcl_s)
        dbc_s[...] = jnp.zeros_like(dbc_s)

        for p in range(npair):
            g = p // ppg
            col = slice(p * LANES, (p + 1) * LANES)
            bcol = slice(g * STATE, (g + 1) * STATE)
            ccolg = slice(gn + g * STATE, gn + (g + 1) * STATE)
            bg = bc_s[:, bcol]
            cg = bc_s[:, ccolg]
            if p % ppg == 0:
                s_s[...] = _dot_nt(cg, bg)
                ds_s[...] = jnp.zeros_like(ds_s)
            heads = (PAIR * p, PAIR * p + 1)
            masks = (low, jnp.logical_not(low))
            masks1 = (low1, jnp.logical_not(low1))
            ccol = [_head_column(cum, lane, h) for h in heads]
            dcol = [_head_column(dt, lane, h) for h in heads]
            cl = [jnp.sum(jnp.where(lane1 == h, clast, 0.0), axis=-1, keepdims=True) for h in heads]
            cum_px = jnp.where(low, ccol[0], ccol[1])
            dt_px = jnp.where(low, dcol[0], dcol[1])
            cl_px = jnp.where(low1, cl[0], cl[1])
            e_px = jnp.exp(cum_px)
            dec_end = jnp.exp(cl_px - cum_px)
            gdec = jnp.exp(cl_px)
            xs_p = xs_s[:, col]
            xdt = xs_p * dt_px
            dyp = dy_s[:, col]
            hc = hs_ref[0, p]
            hcb = hc.astype(BF16)
            dhn = dh_s[p]
            dhnb = dhn.astype(BF16)

            ddsk_ref[:, col] += jnp.sum(dyp * xs_p, axis=0, keepdims=True)
            dxs_acc = dsk_ref[:, col] * dyp
            dye = dyp * e_px
            dyeb = dye.astype(BF16)
            dbc_s[:, ccolg] += _dot_nt(dyeb, hcb)
            dcum_lane = dye * _dot(cg, hcb)
            dh_from_y = _dot_tn(cg, dyeb)
            xd = xdt * dec_end
            dxd = _dot(bg, dhnb)
            dbc_s[:, bcol] += _dot_nt(xd.astype(BF16), dhnb)
            dxdt = dxd * dec_end
            t1 = dxd * xd
            dcum_lane = dcum_lane - t1
            dcl_lane = jnp.sum(t1, axis=0, keepdims=True) + jnp.sum(dhn * hc, axis=0, keepdims=True) * gdec
            dh_s[p] = gdec * dhn + dh_from_y
            xdtb = xdt.astype(BF16)
            for hi, h in enumerate(heads):
                dym = jnp.where(masks[hi], dyp, 0.0).astype(BF16)
                dw = _dot_nt(dym, xdtb)
                dec = jnp.exp(jnp.where(tril, ccol[hi] - cumt_s[pl.ds(h, 1), :], -jnp.inf))
                wm = s_s[...] * dec
                dxdt = dxdt + _dot_tn(wm.astype(BF16), dym)
                ds_s[...] += dw * dec
                gm = dw * wm
                rowsum = jnp.sum(gm, axis=-1, keepdims=True)
                lanesum = jnp.sum(jnp.where(masks[hi], dcum_lane, 0.0), axis=-1, keepdims=True)
                dccol_s[...] += jnp.where(lane == h, rowsum + lanesum, 0.0)
                dcrow_s[pl.ds(h, 1), :] = jnp.sum(gm, axis=0, keepdims=True)
                dcl_h = jnp.sum(jnp.where(masks1[hi], dcl_lane, 0.0), axis=-1, keepdims=True)
                dcl_s[...] += jnp.where(lane1 == h, dcl_h, 0.0)
            ddt_lane = dxdt * xs_p
            for hi, h in enumerate(heads):
                s = jnp.sum(jnp.where(masks[hi], ddt_lane, 0.0), axis=-1, keepdims=True)
                ddtcol_s[...] += jnp.where(lane == h, s, 0.0)
            dxs_s[:, col] = dxs_acc + dxdt * dt_px
            if p % ppg == ppg - 1:
                dsb = ds_s[...].astype(BF16)
                dbc_s[:, ccolg] += _dot(dsb, bg)
                dbc_s[:, bcol] += _dot_tn(dsb, cg)

        rowi = lax.broadcasted_iota(jnp.int32, (ch, LANES), 0)
        dcum = dccol_s[...] - dcrow_s[...].T + jnp.where(rowi == ch - 1, dcl_s[...], 0.0)
        ri = lax.broadcasted_iota(jnp.int32, (ch, ch), 0)
        ci = lax.broadcasted_iota(jnp.int32, (ch, ch), 1)
        dal = _dot_exact((ri <= ci).astype(F32), dcum)
        ddt = dal * aneg_ref[...] + ddtcol_s[...]
        da_ref[...] += jnp.sum(dal * dt, axis=0, keepdims=True)
        ddtr = ddt * _sigmoid(dtr_ref[...] + dtb_ref[...])
        ddt_ref[...] = ddtr
        ddtb_ref[...] += jnp.sum(ddtr, axis=0, keepdims=True)

        for (dpost, dsl, extd, ext, kw_ref, dkw_ref, db_ref, out_ref) in (
                (dxs_s, dsx_s, extdx, extx, kwx_ref, dkwx_ref, dbx_ref, dxs_ref),
                (dbc_s, dsb_s, extdb, extb, kwb_ref, dkwb_ref, dbb_ref, dbc_ref)):
            dxc = dpost[...] * dsl[...]
            extd[0:ch, :] = dxc
            draw = kw_ref[pl.ds(SSD_K - 1, 1), :] * dxc
            for k in range(SSD_K - 1):
                draw = draw + kw_ref[pl.ds(k, 1), :] * extd[pl.ds(SSD_K - 1 - k, ch), :]
            out_ref[...] = draw.astype(BF16)
            db_ref[...] += jnp.sum(dxc, axis=0, keepdims=True)
            for k in range(SSD_K):
                sh = ext[pl.ds(8 - (SSD_K - 1) + k, ch), :]
                dkw_ref[pl.ds(k, 1), :] += jnp.sum(dxc * sh, axis=0, keepdims=True)
            extd[ch:ch + 8, :] = extd[0:8, :]

    full = lambda shape: pl.BlockSpec(shape, lambda i: tuple(0 for _ in shape))
    rev = lambda blk: (lambda i: (nc - 1 - i, blk))
    prev = lambda blk: (lambda i: (jnp.maximum((nc - 1 - i) * hb - 1, 0), blk))
    small = [(SSD_K, si), (SSD_K, 2 * gn), (1, si), (1, 2 * gn), (1, LANES), (1, LANES), (1, si), (1, si)]
    return _ride(
        body, (nc,),
        [pl.BlockSpec((ch, si), rev(zblk)), pl.BlockSpec((ch, si), rev(xsblk)),
         pl.BlockSpec((ch, 2 * gn), rev(bcblk)),
         pl.BlockSpec((HALO, si), prev(xsblk)), pl.BlockSpec((HALO, 2 * gn), prev(bcblk)),
         pl.BlockSpec((ch, LANES), rev(0)), pl.BlockSpec((ch, si), rev(0)),
         pl.BlockSpec((1, npair, STATE, LANES), lambda i: (nc - 1 - i, 0, 0, 0)),
         pl.BlockSpec((ch, si), rev(cw // si))] + [full(s) for s in small],
        [pl.BlockSpec((ch, si), rev(0)), pl.BlockSpec((ch, si), rev(0)),
         pl.BlockSpec((ch, 2 * gn), rev(0)), pl.BlockSpec((ch, LANES), rev(0))] + [full(s) for s in small],
        [jax.ShapeDtypeStruct((t, si), BF16), jax.ShapeDtypeStruct((t, si), BF16),
         jax.ShapeDtypeStruct((t, 2 * gn), BF16), jax.ShapeDtypeStruct((t, LANES), F32)]
        + [jax.ShapeDtypeStruct(s, F32) for s in small],
        [pltpu.VMEM((ch + 8, si), F32), pltpu.VMEM((ch + 8, 2 * gn), F32),
         pltpu.VMEM((ch + 8, si), F32), pltpu.VMEM((ch + 8, 2 * gn), F32),
         pltpu.VMEM((ch, si), F32), pltpu.VMEM((ch, 2 * gn), BF16),
         pltpu.VMEM((ch, si), F32), pltpu.VMEM((ch, 2 * gn), F32),
         pltpu.VMEM((ch, si), F32), pltpu.VMEM((ch, si), F32), pltpu.VMEM((ch, 2 * gn), F32),
         pltpu.VMEM((npair, STATE, LANES), F32),
         pltpu.VMEM((ch, ch), F32), pltpu.VMEM((ch, ch), F32),
         pltpu.VMEM((ch, LANES), F32), pltpu.VMEM((LANES, ch), F32),
         pltpu.VMEM((ch, LANES), F32), pltpu.VMEM((LANES, ch), F32),
         pltpu.VMEM((ch, LANES), F32), pltpu.VMEM((1, LANES), F32)],
        [proj, proj, proj, proj, proj, dt_raw, ys, hsave, dy, kw_xs, kw_bc, b_xs, b_bc, dtb, aneg, dskip, normw],
        {}, rider, ("arbitrary",), name)


def final_loss(x, nw, tgt, name):
    t, d = x.shape
    tm = _tile(t, 512, 8)

    def body(x_ref, nw_ref, t_ref, dx_ref, dnw_ref, ls_ref):
        xf = x_ref[...]
        r = lax.rsqrt(jnp.mean(xf * xf, axis=-1, keepdims=True) + EPS)
        nx = xf * r
        e = nx * nw_ref[...] - t_ref[...]
        dyv = e * (1.0 / d)
        dn = dyv * nw_ref[...]
        dx_ref[...] = r * (dn - nx * jnp.mean(dn * nx, axis=-1, keepdims=True))

        @pl.when(pl.program_id(0) == 0)
        def _():
            dnw_ref[...] = jnp.zeros_like(dnw_ref)
            ls_ref[...] = jnp.zeros_like(ls_ref)

        dnw_ref[...] += jnp.sum(dyv * nx, axis=0, keepdims=True)
        ls_ref[...] += jnp.sum(e * e, axis=0, keepdims=True) * (0.5 / d)

    return pl.pallas_call(
        body, name=name, grid=(t // tm,),
        in_specs=[pl.BlockSpec((tm, d), lambda i: (i, 0)), pl.BlockSpec((1, d), lambda i: (0, 0)),
                  pl.BlockSpec((tm, d), lambda i: (i, 0))],
        out_specs=[pl.BlockSpec((tm, d), lambda i: (i, 0)), pl.BlockSpec((1, d), lambda i: (0, 0)),
                   pl.BlockSpec((1, d), lambda i: (0, 0))],
        out_shape=[jax.ShapeDtypeStruct((t, d), F32), jax.ShapeDtypeStruct((1, d), F32),
                   jax.ShapeDtypeStruct((1, d), F32)],
        compiler_params=_params(("arbitrary",)),
    )(x, nw, tgt)


def _rows3(a):
    if a.ndim == 1:
        return a.reshape(1, 1, a.shape[0])
    if a.ndim == 2:
        return a.reshape(1, *a.shape)
    return a.reshape(-1, a.shape[-2], a.shape[-1])


def adamw(w, g, m, v, name):
    shape = w.shape
    views = [_rows3(a) for a in (w, g, m, v)]
    b, r, c = views[0].shape
    tr = _tile(r, 256, 16) if r % 16 == 0 else r

    def body(w_ref, g_ref, m_ref, v_ref, d_ref, nm_ref, nv_ref):
        g = g_ref[...]
        m = ADAM_B1 * m_ref[...] + (1.0 - ADAM_B1) * g
        v = ADAM_B2 * v_ref[...] + (1.0 - ADAM_B2) * (g * g)
        m_hat = m / (1.0 - ADAM_B1 ** ADAM_STEP)
        v_hat = v / (1.0 - ADAM_B2 ** ADAM_STEP)
        d_ref[...] = -ADAM_LR * (m_hat / (jnp.sqrt(v_hat) + ADAM_EPS) + ADAM_WD * w_ref[...])
        nm_ref[...] = m
        nv_ref[...] = v

    spec = pl.BlockSpec((1, tr, c), lambda i, j: (i, j, 0))
    outs = pl.pallas_call(
        body, name=name, grid=(b, r // tr), in_specs=[spec] * 4, out_specs=[spec] * 3,
        out_shape=[jax.ShapeDtypeStruct((b, r, c), F32)] * 3,
        compiler_params=_params(("parallel", "parallel")),
    )(*views)
    return [o.reshape(shape) for o in outs]


def adamw_halves(w, g_mine, g_theirs, m, v, name):
    nl, r, c = w.shape
    r2 = r // 2
    tr = _tile(r2, 256, 16)
    nb = r2 // tr

    def body(w_ref, gm_ref, gt_ref, m_ref, v_ref, g_ref, d_ref, nm_ref, nv_ref):
        mine = (pl.program_id(1) // nb) == lax.axis_index("c")
        g = jnp.where(mine, gm_ref[...], gt_ref[...])
        m = ADAM_B1 * m_ref[...] + (1.0 - ADAM_B1) * g
        v = ADAM_B2 * v_ref[...] + (1.0 - ADAM_B2) * (g * g)
        m_hat = m / (1.0 - ADAM_B1 ** ADAM_STEP)
        v_hat = v / (1.0 - ADAM_B2 ** ADAM_STEP)
        g_ref[...] = g
        d_ref[...] = -ADAM_LR * (m_hat / (jnp.sqrt(v_hat) + ADAM_EPS) + ADAM_WD * w_ref[...])
        nm_ref[...] = m
        nv_ref[...] = v

    whole = pl.BlockSpec((1, tr, c), lambda l, i: (l, i, 0))
    half = pl.BlockSpec((1, tr, c), lambda l, i: (l, i % nb, 0))
    return pl.pallas_call(
        body, name=name, grid=(nl, 2 * nb), in_specs=[whole, half, half, whole, whole], out_specs=[whole] * 4,
        out_shape=[jax.ShapeDtypeStruct((nl, r, c), F32)] * 4,
        compiler_params=_params(("parallel", "parallel")),
    )(w, g_mine, g_theirs, m, v)


def _coords():
    return lax.axis_index("x"), lax.axis_index("y"), lax.axis_index("c")


def _ici_peers(x, y):
    chips = [(1 - x, y), (x, 1 - y), (1 - x, 1 - y)]
    return chips, [2 * cx + cy for cx, cy in chips]


def _place(ref, how, chip, layers, per):
    if how == "lead":
        return ref.at[chip, layers]
    start = pl.multiple_of(chip * per, per)
    if how == "rows":
        return ref.at[layers, pl.ds(start, per), :]
    return ref.at[layers, :, pl.ds(start, per)]


def gather_weights(shards, hows, name):
    na = len(shards)
    out_shape = []
    for s, how in zip(shards, hows):
        assert s.shape[0] % 2 == 0
        if how == "lead":
            shp = (N_CHIPS, *s.shape)
        elif how == "rows":
            shp = (s.shape[0], N_CHIPS * s.shape[1], s.shape[2])
        else:
            shp = (s.shape[0], s.shape[1], N_CHIPS * s.shape[2])
        out_shape.append(jax.ShapeDtypeStruct(shp, s.dtype))

    def body(*refs):
        ins = refs[:na]
        outs = refs[na:2 * na]
        send_sems, recv_sems = refs[2 * na:]
        x, y, c = _coords()
        me = 2 * x + y
        chips, chip_ids = _ici_peers(x, y)
        sibling = (x, y, 1 - c)

        def dst(a, chip, layers):
            per = {"lead": 0, "rows": ins[a].shape[1], "cols": ins[a].shape[-1]}[hows[a]]
            return _place(outs[a], hows[a], chip, layers, per)

        def copy(a, k, src, dst_ref, to):
            return pltpu.make_async_remote_copy(
                src_ref=src, dst_ref=dst_ref, send_sem=send_sems.at[7 * a + k], recv_sem=recv_sems.at[7 * a + k],
                device_id=to, device_id_type=MESH)

        started = []
        halves = []
        for a in range(na):
            nl = ins[a].shape[0]
            hl = nl // 2
            mine = pl.ds(c * hl, hl)
            theirs = pl.ds((1 - c) * hl, hl)
            halves.append((mine, theirs))
            for k in range(3):
                cp = copy(a, k, ins[a].at[mine], dst(a, me, mine), (*chips[k], c))
                cp.start()
                started.append(cp)
            own = copy(a, 6, ins[a], dst(a, me, pl.ds(0, nl)), sibling)
            own.start()
            started.append(own)
        for a in range(na):
            mine, _ = halves[a]
            for k in range(3):
                landed = dst(a, chip_ids[k], mine)
                copy(a, k, landed, landed, (*chips[k], c)).wait_recv()
                fw = copy(a, 3 + k, landed, landed, sibling)
                fw.start()
                started.append(fw)
        for a in range(na):
            _, theirs = halves[a]
            for k in range(3):
                got = dst(a, chip_ids[k], theirs)
                copy(a, 3 + k, got, got, sibling).wait_recv()
            whole = dst(a, me, pl.ds(0, ins[a].shape[0]))
            copy(a, 6, whole, whole, sibling).wait_recv()
        for cp in started:
            cp.wait_send()

    return pl.pallas_call(
        body, name=name, in_specs=_any_specs(na), out_specs=_any_specs(na), out_shape=out_shape,
        scratch_shapes=[pltpu.SemaphoreType.DMA((7 * na,)), pltpu.SemaphoreType.DMA((7 * na,))],
        compiler_params=pltpu.CompilerParams(has_side_effects=True),
    )(*shards)


def _remote(src, dst, send_sems, recv_sems, k, to):
    return pltpu.make_async_remote_copy(src_ref=src, dst_ref=dst, send_sem=send_sems.at[k], recv_sem=recv_sems.at[k],
                                        device_id=to, device_id_type=MESH)


LAYER_HOW = ("lead", "rows", "cols", "rows")


def _layer_place(ref, how, chip, shard_shape, start, size):
    r, c = shard_shape
    if how == "lead":
        return ref.at[chip, :, pl.ds(start, size), :]
    if how == "rows":
        return ref.at[:, pl.ds(pl.multiple_of(chip * r + start, HALO), size), :]
    return ref.at[:, pl.ds(start, size), pl.ds(pl.multiple_of(chip * c, LANES), c)]


def weight_rider_ici(shards, layer):
    shapes = [tuple(s.shape[1:]) for s in shards]
    out_shapes = []
    for (r, c), how, s in zip(shapes, LAYER_HOW, shards):
        shp = {"lead": (N_CHIPS, 1, r, c), "rows": (1, N_CHIPS * r, c), "cols": (1, r, N_CHIPS * c)}[how]
        out_shapes.append(jax.ShapeDtypeStruct(shp, s.dtype))

    def copies(ins, outs, send_sems, recv_sems):
        x, y, c = _coords()
        me = 2 * x + y
        chips, chip_ids = _ici_peers(x, y)
        sibling = (x, y, 1 - c)
        pairs = []
        for a, (shape, how) in enumerate(zip(shapes, LAYER_HOW)):
            half = shape[0] // 2
            mine = pl.multiple_of(c * half, HALO)
            src = ins[a].at[pl.ds(layer, 1)]
            for k in range(3):
                to = (*chips[k], c)
                land = _layer_place(outs[a], how, chip_ids[k], shape, mine, half)
                pairs.append((_remote(src.at[:, pl.ds(mine, half), :], _layer_place(outs[a], how, me, shape, mine, half),
                                      send_sems, recv_sems, 4 * a + k, to),
                              _remote(land, land, send_sems, recv_sems, 4 * a + k, to)))
            whole = _layer_place(outs[a], how, me, shape, 0, shape[0])
            pairs.append((_remote(src, whole, send_sems, recv_sems, 4 * a + 3, sibling),
                          _remote(whole, whole, send_sems, recv_sems, 4 * a + 3, sibling)))
        return pairs

    return Rider(list(shards), out_shapes, {}, 4 * len(shards), copies)


def weight_rider_d2d(bufs, shapes):
    def copies(ins, outs, send_sems, recv_sems):
        x, y, c = _coords()
        _, chip_ids = _ici_peers(x, y)
        sibling = (x, y, 1 - c)
        pairs = []
        for a, (shape, how) in enumerate(zip(shapes, LAYER_HOW)):
            half = shape[0] // 2
            mine = pl.multiple_of(c * half, HALO)
            theirs = pl.multiple_of((1 - c) * half, HALO)
            for k in range(3):
                land = _layer_place(outs[a], how, chip_ids[k], shape, theirs, half)
                pairs.append((_remote(_layer_place(ins[a], how, chip_ids[k], shape, mine, half),
                                      _layer_place(outs[a], how, chip_ids[k], shape, mine, half),
                                      send_sems, recv_sems, 3 * a + k, sibling),
                              _remote(land, land, send_sems, recv_sems, 3 * a + k, sibling)))
        return pairs

    return Rider(list(bufs), [jax.ShapeDtypeStruct(b.shape, b.dtype) for b in bufs],
                 {a: a for a in range(len(bufs))}, 3 * len(bufs), copies)


def grads_rider_sibling(arrs):
    def copies(ins, outs, send_sems, recv_sems):
        x, y, c = _coords()
        sibling = (x, y, 1 - c)
        pairs = []
        for a in range(len(arrs)):
            r2 = ins[a].shape[1] // 2
            src = ins[a].at[:, pl.ds(pl.multiple_of((1 - c) * r2, 8), r2), :]
            pairs.append((_remote(src, outs[a], send_sems, recv_sems, a, sibling),
                          _remote(outs[a], outs[a], send_sems, recv_sems, a, sibling)))
        return pairs

    return Rider(list(arrs), [jax.ShapeDtypeStruct((a.shape[0], a.shape[1] // 2, a.shape[2]), a.dtype) for a in arrs],
                 {}, len(arrs), copies)


def chip_sum(g, recv, name):
    nch, r, c = g.shape
    r2 = r // 2
    tr = _tile(r2, 256, 16)
    nb = r2 // tr

    def body(g0_ref, g1_ref, r_ref, o32_ref, o16_ref):
        s = jnp.where(lax.axis_index("c") == 0, g0_ref[...], g1_ref[...]) + r_ref[...]
        o32_ref[...] = s
        o16_ref[...] = s.astype(BF16)

    here = pl.BlockSpec((1, tr, c), lambda i, j: (i, j, 0))
    return pl.pallas_call(
        body, name=name, grid=(nch, nb),
        in_specs=[here, pl.BlockSpec((1, tr, c), lambda i, j: (i, nb + j, 0)), here],
        out_specs=[here, here],
        out_shape=[jax.ShapeDtypeStruct((nch, r2, c), F32), jax.ShapeDtypeStruct((nch, r2, c), BF16)],
        compiler_params=_params(("parallel", "parallel")),
    )(g, g, recv)


def grads_rider_chips(arrs):
    def copies(ins, outs, send_sems, recv_sems):
        x, y, c = _coords()
        chips, chip_ids = _ici_peers(x, y)
        pairs = []
        for a in range(len(arrs)):
            for k in range(3):
                to = (*chips[k], c)
                pairs.append((_remote(ins[a].at[chip_ids[k]], outs[a].at[k], send_sems, recv_sems, 3 * a + k, to),
                              _remote(outs[a].at[k], outs[a].at[k], send_sems, recv_sems, 3 * a + k, to)))
        return pairs

    return Rider(list(arrs), [jax.ShapeDtypeStruct((3, *a.shape[1:]), a.dtype) for a in arrs], {}, 3 * len(arrs),
                 copies)


def grad_sum(p32, recv, layer, nl, buf, name):
    _, r2, c = p32.shape
    tr = _tile(r2, 256, 16)
    nb = r2 // tr

    def body(p0_ref, p1_ref, p2_ref, p3_ref, r0_ref, r1_ref, r2_ref, *rest):
        o_ref = rest[-1]
        x, y, _ = _coords()
        chip = 2 * x + y
        own = jnp.where(chip == 0, p0_ref[...], jnp.where(chip == 1, p1_ref[...],
                                                        jnp.where(chip == 2, p2_ref[...], p3_ref[...])))
        o_ref[...] = own + r0_ref[...].astype(F32) + r1_ref[...].astype(F32) + r2_ref[...].astype(F32)

    slot = lambda k: pl.BlockSpec((1, tr, c), lambda j: (k, j, 0))
    in_specs = [slot(k) for k in range(N_CHIPS)] + [slot(k) for k in range(3)]
    args = [p32] * N_CHIPS + [recv] * 3
    aliases = {}
    if buf is not None:
        in_specs.append(pl.BlockSpec(memory_space=pl.ANY))
        args.append(buf)
        aliases = {len(args) - 1: 0}
    return pl.pallas_call(
        body, name=name, grid=(nb,), in_specs=in_specs,
        out_specs=pl.BlockSpec((1, tr, c), lambda j: (layer, j, 0)),
        out_shape=jax.ShapeDtypeStruct((nl, r2, c), F32),
        input_output_aliases=aliases,
        compiler_params=_params(("parallel",)),
    )(*args)


def grads_rider_exchange(bufs):
    def copies(ins, outs, send_sems, recv_sems):
        x, y, c = _coords()
        sibling = (x, y, 1 - c)
        return [(_remote(ins[a], outs[a], send_sems, recv_sems, a, sibling),
                 _remote(outs[a], outs[a], send_sems, recv_sems, a, sibling)) for a in range(len(bufs))]

    return Rider(list(bufs), [jax.ShapeDtypeStruct(b.shape, b.dtype) for b in bufs], {}, len(bufs), copies)


def allreduce_small(buf, name):
    r, cdim = buf.shape

    def body(x_ref, o_ref, gath, send_sems, recv_sems):
        x, y, c = _coords()
        me, sibling = (x, y, c), (x, y, 1 - c)
        chips, _ = _ici_peers(x, y)

        def slot(px, py, pc):
            return gath.at[4 * px + 2 * py + pc]

        def copy(k, block, to, src=None):
            return pltpu.make_async_remote_copy(
                src_ref=slot(*block) if src is None else src, dst_ref=slot(*block),
                send_sem=send_sems.at[k], recv_sem=recv_sems.at[k], device_id=to, device_id_type=MESH)

        gath[4 * x + 2 * y + c] = x_ref[...]
        first = [copy(0, me, sibling, src=x_ref)]
        first += [copy(1 + j, me, (*chip, c), src=x_ref) for j, chip in enumerate(chips)]
        for cp in first:
            cp.start()
        passed = [copy(4 + j, (*chip, c), sibling) for j, chip in enumerate(chips)]
        for j, chip in enumerate(chips):
            copy(1 + j, (*chip, c), me).wait_recv()
            passed[j].start()
        copy(0, sibling, me).wait_recv()
        for j, chip in enumerate(chips):
            copy(4 + j, (*chip, 1 - c), me).wait_recv()
        for cp in first + passed:
            cp.wait_send()
        acc = gath[0]
        for d in range(1, 8):
            acc = acc + gath[d]
        o_ref[...] = acc

    return pl.pallas_call(
        body, name=name,
        in_specs=[pl.BlockSpec(memory_space=pltpu.VMEM)], out_specs=pl.BlockSpec(memory_space=pltpu.VMEM),
        out_shape=jax.ShapeDtypeStruct((r, cdim), F32),
        scratch_shapes=[pltpu.VMEM((8, r, cdim), F32), pltpu.SemaphoreType.DMA((7,)), pltpu.SemaphoreType.DMA((7,))],
        compiler_params=pltpu.CompilerParams(has_side_effects=True),
    )(buf)


def _expand_heads(v):
    return jnp.repeat(v.astype(F32), HEAD_DIM).reshape(1, -1)


def _pad_lanes(v):
    return jnp.pad(v.astype(F32), (0, LANES - v.shape[0])).reshape(1, LANES)


def local_step(x, tgt, p, comm, cols):
    nl = p["norm_mix_w"].shape[0]
    d = x.shape[1]
    cw = p["short_conv_w"].shape[2]
    si = p["ssd_norm_w"].shape[1]
    w0 = comm.weights(0)
    ff = w0["w_up"].shape[2]
    nh = si // HEAD_DIM
    gn = GROUPS * STATE
    npad = w0["w_in"].shape[2]
    dt_off = 3 * cw + si + si + 2 * gn
    assert cols == dt_off + nh and nh <= LANES and dt_off % LANES == 0 and npad == dt_off + LANES
    pieces = [(0, cw), (cw, cw), (2 * cw, cw), (3 * cw, si), (3 * cw + si, si), (3 * cw + 2 * si, 2 * gn),
              (dt_off, LANES)]

    saved = []
    for l in range(nl):
        nw1 = p["norm_mix_w"][l].reshape(1, d)
        nw2 = p["norm_mlp_w"][l].reshape(1, d)
        kw3 = p["short_conv_w"][l]
        kwx, kwb = p["ssd_conv_w"][l][:, :si], p["ssd_conv_w"][l][:, si:]
        bx, bb = p["ssd_conv_b"][l][:si].reshape(1, si), p["ssd_conv_b"][l][si:].reshape(1, 2 * gn)
        dtb = _pad_lanes(p["dt_bias"][l])
        aneg = _pad_lanes(-jnp.exp(p["a_log"][l]))
        dsk = _expand_heads(p["d_skip"][l])
        snw = p["ssd_norm_w"][l].reshape(1, si)
        ssd_args = (kwx, kwb, bx, bb, dtb, aneg, dsk, snw)

        wl = comm.weights(l)
        (proj, h), sent = norm_matmul(x, nw1, wl["w_in"], 0, npad, 0, BF16, "in_proj", rider=comm.forward_rider_a(l))
        (dt_raw,), _ = norm_matmul(x, nw1, wl["w_in"], 0, LANES, dt_off // LANES, F32, "dt_proj", emit_h=False)
        y_mix = conv_mixer_fwd(proj, kw3, cw, cw + si, "conv_mixer_fwd")
        (y_mix, ys, hsave), sent = ssd_fwd(proj, dt_raw, y_mix, *ssd_args, cw, si, "ssd_fwd",
                                           rider=comm.forward_rider_b(l, sent))
        comm.forward_done(l, sent)
        x2, _ = matmul(y_mix, wl["w_out"], 0, False, d, F32, "out_proj", residual=x)
        (up, h2), _ = norm_matmul(x2, nw2, wl["w_up"], 0, ff, 0, BF16, "up_proj")
        x3, _ = matmul(up, wl["w_down"], 0, False, d, F32, "down_proj", lhs_fn=_relu2, residual=x2)
        saved.append((x, h, proj, dt_raw, y_mix, ys, hsave, x2, h2, up, nw1, nw2, kw3, ssd_args))
        x = x3

    dx, dwf, lvec = final_loss(x, p["final_norm_w"].reshape(1, d), tgt, "final_loss")
    loss = jnp.sum(lvec)

    names = ("norm_mix_w", "short_conv_w", "ssd_conv_w", "ssd_conv_b", "dt_bias", "a_log", "d_skip",
             "ssd_norm_w", "norm_mlp_w")
    grads = {k: [None] * nl for k in names}
    for l in reversed(range(nl)):
        x0, h, proj, dt_raw, y_mix, ys, hsave, x2, h2, up, nw1, nw2, kw3, ssd_args = saved[l]
        wl = comm.weights(l)
        dup, sent = matmul(dx, wl["w_down"], 0, True, ff, BF16, "down_bwd", relu_gate=up,
                           rider=comm.backward_rider_a(l))
        comm.backward_a_done(l, sent)
        g_down = matmul_tn(up, dx, "down_wgrad", a_fn=_relu2)
        dx2, dnw2 = matmul_normbwd([dup], [(0, ff)], wl["w_up"], 0, x2, nw2, dx, "up_bwd")
        g_up = matmul_tn(h2, dup, "up_wgrad", by_chip=True)
        grads["norm_mlp_w"][l] = dnw2.reshape(d)
        dy, _ = matmul(dx2, wl["w_out"], 0, True, cw + si, BF16, "out_bwd")
        g_out = matmul_tn(y_mix, dx2, "out_wgrad")
        dub, duc, duh, dkw3 = conv_mixer_bwd(proj, dy, kw3, cw, "conv_mixer_bwd")
        (dz, dxs, dbc, ddt, dkwx, dkwb, dbx, dbb, ddtb, da, ddsk, dsnw), sent = ssd_bwd(
            proj, dt_raw, ys, hsave, dy, *ssd_args, cw, si, "ssd_bwd", rider=comm.backward_rider_b(l))
        comm.backward_b_done(l, sent)
        dpieces = [dub, duc, duh, dz, dxs, dbc, ddt]
        dxl, dnw1 = matmul_normbwd(dpieces, pieces, wl["w_in"], 0, x0, nw1, dx2, "in_bwd")
        g_in = split_to_chips(
            [matmul_tn(h, dp, "in_wgrad_%d" % i) for i, dp in enumerate(dpieces)], cols, "in_wgrad_split")
        comm.take_gradients(l, dict(w_in=g_in, w_out=g_out, w_up=g_up, w_down=g_down))
        grads["norm_mix_w"][l] = dnw1.reshape(d)
        grads["short_conv_w"][l] = dkw3
        grads["ssd_conv_w"][l] = jnp.concatenate([dkwx, dkwb], axis=1)
        grads["ssd_conv_b"][l] = jnp.concatenate([dbx, dbb], axis=1).reshape(-1)
        grads["dt_bias"][l] = ddtb[0, :nh]
        grads["a_log"][l] = da[0, :nh] * ssd_args[5][0, :nh]
        grads["d_skip"][l] = jnp.sum(ddsk.reshape(nh, HEAD_DIM), axis=1)
        grads["ssd_norm_w"][l] = dsnw.reshape(si)
        dx = dxl

    grads = {k: jnp.stack(v) for k, v in grads.items()}
    grads["final_norm_w"] = dwf.reshape(d)
    return loss, dx, grads


BIG = ("w_in", "w_out", "w_up", "w_down")
SMALL_SHARDED = ("short_conv_w", "ssd_conv_w")
SMALL_REPL = ("norm_mix_w", "ssd_conv_b", "dt_bias", "a_log", "d_skip", "ssd_norm_w", "norm_mlp_w", "final_norm_w")
WEIGHTS = ("norm_mix_w", "w_in", "short_conv_w", "ssd_conv_w", "ssd_conv_b", "dt_bias", "a_log", "d_skip",
           "ssd_norm_w", "w_out", "norm_mlp_w", "w_up", "w_down", "final_norm_w")
SMALL_COLS = 1024


def _pack_small(named):
    flat = jnp.concatenate([v.reshape(-1).astype(F32) for v in named])
    n = flat.shape[0]
    rows = -(-n // SMALL_COLS)
    rows = -(-rows // 8) * 8
    return jnp.pad(flat, (0, rows * SMALL_COLS - n)).reshape(rows, SMALL_COLS)


def _unpack_small(buf, like):
    flat = buf.reshape(-1)
    out, off = [], 0
    for v in like:
        out.append(flat[off:off + v.size].reshape(v.shape))
        off += v.size
    return out


class ChipComm:
    def __init__(self, shards, nl, npad):
        self.shards, self.nl, self.npad = shards, nl, npad
        self.shapes = [tuple(s.shape[1:]) for s in shards]
        self.w = {}
        self.pending = None
        self.sums = None
        self.bufs = [None] * len(shards)
        landed = run_rider(weight_rider_ici(shards, 0), "gather_first_ici")
        self.forward_done(-1, run_rider(weight_rider_d2d(landed, self.shapes), "gather_first_d2d"))

    def weights(self, l):
        return self.w[l]

    def forward_rider_a(self, l):
        return weight_rider_ici(self.shards, l + 1) if l + 1 < self.nl else None

    def forward_rider_b(self, l, landed):
        return weight_rider_d2d(landed, self.shapes) if landed else None

    def forward_done(self, l, gathered):
        if gathered:
            w_in = join_from_chips(gathered[0], self.npad, "w_in_join")
            self.w[l + 1] = dict(w_in=w_in, w_out=gathered[1], w_up=gathered[2], w_down=gathered[3])

    def take_gradients(self, l, g):
        by_chip = [g[k] if g[k].ndim == 3 else g[k].reshape(N_CHIPS, g[k].shape[0] // N_CHIPS, g[k].shape[1])
                   for k in BIG]
        self.pending = (l, by_chip)

    def backward_rider_a(self, l):
        return grads_rider_sibling(self.pending[1]) if self.pending else None

    def backward_a_done(self, l, from_sibling):
        if from_sibling:
            self.sums = [chip_sum(g, r, "chip_sum") for g, r in zip(self.pending[1], from_sibling)]

    def backward_rider_b(self, l):
        return grads_rider_chips([s[1] for s in self.sums]) if self.pending else None

    def backward_b_done(self, l, from_chips):
        if from_chips:
            layer = self.pending[0]
            self.bufs = [grad_sum(s[0], r, layer, self.nl, buf, "grad_sum")
                         for s, r, buf in zip(self.sums, from_chips, self.bufs)]
            self.pending = None

    def finish(self):
        self.backward_a_done(None, run_rider(self.backward_rider_a(None), "grads_to_sibling"))
        self.backward_b_done(None, run_rider(self.backward_rider_b(None), "grads_to_chips"))
        theirs = run_rider(grads_rider_exchange(self.bufs), "grads_exchange")
        return dict(zip(BIG, zip(self.bufs, theirs)))


def kernel(x, norm_mix_w, w_in, short_conv_w, ssd_conv_w, ssd_conv_b, dt_bias, a_log, d_skip, ssd_norm_w, w_out, norm_mlp_w, w_up, w_down, final_norm_w, loss_target, m_norm_mix_w, m_w_in, m_short_conv_w, m_ssd_conv_w, m_ssd_conv_b, m_dt_bias, m_a_log, m_d_skip, m_ssd_norm_w, m_w_out, m_norm_mlp_w, m_w_up, m_w_down, m_final_norm_w, v_norm_mix_w, v_w_in, v_short_conv_w, v_ssd_conv_w, v_ssd_conv_b, v_dt_bias, v_a_log, v_d_skip, v_ssd_norm_w, v_w_out, v_norm_mlp_w, v_w_up, v_w_down, v_final_norm_w):
    w = dict(norm_mix_w=norm_mix_w, w_in=w_in, short_conv_w=short_conv_w, ssd_conv_w=ssd_conv_w,
             ssd_conv_b=ssd_conv_b, dt_bias=dt_bias, a_log=a_log, d_skip=d_skip, ssd_norm_w=ssd_norm_w, w_out=w_out,
             norm_mlp_w=norm_mlp_w, w_up=w_up, w_down=w_down, final_norm_w=final_norm_w)
    m = dict(norm_mix_w=m_norm_mix_w, w_in=m_w_in, short_conv_w=m_short_conv_w, ssd_conv_w=m_ssd_conv_w,
             ssd_conv_b=m_ssd_conv_b, dt_bias=m_dt_bias, a_log=m_a_log, d_skip=m_d_skip, ssd_norm_w=m_ssd_norm_w,
             w_out=m_w_out, norm_mlp_w=m_norm_mlp_w, w_up=m_w_up, w_down=m_w_down, final_norm_w=m_final_norm_w)
    v = dict(norm_mix_w=v_norm_mix_w, w_in=v_w_in, short_conv_w=v_short_conv_w, ssd_conv_w=v_ssd_conv_w,
             ssd_conv_b=v_ssd_conv_b, dt_bias=v_dt_bias, a_log=v_a_log, d_skip=v_d_skip, ssd_norm_w=v_ssd_norm_w,
             w_out=v_w_out, norm_mlp_w=v_norm_mlp_w, w_up=v_w_up, w_down=v_w_down, final_norm_w=v_final_norm_w)
    xi, yi, ci = lax.axis_index("x"), lax.axis_index("y"), lax.axis_index("c")
    chip = 2 * xi + yi
    nl = w_up.shape[0]
    cols = N_CHIPS * w_in.shape[2]
    npad = cols // LANES * LANES + LANES

    full = dict(w)
    small_gathered = gather_weights([w[k] for k in SMALL_SHARDED], ["lead"] * len(SMALL_SHARDED), "gather_small")
    for k, g4 in zip(SMALL_SHARDED, small_gathered):
        full[k] = jnp.concatenate([g4[j] for j in range(N_CHIPS)], axis=2)
    comm = ChipComm([w[k].astype(BF16) for k in BIG], nl, npad)

    loss, grad_x, grads = local_step(x[0], loss_target[0], full, comm, cols)
    loss = lax.psum(loss, ("x", "y", "c"))
    halves = comm.finish()
    g_shard = {}

    small_names = SMALL_REPL + SMALL_SHARDED
    small_sum = allreduce_small(_pack_small([grads[k] for k in small_names]), "allreduce_small")
    for k, g in zip(small_names, _unpack_small(small_sum, [grads[k] for k in small_names])):
        if k in SMALL_SHARDED:
            width = w[k].shape[2]
            g = lax.dynamic_slice_in_dim(g, chip * width, width, axis=2)
        g_shard[k] = g

    delta, new_m, new_v = {}, {}, {}
    for k in BIG:
        g_shard[k], delta[k], new_m[k], new_v[k] = adamw_halves(w[k], *halves[k], m[k], v[k], "adamw_%s" % k)
    packed = [_pack_small([d_[k] for k in small_names]) for d_ in (w, g_shard, m, v)]
    outs = adamw(*packed, "adamw_small")
    for d_, buf in zip((delta, new_m, new_v), outs):
        for k, val in zip(small_names, _unpack_small(buf, [w[k] for k in small_names])):
            d_[k] = val

    return (loss, grad_x[None], *[g_shard[k] for k in WEIGHTS], *[delta[k] for k in WEIGHTS],
            *[new_m[k] for k in WEIGHTS], *[new_v[k] for k in WEIGHTS])
```

```python
import functools

import jax
import jax.numpy as jnp
from jax import lax
from jax.experimental import pallas as pl
from jax.experimental.pallas import tpu as pltpu

F32 = jnp.float32
BF16 = jnp.bfloat16

EPS = 1e-5
HEAD_DIM = 64
STATE = 128
GROUPS = 2
SHORT_K = 3
SSD_K = 4
LANES = 128
PAIR = LANES // HEAD_DIM
SCAN_CHUNK = 256
HALO = 16
N_CHIPS = 4
VMEM_LIMIT = 56 * 1024 * 1024

ADAM_LR = 0.001
ADAM_B1 = 0.9
ADAM_B2 = 0.999
ADAM_EPS = 1e-08
ADAM_WD = 0.01
ADAM_STEP = 10

MESH = pl.DeviceIdType.MESH


def _params(sem):
    return pltpu.CompilerParams(dimension_semantics=sem, vmem_limit_bytes=VMEM_LIMIT)


def _tile(n, cap, quantum):
    if n <= cap:
        return n
    best = None
    for t in range(quantum, cap + 1, quantum):
        if n % t == 0:
            best = t
    assert best is not None, (n, cap, quantum)
    return best


def _dot(a, b):
    return jnp.dot(a, b, preferred_element_type=F32)


def _dot_nt(a, b):
    return lax.dot_general(a, b, (((1,), (1,)), ((), ())), preferred_element_type=F32)


def _dot_tn(a, b):
    return lax.dot_general(a, b, (((0,), (0,)), ((), ())), preferred_element_type=F32)


def _dot_exact(a, b):
    return jnp.dot(a, b, precision=lax.Precision.HIGHEST, preferred_element_type=F32)


def _sigmoid(x):
    return pl.reciprocal(1.0 + jnp.exp(-x), approx=True)


def _softplus(x):
    return jnp.maximum(x, 0.0) + jnp.log(1.0 + jnp.exp(-jnp.abs(x)))


def _relu2(v):
    return jnp.square(jnp.maximum(v, 0.0))


class Rider:
    def __init__(self, ins, out_shapes, aliases, n_sems, copies):
        self.ins, self.out_shapes, self.aliases, self.n_sems, self.copies = ins, out_shapes, aliases, n_sems, copies


def _any_specs(n):
    return [pl.BlockSpec(memory_space=pl.ANY)] * n


def _ride(body, grid, in_specs, out_specs, out_shape, scratch, args, aliases, rider, sem, name):
    n_in, n_out, n_scr = len(in_specs), len(out_specs), len(scratch)
    if rider is None:
        outs = pl.pallas_call(
            body, name=name, grid=grid, in_specs=in_specs, out_specs=out_specs, out_shape=out_shape,
            scratch_shapes=scratch, input_output_aliases=aliases, compiler_params=_params(sem))(*args)
        return list(outs), []
    ri, ro = len(rider.ins), len(rider.out_shapes)
    last = tuple(g - 1 for g in grid)

    def wrapped(*refs):
        ins = refs[:n_in]
        r_ins = refs[n_in:n_in + ri]
        outs = refs[n_in + ri:n_in + ri + n_out]
        r_outs = refs[n_in + ri + n_out:n_in + ri + n_out + ro]
        scr = refs[n_in + ri + n_out + ro:n_in + ri + n_out + ro + n_scr]
        send_sems, recv_sems = refs[-2:]
        ids = [pl.program_id(a) for a in range(len(grid))]
        at_first = functools.reduce(jnp.logical_and, [i == 0 for i in ids])
        at_last = functools.reduce(jnp.logical_and, [i == e for i, e in zip(ids, last)])

        @pl.when(at_first)
        def _():
            for cp, _ in rider.copies(r_ins, r_outs, send_sems, recv_sems):
                cp.start()

        body(*ins, *outs, *scr)

        @pl.when(at_last)
        def _():
            for cp, landed in rider.copies(r_ins, r_outs, send_sems, recv_sems):
                cp.wait_send()
                landed.wait_recv()

    all_aliases = dict(aliases)
    all_aliases.update({n_in + a: n_out + b for a, b in rider.aliases.items()})
    outs = pl.pallas_call(
        wrapped, name=name, grid=grid, in_specs=list(in_specs) + _any_specs(ri),
        out_specs=list(out_specs) + _any_specs(ro), out_shape=list(out_shape) + list(rider.out_shapes),
        scratch_shapes=list(scratch) + [pltpu.SemaphoreType.DMA((rider.n_sems,)),
                                        pltpu.SemaphoreType.DMA((rider.n_sems,))],
        input_output_aliases=all_aliases, compiler_params=_params(sem))(*args, *rider.ins)
    return list(outs[:n_out]), list(outs[n_out:])


def run_rider(rider, name):
    ri, ro = len(rider.ins), len(rider.out_shapes)

    def body(*refs):
        send_sems, recv_sems = refs[-2:]
        pairs = rider.copies(refs[:ri], refs[ri:ri + ro], send_sems, recv_sems)
        for cp, _ in pairs:
            cp.start()
        for cp, landed in pairs:
            cp.wait_send()
            landed.wait_recv()

    return list(pl.pallas_call(
        body, name=name, in_specs=_any_specs(ri), out_specs=_any_specs(ro), out_shape=list(rider.out_shapes),
        scratch_shapes=[pltpu.SemaphoreType.DMA((rider.n_sems,)), pltpu.SemaphoreType.DMA((rider.n_sems,))],
        input_output_aliases=dict(rider.aliases),
        compiler_params=pltpu.CompilerParams(has_side_effects=True))(*rider.ins))


def norm_matmul(x, nw, w, layer, n, col0, out_dtype, name, emit_h=True, rider=None):
    t, d = x.shape
    tm = _tile(t, 1024, 8)
    tn = _tile(n, 1536, LANES)
    nj = n // tn

    def body(x_ref, nw_ref, w_ref, o_ref, h_ref):
        @pl.when(pl.program_id(1) == 0)
        def _():
            xf = x_ref[...]
            r = lax.rsqrt(jnp.mean(xf * xf, axis=-1, keepdims=True) + EPS)
            h_ref[...] = (xf * r * nw_ref[...]).astype(BF16)

        o_ref[...] = _dot(h_ref[...], w_ref[...]).astype(out_dtype)

    out_specs = [pl.BlockSpec((tm, tn), lambda i, j: (i, j))]
    out_shape = [jax.ShapeDtypeStruct((t, n), out_dtype)]
    if emit_h:
        out_specs.append(pl.BlockSpec((tm, d), lambda i, j: (i, 0)))
        out_shape.append(jax.ShapeDtypeStruct((t, d), BF16))
    return _ride(
        body, (t // tm, nj),
        [pl.BlockSpec((tm, d), lambda i, j: (i, 0)), pl.BlockSpec((1, d), lambda i, j: (0, 0)),
         pl.BlockSpec((None, d, tn), lambda i, j: (layer, 0, col0 * nj + j))],
        out_specs, out_shape, [] if emit_h else [pltpu.VMEM((tm, d), BF16)], [x, nw, w], {}, rider,
        ("parallel", "arbitrary"), name)


def matmul(lhs, w, layer, transposed, n, out_dtype, name, *, lhs_fn=None, residual=None, relu_gate=None,
           rider=None):
    t, k = lhs.shape
    big = k > 2048
    tm = _tile(t, 512 if big else 1024, 8)
    tn = _tile(n, 512 if big else 1024, LANES)
    staged = lhs.dtype != BF16 or lhs_fn is not None
    fn = lhs_fn if lhs_fn is not None else (lambda v: v)
    has_extra = residual is not None or relu_gate is not None
    dot = _dot_nt if transposed else _dot

    def body(*refs):
        a_ref, w_ref = refs[:2]
        extra = refs[2] if has_extra else None
        o_ref = refs[3] if has_extra else refs[2]
        if staged:
            s_ref = refs[-1]

            @pl.when(pl.program_id(1) == 0)
            def _():
                s_ref[...] = fn(a_ref[...].astype(F32)).astype(BF16)

            a_ref = s_ref
        acc = dot(a_ref[...], w_ref[...])
        if residual is not None:
            acc = acc + extra[...]
        if relu_gate is not None:
            acc = acc * (2.0 * jnp.maximum(extra[...].astype(F32), 0.0))
        o_ref[...] = acc.astype(out_dtype)

    if transposed:
        w_spec = pl.BlockSpec((None, tn, k), lambda i, j: (layer, j, 0))
    else:
        w_spec = pl.BlockSpec((None, k, tn), lambda i, j: (layer, 0, j))
    in_specs = [pl.BlockSpec((tm, k), lambda i, j: (i, 0)), w_spec]
    args = [lhs, w]
    if has_extra:
        in_specs.append(pl.BlockSpec((tm, tn), lambda i, j: (i, j)))
        args.append(residual if residual is not None else relu_gate)
    outs, extra = _ride(
        body, (t // tm, n // tn), in_specs, [pl.BlockSpec((tm, tn), lambda i, j: (i, j))],
        [jax.ShapeDtypeStruct((t, n), out_dtype)], [pltpu.VMEM((tm, k), BF16)] if staged else [], args, {},
        rider, ("parallel", "arbitrary"), name)
    return outs[0], extra


def matmul_normbwd(lhs, pieces, w, layer, x, nw, dres, name):
    t, d = x.shape
    nl = len(lhs)
    tm = _tile(t, 256, 8)
    for off, width in pieces:
        assert off % width == 0

    def body(*refs):
        lrefs = refs[:nl]
        wrefs = refs[nl:2 * nl]
        x_ref, nw_ref, dres_ref, dx_ref, dnw_ref = refs[2 * nl:]
        dh = _dot_nt(lrefs[0][...].astype(BF16), wrefs[0][...])
        for a_ref, w_ref in zip(lrefs[1:], wrefs[1:]):
            dh = dh + _dot_nt(a_ref[...].astype(BF16), w_ref[...])
        xf = x_ref[...]
        r = lax.rsqrt(jnp.mean(xf * xf, axis=-1, keepdims=True) + EPS)
        nx = xf * r
        dn = dh * nw_ref[...]
        dx = r * (dn - nx * jnp.mean(dn * nx, axis=-1, keepdims=True))
        dx_ref[...] = dres_ref[...] + dx

        @pl.when(pl.program_id(0) == 0)
        def _():
            dnw_ref[...] = jnp.zeros_like(dnw_ref)

        dnw_ref[...] += jnp.sum(dh * nx, axis=0, keepdims=True)

    in_specs = [pl.BlockSpec((tm, width), lambda i: (i, 0)) for _, width in pieces]
    in_specs += [pl.BlockSpec((None, d, width), (lambda blk: (lambda i: (layer, 0, blk)))(off // width))
                 for off, width in pieces]
    in_specs += [pl.BlockSpec((tm, d), lambda i: (i, 0)), pl.BlockSpec((1, d), lambda i: (0, 0)),
                 pl.BlockSpec((tm, d), lambda i: (i, 0))]
    return pl.pallas_call(
        body, name=name, grid=(t // tm,), in_specs=in_specs,
        out_specs=[pl.BlockSpec((tm, d), lambda i: (i, 0)), pl.BlockSpec((1, d), lambda i: (0, 0))],
        out_shape=[jax.ShapeDtypeStruct((t, d), F32), jax.ShapeDtypeStruct((1, d), F32)],
        compiler_params=_params(("arbitrary",)),
    )(*lhs, *([w] * nl), x, nw, dres)


def matmul_tn(a, b, name, *, a_fn=None, by_chip=False):
    t, k = a.shape
    n = b.shape[1]
    tk = _tile(k, 1024, LANES)
    tn = _tile(n // N_CHIPS if by_chip else n, 1024, LANES)
    tt = _tile(t, 1024, 8)
    nt = t // tt
    fn = a_fn if a_fn is not None else (lambda v: v)

    def body(a_ref, b_ref, o_ref, acc_ref):
        @pl.when(pl.program_id(2) == 0)
        def _():
            acc_ref[...] = jnp.zeros_like(acc_ref)

        av = a_ref[...]
        if a_fn is not None:
            av = fn(av.astype(F32))
        acc_ref[...] += _dot_tn(av.astype(BF16), b_ref[...].astype(BF16))

        @pl.when(pl.program_id(2) == nt - 1)
        def _():
            o_ref[...] = acc_ref[...]

    if by_chip:
        per = n // N_CHIPS // tn
        out_spec = pl.BlockSpec((None, tk, tn), lambda i, j, s: (j // per, i, j % per))
        out_shape = jax.ShapeDtypeStruct((N_CHIPS, k, n // N_CHIPS), F32)
    else:
        out_spec = pl.BlockSpec((tk, tn), lambda i, j, s: (i, j))
        out_shape = jax.ShapeDtypeStruct((k, n), F32)
    return pl.pallas_call(
        body, name=name, grid=(k // tk, n // tn, nt),
        in_specs=[pl.BlockSpec((tt, tk), lambda i, j, s: (s, i)),
                  pl.BlockSpec((tt, tn), lambda i, j, s: (s, j))],
        out_specs=out_spec, out_shape=out_shape,
        scratch_shapes=[pltpu.VMEM((tk, tn), F32)],
        compiler_params=_params(("parallel", "parallel", "arbitrary")),
    )(a, b)


def split_to_chips(pieces, cols, name):
    d = pieces[0].shape[0]
    widths = [p.shape[1] for p in pieces]
    w = cols // N_CHIPS
    tr = _tile(d, 256, 8)
    npc = len(pieces)

    def body(*refs):
        o_ref, row = refs[npc], refs[npc + 1]
        off = 0
        for r, n in zip(refs[:npc], widths):
            row[:, off:off + n] = r[...]
            off += n
        for j in range(N_CHIPS):
            o_ref[j] = row[:, j * w:(j + 1) * w]

    return pl.pallas_call(
        body, name=name, grid=(d // tr,),
        in_specs=[pl.BlockSpec((tr, n), lambda i: (i, 0)) for n in widths],
        out_specs=pl.BlockSpec((N_CHIPS, tr, w), lambda i: (0, i, 0)),
        out_shape=jax.ShapeDtypeStruct((N_CHIPS, d, w), F32),
        scratch_shapes=[pltpu.VMEM((tr, sum(widths)), F32)],
        compiler_params=_params(("parallel",)),
    )(*pieces)


def join_from_chips(g4, npad, name):
    _, nl, d, w = g4.shape
    tr = _tile(d, 256, HALO)

    def body(g_ref, o_ref):
        for j in range(N_CHIPS):
            o_ref[:, j * w:(j + 1) * w] = g_ref[j]
        o_ref[:, N_CHIPS * w:] = jnp.zeros((tr, npad - N_CHIPS * w), o_ref.dtype)

    return pl.pallas_call(
        body, name=name, grid=(nl, d // tr),
        in_specs=[pl.BlockSpec((N_CHIPS, None, tr, w), lambda l, i: (0, l, i, 0))],
        out_specs=pl.BlockSpec((None, tr, npad), lambda l, i: (l, i, 0)),
        out_shape=jax.ShapeDtypeStruct((nl, d, npad), g4.dtype),
        compiler_params=_params(("parallel", "parallel")),
    )(g4)


def conv_mixer_fwd(proj, kw, cw, out_cols, name):
    t = proj.shape[0]
    tm = _tile(t, 512, HALO)
    tc = _tile(cw, 512, LANES)
    nj = cw // tc
    hb = tm // HALO

    def body(ub_ref, uc_ref, uh_ref, ucp_ref, uhp_ref, kw_ref, y_ref, ext):
        i = pl.program_id(0)
        v = uc_ref[...].astype(F32) * uh_ref[...].astype(F32)
        vp = ucp_ref[...].astype(F32) * uhp_ref[...].astype(F32)
        ext[0:HALO, :] = jnp.where(i > 0, vp, 0.0)
        ext[HALO:HALO + tm, :] = v
        cv = kw_ref[pl.ds(SHORT_K - 1, 1), :] * v
        for k in range(SHORT_K - 1):
            cv = cv + kw_ref[pl.ds(k, 1), :] * ext[pl.ds(HALO - (SHORT_K - 1) + k, tm), :]
        y_ref[...] = (ub_ref[...].astype(F32) * cv).astype(BF16)

    prev = lambda off: (lambda i, j: (jnp.maximum(i * hb - 1, 0), off + j))
    return pl.pallas_call(
        body, name=name, grid=(t // tm, nj),
        in_specs=[pl.BlockSpec((tm, tc), lambda i, j: (i, j)),
                  pl.BlockSpec((tm, tc), lambda i, j: (i, nj + j)),
                  pl.BlockSpec((tm, tc), lambda i, j: (i, 2 * nj + j)),
                  pl.BlockSpec((HALO, tc), prev(nj)),
                  pl.BlockSpec((HALO, tc), prev(2 * nj)),
                  pl.BlockSpec((SHORT_K, tc), lambda i, j: (0, j))],
        out_specs=pl.BlockSpec((tm, tc), lambda i, j: (i, j)),
        out_shape=jax.ShapeDtypeStruct((t, out_cols), BF16),
        scratch_shapes=[pltpu.VMEM((tm + HALO, tc), F32)],
        compiler_params=_params(("parallel", "parallel")),
    )(proj, proj, proj, proj, proj, kw)


def conv_mixer_bwd(proj, dy, kw, cw, name):
    t = proj.shape[0]
    tm = _tile(t, 512, HALO)
    tc = _tile(cw, 512, LANES)
    nj = cw // tc
    hb = tm // HALO
    ni = t // tm
    last_hb = t // HALO - 1

    def body(ub_ref, uc_ref, uh_ref, dy_ref, ucp_ref, uhp_ref, ubn_ref, dyn_ref, kw_ref,
             dub_ref, duc_ref, duh_ref, dkw_ref, ext, extd):
        i = pl.program_id(1)
        ub = ub_ref[...].astype(F32)
        uc = uc_ref[...].astype(F32)
        uh = uh_ref[...].astype(F32)
        dyv = dy_ref[...].astype(F32)
        v = uc * uh
        vp = ucp_ref[...].astype(F32) * uhp_ref[...].astype(F32)
        ext[0:HALO, :] = jnp.where(i > 0, vp, 0.0)
        ext[HALO:HALO + tm, :] = v
        cv = kw_ref[pl.ds(SHORT_K - 1, 1), :] * v
        for k in range(SHORT_K - 1):
            cv = cv + kw_ref[pl.ds(k, 1), :] * ext[pl.ds(HALO - (SHORT_K - 1) + k, tm), :]
        dcv = dyv * ub
        dcvn = dyn_ref[...].astype(F32) * ubn_ref[...].astype(F32)
        extd[0:tm, :] = dcv
        extd[tm:tm + HALO, :] = jnp.where(i < ni - 1, dcvn, 0.0)
        dv = kw_ref[pl.ds(SHORT_K - 1, 1), :] * dcv
        for k in range(SHORT_K - 1):
            dv = dv + kw_ref[pl.ds(k, 1), :] * extd[pl.ds(SHORT_K - 1 - k, tm), :]
        dub_ref[...] = (dyv * cv).astype(BF16)
        duc_ref[...] = (dv * uh).astype(BF16)
        duh_ref[...] = (dv * uc).astype(BF16)

        @pl.when(i == 0)
        def _():
            dkw_ref[...] = jnp.zeros_like(dkw_ref)

        for k in range(SHORT_K):
            sh = ext[pl.ds(HALO - (SHORT_K - 1) + k, tm), :]
            dkw_ref[pl.ds(k, 1), :] += jnp.sum(dcv * sh, axis=0, keepdims=True)

    prev = lambda off: (lambda j, i: (jnp.maximum(i * hb - 1, 0), off + j))
    nxt = lambda off: (lambda j, i: (jnp.minimum((i + 1) * hb, last_hb), off + j))
    cur = lambda off: (lambda j, i: (i, off + j))
    return pl.pallas_call(
        body, name=name, grid=(nj, ni),
        in_specs=[pl.BlockSpec((tm, tc), cur(0)), pl.BlockSpec((tm, tc), cur(nj)),
                  pl.BlockSpec((tm, tc), cur(2 * nj)), pl.BlockSpec((tm, tc), cur(0)),
                  pl.BlockSpec((HALO, tc), prev(nj)), pl.BlockSpec((HALO, tc), prev(2 * nj)),
                  pl.BlockSpec((HALO, tc), nxt(0)), pl.BlockSpec((HALO, tc), nxt(0)),
                  pl.BlockSpec((SHORT_K, tc), lambda j, i: (0, j))],
        out_specs=[pl.BlockSpec((tm, tc), cur(0)), pl.BlockSpec((tm, tc), cur(0)),
                   pl.BlockSpec((tm, tc), cur(0)), pl.BlockSpec((SHORT_K, tc), lambda j, i: (0, j))],
        out_shape=[jax.ShapeDtypeStruct((t, cw), BF16)] * 3 + [jax.ShapeDtypeStruct((SHORT_K, cw), F32)],
        scratch_shapes=[pltpu.VMEM((tm + HALO, tc), F32), pltpu.VMEM((tm + HALO, tc), F32)],
        compiler_params=_params(("parallel", "arbitrary")),
    )(proj, proj, proj, dy, proj, proj, proj, dy, kw)


def _head_column(mat, lane, h):
    return jnp.sum(jnp.where(lane == h, mat, 0.0), axis=-1, keepdims=True)


def _ssd_common(dt_raw_ref, dtb_ref, aneg_ref, cum_s, cumt_s, chunk):
    dt = _softplus(dt_raw_ref[...] + dtb_ref[...])
    al = dt * aneg_ref[...]
    ri = lax.broadcasted_iota(jnp.int32, (chunk, chunk), 0)
    ci = lax.broadcasted_iota(jnp.int32, (chunk, chunk), 1)
    cum = _dot_exact((ri >= ci).astype(F32), al)
    cum_s[...] = cum
    cumt_s[...] = cum.T
    return dt, cum, ri >= ci


EDGE = 16


def _shift_matrices(shift_s, chunk, kk, up):
    ri = lax.broadcasted_iota(jnp.int32, (chunk, chunk), 0)
    ci = lax.broadcasted_iota(jnp.int32, (chunk, chunk), 1)
    for k in range(kk - 1):
        s = kk - 1 - k
        shift_s[k] = ((ci - ri if up else ri - ci) == s).astype(BF16)


def _causal_conv(cur, head, kw_ref, b_ref, shift_s, kk):
    acc = b_ref[...] + kw_ref[pl.ds(kk - 1, 1), :] * cur.astype(F32)
    top = b_ref[...] + kw_ref[pl.ds(kk - 1, 1), :] * head[pl.ds(8, EDGE), :]
    for k in range(kk - 1):
        acc = acc + kw_ref[pl.ds(k, 1), :] * _dot(shift_s[k], cur)
        top = top + kw_ref[pl.ds(k, 1), :] * head[pl.ds(8 - (kk - 1) + k, EDGE), :]
    return acc, top


def ssd_fwd(proj, dt_raw, y_mix, kw_xs, kw_bc, b_xs, b_bc, dtb, aneg, dskip, normw, cw, si, name, rider=None):
    t = proj.shape[0]
    ch = min(SCAN_CHUNK, t)
    nc = t // ch
    npair = si // LANES
    ppg = npair // GROUPS
    gn = GROUPS * STATE
    gw = si // GROUPS
    assert cw == si and (3 * cw + 2 * si) % (2 * gn) == 0
    zblk = 3 * cw // si
    xsblk = zblk + 1
    bcblk = (3 * cw + 2 * si) // (2 * gn)

    def body(z_ref, xs_ref, bc_ref, dtr_ref, ymix_ref, kwx_ref, kwb_ref, bx_ref, bb_ref, dtb_ref, aneg_ref, dsk_ref,
             nw_ref, yb_ref, ys_ref, hs_ref, xcx_ref, xcb_ref,
             headx, headb, shift_s, xs_s, bc_s, h_s, gated_s, s_s, cum_s, cumt_s):
        del ymix_ref
        c = pl.program_id(0)

        @pl.when(c == 0)
        def _():
            h_s[...] = jnp.zeros_like(h_s)
            headx[0:8, :] = jnp.zeros((8, si), F32)
            headb[0:8, :] = jnp.zeros((8, 2 * gn), F32)
            _shift_matrices(shift_s, ch, SSD_K, up=False)

        for raw_ref, head, kw_ref, b_ref, pre_ref, act_s in ((xs_ref, headx, kwx_ref, bx_ref, xcx_ref, xs_s),
                                                           (bc_ref, headb, kwb_ref, bb_ref, xcb_ref, bc_s)):
            head[8:8 + EDGE, :] = raw_ref[0:EDGE, :].astype(F32)
            pre, top = _causal_conv(raw_ref[...], head, kw_ref, b_ref, shift_s, SSD_K)
            head[0:8, :] = raw_ref[ch - EDGE:ch, :].astype(F32)[EDGE - 8:EDGE]
            pre_ref[...] = pre.astype(BF16)
            pre_ref[0:EDGE, :] = top.astype(BF16)
            act_s[...] = (pre * _sigmoid(pre)).astype(act_s.dtype)
            act_s[0:EDGE, :] = (top * _sigmoid(top)).astype(act_s.dtype)

        dt, cum, tril = _ssd_common(dtr_ref, dtb_ref, aneg_ref, cum_s, cumt_s, ch)
        lane = lax.broadcasted_iota(jnp.int32, (ch, LANES), 1)
        lane1 = lax.broadcasted_iota(jnp.int32, (1, LANES), 1)
        low = lane < HEAD_DIM
        clast = cum_s[pl.ds(ch - 1, 1), :]

        for p in range(npair):
            g = p // ppg
            col = slice(p * LANES, (p + 1) * LANES)
            bg = bc_s[:, g * STATE:(g + 1) * STATE]
            cg = bc_s[:, gn + g * STATE:gn + (g + 1) * STATE]
            if p % ppg == 0:
                s_s[...] = _dot_nt(cg, bg)
            heads = (PAIR * p, PAIR * p + 1)
            ccol = [_head_column(cum, lane, h) for h in heads]
            dcol = [_head_column(dt, lane, h) for h in heads]
            cl = [jnp.sum(jnp.where(lane1 == h, clast, 0.0), axis=-1, keepdims=True) for h in heads]
            cum_px = jnp.where(low, ccol[0], ccol[1])
            dt_px = jnp.where(low, dcol[0], dcol[1])
            cl_px = jnp.where(lane1 < HEAD_DIM, cl[0], cl[1])
            xs_p = xs_s[:, col]
            xdt = xs_p * dt_px
            y = dsk_ref[:, col] * xs_p
            for hi, h in enumerate(heads):
                dec = jnp.exp(jnp.where(tril, ccol[hi] - cumt_s[pl.ds(h, 1), :], -jnp.inf))
                wm = (s_s[...] * dec).astype(BF16)
                xm = jnp.where(low if hi == 0 else jnp.logical_not(low), xdt, 0.0).astype(BF16)
                y = y + _dot(wm, xm)
            hp = h_s[p]
            hs_ref[0, p] = hp
            y = y + _dot(cg, hp.astype(BF16)) * jnp.exp(cum_px)
            st = _dot_tn(bg, (xdt * jnp.exp(cl_px - cum_px)).astype(BF16))
            h_s[p] = jnp.exp(cl_px) * hp + st
            ys_ref[:, col] = y.astype(BF16)
            zp = z_ref[:, col].astype(F32)
            gated_s[:, col] = y * zp * _sigmoid(zp)

        for g in range(GROUPS):
            col = slice(g * gw, (g + 1) * gw)
            gg = gated_s[:, col]
            r = lax.rsqrt(jnp.mean(gg * gg, axis=-1, keepdims=True) + EPS)
            yb_ref[:, col] = (gg * r * nw_ref[:, col]).astype(BF16)

    full = lambda shape: pl.BlockSpec(shape, lambda c: tuple(0 for _ in shape))
    return _ride(
        body, (nc,),
        [pl.BlockSpec((ch, si), lambda c: (c, zblk)),
         pl.BlockSpec((ch, si), lambda c: (c, xsblk)),
         pl.BlockSpec((ch, 2 * gn), lambda c: (c, bcblk)),
         pl.BlockSpec((ch, LANES), lambda c: (c, 0)),
         pl.BlockSpec(memory_space=pl.ANY),
         full((SSD_K, si)), full((SSD_K, 2 * gn)), full((1, si)), full((1, 2 * gn)),
         full((1, LANES)), full((1, LANES)), full((1, si)), full((1, si))],
        [pl.BlockSpec((ch, si), lambda c: (c, cw // si)),
         pl.BlockSpec((ch, si), lambda c: (c, 0)),
         pl.BlockSpec((1, npair, STATE, LANES), lambda c: (c, 0, 0, 0)),
         pl.BlockSpec((ch, si), lambda c: (c, 0)), pl.BlockSpec((ch, 2 * gn), lambda c: (c, 0))],
        [jax.ShapeDtypeStruct(y_mix.shape, BF16), jax.ShapeDtypeStruct((t, si), BF16),
         jax.ShapeDtypeStruct((nc, npair, STATE, LANES), F32),
         jax.ShapeDtypeStruct((t, si), BF16), jax.ShapeDtypeStruct((t, 2 * gn), BF16)],
        [pltpu.VMEM((8 + EDGE, si), F32), pltpu.VMEM((8 + EDGE, 2 * gn), F32),
         pltpu.VMEM((SSD_K - 1, ch, ch), BF16),
         pltpu.VMEM((ch, si), F32), pltpu.VMEM((ch, 2 * gn), BF16),
         pltpu.VMEM((npair, STATE, LANES), F32), pltpu.VMEM((ch, si), F32),
         pltpu.VMEM((ch, ch), F32), pltpu.VMEM((ch, LANES), F32), pltpu.VMEM((LANES, ch), F32)],
        [proj, proj, proj, dt_raw, y_mix, kw_xs, kw_bc, b_xs, b_bc, dtb, aneg, dskip, normw], {4: 0}, rider,
        ("arbitrary",), name)


def ssd_bwd(proj, dt_raw, ys, hsave, pre_xs, pre_bc, dy, kw_xs, kw_bc, dtb, aneg, dskip, normw, cw, si, name,
            rider=None):
    t = proj.shape[0]
    ch = min(SCAN_CHUNK, t)
    nc = t // ch
    npair = si // LANES
    ppg = npair // GROUPS
    gn = GROUPS * STATE
    gw = si // GROUPS
    zblk = 3 * cw // si
    xsblk = zblk + 1
    bcblk = (3 * cw + 2 * si) // (2 * gn)

    def body(z_ref, xs_ref, bc_ref, xcx_ref, xcb_ref, dtr_ref, ys_ref, hs_ref, dyb_ref,
             kwx_ref, kwb_ref, dtb_ref, aneg_ref, dsk_ref, nw_ref,
             dz_ref, dxs_ref, dbc_ref, ddt_ref, dkwx_ref, dkwb_ref, dbx_ref, dbb_ref, ddtb_ref, da_ref, ddsk_ref,
             dnw_ref,
             tailx, tailb, shift_s, xs_s, bc_s, dsx_s, dsb_s, dy_s, dxs_s, dbc_s, dh_s, s_s, ds_s,
             cum_s, cumt_s, dccol_s, dcrow_s, ddtcol_s, dcl_s):
        i = pl.program_id(0)

        @pl.when(i == 0)
        def _():
            dh_s[...] = jnp.zeros_like(dh_s)
            tailx[EDGE:EDGE + 8, :] = jnp.zeros((8, si), F32)
            tailb[EDGE:EDGE + 8, :] = jnp.zeros((8, 2 * gn), F32)
            _shift_matrices(shift_s, ch, SSD_K, up=True)
            for r in (dkwx_ref, dkwb_ref, dbx_ref, dbb_ref, ddtb_ref, da_ref, ddsk_ref, dnw_ref):
                r[...] = jnp.zeros_like(r)

        xc = xcx_ref[...].astype(F32)
        sg = _sigmoid(xc)
        xs_s[...] = xc * sg
        dsx_s[...] = sg * (1.0 + xc * (1.0 - sg))
        bcc = xcb_ref[...].astype(F32)
        sgb = _sigmoid(bcc)
        bc_s[...] = (bcc * sgb).astype(BF16)
        dsb_s[...] = sgb * (1.0 + bcc * (1.0 - sgb))

        dt, cum, tril = _ssd_common(dtr_ref, dtb_ref, aneg_ref, cum_s, cumt_s, ch)
        lane = lax.broadcasted_iota(jnp.int32, (ch, LANES), 1)
        lane1 = lax.broadcasted_iota(jnp.int32, (1, LANES), 1)
        low = lane < HEAD_DIM
        low1 = lane1 < HEAD_DIM
        clast = cum_s[pl.ds(ch - 1, 1), :]

        for g in range(GROUPS):
            col = slice(g * gw, (g + 1) * gw)
            ysf = ys_ref[:, col].astype(F32)
            zf = z_ref[:, col].astype(F32)
            sz = _sigmoid(zf)
            silz = zf * sz
            gg = ysf * silz
            r = lax.rsqrt(jnp.mean(gg * gg, axis=-1, keepdims=True) + EPS)
            nrm = gg * r
            dyb = dyb_ref[:, col].astype(F32)
            dnw_ref[:, col] += jnp.sum(dyb * nrm, axis=0, keepdims=True)
            dn = dyb * nw_ref[:, col]
            dgg = r * (dn - nrm * jnp.mean(dn * nrm, axis=-1, keepdims=True))
            dy_s[:, col] = dgg * silz
            dz_ref[:, col] = (dgg * ysf * (sz * (1.0 + zf * (1.0 - sz)))).astype(BF16)

        dccol_s[...] = jnp.zeros_like(dccol_s)
        dcrow_s[...] = jnp.zeros_like(dcrow_s)
        ddtcol_s[...] = jnp.zeros_like(ddtcol_s)
        dcl_s[...] = jnp.zeros_like(dcl_s)
        dbc_s[...] = jnp.zeros_like(dbc_s)

        for p in range(npair):
            g = p // ppg
            col = slice(p * LANES, (p + 1) * LANES)
            bcol = slice(g * STATE, (g + 1) * STATE)
            ccolg = slice(gn + g * STATE, gn + (g + 1) * STATE)
            bg = bc_s[:, bcol]
            cg = bc_s[:, ccolg]
            if p % ppg == 0:
                s_s[...] = _dot_nt(cg, bg)
                ds_s[...] = jnp.zeros_like(ds_s)
            heads = (PAIR * p, PAIR * p + 1)
            masks = (low, jnp.logical_not(low))
            masks1 = (low1, jnp.logical_not(low1))
            ccol = [_head_column(cum, lane, h) for h in heads]
            dcol = [_head_column(dt, lane, h) for h in heads]
            cl = [jnp.sum(jnp.where(lane1 == h, clast, 0.0), axis=-1, keepdims=True) for h in heads]
            cum_px = jnp.where(low, ccol[0], ccol[1])
            dt_px = jnp.where(low, dcol[0], dcol[1])
            cl_px = jnp.where(low1, cl[0], cl[1])
            e_px = jnp.exp(cum_px)
            dec_end = jnp.exp(cl_px - cum_px)
            gdec = jnp.exp(cl_px)
            xs_p = xs_s[:, col]
            xdt = xs_p * dt_px
            dyp = dy_s[:, col]
            hc = hs_ref[0, p]
            hcb = hc.astype(BF16)
            dhn = dh_s[p]
            dhnb = dhn.astype(BF16)

            ddsk_ref[:, col] += jnp.sum(dyp * xs_p, axis=0, keepdims=True)
            dxs_acc = dsk_ref[:, col] * dyp
            dye = dyp * e_px
            dyeb = dye.astype(BF16)
            dbc_s[:, ccolg] += _dot_nt(dyeb, hcb)
            dcum_lane = dye * _dot(cg, hcb)
            dh_from_y = _dot_tn(cg, dyeb)
            xd = xdt * dec_end
            dxd = _dot(bg, dhnb)
            dbc_s[:, bcol] += _dot_nt(xd.astype(BF16), dhnb)
            dxdt = dxd * dec_end
            t1 = dxd * xd
            dcum_lane = dcum_lane - t1
            dcl_lane = jnp.sum(t1, axis=0, keepdims=True) + jnp.sum(dhn * hc, axis=0, keepdims=True) * gdec
            dh_s[p] = gdec * dhn + dh_from_y
            xdtb = xdt.astype(BF16)
            for hi, h in enumerate(heads):
                dym = jnp.where(masks[hi], dyp, 0.0).astype(BF16)
                dw = _dot_nt(dym, xdtb)
                dec = jnp.exp(jnp.where(tril, ccol[hi] - cumt_s[pl.ds(h, 1), :], -jnp.inf))
                wm = s_s[...] * dec
                dxdt = dxdt + _dot_tn(wm.astype(BF16), dym)
                ds_s[...] += dw * dec
                gm = dw * wm
                rowsum = jnp.sum(gm, axis=-1, keepdims=True)
                lanesum = jnp.sum(jnp.where(masks[hi], dcum_lane, 0.0), axis=-1, keepdims=True)
                dccol_s[...] += jnp.where(lane == h, rowsum + lanesum, 0.0)
                dcrow_s[pl.ds(h, 1), :] = jnp.sum(gm, axis=0, keepdims=True)
                dcl_h = jnp.sum(jnp.where(masks1[hi], dcl_lane, 0.0), axis=-1, keepdims=True)
                dcl_s[...] += jnp.where(lane1 == h, dcl_h, 0.0)
            ddt_lane = dxdt * xs_p
            for hi, h in enumerate(heads):
                s = jnp.sum(jnp.where(masks[hi], ddt_lane, 0.0), axis=-1, keepdims=True)
                ddtcol_s[...] += jnp.where(lane == h, s, 0.0)
            dxs_s[:, col] = dxs_acc + dxdt * dt_px
            if p % ppg == ppg - 1:
                dsb = ds_s[...].astype(BF16)
                dbc_s[:, ccolg] += _dot(dsb, bg)
                dbc_s[:, bcol] += _dot_tn(dsb, cg)

        rowi = lax.broadcasted_iota(jnp.int32, (ch, LANES), 0)
        dcum = dccol_s[...] - dcrow_s[...].T + jnp.where(rowi == ch - 1, dcl_s[...], 0.0)
        ri = lax.broadcasted_iota(jnp.int32, (ch, ch), 0)
        ci = lax.broadcasted_iota(jnp.int32, (ch, ch), 1)
        dal = _dot_exact((ri <= ci).astype(F32), dcum)
        ddt = dal * aneg_ref[...] + ddtcol_s[...]
        da_ref[...] += jnp.sum(dal * dt, axis=0, keepdims=True)
        ddtr = ddt * _sigmoid(dtr_ref[...] + dtb_ref[...])
        ddt_ref[...] = ddtr
        ddtb_ref[...] += jnp.sum(ddtr, axis=0, keepdims=True)

        for (dpost, dsl, tail, raw_ref, kw_ref, dkw_ref, db_ref, out_ref) in (
                (dxs_s, dsx_s, tailx, xs_ref, kwx_ref, dkwx_ref, dbx_ref, dxs_ref),
                (dbc_s, dsb_s, tailb, bc_ref, kwb_ref, dkwb_ref, dbb_ref, dbc_ref)):
            dxc = dpost[...] * dsl[...]
            dxcb = dxc.astype(BF16)
            raw = raw_ref[...].astype(F32)
            raw_end = raw_ref[ch - EDGE:ch, :].astype(F32)
            tail[0:EDGE, :] = dxcb[ch - EDGE:ch].astype(F32)
            db_ref[...] += jnp.sum(dxc, axis=0, keepdims=True)
            draw = kw_ref[pl.ds(SSD_K - 1, 1), :] * dxc
            dkw_ref[pl.ds(SSD_K - 1, 1), :] += jnp.sum(dxc * raw, axis=0, keepdims=True)
            fix = jnp.zeros((EDGE, dxc.shape[1]), F32)
            for k in range(SSD_K - 1):
                moved = _dot(shift_s[k], dxcb)
                miss = tail[pl.ds(SSD_K - 1 - k, EDGE), :] - moved[ch - EDGE:ch]
                draw = draw + kw_ref[pl.ds(k, 1), :] * moved
                fix = fix + kw_ref[pl.ds(k, 1), :] * miss
                dkw_ref[pl.ds(k, 1), :] += (jnp.sum(moved * raw, axis=0, keepdims=True)
                                            + jnp.sum(miss * raw_end, axis=0, keepdims=True))
            out_ref[...] = draw.astype(BF16)
            out_ref[ch - EDGE:ch, :] = (draw[ch - EDGE:ch] + fix).astype(BF16)
            tail[EDGE:EDGE + 8, :] = dxcb[0:EDGE].astype(F32)[0:8]

    full = lambda shape: pl.BlockSpec(shape, lambda i: tuple(0 for _ in shape))
    rev = lambda blk: (lambda i: (nc - 1 - i, blk))
    small_in = [(SSD_K, si), (SSD_K, 2 * gn), (1, LANES), (1, LANES), (1, si), (1, si)]
    small = [(SSD_K, si), (SSD_K, 2 * gn), (1, si), (1, 2 * gn), (1, LANES), (1, LANES), (1, si), (1, si)]
    return _ride(
        body, (nc,),
        [pl.BlockSpec((ch, si), rev(zblk)), pl.BlockSpec((ch, si), rev(xsblk)),
         pl.BlockSpec((ch, 2 * gn), rev(bcblk)),
         pl.BlockSpec((ch, si), rev(0)), pl.BlockSpec((ch, 2 * gn), rev(0)),
         pl.BlockSpec((ch, LANES), rev(0)), pl.BlockSpec((ch, si), rev(0)),
         pl.BlockSpec((1, npair, STATE, LANES), lambda i: (nc - 1 - i, 0, 0, 0)),
         pl.BlockSpec((ch, si), rev(cw // si))] + [full(s) for s in small_in],
        [pl.BlockSpec((ch, si), rev(0)), pl.BlockSpec((ch, si), rev(0)),
         pl.BlockSpec((ch, 2 * gn), rev(0)), pl.BlockSpec((ch, LANES), rev(0))] + [full(s) for s in small],
        [jax.ShapeDtypeStruct((t, si), BF16), jax.ShapeDtypeStruct((t, si), BF16),
         jax.ShapeDtypeStruct((t, 2 * gn), BF16), jax.ShapeDtypeStruct((t, LANES), F32)]
        + [jax.ShapeDtypeStruct(s, F32) for s in small],
        [pltpu.VMEM((EDGE + 8, si), F32), pltpu.VMEM((EDGE + 8, 2 * gn), F32),
         pltpu.VMEM((SSD_K - 1, ch, ch), BF16),
         pltpu.VMEM((ch, si), F32), pltpu.VMEM((ch, 2 * gn), BF16),
         pltpu.VMEM((ch, si), F32), pltpu.VMEM((ch, 2 * gn), F32),
         pltpu.VMEM((ch, si), F32), pltpu.VMEM((ch, si), F32), pltpu.VMEM((ch, 2 * gn), F32),
         pltpu.VMEM((npair, STATE, LANES), F32),
         pltpu.VMEM((ch, ch), F32), pltpu.VMEM((ch, ch), F32),
         pltpu.VMEM((ch, LANES), F32), pltpu.VMEM((LANES, ch), F32),
         pltpu.VMEM((ch, LANES), F32), pltpu.VMEM((LANES, ch), F32),
         pltpu.VMEM((ch, LANES), F32), pltpu.VMEM((1, LANES), F32)],
        [proj, proj, proj, pre_xs, pre_bc, dt_raw, ys, hsave, dy, kw_xs, kw_bc, dtb, aneg, dskip, normw],
        {}, rider, ("arbitrary",), name)


def final_loss(x, nw, tgt, name):
    t, d = x.shape
    tm = _tile(t, 512, 8)

    def body(x_ref, nw_ref, t_ref, dx_ref, dnw_ref, ls_ref):
        xf = x_ref[...]
        r = lax.rsqrt(jnp.mean(xf * xf, axis=-1, keepdims=True) + EPS)
        nx = xf * r
        e = nx * nw_ref[...] - t_ref[...]
        dyv = e * (1.0 / d)
        dn = dyv * nw_ref[...]
        dx_ref[...] = r * (dn - nx * jnp.mean(dn * nx, axis=-1, keepdims=True))

        @pl.when(pl.program_id(0) == 0)
        def _():
            dnw_ref[...] = jnp.zeros_like(dnw_ref)
            ls_ref[...] = jnp.zeros_like(ls_ref)

        dnw_ref[...] += jnp.sum(dyv * nx, axis=0, keepdims=True)
        ls_ref[...] += jnp.sum(e * e, axis=0, keepdims=True) * (0.5 / d)

    return pl.pallas_call(
        body, name=name, grid=(t // tm,),
        in_specs=[pl.BlockSpec((tm, d), lambda i: (i, 0)), pl.BlockSpec((1, d), lambda i: (0, 0)),
                  pl.BlockSpec((tm, d), lambda i: (i, 0))],
        out_specs=[pl.BlockSpec((tm, d), lambda i: (i, 0)), pl.BlockSpec((1, d), lambda i: (0, 0)),
                   pl.BlockSpec((1, d), lambda i: (0, 0))],
        out_shape=[jax.ShapeDtypeStruct((t, d), F32), jax.ShapeDtypeStruct((1, d), F32),
                   jax.ShapeDtypeStruct((1, d), F32)],
        compiler_params=_params(("arbitrary",)),
    )(x, nw, tgt)


def _rows3(a):
    if a.ndim == 1:
        return a.reshape(1, 1, a.shape[0])
    if a.ndim == 2:
        return a.reshape(1, *a.shape)
    return a.reshape(-1, a.shape[-2], a.shape[-1])


def adamw(w, g, m, v, name):
    shape = w.shape
    views = [_rows3(a) for a in (w, g, m, v)]
    b, r, c = views[0].shape
    tr = _tile(r, 256, 16) if r % 16 == 0 else r

    def body(w_ref, g_ref, m_ref, v_ref, d_ref, nm_ref, nv_ref):
        g = g_ref[...]
        m = ADAM_B1 * m_ref[...] + (1.0 - ADAM_B1) * g
        v = ADAM_B2 * v_ref[...] + (1.0 - ADAM_B2) * (g * g)
        m_hat = m / (1.0 - ADAM_B1 ** ADAM_STEP)
        v_hat = v / (1.0 - ADAM_B2 ** ADAM_STEP)
        d_ref[...] = -ADAM_LR * (m_hat / (jnp.sqrt(v_hat) + ADAM_EPS) + ADAM_WD * w_ref[...])
        nm_ref[...] = m
        nv_ref[...] = v

    spec = pl.BlockSpec((1, tr, c), lambda i, j: (i, j, 0))
    outs = pl.pallas_call(
        body, name=name, grid=(b, r // tr), in_specs=[spec] * 4, out_specs=[spec] * 3,
        out_shape=[jax.ShapeDtypeStruct((b, r, c), F32)] * 3,
        compiler_params=_params(("parallel", "parallel")),
    )(*views)
    return [o.reshape(shape) for o in outs]


def adamw_halves(w, g_mine, g_theirs, m, v, name):
    nl, r, c = w.shape
    r2 = r // 2
    tr = _tile(r2, 256, 16)
    nb = r2 // tr

    def body(w_ref, gm_ref, gt_ref, m_ref, v_ref, g_ref, d_ref, nm_ref, nv_ref):
        mine = (pl.program_id(1) // nb) == lax.axis_index("c")
        g = jnp.where(mine, gm_ref[...], gt_ref[...])
        m = ADAM_B1 * m_ref[...] + (1.0 - ADAM_B1) * g
        v = ADAM_B2 * v_ref[...] + (1.0 - ADAM_B2) * (g * g)
        m_hat = m / (1.0 - ADAM_B1 ** ADAM_STEP)
        v_hat = v / (1.0 - ADAM_B2 ** ADAM_STEP)
        g_ref[...] = g
        d_ref[...] = -ADAM_LR * (m_hat / (jnp.sqrt(v_hat) + ADAM_EPS) + ADAM_WD * w_ref[...])
        nm_ref[...] = m
        nv_ref[...] = v

    whole = pl.BlockSpec((1, tr, c), lambda l, i: (l, i, 0))
    half = pl.BlockSpec((1, tr, c), lambda l, i: (l, i % nb, 0))
    return pl.pallas_call(
        body, name=name, grid=(nl, 2 * nb), in_specs=[whole, half, half, whole, whole], out_specs=[whole] * 4,
        out_shape=[jax.ShapeDtypeStruct((nl, r, c), F32)] * 4,
        compiler_params=_params(("parallel", "parallel")),
    )(w, g_mine, g_theirs, m, v)


def _coords():
    return lax.axis_index("x"), lax.axis_index("y"), lax.axis_index("c")


def _ici_peers(x, y):
    chips = [(1 - x, y), (x, 1 - y), (1 - x, 1 - y)]
    return chips, [2 * cx + cy for cx, cy in chips]


def _place(ref, how, chip, layers, per):
    if how == "lead":
        return ref.at[chip, layers]
    start = pl.multiple_of(chip * per, per)
    if how == "rows":
        return ref.at[layers, pl.ds(start, per), :]
    return ref.at[layers, :, pl.ds(start, per)]


def gather_weights(shards, hows, name):
    na = len(shards)
    out_shape = []
    for s, how in zip(shards, hows):
        assert s.shape[0] % 2 == 0
        if how == "lead":
            shp = (N_CHIPS, *s.shape)
        elif how == "rows":
            shp = (s.shape[0], N_CHIPS * s.shape[1], s.shape[2])
        else:
            shp = (s.shape[0], s.shape[1], N_CHIPS * s.shape[2])
        out_shape.append(jax.ShapeDtypeStruct(shp, s.dtype))

    def body(*refs):
        ins = refs[:na]
        outs = refs[na:2 * na]
        send_sems, recv_sems = refs[2 * na:]
        x, y, c = _coords()
        me = 2 * x + y
        chips, chip_ids = _ici_peers(x, y)
        sibling = (x, y, 1 - c)

        def dst(a, chip, layers):
            per = {"lead": 0, "rows": ins[a].shape[1], "cols": ins[a].shape[-1]}[hows[a]]
            return _place(outs[a], hows[a], chip, layers, per)

        def copy(a, k, src, dst_ref, to):
            return pltpu.make_async_remote_copy(
                src_ref=src, dst_ref=dst_ref, send_sem=send_sems.at[7 * a + k], recv_sem=recv_sems.at[7 * a + k],
                device_id=to, device_id_type=MESH)

        started = []
        halves = []
        for a in range(na):
            nl = ins[a].shape[0]
            hl = nl // 2
            mine = pl.ds(c * hl, hl)
            theirs = pl.ds((1 - c) * hl, hl)
            halves.append((mine, theirs))
            for k in range(3):
                cp = copy(a, k, ins[a].at[mine], dst(a, me, mine), (*chips[k], c))
                cp.start()
                started.append(cp)
            own = copy(a, 6, ins[a], dst(a, me, pl.ds(0, nl)), sibling)
            own.start()
            started.append(own)
        for a in range(na):
            mine, _ = halves[a]
            for k in range(3):
                landed = dst(a, chip_ids[k], mine)
                copy(a, k, landed, landed, (*chips[k], c)).wait_recv()
                fw = copy(a, 3 + k, landed, landed, sibling)
                fw.start()
                started.append(fw)
        for a in range(na):
            _, theirs = halves[a]
            for k in range(3):
                got = dst(a, chip_ids[k], theirs)
                copy(a, 3 + k, got, got, sibling).wait_recv()
            whole = dst(a, me, pl.ds(0, ins[a].shape[0]))
            copy(a, 6, whole, whole, sibling).wait_recv()
        for cp in started:
            cp.wait_send()

    return pl.pallas_call(
        body, name=name, in_specs=_any_specs(na), out_specs=_any_specs(na), out_shape=out_shape,
        scratch_shapes=[pltpu.SemaphoreType.DMA((7 * na,)), pltpu.SemaphoreType.DMA((7 * na,))],
        compiler_params=pltpu.CompilerParams(has_side_effects=True),
    )(*shards)


def _remote(src, dst, send_sems, recv_sems, k, to):
    return pltpu.make_async_remote_copy(src_ref=src, dst_ref=dst, send_sem=send_sems.at[k], recv_sem=recv_sems.at[k],
                                        device_id=to, device_id_type=MESH)


LAYER_HOW = ("lead", "rows", "cols", "rows")


def _layer_place(ref, how, chip, shard_shape, start, size):
    r, c = shard_shape
    if how == "lead":
        return ref.at[chip, :, pl.ds(start, size), :]
    if how == "rows":
        return ref.at[:, pl.ds(pl.multiple_of(chip * r + start, HALO), size), :]
    return ref.at[:, pl.ds(start, size), pl.ds(pl.multiple_of(chip * c, LANES), c)]


def weight_rider_ici(shards, layer):
    shapes = [tuple(s.shape[1:]) for s in shards]
    out_shapes = []
    for (r, c), how, s in zip(shapes, LAYER_HOW, shards):
        shp = {"lead": (N_CHIPS, 1, r, c), "rows": (1, N_CHIPS * r, c), "cols": (1, r, N_CHIPS * c)}[how]
        out_shapes.append(jax.ShapeDtypeStruct(shp, s.dtype))

    def copies(ins, outs, send_sems, recv_sems):
        x, y, c = _coords()
        me = 2 * x + y
        chips, chip_ids = _ici_peers(x, y)
        sibling = (x, y, 1 - c)
        pairs = []
        for a, (shape, how) in enumerate(zip(shapes, LAYER_HOW)):
            half = shape[0] // 2
            mine = pl.multiple_of(c * half, HALO)
            src = ins[a].at[pl.ds(layer, 1)]
            for k in range(3):
                to = (*chips[k], c)
                land = _layer_place(outs[a], how, chip_ids[k], shape, mine, half)
                pairs.append((_remote(src.at[:, pl.ds(mine, half), :], _layer_place(outs[a], how, me, shape, mine, half),
                                      send_sems, recv_sems, 4 * a + k, to),
                              _remote(land, land, send_sems, recv_sems, 4 * a + k, to)))
            whole = _layer_place(outs[a], how, me, shape, 0, shape[0])
            pairs.append((_remote(src, whole, send_sems, recv_sems, 4 * a + 3, sibling),
                          _remote(whole, whole, send_sems, recv_sems, 4 * a + 3, sibling)))
        return pairs

    return Rider(list(shards), out_shapes, {}, 4 * len(shards), copies)


def weight_rider_d2d(bufs, shapes):
    def copies(ins, outs, send_sems, recv_sems):
        x, y, c = _coords()
        _, chip_ids = _ici_peers(x, y)
        sibling = (x, y, 1 - c)
        pairs = []
        for a, (shape, how) in enumerate(zip(shapes, LAYER_HOW)):
            half = shape[0] // 2
            mine = pl.multiple_of(c * half, HALO)
            theirs = pl.multiple_of((1 - c) * half, HALO)
            for k in range(3):
                land = _layer_place(outs[a], how, chip_ids[k], shape, theirs, half)
                pairs.append((_remote(_layer_place(ins[a], how, chip_ids[k], shape, mine, half),
                                      _layer_place(outs[a], how, chip_ids[k], shape, mine, half),
                                      send_sems, recv_sems, 3 * a + k, sibling),
                              _remote(land, land, send_sems, recv_sems, 3 * a + k, sibling)))
        return pairs

    return Rider(list(bufs), [jax.ShapeDtypeStruct(b.shape, b.dtype) for b in bufs],
                 {a: a for a in range(len(bufs))}, 3 * len(bufs), copies)


def grads_rider_sibling(arrs):
    def copies(ins, outs, send_sems, recv_sems):
        x, y, c = _coords()
        sibling = (x, y, 1 - c)
        pairs = []
        for a in range(len(arrs)):
            r2 = ins[a].shape[1] // 2
            src = ins[a].at[:, pl.ds(pl.multiple_of((1 - c) * r2, 8), r2), :]
            pairs.append((_remote(src, outs[a], send_sems, recv_sems, a, sibling),
                          _remote(outs[a], outs[a], send_sems, recv_sems, a, sibling)))
        return pairs

    return Rider(list(arrs), [jax.ShapeDtypeStruct((a.shape[0], a.shape[1] // 2, a.shape[2]), a.dtype) for a in arrs],
                 {}, len(arrs), copies)


def chip_sum(g, recv, name):
    nch, r, c = g.shape
    r2 = r // 2
    tr = _tile(r2, 256, 16)
    nb = r2 // tr

    def body(g0_ref, g1_ref, r_ref, o32_ref, o16_ref):
        s = jnp.where(lax.axis_index("c") == 0, g0_ref[...], g1_ref[...]) + r_ref[...]
        o32_ref[...] = s
        o16_ref[...] = s.astype(BF16)

    here = pl.BlockSpec((1, tr, c), lambda i, j: (i, j, 0))
    return pl.pallas_call(
        body, name=name, grid=(nch, nb),
        in_specs=[here, pl.BlockSpec((1, tr, c), lambda i, j: (i, nb + j, 0)), here],
        out_specs=[here, here],
        out_shape=[jax.ShapeDtypeStruct((nch, r2, c), F32), jax.ShapeDtypeStruct((nch, r2, c), BF16)],
        compiler_params=_params(("parallel", "parallel")),
    )(g, g, recv)


def grads_rider_chips(arrs):
    def copies(ins, outs, send_sems, recv_sems):
        x, y, c = _coords()
        chips, chip_ids = _ici_peers(x, y)
        pairs = []
        for a in range(len(arrs)):
            for k in range(3):
                to = (*chips[k], c)
                pairs.append((_remote(ins[a].at[chip_ids[k]], outs[a].at[k], send_sems, recv_sems, 3 * a + k, to),
                              _remote(outs[a].at[k], outs[a].at[k], send_sems, recv_sems, 3 * a + k, to)))
        return pairs

    return Rider(list(arrs), [jax.ShapeDtypeStruct((3, *a.shape[1:]), a.dtype) for a in arrs], {}, 3 * len(arrs),
                 copies)


def grad_sum(p32, recv, layer, nl, buf, name):
    _, r2, c = p32.shape
    tr = _tile(r2, 256, 16)
    nb = r2 // tr

    def body(p0_ref, p1_ref, p2_ref, p3_ref, r0_ref, r1_ref, r2_ref, *rest):
        o_ref = rest[-1]
        x, y, _ = _coords()
        chip = 2 * x + y
        own = jnp.where(chip == 0, p0_ref[...], jnp.where(chip == 1, p1_ref[...],
                                                        jnp.where(chip == 2, p2_ref[...], p3_ref[...])))
        o_ref[...] = own + r0_ref[...].astype(F32) + r1_ref[...].astype(F32) + r2_ref[...].astype(F32)

    slot = lambda k: pl.BlockSpec((1, tr, c), lambda j: (k, j, 0))
    in_specs = [slot(k) for k in range(N_CHIPS)] + [slot(k) for k in range(3)]
    args = [p32] * N_CHIPS + [recv] * 3
    aliases = {}
    if buf is not None:
        in_specs.append(pl.BlockSpec(memory_space=pl.ANY))
        args.append(buf)
        aliases = {len(args) - 1: 0}
    return pl.pallas_call(
        body, name=name, grid=(nb,), in_specs=in_specs,
        out_specs=pl.BlockSpec((1, tr, c), lambda j: (layer, j, 0)),
        out_shape=jax.ShapeDtypeStruct((nl, r2, c), F32),
        input_output_aliases=aliases,
        compiler_params=_params(("parallel",)),
    )(*args)


def grads_rider_exchange(bufs):
    def copies(ins, outs, send_sems, recv_sems):
        x, y, c = _coords()
        sibling = (x, y, 1 - c)
        return [(_remote(ins[a], outs[a], send_sems, recv_sems, a, sibling),
                 _remote(outs[a], outs[a], send_sems, recv_sems, a, sibling)) for a in range(len(bufs))]

    return Rider(list(bufs), [jax.ShapeDtypeStruct(b.shape, b.dtype) for b in bufs], {}, len(bufs), copies)


def allreduce_small(buf, name):
    r, cdim = buf.shape

    def body(x_ref, o_ref, gath, send_sems, recv_sems):
        x, y, c = _coords()
        me, sibling = (x, y, c), (x, y, 1 - c)
        chips, _ = _ici_peers(x, y)

        def slot(px, py, pc):
            return gath.at[4 * px + 2 * py + pc]

        def copy(k, block, to, src=None):
            return pltpu.make_async_remote_copy(
                src_ref=slot(*block) if src is None else src, dst_ref=slot(*block),
                send_sem=send_sems.at[k], recv_sem=recv_sems.at[k], device_id=to, device_id_type=MESH)

        gath[4 * x + 2 * y + c] = x_ref[...]
        first = [copy(0, me, sibling, src=x_ref)]
        first += [copy(1 + j, me, (*chip, c), src=x_ref) for j, chip in enumerate(chips)]
        for cp in first:
            cp.start()
        passed = [copy(4 + j, (*chip, c), sibling) for j, chip in enumerate(chips)]
        for j, chip in enumerate(chips):
            copy(1 + j, (*chip, c), me).wait_recv()
            passed[j].start()
        copy(0, sibling, me).wait_recv()
        for j, chip in enumerate(chips):
            copy(4 + j, (*chip, 1 - c), me).wait_recv()
        for cp in first + passed:
            cp.wait_send()
        acc = gath[0]
        for d in range(1, 8):
            acc = acc + gath[d]
        o_ref[...] = acc

    return pl.pallas_call(
        body, name=name,
        in_specs=[pl.BlockSpec(memory_space=pltpu.VMEM)], out_specs=pl.BlockSpec(memory_space=pltpu.VMEM),
        out_shape=jax.ShapeDtypeStruct((r, cdim), F32),
        scratch_shapes=[pltpu.VMEM((8, r, cdim), F32), pltpu.SemaphoreType.DMA((7,)), pltpu.SemaphoreType.DMA((7,))],
        compiler_params=pltpu.CompilerParams(has_side_effects=True),
    )(buf)


def _expand_heads(v):
    return jnp.repeat(v.astype(F32), HEAD_DIM).reshape(1, -1)


def _pad_lanes(v):
    return jnp.pad(v.astype(F32), (0, LANES - v.shape[0])).reshape(1, LANES)


def local_step(x, tgt, p, comm, cols):
    nl = p["norm_mix_w"].shape[0]
    d = x.shape[1]
    cw = p["short_conv_w"].shape[2]
    si = p["ssd_norm_w"].shape[1]
    w0 = comm.weights(0)
    ff = w0["w_up"].shape[2]
    nh = si // HEAD_DIM
    gn = GROUPS * STATE
    npad = w0["w_in"].shape[2]
    dt_off = 3 * cw + si + si + 2 * gn
    assert cols == dt_off + nh and nh <= LANES and dt_off % LANES == 0 and npad == dt_off + LANES
    pieces = [(0, cw), (cw, cw), (2 * cw, cw), (3 * cw, si), (3 * cw + si, si), (3 * cw + 2 * si, 2 * gn),
              (dt_off, LANES)]

    saved = []
    for l in range(nl):
        nw1 = p["norm_mix_w"][l].reshape(1, d)
        nw2 = p["norm_mlp_w"][l].reshape(1, d)
        kw3 = p["short_conv_w"][l]
        kwx, kwb = p["ssd_conv_w"][l][:, :si], p["ssd_conv_w"][l][:, si:]
        bx, bb = p["ssd_conv_b"][l][:si].reshape(1, si), p["ssd_conv_b"][l][si:].reshape(1, 2 * gn)
        dtb = _pad_lanes(p["dt_bias"][l])
        aneg = _pad_lanes(-jnp.exp(p["a_log"][l]))
        dsk = _expand_heads(p["d_skip"][l])
        snw = p["ssd_norm_w"][l].reshape(1, si)
        ssd_args = (kwx, kwb, bx, bb, dtb, aneg, dsk, snw)

        wl = comm.weights(l)
        (proj, h), sent = norm_matmul(x, nw1, wl["w_in"], 0, npad, 0, BF16, "in_proj", rider=comm.forward_rider_a(l))
        (dt_raw,), _ = norm_matmul(x, nw1, wl["w_in"], 0, LANES, dt_off // LANES, F32, "dt_proj", emit_h=False)
        y_mix = conv_mixer_fwd(proj, kw3, cw, cw + si, "conv_mixer_fwd")
        (y_mix, *ssd_saved), sent = ssd_fwd(proj, dt_raw, y_mix, *ssd_args, cw, si, "ssd_fwd",
                                            rider=comm.forward_rider_b(l, sent))
        comm.forward_done(l, sent)
        x2, _ = matmul(y_mix, wl["w_out"], 0, False, d, F32, "out_proj", residual=x)
        (up, h2), _ = norm_matmul(x2, nw2, wl["w_up"], 0, ff, 0, BF16, "up_proj")
        x3, _ = matmul(up, wl["w_down"], 0, False, d, F32, "down_proj", lhs_fn=_relu2, residual=x2)
        saved.append((x, h, proj, dt_raw, y_mix, ssd_saved, x2, h2, up, nw1, nw2, kw3, ssd_args))
        x = x3

    dx, dwf, lvec = final_loss(x, p["final_norm_w"].reshape(1, d), tgt, "final_loss")
    loss = jnp.sum(lvec)

    names = ("norm_mix_w", "short_conv_w", "ssd_conv_w", "ssd_conv_b", "dt_bias", "a_log", "d_skip",
             "ssd_norm_w", "norm_mlp_w")
    grads = {k: [None] * nl for k in names}
    for l in reversed(range(nl)):
        x0, h, proj, dt_raw, y_mix, ssd_saved, x2, h2, up, nw1, nw2, kw3, ssd_args = saved[l]
        kwx, kwb, _, _, dtb, aneg, dsk, snw = ssd_args
        wl = comm.weights(l)
        dup, sent = matmul(dx, wl["w_down"], 0, True, ff, BF16, "down_bwd", relu_gate=up,
                           rider=comm.backward_rider_a(l))
        comm.backward_a_done(l, sent)
        g_down = matmul_tn(up, dx, "down_wgrad", a_fn=_relu2)
        dx2, dnw2 = matmul_normbwd([dup], [(0, ff)], wl["w_up"], 0, x2, nw2, dx, "up_bwd")
        g_up = matmul_tn(h2, dup, "up_wgrad", by_chip=True)
        grads["norm_mlp_w"][l] = dnw2.reshape(d)
        dy, _ = matmul(dx2, wl["w_out"], 0, True, cw + si, BF16, "out_bwd")
        g_out = matmul_tn(y_mix, dx2, "out_wgrad")
        dub, duc, duh, dkw3 = conv_mixer_bwd(proj, dy, kw3, cw, "conv_mixer_bwd")
        (dz, dxs, dbc, ddt, dkwx, dkwb, dbx, dbb, ddtb, da, ddsk, dsnw), sent = ssd_bwd(
            proj, dt_raw, *ssd_saved, dy, kwx, kwb, dtb, aneg, dsk, snw, cw, si, "ssd_bwd",
            rider=comm.backward_rider_b(l))
        comm.backward_b_done(l, sent)
        dpieces = [dub, duc, duh, dz, dxs, dbc, ddt]
        dxl, dnw1 = matmul_normbwd(dpieces, pieces, wl["w_in"], 0, x0, nw1, dx2, "in_bwd")
        g_in = split_to_chips(
            [matmul_tn(h, dp, "in_wgrad_%d" % i) for i, dp in enumerate(dpieces)], cols, "in_wgrad_split")
        comm.take_gradients(l, dict(w_in=g_in, w_out=g_out, w_up=g_up, w_down=g_down))
        grads["norm_mix_w"][l] = dnw1.reshape(d)
        grads["short_conv_w"][l] = dkw3
        grads["ssd_conv_w"][l] = jnp.concatenate([dkwx, dkwb], axis=1)
        grads["ssd_conv_b"][l] = jnp.concatenate([dbx, dbb], axis=1).reshape(-1)
        grads["dt_bias"][l] = ddtb[0, :nh]
        grads["a_log"][l] = da[0, :nh] * aneg[0, :nh]
        grads["d_skip"][l] = jnp.sum(ddsk.reshape(nh, HEAD_DIM), axis=1)
        grads["ssd_norm_w"][l] = dsnw.reshape(si)
        dx = dxl

    grads = {k: jnp.stack(v) for k, v in grads.items()}
    grads["final_norm_w"] = dwf.reshape(d)
    return loss, dx, grads


BIG = ("w_in", "w_out", "w_up", "w_down")
SMALL_SHARDED = ("short_conv_w", "ssd_conv_w")
SMALL_REPL = ("norm_mix_w", "ssd_conv_b", "dt_bias", "a_log", "d_skip", "ssd_norm_w", "norm_mlp_w", "final_norm_w")
WEIGHTS = ("norm_mix_w", "w_in", "short_conv_w", "ssd_conv_w", "ssd_conv_b", "dt_bias", "a_log", "d_skip",
           "ssd_norm_w", "w_out", "norm_mlp_w", "w_up", "w_down", "final_norm_w")
SMALL_COLS = 1024


def _pack_small(named):
    flat = jnp.concatenate([v.reshape(-1).astype(F32) for v in named])
    n = flat.shape[0]
    rows = -(-n // SMALL_COLS)
    rows = -(-rows // 8) * 8
    return jnp.pad(flat, (0, rows * SMALL_COLS - n)).reshape(rows, SMALL_COLS)


def _unpack_small(buf, like):
    flat = buf.reshape(-1)
    out, off = [], 0
    for v in like:
        out.append(flat[off:off + v.size].reshape(v.shape))
        off += v.size
    return out


class ChipComm:
    def __init__(self, shards, nl, npad):
        self.shards, self.nl, self.npad = shards, nl, npad
        self.shapes = [tuple(s.shape[1:]) for s in shards]
        self.w = {}
        self.pending = None
        self.sums = None
        self.bufs = [None] * len(shards)
        landed = run_rider(weight_rider_ici(shards, 0), "gather_first_ici")
        self.forward_done(-1, run_rider(weight_rider_d2d(landed, self.shapes), "gather_first_d2d"))

    def weights(self, l):
        return self.w[l]

    def forward_rider_a(self, l):
        return weight_rider_ici(self.shards, l + 1) if l + 1 < self.nl else None

    def forward_rider_b(self, l, landed):
        return weight_rider_d2d(landed, self.shapes) if landed else None

    def forward_done(self, l, gathered):
        if gathered:
            w_in = join_from_chips(gathered[0], self.npad, "w_in_join")
            self.w[l + 1] = dict(w_in=w_in, w_out=gathered[1], w_up=gathered[2], w_down=gathered[3])

    def take_gradients(self, l, g):
        by_chip = [g[k] if g[k].ndim == 3 else g[k].reshape(N_CHIPS, g[k].shape[0] // N_CHIPS, g[k].shape[1])
                   for k in BIG]
        self.pending = (l, by_chip)

    def backward_rider_a(self, l):
        return grads_rider_sibling(self.pending[1]) if self.pending else None

    def backward_a_done(self, l, from_sibling):
        if from_sibling:
            self.sums = [chip_sum(g, r, "chip_sum") for g, r in zip(self.pending[1], from_sibling)]

    def backward_rider_b(self, l):
        return grads_rider_chips([s[1] for s in self.sums]) if self.pending else None

    def backward_b_done(self, l, from_chips):
        if from_chips:
            layer = self.pending[0]
            self.bufs = [grad_sum(s[0], r, layer, self.nl, buf, "grad_sum")
                         for s, r, buf in zip(self.sums, from_chips, self.bufs)]
            self.pending = None

    def finish(self):
        self.backward_a_done(None, run_rider(self.backward_rider_a(None), "grads_to_sibling"))
        self.backward_b_done(None, run_rider(self.backward_rider_b(None), "grads_to_chips"))
        theirs = run_rider(grads_rider_exchange(self.bufs), "grads_exchange")
        return dict(zip(BIG, zip(self.bufs, theirs)))


def kernel(x, norm_mix_w, w_in, short_conv_w, ssd_conv_w, ssd_conv_b, dt_bias, a_log, d_skip, ssd_norm_w, w_out, norm_mlp_w, w_up, w_down, final_norm_w, loss_target, m_norm_mix_w, m_w_in, m_short_conv_w, m_ssd_conv_w, m_ssd_conv_b, m_dt_bias, m_a_log, m_d_skip, m_ssd_norm_w, m_w_out, m_norm_mlp_w, m_w_up, m_w_down, m_final_norm_w, v_norm_mix_w, v_w_in, v_short_conv_w, v_ssd_conv_w, v_ssd_conv_b, v_dt_bias, v_a_log, v_d_skip, v_ssd_norm_w, v_w_out, v_norm_mlp_w, v_w_up, v_w_down, v_final_norm_w):
    w = dict(norm_mix_w=norm_mix_w, w_in=w_in, short_conv_w=short_conv_w, ssd_conv_w=ssd_conv_w,
             ssd_conv_b=ssd_conv_b, dt_bias=dt_bias, a_log=a_log, d_skip=d_skip, ssd_norm_w=ssd_norm_w, w_out=w_out,
             norm_mlp_w=norm_mlp_w, w_up=w_up, w_down=w_down, final_norm_w=final_norm_w)
    m = dict(norm_mix_w=m_norm_mix_w, w_in=m_w_in, short_conv_w=m_short_conv_w, ssd_conv_w=m_ssd_conv_w,
             ssd_conv_b=m_ssd_conv_b, dt_bias=m_dt_bias, a_log=m_a_log, d_skip=m_d_skip, ssd_norm_w=m_ssd_norm_w,
             w_out=m_w_out, norm_mlp_w=m_norm_mlp_w, w_up=m_w_up, w_down=m_w_down, final_norm_w=m_final_norm_w)
    v = dict(norm_mix_w=v_norm_mix_w, w_in=v_w_in, short_conv_w=v_short_conv_w, ssd_conv_w=v_ssd_conv_w,
             ssd_conv_b=v_ssd_conv_b, dt_bias=v_dt_bias, a_log=v_a_log, d_skip=v_d_skip, ssd_norm_w=v_ssd_norm_w,
             w_out=v_w_out, norm_mlp_w=v_norm_mlp_w, w_up=v_w_up, w_down=v_w_down, final_norm_w=v_final_norm_w)
    xi, yi, ci = lax.axis_index("x"), lax.axis_index("y"), lax.axis_index("c")
    chip = 2 * xi + yi
    nl = w_up.shape[0]
    cols = N_CHIPS * w_in.shape[2]
    npad = cols // LANES * LANES + LANES

    full = dict(w)
    small_gathered = gather_weights([w[k] for k in SMALL_SHARDED], ["lead"] * len(SMALL_SHARDED), "gather_small")
    for k, g4 in zip(SMALL_SHARDED, small_gathered):
        full[k] = jnp.concatenate([g4[j] for j in range(N_CHIPS)], axis=2)
    comm = ChipComm([w[k].astype(BF16) for k in BIG], nl, npad)

    loss, grad_x, grads = local_step(x[0], loss_target[0], full, comm, cols)
    loss = lax.psum(loss, ("x", "y", "c"))
    halves = comm.finish()
    g_shard = {}

    small_names = SMALL_REPL + SMALL_SHARDED
    small_sum = allreduce_small(_pack_small([grads[k] for k in small_names]), "allreduce_small")
    for k, g in zip(small_names, _unpack_small(small_sum, [grads[k] for k in small_names])):
        if k in SMALL_SHARDED:
            width = w[k].shape[2]
            g = lax.dynamic_slice_in_dim(g, chip * width, width, axis=2)
        g_shard[k] = g

    delta, new_m, new_v = {}, {}, {}
    for k in BIG:
        g_shard[k], delta[k], new_m[k], new_v[k] = adamw_halves(w[k], *halves[k], m[k], v[k], "adamw_%s" % k)
    packed = [_pack_small([d_[k] for k in small_names]) for d_ in (w, g_shard, m, v)]
    outs = adamw(*packed, "adamw_small")
    for d_, buf in zip((delta, new_m, new_v), outs):
        for k, val in zip(small_names, _unpack_small(buf, [w[k] for k in small_names])):
            d_[k] = val

    return (loss, grad_x[None], *[g_shard[k] for k in WEIGHTS], *[delta[k] for k in WEIGHTS],
            *[new_m[k] for k in WEIGHTS], *[new_v[k] for k in WEIGHTS])
```

```python
import functools

import jax
import jax.numpy as jnp
from jax import lax
from jax.experimental import pallas as pl
from jax.experimental.pallas import tpu as pltpu

F32 = jnp.float32
BF16 = jnp.bfloat16

EPS = 1e-5
HEAD_DIM = 64
STATE = 128
GROUPS = 2
SHORT_K = 3
SSD_K = 4
LANES = 128
PAIR = LANES // HEAD_DIM
SCAN_CHUNK = 256
HALO = 16
N_CHIPS = 4
VMEM_LIMIT = 56 * 1024 * 1024

ADAM_LR = 0.001
ADAM_B1 = 0.9
ADAM_B2 = 0.999
ADAM_EPS = 1e-08
ADAM_WD = 0.01
ADAM_STEP = 10

MESH = pl.DeviceIdType.MESH


def _params(sem):
    return pltpu.CompilerParams(dimension_semantics=sem, vmem_limit_bytes=VMEM_LIMIT)


def _tile(n, cap, quantum):
    if n <= cap:
        return n
    best = None
    for t in range(quantum, cap + 1, quantum):
        if n % t == 0:
            best = t
    assert best is not None, (n, cap, quantum)
    return best


def _dot(a, b):
    return jnp.dot(a, b, preferred_element_type=F32)


def _dot_nt(a, b):
    return lax.dot_general(a, b, (((1,), (1,)), ((), ())), preferred_element_type=F32)


def _dot_tn(a, b):
    return lax.dot_general(a, b, (((0,), (0,)), ((), ())), preferred_element_type=F32)


def _dot_exact(a, b):
    return jnp.dot(a, b, precision=lax.Precision.HIGHEST, preferred_element_type=F32)


def _sigmoid(x):
    return pl.reciprocal(1.0 + jnp.exp(-x), approx=True)


def _softplus(x):
    return jnp.maximum(x, 0.0) + jnp.log(1.0 + jnp.exp(-jnp.abs(x)))


def _relu2(v):
    return jnp.square(jnp.maximum(v, 0.0))


class Rider:
    def __init__(self, ins, out_shapes, aliases, n_sems, copies):
        self.ins, self.out_shapes, self.aliases, self.n_sems, self.copies = ins, out_shapes, aliases, n_sems, copies


def _any_specs(n):
    return [pl.BlockSpec(memory_space=pl.ANY)] * n


def _ride(body, grid, in_specs, out_specs, out_shape, scratch, args, aliases, rider, sem, name):
    n_in, n_out, n_scr = len(in_specs), len(out_specs), len(scratch)
    if rider is None:
        outs = pl.pallas_call(
            body, name=name, grid=grid, in_specs=in_specs, out_specs=out_specs, out_shape=out_shape,
            scratch_shapes=scratch, input_output_aliases=aliases, compiler_params=_params(sem))(*args)
        return list(outs), []
    ri, ro = len(rider.ins), len(rider.out_shapes)
    last = tuple(g - 1 for g in grid)

    def wrapped(*refs):
        ins = refs[:n_in]
        r_ins = refs[n_in:n_in + ri]
        outs = refs[n_in + ri:n_in + ri + n_out]
        r_outs = refs[n_in + ri + n_out:n_in + ri + n_out + ro]
        scr = refs[n_in + ri + n_out + ro:n_in + ri + n_out + ro + n_scr]
        send_sems, recv_sems = refs[-2:]
        ids = [pl.program_id(a) for a in range(len(grid))]
        at_first = functools.reduce(jnp.logical_and, [i == 0 for i in ids])
        at_last = functools.reduce(jnp.logical_and, [i == e for i, e in zip(ids, last)])

        @pl.when(at_first)
        def _():
            for cp, _ in rider.copies(r_ins, r_outs, send_sems, recv_sems):
                cp.start()

        body(*ins, *outs, *scr)

        @pl.when(at_last)
        def _():
            for cp, landed in rider.copies(r_ins, r_outs, send_sems, recv_sems):
                cp.wait_send()
                landed.wait_recv()

    all_aliases = dict(aliases)
    all_aliases.update({n_in + a: n_out + b for a, b in rider.aliases.items()})
    outs = pl.pallas_call(
        wrapped, name=name, grid=grid, in_specs=list(in_specs) + _any_specs(ri),
        out_specs=list(out_specs) + _any_specs(ro), out_shape=list(out_shape) + list(rider.out_shapes),
        scratch_shapes=list(scratch) + [pltpu.SemaphoreType.DMA((rider.n_sems,)),
                                        pltpu.SemaphoreType.DMA((rider.n_sems,))],
        input_output_aliases=all_aliases, compiler_params=_params(sem))(*args, *rider.ins)
    return list(outs[:n_out]), list(outs[n_out:])


def run_rider(rider, name):
    ri, ro = len(rider.ins), len(rider.out_shapes)

    def body(*refs):
        send_sems, recv_sems = refs[-2:]
        pairs = rider.copies(refs[:ri], refs[ri:ri + ro], send_sems, recv_sems)
        for cp, _ in pairs:
            cp.start()
        for cp, landed in pairs:
            cp.wait_send()
            landed.wait_recv()

    return list(pl.pallas_call(
        body, name=name, in_specs=_any_specs(ri), out_specs=_any_specs(ro), out_shape=list(rider.out_shapes),
        scratch_shapes=[pltpu.SemaphoreType.DMA((rider.n_sems,)), pltpu.SemaphoreType.DMA((rider.n_sems,))],
        input_output_aliases=dict(rider.aliases),
        compiler_params=pltpu.CompilerParams(has_side_effects=True))(*rider.ins))


def norm_matmul(x, nw, w, layer, n, col0, out_dtype, name, emit_h=True, rider=None):
    t, d = x.shape
    mxu_cols = 2 * LANES
    tn = _tile(n, 1536, mxu_cols if n % mxu_cols == 0 else LANES)
    if n % mxu_cols == 0 and tn < 1024 <= n:
        tn = _tile(n, 3072, mxu_cols)
    tm = _tile(t, 512 if tn > 1536 else 1024, 8)
    nj = n // tn

    def body(x_ref, nw_ref, w_ref, o_ref, h_ref):
        @pl.when(pl.program_id(1) == 0)
        def _():
            xf = x_ref[...]
            r = lax.rsqrt(jnp.mean(xf * xf, axis=-1, keepdims=True) + EPS)
            h_ref[...] = (xf * r * nw_ref[...]).astype(BF16)

        o_ref[...] = _dot(h_ref[...], w_ref[...]).astype(out_dtype)

    out_specs = [pl.BlockSpec((tm, tn), lambda i, j: (i, j))]
    out_shape = [jax.ShapeDtypeStruct((t, n), out_dtype)]
    if emit_h:
        out_specs.append(pl.BlockSpec((tm, d), lambda i, j: (i, 0)))
        out_shape.append(jax.ShapeDtypeStruct((t, d), BF16))
    return _ride(
        body, (t // tm, nj),
        [pl.BlockSpec((tm, d), lambda i, j: (i, 0)), pl.BlockSpec((1, d), lambda i, j: (0, 0)),
         pl.BlockSpec((None, d, tn), lambda i, j: (layer, 0, col0 * nj + j))],
        out_specs, out_shape, [] if emit_h else [pltpu.VMEM((tm, d), BF16)], [x, nw, w], {}, rider,
        ("parallel", "arbitrary"), name)


def matmul(lhs, w, layer, transposed, n, out_dtype, name, *, lhs_fn=None, residual=None, relu_gate=None,
           rider=None):
    t, k = lhs.shape
    tm = _tile(t, 512 if k > 2048 else 1024, 8)
    tn = _tile(n, 1024, LANES)
    staged = lhs.dtype != BF16 or lhs_fn is not None
    fn = lhs_fn if lhs_fn is not None else (lambda v: v)
    has_extra = residual is not None or relu_gate is not None
    dot = _dot_nt if transposed else _dot

    def body(*refs):
        a_ref, w_ref = refs[:2]
        extra = refs[2] if has_extra else None
        o_ref = refs[3] if has_extra else refs[2]
        if staged:
            s_ref = refs[-1]

            @pl.when(pl.program_id(1) == 0)
            def _():
                s_ref[...] = fn(a_ref[...].astype(F32)).astype(BF16)

            a_ref = s_ref
        acc = dot(a_ref[...], w_ref[...])
        if residual is not None:
            acc = acc + extra[...]
        if relu_gate is not None:
            acc = acc * (2.0 * jnp.maximum(extra[...].astype(F32), 0.0))
        o_ref[...] = acc.astype(out_dtype)

    if transposed:
        w_spec = pl.BlockSpec((None, tn, k), lambda i, j: (layer, j, 0))
    else:
        w_spec = pl.BlockSpec((None, k, tn), lambda i, j: (layer, 0, j))
    in_specs = [pl.BlockSpec((tm, k), lambda i, j: (i, 0)), w_spec]
    args = [lhs, w]
    if has_extra:
        in_specs.append(pl.BlockSpec((tm, tn), lambda i, j: (i, j)))
        args.append(residual if residual is not None else relu_gate)
    outs, extra = _ride(
        body, (t // tm, n // tn), in_specs, [pl.BlockSpec((tm, tn), lambda i, j: (i, j))],
        [jax.ShapeDtypeStruct((t, n), out_dtype)], [pltpu.VMEM((tm, k), BF16)] if staged else [], args, {},
        rider, ("parallel", "arbitrary"), name)
    return outs[0], extra


def matmul_normbwd(lhs, pieces, w, layer, x, nw, dres, name):
    t, d = x.shape
    nl = len(lhs)
    tm = _tile(t, 256, 8)
    for off, width in pieces:
        assert off % width == 0

    def body(*refs):
        lrefs = refs[:nl]
        wrefs = refs[nl:2 * nl]
        x_ref, nw_ref, dres_ref, dx_ref, dnw_ref = refs[2 * nl:]
        dh = _dot_nt(lrefs[0][...].astype(BF16), wrefs[0][...])
        for a_ref, w_ref in zip(lrefs[1:], wrefs[1:]):
            dh = dh + _dot_nt(a_ref[...].astype(BF16), w_ref[...])
        xf = x_ref[...]
        r = lax.rsqrt(jnp.mean(xf * xf, axis=-1, keepdims=True) + EPS)
        nx = xf * r
        dn = dh * nw_ref[...]
        dx = r * (dn - nx * jnp.mean(dn * nx, axis=-1, keepdims=True))
        dx_ref[...] = dres_ref[...] + dx

        @pl.when(pl.program_id(0) == 0)
        def _():
            dnw_ref[...] = jnp.zeros_like(dnw_ref)

        dnw_ref[...] += jnp.sum(dh * nx, axis=0, keepdims=True)

    in_specs = [pl.BlockSpec((tm, width), lambda i: (i, 0)) for _, width in pieces]
    in_specs += [pl.BlockSpec((None, d, width), (lambda blk: (lambda i: (layer, 0, blk)))(off // width))
                 for off, width in pieces]
    in_specs += [pl.BlockSpec((tm, d), lambda i: (i, 0)), pl.BlockSpec((1, d), lambda i: (0, 0)),
                 pl.BlockSpec((tm, d), lambda i: (i, 0))]
    return pl.pallas_call(
        body, name=name, grid=(t // tm,), in_specs=in_specs,
        out_specs=[pl.BlockSpec((tm, d), lambda i: (i, 0)), pl.BlockSpec((1, d), lambda i: (0, 0))],
        out_shape=[jax.ShapeDtypeStruct((t, d), F32), jax.ShapeDtypeStruct((1, d), F32)],
        compiler_params=_params(("arbitrary",)),
    )(*lhs, *([w] * nl), x, nw, dres)


def matmul_tn(a, b, name, *, a_fn=None, by_chip=False):
    t, k = a.shape
    n = b.shape[1]
    tk = _tile(k, 1024, LANES)
    tn = _tile(n // N_CHIPS if by_chip else n, 1024, LANES)
    tt = _tile(t, 1024, 8)
    nt = t // tt
    fn = a_fn if a_fn is not None else (lambda v: v)

    def body(a_ref, b_ref, o_ref, acc_ref):
        @pl.when(pl.program_id(2) == 0)
        def _():
            acc_ref[...] = jnp.zeros_like(acc_ref)

        av = a_ref[...]
        if a_fn is not None:
            av = fn(av.astype(F32))
        acc_ref[...] += _dot_tn(av.astype(BF16), b_ref[...].astype(BF16))

        @pl.when(pl.program_id(2) == nt - 1)
        def _():
            o_ref[...] = acc_ref[...]

    if by_chip:
        per = n // N_CHIPS // tn
        out_spec = pl.BlockSpec((None, tk, tn), lambda i, j, s: (j // per, i, j % per))
        out_shape = jax.ShapeDtypeStruct((N_CHIPS, k, n // N_CHIPS), F32)
    else:
        out_spec = pl.BlockSpec((tk, tn), lambda i, j, s: (i, j))
        out_shape = jax.ShapeDtypeStruct((k, n), F32)
    return pl.pallas_call(
        body, name=name, grid=(k // tk, n // tn, nt),
        in_specs=[pl.BlockSpec((tt, tk), lambda i, j, s: (s, i)),
                  pl.BlockSpec((tt, tn), lambda i, j, s: (s, j))],
        out_specs=out_spec, out_shape=out_shape,
        scratch_shapes=[pltpu.VMEM((tk, tn), F32)],
        compiler_params=_params(("parallel", "parallel", "arbitrary")),
    )(a, b)


def split_to_chips(pieces, cols, name):
    d = pieces[0].shape[0]
    widths = [p.shape[1] for p in pieces]
    w = cols // N_CHIPS
    tr = _tile(d, 256, 8)
    npc = len(pieces)

    def body(*refs):
        o_ref, row = refs[npc], refs[npc + 1]
        off = 0
        for r, n in zip(refs[:npc], widths):
            row[:, off:off + n] = r[...]
            off += n
        for j in range(N_CHIPS):
            o_ref[j] = row[:, j * w:(j + 1) * w]

    return pl.pallas_call(
        body, name=name, grid=(d // tr,),
        in_specs=[pl.BlockSpec((tr, n), lambda i: (i, 0)) for n in widths],
        out_specs=pl.BlockSpec((N_CHIPS, tr, w), lambda i: (0, i, 0)),
        out_shape=jax.ShapeDtypeStruct((N_CHIPS, d, w), F32),
        scratch_shapes=[pltpu.VMEM((tr, sum(widths)), F32)],
        compiler_params=_params(("parallel",)),
    )(*pieces)


def join_from_chips(g4, npad, name):
    _, nl, d, w = g4.shape
    tr = _tile(d, 256, HALO)

    def body(g_ref, o_ref):
        for j in range(N_CHIPS):
            o_ref[:, j * w:(j + 1) * w] = g_ref[j]
        o_ref[:, N_CHIPS * w:] = jnp.zeros((tr, npad - N_CHIPS * w), o_ref.dtype)

    return pl.pallas_call(
        body, name=name, grid=(nl, d // tr),
        in_specs=[pl.BlockSpec((N_CHIPS, None, tr, w), lambda l, i: (0, l, i, 0))],
        out_specs=pl.BlockSpec((None, tr, npad), lambda l, i: (l, i, 0)),
        out_shape=jax.ShapeDtypeStruct((nl, d, npad), g4.dtype),
        compiler_params=_params(("parallel", "parallel")),
    )(g4)


def conv_mixer_fwd(proj, kw, cw, out_cols, name):
    t = proj.shape[0]
    tm = _tile(t, 1024, HALO)
    tc = _tile(cw, 1024, LANES)
    nj = cw // tc
    hb = tm // HALO

    def body(ub_ref, uc_ref, uh_ref, ucp_ref, uhp_ref, kw_ref, y_ref, ext):
        i = pl.program_id(0)
        v = uc_ref[...].astype(F32) * uh_ref[...].astype(F32)
        vp = ucp_ref[...].astype(F32) * uhp_ref[...].astype(F32)
        ext[0:HALO, :] = jnp.where(i > 0, vp, 0.0)
        ext[HALO:HALO + tm, :] = v
        cv = kw_ref[pl.ds(SHORT_K - 1, 1), :] * v
        for k in range(SHORT_K - 1):
            cv = cv + kw_ref[pl.ds(k, 1), :] * ext[pl.ds(HALO - (SHORT_K - 1) + k, tm), :]
        y_ref[...] = (ub_ref[...].astype(F32) * cv).astype(BF16)

    prev = lambda off: (lambda i, j: (jnp.maximum(i * hb - 1, 0), off + j))
    return pl.pallas_call(
        body, name=name, grid=(t // tm, nj),
        in_specs=[pl.BlockSpec((tm, tc), lambda i, j: (i, j)),
                  pl.BlockSpec((tm, tc), lambda i, j: (i, nj + j)),
                  pl.BlockSpec((tm, tc), lambda i, j: (i, 2 * nj + j)),
                  pl.BlockSpec((HALO, tc), prev(nj)),
                  pl.BlockSpec((HALO, tc), prev(2 * nj)),
                  pl.BlockSpec((SHORT_K, tc), lambda i, j: (0, j))],
        out_specs=pl.BlockSpec((tm, tc), lambda i, j: (i, j)),
        out_shape=jax.ShapeDtypeStruct((t, out_cols), BF16),
        scratch_shapes=[pltpu.VMEM((tm + HALO, tc), F32)],
        compiler_params=_params(("parallel", "parallel")),
    )(proj, proj, proj, proj, proj, kw)


def conv_mixer_bwd(proj, dy, kw, cw, name):
    t = proj.shape[0]
    tm = _tile(t, 1024, HALO)
    tc = _tile(cw, 512, LANES)
    nj = cw // tc
    hb = tm // HALO
    ni = t // tm
    last_hb = t // HALO - 1

    def body(ub_ref, uc_ref, uh_ref, dy_ref, ucp_ref, uhp_ref, ubn_ref, dyn_ref, kw_ref,
             dub_ref, duc_ref, duh_ref, dkw_ref, ext, extd):
        i = pl.program_id(1)
        ub = ub_ref[...].astype(F32)
        uc = uc_ref[...].astype(F32)
        uh = uh_ref[...].astype(F32)
        dyv = dy_ref[...].astype(F32)
        v = uc * uh
        vp = ucp_ref[...].astype(F32) * uhp_ref[...].astype(F32)
        ext[0:HALO, :] = jnp.where(i > 0, vp, 0.0)
        ext[HALO:HALO + tm, :] = v
        cv = kw_ref[pl.ds(SHORT_K - 1, 1), :] * v
        for k in range(SHORT_K - 1):
            cv = cv + kw_ref[pl.ds(k, 1), :] * ext[pl.ds(HALO - (SHORT_K - 1) + k, tm), :]
        dcv = dyv * ub
        dcvn = dyn_ref[...].astype(F32) * ubn_ref[...].astype(F32)
        extd[0:tm, :] = dcv
        extd[tm:tm + HALO, :] = jnp.where(i < ni - 1, dcvn, 0.0)
        dv = kw_ref[pl.ds(SHORT_K - 1, 1), :] * dcv
        for k in range(SHORT_K - 1):
            dv = dv + kw_ref[pl.ds(k, 1), :] * extd[pl.ds(SHORT_K - 1 - k, tm), :]
        dub_ref[...] = (dyv * cv).astype(BF16)
        duc_ref[...] = (dv * uh).astype(BF16)
        duh_ref[...] = (dv * uc).astype(BF16)

        @pl.when(i == 0)
        def _():
            dkw_ref[...] = jnp.zeros_like(dkw_ref)

        for k in range(SHORT_K):
            sh = ext[pl.ds(HALO - (SHORT_K - 1) + k, tm), :]
            dkw_ref[pl.ds(k, 1), :] += jnp.sum(dcv * sh, axis=0, keepdims=True)

    prev = lambda off: (lambda j, i: (jnp.maximum(i * hb - 1, 0), off + j))
    nxt = lambda off: (lambda j, i: (jnp.minimum((i + 1) * hb, last_hb), off + j))
    cur = lambda off: (lambda j, i: (i, off + j))
    return pl.pallas_call(
        body, name=name, grid=(nj, ni),
        in_specs=[pl.BlockSpec((tm, tc), cur(0)), pl.BlockSpec((tm, tc), cur(nj)),
                  pl.BlockSpec((tm, tc), cur(2 * nj)), pl.BlockSpec((tm, tc), cur(0)),
                  pl.BlockSpec((HALO, tc), prev(nj)), pl.BlockSpec((HALO, tc), prev(2 * nj)),
                  pl.BlockSpec((HALO, tc), nxt(0)), pl.BlockSpec((HALO, tc), nxt(0)),
                  pl.BlockSpec((SHORT_K, tc), lambda j, i: (0, j))],
        out_specs=[pl.BlockSpec((tm, tc), cur(0)), pl.BlockSpec((tm, tc), cur(0)),
                   pl.BlockSpec((tm, tc), cur(0)), pl.BlockSpec((SHORT_K, tc), lambda j, i: (0, j))],
        out_shape=[jax.ShapeDtypeStruct((t, cw), BF16)] * 3 + [jax.ShapeDtypeStruct((SHORT_K, cw), F32)],
        scratch_shapes=[pltpu.VMEM((tm + HALO, tc), F32), pltpu.VMEM((tm + HALO, tc), F32)],
        compiler_params=_params(("parallel", "arbitrary")),
    )(proj, proj, proj, dy, proj, proj, proj, dy, kw)


def _head_column(mat, lane, h):
    return jnp.sum(jnp.where(lane == h, mat, 0.0), axis=-1, keepdims=True)


def _ssd_common(dt_raw_ref, dtb_ref, aneg_ref, cum_s, cumt_s, chunk):
    dt = _softplus(dt_raw_ref[...] + dtb_ref[...])
    al = dt * aneg_ref[...]
    ri = lax.broadcasted_iota(jnp.int32, (chunk, chunk), 0)
    ci = lax.broadcasted_iota(jnp.int32, (chunk, chunk), 1)
    cum = _dot_exact((ri >= ci).astype(F32), al)
    cum_s[...] = cum
    cumt_s[...] = cum.T
    return dt, cum, ri >= ci


EDGE = 16


def _shift_matrices(shift_s, chunk, kk, up):
    ri = lax.broadcasted_iota(jnp.int32, (chunk, chunk), 0)
    ci = lax.broadcasted_iota(jnp.int32, (chunk, chunk), 1)
    for k in range(kk - 1):
        s = kk - 1 - k
        shift_s[k] = ((ci - ri if up else ri - ci) == s).astype(BF16)


def _causal_conv(cur, head, kw_ref, b_ref, shift_s, kk):
    acc = b_ref[...] + kw_ref[pl.ds(kk - 1, 1), :] * cur.astype(F32)
    top = b_ref[...] + kw_ref[pl.ds(kk - 1, 1), :] * head[pl.ds(8, EDGE), :]
    for k in range(kk - 1):
        acc = acc + kw_ref[pl.ds(k, 1), :] * _dot(shift_s[k], cur)
        top = top + kw_ref[pl.ds(k, 1), :] * head[pl.ds(8 - (kk - 1) + k, EDGE), :]
    return acc, top


def ssd_fwd(proj, dt_raw, y_mix, kw_xs, kw_bc, b_xs, b_bc, dtb, aneg, dskip, normw, cw, si, name, rider=None):
    t = proj.shape[0]
    ch = min(SCAN_CHUNK, t)
    nc = t // ch
    npair = si // LANES
    ppg = npair // GROUPS
    gn = GROUPS * STATE
    gw = si // GROUPS
    assert cw == si and (3 * cw + 2 * si) % (2 * gn) == 0
    zblk = 3 * cw // si
    xsblk = zblk + 1
    bcblk = (3 * cw + 2 * si) // (2 * gn)

    def body(z_ref, xs_ref, bc_ref, dtr_ref, ymix_ref, kwx_ref, kwb_ref, bx_ref, bb_ref, dtb_ref, aneg_ref, dsk_ref,
             nw_ref, yb_ref, ys_ref, hs_ref, xcx_ref, xcb_ref,
             headx, headb, shift_s, xs_s, bc_s, h_s, gated_s, s_s, cum_s, cumt_s):
        del ymix_ref
        c = pl.program_id(0)

        @pl.when(c == 0)
        def _():
            h_s[...] = jnp.zeros_like(h_s)
            headx[0:8, :] = jnp.zeros((8, si), F32)
            headb[0:8, :] = jnp.zeros((8, 2 * gn), F32)
            _shift_matrices(shift_s, ch, SSD_K, up=False)

        for raw_ref, head, kw_ref, b_ref, pre_ref, act_s in ((xs_ref, headx, kwx_ref, bx_ref, xcx_ref, xs_s),
                                                           (bc_ref, headb, kwb_ref, bb_ref, xcb_ref, bc_s)):
            head[8:8 + EDGE, :] = raw_ref[0:EDGE, :].astype(F32)
            pre, top = _causal_conv(raw_ref[...], head, kw_ref, b_ref, shift_s, SSD_K)
            head[0:8, :] = raw_ref[ch - EDGE:ch, :].astype(F32)[EDGE - 8:EDGE]
            pre_ref[...] = pre.astype(BF16)
            pre_ref[0:EDGE, :] = top.astype(BF16)
            act_s[...] = (pre * _sigmoid(pre)).astype(act_s.dtype)
            act_s[0:EDGE, :] = (top * _sigmoid(top)).astype(act_s.dtype)

        dt, cum, tril = _ssd_common(dtr_ref, dtb_ref, aneg_ref, cum_s, cumt_s, ch)
        lane = lax.broadcasted_iota(jnp.int32, (ch, LANES), 1)
        lane1 = lax.broadcasted_iota(jnp.int32, (1, LANES), 1)
        low = lane < HEAD_DIM
        clast = cum_s[pl.ds(ch - 1, 1), :]

        for p in range(npair):
            g = p // ppg
            col = slice(p * LANES, (p + 1) * LANES)
            bg = bc_s[:, g * STATE:(g + 1) * STATE]
            cg = bc_s[:, gn + g * STATE:gn + (g + 1) * STATE]
            if p % ppg == 0:
                s_s[...] = _dot_nt(cg, bg)
            heads = (PAIR * p, PAIR * p + 1)
            ccol = [_head_column(cum, lane, h) for h in heads]
            dcol = [_head_column(dt, lane, h) for h in heads]
            cl = [jnp.sum(jnp.where(lane1 == h, clast, 0.0), axis=-1, keepdims=True) for h in heads]
            cum_px = jnp.where(low, ccol[0], ccol[1])
            dt_px = jnp.where(low, dcol[0], dcol[1])
            cl_px = jnp.where(lane1 < HEAD_DIM, cl[0], cl[1])
            xs_p = xs_s[:, col]
            xdt = xs_p * dt_px
            y = dsk_ref[:, col] * xs_p
            for hi, h in enumerate(heads):
                dec = jnp.exp(jnp.where(tril, ccol[hi] - cumt_s[pl.ds(h, 1), :], -jnp.inf))
                wm = (s_s[...] * dec).astype(BF16)
                xm = jnp.where(low if hi == 0 else jnp.logical_not(low), xdt, 0.0).astype(BF16)
                y = y + _dot(wm, xm)
            hp = h_s[p]
            hs_ref[0, p] = hp
            y = y + _dot(cg, hp.astype(BF16)) * jnp.exp(cum_px)
            st = _dot_tn(bg, (xdt * jnp.exp(cl_px - cum_px)).astype(BF16))
            h_s[p] = jnp.exp(cl_px) * hp + st
            ys_ref[:, col] = y.astype(BF16)
            zp = z_ref[:, col].astype(F32)
            gated_s[:, col] = y * zp * _sigmoid(zp)

        for g in range(GROUPS):
            col = slice(g * gw, (g + 1) * gw)
            gg = gated_s[:, col]
            r = lax.rsqrt(jnp.mean(gg * gg, axis=-1, keepdims=True) + EPS)
            yb_ref[:, col] = (gg * r * nw_ref[:, col]).astype(BF16)

    full = lambda shape: pl.BlockSpec(shape, lambda c: tuple(0 for _ in shape))
    return _ride(
        body, (nc,),
        [pl.BlockSpec((ch, si), lambda c: (c, zblk)),
         pl.BlockSpec((ch, si), lambda c: (c, xsblk)),
         pl.BlockSpec((ch, 2 * gn), lambda c: (c, bcblk)),
         pl.BlockSpec((ch, LANES), lambda c: (c, 0)),
         pl.BlockSpec(memory_space=pl.ANY),
         full((SSD_K, si)), full((SSD_K, 2 * gn)), full((1, si)), full((1, 2 * gn)),
         full((1, LANES)), full((1, LANES)), full((1, si)), full((1, si))],
        [pl.BlockSpec((ch, si), lambda c: (c, cw // si)),
         pl.BlockSpec((ch, si), lambda c: (c, 0)),
         pl.BlockSpec((1, npair, STATE, LANES), lambda c: (c, 0, 0, 0)),
         pl.BlockSpec((ch, si), lambda c: (c, 0)), pl.BlockSpec((ch, 2 * gn), lambda c: (c, 0))],
        [jax.ShapeDtypeStruct(y_mix.shape, BF16), jax.ShapeDtypeStruct((t, si), BF16),
         jax.ShapeDtypeStruct((nc, npair, STATE, LANES), F32),
         jax.ShapeDtypeStruct((t, si), BF16), jax.ShapeDtypeStruct((t, 2 * gn), BF16)],
        [pltpu.VMEM((8 + EDGE, si), F32), pltpu.VMEM((8 + EDGE, 2 * gn), F32),
         pltpu.VMEM((SSD_K - 1, ch, ch), BF16),
         pltpu.VMEM((ch, si), F32), pltpu.VMEM((ch, 2 * gn), BF16),
         pltpu.VMEM((npair, STATE, LANES), F32), pltpu.VMEM((ch, si), F32),
         pltpu.VMEM((ch, ch), F32), pltpu.VMEM((ch, LANES), F32), pltpu.VMEM((LANES, ch), F32)],
        [proj, proj, proj, dt_raw, y_mix, kw_xs, kw_bc, b_xs, b_bc, dtb, aneg, dskip, normw], {4: 0}, rider,
        ("arbitrary",), name)


def ssd_bwd(proj, dt_raw, ys, hsave, pre_xs, pre_bc, dy, kw_xs, kw_bc, dtb, aneg, dskip, normw, cw, si, name,
            rider=None):
    t = proj.shape[0]
    ch = min(SCAN_CHUNK, t)
    nc = t // ch
    npair = si // LANES
    ppg = npair // GROUPS
    gn = GROUPS * STATE
    gw = si // GROUPS
    zblk = 3 * cw // si
    xsblk = zblk + 1
    bcblk = (3 * cw + 2 * si) // (2 * gn)

    def body(z_ref, xs_ref, bc_ref, xcx_ref, xcb_ref, dtr_ref, ys_ref, hs_ref, dyb_ref,
             kwx_ref, kwb_ref, dtb_ref, aneg_ref, dsk_ref, nw_ref,
             dz_ref, dxs_ref, dbc_ref, ddt_ref, dkwx_ref, dkwb_ref, dbx_ref, dbb_ref, ddtb_ref, da_ref, ddsk_ref,
             dnw_ref,
             tailx, tailb, shift_s, xs_s, bc_s, dsx_s, dsb_s, dy_s, dxs_s, dbc_s, dh_s, s_s, ds_s,
             cum_s, cumt_s, dccol_s, dcrow_s, ddtcol_s, dcl_s):
        i = pl.program_id(0)

        @pl.when(i == 0)
        def _():
            dh_s[...] = jnp.zeros_like(dh_s)
            tailx[EDGE:EDGE + 8, :] = jnp.zeros((8, si), F32)
            tailb[EDGE:EDGE + 8, :] = jnp.zeros((8, 2 * gn), F32)
            _shift_matrices(shift_s, ch, SSD_K, up=True)
            for r in (dkwx_ref, dkwb_ref, dbx_ref, dbb_ref, ddtb_ref, da_ref, ddsk_ref, dnw_ref):
                r[...] = jnp.zeros_like(r)

        xc = xcx_ref[...].astype(F32)
        sg = _sigmoid(xc)
        xs_s[...] = xc * sg
        dsx_s[...] = sg * (1.0 + xc * (1.0 - sg))
        bcc = xcb_ref[...].astype(F32)
        sgb = _sigmoid(bcc)
        bc_s[...] = (bcc * sgb).astype(BF16)
        dsb_s[...] = sgb * (1.0 + bcc * (1.0 - sgb))

        dt, cum, tril = _ssd_common(dtr_ref, dtb_ref, aneg_ref, cum_s, cumt_s, ch)
        lane = lax.broadcasted_iota(jnp.int32, (ch, LANES), 1)
        lane1 = lax.broadcasted_iota(jnp.int32, (1, LANES), 1)
        low = lane < HEAD_DIM
        low1 = lane1 < HEAD_DIM
        clast = cum_s[pl.ds(ch - 1, 1), :]

        for g in range(GROUPS):
            col = slice(g * gw, (g + 1) * gw)
            ysf = ys_ref[:, col].astype(F32)
            zf = z_ref[:, col].astype(F32)
            sz = _sigmoid(zf)
            silz = zf * sz
            gg = ysf * silz
            r = lax.rsqrt(jnp.mean(gg * gg, axis=-1, keepdims=True) + EPS)
            nrm = gg * r
            dyb = dyb_ref[:, col].astype(F32)
            dnw_ref[:, col] += jnp.sum(dyb * nrm, axis=0, keepdims=True)
            dn = dyb * nw_ref[:, col]
            dgg = r * (dn - nrm * jnp.mean(dn * nrm, axis=-1, keepdims=True))
            dy_s[:, col] = dgg * silz
            dz_ref[:, col] = (dgg * ysf * (sz * (1.0 + zf * (1.0 - sz)))).astype(BF16)

        dccol_s[...] = jnp.zeros_like(dccol_s)
        dcrow_s[...] = jnp.zeros_like(dcrow_s)
        ddtcol_s[...] = jnp.zeros_like(ddtcol_s)
        dcl_s[...] = jnp.zeros_like(dcl_s)
        dbc_s[...] = jnp.zeros_like(dbc_s)

        for p in range(npair):
            g = p // ppg
            col = slice(p * LANES, (p + 1) * LANES)
            bcol = slice(g * STATE, (g + 1) * STATE)
            ccolg = slice(gn + g * STATE, gn + (g + 1) * STATE)
            bg = bc_s[:, bcol]
            cg = bc_s[:, ccolg]
            if p % ppg == 0:
                s_s[...] = _dot_nt(cg, bg)
                ds_s[...] = jnp.zeros_like(ds_s)
            heads = (PAIR * p, PAIR * p + 1)
            masks = (low, jnp.logical_not(low))
            masks1 = (low1, jnp.logical_not(low1))
            ccol = [_head_column(cum, lane, h) for h in heads]
            dcol = [_head_column(dt, lane, h) for h in heads]
            cl = [jnp.sum(jnp.where(lane1 == h, clast, 0.0), axis=-1, keepdims=True) for h in heads]
            cum_px = jnp.where(low, ccol[0], ccol[1])
            dt_px = jnp.where(low, dcol[0], dcol[1])
            cl_px = jnp.where(low1, cl[0], cl[1])
            e_px = jnp.exp(cum_px)
            dec_end = jnp.exp(cl_px - cum_px)
            gdec = jnp.exp(cl_px)
            xs_p = xs_s[:, col]
            xdt = xs_p * dt_px
            dyp = dy_s[:, col]
            hc = hs_ref[0, p]
            hcb = hc.astype(BF16)
            dhn = dh_s[p]
            dhnb = dhn.astype(BF16)

            ddsk_ref[:, col] += jnp.sum(dyp * xs_p, axis=0, keepdims=True)
            dxs_acc = dsk_ref[:, col] * dyp
            dye = dyp * e_px
            dyeb = dye.astype(BF16)
            dbc_s[:, ccolg] += _dot_nt(dyeb, hcb)
            dcum_lane = dye * _dot(cg, hcb)
            dh_from_y = _dot_tn(cg, dyeb)
            xd = xdt * dec_end
            dxd = _dot(bg, dhnb)
            dbc_s[:, bcol] += _dot_nt(xd.astype(BF16), dhnb)
            dxdt = dxd * dec_end
            t1 = dxd * xd
            dcum_lane = dcum_lane - t1
            dcl_lane = jnp.sum(t1, axis=0, keepdims=True) + jnp.sum(dhn * hc, axis=0, keepdims=True) * gdec
            dh_s[p] = gdec * dhn + dh_from_y
            xdtb = xdt.astype(BF16)
            for hi, h in enumerate(heads):
                dym = jnp.where(masks[hi], dyp, 0.0).astype(BF16)
                dw = _dot_nt(dym, xdtb)
                dec = jnp.exp(jnp.where(tril, ccol[hi] - cumt_s[pl.ds(h, 1), :], -jnp.inf))
                wm = s_s[...] * dec
                dxdt = dxdt + _dot_tn(wm.astype(BF16), dym)
                ds_s[...] += dw * dec
                gm = dw * wm
                rowsum = jnp.sum(gm, axis=-1, keepdims=True)
                lanesum = jnp.sum(jnp.where(masks[hi], dcum_lane, 0.0), axis=-1, keepdims=True)
                dccol_s[...] += jnp.where(lane == h, rowsum + lanesum, 0.0)
                dcrow_s[pl.ds(h, 1), :] = jnp.sum(gm, axis=0, keepdims=True)
                dcl_h = jnp.sum(jnp.where(masks1[hi], dcl_lane, 0.0), axis=-1, keepdims=True)
                dcl_s[...] += jnp.where(lane1 == h, dcl_h, 0.0)
            ddt_lane = dxdt * xs_p
            for hi, h in enumerate(heads):
                s = jnp.sum(jnp.where(masks[hi], ddt_lane, 0.0), axis=-1, keepdims=True)
                ddtcol_s[...] += jnp.where(lane == h, s, 0.0)
            dxs_s[:, col] = dxs_acc + dxdt * dt_px
            if p % ppg == ppg - 1:
                dsb = ds_s[...].astype(BF16)
                dbc_s[:, ccolg] += _dot(dsb, bg)
                dbc_s[:, bcol] += _dot_tn(dsb, cg)

        rowi = lax.broadcasted_iota(jnp.int32, (ch, LANES), 0)
        dcum = dccol_s[...] - dcrow_s[...].T + jnp.where(rowi == ch - 1, dcl_s[...], 0.0)
        ri = lax.broadcasted_iota(jnp.int32, (ch, ch), 0)
        ci = lax.broadcasted_iota(jnp.int32, (ch, ch), 1)
        dal = _dot_exact((ri <= ci).astype(F32), dcum)
        ddt = dal * aneg_ref[...] + ddtcol_s[...]
        da_ref[...] += jnp.sum(dal * dt, axis=0, keepdims=True)
        ddtr = ddt * _sigmoid(dtr_ref[...] + dtb_ref[...])
        ddt_ref[...] = ddtr
        ddtb_ref[...] += jnp.sum(ddtr, axis=0, keepdims=True)

        for (dpost, dsl, tail, raw_ref, kw_ref, dkw_ref, db_ref, out_ref) in (
                (dxs_s, dsx_s, tailx, xs_ref, kwx_ref, dkwx_ref, dbx_ref, dxs_ref),
                (dbc_s, dsb_s, tailb, bc_ref, kwb_ref, dkwb_ref, dbb_ref, dbc_ref)):
            dxc = dpost[...] * dsl[...]
            dxcb = dxc.astype(BF16)
            raw = raw_ref[...].astype(F32)
            raw_end = raw_ref[ch - EDGE:ch, :].astype(F32)
            tail[0:EDGE, :] = dxcb[ch - EDGE:ch].astype(F32)
            db_ref[...] += jnp.sum(dxc, axis=0, keepdims=True)
            draw = kw_ref[pl.ds(SSD_K - 1, 1), :] * dxc
            dkw_ref[pl.ds(SSD_K - 1, 1), :] += jnp.sum(dxc * raw, axis=0, keepdims=True)
            fix = jnp.zeros((EDGE, dxc.shape[1]), F32)
            for k in range(SSD_K - 1):
                moved = _dot(shift_s[k], dxcb)
                miss = tail[pl.ds(SSD_K - 1 - k, EDGE), :] - moved[ch - EDGE:ch]
                draw = draw + kw_ref[pl.ds(k, 1), :] * moved
                fix = fix + kw_ref[pl.ds(k, 1), :] * miss
                dkw_ref[pl.ds(k, 1), :] += (jnp.sum(moved * raw, axis=0, keepdims=True)
                                            + jnp.sum(miss * raw_end, axis=0, keepdims=True))
            out_ref[...] = draw.astype(BF16)
            out_ref[ch - EDGE:ch, :] = (draw[ch - EDGE:ch] + fix).astype(BF16)
            tail[EDGE:EDGE + 8, :] = dxcb[0:EDGE].astype(F32)[0:8]

    full = lambda shape: pl.BlockSpec(shape, lambda i: tuple(0 for _ in shape))
    rev = lambda blk: (lambda i: (nc - 1 - i, blk))
    small_in = [(SSD_K, si), (SSD_K, 2 * gn), (1, LANES), (1, LANES), (1, si), (1, si)]
    small = [(SSD_K, si), (SSD_K, 2 * gn), (1, si), (1, 2 * gn), (1, LANES), (1, LANES), (1, si), (1, si)]
    return _ride(
        body, (nc,),
        [pl.BlockSpec((ch, si), rev(zblk)), pl.BlockSpec((ch, si), rev(xsblk)),
         pl.BlockSpec((ch, 2 * gn), rev(bcblk)),
         pl.BlockSpec((ch, si), rev(0)), pl.BlockSpec((ch, 2 * gn), rev(0)),
         pl.BlockSpec((ch, LANES), rev(0)), pl.BlockSpec((ch, si), rev(0)),
         pl.BlockSpec((1, npair, STATE, LANES), lambda i: (nc - 1 - i, 0, 0, 0)),
         pl.BlockSpec((ch, si), rev(cw // si))] + [full(s) for s in small_in],
        [pl.BlockSpec((ch, si), rev(0)), pl.BlockSpec((ch, si), rev(0)),
         pl.BlockSpec((ch, 2 * gn), rev(0)), pl.BlockSpec((ch, LANES), rev(0))] + [full(s) for s in small],
        [jax.ShapeDtypeStruct((t, si), BF16), jax.ShapeDtypeStruct((t, si), BF16),
         jax.ShapeDtypeStruct((t, 2 * gn), BF16), jax.ShapeDtypeStruct((t, LANES), F32)]
        + [jax.ShapeDtypeStruct(s, F32) for s in small],
        [pltpu.VMEM((EDGE + 8, si), F32), pltpu.VMEM((EDGE + 8, 2 * gn), F32),
         pltpu.VMEM((SSD_K - 1, ch, ch), BF16),
         pltpu.VMEM((ch, si), F32), pltpu.VMEM((ch, 2 * gn), BF16),
         pltpu.VMEM((ch, si), F32), pltpu.VMEM((ch, 2 * gn), F32),
         pltpu.VMEM((ch, si), F32), pltpu.VMEM((ch, si), F32), pltpu.VMEM((ch, 2 * gn), F32),
         pltpu.VMEM((npair, STATE, LANES), F32),
         pltpu.VMEM((ch, ch), F32), pltpu.VMEM((ch, ch), F32),
         pltpu.VMEM((ch, LANES), F32), pltpu.VMEM((LANES, ch), F32),
         pltpu.VMEM((ch, LANES), F32), pltpu.VMEM((LANES, ch), F32),
         pltpu.VMEM((ch, LANES), F32), pltpu.VMEM((1, LANES), F32)],
        [proj, proj, proj, pre_xs, pre_bc, dt_raw, ys, hsave, dy, kw_xs, kw_bc, dtb, aneg, dskip, normw],
        {}, rider, ("arbitrary",), name)


def final_loss(x, nw, tgt, name):
    t, d = x.shape
    tm = _tile(t, 512, 8)

    def body(x_ref, nw_ref, t_ref, dx_ref, dnw_ref, ls_ref):
        xf = x_ref[...]
        r = lax.rsqrt(jnp.mean(xf * xf, axis=-1, keepdims=True) + EPS)
        nx = xf * r
        e = nx * nw_ref[...] - t_ref[...]
        dyv = e * (1.0 / d)
        dn = dyv * nw_ref[...]
        dx_ref[...] = r * (dn - nx * jnp.mean(dn * nx, axis=-1, keepdims=True))

        @pl.when(pl.program_id(0) == 0)
        def _():
            dnw_ref[...] = jnp.zeros_like(dnw_ref)
            ls_ref[...] = jnp.zeros_like(ls_ref)

        dnw_ref[...] += jnp.sum(dyv * nx, axis=0, keepdims=True)
        ls_ref[...] += jnp.sum(e * e, axis=0, keepdims=True) * (0.5 / d)

    return pl.pallas_call(
        body, name=name, grid=(t // tm,),
        in_specs=[pl.BlockSpec((tm, d), lambda i: (i, 0)), pl.BlockSpec((1, d), lambda i: (0, 0)),
                  pl.BlockSpec((tm, d), lambda i: (i, 0))],
        out_specs=[pl.BlockSpec((tm, d), lambda i: (i, 0)), pl.BlockSpec((1, d), lambda i: (0, 0)),
                   pl.BlockSpec((1, d), lambda i: (0, 0))],
        out_shape=[jax.ShapeDtypeStruct((t, d), F32), jax.ShapeDtypeStruct((1, d), F32),
                   jax.ShapeDtypeStruct((1, d), F32)],
        compiler_params=_params(("arbitrary",)),
    )(x, nw, tgt)


def _rows3(a):
    if a.ndim == 1:
        return a.reshape(1, 1, a.shape[0])
    if a.ndim == 2:
        return a.reshape(1, *a.shape)
    return a.reshape(-1, a.shape[-2], a.shape[-1])


def adamw(w, g, m, v, name):
    shape = w.shape
    views = [_rows3(a) for a in (w, g, m, v)]
    b, r, c = views[0].shape
    tr = _tile(r, 256, 16) if r % 16 == 0 else r

    def body(w_ref, g_ref, m_ref, v_ref, d_ref, nm_ref, nv_ref):
        g = g_ref[...]
        m = ADAM_B1 * m_ref[...] + (1.0 - ADAM_B1) * g
        v = ADAM_B2 * v_ref[...] + (1.0 - ADAM_B2) * (g * g)
        m_hat = m / (1.0 - ADAM_B1 ** ADAM_STEP)
        v_hat = v / (1.0 - ADAM_B2 ** ADAM_STEP)
        d_ref[...] = -ADAM_LR * (m_hat / (jnp.sqrt(v_hat) + ADAM_EPS) + ADAM_WD * w_ref[...])
        nm_ref[...] = m
        nv_ref[...] = v

    spec = pl.BlockSpec((1, tr, c), lambda i, j: (i, j, 0))
    outs = pl.pallas_call(
        body, name=name, grid=(b, r // tr), in_specs=[spec] * 4, out_specs=[spec] * 3,
        out_shape=[jax.ShapeDtypeStruct((b, r, c), F32)] * 3,
        compiler_params=_params(("parallel", "parallel")),
    )(*views)
    return [o.reshape(shape) for o in outs]


def adamw_halves(w, g_mine, g_theirs, m, v, name):
    nl, r, c = w.shape
    r2 = r // 2
    tr = _tile(r2, 256, 16)
    nb = r2 // tr

    def body(w_ref, gm_ref, gt_ref, m_ref, v_ref, g_ref, d_ref, nm_ref, nv_ref):
        mine = (pl.program_id(1) // nb) == lax.axis_index("c")
        g = jnp.where(mine, gm_ref[...], gt_ref[...])
        m = ADAM_B1 * m_ref[...] + (1.0 - ADAM_B1) * g
        v = ADAM_B2 * v_ref[...] + (1.0 - ADAM_B2) * (g * g)
        m_hat = m / (1.0 - ADAM_B1 ** ADAM_STEP)
        v_hat = v / (1.0 - ADAM_B2 ** ADAM_STEP)
        g_ref[...] = g
        d_ref[...] = -ADAM_LR * (m_hat / (jnp.sqrt(v_hat) + ADAM_EPS) + ADAM_WD * w_ref[...])
        nm_ref[...] = m
        nv_ref[...] = v

    whole = pl.BlockSpec((1, tr, c), lambda l, i: (l, i, 0))
    half = pl.BlockSpec((1, tr, c), lambda l, i: (l, i % nb, 0))
    return pl.pallas_call(
        body, name=name, grid=(nl, 2 * nb), in_specs=[whole, half, half, whole, whole], out_specs=[whole] * 4,
        out_shape=[jax.ShapeDtypeStruct((nl, r, c), F32)] * 4,
        compiler_params=_params(("parallel", "parallel")),
    )(w, g_mine, g_theirs, m, v)


def _coords():
    return lax.axis_index("x"), lax.axis_index("y"), lax.axis_index("c")


def _ici_peers(x, y):
    chips = [(1 - x, y), (x, 1 - y), (1 - x, 1 - y)]
    return chips, [2 * cx + cy for cx, cy in chips]


def _place(ref, how, chip, layers, per):
    if how == "lead":
        return ref.at[chip, layers]
    start = pl.multiple_of(chip * per, per)
    if how == "rows":
        return ref.at[layers, pl.ds(start, per), :]
    return ref.at[layers, :, pl.ds(start, per)]


def gather_weights(shards, hows, name):
    na = len(shards)
    out_shape = []
    for s, how in zip(shards, hows):
        assert s.shape[0] % 2 == 0
        if how == "lead":
            shp = (N_CHIPS, *s.shape)
        elif how == "rows":
            shp = (s.shape[0], N_CHIPS * s.shape[1], s.shape[2])
        else:
            shp = (s.shape[0], s.shape[1], N_CHIPS * s.shape[2])
        out_shape.append(jax.ShapeDtypeStruct(shp, s.dtype))

    def body(*refs):
        ins = refs[:na]
        outs = refs[na:2 * na]
        send_sems, recv_sems = refs[2 * na:]
        x, y, c = _coords()
        me = 2 * x + y
        chips, chip_ids = _ici_peers(x, y)
        sibling = (x, y, 1 - c)

        def dst(a, chip, layers):
            per = {"lead": 0, "rows": ins[a].shape[1], "cols": ins[a].shape[-1]}[hows[a]]
            return _place(outs[a], hows[a], chip, layers, per)

        def copy(a, k, src, dst_ref, to):
            return pltpu.make_async_remote_copy(
                src_ref=src, dst_ref=dst_ref, send_sem=send_sems.at[7 * a + k], recv_sem=recv_sems.at[7 * a + k],
                device_id=to, device_id_type=MESH)

        started = []
        halves = []
        for a in range(na):
            nl = ins[a].shape[0]
            hl = nl // 2
            mine = pl.ds(c * hl, hl)
            theirs = pl.ds((1 - c) * hl, hl)
            halves.append((mine, theirs))
            for k in range(3):
                cp = copy(a, k, ins[a].at[mine], dst(a, me, mine), (*chips[k], c))
                cp.start()
                started.append(cp)
            own = copy(a, 6, ins[a], dst(a, me, pl.ds(0, nl)), sibling)
            own.start()
            started.append(own)
        for a in range(na):
            mine, _ = halves[a]
            for k in range(3):
                landed = dst(a, chip_ids[k], mine)
                copy(a, k, landed, landed, (*chips[k], c)).wait_recv()
                fw = copy(a, 3 + k, landed, landed, sibling)
                fw.start()
                started.append(fw)
        for a in range(na):
            _, theirs = halves[a]
            for k in range(3):
                got = dst(a, chip_ids[k], theirs)
                copy(a, 3 + k, got, got, sibling).wait_recv()
            whole = dst(a, me, pl.ds(0, ins[a].shape[0]))
            copy(a, 6, whole, whole, sibling).wait_recv()
        for cp in started:
            cp.wait_send()

    return pl.pallas_call(
        body, name=name, in_specs=_any_specs(na), out_specs=_any_specs(na), out_shape=out_shape,
        scratch_shapes=[pltpu.SemaphoreType.DMA((7 * na,)), pltpu.SemaphoreType.DMA((7 * na,))],
        compiler_params=pltpu.CompilerParams(has_side_effects=True),
    )(*shards)


def _remote(src, dst, send_sems, recv_sems, k, to):
    return pltpu.make_async_remote_copy(src_ref=src, dst_ref=dst, send_sem=send_sems.at[k], recv_sem=recv_sems.at[k],
                                        device_id=to, device_id_type=MESH)


LAYER_HOW = ("lead", "rows", "cols", "rows")


def _layer_place(ref, how, chip, shard_shape, start, size):
    r, c = shard_shape
    if how == "lead":
        return ref.at[chip, :, pl.ds(start, size), :]
    if how == "rows":
        return ref.at[:, pl.ds(pl.multiple_of(chip * r + start, HALO), size), :]
    return ref.at[:, pl.ds(start, size), pl.ds(pl.multiple_of(chip * c, LANES), c)]


def weight_rider_ici(shards, layer):
    shapes = [tuple(s.shape[1:]) for s in shards]
    out_shapes = []
    for (r, c), how, s in zip(shapes, LAYER_HOW, shards):
        shp = {"lead": (N_CHIPS, 1, r, c), "rows": (1, N_CHIPS * r, c), "cols": (1, r, N_CHIPS * c)}[how]
        out_shapes.append(jax.ShapeDtypeStruct(shp, s.dtype))

    def copies(ins, outs, send_sems, recv_sems):
        x, y, c = _coords()
        me = 2 * x + y
        chips, chip_ids = _ici_peers(x, y)
        sibling = (x, y, 1 - c)
        pairs = []
        for a, (shape, how) in enumerate(zip(shapes, LAYER_HOW)):
            half = shape[0] // 2
            mine = pl.multiple_of(c * half, HALO)
            src = ins[a].at[pl.ds(layer, 1)]
            for k in range(3):
                to = (*chips[k], c)
                land = _layer_place(outs[a], how, chip_ids[k], shape, mine, half)
                pairs.append((_remote(src.at[:, pl.ds(mine, half), :], _layer_place(outs[a], how, me, shape, mine, half),
                                      send_sems, recv_sems, 4 * a + k, to),
                              _remote(land, land, send_sems, recv_sems, 4 * a + k, to)))
            whole = _layer_place(outs[a], how, me, shape, 0, shape[0])
            pairs.append((_remote(src, whole, send_sems, recv_sems, 4 * a + 3, sibling),
                          _remote(whole, whole, send_sems, recv_sems, 4 * a + 3, sibling)))
        return pairs

    return Rider(list(shards), out_shapes, {}, 4 * len(shards), copies)


def weight_rider_d2d(bufs, shapes):
    def copies(ins, outs, send_sems, recv_sems):
        x, y, c = _coords()
        _, chip_ids = _ici_peers(x, y)
        sibling = (x, y, 1 - c)
        pairs = []
        for a, (shape, how) in enumerate(zip(shapes, LAYER_HOW)):
            half = shape[0] // 2
            mine = pl.multiple_of(c * half, HALO)
            theirs = pl.multiple_of((1 - c) * half, HALO)
            for k in range(3):
                land = _layer_place(outs[a], how, chip_ids[k], shape, theirs, half)
                pairs.append((_remote(_layer_place(ins[a], how, chip_ids[k], shape, mine, half),
                                      _layer_place(outs[a], how, chip_ids[k], shape, mine, half),
                                      send_sems, recv_sems, 3 * a + k, sibling),
                              _remote(land, land, send_sems, recv_sems, 3 * a + k, sibling)))
        return pairs

    return Rider(list(bufs), [jax.ShapeDtypeStruct(b.shape, b.dtype) for b in bufs],
                 {a: a for a in range(len(bufs))}, 3 * len(bufs), copies)


def grads_rider_sibling(arrs):
    def copies(ins, outs, send_sems, recv_sems):
        x, y, c = _coords()
        sibling = (x, y, 1 - c)
        pairs = []
        for a in range(len(arrs)):
            r2 = ins[a].shape[1] // 2
            src = ins[a].at[:, pl.ds(pl.multiple_of((1 - c) * r2, 8), r2), :]
            pairs.append((_remote(src, outs[a], send_sems, recv_sems, a, sibling),
                          _remote(outs[a], outs[a], send_sems, recv_sems, a, sibling)))
        return pairs

    return Rider(list(arrs), [jax.ShapeDtypeStruct((a.shape[0], a.shape[1] // 2, a.shape[2]), a.dtype) for a in arrs],
                 {}, len(arrs), copies)


def chip_sum(g, recv, name):
    nch, r, c = g.shape
    r2 = r // 2
    tr = _tile(r2, 256, 16)
    nb = r2 // tr

    def body(g0_ref, g1_ref, r_ref, o32_ref, o16_ref):
        s = jnp.where(lax.axis_index("c") == 0, g0_ref[...], g1_ref[...]) + r_ref[...]
        o32_ref[...] = s
        o16_ref[...] = s.astype(BF16)

    here = pl.BlockSpec((1, tr, c), lambda i, j: (i, j, 0))
    return pl.pallas_call(
        body, name=name, grid=(nch, nb),
        in_specs=[here, pl.BlockSpec((1, tr, c), lambda i, j: (i, nb + j, 0)), here],
        out_specs=[here, here],
        out_shape=[jax.ShapeDtypeStruct((nch, r2, c), F32), jax.ShapeDtypeStruct((nch, r2, c), BF16)],
        compiler_params=_params(("parallel", "parallel")),
    )(g, g, recv)


def grads_rider_chips(arrs):
    def copies(ins, outs, send_sems, recv_sems):
        x, y, c = _coords()
        chips, chip_ids = _ici_peers(x, y)
        pairs = []
        for a in range(len(arrs)):
            for k in range(3):
                to = (*chips[k], c)
                pairs.append((_remote(ins[a].at[chip_ids[k]], outs[a].at[k], send_sems, recv_sems, 3 * a + k, to),
                              _remote(outs[a].at[k], outs[a].at[k], send_sems, recv_sems, 3 * a + k, to)))
        return pairs

    return Rider(list(arrs), [jax.ShapeDtypeStruct((3, *a.shape[1:]), a.dtype) for a in arrs], {}, 3 * len(arrs),
                 copies)


def grad_sum(p32, recv, layer, nl, buf, name):
    _, r2, c = p32.shape
    tr = _tile(r2, 256, 16)
    nb = r2 // tr

    def body(p0_ref, p1_ref, p2_ref, p3_ref, r0_ref, r1_ref, r2_ref, *rest):
        o_ref = rest[-1]
        x, y, _ = _coords()
        chip = 2 * x + y
        own = jnp.where(chip == 0, p0_ref[...], jnp.where(chip == 1, p1_ref[...],
                                                        jnp.where(chip == 2, p2_ref[...], p3_ref[...])))
        o_ref[...] = own + r0_ref[...].astype(F32) + r1_ref[...].astype(F32) + r2_ref[...].astype(F32)

    slot = lambda k: pl.BlockSpec((1, tr, c), lambda j: (k, j, 0))
    in_specs = [slot(k) for k in range(N_CHIPS)] + [slot(k) for k in range(3)]
    args = [p32] * N_CHIPS + [recv] * 3
    aliases = {}
    if buf is not None:
        in_specs.append(pl.BlockSpec(memory_space=pl.ANY))
        args.append(buf)
        aliases = {len(args) - 1: 0}
    return pl.pallas_call(
        body, name=name, grid=(nb,), in_specs=in_specs,
        out_specs=pl.BlockSpec((1, tr, c), lambda j: (layer, j, 0)),
        out_shape=jax.ShapeDtypeStruct((nl, r2, c), F32),
        input_output_aliases=aliases,
        compiler_params=_params(("parallel",)),
    )(*args)


def grads_rider_exchange(bufs):
    def copies(ins, outs, send_sems, recv_sems):
        x, y, c = _coords()
        sibling = (x, y, 1 - c)
        return [(_remote(ins[a], outs[a], send_sems, recv_sems, a, sibling),
                 _remote(outs[a], outs[a], send_sems, recv_sems, a, sibling)) for a in range(len(bufs))]

    return Rider(list(bufs), [jax.ShapeDtypeStruct(b.shape, b.dtype) for b in bufs], {}, len(bufs), copies)


def allreduce_small(buf, name):
    r, cdim = buf.shape

    def body(x_ref, o_ref, gath, send_sems, recv_sems):
        x, y, c = _coords()
        me, sibling = (x, y, c), (x, y, 1 - c)
        chips, _ = _ici_peers(x, y)

        def slot(px, py, pc):
            return gath.at[4 * px + 2 * py + pc]

        def copy(k, block, to, src=None):
            return pltpu.make_async_remote_copy(
                src_ref=slot(*block) if src is None else src, dst_ref=slot(*block),
                send_sem=send_sems.at[k], recv_sem=recv_sems.at[k], device_id=to, device_id_type=MESH)

        gath[4 * x + 2 * y + c] = x_ref[...]
        first = [copy(0, me, sibling, src=x_ref)]
        first += [copy(1 + j, me, (*chip, c), src=x_ref) for j, chip in enumerate(chips)]
        for cp in first:
            cp.start()
        passed = [copy(4 + j, (*chip, c), sibling) for j, chip in enumerate(chips)]
        for j, chip in enumerate(chips):
            copy(1 + j, (*chip, c), me).wait_recv()
            passed[j].start()
        copy(0, sibling, me).wait_recv()
        for j, chip in enumerate(chips):
            copy(4 + j, (*chip, 1 - c), me).wait_recv()
        for cp in first + passed:
            cp.wait_send()
        acc = gath[0]
        for d in range(1, 8):
            acc = acc + gath[d]
        o_ref[...] = acc

    return pl.pallas_call(
        body, name=name,
        in_specs=[pl.BlockSpec(memory_space=pltpu.VMEM)], out_specs=pl.BlockSpec(memory_space=pltpu.VMEM),
        out_shape=jax.ShapeDtypeStruct((r, cdim), F32),
        scratch_shapes=[pltpu.VMEM((8, r, cdim), F32), pltpu.SemaphoreType.DMA((7,)), pltpu.SemaphoreType.DMA((7,))],
        compiler_params=pltpu.CompilerParams(has_side_effects=True),
    )(buf)


def _expand_heads(v):
    return jnp.repeat(v.astype(F32), HEAD_DIM).reshape(1, -1)


def _pad_lanes(v):
    return jnp.pad(v.astype(F32), (0, LANES - v.shape[0])).reshape(1, LANES)


def local_step(x, tgt, p, comm, cols):
    nl = p["norm_mix_w"].shape[0]
    d = x.shape[1]
    cw = p["short_conv_w"].shape[2]
    si = p["ssd_norm_w"].shape[1]
    w0 = comm.weights(0)
    ff = w0["w_up"].shape[2]
    nh = si // HEAD_DIM
    gn = GROUPS * STATE
    npad = w0["w_in"].shape[2]
    dt_off = 3 * cw + si + si + 2 * gn
    assert cols == dt_off + nh and nh <= LANES and dt_off % LANES == 0 and npad == dt_off + LANES
    pieces = [(0, cw), (cw, cw), (2 * cw, cw), (3 * cw, si), (3 * cw + si, si), (3 * cw + 2 * si, 2 * gn),
              (dt_off, LANES)]

    saved = []
    for l in range(nl):
        nw1 = p["norm_mix_w"][l].reshape(1, d)
        nw2 = p["norm_mlp_w"][l].reshape(1, d)
        kw3 = p["short_conv_w"][l]
        kwx, kwb = p["ssd_conv_w"][l][:, :si], p["ssd_conv_w"][l][:, si:]
        bx, bb = p["ssd_conv_b"][l][:si].reshape(1, si), p["ssd_conv_b"][l][si:].reshape(1, 2 * gn)
        dtb = _pad_lanes(p["dt_bias"][l])
        aneg = _pad_lanes(-jnp.exp(p["a_log"][l]))
        dsk = _expand_heads(p["d_skip"][l])
        snw = p["ssd_norm_w"][l].reshape(1, si)
        ssd_args = (kwx, kwb, bx, bb, dtb, aneg, dsk, snw)

        wl = comm.weights(l)
        (proj, h), sent = norm_matmul(x, nw1, wl["w_in"], 0, dt_off, 0, BF16, "in_proj",
                                      rider=comm.forward_rider_a(l))
        (dt_raw,), _ = norm_matmul(x, nw1, wl["w_in"], 0, LANES, dt_off // LANES, F32, "dt_proj", emit_h=False)
        y_mix = conv_mixer_fwd(proj, kw3, cw, cw + si, "conv_mixer_fwd")
        (y_mix, *ssd_saved), sent = ssd_fwd(proj, dt_raw, y_mix, *ssd_args, cw, si, "ssd_fwd",
                                            rider=comm.forward_rider_b(l, sent))
        comm.forward_done(l, sent)
        x2, _ = matmul(y_mix, wl["w_out"], 0, False, d, F32, "out_proj", residual=x)
        (up, h2), _ = norm_matmul(x2, nw2, wl["w_up"], 0, ff, 0, BF16, "up_proj")
        x3, _ = matmul(up, wl["w_down"], 0, False, d, F32, "down_proj", lhs_fn=_relu2, residual=x2)
        saved.append((x, h, proj, dt_raw, y_mix, ssd_saved, x2, h2, up, nw1, nw2, kw3, ssd_args))
        x = x3

    dx, dwf, lvec = final_loss(x, p["final_norm_w"].reshape(1, d), tgt, "final_loss")
    loss = jnp.sum(lvec)

    names = ("norm_mix_w", "short_conv_w", "ssd_conv_w", "ssd_conv_b", "dt_bias", "a_log", "d_skip",
             "ssd_norm_w", "norm_mlp_w")
    grads = {k: [None] * nl for k in names}
    for l in reversed(range(nl)):
        x0, h, proj, dt_raw, y_mix, ssd_saved, x2, h2, up, nw1, nw2, kw3, ssd_args = saved[l]
        kwx, kwb, _, _, dtb, aneg, dsk, snw = ssd_args
        wl = comm.weights(l)
        dup, sent = matmul(dx, wl["w_down"], 0, True, ff, BF16, "down_bwd", relu_gate=up,
                           rider=comm.backward_rider_a(l))
        comm.backward_a_done(l, sent)
        g_down = matmul_tn(up, dx, "down_wgrad", a_fn=_relu2)
        dx2, dnw2 = matmul_normbwd([dup], [(0, ff)], wl["w_up"], 0, x2, nw2, dx, "up_bwd")
        g_up = matmul_tn(h2, dup, "up_wgrad", by_chip=True)
        grads["norm_mlp_w"][l] = dnw2.reshape(d)
        dy, _ = matmul(dx2, wl["w_out"], 0, True, cw + si, BF16, "out_bwd")
        g_out = matmul_tn(y_mix, dx2, "out_wgrad")
        dub, duc, duh, dkw3 = conv_mixer_bwd(proj, dy, kw3, cw, "conv_mixer_bwd")
        (dz, dxs, dbc, ddt, dkwx, dkwb, dbx, dbb, ddtb, da, ddsk, dsnw), sent = ssd_bwd(
            proj, dt_raw, *ssd_saved, dy, kwx, kwb, dtb, aneg, dsk, snw, cw, si, "ssd_bwd",
            rider=comm.backward_rider_b(l))
        comm.backward_b_done(l, sent)
        dpieces = [dub, duc, duh, dz, dxs, dbc, ddt]
        dxl, dnw1 = matmul_normbwd(dpieces, pieces, wl["w_in"], 0, x0, nw1, dx2, "in_bwd")
        g_in = split_to_chips(
            [matmul_tn(h, dp, "in_wgrad_%d" % i) for i, dp in enumerate(dpieces)], cols, "in_wgrad_split")
        comm.take_gradients(l, dict(w_in=g_in, w_out=g_out, w_up=g_up, w_down=g_down))
        grads["norm_mix_w"][l] = dnw1.reshape(d)
        grads["short_conv_w"][l] = dkw3
        grads["ssd_conv_w"][l] = jnp.concatenate([dkwx, dkwb], axis=1)
        grads["ssd_conv_b"][l] = jnp.concatenate([dbx, dbb], axis=1).reshape(-1)
        grads["dt_bias"][l] = ddtb[0, :nh]
        grads["a_log"][l] = da[0, :nh] * aneg[0, :nh]
        grads["d_skip"][l] = jnp.sum(ddsk.reshape(nh, HEAD_DIM), axis=1)
        grads["ssd_norm_w"][l] = dsnw.reshape(si)
        dx = dxl

    grads = {k: jnp.stack(v) for k, v in grads.items()}
    grads["final_norm_w"] = dwf.reshape(d)
    return loss, dx, grads


BIG = ("w_in", "w_out", "w_up", "w_down")
SMALL_SHARDED = ("short_conv_w", "ssd_conv_w")
SMALL_REPL = ("norm_mix_w", "ssd_conv_b", "dt_bias", "a_log", "d_skip", "ssd_norm_w", "norm_mlp_w", "final_norm_w")
WEIGHTS = ("norm_mix_w", "w_in", "short_conv_w", "ssd_conv_w", "ssd_conv_b", "dt_bias", "a_log", "d_skip",
           "ssd_norm_w", "w_out", "norm_mlp_w", "w_up", "w_down", "final_norm_w")
SMALL_COLS = 1024


def _pack_small(named):
    flat = jnp.concatenate([v.reshape(-1).astype(F32) for v in named])
    n = flat.shape[0]
    rows = -(-n // SMALL_COLS)
    rows = -(-rows // 8) * 8
    return jnp.pad(flat, (0, rows * SMALL_COLS - n)).reshape(rows, SMALL_COLS)


def _unpack_small(buf, like):
    flat = buf.reshape(-1)
    out, off = [], 0
    for v in like:
        out.append(flat[off:off + v.size].reshape(v.shape))
        off += v.size
    return out


class ChipComm:
    def __init__(self, shards, nl, npad):
        self.shards, self.nl, self.npad = shards, nl, npad
        self.shapes = [tuple(s.shape[1:]) for s in shards]
        self.w = {}
        self.pending = None
        self.sums = None
        self.bufs = [None] * len(shards)
        landed = run_rider(weight_rider_ici(shards, 0), "gather_first_ici")
        self.forward_done(-1, run_rider(weight_rider_d2d(landed, self.shapes), "gather_first_d2d"))

    def weights(self, l):
        return self.w[l]

    def forward_rider_a(self, l):
        return weight_rider_ici(self.shards, l + 1) if l + 1 < self.nl else None

    def forward_rider_b(self, l, landed):
        return weight_rider_d2d(landed, self.shapes) if landed else None

    def forward_done(self, l, gathered):
        if gathered:
            w_in = join_from_chips(gathered[0], self.npad, "w_in_join")
            self.w[l + 1] = dict(w_in=w_in, w_out=gathered[1], w_up=gathered[2], w_down=gathered[3])

    def take_gradients(self, l, g):
        by_chip = [g[k] if g[k].ndim == 3 else g[k].reshape(N_CHIPS, g[k].shape[0] // N_CHIPS, g[k].shape[1])
                   for k in BIG]
        self.pending = (l, by_chip)

    def backward_rider_a(self, l):
        return grads_rider_sibling(self.pending[1]) if self.pending else None

    def backward_a_done(self, l, from_sibling):
        if from_sibling:
            self.sums = [chip_sum(g, r, "chip_sum") for g, r in zip(self.pending[1], from_sibling)]

    def backward_rider_b(self, l):
        return grads_rider_chips([s[1] for s in self.sums]) if self.pending else None

    def backward_b_done(self, l, from_chips):
        if from_chips:
            layer = self.pending[0]
            self.bufs = [grad_sum(s[0], r, layer, self.nl, buf, "grad_sum")
                         for s, r, buf in zip(self.sums, from_chips, self.bufs)]
            self.pending = None

    def finish(self):
        self.backward_a_done(None, run_rider(self.backward_rider_a(None), "grads_to_sibling"))
        self.backward_b_done(None, run_rider(self.backward_rider_b(None), "grads_to_chips"))
        theirs = run_rider(grads_rider_exchange(self.bufs), "grads_exchange")
        return dict(zip(BIG, zip(self.bufs, theirs)))


def kernel(x, norm_mix_w, w_in, short_conv_w, ssd_conv_w, ssd_conv_b, dt_bias, a_log, d_skip, ssd_norm_w, w_out, norm_mlp_w, w_up, w_down, final_norm_w, loss_target, m_norm_mix_w, m_w_in, m_short_conv_w, m_ssd_conv_w, m_ssd_conv_b, m_dt_bias, m_a_log, m_d_skip, m_ssd_norm_w, m_w_out, m_norm_mlp_w, m_w_up, m_w_down, m_final_norm_w, v_norm_mix_w, v_w_in, v_short_conv_w, v_ssd_conv_w, v_ssd_conv_b, v_dt_bias, v_a_log, v_d_skip, v_ssd_norm_w, v_w_out, v_norm_mlp_w, v_w_up, v_w_down, v_final_norm_w):
    w = dict(norm_mix_w=norm_mix_w, w_in=w_in, short_conv_w=short_conv_w, ssd_conv_w=ssd_conv_w,
             ssd_conv_b=ssd_conv_b, dt_bias=dt_bias, a_log=a_log, d_skip=d_skip, ssd_norm_w=ssd_norm_w, w_out=w_out,
             norm_mlp_w=norm_mlp_w, w_up=w_up, w_down=w_down, final_norm_w=final_norm_w)
    m = dict(norm_mix_w=m_norm_mix_w, w_in=m_w_in, short_conv_w=m_short_conv_w, ssd_conv_w=m_ssd_conv_w,
             ssd_conv_b=m_ssd_conv_b, dt_bias=m_dt_bias, a_log=m_a_log, d_skip=m_d_skip, ssd_norm_w=m_ssd_norm_w,
             w_out=m_w_out, norm_mlp_w=m_norm_mlp_w, w_up=m_w_up, w_down=m_w_down, final_norm_w=m_final_norm_w)
    v = dict(norm_mix_w=v_norm_mix_w, w_in=v_w_in, short_conv_w=v_short_conv_w, ssd_conv_w=v_ssd_conv_w,
             ssd_conv_b=v_ssd_conv_b, dt_bias=v_dt_bias, a_log=v_a_log, d_skip=v_d_skip, ssd_norm_w=v_ssd_norm_w,
             w_out=v_w_out, norm_mlp_w=v_norm_mlp_w, w_up=v_w_up, w_down=v_w_down, final_norm_w=v_final_norm_w)
    xi, yi, ci = lax.axis_index("x"), lax.axis_index("y"), lax.axis_index("c")
    chip = 2 * xi + yi
    nl = w_up.shape[0]
    cols = N_CHIPS * w_in.shape[2]
    npad = cols // LANES * LANES + LANES

    full = dict(w)
    small_gathered = gather_weights([w[k] for k in SMALL_SHARDED], ["lead"] * len(SMALL_SHARDED), "gather_small")
    for k, g4 in zip(SMALL_SHARDED, small_gathered):
        full[k] = jnp.concatenate([g4[j] for j in range(N_CHIPS)], axis=2)
    comm = ChipComm([w[k].astype(BF16) for k in BIG], nl, npad)

    loss, grad_x, grads = local_step(x[0], loss_target[0], full, comm, cols)
    loss = lax.psum(loss, ("x", "y", "c"))
    halves = comm.finish()
    g_shard = {}

    small_names = SMALL_REPL + SMALL_SHARDED
    small_sum = allreduce_small(_pack_small([grads[k] for k in small_names]), "allreduce_small")
    for k, g in zip(small_names, _unpack_small(small_sum, [grads[k] for k in small_names])):
        if k in SMALL_SHARDED:
            width = w[k].shape[2]
            g = lax.dynamic_slice_in_dim(g, chip * width, width, axis=2)
        g_shard[k] = g

    delta, new_m, new_v = {}, {}, {}
    for k in BIG:
        g_shard[k], delta[k], new_m[k], new_v[k] = adamw_halves(w[k], *halves[k], m[k], v[k], "adamw_%s" % k)
    packed = [_pack_small([d_[k] for k in small_names]) for d_ in (w, g_shard, m, v)]
    outs = adamw(*packed, "adamw_small")
    for d_, buf in zip((delta, new_m, new_v), outs):
        for k, val in zip(small_names, _unpack_small(buf, [w[k] for k in small_names])):
            d_[k] = val

    return (loss, grad_x[None], *[g_shard[k] for k in WEIGHTS], *[delta[k] for k in WEIGHTS],
            *[new_m[k] for k in WEIGHTS], *[new_v[k] for k in WEIGHTS])
```

```python
import functools

import jax
import jax.numpy as jnp
from jax import lax
from jax.experimental import pallas as pl
from jax.experimental.pallas import tpu as pltpu

F32 = jnp.float32
BF16 = jnp.bfloat16

EPS = 1e-5
HEAD_DIM = 64
STATE = 128
GROUPS = 2
SHORT_K = 3
SSD_K = 4
LANES = 128
PAIR = LANES // HEAD_DIM
SCAN_CHUNK = 256
HALO = 16
N_CHIPS = 4
VMEM_LIMIT = 56 * 1024 * 1024

ADAM_LR = 0.001
ADAM_B1 = 0.9
ADAM_B2 = 0.999
ADAM_EPS = 1e-08
ADAM_WD = 0.01
ADAM_STEP = 10

MESH = pl.DeviceIdType.MESH


def _params(sem):
    return pltpu.CompilerParams(dimension_semantics=sem, vmem_limit_bytes=VMEM_LIMIT)


def _tile(n, cap, quantum):
    if n <= cap:
        return n
    best = None
    for t in range(quantum, cap + 1, quantum):
        if n % t == 0:
            best = t
    assert best is not None, (n, cap, quantum)
    return best


def _dot(a, b):
    return jnp.dot(a, b, preferred_element_type=F32)


def _dot_nt(a, b):
    return lax.dot_general(a, b, (((1,), (1,)), ((), ())), preferred_element_type=F32)


def _dot_tn(a, b):
    return lax.dot_general(a, b, (((0,), (0,)), ((), ())), preferred_element_type=F32)


def _dot_exact(a, b):
    return jnp.dot(a, b, precision=lax.Precision.HIGHEST, preferred_element_type=F32)


def _sigmoid(x):
    return pl.reciprocal(1.0 + jnp.exp(-x), approx=True)


def _softplus(x):
    return jnp.maximum(x, 0.0) + jnp.log(1.0 + jnp.exp(-jnp.abs(x)))


def _relu2(v):
    return jnp.square(jnp.maximum(v, 0.0))


class Rider:
    def __init__(self, ins, out_shapes, aliases, n_sems, copies):
        self.ins, self.out_shapes, self.aliases, self.n_sems, self.copies = ins, out_shapes, aliases, n_sems, copies


def _any_specs(n):
    return [pl.BlockSpec(memory_space=pl.ANY)] * n


def _ride(body, grid, in_specs, out_specs, out_shape, scratch, args, aliases, rider, sem, name):
    n_in, n_out, n_scr = len(in_specs), len(out_specs), len(scratch)
    if rider is None:
        outs = pl.pallas_call(
            body, name=name, grid=grid, in_specs=in_specs, out_specs=out_specs, out_shape=out_shape,
            scratch_shapes=scratch, input_output_aliases=aliases, compiler_params=_params(sem))(*args)
        return list(outs), []
    ri, ro = len(rider.ins), len(rider.out_shapes)
    last = tuple(g - 1 for g in grid)

    def wrapped(*refs):
        ins = refs[:n_in]
        r_ins = refs[n_in:n_in + ri]
        outs = refs[n_in + ri:n_in + ri + n_out]
        r_outs = refs[n_in + ri + n_out:n_in + ri + n_out + ro]
        scr = refs[n_in + ri + n_out + ro:n_in + ri + n_out + ro + n_scr]
        send_sems, recv_sems = refs[-2:]
        ids = [pl.program_id(a) for a in range(len(grid))]
        at_first = functools.reduce(jnp.logical_and, [i == 0 for i in ids])
        at_last = functools.reduce(jnp.logical_and, [i == e for i, e in zip(ids, last)])

        @pl.when(at_first)
        def _():
            for cp, _ in rider.copies(r_ins, r_outs, send_sems, recv_sems):
                cp.start()

        body(*ins, *outs, *scr)

        @pl.when(at_last)
        def _():
            for cp, landed in rider.copies(r_ins, r_outs, send_sems, recv_sems):
                cp.wait_send()
                landed.wait_recv()

    all_aliases = dict(aliases)
    all_aliases.update({n_in + a: n_out + b for a, b in rider.aliases.items()})
    outs = pl.pallas_call(
        wrapped, name=name, grid=grid, in_specs=list(in_specs) + _any_specs(ri),
        out_specs=list(out_specs) + _any_specs(ro), out_shape=list(out_shape) + list(rider.out_shapes),
        scratch_shapes=list(scratch) + [pltpu.SemaphoreType.DMA((rider.n_sems,)),
                                        pltpu.SemaphoreType.DMA((rider.n_sems,))],
        input_output_aliases=all_aliases, compiler_params=_params(sem))(*args, *rider.ins)
    return list(outs[:n_out]), list(outs[n_out:])


def run_rider(rider, name):
    ri, ro = len(rider.ins), len(rider.out_shapes)

    def body(*refs):
        send_sems, recv_sems = refs[-2:]
        pairs = rider.copies(refs[:ri], refs[ri:ri + ro], send_sems, recv_sems)
        for cp, _ in pairs:
            cp.start()
        for cp, landed in pairs:
            cp.wait_send()
            landed.wait_recv()

    return list(pl.pallas_call(
        body, name=name, in_specs=_any_specs(ri), out_specs=_any_specs(ro), out_shape=list(rider.out_shapes),
        scratch_shapes=[pltpu.SemaphoreType.DMA((rider.n_sems,)), pltpu.SemaphoreType.DMA((rider.n_sems,))],
        input_output_aliases=dict(rider.aliases),
        compiler_params=pltpu.CompilerParams(has_side_effects=True))(*rider.ins))


def norm_matmul(x, nw, w, layer, n, col0, out_dtype, name, emit_h=True, rider=None):
    t, d = x.shape
    mxu_cols = 2 * LANES
    tn = _tile(n, 1536, mxu_cols if n % mxu_cols == 0 else LANES)
    if n % mxu_cols == 0 and tn < 1024 <= n:
        tn = _tile(n, 3072, mxu_cols)
    tm = _tile(t, 512 if tn > 1536 else 1024, 8)
    nj = n // tn

    def body(x_ref, nw_ref, w_ref, o_ref, h_ref):
        @pl.when(pl.program_id(1) == 0)
        def _():
            xf = x_ref[...]
            r = lax.rsqrt(jnp.mean(xf * xf, axis=-1, keepdims=True) + EPS)
            h_ref[...] = (xf * r * nw_ref[...]).astype(BF16)

        o_ref[...] = _dot(h_ref[...], w_ref[...]).astype(out_dtype)

    out_specs = [pl.BlockSpec((tm, tn), lambda i, j: (i, j))]
    out_shape = [jax.ShapeDtypeStruct((t, n), out_dtype)]
    if emit_h:
        out_specs.append(pl.BlockSpec((tm, d), lambda i, j: (i, 0)))
        out_shape.append(jax.ShapeDtypeStruct((t, d), BF16))
    return _ride(
        body, (t // tm, nj),
        [pl.BlockSpec((tm, d), lambda i, j: (i, 0)), pl.BlockSpec((1, d), lambda i, j: (0, 0)),
         pl.BlockSpec((None, d, tn), lambda i, j: (layer, 0, col0 * nj + j))],
        out_specs, out_shape, [] if emit_h else [pltpu.VMEM((tm, d), BF16)], [x, nw, w], {}, rider,
        ("parallel", "arbitrary"), name)


def matmul(lhs, w, layer, transposed, n, out_dtype, name, *, lhs_fn=None, residual=None, relu_gate=None,
           rider=None):
    t, k = lhs.shape
    tm = _tile(t, 512 if k > 2048 else 1024, 8)
    tn = _tile(n, 1024, LANES)
    staged = lhs.dtype != BF16 or lhs_fn is not None
    fn = lhs_fn if lhs_fn is not None else (lambda v: v)
    has_extra = residual is not None or relu_gate is not None
    dot = _dot_nt if transposed else _dot

    def body(*refs):
        a_ref, w_ref = refs[:2]
        extra = refs[2] if has_extra else None
        o_ref = refs[3] if has_extra else refs[2]
        if staged:
            s_ref = refs[-1]

            @pl.when(pl.program_id(1) == 0)
            def _():
                s_ref[...] = fn(a_ref[...].astype(F32)).astype(BF16)

            a_ref = s_ref
        acc = dot(a_ref[...], w_ref[...])
        if residual is not None:
            acc = acc + extra[...]
        if relu_gate is not None:
            acc = acc * (2.0 * jnp.maximum(extra[...].astype(F32), 0.0))
        o_ref[...] = acc.astype(out_dtype)

    if transposed:
        w_spec = pl.BlockSpec((None, tn, k), lambda i, j: (layer, j, 0))
    else:
        w_spec = pl.BlockSpec((None, k, tn), lambda i, j: (layer, 0, j))
    in_specs = [pl.BlockSpec((tm, k), lambda i, j: (i, 0)), w_spec]
    args = [lhs, w]
    if has_extra:
        in_specs.append(pl.BlockSpec((tm, tn), lambda i, j: (i, j)))
        args.append(residual if residual is not None else relu_gate)
    outs, extra = _ride(
        body, (t // tm, n // tn), in_specs, [pl.BlockSpec((tm, tn), lambda i, j: (i, j))],
        [jax.ShapeDtypeStruct((t, n), out_dtype)], [pltpu.VMEM((tm, k), BF16)] if staged else [], args, {},
        rider, ("parallel", "arbitrary"), name)
    return outs[0], extra


def matmul_normbwd(lhs, pieces, w, layer, x, nw, dres, name, rider=None):
    t, d = x.shape
    nl = len(lhs)
    tm = _tile(t, 256, 8)
    for off, width in pieces:
        assert off % width == 0

    def body(*refs):
        lrefs = refs[:nl]
        wrefs = refs[nl:2 * nl]
        x_ref, nw_ref, dres_ref, dx_ref, dnw_ref = refs[2 * nl:]
        dh = _dot_nt(lrefs[0][...].astype(BF16), wrefs[0][...])
        for a_ref, w_ref in zip(lrefs[1:], wrefs[1:]):
            dh = dh + _dot_nt(a_ref[...].astype(BF16), w_ref[...])
        xf = x_ref[...]
        r = lax.rsqrt(jnp.mean(xf * xf, axis=-1, keepdims=True) + EPS)
        nx = xf * r
        dn = dh * nw_ref[...]
        dx = r * (dn - nx * jnp.mean(dn * nx, axis=-1, keepdims=True))
        dx_ref[...] = dres_ref[...] + dx

        @pl.when(pl.program_id(0) == 0)
        def _():
            dnw_ref[...] = jnp.zeros_like(dnw_ref)

        dnw_ref[...] += jnp.sum(dh * nx, axis=0, keepdims=True)

    in_specs = [pl.BlockSpec((tm, width), lambda i: (i, 0)) for _, width in pieces]
    in_specs += [pl.BlockSpec((None, d, width), (lambda blk: (lambda i: (layer, 0, blk)))(off // width))
                 for off, width in pieces]
    in_specs += [pl.BlockSpec((tm, d), lambda i: (i, 0)), pl.BlockSpec((1, d), lambda i: (0, 0)),
                 pl.BlockSpec((tm, d), lambda i: (i, 0))]
    return _ride(
        body, (t // tm,), in_specs,
        [pl.BlockSpec((tm, d), lambda i: (i, 0)), pl.BlockSpec((1, d), lambda i: (0, 0))],
        [jax.ShapeDtypeStruct((t, d), F32), jax.ShapeDtypeStruct((1, d), F32)], [],
        [*lhs, *([w] * nl), x, nw, dres], {}, rider, ("arbitrary",), name)


def matmul_tn(a, b, name, *, a_fn=None, by_chip=False):
    t, k = a.shape
    n = b.shape[1]
    tk = _tile(k, 1024, LANES)
    tn = _tile(n // N_CHIPS if by_chip else n, 1024, LANES)
    tt = _tile(t, 1024, 8)
    nt = t // tt
    fn = a_fn if a_fn is not None else (lambda v: v)

    def body(a_ref, b_ref, o_ref, acc_ref):
        @pl.when(pl.program_id(2) == 0)
        def _():
            acc_ref[...] = jnp.zeros_like(acc_ref)

        av = a_ref[...]
        if a_fn is not None:
            av = fn(av.astype(F32))
        acc_ref[...] += _dot_tn(av.astype(BF16), b_ref[...].astype(BF16))

        @pl.when(pl.program_id(2) == nt - 1)
        def _():
            o_ref[...] = acc_ref[...]

    if by_chip:
        per = n // N_CHIPS // tn
        out_spec = pl.BlockSpec((None, tk, tn), lambda i, j, s: (j // per, i, j % per))
        out_shape = jax.ShapeDtypeStruct((N_CHIPS, k, n // N_CHIPS), F32)
    else:
        out_spec = pl.BlockSpec((tk, tn), lambda i, j, s: (i, j))
        out_shape = jax.ShapeDtypeStruct((k, n), F32)
    return pl.pallas_call(
        body, name=name, grid=(k // tk, n // tn, nt),
        in_specs=[pl.BlockSpec((tt, tk), lambda i, j, s: (s, i)),
                  pl.BlockSpec((tt, tn), lambda i, j, s: (s, j))],
        out_specs=out_spec, out_shape=out_shape,
        scratch_shapes=[pltpu.VMEM((tk, tn), F32)],
        compiler_params=_params(("parallel", "parallel", "arbitrary")),
    )(a, b)


def split_to_chips(pieces, cols, name):
    d = pieces[0].shape[0]
    widths = [p.shape[1] for p in pieces]
    w = cols // N_CHIPS
    tr = _tile(d, 256, 8)
    npc = len(pieces)

    def body(*refs):
        o_ref, row = refs[npc], refs[npc + 1]
        off = 0
        for r, n in zip(refs[:npc], widths):
            row[:, off:off + n] = r[...]
            off += n
        for j in range(N_CHIPS):
            o_ref[j] = row[:, j * w:(j + 1) * w]

    return pl.pallas_call(
        body, name=name, grid=(d // tr,),
        in_specs=[pl.BlockSpec((tr, n), lambda i: (i, 0)) for n in widths],
        out_specs=pl.BlockSpec((N_CHIPS, tr, w), lambda i: (0, i, 0)),
        out_shape=jax.ShapeDtypeStruct((N_CHIPS, d, w), F32),
        scratch_shapes=[pltpu.VMEM((tr, sum(widths)), F32)],
        compiler_params=_params(("parallel",)),
    )(*pieces)


def join_from_chips(g4, npad, name):
    _, nl, d, w = g4.shape
    tr = _tile(d, 256, HALO)

    def body(g_ref, o_ref):
        for j in range(N_CHIPS):
            o_ref[:, j * w:(j + 1) * w] = g_ref[j]
        o_ref[:, N_CHIPS * w:] = jnp.zeros((tr, npad - N_CHIPS * w), o_ref.dtype)

    return pl.pallas_call(
        body, name=name, grid=(nl, d // tr),
        in_specs=[pl.BlockSpec((N_CHIPS, None, tr, w), lambda l, i: (0, l, i, 0))],
        out_specs=pl.BlockSpec((None, tr, npad), lambda l, i: (l, i, 0)),
        out_shape=jax.ShapeDtypeStruct((nl, d, npad), g4.dtype),
        compiler_params=_params(("parallel", "parallel")),
    )(g4)


def conv_mixer_fwd(proj, kw, cw, out_cols, name):
    t = proj.shape[0]
    tm = _tile(t, 1024, HALO)
    tc = _tile(cw, 1024, LANES)
    nj = cw // tc
    hb = tm // HALO

    def body(ub_ref, uc_ref, uh_ref, ucp_ref, uhp_ref, kw_ref, y_ref, ext):
        i = pl.program_id(0)
        v = uc_ref[...].astype(F32) * uh_ref[...].astype(F32)
        vp = ucp_ref[...].astype(F32) * uhp_ref[...].astype(F32)
        ext[0:HALO, :] = jnp.where(i > 0, vp, 0.0)
        ext[HALO:HALO + tm, :] = v
        cv = kw_ref[pl.ds(SHORT_K - 1, 1), :] * v
        for k in range(SHORT_K - 1):
            cv = cv + kw_ref[pl.ds(k, 1), :] * ext[pl.ds(HALO - (SHORT_K - 1) + k, tm), :]
        y_ref[...] = (ub_ref[...].astype(F32) * cv).astype(BF16)

    prev = lambda off: (lambda i, j: (jnp.maximum(i * hb - 1, 0), off + j))
    return pl.pallas_call(
        body, name=name, grid=(t // tm, nj),
        in_specs=[pl.BlockSpec((tm, tc), lambda i, j: (i, j)),
                  pl.BlockSpec((tm, tc), lambda i, j: (i, nj + j)),
                  pl.BlockSpec((tm, tc), lambda i, j: (i, 2 * nj + j)),
                  pl.BlockSpec((HALO, tc), prev(nj)),
                  pl.BlockSpec((HALO, tc), prev(2 * nj)),
                  pl.BlockSpec((SHORT_K, tc), lambda i, j: (0, j))],
        out_specs=pl.BlockSpec((tm, tc), lambda i, j: (i, j)),
        out_shape=jax.ShapeDtypeStruct((t, out_cols), BF16),
        scratch_shapes=[pltpu.VMEM((tm + HALO, tc), F32)],
        compiler_params=_params(("parallel", "parallel")),
    )(proj, proj, proj, proj, proj, kw)


def conv_mixer_bwd(proj, dy, kw, cw, name):
    t = proj.shape[0]
    tm = _tile(t, 1024, HALO)
    tc = _tile(cw, 512, LANES)
    nj = cw // tc
    hb = tm // HALO
    ni = t // tm
    last_hb = t // HALO - 1

    def body(ub_ref, uc_ref, uh_ref, dy_ref, ucp_ref, uhp_ref, ubn_ref, dyn_ref, kw_ref,
             dub_ref, duc_ref, duh_ref, dkw_ref, ext, extd):
        i = pl.program_id(1)
        ub = ub_ref[...].astype(F32)
        uc = uc_ref[...].astype(F32)
        uh = uh_ref[...].astype(F32)
        dyv = dy_ref[...].astype(F32)
        v = uc * uh
        vp = ucp_ref[...].astype(F32) * uhp_ref[...].astype(F32)
        ext[0:HALO, :] = jnp.where(i > 0, vp, 0.0)
        ext[HALO:HALO + tm, :] = v
        cv = kw_ref[pl.ds(SHORT_K - 1, 1), :] * v
        for k in range(SHORT_K - 1):
            cv = cv + kw_ref[pl.ds(k, 1), :] * ext[pl.ds(HALO - (SHORT_K - 1) + k, tm), :]
        dcv = dyv * ub
        dcvn = dyn_ref[...].astype(F32) * ubn_ref[...].astype(F32)
        extd[0:tm, :] = dcv
        extd[tm:tm + HALO, :] = jnp.where(i < ni - 1, dcvn, 0.0)
        dv = kw_ref[pl.ds(SHORT_K - 1, 1), :] * dcv
        for k in range(SHORT_K - 1):
            dv = dv + kw_ref[pl.ds(k, 1), :] * extd[pl.ds(SHORT_K - 1 - k, tm), :]
        dub_ref[...] = (dyv * cv).astype(BF16)
        duc_ref[...] = (dv * uh).astype(BF16)
        duh_ref[...] = (dv * uc).astype(BF16)

        @pl.when(i == 0)
        def _():
            dkw_ref[...] = jnp.zeros_like(dkw_ref)

        for k in range(SHORT_K):
            sh = ext[pl.ds(HALO - (SHORT_K - 1) + k, tm), :]
            dkw_ref[pl.ds(k, 1), :] += jnp.sum(dcv * sh, axis=0, keepdims=True)

    prev = lambda off: (lambda j, i: (jnp.maximum(i * hb - 1, 0), off + j))
    nxt = lambda off: (lambda j, i: (jnp.minimum((i + 1) * hb, last_hb), off + j))
    cur = lambda off: (lambda j, i: (i, off + j))
    return pl.pallas_call(
        body, name=name, grid=(nj, ni),
        in_specs=[pl.BlockSpec((tm, tc), cur(0)), pl.BlockSpec((tm, tc), cur(nj)),
                  pl.BlockSpec((tm, tc), cur(2 * nj)), pl.BlockSpec((tm, tc), cur(0)),
                  pl.BlockSpec((HALO, tc), prev(nj)), pl.BlockSpec((HALO, tc), prev(2 * nj)),
                  pl.BlockSpec((HALO, tc), nxt(0)), pl.BlockSpec((HALO, tc), nxt(0)),
                  pl.BlockSpec((SHORT_K, tc), lambda j, i: (0, j))],
        out_specs=[pl.BlockSpec((tm, tc), cur(0)), pl.BlockSpec((tm, tc), cur(0)),
                   pl.BlockSpec((tm, tc), cur(0)), pl.BlockSpec((SHORT_K, tc), lambda j, i: (0, j))],
        out_shape=[jax.ShapeDtypeStruct((t, cw), BF16)] * 3 + [jax.ShapeDtypeStruct((SHORT_K, cw), F32)],
        scratch_shapes=[pltpu.VMEM((tm + HALO, tc), F32), pltpu.VMEM((tm + HALO, tc), F32)],
        compiler_params=_params(("parallel", "arbitrary")),
    )(proj, proj, proj, dy, proj, proj, proj, dy, kw)


def _head_column(mat, lane, h):
    return jnp.sum(jnp.where(lane == h, mat, 0.0), axis=-1, keepdims=True)


def _ssd_common(dt_raw_ref, dtb_ref, aneg_ref, cum_s, cumt_s, chunk):
    dt = _softplus(dt_raw_ref[...] + dtb_ref[...])
    al = dt * aneg_ref[...]
    ri = lax.broadcasted_iota(jnp.int32, (chunk, chunk), 0)
    ci = lax.broadcasted_iota(jnp.int32, (chunk, chunk), 1)
    cum = _dot_exact((ri >= ci).astype(F32), al)
    cum_s[...] = cum
    cumt_s[...] = cum.T
    return dt, cum, ri >= ci


EDGE = 16


def _shift_matrices(shift_s, chunk, kk, up):
    ri = lax.broadcasted_iota(jnp.int32, (chunk, chunk), 0)
    ci = lax.broadcasted_iota(jnp.int32, (chunk, chunk), 1)
    for k in range(kk - 1):
        s = kk - 1 - k
        shift_s[k] = ((ci - ri if up else ri - ci) == s).astype(BF16)


def _causal_conv(cur, head, kw_ref, b_ref, shift_s, kk):
    acc = b_ref[...] + kw_ref[pl.ds(kk - 1, 1), :] * cur.astype(F32)
    top = b_ref[...] + kw_ref[pl.ds(kk - 1, 1), :] * head[pl.ds(8, EDGE), :]
    for k in range(kk - 1):
        acc = acc + kw_ref[pl.ds(k, 1), :] * _dot(shift_s[k], cur)
        top = top + kw_ref[pl.ds(k, 1), :] * head[pl.ds(8 - (kk - 1) + k, EDGE), :]
    return acc, top


def ssd_fwd(proj, dt_raw, y_mix, kw_xs, kw_bc, b_xs, b_bc, dtb, aneg, dskip, normw, cw, si, name, rider=None):
    t = proj.shape[0]
    ch = min(SCAN_CHUNK, t)
    nc = t // ch
    npair = si // LANES
    ppg = npair // GROUPS
    gn = GROUPS * STATE
    gw = si // GROUPS
    assert cw == si and (3 * cw + 2 * si) % (2 * gn) == 0
    zblk = 3 * cw // si
    xsblk = zblk + 1
    bcblk = (3 * cw + 2 * si) // (2 * gn)

    def body(z_ref, xs_ref, bc_ref, dtr_ref, ymix_ref, kwx_ref, kwb_ref, bx_ref, bb_ref, dtb_ref, aneg_ref, dsk_ref,
             nw_ref, yb_ref, ys_ref, hs_ref, xcx_ref, xcb_ref,
             headx, headb, shift_s, xs_s, bc_s, h_s, gated_s, s_s, cum_s, cumt_s):
        del ymix_ref
        c = pl.program_id(0)

        @pl.when(c == 0)
        def _():
            h_s[...] = jnp.zeros_like(h_s)
            headx[0:8, :] = jnp.zeros((8, si), F32)
            headb[0:8, :] = jnp.zeros((8, 2 * gn), F32)
            _shift_matrices(shift_s, ch, SSD_K, up=False)

        for raw_ref, head, kw_ref, b_ref, pre_ref, act_s in ((xs_ref, headx, kwx_ref, bx_ref, xcx_ref, xs_s),
                                                           (bc_ref, headb, kwb_ref, bb_ref, xcb_ref, bc_s)):
            head[8:8 + EDGE, :] = raw_ref[0:EDGE, :].astype(F32)
            pre, top = _causal_conv(raw_ref[...], head, kw_ref, b_ref, shift_s, SSD_K)
            head[0:8, :] = raw_ref[ch - EDGE:ch, :].astype(F32)[EDGE - 8:EDGE]
            pre_ref[...] = pre.astype(BF16)
            pre_ref[0:EDGE, :] = top.astype(BF16)
            act_s[...] = (pre * _sigmoid(pre)).astype(act_s.dtype)
            act_s[0:EDGE, :] = (top * _sigmoid(top)).astype(act_s.dtype)

        dt, cum, tril = _ssd_common(dtr_ref, dtb_ref, aneg_ref, cum_s, cumt_s, ch)
        lane = lax.broadcasted_iota(jnp.int32, (ch, LANES), 1)
        lane1 = lax.broadcasted_iota(jnp.int32, (1, LANES), 1)
        low = lane < HEAD_DIM
        clast = cum_s[pl.ds(ch - 1, 1), :]

        for p in range(npair):
            g = p // ppg
            col = slice(p * LANES, (p + 1) * LANES)
            bg = bc_s[:, g * STATE:(g + 1) * STATE]
            cg = bc_s[:, gn + g * STATE:gn + (g + 1) * STATE]
            if p % ppg == 0:
                s_s[...] = _dot_nt(cg, bg)
            heads = (PAIR * p, PAIR * p + 1)
            ccol = [_head_column(cum, lane, h) for h in heads]
            dcol = [_head_column(dt, lane, h) for h in heads]
            cl = [jnp.sum(jnp.where(lane1 == h, clast, 0.0), axis=-1, keepdims=True) for h in heads]
            cum_px = jnp.where(low, ccol[0], ccol[1])
            dt_px = jnp.where(low, dcol[0], dcol[1])
            cl_px = jnp.where(lane1 < HEAD_DIM, cl[0], cl[1])
            xs_p = xs_s[:, col]
            xdt = xs_p * dt_px
            y = dsk_ref[:, col] * xs_p
            for hi, h in enumerate(heads):
                dec = jnp.exp(jnp.where(tril, ccol[hi] - cumt_s[pl.ds(h, 1), :], -jnp.inf))
                wm = (s_s[...] * dec).astype(BF16)
                xm = jnp.where(low if hi == 0 else jnp.logical_not(low), xdt, 0.0).astype(BF16)
                y = y + _dot(wm, xm)
            hp = h_s[p]
            hs_ref[0, p] = hp
            y = y + _dot(cg, hp.astype(BF16)) * jnp.exp(cum_px)
            st = _dot_tn(bg, (xdt * jnp.exp(cl_px - cum_px)).astype(BF16))
            h_s[p] = jnp.exp(cl_px) * hp + st
            ys_ref[:, col] = y.astype(BF16)
            zp = z_ref[:, col].astype(F32)
            gated_s[:, col] = y * zp * _sigmoid(zp)

        for g in range(GROUPS):
            col = slice(g * gw, (g + 1) * gw)
            gg = gated_s[:, col]
            r = lax.rsqrt(jnp.mean(gg * gg, axis=-1, keepdims=True) + EPS)
            yb_ref[:, col] = (gg * r * nw_ref[:, col]).astype(BF16)

    full = lambda shape: pl.BlockSpec(shape, lambda c: tuple(0 for _ in shape))
    return _ride(
        body, (nc,),
        [pl.BlockSpec((ch, si), lambda c: (c, zblk)),
         pl.BlockSpec((ch, si), lambda c: (c, xsblk)),
         pl.BlockSpec((ch, 2 * gn), lambda c: (c, bcblk)),
         pl.BlockSpec((ch, LANES), lambda c: (c, 0)),
         pl.BlockSpec(memory_space=pl.ANY),
         full((SSD_K, si)), full((SSD_K, 2 * gn)), full((1, si)), full((1, 2 * gn)),
         full((1, LANES)), full((1, LANES)), full((1, si)), full((1, si))],
        [pl.BlockSpec((ch, si), lambda c: (c, cw // si)),
         pl.BlockSpec((ch, si), lambda c: (c, 0)),
         pl.BlockSpec((1, npair, STATE, LANES), lambda c: (c, 0, 0, 0)),
         pl.BlockSpec((ch, si), lambda c: (c, 0)), pl.BlockSpec((ch, 2 * gn), lambda c: (c, 0))],
        [jax.ShapeDtypeStruct(y_mix.shape, BF16), jax.ShapeDtypeStruct((t, si), BF16),
         jax.ShapeDtypeStruct((nc, npair, STATE, LANES), F32),
         jax.ShapeDtypeStruct((t, si), BF16), jax.ShapeDtypeStruct((t, 2 * gn), BF16)],
        [pltpu.VMEM((8 + EDGE, si), F32), pltpu.VMEM((8 + EDGE, 2 * gn), F32),
         pltpu.VMEM((SSD_K - 1, ch, ch), BF16),
         pltpu.VMEM((ch, si), F32), pltpu.VMEM((ch, 2 * gn), BF16),
         pltpu.VMEM((npair, STATE, LANES), F32), pltpu.VMEM((ch, si), F32),
         pltpu.VMEM((ch, ch), F32), pltpu.VMEM((ch, LANES), F32), pltpu.VMEM((LANES, ch), F32)],
        [proj, proj, proj, dt_raw, y_mix, kw_xs, kw_bc, b_xs, b_bc, dtb, aneg, dskip, normw], {4: 0}, rider,
        ("arbitrary",), name)


def ssd_bwd(proj, dt_raw, ys, hsave, pre_xs, pre_bc, dy, kw_xs, kw_bc, dtb, aneg, dskip, normw, cw, si, name,
            rider=None):
    t = proj.shape[0]
    ch = min(SCAN_CHUNK, t)
    nc = t // ch
    npair = si // LANES
    ppg = npair // GROUPS
    gn = GROUPS * STATE
    gw = si // GROUPS
    zblk = 3 * cw // si
    xsblk = zblk + 1
    bcblk = (3 * cw + 2 * si) // (2 * gn)

    def body(z_ref, xs_ref, bc_ref, xcx_ref, xcb_ref, dtr_ref, ys_ref, hs_ref, dyb_ref,
             kwx_ref, kwb_ref, dtb_ref, aneg_ref, dsk_ref, nw_ref,
             dz_ref, dxs_ref, dbc_ref, ddt_ref, dkwx_ref, dkwb_ref, dbx_ref, dbb_ref, ddtb_ref, da_ref, ddsk_ref,
             dnw_ref,
             tailx, tailb, shift_s, xs_s, bc_s, dsx_s, dsb_s, dy_s, dxs_s, dbc_s, dh_s, s_s, ds_s,
             cum_s, cumt_s, dccol_s, dcrow_s, ddtcol_s, dcl_s):
        i = pl.program_id(0)

        @pl.when(i == 0)
        def _():
            dh_s[...] = jnp.zeros_like(dh_s)
            tailx[EDGE:EDGE + 8, :] = jnp.zeros((8, si), F32)
            tailb[EDGE:EDGE + 8, :] = jnp.zeros((8, 2 * gn), F32)
            _shift_matrices(shift_s, ch, SSD_K, up=True)
            for r in (dkwx_ref, dkwb_ref, dbx_ref, dbb_ref, ddtb_ref, da_ref, ddsk_ref, dnw_ref):
                r[...] = jnp.zeros_like(r)

        xc = xcx_ref[...].astype(F32)
        sg = _sigmoid(xc)
        xs_s[...] = xc * sg
        dsx_s[...] = sg * (1.0 + xc * (1.0 - sg))
        bcc = xcb_ref[...].astype(F32)
        sgb = _sigmoid(bcc)
        bc_s[...] = (bcc * sgb).astype(BF16)
        dsb_s[...] = sgb * (1.0 + bcc * (1.0 - sgb))

        dt, cum, tril = _ssd_common(dtr_ref, dtb_ref, aneg_ref, cum_s, cumt_s, ch)
        lane = lax.broadcasted_iota(jnp.int32, (ch, LANES), 1)
        lane1 = lax.broadcasted_iota(jnp.int32, (1, LANES), 1)
        low = lane < HEAD_DIM
        low1 = lane1 < HEAD_DIM
        clast = cum_s[pl.ds(ch - 1, 1), :]

        for g in range(GROUPS):
            col = slice(g * gw, (g + 1) * gw)
            ysf = ys_ref[:, col].astype(F32)
            zf = z_ref[:, col].astype(F32)
            sz = _sigmoid(zf)
            silz = zf * sz
            gg = ysf * silz
            r = lax.rsqrt(jnp.mean(gg * gg, axis=-1, keepdims=True) + EPS)
            nrm = gg * r
            dyb = dyb_ref[:, col].astype(F32)
            dnw_ref[:, col] += jnp.sum(dyb * nrm, axis=0, keepdims=True)
            dn = dyb * nw_ref[:, col]
            dgg = r * (dn - nrm * jnp.mean(dn * nrm, axis=-1, keepdims=True))
            dy_s[:, col] = dgg * silz
            dz_ref[:, col] = (dgg * ysf * (sz * (1.0 + zf * (1.0 - sz)))).astype(BF16)

        dccol_s[...] = jnp.zeros_like(dccol_s)
        dcrow_s[...] = jnp.zeros_like(dcrow_s)
        ddtcol_s[...] = jnp.zeros_like(ddtcol_s)
        dcl_s[...] = jnp.zeros_like(dcl_s)
        dbc_s[...] = jnp.zeros_like(dbc_s)

        for p in range(npair):
            g = p // ppg
            col = slice(p * LANES, (p + 1) * LANES)
            bcol = slice(g * STATE, (g + 1) * STATE)
            ccolg = slice(gn + g * STATE, gn + (g + 1) * STATE)
            bg = bc_s[:, bcol]
            cg = bc_s[:, ccolg]
            if p % ppg == 0:
                s_s[...] = _dot_nt(cg, bg)
                ds_s[...] = jnp.zeros_like(ds_s)
            heads = (PAIR * p, PAIR * p + 1)
            masks = (low, jnp.logical_not(low))
            masks1 = (low1, jnp.logical_not(low1))
            ccol = [_head_column(cum, lane, h) for h in heads]
            dcol = [_head_column(dt, lane, h) for h in heads]
            cl = [jnp.sum(jnp.where(lane1 == h, clast, 0.0), axis=-1, keepdims=True) for h in heads]
            cum_px = jnp.where(low, ccol[0], ccol[1])
            dt_px = jnp.where(low, dcol[0], dcol[1])
            cl_px = jnp.where(low1, cl[0], cl[1])
            e_px = jnp.exp(cum_px)
            dec_end = jnp.exp(cl_px - cum_px)
            gdec = jnp.exp(cl_px)
            xs_p = xs_s[:, col]
            xdt = xs_p * dt_px
            dyp = dy_s[:, col]
            hc = hs_ref[0, p]
            hcb = hc.astype(BF16)
            dhn = dh_s[p]
            dhnb = dhn.astype(BF16)

            ddsk_ref[:, col] += jnp.sum(dyp * xs_p, axis=0, keepdims=True)
            dxs_acc = dsk_ref[:, col] * dyp
            dye = dyp * e_px
            dyeb = dye.astype(BF16)
            dbc_s[:, ccolg] += _dot_nt(dyeb, hcb)
            dcum_lane = dye * _dot(cg, hcb)
            dh_from_y = _dot_tn(cg, dyeb)
            xd = xdt * dec_end
            dxd = _dot(bg, dhnb)
            dbc_s[:, bcol] += _dot_nt(xd.astype(BF16), dhnb)
            dxdt = dxd * dec_end
            t1 = dxd * xd
            dcum_lane = dcum_lane - t1
            dcl_lane = jnp.sum(t1, axis=0, keepdims=True) + jnp.sum(dhn * hc, axis=0, keepdims=True) * gdec
            dh_s[p] = gdec * dhn + dh_from_y
            xdtb = xdt.astype(BF16)
            for hi, h in enumerate(heads):
                dym = jnp.where(masks[hi], dyp, 0.0).astype(BF16)
                dw = _dot_nt(dym, xdtb)
                dec = jnp.exp(jnp.where(tril, ccol[hi] - cumt_s[pl.ds(h, 1), :], -jnp.inf))
                wm = s_s[...] * dec
                dxdt = dxdt + _dot_tn(wm.astype(BF16), dym)
                ds_s[...] += dw * dec
                gm = dw * wm
                rowsum = jnp.sum(gm, axis=-1, keepdims=True)
                lanesum = jnp.sum(jnp.where(masks[hi], dcum_lane, 0.0), axis=-1, keepdims=True)
                dccol_s[...] += jnp.where(lane == h, rowsum + lanesum, 0.0)
                dcrow_s[pl.ds(h, 1), :] = jnp.sum(gm, axis=0, keepdims=True)
                dcl_h = jnp.sum(jnp.where(masks1[hi], dcl_lane, 0.0), axis=-1, keepdims=True)
                dcl_s[...] += jnp.where(lane1 == h, dcl_h, 0.0)
            ddt_lane = dxdt * xs_p
            for hi, h in enumerate(heads):
                s = jnp.sum(jnp.where(masks[hi], ddt_lane, 0.0), axis=-1, keepdims=True)
                ddtcol_s[...] += jnp.where(lane == h, s, 0.0)
            dxs_s[:, col] = dxs_acc + dxdt * dt_px
            if p % ppg == ppg - 1:
                dsb = ds_s[...].astype(BF16)
                dbc_s[:, ccolg] += _dot(dsb, bg)
                dbc_s[:, bcol] += _dot_tn(dsb, cg)

        rowi = lax.broadcasted_iota(jnp.int32, (ch, LANES), 0)
        dcum = dccol_s[...] - dcrow_s[...].T + jnp.where(rowi == ch - 1, dcl_s[...], 0.0)
        ri = lax.broadcasted_iota(jnp.int32, (ch, ch), 0)
        ci = lax.broadcasted_iota(jnp.int32, (ch, ch), 1)
        dal = _dot_exact((ri <= ci).astype(F32), dcum)
        ddt = dal * aneg_ref[...] + ddtcol_s[...]
        da_ref[...] += jnp.sum(dal * dt, axis=0, keepdims=True)
        ddtr = ddt * _sigmoid(dtr_ref[...] + dtb_ref[...])
        ddt_ref[...] = ddtr
        ddtb_ref[...] += jnp.sum(ddtr, axis=0, keepdims=True)

        for (dpost, dsl, tail, raw_ref, kw_ref, dkw_ref, db_ref, out_ref) in (
                (dxs_s, dsx_s, tailx, xs_ref, kwx_ref, dkwx_ref, dbx_ref, dxs_ref),
                (dbc_s, dsb_s, tailb, bc_ref, kwb_ref, dkwb_ref, dbb_ref, dbc_ref)):
            dxc = dpost[...] * dsl[...]
            dxcb = dxc.astype(BF16)
            raw = raw_ref[...].astype(F32)
            raw_end = raw_ref[ch - EDGE:ch, :].astype(F32)
            tail[0:EDGE, :] = dxcb[ch - EDGE:ch].astype(F32)
            db_ref[...] += jnp.sum(dxc, axis=0, keepdims=True)
            draw = kw_ref[pl.ds(SSD_K - 1, 1), :] * dxc
            dkw_ref[pl.ds(SSD_K - 1, 1), :] += jnp.sum(dxc * raw, axis=0, keepdims=True)
            fix = jnp.zeros((EDGE, dxc.shape[1]), F32)
            for k in range(SSD_K - 1):
                moved = _dot(shift_s[k], dxcb)
                miss = tail[pl.ds(SSD_K - 1 - k, EDGE), :] - moved[ch - EDGE:ch]
                draw = draw + kw_ref[pl.ds(k, 1), :] * moved
                fix = fix + kw_ref[pl.ds(k, 1), :] * miss
                dkw_ref[pl.ds(k, 1), :] += (jnp.sum(moved * raw, axis=0, keepdims=True)
                                            + jnp.sum(miss * raw_end, axis=0, keepdims=True))
            out_ref[...] = draw.astype(BF16)
            out_ref[ch - EDGE:ch, :] = (draw[ch - EDGE:ch] + fix).astype(BF16)
            tail[EDGE:EDGE + 8, :] = dxcb[0:EDGE].astype(F32)[0:8]

    full = lambda shape: pl.BlockSpec(shape, lambda i: tuple(0 for _ in shape))
    rev = lambda blk: (lambda i: (nc - 1 - i, blk))
    small_in = [(SSD_K, si), (SSD_K, 2 * gn), (1, LANES), (1, LANES), (1, si), (1, si)]
    small = [(SSD_K, si), (SSD_K, 2 * gn), (1, si), (1, 2 * gn), (1, LANES), (1, LANES), (1, si), (1, si)]
    return _ride(
        body, (nc,),
        [pl.BlockSpec((ch, si), rev(zblk)), pl.BlockSpec((ch, si), rev(xsblk)),
         pl.BlockSpec((ch, 2 * gn), rev(bcblk)),
         pl.BlockSpec((ch, si), rev(0)), pl.BlockSpec((ch, 2 * gn), rev(0)),
         pl.BlockSpec((ch, LANES), rev(0)), pl.BlockSpec((ch, si), rev(0)),
         pl.BlockSpec((1, npair, STATE, LANES), lambda i: (nc - 1 - i, 0, 0, 0)),
         pl.BlockSpec((ch, si), rev(cw // si))] + [full(s) for s in small_in],
        [pl.BlockSpec((ch, si), rev(0)), pl.BlockSpec((ch, si), rev(0)),
         pl.BlockSpec((ch, 2 * gn), rev(0)), pl.BlockSpec((ch, LANES), rev(0))] + [full(s) for s in small],
        [jax.ShapeDtypeStruct((t, si), BF16), jax.ShapeDtypeStruct((t, si), BF16),
         jax.ShapeDtypeStruct((t, 2 * gn), BF16), jax.ShapeDtypeStruct((t, LANES), F32)]
        + [jax.ShapeDtypeStruct(s, F32) for s in small],
        [pltpu.VMEM((EDGE + 8, si), F32), pltpu.VMEM((EDGE + 8, 2 * gn), F32),
         pltpu.VMEM((SSD_K - 1, ch, ch), BF16),
         pltpu.VMEM((ch, si), F32), pltpu.VMEM((ch, 2 * gn), BF16),
         pltpu.VMEM((ch, si), F32), pltpu.VMEM((ch, 2 * gn), F32),
         pltpu.VMEM((ch, si), F32), pltpu.VMEM((ch, si), F32), pltpu.VMEM((ch, 2 * gn), F32),
         pltpu.VMEM((npair, STATE, LANES), F32),
         pltpu.VMEM((ch, ch), F32), pltpu.VMEM((ch, ch), F32),
         pltpu.VMEM((ch, LANES), F32), pltpu.VMEM((LANES, ch), F32),
         pltpu.VMEM((ch, LANES), F32), pltpu.VMEM((LANES, ch), F32),
         pltpu.VMEM((ch, LANES), F32), pltpu.VMEM((1, LANES), F32)],
        [proj, proj, proj, pre_xs, pre_bc, dt_raw, ys, hsave, dy, kw_xs, kw_bc, dtb, aneg, dskip, normw],
        {}, rider, ("arbitrary",), name)


def final_loss(x, nw, tgt, name):
    t, d = x.shape
    tm = _tile(t, 512, 8)

    def body(x_ref, nw_ref, t_ref, dx_ref, dnw_ref, ls_ref):
        xf = x_ref[...]
        r = lax.rsqrt(jnp.mean(xf * xf, axis=-1, keepdims=True) + EPS)
        nx = xf * r
        e = nx * nw_ref[...] - t_ref[...]
        dyv = e * (1.0 / d)
        dn = dyv * nw_ref[...]
        dx_ref[...] = r * (dn - nx * jnp.mean(dn * nx, axis=-1, keepdims=True))

        @pl.when(pl.program_id(0) == 0)
        def _():
            dnw_ref[...] = jnp.zeros_like(dnw_ref)
            ls_ref[...] = jnp.zeros_like(ls_ref)

        dnw_ref[...] += jnp.sum(dyv * nx, axis=0, keepdims=True)
        ls_ref[...] += jnp.sum(e * e, axis=0, keepdims=True) * (0.5 / d)

    return pl.pallas_call(
        body, name=name, grid=(t // tm,),
        in_specs=[pl.BlockSpec((tm, d), lambda i: (i, 0)), pl.BlockSpec((1, d), lambda i: (0, 0)),
                  pl.BlockSpec((tm, d), lambda i: (i, 0))],
        out_specs=[pl.BlockSpec((tm, d), lambda i: (i, 0)), pl.BlockSpec((1, d), lambda i: (0, 0)),
                   pl.BlockSpec((1, d), lambda i: (0, 0))],
        out_shape=[jax.ShapeDtypeStruct((t, d), F32), jax.ShapeDtypeStruct((1, d), F32),
                   jax.ShapeDtypeStruct((1, d), F32)],
        compiler_params=_params(("arbitrary",)),
    )(x, nw, tgt)


def _rows3(a):
    if a.ndim == 1:
        return a.reshape(1, 1, a.shape[0])
    if a.ndim == 2:
        return a.reshape(1, *a.shape)
    return a.reshape(-1, a.shape[-2], a.shape[-1])


def adamw(w, g, m, v, name):
    shape = w.shape
    views = [_rows3(a) for a in (w, g, m, v)]
    b, r, c = views[0].shape
    tr = _tile(r, 256, 16) if r % 16 == 0 else r

    def body(w_ref, g_ref, m_ref, v_ref, d_ref, nm_ref, nv_ref):
        g = g_ref[...]
        m = ADAM_B1 * m_ref[...] + (1.0 - ADAM_B1) * g
        v = ADAM_B2 * v_ref[...] + (1.0 - ADAM_B2) * (g * g)
        m_hat = m / (1.0 - ADAM_B1 ** ADAM_STEP)
        v_hat = v / (1.0 - ADAM_B2 ** ADAM_STEP)
        d_ref[...] = -ADAM_LR * (m_hat / (jnp.sqrt(v_hat) + ADAM_EPS) + ADAM_WD * w_ref[...])
        nm_ref[...] = m
        nv_ref[...] = v

    spec = pl.BlockSpec((1, tr, c), lambda i, j: (i, j, 0))
    outs = pl.pallas_call(
        body, name=name, grid=(b, r // tr), in_specs=[spec] * 4, out_specs=[spec] * 3,
        out_shape=[jax.ShapeDtypeStruct((b, r, c), F32)] * 3,
        compiler_params=_params(("parallel", "parallel")),
    )(*views)
    return [o.reshape(shape) for o in outs]


def adamw_halves(w, g_mine, g_theirs, m, v, name):
    nl, r, c = w.shape
    r2 = r // 2
    tr = _tile(r2, 256, 16)
    nb = r2 // tr

    def body(w_ref, gm_ref, gt_ref, m_ref, v_ref, g_ref, d_ref, nm_ref, nv_ref):
        mine = (pl.program_id(1) // nb) == lax.axis_index("c")
        g = jnp.where(mine, gm_ref[...], gt_ref[...])
        m = ADAM_B1 * m_ref[...] + (1.0 - ADAM_B1) * g
        v = ADAM_B2 * v_ref[...] + (1.0 - ADAM_B2) * (g * g)
        m_hat = m / (1.0 - ADAM_B1 ** ADAM_STEP)
        v_hat = v / (1.0 - ADAM_B2 ** ADAM_STEP)
        g_ref[...] = g
        d_ref[...] = -ADAM_LR * (m_hat / (jnp.sqrt(v_hat) + ADAM_EPS) + ADAM_WD * w_ref[...])
        nm_ref[...] = m
        nv_ref[...] = v

    whole = pl.BlockSpec((1, tr, c), lambda l, i: (l, i, 0))
    half = pl.BlockSpec((1, tr, c), lambda l, i: (l, i % nb, 0))
    return pl.pallas_call(
        body, name=name, grid=(nl, 2 * nb), in_specs=[whole, half, half, whole, whole], out_specs=[whole] * 4,
        out_shape=[jax.ShapeDtypeStruct((nl, r, c), F32)] * 4,
        compiler_params=_params(("parallel", "parallel")),
    )(w, g_mine, g_theirs, m, v)


def _coords():
    return lax.axis_index("x"), lax.axis_index("y"), lax.axis_index("c")


def _ici_peers(x, y):
    chips = [(1 - x, y), (x, 1 - y), (1 - x, 1 - y)]
    return chips, [2 * cx + cy for cx, cy in chips]


def _place(ref, how, chip, layers, per):
    if how == "lead":
        return ref.at[chip, layers]
    start = pl.multiple_of(chip * per, per)
    if how == "rows":
        return ref.at[layers, pl.ds(start, per), :]
    return ref.at[layers, :, pl.ds(start, per)]


def gather_weights(shards, hows, name):
    na = len(shards)
    out_shape = []
    for s, how in zip(shards, hows):
        assert s.shape[0] % 2 == 0
        if how == "lead":
            shp = (N_CHIPS, *s.shape)
        elif how == "rows":
            shp = (s.shape[0], N_CHIPS * s.shape[1], s.shape[2])
        else:
            shp = (s.shape[0], s.shape[1], N_CHIPS * s.shape[2])
        out_shape.append(jax.ShapeDtypeStruct(shp, s.dtype))

    def body(*refs):
        ins = refs[:na]
        outs = refs[na:2 * na]
        send_sems, recv_sems = refs[2 * na:]
        x, y, c = _coords()
        me = 2 * x + y
        chips, chip_ids = _ici_peers(x, y)
        sibling = (x, y, 1 - c)

        def dst(a, chip, layers):
            per = {"lead": 0, "rows": ins[a].shape[1], "cols": ins[a].shape[-1]}[hows[a]]
            return _place(outs[a], hows[a], chip, layers, per)

        def copy(a, k, src, dst_ref, to):
            return pltpu.make_async_remote_copy(
                src_ref=src, dst_ref=dst_ref, send_sem=send_sems.at[7 * a + k], recv_sem=recv_sems.at[7 * a + k],
                device_id=to, device_id_type=MESH)

        started = []
        halves = []
        for a in range(na):
            nl = ins[a].shape[0]
            hl = nl // 2
            mine = pl.ds(c * hl, hl)
            theirs = pl.ds((1 - c) * hl, hl)
            halves.append((mine, theirs))
            for k in range(3):
                cp = copy(a, k, ins[a].at[mine], dst(a, me, mine), (*chips[k], c))
                cp.start()
                started.append(cp)
            own = copy(a, 6, ins[a], dst(a, me, pl.ds(0, nl)), sibling)
            own.start()
            started.append(own)
        for a in range(na):
            mine, _ = halves[a]
            for k in range(3):
                landed = dst(a, chip_ids[k], mine)
                copy(a, k, landed, landed, (*chips[k], c)).wait_recv()
                fw = copy(a, 3 + k, landed, landed, sibling)
                fw.start()
                started.append(fw)
        for a in range(na):
            _, theirs = halves[a]
            for k in range(3):
                got = dst(a, chip_ids[k], theirs)
                copy(a, 3 + k, got, got, sibling).wait_recv()
            whole = dst(a, me, pl.ds(0, ins[a].shape[0]))
            copy(a, 6, whole, whole, sibling).wait_recv()
        for cp in started:
            cp.wait_send()

    return pl.pallas_call(
        body, name=name, in_specs=_any_specs(na), out_specs=_any_specs(na), out_shape=out_shape,
        scratch_shapes=[pltpu.SemaphoreType.DMA((7 * na,)), pltpu.SemaphoreType.DMA((7 * na,))],
        compiler_params=pltpu.CompilerParams(has_side_effects=True),
    )(*shards)


def _remote(src, dst, send_sems, recv_sems, k, to):
    return pltpu.make_async_remote_copy(src_ref=src, dst_ref=dst, send_sem=send_sems.at[k], recv_sem=recv_sems.at[k],
                                        device_id=to, device_id_type=MESH)


LAYER_HOW = ("lead", "rows", "cols", "rows")


def _layer_place(ref, how, chip, shard_shape, start, size):
    r, c = shard_shape
    if how == "lead":
        return ref.at[chip, :, pl.ds(start, size), :]
    if how == "rows":
        return ref.at[:, pl.ds(pl.multiple_of(chip * r + start, HALO), size), :]
    return ref.at[:, pl.ds(start, size), pl.ds(pl.multiple_of(chip * c, LANES), c)]


def weight_rider_ici(shards, hows, layer):
    shapes = [tuple(s.shape[1:]) for s in shards]
    out_shapes = []
    for (r, c), how, s in zip(shapes, hows, shards):
        shp = {"lead": (N_CHIPS, 1, r, c), "rows": (1, N_CHIPS * r, c), "cols": (1, r, N_CHIPS * c)}[how]
        out_shapes.append(jax.ShapeDtypeStruct(shp, s.dtype))

    def copies(ins, outs, send_sems, recv_sems):
        x, y, c = _coords()
        me = 2 * x + y
        chips, chip_ids = _ici_peers(x, y)
        sibling = (x, y, 1 - c)
        pairs = []
        for a, (shape, how) in enumerate(zip(shapes, hows)):
            half = shape[0] // 2
            mine = pl.multiple_of(c * half, HALO)
            src = ins[a].at[pl.ds(layer, 1)]
            for k in range(3):
                to = (*chips[k], c)
                land = _layer_place(outs[a], how, chip_ids[k], shape, mine, half)
                pairs.append((_remote(src.at[:, pl.ds(mine, half), :], _layer_place(outs[a], how, me, shape, mine, half),
                                      send_sems, recv_sems, 4 * a + k, to),
                              _remote(land, land, send_sems, recv_sems, 4 * a + k, to)))
            whole = _layer_place(outs[a], how, me, shape, 0, shape[0])
            pairs.append((_remote(src, whole, send_sems, recv_sems, 4 * a + 3, sibling),
                          _remote(whole, whole, send_sems, recv_sems, 4 * a + 3, sibling)))
        return pairs

    return Rider(list(shards), out_shapes, {}, 4 * len(shards), copies)


def weight_rider_d2d(bufs, shapes, hows):
    def copies(ins, outs, send_sems, recv_sems):
        x, y, c = _coords()
        _, chip_ids = _ici_peers(x, y)
        sibling = (x, y, 1 - c)
        pairs = []
        for a, (shape, how) in enumerate(zip(shapes, hows)):
            half = shape[0] // 2
            mine = pl.multiple_of(c * half, HALO)
            theirs = pl.multiple_of((1 - c) * half, HALO)
            for k in range(3):
                land = _layer_place(outs[a], how, chip_ids[k], shape, theirs, half)
                pairs.append((_remote(_layer_place(ins[a], how, chip_ids[k], shape, mine, half),
                                      _layer_place(outs[a], how, chip_ids[k], shape, mine, half),
                                      send_sems, recv_sems, 3 * a + k, sibling),
                              _remote(land, land, send_sems, recv_sems, 3 * a + k, sibling)))
        return pairs

    return Rider(list(bufs), [jax.ShapeDtypeStruct(b.shape, b.dtype) for b in bufs],
                 {a: a for a in range(len(bufs))}, 3 * len(bufs), copies)


def grads_rider_sibling(arrs):
    def copies(ins, outs, send_sems, recv_sems):
        x, y, c = _coords()
        sibling = (x, y, 1 - c)
        pairs = []
        for a in range(len(arrs)):
            r2 = ins[a].shape[1] // 2
            src = ins[a].at[:, pl.ds(pl.multiple_of((1 - c) * r2, 8), r2), :]
            pairs.append((_remote(src, outs[a], send_sems, recv_sems, a, sibling),
                          _remote(outs[a], outs[a], send_sems, recv_sems, a, sibling)))
        return pairs

    return Rider(list(arrs), [jax.ShapeDtypeStruct((a.shape[0], a.shape[1] // 2, a.shape[2]), a.dtype) for a in arrs],
                 {}, len(arrs), copies)


def chip_sum(g, recv, name):
    nch, r, c = g.shape
    r2 = r // 2
    tr = _tile(r2, 256, 16)
    nb = r2 // tr

    def body(g0_ref, g1_ref, r_ref, o32_ref, o16_ref):
        s = jnp.where(lax.axis_index("c") == 0, g0_ref[...], g1_ref[...]) + r_ref[...]
        o32_ref[...] = s
        o16_ref[...] = s.astype(BF16)

    here = pl.BlockSpec((1, tr, c), lambda i, j: (i, j, 0))
    return pl.pallas_call(
        body, name=name, grid=(nch, nb),
        in_specs=[here, pl.BlockSpec((1, tr, c), lambda i, j: (i, nb + j, 0)), here],
        out_specs=[here, here],
        out_shape=[jax.ShapeDtypeStruct((nch, r2, c), F32), jax.ShapeDtypeStruct((nch, r2, c), BF16)],
        compiler_params=_params(("parallel", "parallel")),
    )(g, g, recv)


def grads_rider_chips(arrs):
    def copies(ins, outs, send_sems, recv_sems):
        x, y, c = _coords()
        chips, chip_ids = _ici_peers(x, y)
        pairs = []
        for a in range(len(arrs)):
            for k in range(3):
                to = (*chips[k], c)
                pairs.append((_remote(ins[a].at[chip_ids[k]], outs[a].at[k], send_sems, recv_sems, 3 * a + k, to),
                              _remote(outs[a].at[k], outs[a].at[k], send_sems, recv_sems, 3 * a + k, to)))
        return pairs

    return Rider(list(arrs), [jax.ShapeDtypeStruct((3, *a.shape[1:]), a.dtype) for a in arrs], {}, 3 * len(arrs),
                 copies)


def grad_sum(p32, recv, layer, nl, buf, name):
    _, r2, c = p32.shape
    tr = _tile(r2, 256, 16)
    nb = r2 // tr

    def body(p0_ref, p1_ref, p2_ref, p3_ref, r0_ref, r1_ref, r2_ref, *rest):
        o_ref = rest[-1]
        x, y, _ = _coords()
        chip = 2 * x + y
        own = jnp.where(chip == 0, p0_ref[...], jnp.where(chip == 1, p1_ref[...],
                                                        jnp.where(chip == 2, p2_ref[...], p3_ref[...])))
        o_ref[...] = own + r0_ref[...].astype(F32) + r1_ref[...].astype(F32) + r2_ref[...].astype(F32)

    slot = lambda k: pl.BlockSpec((1, tr, c), lambda j: (k, j, 0))
    in_specs = [slot(k) for k in range(N_CHIPS)] + [slot(k) for k in range(3)]
    args = [p32] * N_CHIPS + [recv] * 3
    aliases = {}
    if buf is not None:
        in_specs.append(pl.BlockSpec(memory_space=pl.ANY))
        args.append(buf)
        aliases = {len(args) - 1: 0}
    return pl.pallas_call(
        body, name=name, grid=(nb,), in_specs=in_specs,
        out_specs=pl.BlockSpec((1, tr, c), lambda j: (layer, j, 0)),
        out_shape=jax.ShapeDtypeStruct((nl, r2, c), F32),
        input_output_aliases=aliases,
        compiler_params=_params(("parallel",)),
    )(*args)


def grads_rider_exchange(bufs):
    def copies(ins, outs, send_sems, recv_sems):
        x, y, c = _coords()
        sibling = (x, y, 1 - c)
        return [(_remote(ins[a], outs[a], send_sems, recv_sems, a, sibling),
                 _remote(outs[a], outs[a], send_sems, recv_sems, a, sibling)) for a in range(len(bufs))]

    return Rider(list(bufs), [jax.ShapeDtypeStruct(b.shape, b.dtype) for b in bufs], {}, len(bufs), copies)


def allreduce_small(buf, name):
    r, cdim = buf.shape

    def body(x_ref, o_ref, gath, send_sems, recv_sems):
        x, y, c = _coords()
        me, sibling = (x, y, c), (x, y, 1 - c)
        chips, _ = _ici_peers(x, y)

        def slot(px, py, pc):
            return gath.at[4 * px + 2 * py + pc]

        def copy(k, block, to, src=None):
            return pltpu.make_async_remote_copy(
                src_ref=slot(*block) if src is None else src, dst_ref=slot(*block),
                send_sem=send_sems.at[k], recv_sem=recv_sems.at[k], device_id=to, device_id_type=MESH)

        gath[4 * x + 2 * y + c] = x_ref[...]
        first = [copy(0, me, sibling, src=x_ref)]
        first += [copy(1 + j, me, (*chip, c), src=x_ref) for j, chip in enumerate(chips)]
        for cp in first:
            cp.start()
        passed = [copy(4 + j, (*chip, c), sibling) for j, chip in enumerate(chips)]
        for j, chip in enumerate(chips):
            copy(1 + j, (*chip, c), me).wait_recv()
            passed[j].start()
        copy(0, sibling, me).wait_recv()
        for j, chip in enumerate(chips):
            copy(4 + j, (*chip, 1 - c), me).wait_recv()
        for cp in first + passed:
            cp.wait_send()
        acc = gath[0]
        for d in range(1, 8):
            acc = acc + gath[d]
        o_ref[...] = acc

    return pl.pallas_call(
        body, name=name,
        in_specs=[pl.BlockSpec(memory_space=pltpu.VMEM)], out_specs=pl.BlockSpec(memory_space=pltpu.VMEM),
        out_shape=jax.ShapeDtypeStruct((r, cdim), F32),
        scratch_shapes=[pltpu.VMEM((8, r, cdim), F32), pltpu.SemaphoreType.DMA((7,)), pltpu.SemaphoreType.DMA((7,))],
        compiler_params=pltpu.CompilerParams(has_side_effects=True),
    )(buf)


def _expand_heads(v):
    return jnp.repeat(v.astype(F32), HEAD_DIM).reshape(1, -1)


def _pad_lanes(v):
    return jnp.pad(v.astype(F32), (0, LANES - v.shape[0])).reshape(1, LANES)


def local_step(x, tgt, p, comm, cols):
    nl = p["norm_mix_w"].shape[0]
    d = x.shape[1]
    cw = p["short_conv_w"].shape[2]
    si = p["ssd_norm_w"].shape[1]
    ff, npad = comm.ff, comm.npad
    nh = si // HEAD_DIM
    gn = GROUPS * STATE
    dt_off = 3 * cw + si + si + 2 * gn
    assert cols == dt_off + nh and nh <= LANES and dt_off % LANES == 0 and npad == dt_off + LANES
    pieces = [(0, cw), (cw, cw), (2 * cw, cw), (3 * cw, si), (3 * cw + si, si), (3 * cw + 2 * si, 2 * gn),
              (dt_off, LANES)]

    saved = []
    for l in range(nl):
        nw1 = p["norm_mix_w"][l].reshape(1, d)
        nw2 = p["norm_mlp_w"][l].reshape(1, d)
        kw3 = p["short_conv_w"][l]
        kwx, kwb = p["ssd_conv_w"][l][:, :si], p["ssd_conv_w"][l][:, si:]
        bx, bb = p["ssd_conv_b"][l][:si].reshape(1, si), p["ssd_conv_b"][l][si:].reshape(1, 2 * gn)
        dtb = _pad_lanes(p["dt_bias"][l])
        aneg = _pad_lanes(-jnp.exp(p["a_log"][l]))
        dsk = _expand_heads(p["d_skip"][l])
        snw = p["ssd_norm_w"][l].reshape(1, si)
        ssd_args = (kwx, kwb, bx, bb, dtb, aneg, dsk, snw)

        w_in = comm.weight(l, "w_in")
        (proj, h), sent = norm_matmul(x, nw1, w_in, 0, dt_off, 0, BF16, "in_proj", rider=comm.rider("in_proj", l))
        comm.done("in_proj", l, sent)
        (dt_raw,), _ = norm_matmul(x, nw1, w_in, 0, LANES, dt_off // LANES, F32, "dt_proj", emit_h=False)
        y_mix = conv_mixer_fwd(proj, kw3, cw, cw + si, "conv_mixer_fwd")
        (y_mix, *ssd_saved), sent = ssd_fwd(proj, dt_raw, y_mix, *ssd_args, cw, si, "ssd_fwd",
                                            rider=comm.rider("ssd_fwd", l))
        comm.done("ssd_fwd", l, sent)
        x2, _ = matmul(y_mix, comm.weight(l, "w_out"), 0, False, d, F32, "out_proj", residual=x)
        (up, h2), sent = norm_matmul(x2, nw2, comm.weight(l, "w_up"), 0, ff, 0, BF16, "up_proj",
                                     rider=comm.rider("up_proj", l))
        comm.done("up_proj", l, sent)
        x3, sent = matmul(up, comm.weight(l, "w_down"), 0, False, d, F32, "down_proj", lhs_fn=_relu2, residual=x2,
                          rider=comm.rider("down_proj", l))
        comm.done("down_proj", l, sent)
        saved.append((x, h, proj, dt_raw, y_mix, ssd_saved, x2, h2, up, nw1, nw2, kw3, ssd_args))
        x = x3

    dx, dwf, lvec = final_loss(x, p["final_norm_w"].reshape(1, d), tgt, "final_loss")
    loss = jnp.sum(lvec)

    names = ("norm_mix_w", "short_conv_w", "ssd_conv_w", "ssd_conv_b", "dt_bias", "a_log", "d_skip",
             "ssd_norm_w", "norm_mlp_w")
    grads = {k: [None] * nl for k in names}
    for l in reversed(range(nl)):
        x0, h, proj, dt_raw, y_mix, ssd_saved, x2, h2, up, nw1, nw2, kw3, ssd_args = saved[l]
        kwx, kwb, _, _, dtb, aneg, dsk, snw = ssd_args
        dup, sent = matmul(dx, comm.weight(l, "w_down"), 0, True, ff, BF16, "down_bwd", relu_gate=up,
                           rider=comm.rider("down_bwd", l))
        comm.done("down_bwd", l, sent)
        comm.take_gradient(l, "w_down", matmul_tn(up, dx, "down_wgrad", a_fn=_relu2))
        (dx2, dnw2), _ = matmul_normbwd([dup], [(0, ff)], comm.weight(l, "w_up"), 0, x2, nw2, dx, "up_bwd")
        comm.take_gradient(l, "w_up", matmul_tn(h2, dup, "up_wgrad", by_chip=True))
        grads["norm_mlp_w"][l] = dnw2.reshape(d)
        dy, sent = matmul(dx2, comm.weight(l, "w_out"), 0, True, cw + si, BF16, "out_bwd",
                          rider=comm.rider("out_bwd", l))
        comm.done("out_bwd", l, sent)
        comm.take_gradient(l, "w_out", matmul_tn(y_mix, dx2, "out_wgrad"))
        dub, duc, duh, dkw3 = conv_mixer_bwd(proj, dy, kw3, cw, "conv_mixer_bwd")
        (dz, dxs, dbc, ddt, dkwx, dkwb, dbx, dbb, ddtb, da, ddsk, dsnw), sent = ssd_bwd(
            proj, dt_raw, *ssd_saved, dy, kwx, kwb, dtb, aneg, dsk, snw, cw, si, "ssd_bwd",
            rider=comm.rider("ssd_bwd", l))
        comm.done("ssd_bwd", l, sent)
        dpieces = [dub, duc, duh, dz, dxs, dbc, ddt]
        (dxl, dnw1), sent = matmul_normbwd(dpieces, pieces, comm.weight(l, "w_in"), 0, x0, nw1, dx2, "in_bwd",
                                           rider=comm.rider("in_bwd", l))
        comm.done("in_bwd", l, sent)
        comm.take_gradient(l, "w_in", split_to_chips(
            [matmul_tn(h, dp, "in_wgrad_%d" % i) for i, dp in enumerate(dpieces)], cols, "in_wgrad_split"))
        grads["norm_mix_w"][l] = dnw1.reshape(d)
        grads["short_conv_w"][l] = dkw3
        grads["ssd_conv_w"][l] = jnp.concatenate([dkwx, dkwb], axis=1)
        grads["ssd_conv_b"][l] = jnp.concatenate([dbx, dbb], axis=1).reshape(-1)
        grads["dt_bias"][l] = ddtb[0, :nh]
        grads["a_log"][l] = da[0, :nh] * aneg[0, :nh]
        grads["d_skip"][l] = jnp.sum(ddsk.reshape(nh, HEAD_DIM), axis=1)
        grads["ssd_norm_w"][l] = dsnw.reshape(si)
        dx = dxl

    grads = {k: jnp.stack(v) for k, v in grads.items()}
    grads["final_norm_w"] = dwf.reshape(d)
    return loss, dx, grads


BIG = ("w_in", "w_out", "w_up", "w_down")
SMALL_SHARDED = ("short_conv_w", "ssd_conv_w")
SMALL_REPL = ("norm_mix_w", "ssd_conv_b", "dt_bias", "a_log", "d_skip", "ssd_norm_w", "norm_mlp_w", "final_norm_w")
WEIGHTS = ("norm_mix_w", "w_in", "short_conv_w", "ssd_conv_w", "ssd_conv_b", "dt_bias", "a_log", "d_skip",
           "ssd_norm_w", "w_out", "norm_mlp_w", "w_up", "w_down", "final_norm_w")
SMALL_COLS = 1024


def _pack_small(named):
    flat = jnp.concatenate([v.reshape(-1).astype(F32) for v in named])
    n = flat.shape[0]
    rows = -(-n // SMALL_COLS)
    rows = -(-rows // 8) * 8
    return jnp.pad(flat, (0, rows * SMALL_COLS - n)).reshape(rows, SMALL_COLS)


def _unpack_small(buf, like):
    flat = buf.reshape(-1)
    out, off = [], 0
    for v in like:
        out.append(flat[off:off + v.size].reshape(v.shape))
        off += v.size
    return out


class ChipComm:
    IO = ("w_in", "w_out")
    MLP = ("w_up", "w_down")

    def __init__(self, shards, nl, npad):
        self.shards, self.nl, self.npad = shards, nl, npad
        self.how = dict(zip(BIG, LAYER_HOW))
        self.ff = N_CHIPS * shards["w_up"].shape[2]
        self.w = {}
        self.landed = {}
        self.grad = {}
        self.sums = {}
        self.bufs = {k: None for k in BIG}
        first = run_rider(self._ici(self.IO, 0), "gather_first_ici")
        self._gathered(self.IO, 0, run_rider(self._d2d(self.IO, first), "gather_first_d2d"))

    def _ici(self, group, l):
        return weight_rider_ici([self.shards[k] for k in group], [self.how[k] for k in group], l)

    def _d2d(self, group, landed):
        return weight_rider_d2d(landed, [tuple(self.shards[k].shape[1:]) for k in group],
                                [self.how[k] for k in group])

    def _gathered(self, group, l, arrays):
        for k, g in zip(group, arrays):
            self.w[(l, k)] = join_from_chips(g, self.npad, "w_in_join") if k == "w_in" else g

    def _to_sibling(self, group, l):
        return grads_rider_sibling([self.grad[(l, k)] for k in group])

    def _summed(self, group, l, from_sibling):
        self.sums[group] = (l, [chip_sum(self.grad.pop((l, k)), r, "chip_sum") for k, r in zip(group, from_sibling)])

    def _to_chips(self, group):
        return grads_rider_chips([s[1] for s in self.sums[group][1]])

    def _reduced(self, group, from_chips):
        l, sums = self.sums.pop(group)
        for k, s, r in zip(group, sums, from_chips):
            self.bufs[k] = grad_sum(s[0], r, l, self.nl, self.bufs[k], "grad_sum")

    def weight(self, l, name):
        return self.w[(l, name)]

    def take_gradient(self, l, name, g):
        self.grad[(l, name)] = g if g.ndim == 3 else g.reshape(N_CHIPS, g.shape[0] // N_CHIPS, g.shape[1])

    def rider(self, point, l):
        more = l + 1 < self.nl
        if point == "in_proj":
            return self._ici(self.MLP, l)
        if point == "ssd_fwd":
            return self._d2d(self.MLP, self.landed[self.MLP])
        if point == "up_proj":
            return self._ici(self.IO, l + 1) if more else None
        if point == "down_proj":
            return self._d2d(self.IO, self.landed[self.IO]) if more else None
        if point == "down_bwd":
            return self._to_sibling(self.IO, l + 1) if more else None
        if point == "out_bwd":
            return self._to_sibling(self.MLP, l)
        if point == "ssd_bwd":
            return self._to_chips(self.IO) if more else None
        return self._to_chips(self.MLP)

    def done(self, point, l, results):
        if not results:
            return
        if point in ("in_proj", "up_proj"):
            self.landed[self.MLP if point == "in_proj" else self.IO] = results
        elif point == "ssd_fwd":
            self._gathered(self.MLP, l, results)
        elif point == "down_proj":
            self._gathered(self.IO, l + 1, results)
        elif point == "down_bwd":
            self._summed(self.IO, l + 1, results)
        elif point == "out_bwd":
            self._summed(self.MLP, l, results)
        elif point == "ssd_bwd":
            self._reduced(self.IO, results)
        else:
            self._reduced(self.MLP, results)

    def finish(self):
        self._summed(self.IO, 0, run_rider(self._to_sibling(self.IO, 0), "grads_to_sibling"))
        self._reduced(self.IO, run_rider(self._to_chips(self.IO), "grads_to_chips"))
        mine = [self.bufs[k] for k in BIG]
        theirs = run_rider(grads_rider_exchange(mine), "grads_exchange")
        return dict(zip(BIG, zip(mine, theirs)))


def kernel(x, norm_mix_w, w_in, short_conv_w, ssd_conv_w, ssd_conv_b, dt_bias, a_log, d_skip, ssd_norm_w, w_out, norm_mlp_w, w_up, w_down, final_norm_w, loss_target, m_norm_mix_w, m_w_in, m_short_conv_w, m_ssd_conv_w, m_ssd_conv_b, m_dt_bias, m_a_log, m_d_skip, m_ssd_norm_w, m_w_out, m_norm_mlp_w, m_w_up, m_w_down, m_final_norm_w, v_norm_mix_w, v_w_in, v_short_conv_w, v_ssd_conv_w, v_ssd_conv_b, v_dt_bias, v_a_log, v_d_skip, v_ssd_norm_w, v_w_out, v_norm_mlp_w, v_w_up, v_w_down, v_final_norm_w):
    w = dict(norm_mix_w=norm_mix_w, w_in=w_in, short_conv_w=short_conv_w, ssd_conv_w=ssd_conv_w,
             ssd_conv_b=ssd_conv_b, dt_bias=dt_bias, a_log=a_log, d_skip=d_skip, ssd_norm_w=ssd_norm_w, w_out=w_out,
             norm_mlp_w=norm_mlp_w, w_up=w_up, w_down=w_down, final_norm_w=final_norm_w)
    m = dict(norm_mix_w=m_norm_mix_w, w_in=m_w_in, short_conv_w=m_short_conv_w, ssd_conv_w=m_ssd_conv_w,
             ssd_conv_b=m_ssd_conv_b, dt_bias=m_dt_bias, a_log=m_a_log, d_skip=m_d_skip, ssd_norm_w=m_ssd_norm_w,
             w_out=m_w_out, norm_mlp_w=m_norm_mlp_w, w_up=m_w_up, w_down=m_w_down, final_norm_w=m_final_norm_w)
    v = dict(norm_mix_w=v_norm_mix_w, w_in=v_w_in, short_conv_w=v_short_conv_w, ssd_conv_w=v_ssd_conv_w,
             ssd_conv_b=v_ssd_conv_b, dt_bias=v_dt_bias, a_log=v_a_log, d_skip=v_d_skip, ssd_norm_w=v_ssd_norm_w,
             w_out=v_w_out, norm_mlp_w=v_norm_mlp_w, w_up=v_w_up, w_down=v_w_down, final_norm_w=v_final_norm_w)
    xi, yi, ci = lax.axis_index("x"), lax.axis_index("y"), lax.axis_index("c")
    chip = 2 * xi + yi
    nl = w_up.shape[0]
    cols = N_CHIPS * w_in.shape[2]
    npad = cols // LANES * LANES + LANES

    full = dict(w)
    small_gathered = gather_weights([w[k] for k in SMALL_SHARDED], ["lead"] * len(SMALL_SHARDED), "gather_small")
    for k, g4 in zip(SMALL_SHARDED, small_gathered):
        full[k] = jnp.concatenate([g4[j] for j in range(N_CHIPS)], axis=2)
    comm = ChipComm({k: w[k].astype(BF16) for k in BIG}, nl, npad)

    loss, grad_x, grads = local_step(x[0], loss_target[0], full, comm, cols)
    loss = lax.psum(loss, ("x", "y", "c"))
    halves = comm.finish()
    g_shard = {}

    small_names = SMALL_REPL + SMALL_SHARDED
    small_sum = allreduce_small(_pack_small([grads[k] for k in small_names]), "allreduce_small")
    for k, g in zip(small_names, _unpack_small(small_sum, [grads[k] for k in small_names])):
        if k in SMALL_SHARDED:
            width = w[k].shape[2]
            g = lax.dynamic_slice_in_dim(g, chip * width, width, axis=2)
        g_shard[k] = g

    delta, new_m, new_v = {}, {}, {}
    for k in BIG:
        g_shard[k], delta[k], new_m[k], new_v[k] = adamw_halves(w[k], *halves[k], m[k], v[k], "adamw_%s" % k)
    packed = [_pack_small([d_[k] for k in small_names]) for d_ in (w, g_shard, m, v)]
    outs = adamw(*packed, "adamw_small")
    for d_, buf in zip((delta, new_m, new_v), outs):
        for k, val in zip(small_names, _unpack_small(buf, [w[k] for k in small_names])):
            d_[k] = val

    return (loss, grad_x[None], *[g_shard[k] for k in WEIGHTS], *[delta[k] for k in WEIGHTS],
            *[new_m[k] for k in WEIGHTS], *[new_v[k] for k in WEIGHTS])
```

```python
import functools

import jax
import jax.numpy as jnp
from jax import lax
from jax.experimental import pallas as pl
from jax.experimental.pallas import tpu as pltpu

F32 = jnp.float32
BF16 = jnp.bfloat16

EPS = 1e-5
HEAD_DIM = 64
STATE = 128
GROUPS = 2
SHORT_K = 3
SSD_K = 4
LANES = 128
PAIR = LANES // HEAD_DIM
SCAN_CHUNK = 256
HALO = 16
N_CHIPS = 4
VMEM_LIMIT = 56 * 1024 * 1024

ADAM_LR = 0.001
ADAM_B1 = 0.9
ADAM_B2 = 0.999
ADAM_EPS = 1e-08
ADAM_WD = 0.01
ADAM_STEP = 10

MESH = pl.DeviceIdType.MESH


def _params(sem):
    return pltpu.CompilerParams(dimension_semantics=sem, vmem_limit_bytes=VMEM_LIMIT)


def _tile(n, cap, quantum):
    if n <= cap:
        return n
    best = None
    for t in range(quantum, cap + 1, quantum):
        if n % t == 0:
            best = t
    assert best is not None, (n, cap, quantum)
    return best


def _dot(a, b):
    return jnp.dot(a, b, preferred_element_type=F32)


def _dot_nt(a, b):
    return lax.dot_general(a, b, (((1,), (1,)), ((), ())), preferred_element_type=F32)


def _dot_tn(a, b):
    return lax.dot_general(a, b, (((0,), (0,)), ((), ())), preferred_element_type=F32)


def _dot_exact(a, b):
    return jnp.dot(a, b, precision=lax.Precision.HIGHEST, preferred_element_type=F32)


def _sigmoid(x):
    return pl.reciprocal(1.0 + jnp.exp(-x), approx=True)


def _softplus(x):
    return jnp.maximum(x, 0.0) + jnp.log(1.0 + jnp.exp(-jnp.abs(x)))


def _relu2(v):
    return jnp.square(jnp.maximum(v, 0.0))


class Rider:
    def __init__(self, ins, out_shapes, aliases, n_sems, copies):
        self.ins, self.out_shapes, self.aliases, self.n_sems, self.copies = ins, out_shapes, aliases, n_sems, copies


def _any_specs(n):
    return [pl.BlockSpec(memory_space=pl.ANY)] * n


def _ride(body, grid, in_specs, out_specs, out_shape, scratch, args, aliases, rider, sem, name):
    n_in, n_out, n_scr = len(in_specs), len(out_specs), len(scratch)
    if rider is None:
        outs = pl.pallas_call(
            body, name=name, grid=grid, in_specs=in_specs, out_specs=out_specs, out_shape=out_shape,
            scratch_shapes=scratch, input_output_aliases=aliases, compiler_params=_params(sem))(*args)
        return list(outs), []
    ri, ro = len(rider.ins), len(rider.out_shapes)
    last = tuple(g - 1 for g in grid)

    def wrapped(*refs):
        ins = refs[:n_in]
        r_ins = refs[n_in:n_in + ri]
        outs = refs[n_in + ri:n_in + ri + n_out]
        r_outs = refs[n_in + ri + n_out:n_in + ri + n_out + ro]
        scr = refs[n_in + ri + n_out + ro:n_in + ri + n_out + ro + n_scr]
        send_sems, recv_sems = refs[-2:]
        ids = [pl.program_id(a) for a in range(len(grid))]
        at_first = functools.reduce(jnp.logical_and, [i == 0 for i in ids])
        at_last = functools.reduce(jnp.logical_and, [i == e for i, e in zip(ids, last)])

        @pl.when(at_first)
        def _():
            for cp, _ in rider.copies(r_ins, r_outs, send_sems, recv_sems):
                cp.start()

        body(*ins, *outs, *scr)

        @pl.when(at_last)
        def _():
            for cp, landed in rider.copies(r_ins, r_outs, send_sems, recv_sems):
                cp.wait_send()
                landed.wait_recv()

    all_aliases = dict(aliases)
    all_aliases.update({n_in + a: n_out + b for a, b in rider.aliases.items()})
    outs = pl.pallas_call(
        wrapped, name=name, grid=grid, in_specs=list(in_specs) + _any_specs(ri),
        out_specs=list(out_specs) + _any_specs(ro), out_shape=list(out_shape) + list(rider.out_shapes),
        scratch_shapes=list(scratch) + [pltpu.SemaphoreType.DMA((rider.n_sems,)),
                                        pltpu.SemaphoreType.DMA((rider.n_sems,))],
        input_output_aliases=all_aliases, compiler_params=_params(sem))(*args, *rider.ins)
    return list(outs[:n_out]), list(outs[n_out:])


def run_rider(rider, name):
    ri, ro = len(rider.ins), len(rider.out_shapes)

    def body(*refs):
        send_sems, recv_sems = refs[-2:]
        pairs = rider.copies(refs[:ri], refs[ri:ri + ro], send_sems, recv_sems)
        for cp, _ in pairs:
            cp.start()
        for cp, landed in pairs:
            cp.wait_send()
            landed.wait_recv()

    return list(pl.pallas_call(
        body, name=name, in_specs=_any_specs(ri), out_specs=_any_specs(ro), out_shape=list(rider.out_shapes),
        scratch_shapes=[pltpu.SemaphoreType.DMA((rider.n_sems,)), pltpu.SemaphoreType.DMA((rider.n_sems,))],
        input_output_aliases=dict(rider.aliases),
        compiler_params=pltpu.CompilerParams(has_side_effects=True))(*rider.ins))


def norm_matmul(x, nw, w, layer, n, col0, out_dtype, name, emit_h=True, rider=None):
    t, d = x.shape
    mxu_cols = 2 * LANES
    tn = _tile(n, 1536, mxu_cols if n % mxu_cols == 0 else LANES)
    if n % mxu_cols == 0 and tn < 1024 <= n:
        tn = _tile(n, 3072, mxu_cols)
    tm = _tile(t, 512 if tn > 1536 else 1024, 8)
    nj = n // tn

    def body(x_ref, nw_ref, w_ref, o_ref, h_ref):
        @pl.when(pl.program_id(1) == 0)
        def _():
            xf = x_ref[...]
            r = lax.rsqrt(jnp.mean(xf * xf, axis=-1, keepdims=True) + EPS)
            h_ref[...] = (xf * r * nw_ref[...]).astype(BF16)

        o_ref[...] = _dot(h_ref[...], w_ref[...]).astype(out_dtype)

    out_specs = [pl.BlockSpec((tm, tn), lambda i, j: (i, j))]
    out_shape = [jax.ShapeDtypeStruct((t, n), out_dtype)]
    if emit_h:
        out_specs.append(pl.BlockSpec((tm, d), lambda i, j: (i, 0)))
        out_shape.append(jax.ShapeDtypeStruct((t, d), BF16))
    return _ride(
        body, (t // tm, nj),
        [pl.BlockSpec((tm, d), lambda i, j: (i, 0)), pl.BlockSpec((1, d), lambda i, j: (0, 0)),
         pl.BlockSpec((None, d, tn), lambda i, j: (layer, 0, col0 * nj + j))],
        out_specs, out_shape, [] if emit_h else [pltpu.VMEM((tm, d), BF16)], [x, nw, w], {}, rider,
        ("parallel", "arbitrary"), name)


def matmul(lhs, w, layer, transposed, n, out_dtype, name, *, lhs_fn=None, residual=None, relu_gate=None,
           rider=None):
    t, k = lhs.shape
    tm = _tile(t, 512 if k > 2048 else 1024, 8)
    tn = _tile(n, 1024, LANES)
    staged = lhs.dtype != BF16 or lhs_fn is not None
    fn = lhs_fn if lhs_fn is not None else (lambda v: v)
    has_extra = residual is not None or relu_gate is not None
    dot = _dot_nt if transposed else _dot

    def body(*refs):
        a_ref, w_ref = refs[:2]
        extra = refs[2] if has_extra else None
        o_ref = refs[3] if has_extra else refs[2]
        if staged:
            s_ref = refs[-1]

            @pl.when(pl.program_id(1) == 0)
            def _():
                s_ref[...] = fn(a_ref[...].astype(F32)).astype(BF16)

            a_ref = s_ref
        acc = dot(a_ref[...], w_ref[...])
        if residual is not None:
            acc = acc + extra[...]
        if relu_gate is not None:
            acc = acc * (2.0 * jnp.maximum(extra[...].astype(F32), 0.0))
        o_ref[...] = acc.astype(out_dtype)

    if transposed:
        w_spec = pl.BlockSpec((None, tn, k), lambda i, j: (layer, j, 0))
    else:
        w_spec = pl.BlockSpec((None, k, tn), lambda i, j: (layer, 0, j))
    in_specs = [pl.BlockSpec((tm, k), lambda i, j: (i, 0)), w_spec]
    args = [lhs, w]
    if has_extra:
        in_specs.append(pl.BlockSpec((tm, tn), lambda i, j: (i, j)))
        args.append(residual if residual is not None else relu_gate)
    outs, extra = _ride(
        body, (t // tm, n // tn), in_specs, [pl.BlockSpec((tm, tn), lambda i, j: (i, j))],
        [jax.ShapeDtypeStruct((t, n), out_dtype)], [pltpu.VMEM((tm, k), BF16)] if staged else [], args, {},
        rider, ("parallel", "arbitrary"), name)
    return outs[0], extra


def matmul_normbwd(lhs, pieces, w, layer, x, nw, dres, name, rider=None):
    t, d = x.shape
    nl = len(lhs)
    tm = _tile(t, 256, 8)
    for off, width in pieces:
        assert off % width == 0

    def body(*refs):
        lrefs = refs[:nl]
        wrefs = refs[nl:2 * nl]
        x_ref, nw_ref, dres_ref, dx_ref, dnw_ref = refs[2 * nl:]
        dh = _dot_nt(lrefs[0][...].astype(BF16), wrefs[0][...])
        for a_ref, w_ref in zip(lrefs[1:], wrefs[1:]):
            dh = dh + _dot_nt(a_ref[...].astype(BF16), w_ref[...])
        xf = x_ref[...]
        r = lax.rsqrt(jnp.mean(xf * xf, axis=-1, keepdims=True) + EPS)
        nx = xf * r
        dn = dh * nw_ref[...]
        dx = r * (dn - nx * jnp.mean(dn * nx, axis=-1, keepdims=True))
        dx_ref[...] = dres_ref[...] + dx

        @pl.when(pl.program_id(0) == 0)
        def _():
            dnw_ref[...] = jnp.zeros_like(dnw_ref)

        dnw_ref[...] += jnp.sum(dh * nx, axis=0, keepdims=True)

    in_specs = [pl.BlockSpec((tm, width), lambda i: (i, 0)) for _, width in pieces]
    in_specs += [pl.BlockSpec((None, d, width), (lambda blk: (lambda i: (layer, 0, blk)))(off // width))
                 for off, width in pieces]
    in_specs += [pl.BlockSpec((tm, d), lambda i: (i, 0)), pl.BlockSpec((1, d), lambda i: (0, 0)),
                 pl.BlockSpec((tm, d), lambda i: (i, 0))]
    return _ride(
        body, (t // tm,), in_specs,
        [pl.BlockSpec((tm, d), lambda i: (i, 0)), pl.BlockSpec((1, d), lambda i: (0, 0))],
        [jax.ShapeDtypeStruct((t, d), F32), jax.ShapeDtypeStruct((1, d), F32)], [],
        [*lhs, *([w] * nl), x, nw, dres], {}, rider, ("arbitrary",), name)


def matmul_tn(a, b, name, *, a_fn=None, by_chip=False):
    t, k = a.shape
    n = b.shape[1]
    tk = _tile(k, 1024, LANES)
    tn = _tile(n // N_CHIPS if by_chip else n, 1024, LANES)
    tt = _tile(t, 1024, 8)
    nt = t // tt
    fn = a_fn if a_fn is not None else (lambda v: v)

    def body(a_ref, b_ref, o_ref, acc_ref):
        @pl.when(pl.program_id(2) == 0)
        def _():
            acc_ref[...] = jnp.zeros_like(acc_ref)

        av = a_ref[...]
        if a_fn is not None:
            av = fn(av.astype(F32))
        acc_ref[...] += _dot_tn(av.astype(BF16), b_ref[...].astype(BF16))

        @pl.when(pl.program_id(2) == nt - 1)
        def _():
            o_ref[...] = acc_ref[...]

    if by_chip:
        per = n // N_CHIPS // tn
        out_spec = pl.BlockSpec((None, tk, tn), lambda i, j, s: (j // per, i, j % per))
        out_shape = jax.ShapeDtypeStruct((N_CHIPS, k, n // N_CHIPS), F32)
    else:
        out_spec = pl.BlockSpec((tk, tn), lambda i, j, s: (i, j))
        out_shape = jax.ShapeDtypeStruct((k, n), F32)
    return pl.pallas_call(
        body, name=name, grid=(k // tk, n // tn, nt),
        in_specs=[pl.BlockSpec((tt, tk), lambda i, j, s: (s, i)),
                  pl.BlockSpec((tt, tn), lambda i, j, s: (s, j))],
        out_specs=out_spec, out_shape=out_shape,
        scratch_shapes=[pltpu.VMEM((tk, tn), F32)],
        compiler_params=_params(("parallel", "parallel", "arbitrary")),
    )(a, b)


def split_to_chips(pieces, cols, name):
    d = pieces[0].shape[0]
    widths = [p.shape[1] for p in pieces]
    w = cols // N_CHIPS
    tr = _tile(d, 256, 8)
    npc = len(pieces)

    def body(*refs):
        o_ref, row = refs[npc], refs[npc + 1]
        off = 0
        for r, n in zip(refs[:npc], widths):
            row[:, off:off + n] = r[...]
            off += n
        for j in range(N_CHIPS):
            o_ref[j] = row[:, j * w:(j + 1) * w]

    return pl.pallas_call(
        body, name=name, grid=(d // tr,),
        in_specs=[pl.BlockSpec((tr, n), lambda i: (i, 0)) for n in widths],
        out_specs=pl.BlockSpec((N_CHIPS, tr, w), lambda i: (0, i, 0)),
        out_shape=jax.ShapeDtypeStruct((N_CHIPS, d, w), F32),
        scratch_shapes=[pltpu.VMEM((tr, sum(widths)), F32)],
        compiler_params=_params(("parallel",)),
    )(*pieces)


def join_from_chips(g4, npad, name):
    _, nl, d, w = g4.shape
    tr = _tile(d, 256, HALO)

    def body(g_ref, o_ref):
        for j in range(N_CHIPS):
            o_ref[:, j * w:(j + 1) * w] = g_ref[j]
        o_ref[:, N_CHIPS * w:] = jnp.zeros((tr, npad - N_CHIPS * w), o_ref.dtype)

    return pl.pallas_call(
        body, name=name, grid=(nl, d // tr),
        in_specs=[pl.BlockSpec((N_CHIPS, None, tr, w), lambda l, i: (0, l, i, 0))],
        out_specs=pl.BlockSpec((None, tr, npad), lambda l, i: (l, i, 0)),
        out_shape=jax.ShapeDtypeStruct((nl, d, npad), g4.dtype),
        compiler_params=_params(("parallel", "parallel")),
    )(g4)


def conv_mixer_fwd(proj, kw, cw, out_cols, name):
    t = proj.shape[0]
    tm = _tile(t, 1024, HALO)
    tc = _tile(cw, 1024, LANES)
    nj = cw // tc
    hb = tm // HALO

    def body(ub_ref, uc_ref, uh_ref, ucp_ref, uhp_ref, kw_ref, y_ref):
        i = pl.program_id(0)
        taps = [kw_ref[pl.ds(k, 1), :] for k in range(SHORT_K)]
        row = lax.broadcasted_iota(jnp.int32, (8, tc), 0)
        vp = ucp_ref[...].astype(F32) * uhp_ref[...].astype(F32)

        def conv(block, before):
            acc = taps[SHORT_K - 1] * block
            for k in range(SHORT_K - 1):
                s = SHORT_K - 1 - k
                acc = acc + taps[k] * jnp.where(row >= s, pltpu.roll(block, s, 0), pltpu.roll(before, s, 0))
            return acc

        def strip(s, before):
            rows = pl.ds(pl.multiple_of(s * HALO, HALO), HALO)
            v = uc_ref[rows, :].astype(F32) * uh_ref[rows, :].astype(F32)
            top, bottom = v[0:8], v[8:HALO]
            cv = jnp.concatenate([conv(top, before), conv(bottom, top)], axis=0)
            y_ref[rows, :] = (ub_ref[rows, :].astype(F32) * cv).astype(BF16)
            return bottom

        lax.fori_loop(0, tm // HALO, strip, jnp.where(i > 0, vp[8:HALO], 0.0))

    prev = lambda off: (lambda i, j: (jnp.maximum(i * hb - 1, 0), off + j))
    return pl.pallas_call(
        body, name=name, grid=(t // tm, nj),
        in_specs=[pl.BlockSpec((tm, tc), lambda i, j: (i, j)),
                  pl.BlockSpec((tm, tc), lambda i, j: (i, nj + j)),
                  pl.BlockSpec((tm, tc), lambda i, j: (i, 2 * nj + j)),
                  pl.BlockSpec((HALO, tc), prev(nj)),
                  pl.BlockSpec((HALO, tc), prev(2 * nj)),
                  pl.BlockSpec((SHORT_K, tc), lambda i, j: (0, j))],
        out_specs=pl.BlockSpec((tm, tc), lambda i, j: (i, j)),
        out_shape=jax.ShapeDtypeStruct((t, out_cols), BF16),
        compiler_params=_params(("parallel", "parallel")),
    )(proj, proj, proj, proj, proj, kw)


def conv_mixer_bwd(proj, dy, kw, cw, name):
    t = proj.shape[0]
    tm = _tile(t, 1024, HALO)
    tc = _tile(cw, 1024, LANES)
    nj = cw // tc
    hb = tm // HALO
    ni = t // tm
    last_hb = t // HALO - 1

    def body(ub_ref, uc_ref, uh_ref, dy_ref, ucp_ref, uhp_ref, ubn_ref, dyn_ref, kw_ref,
             dub_ref, duc_ref, duh_ref, dkw_ref):
        i = pl.program_id(1)
        nstrips = tm // HALO
        taps = [kw_ref[pl.ds(k, 1), :] for k in range(SHORT_K)]
        row = lax.broadcasted_iota(jnp.int32, (8, tc), 0)
        vp = ucp_ref[...].astype(F32) * uhp_ref[...].astype(F32)
        dcvn = dyn_ref[...].astype(F32) * ubn_ref[...].astype(F32)

        def shifted(block, before, s):
            return jnp.where(row >= s, pltpu.roll(block, s, 0), pltpu.roll(before, s, 0))

        def lifted(block, after, s):
            return jnp.where(row < 8 - s, pltpu.roll(block, 8 - s, 0), pltpu.roll(after, 8 - s, 0))

        def down(s, carry):
            before, sums = carry
            rows = pl.ds(pl.multiple_of(s * HALO, HALO), HALO)
            v = uc_ref[rows, :].astype(F32) * uh_ref[rows, :].astype(F32)
            dyv = dy_ref[rows, :].astype(F32)
            dcv = dyv * ub_ref[rows, :].astype(F32)
            cvs = []
            sums = list(sums)
            for block, above, dcb in ((v[0:8], before, dcv[0:8]), (v[8:HALO], v[0:8], dcv[8:HALO])):
                moved = [shifted(block, above, SHORT_K - 1 - k) for k in range(SHORT_K - 1)] + [block]
                cvs.append(sum(taps[k] * moved[k] for k in range(SHORT_K)))
                sums = [sums[k] + dcb * moved[k] for k in range(SHORT_K)]
            dub_ref[rows, :] = (dyv * jnp.concatenate(cvs, axis=0)).astype(BF16)
            return v[8:HALO], tuple(sums)

        zero = jnp.zeros((8, tc), F32)
        _, sums = lax.fori_loop(0, nstrips, down, (jnp.where(i > 0, vp[8:HALO], 0.0), (zero,) * SHORT_K))

        def up(n, after):
            rows = pl.ds(pl.multiple_of((nstrips - 1 - n) * HALO, HALO), HALO)
            uc = uc_ref[rows, :].astype(F32)
            uh = uh_ref[rows, :].astype(F32)
            dcv = dy_ref[rows, :].astype(F32) * ub_ref[rows, :].astype(F32)
            dvs = []
            for block, below in ((dcv[0:8], dcv[8:HALO]), (dcv[8:HALO], after)):
                dvs.append(taps[SHORT_K - 1] * block
                           + sum(taps[k] * lifted(block, below, SHORT_K - 1 - k) for k in range(SHORT_K - 1)))
            dv = jnp.concatenate(dvs, axis=0)
            duc_ref[rows, :] = (dv * uh).astype(BF16)
            duh_ref[rows, :] = (dv * uc).astype(BF16)
            return dcv[0:8]

        lax.fori_loop(0, nstrips, up, jnp.where(i < ni - 1, dcvn[0:8], 0.0))

        @pl.when(i == 0)
        def _():
            dkw_ref[...] = jnp.zeros_like(dkw_ref)

        for k in range(SHORT_K):
            dkw_ref[pl.ds(k, 1), :] += jnp.sum(sums[k], axis=0, keepdims=True)

    prev = lambda off: (lambda j, i: (jnp.maximum(i * hb - 1, 0), off + j))
    nxt = lambda off: (lambda j, i: (jnp.minimum((i + 1) * hb, last_hb), off + j))
    cur = lambda off: (lambda j, i: (i, off + j))
    return pl.pallas_call(
        body, name=name, grid=(nj, ni),
        in_specs=[pl.BlockSpec((tm, tc), cur(0)), pl.BlockSpec((tm, tc), cur(nj)),
                  pl.BlockSpec((tm, tc), cur(2 * nj)), pl.BlockSpec((tm, tc), cur(0)),
                  pl.BlockSpec((HALO, tc), prev(nj)), pl.BlockSpec((HALO, tc), prev(2 * nj)),
                  pl.BlockSpec((HALO, tc), nxt(0)), pl.BlockSpec((HALO, tc), nxt(0)),
                  pl.BlockSpec((SHORT_K, tc), lambda j, i: (0, j))],
        out_specs=[pl.BlockSpec((tm, tc), cur(0)), pl.BlockSpec((tm, tc), cur(0)),
                   pl.BlockSpec((tm, tc), cur(0)), pl.BlockSpec((SHORT_K, tc), lambda j, i: (0, j))],
        out_shape=[jax.ShapeDtypeStruct((t, cw), BF16)] * 3 + [jax.ShapeDtypeStruct((SHORT_K, cw), F32)],
        compiler_params=_params(("parallel", "arbitrary")),
    )(proj, proj, proj, dy, proj, proj, proj, dy, kw)


def _head_column(mat, lane, h):
    return jnp.sum(jnp.where(lane == h, mat, 0.0), axis=-1, keepdims=True)


def _ssd_common(dt_raw_ref, dtb_ref, aneg_ref, cum_s, cumt_s, chunk):
    dt = _softplus(dt_raw_ref[...] + dtb_ref[...])
    al = dt * aneg_ref[...]
    ri = lax.broadcasted_iota(jnp.int32, (chunk, chunk), 0)
    ci = lax.broadcasted_iota(jnp.int32, (chunk, chunk), 1)
    cum = _dot_exact((ri >= ci).astype(F32), al)
    cum_s[...] = cum
    cumt_s[...] = cum.T
    return dt, cum, ri >= ci


EDGE = 16


def _shift_matrices(shift_s, chunk, kk, up):
    ri = lax.broadcasted_iota(jnp.int32, (chunk, chunk), 0)
    ci = lax.broadcasted_iota(jnp.int32, (chunk, chunk), 1)
    for k in range(kk - 1):
        s = kk - 1 - k
        shift_s[k] = ((ci - ri if up else ri - ci) == s).astype(BF16)


def _causal_conv(cur, head, kw_ref, b_ref, shift_s, kk):
    acc = b_ref[...] + kw_ref[pl.ds(kk - 1, 1), :] * cur.astype(F32)
    top = b_ref[...] + kw_ref[pl.ds(kk - 1, 1), :] * head[pl.ds(8, EDGE), :]
    for k in range(kk - 1):
        acc = acc + kw_ref[pl.ds(k, 1), :] * _dot(shift_s[k], cur)
        top = top + kw_ref[pl.ds(k, 1), :] * head[pl.ds(8 - (kk - 1) + k, EDGE), :]
    return acc, top


def ssd_fwd(proj, dt_raw, y_mix, kw_xs, kw_bc, b_xs, b_bc, dtb, aneg, dskip, normw, cw, si, name, rider=None):
    t = proj.shape[0]
    ch = min(SCAN_CHUNK, t)
    nc = t // ch
    npair = si // LANES
    ppg = npair // GROUPS
    gn = GROUPS * STATE
    gw = si // GROUPS
    assert cw == si and (3 * cw + 2 * si) % (2 * gn) == 0
    zblk = 3 * cw // si
    xsblk = zblk + 1
    bcblk = (3 * cw + 2 * si) // (2 * gn)

    def body(z_ref, xs_ref, bc_ref, dtr_ref, ymix_ref, kwx_ref, kwb_ref, bx_ref, bb_ref, dtb_ref, aneg_ref, dsk_ref,
             nw_ref, yb_ref, ys_ref, hs_ref, xcx_ref, xcb_ref,
             headx, headb, shift_s, xs_s, bc_s, h_s, gated_s, s_s, cum_s, cumt_s):
        del ymix_ref
        c = pl.program_id(0)

        @pl.when(c == 0)
        def _():
            h_s[...] = jnp.zeros_like(h_s)
            headx[0:8, :] = jnp.zeros((8, si), F32)
            headb[0:8, :] = jnp.zeros((8, 2 * gn), F32)
            _shift_matrices(shift_s, ch, SSD_K, up=False)

        for raw_ref, head, kw_ref, b_ref, pre_ref, act_s in ((xs_ref, headx, kwx_ref, bx_ref, xcx_ref, xs_s),
                                                           (bc_ref, headb, kwb_ref, bb_ref, xcb_ref, bc_s)):
            head[8:8 + EDGE, :] = raw_ref[0:EDGE, :].astype(F32)
            pre, top = _causal_conv(raw_ref[...], head, kw_ref, b_ref, shift_s, SSD_K)
            head[0:8, :] = raw_ref[ch - EDGE:ch, :].astype(F32)[EDGE - 8:EDGE]
            pre_ref[...] = pre.astype(BF16)
            pre_ref[0:EDGE, :] = top.astype(BF16)
            act_s[...] = (pre * _sigmoid(pre)).astype(act_s.dtype)
            act_s[0:EDGE, :] = (top * _sigmoid(top)).astype(act_s.dtype)

        dt, cum, tril = _ssd_common(dtr_ref, dtb_ref, aneg_ref, cum_s, cumt_s, ch)
        lane = lax.broadcasted_iota(jnp.int32, (ch, LANES), 1)
        lane1 = lax.broadcasted_iota(jnp.int32, (1, LANES), 1)
        low = lane < HEAD_DIM
        clast = cum_s[pl.ds(ch - 1, 1), :]

        for p in range(npair):
            g = p // ppg
            col = slice(p * LANES, (p + 1) * LANES)
            bg = bc_s[:, g * STATE:(g + 1) * STATE]
            cg = bc_s[:, gn + g * STATE:gn + (g + 1) * STATE]
            if p % ppg == 0:
                s_s[...] = _dot_nt(cg, bg)
            heads = (PAIR * p, PAIR * p + 1)
            ccol = [_head_column(cum, lane, h) for h in heads]
            dcol = [_head_column(dt, lane, h) for h in heads]
            cl = [jnp.sum(jnp.where(lane1 == h, clast, 0.0), axis=-1, keepdims=True) for h in heads]
            cum_px = jnp.where(low, ccol[0], ccol[1])
            dt_px = jnp.where(low, dcol[0], dcol[1])
            cl_px = jnp.where(lane1 < HEAD_DIM, cl[0], cl[1])
            xs_p = xs_s[:, col]
            xdt = xs_p * dt_px
            y = dsk_ref[:, col] * xs_p
            for hi, h in enumerate(heads):
                dec = jnp.exp(jnp.where(tril, ccol[hi] - cumt_s[pl.ds(h, 1), :], -jnp.inf))
                wm = (s_s[...] * dec).astype(BF16)
                xm = jnp.where(low if hi == 0 else jnp.logical_not(low), xdt, 0.0).astype(BF16)
                y = y + _dot(wm, xm)
            hp = h_s[p]
            hs_ref[0, p] = hp
            y = y + _dot(cg, hp.astype(BF16)) * jnp.exp(cum_px)
            st = _dot_tn(bg, (xdt * jnp.exp(cl_px - cum_px)).astype(BF16))
            h_s[p] = jnp.exp(cl_px) * hp + st
            ys_ref[:, col] = y.astype(BF16)
            zp = z_ref[:, col].astype(F32)
            gated_s[:, col] = y * zp * _sigmoid(zp)

        for g in range(GROUPS):
            col = slice(g * gw, (g + 1) * gw)
            gg = gated_s[:, col]
            r = lax.rsqrt(jnp.mean(gg * gg, axis=-1, keepdims=True) + EPS)
            yb_ref[:, col] = (gg * r * nw_ref[:, col]).astype(BF16)

    full = lambda shape: pl.BlockSpec(shape, lambda c: tuple(0 for _ in shape))
    return _ride(
        body, (nc,),
        [pl.BlockSpec((ch, si), lambda c: (c, zblk)),
         pl.BlockSpec((ch, si), lambda c: (c, xsblk)),
         pl.BlockSpec((ch, 2 * gn), lambda c: (c, bcblk)),
         pl.BlockSpec((ch, LANES), lambda c: (c, 0)),
         pl.BlockSpec(memory_space=pl.ANY),
         full((SSD_K, si)), full((SSD_K, 2 * gn)), full((1, si)), full((1, 2 * gn)),
         full((1, LANES)), full((1, LANES)), full((1, si)), full((1, si))],
        [pl.BlockSpec((ch, si), lambda c: (c, cw // si)),
         pl.BlockSpec((ch, si), lambda c: (c, 0)),
         pl.BlockSpec((1, npair, STATE, LANES), lambda c: (c, 0, 0, 0)),
         pl.BlockSpec((ch, si), lambda c: (c, 0)), pl.BlockSpec((ch, 2 * gn), lambda c: (c, 0))],
        [jax.ShapeDtypeStruct(y_mix.shape, BF16), jax.ShapeDtypeStruct((t, si), BF16),
         jax.ShapeDtypeStruct((nc, npair, STATE, LANES), F32),
         jax.ShapeDtypeStruct((t, si), BF16), jax.ShapeDtypeStruct((t, 2 * gn), BF16)],
        [pltpu.VMEM((8 + EDGE, si), F32), pltpu.VMEM((8 + EDGE, 2 * gn), F32),
         pltpu.VMEM((SSD_K - 1, ch, ch), BF16),
         pltpu.VMEM((ch, si), F32), pltpu.VMEM((ch, 2 * gn), BF16),
         pltpu.VMEM((npair, STATE, LANES), F32), pltpu.VMEM((ch, si), F32),
         pltpu.VMEM((ch, ch), F32), pltpu.VMEM((ch, LANES), F32), pltpu.VMEM((LANES, ch), F32)],
        [proj, proj, proj, dt_raw, y_mix, kw_xs, kw_bc, b_xs, b_bc, dtb, aneg, dskip, normw], {4: 0}, rider,
        ("arbitrary",), name)


def ssd_bwd(proj, dt_raw, ys, hsave, pre_xs, pre_bc, dy, kw_xs, kw_bc, dtb, aneg, dskip, normw, cw, si, name,
            rider=None):
    t = proj.shape[0]
    ch = min(SCAN_CHUNK, t)
    nc = t // ch
    npair = si // LANES
    ppg = npair // GROUPS
    gn = GROUPS * STATE
    gw = si // GROUPS
    zblk = 3 * cw // si
    xsblk = zblk + 1
    bcblk = (3 * cw + 2 * si) // (2 * gn)

    def body(z_ref, xs_ref, bc_ref, xcx_ref, xcb_ref, dtr_ref, ys_ref, hs_ref, dyb_ref,
             kwx_ref, kwb_ref, dtb_ref, aneg_ref, dsk_ref, nw_ref,
             dz_ref, dxs_ref, dbc_ref, ddt_ref, dkwx_ref, dkwb_ref, dbx_ref, dbb_ref, ddtb_ref, da_ref, ddsk_ref,
             dnw_ref,
             tailx, tailb, shift_s, xs_s, bc_s, dsx_s, dsb_s, dy_s, dxs_s, dbc_s, dh_s, s_s, ds_s,
             cum_s, cumt_s, dccol_s, dcrow_s, ddtcol_s, dcl_s):
        i = pl.program_id(0)

        @pl.when(i == 0)
        def _():
            dh_s[...] = jnp.zeros_like(dh_s)
            tailx[EDGE:EDGE + 8, :] = jnp.zeros((8, si), F32)
            tailb[EDGE:EDGE + 8, :] = jnp.zeros((8, 2 * gn), F32)
            _shift_matrices(shift_s, ch, SSD_K, up=True)
            for r in (dkwx_ref, dkwb_ref, dbx_ref, dbb_ref, ddtb_ref, da_ref, ddsk_ref, dnw_ref):
                r[...] = jnp.zeros_like(r)

        xc = xcx_ref[...].astype(F32)
        sg = _sigmoid(xc)
        xs_s[...] = xc * sg
        dsx_s[...] = sg * (1.0 + xc * (1.0 - sg))
        bcc = xcb_ref[...].astype(F32)
        sgb = _sigmoid(bcc)
        bc_s[...] = (bcc * sgb).astype(BF16)
        dsb_s[...] = sgb * (1.0 + bcc * (1.0 - sgb))

        dt, cum, tril = _ssd_common(dtr_ref, dtb_ref, aneg_ref, cum_s, cumt_s, ch)
        lane = lax.broadcasted_iota(jnp.int32, (ch, LANES), 1)
        lane1 = lax.broadcasted_iota(jnp.int32, (1, LANES), 1)
        low = lane < HEAD_DIM
        low1 = lane1 < HEAD_DIM
        clast = cum_s[pl.ds(ch - 1, 1), :]

        for g in range(GROUPS):
            col = slice(g * gw, (g + 1) * gw)
            ysf = ys_ref[:, col].astype(F32)
            zf = z_ref[:, col].astype(F32)
            sz = _sigmoid(zf)
            silz = zf * sz
            gg = ysf * silz
            r = lax.rsqrt(jnp.mean(gg * gg, axis=-1, keepdims=True) + EPS)
            nrm = gg * r
            dyb = dyb_ref[:, col].astype(F32)
            dnw_ref[:, col] += jnp.sum(dyb * nrm, axis=0, keepdims=True)
            dn = dyb * nw_ref[:, col]
            dgg = r * (dn - nrm * jnp.mean(dn * nrm, axis=-1, keepdims=True))
            dy_s[:, col] = dgg * silz
            dz_ref[:, col] = (dgg * ysf * (sz * (1.0 + zf * (1.0 - sz)))).astype(BF16)

        dccol_s[...] = jnp.zeros_like(dccol_s)
        dcrow_s[...] = jnp.zeros_like(dcrow_s)
        ddtcol_s[...] = jnp.zeros_like(ddtcol_s)
        dcl_s[...] = jnp.zeros_like(dcl_s)
        dbc_s[...] = jnp.zeros_like(dbc_s)

        for p in range(npair):
            g = p // ppg
            col = slice(p * LANES, (p + 1) * LANES)
            bcol = slice(g * STATE, (g + 1) * STATE)
            ccolg = slice(gn + g * STATE, gn + (g + 1) * STATE)
            bg = bc_s[:, bcol]
            cg = bc_s[:, ccolg]
            if p % ppg == 0:
                s_s[...] = _dot_nt(cg, bg)
                ds_s[...] = jnp.zeros_like(ds_s)
            heads = (PAIR * p, PAIR * p + 1)
            masks = (low, jnp.logical_not(low))
            masks1 = (low1, jnp.logical_not(low1))
            ccol = [_head_column(cum, lane, h) for h in heads]
            dcol = [_head_column(dt, lane, h) for h in heads]
            cl = [jnp.sum(jnp.where(lane1 == h, clast, 0.0), axis=-1, keepdims=True) for h in heads]
            cum_px = jnp.where(low, ccol[0], ccol[1])
            dt_px = jnp.where(low, dcol[0], dcol[1])
            cl_px = jnp.where(low1, cl[0], cl[1])
            e_px = jnp.exp(cum_px)
            dec_end = jnp.exp(cl_px - cum_px)
            gdec = jnp.exp(cl_px)
            xs_p = xs_s[:, col]
            xdt = xs_p * dt_px
            dyp = dy_s[:, col]
            hc = hs_ref[0, p]
            hcb = hc.astype(BF16)
            dhn = dh_s[p]
            dhnb = dhn.astype(BF16)

            ddsk_ref[:, col] += jnp.sum(dyp * xs_p, axis=0, keepdims=True)
            dxs_acc = dsk_ref[:, col] * dyp
            dye = dyp * e_px
            dyeb = dye.astype(BF16)
            dbc_s[:, ccolg] += _dot_nt(dyeb, hcb)
            dcum_lane = dye * _dot(cg, hcb)
            dh_from_y = _dot_tn(cg, dyeb)
            xd = xdt * dec_end
            dxd = _dot(bg, dhnb)
            dbc_s[:, bcol] += _dot_nt(xd.astype(BF16), dhnb)
            dxdt = dxd * dec_end
            t1 = dxd * xd
            dcum_lane = dcum_lane - t1
            dcl_lane = jnp.sum(t1, axis=0, keepdims=True) + jnp.sum(dhn * hc, axis=0, keepdims=True) * gdec
            dh_s[p] = gdec * dhn + dh_from_y
            xdtb = xdt.astype(BF16)
            for hi, h in enumerate(heads):
                dym = jnp.where(masks[hi], dyp, 0.0).astype(BF16)
                dw = _dot_nt(dym, xdtb)
                dec = jnp.exp(jnp.where(tril, ccol[hi] - cumt_s[pl.ds(h, 1), :], -jnp.inf))
                wm = s_s[...] * dec
                dxdt = dxdt + _dot_tn(wm.astype(BF16), dym)
                ds_s[...] += dw * dec
                gm = dw * wm
                rowsum = jnp.sum(gm, axis=-1, keepdims=True)
                lanesum = jnp.sum(jnp.where(masks[hi], dcum_lane, 0.0), axis=-1, keepdims=True)
                dccol_s[...] += jnp.where(lane == h, rowsum + lanesum, 0.0)
                dcrow_s[pl.ds(h, 1), :] = jnp.sum(gm, axis=0, keepdims=True)
                dcl_h = jnp.sum(jnp.where(masks1[hi], dcl_lane, 0.0), axis=-1, keepdims=True)
                dcl_s[...] += jnp.where(lane1 == h, dcl_h, 0.0)
            ddt_lane = dxdt * xs_p
            for hi, h in enumerate(heads):
                s = jnp.sum(jnp.where(masks[hi], ddt_lane, 0.0), axis=-1, keepdims=True)
                ddtcol_s[...] += jnp.where(lane == h, s, 0.0)
            dxs_s[:, col] = dxs_acc + dxdt * dt_px
            if p % ppg == ppg - 1:
                dsb = ds_s[...].astype(BF16)
                dbc_s[:, ccolg] += _dot(dsb, bg)
                dbc_s[:, bcol] += _dot_tn(dsb, cg)

        rowi = lax.broadcasted_iota(jnp.int32, (ch, LANES), 0)
        dcum = dccol_s[...] - dcrow_s[...].T + jnp.where(rowi == ch - 1, dcl_s[...], 0.0)
        ri = lax.broadcasted_iota(jnp.int32, (ch, ch), 0)
        ci = lax.broadcasted_iota(jnp.int32, (ch, ch), 1)
        dal = _dot_exact((ri <= ci).astype(F32), dcum)
        ddt = dal * aneg_ref[...] + ddtcol_s[...]
        da_ref[...] += jnp.sum(dal * dt, axis=0, keepdims=True)
        ddtr = ddt * _sigmoid(dtr_ref[...] + dtb_ref[...])
        ddt_ref[...] = ddtr
        ddtb_ref[...] += jnp.sum(ddtr, axis=0, keepdims=True)

        for (dpost, dsl, tail, raw_ref, kw_ref, dkw_ref, db_ref, out_ref) in (
                (dxs_s, dsx_s, tailx, xs_ref, kwx_ref, dkwx_ref, dbx_ref, dxs_ref),
                (dbc_s, dsb_s, tailb, bc_ref, kwb_ref, dkwb_ref, dbb_ref, dbc_ref)):
            dxc = dpost[...] * dsl[...]
            dxcb = dxc.astype(BF16)
            raw = raw_ref[...].astype(F32)
            raw_end = raw_ref[ch - EDGE:ch, :].astype(F32)
            tail[0:EDGE, :] = dxcb[ch - EDGE:ch].astype(F32)
            db_ref[...] += jnp.sum(dxc, axis=0, keepdims=True)
            draw = kw_ref[pl.ds(SSD_K - 1, 1), :] * dxc
            dkw_ref[pl.ds(SSD_K - 1, 1), :] += jnp.sum(dxc * raw, axis=0, keepdims=True)
            fix = jnp.zeros((EDGE, dxc.shape[1]), F32)
            for k in range(SSD_K - 1):
                moved = _dot(shift_s[k], dxcb)
                miss = tail[pl.ds(SSD_K - 1 - k, EDGE), :] - moved[ch - EDGE:ch]
                draw = draw + kw_ref[pl.ds(k, 1), :] * moved
                fix = fix + kw_ref[pl.ds(k, 1), :] * miss
                dkw_ref[pl.ds(k, 1), :] += (jnp.sum(moved * raw, axis=0, keepdims=True)
                                            + jnp.sum(miss * raw_end, axis=0, keepdims=True))
            out_ref[...] = draw.astype(BF16)
            out_ref[ch - EDGE:ch, :] = (draw[ch - EDGE:ch] + fix).astype(BF16)
            tail[EDGE:EDGE + 8, :] = dxcb[0:EDGE].astype(F32)[0:8]

    full = lambda shape: pl.BlockSpec(shape, lambda i: tuple(0 for _ in shape))
    rev = lambda blk: (lambda i: (nc - 1 - i, blk))
    small_in = [(SSD_K, si), (SSD_K, 2 * gn), (1, LANES), (1, LANES), (1, si), (1, si)]
    small = [(SSD_K, si), (SSD_K, 2 * gn), (1, si), (1, 2 * gn), (1, LANES), (1, LANES), (1, si), (1, si)]
    return _ride(
        body, (nc,),
        [pl.BlockSpec((ch, si), rev(zblk)), pl.BlockSpec((ch, si), rev(xsblk)),
         pl.BlockSpec((ch, 2 * gn), rev(bcblk)),
         pl.BlockSpec((ch, si), rev(0)), pl.BlockSpec((ch, 2 * gn), rev(0)),
         pl.BlockSpec((ch, LANES), rev(0)), pl.BlockSpec((ch, si), rev(0)),
         pl.BlockSpec((1, npair, STATE, LANES), lambda i: (nc - 1 - i, 0, 0, 0)),
         pl.BlockSpec((ch, si), rev(cw // si))] + [full(s) for s in small_in],
        [pl.BlockSpec((ch, si), rev(0)), pl.BlockSpec((ch, si), rev(0)),
         pl.BlockSpec((ch, 2 * gn), rev(0)), pl.BlockSpec((ch, LANES), rev(0))] + [full(s) for s in small],
        [jax.ShapeDtypeStruct((t, si), BF16), jax.ShapeDtypeStruct((t, si), BF16),
         jax.ShapeDtypeStruct((t, 2 * gn), BF16), jax.ShapeDtypeStruct((t, LANES), F32)]
        + [jax.ShapeDtypeStruct(s, F32) for s in small],
        [pltpu.VMEM((EDGE + 8, si), F32), pltpu.VMEM((EDGE + 8, 2 * gn), F32),
         pltpu.VMEM((SSD_K - 1, ch, ch), BF16),
         pltpu.VMEM((ch, si), F32), pltpu.VMEM((ch, 2 * gn), BF16),
         pltpu.VMEM((ch, si), F32), pltpu.VMEM((ch, 2 * gn), F32),
         pltpu.VMEM((ch, si), F32), pltpu.VMEM((ch, si), F32), pltpu.VMEM((ch, 2 * gn), F32),
         pltpu.VMEM((npair, STATE, LANES), F32),
         pltpu.VMEM((ch, ch), F32), pltpu.VMEM((ch, ch), F32),
         pltpu.VMEM((ch, LANES), F32), pltpu.VMEM((LANES, ch), F32),
         pltpu.VMEM((ch, LANES), F32), pltpu.VMEM((LANES, ch), F32),
         pltpu.VMEM((ch, LANES), F32), pltpu.VMEM((1, LANES), F32)],
        [proj, proj, proj, pre_xs, pre_bc, dt_raw, ys, hsave, dy, kw_xs, kw_bc, dtb, aneg, dskip, normw],
        {}, rider, ("arbitrary",), name)


def final_loss(x, nw, tgt, name):
    t, d = x.shape
    tm = _tile(t, 512, 8)

    def body(x_ref, nw_ref, t_ref, dx_ref, dnw_ref, ls_ref):
        xf = x_ref[...]
        r = lax.rsqrt(jnp.mean(xf * xf, axis=-1, keepdims=True) + EPS)
        nx = xf * r
        e = nx * nw_ref[...] - t_ref[...]
        dyv = e * (1.0 / d)
        dn = dyv * nw_ref[...]
        dx_ref[...] = r * (dn - nx * jnp.mean(dn * nx, axis=-1, keepdims=True))

        @pl.when(pl.program_id(0) == 0)
        def _():
            dnw_ref[...] = jnp.zeros_like(dnw_ref)
            ls_ref[...] = jnp.zeros_like(ls_ref)

        dnw_ref[...] += jnp.sum(dyv * nx, axis=0, keepdims=True)
        ls_ref[...] += jnp.sum(e * e, axis=0, keepdims=True) * (0.5 / d)

    return pl.pallas_call(
        body, name=name, grid=(t // tm,),
        in_specs=[pl.BlockSpec((tm, d), lambda i: (i, 0)), pl.BlockSpec((1, d), lambda i: (0, 0)),
                  pl.BlockSpec((tm, d), lambda i: (i, 0))],
        out_specs=[pl.BlockSpec((tm, d), lambda i: (i, 0)), pl.BlockSpec((1, d), lambda i: (0, 0)),
                   pl.BlockSpec((1, d), lambda i: (0, 0))],
        out_shape=[jax.ShapeDtypeStruct((t, d), F32), jax.ShapeDtypeStruct((1, d), F32),
                   jax.ShapeDtypeStruct((1, d), F32)],
        compiler_params=_params(("arbitrary",)),
    )(x, nw, tgt)


def _rows3(a):
    if a.ndim == 1:
        return a.reshape(1, 1, a.shape[0])
    if a.ndim == 2:
        return a.reshape(1, *a.shape)
    return a.reshape(-1, a.shape[-2], a.shape[-1])


def adamw(w, g, m, v, name):
    shape = w.shape
    views = [_rows3(a) for a in (w, g, m, v)]
    b, r, c = views[0].shape
    tr = _tile(r, 256, 16) if r % 16 == 0 else r

    def body(w_ref, g_ref, m_ref, v_ref, d_ref, nm_ref, nv_ref):
        g = g_ref[...]
        m = ADAM_B1 * m_ref[...] + (1.0 - ADAM_B1) * g
        v = ADAM_B2 * v_ref[...] + (1.0 - ADAM_B2) * (g * g)
        m_hat = m / (1.0 - ADAM_B1 ** ADAM_STEP)
        v_hat = v / (1.0 - ADAM_B2 ** ADAM_STEP)
        d_ref[...] = -ADAM_LR * (m_hat / (jnp.sqrt(v_hat) + ADAM_EPS) + ADAM_WD * w_ref[...])
        nm_ref[...] = m
        nv_ref[...] = v

    spec = pl.BlockSpec((1, tr, c), lambda i, j: (i, j, 0))
    outs = pl.pallas_call(
        body, name=name, grid=(b, r // tr), in_specs=[spec] * 4, out_specs=[spec] * 3,
        out_shape=[jax.ShapeDtypeStruct((b, r, c), F32)] * 3,
        compiler_params=_params(("parallel", "parallel")),
    )(*views)
    return [o.reshape(shape) for o in outs]


def adamw_halves(w, g_mine, g_theirs, m, v, name):
    nl, r, c = w.shape
    r2 = r // 2
    tr = _tile(r2, 256, 16)
    nb = r2 // tr

    def body(w_ref, gm_ref, gt_ref, m_ref, v_ref, g_ref, d_ref, nm_ref, nv_ref):
        mine = (pl.program_id(1) // nb) == lax.axis_index("c")
        g = jnp.where(mine, gm_ref[...], gt_ref[...])
        m = ADAM_B1 * m_ref[...] + (1.0 - ADAM_B1) * g
        v = ADAM_B2 * v_ref[...] + (1.0 - ADAM_B2) * (g * g)
        m_hat = m / (1.0 - ADAM_B1 ** ADAM_STEP)
        v_hat = v / (1.0 - ADAM_B2 ** ADAM_STEP)
        g_ref[...] = g
        d_ref[...] = -ADAM_LR * (m_hat / (jnp.sqrt(v_hat) + ADAM_EPS) + ADAM_WD * w_ref[...])
        nm_ref[...] = m
        nv_ref[...] = v

    whole = pl.BlockSpec((1, tr, c), lambda l, i: (l, i, 0))
    half = pl.BlockSpec((1, tr, c), lambda l, i: (l, i % nb, 0))
    return pl.pallas_call(
        body, name=name, grid=(nl, 2 * nb), in_specs=[whole, half, half, whole, whole], out_specs=[whole] * 4,
        out_shape=[jax.ShapeDtypeStruct((nl, r, c), F32)] * 4,
        compiler_params=_params(("parallel", "parallel")),
    )(w, g_mine, g_theirs, m, v)


def _coords():
    return lax.axis_index("x"), lax.axis_index("y"), lax.axis_index("c")


def _ici_peers(x, y):
    chips = [(1 - x, y), (x, 1 - y), (1 - x, 1 - y)]
    return chips, [2 * cx + cy for cx, cy in chips]


def _place(ref, how, chip, layers, per):
    if how == "lead":
        return ref.at[chip, layers]
    start = pl.multiple_of(chip * per, per)
    if how == "rows":
        return ref.at[layers, pl.ds(start, per), :]
    return ref.at[layers, :, pl.ds(start, per)]


def gather_weights(shards, hows, name):
    na = len(shards)
    out_shape = []
    for s, how in zip(shards, hows):
        assert s.shape[0] % 2 == 0
        if how == "lead":
            shp = (N_CHIPS, *s.shape)
        elif how == "rows":
            shp = (s.shape[0], N_CHIPS * s.shape[1], s.shape[2])
        else:
            shp = (s.shape[0], s.shape[1], N_CHIPS * s.shape[2])
        out_shape.append(jax.ShapeDtypeStruct(shp, s.dtype))

    def body(*refs):
        ins = refs[:na]
        outs = refs[na:2 * na]
        send_sems, recv_sems = refs[2 * na:]
        x, y, c = _coords()
        me = 2 * x + y
        chips, chip_ids = _ici_peers(x, y)
        sibling = (x, y, 1 - c)

        def dst(a, chip, layers):
            per = {"lead": 0, "rows": ins[a].shape[1], "cols": ins[a].shape[-1]}[hows[a]]
            return _place(outs[a], hows[a], chip, layers, per)

        def copy(a, k, src, dst_ref, to):
            return pltpu.make_async_remote_copy(
                src_ref=src, dst_ref=dst_ref, send_sem=send_sems.at[7 * a + k], recv_sem=recv_sems.at[7 * a + k],
                device_id=to, device_id_type=MESH)

        started = []
        halves = []
        for a in range(na):
            nl = ins[a].shape[0]
            hl = nl // 2
            mine = pl.ds(c * hl, hl)
            theirs = pl.ds((1 - c) * hl, hl)
            halves.append((mine, theirs))
            for k in range(3):
                cp = copy(a, k, ins[a].at[mine], dst(a, me, mine), (*chips[k], c))
                cp.start()
                started.append(cp)
            own = copy(a, 6, ins[a], dst(a, me, pl.ds(0, nl)), sibling)
            own.start()
            started.append(own)
        for a in range(na):
            mine, _ = halves[a]
            for k in range(3):
                landed = dst(a, chip_ids[k], mine)
                copy(a, k, landed, landed, (*chips[k], c)).wait_recv()
                fw = copy(a, 3 + k, landed, landed, sibling)
                fw.start()
                started.append(fw)
        for a in range(na):
            _, theirs = halves[a]
            for k in range(3):
                got = dst(a, chip_ids[k], theirs)
                copy(a, 3 + k, got, got, sibling).wait_recv()
            whole = dst(a, me, pl.ds(0, ins[a].shape[0]))
            copy(a, 6, whole, whole, sibling).wait_recv()
        for cp in started:
            cp.wait_send()

    return pl.pallas_call(
        body, name=name, in_specs=_any_specs(na), out_specs=_any_specs(na), out_shape=out_shape,
        scratch_shapes=[pltpu.SemaphoreType.DMA((7 * na,)), pltpu.SemaphoreType.DMA((7 * na,))],
        compiler_params=pltpu.CompilerParams(has_side_effects=True),
    )(*shards)


def _remote(src, dst, send_sems, recv_sems, k, to):
    return pltpu.make_async_remote_copy(src_ref=src, dst_ref=dst, send_sem=send_sems.at[k], recv_sem=recv_sems.at[k],
                                        device_id=to, device_id_type=MESH)


LAYER_HOW = ("lead", "rows", "cols", "rows")


def _layer_place(ref, how, chip, shard_shape, start, size):
    r, c = shard_shape
    if how == "lead":
        return ref.at[chip, :, pl.ds(start, size), :]
    if how == "rows":
        return ref.at[:, pl.ds(pl.multiple_of(chip * r + start, HALO), size), :]
    return ref.at[:, pl.ds(start, size), pl.ds(pl.multiple_of(chip * c, LANES), c)]


def weight_rider_ici(shards, hows, layer):
    shapes = [tuple(s.shape[1:]) for s in shards]
    out_shapes = []
    for (r, c), how, s in zip(shapes, hows, shards):
        shp = {"lead": (N_CHIPS, 1, r, c), "rows": (1, N_CHIPS * r, c), "cols": (1, r, N_CHIPS * c)}[how]
        out_shapes.append(jax.ShapeDtypeStruct(shp, s.dtype))

    def copies(ins, outs, send_sems, recv_sems):
        x, y, c = _coords()
        me = 2 * x + y
        chips, chip_ids = _ici_peers(x, y)
        sibling = (x, y, 1 - c)
        pairs = []
        for a, (shape, how) in enumerate(zip(shapes, hows)):
            half = shape[0] // 2
            mine = pl.multiple_of(c * half, HALO)
            src = ins[a].at[pl.ds(layer, 1)]
            for k in range(3):
                to = (*chips[k], c)
                land = _layer_place(outs[a], how, chip_ids[k], shape, mine, half)
                pairs.append((_remote(src.at[:, pl.ds(mine, half), :], _layer_place(outs[a], how, me, shape, mine, half),
                                      send_sems, recv_sems, 4 * a + k, to),
                              _remote(land, land, send_sems, recv_sems, 4 * a + k, to)))
            whole = _layer_place(outs[a], how, me, shape, 0, shape[0])
            pairs.append((_remote(src, whole, send_sems, recv_sems, 4 * a + 3, sibling),
                          _remote(whole, whole, send_sems, recv_sems, 4 * a + 3, sibling)))
        return pairs

    return Rider(list(shards), out_shapes, {}, 4 * len(shards), copies)


def weight_rider_d2d(bufs, shapes, hows):
    def copies(ins, outs, send_sems, recv_sems):
        x, y, c = _coords()
        _, chip_ids = _ici_peers(x, y)
        sibling = (x, y, 1 - c)
        pairs = []
        for a, (shape, how) in enumerate(zip(shapes, hows)):
            half = shape[0] // 2
            mine = pl.multiple_of(c * half, HALO)
            theirs = pl.multiple_of((1 - c) * half, HALO)
            for k in range(3):
                land = _layer_place(outs[a], how, chip_ids[k], shape, theirs, half)
                pairs.append((_remote(_layer_place(ins[a], how, chip_ids[k], shape, mine, half),
                                      _layer_place(outs[a], how, chip_ids[k], shape, mine, half),
                                      send_sems, recv_sems, 3 * a + k, sibling),
                              _remote(land, land, send_sems, recv_sems, 3 * a + k, sibling)))
        return pairs

    return Rider(list(bufs), [jax.ShapeDtypeStruct(b.shape, b.dtype) for b in bufs],
                 {a: a for a in range(len(bufs))}, 3 * len(bufs), copies)


def grads_rider_sibling(arrs):
    def copies(ins, outs, send_sems, recv_sems):
        x, y, c = _coords()
        sibling = (x, y, 1 - c)
        pairs = []
        for a in range(len(arrs)):
            r2 = ins[a].shape[1] // 2
            src = ins[a].at[:, pl.ds(pl.multiple_of((1 - c) * r2, 8), r2), :]
            pairs.append((_remote(src, outs[a], send_sems, recv_sems, a, sibling),
                          _remote(outs[a], outs[a], send_sems, recv_sems, a, sibling)))
        return pairs

    return Rider(list(arrs), [jax.ShapeDtypeStruct((a.shape[0], a.shape[1] // 2, a.shape[2]), a.dtype) for a in arrs],
                 {}, len(arrs), copies)


def chip_sum(g, recv, name):
    nch, r, c = g.shape
    r2 = r // 2
    tr = _tile(r2, 256, 16)
    nb = r2 // tr

    def body(g0_ref, g1_ref, r_ref, o32_ref, o16_ref):
        s = jnp.where(lax.axis_index("c") == 0, g0_ref[...], g1_ref[...]) + r_ref[...]
        o32_ref[...] = s
        o16_ref[...] = s.astype(BF16)

    here = pl.BlockSpec((1, tr, c), lambda i, j: (i, j, 0))
    return pl.pallas_call(
        body, name=name, grid=(nch, nb),
        in_specs=[here, pl.BlockSpec((1, tr, c), lambda i, j: (i, nb + j, 0)), here],
        out_specs=[here, here],
        out_shape=[jax.ShapeDtypeStruct((nch, r2, c), F32), jax.ShapeDtypeStruct((nch, r2, c), BF16)],
        compiler_params=_params(("parallel", "parallel")),
    )(g, g, recv)


def grads_rider_chips(arrs):
    def copies(ins, outs, send_sems, recv_sems):
        x, y, c = _coords()
        chips, chip_ids = _ici_peers(x, y)
        pairs = []
        for a in range(len(arrs)):
            for k in range(3):
                to = (*chips[k], c)
                pairs.append((_remote(ins[a].at[chip_ids[k]], outs[a].at[k], send_sems, recv_sems, 3 * a + k, to),
                              _remote(outs[a].at[k], outs[a].at[k], send_sems, recv_sems, 3 * a + k, to)))
        return pairs

    return Rider(list(arrs), [jax.ShapeDtypeStruct((3, *a.shape[1:]), a.dtype) for a in arrs], {}, 3 * len(arrs),
                 copies)


def grad_sum(p32, recv, layer, nl, buf, name):
    _, r2, c = p32.shape
    tr = _tile(r2, 256, 16)
    nb = r2 // tr

    def body(p0_ref, p1_ref, p2_ref, p3_ref, r0_ref, r1_ref, r2_ref, *rest):
        o_ref = rest[-1]
        x, y, _ = _coords()
        chip = 2 * x + y
        own = jnp.where(chip == 0, p0_ref[...], jnp.where(chip == 1, p1_ref[...],
                                                        jnp.where(chip == 2, p2_ref[...], p3_ref[...])))
        o_ref[...] = own + r0_ref[...].astype(F32) + r1_ref[...].astype(F32) + r2_ref[...].astype(F32)

    slot = lambda k: pl.BlockSpec((1, tr, c), lambda j: (k, j, 0))
    in_specs = [slot(k) for k in range(N_CHIPS)] + [slot(k) for k in range(3)]
    args = [p32] * N_CHIPS + [recv] * 3
    aliases = {}
    if buf is not None:
        in_specs.append(pl.BlockSpec(memory_space=pl.ANY))
        args.append(buf)
        aliases = {len(args) - 1: 0}
    return pl.pallas_call(
        body, name=name, grid=(nb,), in_specs=in_specs,
        out_specs=pl.BlockSpec((1, tr, c), lambda j: (layer, j, 0)),
        out_shape=jax.ShapeDtypeStruct((nl, r2, c), F32),
        input_output_aliases=aliases,
        compiler_params=_params(("parallel",)),
    )(*args)


def grads_rider_exchange(bufs):
    def copies(ins, outs, send_sems, recv_sems):
        x, y, c = _coords()
        sibling = (x, y, 1 - c)
        return [(_remote(ins[a], outs[a], send_sems, recv_sems, a, sibling),
                 _remote(outs[a], outs[a], send_sems, recv_sems, a, sibling)) for a in range(len(bufs))]

    return Rider(list(bufs), [jax.ShapeDtypeStruct(b.shape, b.dtype) for b in bufs], {}, len(bufs), copies)


def allreduce_small(buf, name):
    r, cdim = buf.shape

    def body(x_ref, o_ref, gath, send_sems, recv_sems):
        x, y, c = _coords()
        me, sibling = (x, y, c), (x, y, 1 - c)
        chips, _ = _ici_peers(x, y)

        def slot(px, py, pc):
            return gath.at[4 * px + 2 * py + pc]

        def copy(k, block, to, src=None):
            return pltpu.make_async_remote_copy(
                src_ref=slot(*block) if src is None else src, dst_ref=slot(*block),
                send_sem=send_sems.at[k], recv_sem=recv_sems.at[k], device_id=to, device_id_type=MESH)

        gath[4 * x + 2 * y + c] = x_ref[...]
        first = [copy(0, me, sibling, src=x_ref)]
        first += [copy(1 + j, me, (*chip, c), src=x_ref) for j, chip in enumerate(chips)]
        for cp in first:
            cp.start()
        passed = [copy(4 + j, (*chip, c), sibling) for j, chip in enumerate(chips)]
        for j, chip in enumerate(chips):
            copy(1 + j, (*chip, c), me).wait_recv()
            passed[j].start()
        copy(0, sibling, me).wait_recv()
        for j, chip in enumerate(chips):
            copy(4 + j, (*chip, 1 - c), me).wait_recv()
        for cp in first + passed:
            cp.wait_send()
        acc = gath[0]
        for d in range(1, 8):
            acc = acc + gath[d]
        o_ref[...] = acc

    return pl.pallas_call(
        body, name=name,
        in_specs=[pl.BlockSpec(memory_space=pltpu.VMEM)], out_specs=pl.BlockSpec(memory_space=pltpu.VMEM),
        out_shape=jax.ShapeDtypeStruct((r, cdim), F32),
        scratch_shapes=[pltpu.VMEM((8, r, cdim), F32), pltpu.SemaphoreType.DMA((7,)), pltpu.SemaphoreType.DMA((7,))],
        compiler_params=pltpu.CompilerParams(has_side_effects=True),
    )(buf)


def _expand_heads(v):
    return jnp.repeat(v.astype(F32), HEAD_DIM).reshape(1, -1)


def _pad_lanes(v):
    return jnp.pad(v.astype(F32), (0, LANES - v.shape[0])).reshape(1, LANES)


def local_step(x, tgt, p, comm, cols):
    nl = p["norm_mix_w"].shape[0]
    d = x.shape[1]
    cw = p["short_conv_w"].shape[2]
    si = p["ssd_norm_w"].shape[1]
    ff, npad = comm.ff, comm.npad
    nh = si // HEAD_DIM
    gn = GROUPS * STATE
    dt_off = 3 * cw + si + si + 2 * gn
    assert cols == dt_off + nh and nh <= LANES and dt_off % LANES == 0 and npad == dt_off + LANES
    pieces = [(0, cw), (cw, cw), (2 * cw, cw), (3 * cw, si), (3 * cw + si, si), (3 * cw + 2 * si, 2 * gn),
              (dt_off, LANES)]

    saved = []
    for l in range(nl):
        nw1 = p["norm_mix_w"][l].reshape(1, d)
        nw2 = p["norm_mlp_w"][l].reshape(1, d)
        kw3 = p["short_conv_w"][l]
        kwx, kwb = p["ssd_conv_w"][l][:, :si], p["ssd_conv_w"][l][:, si:]
        bx, bb = p["ssd_conv_b"][l][:si].reshape(1, si), p["ssd_conv_b"][l][si:].reshape(1, 2 * gn)
        dtb = _pad_lanes(p["dt_bias"][l])
        aneg = _pad_lanes(-jnp.exp(p["a_log"][l]))
        dsk = _expand_heads(p["d_skip"][l])
        snw = p["ssd_norm_w"][l].reshape(1, si)
        ssd_args = (kwx, kwb, bx, bb, dtb, aneg, dsk, snw)

        w_in = comm.weight(l, "w_in")
        (proj, h), sent = norm_matmul(x, nw1, w_in, 0, dt_off, 0, BF16, "in_proj", rider=comm.rider("in_proj", l))
        comm.done("in_proj", l, sent)
        (dt_raw,), _ = norm_matmul(x, nw1, w_in, 0, LANES, dt_off // LANES, F32, "dt_proj", emit_h=False)
        y_mix = conv_mixer_fwd(proj, kw3, cw, cw + si, "conv_mixer_fwd")
        (y_mix, *ssd_saved), sent = ssd_fwd(proj, dt_raw, y_mix, *ssd_args, cw, si, "ssd_fwd",
                                            rider=comm.rider("ssd_fwd", l))
        comm.done("ssd_fwd", l, sent)
        x2, _ = matmul(y_mix, comm.weight(l, "w_out"), 0, False, d, F32, "out_proj", residual=x)
        (up, h2), sent = norm_matmul(x2, nw2, comm.weight(l, "w_up"), 0, ff, 0, BF16, "up_proj",
                                     rider=comm.rider("up_proj", l))
        comm.done("up_proj", l, sent)
        x3, sent = matmul(up, comm.weight(l, "w_down"), 0, False, d, F32, "down_proj", lhs_fn=_relu2, residual=x2,
                          rider=comm.rider("down_proj", l))
        comm.done("down_proj", l, sent)
        saved.append((x, h, proj, dt_raw, y_mix, ssd_saved, x2, h2, up, nw1, nw2, kw3, ssd_args))
        x = x3

    dx, dwf, lvec = final_loss(x, p["final_norm_w"].reshape(1, d), tgt, "final_loss")
    loss = jnp.sum(lvec)

    names = ("norm_mix_w", "short_conv_w", "ssd_conv_w", "ssd_conv_b", "dt_bias", "a_log", "d_skip",
             "ssd_norm_w", "norm_mlp_w")
    grads = {k: [None] * nl for k in names}
    for l in reversed(range(nl)):
        x0, h, proj, dt_raw, y_mix, ssd_saved, x2, h2, up, nw1, nw2, kw3, ssd_args = saved[l]
        kwx, kwb, _, _, dtb, aneg, dsk, snw = ssd_args
        dup, sent = matmul(dx, comm.weight(l, "w_down"), 0, True, ff, BF16, "down_bwd", relu_gate=up,
                           rider=comm.rider("down_bwd", l))
        comm.done("down_bwd", l, sent)
        comm.take_gradient(l, "w_down", matmul_tn(up, dx, "down_wgrad", a_fn=_relu2))
        (dx2, dnw2), _ = matmul_normbwd([dup], [(0, ff)], comm.weight(l, "w_up"), 0, x2, nw2, dx, "up_bwd")
        comm.take_gradient(l, "w_up", matmul_tn(h2, dup, "up_wgrad", by_chip=True))
        grads["norm_mlp_w"][l] = dnw2.reshape(d)
        dy, sent = matmul(dx2, comm.weight(l, "w_out"), 0, True, cw + si, BF16, "out_bwd",
                          rider=comm.rider("out_bwd", l))
        comm.done("out_bwd", l, sent)
        comm.take_gradient(l, "w_out", matmul_tn(y_mix, dx2, "out_wgrad"))
        dub, duc, duh, dkw3 = conv_mixer_bwd(proj, dy, kw3, cw, "conv_mixer_bwd")
        (dz, dxs, dbc, ddt, dkwx, dkwb, dbx, dbb, ddtb, da, ddsk, dsnw), sent = ssd_bwd(
            proj, dt_raw, *ssd_saved, dy, kwx, kwb, dtb, aneg, dsk, snw, cw, si, "ssd_bwd",
            rider=comm.rider("ssd_bwd", l))
        comm.done("ssd_bwd", l, sent)
        dpieces = [dub, duc, duh, dz, dxs, dbc, ddt]
        (dxl, dnw1), sent = matmul_normbwd(dpieces, pieces, comm.weight(l, "w_in"), 0, x0, nw1, dx2, "in_bwd",
                                           rider=comm.rider("in_bwd", l))
        comm.done("in_bwd", l, sent)
        comm.take_gradient(l, "w_in", split_to_chips(
            [matmul_tn(h, dp, "in_wgrad_%d" % i) for i, dp in enumerate(dpieces)], cols, "in_wgrad_split"))
        grads["norm_mix_w"][l] = dnw1.reshape(d)
        grads["short_conv_w"][l] = dkw3
        grads["ssd_conv_w"][l] = jnp.concatenate([dkwx, dkwb], axis=1)
        grads["ssd_conv_b"][l] = jnp.concatenate([dbx, dbb], axis=1).reshape(-1)
        grads["dt_bias"][l] = ddtb[0, :nh]
        grads["a_log"][l] = da[0, :nh] * aneg[0, :nh]
        grads["d_skip"][l] = jnp.sum(ddsk.reshape(nh, HEAD_DIM), axis=1)
        grads["ssd_norm_w"][l] = dsnw.reshape(si)
        dx = dxl

    grads = {k: jnp.stack(v) for k, v in grads.items()}
    grads["final_norm_w"] = dwf.reshape(d)
    return loss, dx, grads


BIG = ("w_in", "w_out", "w_up", "w_down")
SMALL_SHARDED = ("short_conv_w", "ssd_conv_w")
SMALL_REPL = ("norm_mix_w", "ssd_conv_b", "dt_bias", "a_log", "d_skip", "ssd_norm_w", "norm_mlp_w", "final_norm_w")
WEIGHTS = ("norm_mix_w", "w_in", "short_conv_w", "ssd_conv_w", "ssd_conv_b", "dt_bias", "a_log", "d_skip",
           "ssd_norm_w", "w_out", "norm_mlp_w", "w_up", "w_down", "final_norm_w")
SMALL_COLS = 1024


def _pack_small(named):
    flat = jnp.concatenate([v.reshape(-1).astype(F32) for v in named])
    n = flat.shape[0]
    rows = -(-n // SMALL_COLS)
    rows = -(-rows // 8) * 8
    return jnp.pad(flat, (0, rows * SMALL_COLS - n)).reshape(rows, SMALL_COLS)


def _unpack_small(buf, like):
    flat = buf.reshape(-1)
    out, off = [], 0
    for v in like:
        out.append(flat[off:off + v.size].reshape(v.shape))
        off += v.size
    return out


class ChipComm:
    IO = ("w_in", "w_out")
    MLP = ("w_up", "w_down")

    def __init__(self, shards, nl, npad):
        self.shards, self.nl, self.npad = shards, nl, npad
        self.how = dict(zip(BIG, LAYER_HOW))
        self.ff = N_CHIPS * shards["w_up"].shape[2]
        self.w = {}
        self.landed = {}
        self.grad = {}
        self.sums = {}
        self.bufs = {k: None for k in BIG}
        first = run_rider(self._ici(self.IO, 0), "gather_first_ici")
        self._gathered(self.IO, 0, run_rider(self._d2d(self.IO, first), "gather_first_d2d"))

    def _ici(self, group, l):
        return weight_rider_ici([self.shards[k] for k in group], [self.how[k] for k in group], l)

    def _d2d(self, group, landed):
        return weight_rider_d2d(landed, [tuple(self.shards[k].shape[1:]) for k in group],
                                [self.how[k] for k in group])

    def _gathered(self, group, l, arrays):
        for k, g in zip(group, arrays):
            self.w[(l, k)] = join_from_chips(g, self.npad, "w_in_join") if k == "w_in" else g

    def _to_sibling(self, group, l):
        return grads_rider_sibling([self.grad[(l, k)] for k in group])

    def _summed(self, group, l, from_sibling):
        self.sums[group] = (l, [chip_sum(self.grad.pop((l, k)), r, "chip_sum") for k, r in zip(group, from_sibling)])

    def _to_chips(self, group):
        return grads_rider_chips([s[1] for s in self.sums[group][1]])

    def _reduced(self, group, from_chips):
        l, sums = self.sums.pop(group)
        for k, s, r in zip(group, sums, from_chips):
            self.bufs[k] = grad_sum(s[0], r, l, self.nl, self.bufs[k], "grad_sum")

    def weight(self, l, name):
        return self.w[(l, name)]

    def take_gradient(self, l, name, g):
        self.grad[(l, name)] = g if g.ndim == 3 else g.reshape(N_CHIPS, g.shape[0] // N_CHIPS, g.shape[1])

    def rider(self, point, l):
        more = l + 1 < self.nl
        if point == "in_proj":
            return self._ici(self.MLP, l)
        if point == "ssd_fwd":
            return self._d2d(self.MLP, self.landed[self.MLP])
        if point == "up_proj":
            return self._ici(self.IO, l + 1) if more else None
        if point == "down_proj":
            return self._d2d(self.IO, self.landed[self.IO]) if more else None
        if point == "down_bwd":
            return self._to_sibling(self.IO, l + 1) if more else None
        if point == "out_bwd":
            return self._to_sibling(self.MLP, l)
        if point == "ssd_bwd":
            return self._to_chips(self.IO) if more else None
        return self._to_chips(self.MLP)

    def done(self, point, l, results):
        if not results:
            return
        if point in ("in_proj", "up_proj"):
            self.landed[self.MLP if point == "in_proj" else self.IO] = results
        elif point == "ssd_fwd":
            self._gathered(self.MLP, l, results)
        elif point == "down_proj":
            self._gathered(self.IO, l + 1, results)
        elif point == "down_bwd":
            self._summed(self.IO, l + 1, results)
        elif point == "out_bwd":
            self._summed(self.MLP, l, results)
        elif point == "ssd_bwd":
            self._reduced(self.IO, results)
        else:
            self._reduced(self.MLP, results)

    def finish(self):
        self._summed(self.IO, 0, run_rider(self._to_sibling(self.IO, 0), "grads_to_sibling"))
        self._reduced(self.IO, run_rider(self._to_chips(self.IO), "grads_to_chips"))
        mine = [self.bufs[k] for k in BIG]
        theirs = run_rider(grads_rider_exchange(mine), "grads_exchange")
        return dict(zip(BIG, zip(mine, theirs)))


def kernel(x, norm_mix_w, w_in, short_conv_w, ssd_conv_w, ssd_conv_b, dt_bias, a_log, d_skip, ssd_norm_w, w_out, norm_mlp_w, w_up, w_down, final_norm_w, loss_target, m_norm_mix_w, m_w_in, m_short_conv_w, m_ssd_conv_w, m_ssd_conv_b, m_dt_bias, m_a_log, m_d_skip, m_ssd_norm_w, m_w_out, m_norm_mlp_w, m_w_up, m_w_down, m_final_norm_w, v_norm_mix_w, v_w_in, v_short_conv_w, v_ssd_conv_w, v_ssd_conv_b, v_dt_bias, v_a_log, v_d_skip, v_ssd_norm_w, v_w_out, v_norm_mlp_w, v_w_up, v_w_down, v_final_norm_w):
    w = dict(norm_mix_w=norm_mix_w, w_in=w_in, short_conv_w=short_conv_w, ssd_conv_w=ssd_conv_w,
             ssd_conv_b=ssd_conv_b, dt_bias=dt_bias, a_log=a_log, d_skip=d_skip, ssd_norm_w=ssd_norm_w, w_out=w_out,
             norm_mlp_w=norm_mlp_w, w_up=w_up, w_down=w_down, final_norm_w=final_norm_w)
    m = dict(norm_mix_w=m_norm_mix_w, w_in=m_w_in, short_conv_w=m_short_conv_w, ssd_conv_w=m_ssd_conv_w,
             ssd_conv_b=m_ssd_conv_b, dt_bias=m_dt_bias, a_log=m_a_log, d_skip=m_d_skip, ssd_norm_w=m_ssd_norm_w,
             w_out=m_w_out, norm_mlp_w=m_norm_mlp_w, w_up=m_w_up, w_down=m_w_down, final_norm_w=m_final_norm_w)
    v = dict(norm_mix_w=v_norm_mix_w, w_in=v_w_in, short_conv_w=v_short_conv_w, ssd_conv_w=v_ssd_conv_w,
             ssd_conv_b=v_ssd_conv_b, dt_bias=v_dt_bias, a_log=v_a_log, d_skip=v_d_skip, ssd_norm_w=v_ssd_norm_w,
             w_out=v_w_out, norm_mlp_w=v_norm_mlp_w, w_up=v_w_up, w_down=v_w_down, final_norm_w=v_final_norm_w)
    xi, yi, ci = lax.axis_index("x"), lax.axis_index("y"), lax.axis_index("c")
    chip = 2 * xi + yi
    nl = w_up.shape[0]
    cols = N_CHIPS * w_in.shape[2]
    npad = cols // LANES * LANES + LANES

    full = dict(w)
    small_gathered = gather_weights([w[k] for k in SMALL_SHARDED], ["lead"] * len(SMALL_SHARDED), "gather_small")
    for k, g4 in zip(SMALL_SHARDED, small_gathered):
        full[k] = jnp.concatenate([g4[j] for j in range(N_CHIPS)], axis=2)
    comm = ChipComm({k: w[k].astype(BF16) for k in BIG}, nl, npad)

    loss, grad_x, grads = local_step(x[0], loss_target[0], full, comm, cols)
    loss = lax.psum(loss, ("x", "y", "c"))
    halves = comm.finish()
    g_shard = {}

    small_names = SMALL_REPL + SMALL_SHARDED
    small_sum = allreduce_small(_pack_small([grads[k] for k in small_names]), "allreduce_small")
    for k, g in zip(small_names, _unpack_small(small_sum, [grads[k] for k in small_names])):
        if k in SMALL_SHARDED:
            width = w[k].shape[2]
            g = lax.dynamic_slice_in_dim(g, chip * width, width, axis=2)
        g_shard[k] = g

    delta, new_m, new_v = {}, {}, {}
    for k in BIG:
        g_shard[k], delta[k], new_m[k], new_v[k] = adamw_halves(w[k], *halves[k], m[k], v[k], "adamw_%s" % k)
    packed = [_pack_small([d_[k] for k in small_names]) for d_ in (w, g_shard, m, v)]
    outs = adamw(*packed, "adamw_small")
    for d_, buf in zip((delta, new_m, new_v), outs):
        for k, val in zip(small_names, _unpack_small(buf, [w[k] for k in small_names])):
            d_[k] = val

    return (loss, grad_x[None], *[g_shard[k] for k in WEIGHTS], *[delta[k] for k in WEIGHTS],
            *[new_m[k] for k in WEIGHTS], *[new_v[k] for k in WEIGHTS])
```

```python
import functools

import jax
import jax.numpy as jnp
from jax import lax
from jax.experimental import pallas as pl
from jax.experimental.pallas import tpu as pltpu

F32 = jnp.float32
BF16 = jnp.bfloat16

EPS = 1e-5
HEAD_DIM = 64
STATE = 128
GROUPS = 2
SHORT_K = 3
SSD_K = 4
LANES = 128
PAIR = LANES // HEAD_DIM
SCAN_CHUNK = 256
HALO = 16
N_CHIPS = 4
VMEM_LIMIT = 56 * 1024 * 1024

ADAM_LR = 0.001
ADAM_B1 = 0.9
ADAM_B2 = 0.999
ADAM_EPS = 1e-08
ADAM_WD = 0.01
ADAM_STEP = 10

MESH = pl.DeviceIdType.MESH


def _params(sem):
    return pltpu.CompilerParams(dimension_semantics=sem, vmem_limit_bytes=VMEM_LIMIT)


def _tile(n, cap, quantum):
    if n <= cap:
        return n
    best = None
    for t in range(quantum, cap + 1, quantum):
        if n % t == 0:
            best = t
    assert best is not None, (n, cap, quantum)
    return best


def _dot(a, b):
    return jnp.dot(a, b, preferred_element_type=F32)


def _dot_nt(a, b):
    return lax.dot_general(a, b, (((1,), (1,)), ((), ())), preferred_element_type=F32)


def _dot_tn(a, b):
    return lax.dot_general(a, b, (((0,), (0,)), ((), ())), preferred_element_type=F32)


def _dot_exact(a, b):
    return jnp.dot(a, b, precision=lax.Precision.HIGHEST, preferred_element_type=F32)


def _sigmoid(x):
    return pl.reciprocal(1.0 + jnp.exp(-x), approx=True)


def _softplus(x):
    return jnp.maximum(x, 0.0) + jnp.log(1.0 + jnp.exp(-jnp.abs(x)))


def _relu2(v):
    return jnp.square(jnp.maximum(v, 0.0))


class Rider:
    def __init__(self, ins, out_shapes, aliases, n_sems, copies):
        self.ins, self.out_shapes, self.aliases, self.n_sems, self.copies = ins, out_shapes, aliases, n_sems, copies


def _any_specs(n):
    return [pl.BlockSpec(memory_space=pl.ANY)] * n


def _ride(body, grid, in_specs, out_specs, out_shape, scratch, args, aliases, rider, sem, name):
    n_in, n_out, n_scr = len(in_specs), len(out_specs), len(scratch)
    if rider is None:
        outs = pl.pallas_call(
            body, name=name, grid=grid, in_specs=in_specs, out_specs=out_specs, out_shape=out_shape,
            scratch_shapes=scratch, input_output_aliases=aliases, compiler_params=_params(sem))(*args)
        return list(outs), []
    ri, ro = len(rider.ins), len(rider.out_shapes)
    last = tuple(g - 1 for g in grid)

    def wrapped(*refs):
        ins = refs[:n_in]
        r_ins = refs[n_in:n_in + ri]
        outs = refs[n_in + ri:n_in + ri + n_out]
        r_outs = refs[n_in + ri + n_out:n_in + ri + n_out + ro]
        scr = refs[n_in + ri + n_out + ro:n_in + ri + n_out + ro + n_scr]
        send_sems, recv_sems = refs[-2:]
        ids = [pl.program_id(a) for a in range(len(grid))]
        at_first = functools.reduce(jnp.logical_and, [i == 0 for i in ids])
        at_last = functools.reduce(jnp.logical_and, [i == e for i, e in zip(ids, last)])

        @pl.when(at_first)
        def _():
            for cp, _ in rider.copies(r_ins, r_outs, send_sems, recv_sems):
                cp.start()

        body(*ins, *outs, *scr)

        @pl.when(at_last)
        def _():
            for cp, landed in rider.copies(r_ins, r_outs, send_sems, recv_sems):
                cp.wait_send()
                landed.wait_recv()

    all_aliases = dict(aliases)
    all_aliases.update({n_in + a: n_out + b for a, b in rider.aliases.items()})
    outs = pl.pallas_call(
        wrapped, name=name, grid=grid, in_specs=list(in_specs) + _any_specs(ri),
        out_specs=list(out_specs) + _any_specs(ro), out_shape=list(out_shape) + list(rider.out_shapes),
        scratch_shapes=list(scratch) + [pltpu.SemaphoreType.DMA((rider.n_sems,)),
                                        pltpu.SemaphoreType.DMA((rider.n_sems,))],
        input_output_aliases=all_aliases, compiler_params=_params(sem))(*args, *rider.ins)
    return list(outs[:n_out]), list(outs[n_out:])


def run_rider(rider, name):
    ri, ro = len(rider.ins), len(rider.out_shapes)

    def body(*refs):
        send_sems, recv_sems = refs[-2:]
        pairs = rider.copies(refs[:ri], refs[ri:ri + ro], send_sems, recv_sems)
        for cp, _ in pairs:
            cp.start()
        for cp, landed in pairs:
            cp.wait_send()
            landed.wait_recv()

    return list(pl.pallas_call(
        body, name=name, in_specs=_any_specs(ri), out_specs=_any_specs(ro), out_shape=list(rider.out_shapes),
        scratch_shapes=[pltpu.SemaphoreType.DMA((rider.n_sems,)), pltpu.SemaphoreType.DMA((rider.n_sems,))],
        input_output_aliases=dict(rider.aliases),
        compiler_params=pltpu.CompilerParams(has_side_effects=True))(*rider.ins))


def norm_matmul(x, nw, w, layer, n, col0, out_dtype, name, emit_h=True, rider=None):
    t, d = x.shape
    mxu_cols = 2 * LANES
    tn = _tile(n, 1536, mxu_cols if n % mxu_cols == 0 else LANES)
    if n % mxu_cols == 0 and tn < 1024 <= n:
        tn = _tile(n, 3072, mxu_cols)
    tm = _tile(t, 512 if tn > 1536 else 1024, 8)
    nj = n // tn

    def body(x_ref, nw_ref, w_ref, o_ref, h_ref):
        @pl.when(pl.program_id(1) == 0)
        def _():
            xf = x_ref[...]
            r = lax.rsqrt(jnp.mean(xf * xf, axis=-1, keepdims=True) + EPS)
            h_ref[...] = (xf * r * nw_ref[...]).astype(BF16)

        o_ref[...] = _dot(h_ref[...], w_ref[...]).astype(out_dtype)

    out_specs = [pl.BlockSpec((tm, tn), lambda i, j: (i, j))]
    out_shape = [jax.ShapeDtypeStruct((t, n), out_dtype)]
    if emit_h:
        out_specs.append(pl.BlockSpec((tm, d), lambda i, j: (i, 0)))
        out_shape.append(jax.ShapeDtypeStruct((t, d), BF16))
    return _ride(
        body, (t // tm, nj),
        [pl.BlockSpec((tm, d), lambda i, j: (i, 0)), pl.BlockSpec((1, d), lambda i, j: (0, 0)),
         pl.BlockSpec((None, d, tn), lambda i, j: (layer, 0, col0 * nj + j))],
        out_specs, out_shape, [] if emit_h else [pltpu.VMEM((tm, d), BF16)], [x, nw, w], {}, rider,
        ("parallel", "arbitrary"), name)


def matmul(lhs, w, layer, transposed, n, out_dtype, name, *, lhs_fn=None, residual=None, relu_gate=None,
           rider=None):
    t, k = lhs.shape
    tm = _tile(t, 512 if k > 2048 else 1024, 8)
    tn = _tile(n, 1024, LANES)
    staged = lhs.dtype != BF16 or lhs_fn is not None
    fn = lhs_fn if lhs_fn is not None else (lambda v: v)
    has_extra = residual is not None or relu_gate is not None
    dot = _dot_nt if transposed else _dot

    def body(*refs):
        a_ref, w_ref = refs[:2]
        extra = refs[2] if has_extra else None
        o_ref = refs[3] if has_extra else refs[2]
        if staged:
            s_ref = refs[-1]

            @pl.when(pl.program_id(1) == 0)
            def _():
                s_ref[...] = fn(a_ref[...].astype(F32)).astype(BF16)

            a_ref = s_ref
        acc = dot(a_ref[...], w_ref[...])
        if residual is not None:
            acc = acc + extra[...]
        if relu_gate is not None:
            acc = acc * (2.0 * jnp.maximum(extra[...].astype(F32), 0.0))
        o_ref[...] = acc.astype(out_dtype)

    if transposed:
        w_spec = pl.BlockSpec((None, tn, k), lambda i, j: (layer, j, 0))
    else:
        w_spec = pl.BlockSpec((None, k, tn), lambda i, j: (layer, 0, j))
    in_specs = [pl.BlockSpec((tm, k), lambda i, j: (i, 0)), w_spec]
    args = [lhs, w]
    if has_extra:
        in_specs.append(pl.BlockSpec((tm, tn), lambda i, j: (i, j)))
        args.append(residual if residual is not None else relu_gate)
    outs, extra = _ride(
        body, (t // tm, n // tn), in_specs, [pl.BlockSpec((tm, tn), lambda i, j: (i, j))],
        [jax.ShapeDtypeStruct((t, n), out_dtype)], [pltpu.VMEM((tm, k), BF16)] if staged else [], args, {},
        rider, ("parallel", "arbitrary"), name)
    return outs[0], extra


def matmul_normbwd(lhs, pieces, w, layer, x, nw, dres, name, rider=None):
    t, d = x.shape
    nl = len(lhs)
    tm = _tile(t, 512, 8)
    for off, width in pieces:
        assert off % width == 0

    def body(*refs):
        lrefs = refs[:nl]
        wrefs = refs[nl:2 * nl]
        x_ref, nw_ref, dres_ref, dx_ref, dnw_ref = refs[2 * nl:]
        dh = _dot_nt(lrefs[0][...].astype(BF16), wrefs[0][...])
        for a_ref, w_ref in zip(lrefs[1:], wrefs[1:]):
            dh = dh + _dot_nt(a_ref[...].astype(BF16), w_ref[...])
        xf = x_ref[...]
        r = lax.rsqrt(jnp.mean(xf * xf, axis=-1, keepdims=True) + EPS)
        nx = xf * r
        dn = dh * nw_ref[...]
        dx = r * (dn - nx * jnp.mean(dn * nx, axis=-1, keepdims=True))
        dx_ref[...] = dres_ref[...] + dx

        @pl.when(pl.program_id(0) == 0)
        def _():
            dnw_ref[...] = jnp.zeros_like(dnw_ref)

        dnw_ref[...] += jnp.sum(dh * nx, axis=0, keepdims=True)

    in_specs = [pl.BlockSpec((tm, width), lambda i: (i, 0)) for _, width in pieces]
    in_specs += [pl.BlockSpec((None, d, width), (lambda blk: (lambda i: (layer, 0, blk)))(off // width),
                              pipeline_mode=pl.Buffered(1))
                 for off, width in pieces]
    in_specs += [pl.BlockSpec((tm, d), lambda i: (i, 0)), pl.BlockSpec((1, d), lambda i: (0, 0)),
                 pl.BlockSpec((tm, d), lambda i: (i, 0))]
    return _ride(
        body, (t // tm,), in_specs,
        [pl.BlockSpec((tm, d), lambda i: (i, 0)), pl.BlockSpec((1, d), lambda i: (0, 0))],
        [jax.ShapeDtypeStruct((t, d), F32), jax.ShapeDtypeStruct((1, d), F32)], [],
        [*lhs, *([w] * nl), x, nw, dres], {}, rider, ("arbitrary",), name)


def matmul_tn(a, b, name, *, a_fn=None, by_chip=False):
    t, k = a.shape
    n = b.shape[1]
    tk = _tile(k, 1024, LANES)
    tn = _tile(n // N_CHIPS if by_chip else n, 1024, LANES)
    tt = _tile(t, 1024, 8)
    nt = t // tt
    fn = a_fn if a_fn is not None else (lambda v: v)

    def body(a_ref, b_ref, o_ref, acc_ref):
        @pl.when(pl.program_id(2) == 0)
        def _():
            acc_ref[...] = jnp.zeros_like(acc_ref)

        av = a_ref[...]
        if a_fn is not None:
            av = fn(av.astype(F32))
        acc_ref[...] += _dot_tn(av.astype(BF16), b_ref[...].astype(BF16))

        @pl.when(pl.program_id(2) == nt - 1)
        def _():
            o_ref[...] = acc_ref[...]

    if by_chip:
        per = n // N_CHIPS // tn
        out_spec = pl.BlockSpec((None, tk, tn), lambda i, j, s: (j // per, i, j % per))
        out_shape = jax.ShapeDtypeStruct((N_CHIPS, k, n // N_CHIPS), F32)
    else:
        out_spec = pl.BlockSpec((tk, tn), lambda i, j, s: (i, j))
        out_shape = jax.ShapeDtypeStruct((k, n), F32)
    return pl.pallas_call(
        body, name=name, grid=(k // tk, n // tn, nt),
        in_specs=[pl.BlockSpec((tt, tk), lambda i, j, s: (s, i)),
                  pl.BlockSpec((tt, tn), lambda i, j, s: (s, j))],
        out_specs=out_spec, out_shape=out_shape,
        scratch_shapes=[pltpu.VMEM((tk, tn), F32)],
        compiler_params=_params(("parallel", "parallel", "arbitrary")),
    )(a, b)


def split_to_chips(pieces, cols, name):
    d = pieces[0].shape[0]
    widths = [p.shape[1] for p in pieces]
    w = cols // N_CHIPS
    tr = _tile(d, 256, 8)
    npc = len(pieces)

    def body(*refs):
        o_ref, row = refs[npc], refs[npc + 1]
        off = 0
        for r, n in zip(refs[:npc], widths):
            row[:, off:off + n] = r[...]
            off += n
        for j in range(N_CHIPS):
            o_ref[j] = row[:, j * w:(j + 1) * w]

    return pl.pallas_call(
        body, name=name, grid=(d // tr,),
        in_specs=[pl.BlockSpec((tr, n), lambda i: (i, 0)) for n in widths],
        out_specs=pl.BlockSpec((N_CHIPS, tr, w), lambda i: (0, i, 0)),
        out_shape=jax.ShapeDtypeStruct((N_CHIPS, d, w), F32),
        scratch_shapes=[pltpu.VMEM((tr, sum(widths)), F32)],
        compiler_params=_params(("parallel",)),
    )(*pieces)


def join_from_chips(g4, npad, name):
    _, nl, d, w = g4.shape
    tr = _tile(d, 256, HALO)

    def body(g_ref, o_ref):
        for j in range(N_CHIPS):
            o_ref[:, j * w:(j + 1) * w] = g_ref[j]
        o_ref[:, N_CHIPS * w:] = jnp.zeros((tr, npad - N_CHIPS * w), o_ref.dtype)

    return pl.pallas_call(
        body, name=name, grid=(nl, d // tr),
        in_specs=[pl.BlockSpec((N_CHIPS, None, tr, w), lambda l, i: (0, l, i, 0))],
        out_specs=pl.BlockSpec((None, tr, npad), lambda l, i: (l, i, 0)),
        out_shape=jax.ShapeDtypeStruct((nl, d, npad), g4.dtype),
        compiler_params=_params(("parallel", "parallel")),
    )(g4)


def conv_mixer_fwd(proj, kw, cw, out_cols, name):
    t = proj.shape[0]
    tm = _tile(t, 1024, HALO)
    tc = _tile(cw, 1024, LANES)
    nj = cw // tc
    hb = tm // HALO

    def body(ub_ref, uc_ref, uh_ref, ucp_ref, uhp_ref, kw_ref, y_ref):
        i = pl.program_id(0)
        taps = [kw_ref[pl.ds(k, 1), :] for k in range(SHORT_K)]
        row = lax.broadcasted_iota(jnp.int32, (8, tc), 0)
        vp = ucp_ref[...].astype(F32) * uhp_ref[...].astype(F32)

        def conv(block, before):
            acc = taps[SHORT_K - 1] * block
            for k in range(SHORT_K - 1):
                s = SHORT_K - 1 - k
                acc = acc + taps[k] * jnp.where(row >= s, pltpu.roll(block, s, 0), pltpu.roll(before, s, 0))
            return acc

        def strip(s, before):
            rows = pl.ds(pl.multiple_of(s * HALO, HALO), HALO)
            v = uc_ref[rows, :].astype(F32) * uh_ref[rows, :].astype(F32)
            top, bottom = v[0:8], v[8:HALO]
            cv = jnp.concatenate([conv(top, before), conv(bottom, top)], axis=0)
            y_ref[rows, :] = (ub_ref[rows, :].astype(F32) * cv).astype(BF16)
            return bottom

        lax.fori_loop(0, tm // HALO, strip, jnp.where(i > 0, vp[8:HALO], 0.0))

    prev = lambda off: (lambda i, j: (jnp.maximum(i * hb - 1, 0), off + j))
    return pl.pallas_call(
        body, name=name, grid=(t // tm, nj),
        in_specs=[pl.BlockSpec((tm, tc), lambda i, j: (i, j)),
                  pl.BlockSpec((tm, tc), lambda i, j: (i, nj + j)),
                  pl.BlockSpec((tm, tc), lambda i, j: (i, 2 * nj + j)),
                  pl.BlockSpec((HALO, tc), prev(nj)),
                  pl.BlockSpec((HALO, tc), prev(2 * nj)),
                  pl.BlockSpec((SHORT_K, tc), lambda i, j: (0, j))],
        out_specs=pl.BlockSpec((tm, tc), lambda i, j: (i, j)),
        out_shape=jax.ShapeDtypeStruct((t, out_cols), BF16),
        compiler_params=_params(("parallel", "parallel")),
    )(proj, proj, proj, proj, proj, kw)


def conv_mixer_bwd(proj, dy, kw, cw, name):
    t = proj.shape[0]
    tm = _tile(t, 1024, HALO)
    tc = _tile(cw, 1024, LANES)
    nj = cw // tc
    hb = tm // HALO
    ni = t // tm
    last_hb = t // HALO - 1

    def body(ub_ref, uc_ref, uh_ref, dy_ref, ucp_ref, uhp_ref, ubn_ref, dyn_ref, kw_ref,
             dub_ref, duc_ref, duh_ref, dkw_ref):
        i = pl.program_id(1)
        nstrips = tm // HALO
        taps = [kw_ref[pl.ds(k, 1), :] for k in range(SHORT_K)]
        row = lax.broadcasted_iota(jnp.int32, (8, tc), 0)
        vp = ucp_ref[...].astype(F32) * uhp_ref[...].astype(F32)
        dcvn = dyn_ref[...].astype(F32) * ubn_ref[...].astype(F32)

        def shifted(block, before, s):
            return jnp.where(row >= s, pltpu.roll(block, s, 0), pltpu.roll(before, s, 0))

        def lifted(block, after, s):
            return jnp.where(row < 8 - s, pltpu.roll(block, 8 - s, 0), pltpu.roll(after, 8 - s, 0))

        def down(s, carry):
            before, sums = carry
            rows = pl.ds(pl.multiple_of(s * HALO, HALO), HALO)
            v = uc_ref[rows, :].astype(F32) * uh_ref[rows, :].astype(F32)
            dyv = dy_ref[rows, :].astype(F32)
            dcv = dyv * ub_ref[rows, :].astype(F32)
            cvs = []
            sums = list(sums)
            for block, above, dcb in ((v[0:8], before, dcv[0:8]), (v[8:HALO], v[0:8], dcv[8:HALO])):
                moved = [shifted(block, above, SHORT_K - 1 - k) for k in range(SHORT_K - 1)] + [block]
                cvs.append(sum(taps[k] * moved[k] for k in range(SHORT_K)))
                sums = [sums[k] + dcb * moved[k] for k in range(SHORT_K)]
            dub_ref[rows, :] = (dyv * jnp.concatenate(cvs, axis=0)).astype(BF16)
            return v[8:HALO], tuple(sums)

        zero = jnp.zeros((8, tc), F32)
        _, sums = lax.fori_loop(0, nstrips, down, (jnp.where(i > 0, vp[8:HALO], 0.0), (zero,) * SHORT_K))

        def up(n, after):
            rows = pl.ds(pl.multiple_of((nstrips - 1 - n) * HALO, HALO), HALO)
            uc = uc_ref[rows, :].astype(F32)
            uh = uh_ref[rows, :].astype(F32)
            dcv = dy_ref[rows, :].astype(F32) * ub_ref[rows, :].astype(F32)
            dvs = []
            for block, below in ((dcv[0:8], dcv[8:HALO]), (dcv[8:HALO], after)):
                dvs.append(taps[SHORT_K - 1] * block
                           + sum(taps[k] * lifted(block, below, SHORT_K - 1 - k) for k in range(SHORT_K - 1)))
            dv = jnp.concatenate(dvs, axis=0)
            duc_ref[rows, :] = (dv * uh).astype(BF16)
            duh_ref[rows, :] = (dv * uc).astype(BF16)
            return dcv[0:8]

        lax.fori_loop(0, nstrips, up, jnp.where(i < ni - 1, dcvn[0:8], 0.0))

        @pl.when(i == 0)
        def _():
            dkw_ref[...] = jnp.zeros_like(dkw_ref)

        for k in range(SHORT_K):
            dkw_ref[pl.ds(k, 1), :] += jnp.sum(sums[k], axis=0, keepdims=True)

    prev = lambda off: (lambda j, i: (jnp.maximum(i * hb - 1, 0), off + j))
    nxt = lambda off: (lambda j, i: (jnp.minimum((i + 1) * hb, last_hb), off + j))
    cur = lambda off: (lambda j, i: (i, off + j))
    return pl.pallas_call(
        body, name=name, grid=(nj, ni),
        in_specs=[pl.BlockSpec((tm, tc), cur(0)), pl.BlockSpec((tm, tc), cur(nj)),
                  pl.BlockSpec((tm, tc), cur(2 * nj)), pl.BlockSpec((tm, tc), cur(0)),
                  pl.BlockSpec((HALO, tc), prev(nj)), pl.BlockSpec((HALO, tc), prev(2 * nj)),
                  pl.BlockSpec((HALO, tc), nxt(0)), pl.BlockSpec((HALO, tc), nxt(0)),
                  pl.BlockSpec((SHORT_K, tc), lambda j, i: (0, j))],
        out_specs=[pl.BlockSpec((tm, tc), cur(0)), pl.BlockSpec((tm, tc), cur(0)),
                   pl.BlockSpec((tm, tc), cur(0)), pl.BlockSpec((SHORT_K, tc), lambda j, i: (0, j))],
        out_shape=[jax.ShapeDtypeStruct((t, cw), BF16)] * 3 + [jax.ShapeDtypeStruct((SHORT_K, cw), F32)],
        compiler_params=_params(("parallel", "arbitrary")),
    )(proj, proj, proj, dy, proj, proj, proj, dy, kw)


def _head_column(mat, lane, h):
    return jnp.sum(jnp.where(lane == h, mat, 0.0), axis=-1, keepdims=True)


def _ssd_common(dt_raw_ref, dtb_ref, aneg_ref, cum_s, cumt_s, chunk):
    dt = _softplus(dt_raw_ref[...] + dtb_ref[...])
    al = dt * aneg_ref[...]
    ri = lax.broadcasted_iota(jnp.int32, (chunk, chunk), 0)
    ci = lax.broadcasted_iota(jnp.int32, (chunk, chunk), 1)
    cum = _dot_exact((ri >= ci).astype(F32), al)
    cum_s[...] = cum
    cumt_s[...] = cum.T
    return dt, cum, ri >= ci


EDGE = 16


def _shift_matrices(shift_s, chunk, kk, up):
    ri = lax.broadcasted_iota(jnp.int32, (chunk, chunk), 0)
    ci = lax.broadcasted_iota(jnp.int32, (chunk, chunk), 1)
    for k in range(kk - 1):
        s = kk - 1 - k
        shift_s[k] = ((ci - ri if up else ri - ci) == s).astype(BF16)


def _causal_conv(cur, head, kw_ref, b_ref, shift_s, kk):
    acc = b_ref[...] + kw_ref[pl.ds(kk - 1, 1), :] * cur.astype(F32)
    top = b_ref[...] + kw_ref[pl.ds(kk - 1, 1), :] * head[pl.ds(8, EDGE), :]
    for k in range(kk - 1):
        acc = acc + kw_ref[pl.ds(k, 1), :] * _dot(shift_s[k], cur)
        top = top + kw_ref[pl.ds(k, 1), :] * head[pl.ds(8 - (kk - 1) + k, EDGE), :]
    return acc, top


def ssd_fwd(proj, dt_raw, y_mix, kw_xs, kw_bc, b_xs, b_bc, dtb, aneg, dskip, normw, cw, si, name, rider=None):
    t = proj.shape[0]
    ch = min(SCAN_CHUNK, t)
    nc = t // ch
    npair = si // LANES
    ppg = npair // GROUPS
    gn = GROUPS * STATE
    gw = si // GROUPS
    assert cw == si and (3 * cw + 2 * si) % (2 * gn) == 0
    zblk = 3 * cw // si
    xsblk = zblk + 1
    bcblk = (3 * cw + 2 * si) // (2 * gn)

    def body(z_ref, xs_ref, bc_ref, dtr_ref, ymix_ref, kwx_ref, kwb_ref, bx_ref, bb_ref, dtb_ref, aneg_ref, dsk_ref,
             nw_ref, yb_ref, ys_ref, hs_ref, xcx_ref, xcb_ref,
             headx, headb, shift_s, xs_s, bc_s, h_s, gated_s, s_s, cum_s, cumt_s):
        del ymix_ref
        c = pl.program_id(0)

        @pl.when(c == 0)
        def _():
            h_s[...] = jnp.zeros_like(h_s)
            headx[0:8, :] = jnp.zeros((8, si), F32)
            headb[0:8, :] = jnp.zeros((8, 2 * gn), F32)
            _shift_matrices(shift_s, ch, SSD_K, up=False)

        for raw_ref, head, kw_ref, b_ref, pre_ref, act_s in ((xs_ref, headx, kwx_ref, bx_ref, xcx_ref, xs_s),
                                                           (bc_ref, headb, kwb_ref, bb_ref, xcb_ref, bc_s)):
            head[8:8 + EDGE, :] = raw_ref[0:EDGE, :].astype(F32)
            pre, top = _causal_conv(raw_ref[...], head, kw_ref, b_ref, shift_s, SSD_K)
            head[0:8, :] = raw_ref[ch - EDGE:ch, :].astype(F32)[EDGE - 8:EDGE]
            pre_ref[...] = pre.astype(BF16)
            pre_ref[0:EDGE, :] = top.astype(BF16)
            act_s[...] = (pre * _sigmoid(pre)).astype(act_s.dtype)
            act_s[0:EDGE, :] = (top * _sigmoid(top)).astype(act_s.dtype)

        dt, cum, tril = _ssd_common(dtr_ref, dtb_ref, aneg_ref, cum_s, cumt_s, ch)
        lane = lax.broadcasted_iota(jnp.int32, (ch, LANES), 1)
        lane1 = lax.broadcasted_iota(jnp.int32, (1, LANES), 1)
        low = lane < HEAD_DIM
        clast = cum_s[pl.ds(ch - 1, 1), :]

        for p in range(npair):
            g = p // ppg
            col = slice(p * LANES, (p + 1) * LANES)
            bg = bc_s[:, g * STATE:(g + 1) * STATE]
            cg = bc_s[:, gn + g * STATE:gn + (g + 1) * STATE]
            if p % ppg == 0:
                s_s[...] = _dot_nt(cg, bg)
            heads = (PAIR * p, PAIR * p + 1)
            ccol = [_head_column(cum, lane, h) for h in heads]
            dcol = [_head_column(dt, lane, h) for h in heads]
            cl = [jnp.sum(jnp.where(lane1 == h, clast, 0.0), axis=-1, keepdims=True) for h in heads]
            cum_px = jnp.where(low, ccol[0], ccol[1])
            dt_px = jnp.where(low, dcol[0], dcol[1])
            cl_px = jnp.where(lane1 < HEAD_DIM, cl[0], cl[1])
            xs_p = xs_s[:, col]
            xdt = xs_p * dt_px
            y = dsk_ref[:, col] * xs_p
            for hi, h in enumerate(heads):
                dec = jnp.exp(jnp.where(tril, ccol[hi] - cumt_s[pl.ds(h, 1), :], -jnp.inf))
                wm = (s_s[...] * dec).astype(BF16)
                xm = jnp.where(low if hi == 0 else jnp.logical_not(low), xdt, 0.0).astype(BF16)
                y = y + _dot(wm, xm)
            hp = h_s[p]
            hs_ref[0, p] = hp
            y = y + _dot(cg, hp.astype(BF16)) * jnp.exp(cum_px)
            st = _dot_tn(bg, (xdt * jnp.exp(cl_px - cum_px)).astype(BF16))
            h_s[p] = jnp.exp(cl_px) * hp + st
            ys_ref[:, col] = y.astype(BF16)
            zp = z_ref[:, col].astype(F32)
            gated_s[:, col] = y * zp * _sigmoid(zp)

        for g in range(GROUPS):
            col = slice(g * gw, (g + 1) * gw)
            gg = gated_s[:, col]
            r = lax.rsqrt(jnp.mean(gg * gg, axis=-1, keepdims=True) + EPS)
            yb_ref[:, col] = (gg * r * nw_ref[:, col]).astype(BF16)

    full = lambda shape: pl.BlockSpec(shape, lambda c: tuple(0 for _ in shape))
    return _ride(
        body, (nc,),
        [pl.BlockSpec((ch, si), lambda c: (c, zblk)),
         pl.BlockSpec((ch, si), lambda c: (c, xsblk)),
         pl.BlockSpec((ch, 2 * gn), lambda c: (c, bcblk)),
         pl.BlockSpec((ch, LANES), lambda c: (c, 0)),
         pl.BlockSpec(memory_space=pl.ANY),
         full((SSD_K, si)), full((SSD_K, 2 * gn)), full((1, si)), full((1, 2 * gn)),
         full((1, LANES)), full((1, LANES)), full((1, si)), full((1, si))],
        [pl.BlockSpec((ch, si), lambda c: (c, cw // si)),
         pl.BlockSpec((ch, si), lambda c: (c, 0)),
         pl.BlockSpec((1, npair, STATE, LANES), lambda c: (c, 0, 0, 0)),
         pl.BlockSpec((ch, si), lambda c: (c, 0)), pl.BlockSpec((ch, 2 * gn), lambda c: (c, 0))],
        [jax.ShapeDtypeStruct(y_mix.shape, BF16), jax.ShapeDtypeStruct((t, si), BF16),
         jax.ShapeDtypeStruct((nc, npair, STATE, LANES), F32),
         jax.ShapeDtypeStruct((t, si), BF16), jax.ShapeDtypeStruct((t, 2 * gn), BF16)],
        [pltpu.VMEM((8 + EDGE, si), F32), pltpu.VMEM((8 + EDGE, 2 * gn), F32),
         pltpu.VMEM((SSD_K - 1, ch, ch), BF16),
         pltpu.VMEM((ch, si), F32), pltpu.VMEM((ch, 2 * gn), BF16),
         pltpu.VMEM((npair, STATE, LANES), F32), pltpu.VMEM((ch, si), F32),
         pltpu.VMEM((ch, ch), F32), pltpu.VMEM((ch, LANES), F32), pltpu.VMEM((LANES, ch), F32)],
        [proj, proj, proj, dt_raw, y_mix, kw_xs, kw_bc, b_xs, b_bc, dtb, aneg, dskip, normw], {4: 0}, rider,
        ("arbitrary",), name)


def ssd_bwd(proj, dt_raw, ys, hsave, pre_xs, pre_bc, dy, kw_xs, kw_bc, dtb, aneg, dskip, normw, cw, si, name,
            rider=None):
    t = proj.shape[0]
    ch = min(SCAN_CHUNK, t)
    nc = t // ch
    npair = si // LANES
    ppg = npair // GROUPS
    gn = GROUPS * STATE
    gw = si // GROUPS
    zblk = 3 * cw // si
    xsblk = zblk + 1
    bcblk = (3 * cw + 2 * si) // (2 * gn)

    def body(z_ref, xs_ref, bc_ref, xcx_ref, xcb_ref, dtr_ref, ys_ref, hs_ref, dyb_ref,
             kwx_ref, kwb_ref, dtb_ref, aneg_ref, dsk_ref, nw_ref,
             dz_ref, dxs_ref, dbc_ref, ddt_ref, dkwx_ref, dkwb_ref, dbx_ref, dbb_ref, ddtb_ref, da_ref, ddsk_ref,
             dnw_ref,
             tailx, tailb, shift_s, xs_s, bc_s, dsx_s, dsb_s, dy_s, dxs_s, dbc_s, dh_s, s_s, ds_s,
             cum_s, cumt_s, dccol_s, dcrow_s, ddtcol_s, dcl_s):
        i = pl.program_id(0)

        @pl.when(i == 0)
        def _():
            dh_s[...] = jnp.zeros_like(dh_s)
            tailx[EDGE:EDGE + 8, :] = jnp.zeros((8, si), F32)
            tailb[EDGE:EDGE + 8, :] = jnp.zeros((8, 2 * gn), F32)
            _shift_matrices(shift_s, ch, SSD_K, up=True)
            for r in (dkwx_ref, dkwb_ref, dbx_ref, dbb_ref, ddtb_ref, da_ref, ddsk_ref, dnw_ref):
                r[...] = jnp.zeros_like(r)

        xc = xcx_ref[...].astype(F32)
        sg = _sigmoid(xc)
        xs_s[...] = xc * sg
        dsx_s[...] = sg * (1.0 + xc * (1.0 - sg))
        bcc = xcb_ref[...].astype(F32)
        sgb = _sigmoid(bcc)
        bc_s[...] = (bcc * sgb).astype(BF16)
        dsb_s[...] = sgb * (1.0 + bcc * (1.0 - sgb))

        dt, cum, tril = _ssd_common(dtr_ref, dtb_ref, aneg_ref, cum_s, cumt_s, ch)
        lane = lax.broadcasted_iota(jnp.int32, (ch, LANES), 1)
        lane1 = lax.broadcasted_iota(jnp.int32, (1, LANES), 1)
        low = lane < HEAD_DIM
        low1 = lane1 < HEAD_DIM
        clast = cum_s[pl.ds(ch - 1, 1), :]

        for g in range(GROUPS):
            col = slice(g * gw, (g + 1) * gw)
            ysf = ys_ref[:, col].astype(F32)
            zf = z_ref[:, col].astype(F32)
            sz = _sigmoid(zf)
            silz = zf * sz
            gg = ysf * silz
            r = lax.rsqrt(jnp.mean(gg * gg, axis=-1, keepdims=True) + EPS)
            nrm = gg * r
            dyb = dyb_ref[:, col].astype(F32)
            dnw_ref[:, col] += jnp.sum(dyb * nrm, axis=0, keepdims=True)
            dn = dyb * nw_ref[:, col]
            dgg = r * (dn - nrm * jnp.mean(dn * nrm, axis=-1, keepdims=True))
            dy_s[:, col] = dgg * silz
            dz_ref[:, col] = (dgg * ysf * (sz * (1.0 + zf * (1.0 - sz)))).astype(BF16)

        dccol_s[...] = jnp.zeros_like(dccol_s)
        dcrow_s[...] = jnp.zeros_like(dcrow_s)
        ddtcol_s[...] = jnp.zeros_like(ddtcol_s)
        dcl_s[...] = jnp.zeros_like(dcl_s)
        dbc_s[...] = jnp.zeros_like(dbc_s)

        for p in range(npair):
            g = p // ppg
            col = slice(p * LANES, (p + 1) * LANES)
            bcol = slice(g * STATE, (g + 1) * STATE)
            ccolg = slice(gn + g * STATE, gn + (g + 1) * STATE)
            bg = bc_s[:, bcol]
            cg = bc_s[:, ccolg]
            if p % ppg == 0:
                s_s[...] = _dot_nt(cg, bg)
                ds_s[...] = jnp.zeros_like(ds_s)
            heads = (PAIR * p, PAIR * p + 1)
            masks = (low, jnp.logical_not(low))
            masks1 = (low1, jnp.logical_not(low1))
            ccol = [_head_column(cum, lane, h) for h in heads]
            dcol = [_head_column(dt, lane, h) for h in heads]
            cl = [jnp.sum(jnp.where(lane1 == h, clast, 0.0), axis=-1, keepdims=True) for h in heads]
            cum_px = jnp.where(low, ccol[0], ccol[1])
            dt_px = jnp.where(low, dcol[0], dcol[1])
            cl_px = jnp.where(low1, cl[0], cl[1])
            e_px = jnp.exp(cum_px)
            dec_end = jnp.exp(cl_px - cum_px)
            gdec = jnp.exp(cl_px)
            xs_p = xs_s[:, col]
            xdt = xs_p * dt_px
            dyp = dy_s[:, col]
            hc = hs_ref[0, p]
            hcb = hc.astype(BF16)
            dhn = dh_s[p]
            dhnb = dhn.astype(BF16)

            ddsk_ref[:, col] += jnp.sum(dyp * xs_p, axis=0, keepdims=True)
            dxs_acc = dsk_ref[:, col] * dyp
            dye = dyp * e_px
            dyeb = dye.astype(BF16)
            dbc_s[:, ccolg] += _dot_nt(dyeb, hcb)
            dcum_lane = dye * _dot(cg, hcb)
            dh_from_y = _dot_tn(cg, dyeb)
            xd = xdt * dec_end
            dxd = _dot(bg, dhnb)
            dbc_s[:, bcol] += _dot_nt(xd.astype(BF16), dhnb)
            dxdt = dxd * dec_end
            t1 = dxd * xd
            dcum_lane = dcum_lane - t1
            dcl_lane = jnp.sum(t1, axis=0, keepdims=True) + jnp.sum(dhn * hc, axis=0, keepdims=True) * gdec
            dh_s[p] = gdec * dhn + dh_from_y
            xdtb = xdt.astype(BF16)
            for hi, h in enumerate(heads):
                dym = jnp.where(masks[hi], dyp, 0.0).astype(BF16)
                dw = _dot_nt(dym, xdtb)
                dec = jnp.exp(jnp.where(tril, ccol[hi] - cumt_s[pl.ds(h, 1), :], -jnp.inf))
                wm = s_s[...] * dec
                dxdt = dxdt + _dot_tn(wm.astype(BF16), dym)
                ds_s[...] += dw * dec
                gm = dw * wm
                rowsum = jnp.sum(gm, axis=-1, keepdims=True)
                lanesum = jnp.sum(jnp.where(masks[hi], dcum_lane, 0.0), axis=-1, keepdims=True)
                dccol_s[...] += jnp.where(lane == h, rowsum + lanesum, 0.0)
                dcrow_s[pl.ds(h, 1), :] = jnp.sum(gm, axis=0, keepdims=True)
                dcl_h = jnp.sum(jnp.where(masks1[hi], dcl_lane, 0.0), axis=-1, keepdims=True)
                dcl_s[...] += jnp.where(lane1 == h, dcl_h, 0.0)
            ddt_lane = dxdt * xs_p
            for hi, h in enumerate(heads):
                s = jnp.sum(jnp.where(masks[hi], ddt_lane, 0.0), axis=-1, keepdims=True)
                ddtcol_s[...] += jnp.where(lane == h, s, 0.0)
            dxs_s[:, col] = dxs_acc + dxdt * dt_px
            if p % ppg == ppg - 1:
                dsb = ds_s[...].astype(BF16)
                dbc_s[:, ccolg] += _dot(dsb, bg)
                dbc_s[:, bcol] += _dot_tn(dsb, cg)

        rowi = lax.broadcasted_iota(jnp.int32, (ch, LANES), 0)
        dcum = dccol_s[...] - dcrow_s[...].T + jnp.where(rowi == ch - 1, dcl_s[...], 0.0)
        ri = lax.broadcasted_iota(jnp.int32, (ch, ch), 0)
        ci = lax.broadcasted_iota(jnp.int32, (ch, ch), 1)
        dal = _dot_exact((ri <= ci).astype(F32), dcum)
        ddt = dal * aneg_ref[...] + ddtcol_s[...]
        da_ref[...] += jnp.sum(dal * dt, axis=0, keepdims=True)
        ddtr = ddt * _sigmoid(dtr_ref[...] + dtb_ref[...])
        ddt_ref[...] = ddtr
        ddtb_ref[...] += jnp.sum(ddtr, axis=0, keepdims=True)

        for (dpost, dsl, tail, raw_ref, kw_ref, dkw_ref, db_ref, out_ref) in (
                (dxs_s, dsx_s, tailx, xs_ref, kwx_ref, dkwx_ref, dbx_ref, dxs_ref),
                (dbc_s, dsb_s, tailb, bc_ref, kwb_ref, dkwb_ref, dbb_ref, dbc_ref)):
            dxc = dpost[...] * dsl[...]
            dxcb = dxc.astype(BF16)
            raw = raw_ref[...].astype(F32)
            raw_end = raw_ref[ch - EDGE:ch, :].astype(F32)
            tail[0:EDGE, :] = dxcb[ch - EDGE:ch].astype(F32)
            db_ref[...] += jnp.sum(dxc, axis=0, keepdims=True)
            draw = kw_ref[pl.ds(SSD_K - 1, 1), :] * dxc
            dkw_ref[pl.ds(SSD_K - 1, 1), :] += jnp.sum(dxc * raw, axis=0, keepdims=True)
            fix = jnp.zeros((EDGE, dxc.shape[1]), F32)
            for k in range(SSD_K - 1):
                moved = _dot(shift_s[k], dxcb)
                miss = tail[pl.ds(SSD_K - 1 - k, EDGE), :] - moved[ch - EDGE:ch]
                draw = draw + kw_ref[pl.ds(k, 1), :] * moved
                fix = fix + kw_ref[pl.ds(k, 1), :] * miss
                dkw_ref[pl.ds(k, 1), :] += (jnp.sum(moved * raw, axis=0, keepdims=True)
                                            + jnp.sum(miss * raw_end, axis=0, keepdims=True))
            out_ref[...] = draw.astype(BF16)
            out_ref[ch - EDGE:ch, :] = (draw[ch - EDGE:ch] + fix).astype(BF16)
            tail[EDGE:EDGE + 8, :] = dxcb[0:EDGE].astype(F32)[0:8]

    full = lambda shape: pl.BlockSpec(shape, lambda i: tuple(0 for _ in shape))
    rev = lambda blk: (lambda i: (nc - 1 - i, blk))
    small_in = [(SSD_K, si), (SSD_K, 2 * gn), (1, LANES), (1, LANES), (1, si), (1, si)]
    small = [(SSD_K, si), (SSD_K, 2 * gn), (1, si), (1, 2 * gn), (1, LANES), (1, LANES), (1, si), (1, si)]
    return _ride(
        body, (nc,),
        [pl.BlockSpec((ch, si), rev(zblk)), pl.BlockSpec((ch, si), rev(xsblk)),
         pl.BlockSpec((ch, 2 * gn), rev(bcblk)),
         pl.BlockSpec((ch, si), rev(0)), pl.BlockSpec((ch, 2 * gn), rev(0)),
         pl.BlockSpec((ch, LANES), rev(0)), pl.BlockSpec((ch, si), rev(0)),
         pl.BlockSpec((1, npair, STATE, LANES), lambda i: (nc - 1 - i, 0, 0, 0)),
         pl.BlockSpec((ch, si), rev(cw // si))] + [full(s) for s in small_in],
        [pl.BlockSpec((ch, si), rev(0)), pl.BlockSpec((ch, si), rev(0)),
         pl.BlockSpec((ch, 2 * gn), rev(0)), pl.BlockSpec((ch, LANES), rev(0))] + [full(s) for s in small],
        [jax.ShapeDtypeStruct((t, si), BF16), jax.ShapeDtypeStruct((t, si), BF16),
         jax.ShapeDtypeStruct((t, 2 * gn), BF16), jax.ShapeDtypeStruct((t, LANES), F32)]
        + [jax.ShapeDtypeStruct(s, F32) for s in small],
        [pltpu.VMEM((EDGE + 8, si), F32), pltpu.VMEM((EDGE + 8, 2 * gn), F32),
         pltpu.VMEM((SSD_K - 1, ch, ch), BF16),
         pltpu.VMEM((ch, si), F32), pltpu.VMEM((ch, 2 * gn), BF16),
         pltpu.VMEM((ch, si), F32), pltpu.VMEM((ch, 2 * gn), F32),
         pltpu.VMEM((ch, si), F32), pltpu.VMEM((ch, si), F32), pltpu.VMEM((ch, 2 * gn), F32),
         pltpu.VMEM((npair, STATE, LANES), F32),
         pltpu.VMEM((ch, ch), F32), pltpu.VMEM((ch, ch), F32),
         pltpu.VMEM((ch, LANES), F32), pltpu.VMEM((LANES, ch), F32),
         pltpu.VMEM((ch, LANES), F32), pltpu.VMEM((LANES, ch), F32),
         pltpu.VMEM((ch, LANES), F32), pltpu.VMEM((1, LANES), F32)],
        [proj, proj, proj, pre_xs, pre_bc, dt_raw, ys, hsave, dy, kw_xs, kw_bc, dtb, aneg, dskip, normw],
        {}, rider, ("arbitrary",), name)


def final_loss(x, nw, tgt, name):
    t, d = x.shape
    tm = _tile(t, 512, 8)

    def body(x_ref, nw_ref, t_ref, dx_ref, dnw_ref, ls_ref):
        xf = x_ref[...]
        r = lax.rsqrt(jnp.mean(xf * xf, axis=-1, keepdims=True) + EPS)
        nx = xf * r
        e = nx * nw_ref[...] - t_ref[...]
        dyv = e * (1.0 / d)
        dn = dyv * nw_ref[...]
        dx_ref[...] = r * (dn - nx * jnp.mean(dn * nx, axis=-1, keepdims=True))

        @pl.when(pl.program_id(0) == 0)
        def _():
            dnw_ref[...] = jnp.zeros_like(dnw_ref)
            ls_ref[...] = jnp.zeros_like(ls_ref)

        dnw_ref[...] += jnp.sum(dyv * nx, axis=0, keepdims=True)
        ls_ref[...] += jnp.sum(e * e, axis=0, keepdims=True) * (0.5 / d)

    return pl.pallas_call(
        body, name=name, grid=(t // tm,),
        in_specs=[pl.BlockSpec((tm, d), lambda i: (i, 0)), pl.BlockSpec((1, d), lambda i: (0, 0)),
                  pl.BlockSpec((tm, d), lambda i: (i, 0))],
        out_specs=[pl.BlockSpec((tm, d), lambda i: (i, 0)), pl.BlockSpec((1, d), lambda i: (0, 0)),
                   pl.BlockSpec((1, d), lambda i: (0, 0))],
        out_shape=[jax.ShapeDtypeStruct((t, d), F32), jax.ShapeDtypeStruct((1, d), F32),
                   jax.ShapeDtypeStruct((1, d), F32)],
        compiler_params=_params(("arbitrary",)),
    )(x, nw, tgt)


def _rows3(a):
    if a.ndim == 1:
        return a.reshape(1, 1, a.shape[0])
    if a.ndim == 2:
        return a.reshape(1, *a.shape)
    return a.reshape(-1, a.shape[-2], a.shape[-1])


def adamw(w, g, m, v, name):
    shape = w.shape
    views = [_rows3(a) for a in (w, g, m, v)]
    b, r, c = views[0].shape
    tr = _tile(r, 256, 16) if r % 16 == 0 else r

    def body(w_ref, g_ref, m_ref, v_ref, d_ref, nm_ref, nv_ref):
        g = g_ref[...]
        m = ADAM_B1 * m_ref[...] + (1.0 - ADAM_B1) * g
        v = ADAM_B2 * v_ref[...] + (1.0 - ADAM_B2) * (g * g)
        m_hat = m / (1.0 - ADAM_B1 ** ADAM_STEP)
        v_hat = v / (1.0 - ADAM_B2 ** ADAM_STEP)
        d_ref[...] = -ADAM_LR * (m_hat / (jnp.sqrt(v_hat) + ADAM_EPS) + ADAM_WD * w_ref[...])
        nm_ref[...] = m
        nv_ref[...] = v

    spec = pl.BlockSpec((1, tr, c), lambda i, j: (i, j, 0))
    outs = pl.pallas_call(
        body, name=name, grid=(b, r // tr), in_specs=[spec] * 4, out_specs=[spec] * 3,
        out_shape=[jax.ShapeDtypeStruct((b, r, c), F32)] * 3,
        compiler_params=_params(("parallel", "parallel")),
    )(*views)
    return [o.reshape(shape) for o in outs]


def adamw_halves(w, g_mine, g_theirs, m, v, name):
    nl, r, c = w.shape
    r2 = r // 2
    tr = _tile(r2, 256, 16)
    nb = r2 // tr

    def body(w_ref, gm_ref, gt_ref, m_ref, v_ref, g_ref, d_ref, nm_ref, nv_ref):
        mine = (pl.program_id(1) // nb) == lax.axis_index("c")
        g = jnp.where(mine, gm_ref[...], gt_ref[...])
        m = ADAM_B1 * m_ref[...] + (1.0 - ADAM_B1) * g
        v = ADAM_B2 * v_ref[...] + (1.0 - ADAM_B2) * (g * g)
        m_hat = m / (1.0 - ADAM_B1 ** ADAM_STEP)
        v_hat = v / (1.0 - ADAM_B2 ** ADAM_STEP)
        g_ref[...] = g
        d_ref[...] = -ADAM_LR * (m_hat / (jnp.sqrt(v_hat) + ADAM_EPS) + ADAM_WD * w_ref[...])
        nm_ref[...] = m
        nv_ref[...] = v

    whole = pl.BlockSpec((1, tr, c), lambda l, i: (l, i, 0))
    half = pl.BlockSpec((1, tr, c), lambda l, i: (l, i % nb, 0))
    return pl.pallas_call(
        body, name=name, grid=(nl, 2 * nb), in_specs=[whole, half, half, whole, whole], out_specs=[whole] * 4,
        out_shape=[jax.ShapeDtypeStruct((nl, r, c), F32)] * 4,
        compiler_params=_params(("parallel", "parallel")),
    )(w, g_mine, g_theirs, m, v)


def _coords():
    return lax.axis_index("x"), lax.axis_index("y"), lax.axis_index("c")


def _ici_peers(x, y):
    chips = [(1 - x, y), (x, 1 - y), (1 - x, 1 - y)]
    return chips, [2 * cx + cy for cx, cy in chips]


def _place(ref, how, chip, layers, per):
    if how == "lead":
        return ref.at[chip, layers]
    start = pl.multiple_of(chip * per, per)
    if how == "rows":
        return ref.at[layers, pl.ds(start, per), :]
    return ref.at[layers, :, pl.ds(start, per)]


def gather_weights(shards, hows, name):
    na = len(shards)
    out_shape = []
    for s, how in zip(shards, hows):
        assert s.shape[0] % 2 == 0
        if how == "lead":
            shp = (N_CHIPS, *s.shape)
        elif how == "rows":
            shp = (s.shape[0], N_CHIPS * s.shape[1], s.shape[2])
        else:
            shp = (s.shape[0], s.shape[1], N_CHIPS * s.shape[2])
        out_shape.append(jax.ShapeDtypeStruct(shp, s.dtype))

    def body(*refs):
        ins = refs[:na]
        outs = refs[na:2 * na]
        send_sems, recv_sems = refs[2 * na:]
        x, y, c = _coords()
        me = 2 * x + y
        chips, chip_ids = _ici_peers(x, y)
        sibling = (x, y, 1 - c)

        def dst(a, chip, layers):
            per = {"lead": 0, "rows": ins[a].shape[1], "cols": ins[a].shape[-1]}[hows[a]]
            return _place(outs[a], hows[a], chip, layers, per)

        def copy(a, k, src, dst_ref, to):
            return pltpu.make_async_remote_copy(
                src_ref=src, dst_ref=dst_ref, send_sem=send_sems.at[7 * a + k], recv_sem=recv_sems.at[7 * a + k],
                device_id=to, device_id_type=MESH)

        started = []
        halves = []
        for a in range(na):
            nl = ins[a].shape[0]
            hl = nl // 2
            mine = pl.ds(c * hl, hl)
            theirs = pl.ds((1 - c) * hl, hl)
            halves.append((mine, theirs))
            for k in range(3):
                cp = copy(a, k, ins[a].at[mine], dst(a, me, mine), (*chips[k], c))
                cp.start()
                started.append(cp)
            own = copy(a, 6, ins[a], dst(a, me, pl.ds(0, nl)), sibling)
            own.start()
            started.append(own)
        for a in range(na):
            mine, _ = halves[a]
            for k in range(3):
                landed = dst(a, chip_ids[k], mine)
                copy(a, k, landed, landed, (*chips[k], c)).wait_recv()
                fw = copy(a, 3 + k, landed, landed, sibling)
                fw.start()
                started.append(fw)
        for a in range(na):
            _, theirs = halves[a]
            for k in range(3):
                got = dst(a, chip_ids[k], theirs)
                copy(a, 3 + k, got, got, sibling).wait_recv()
            whole = dst(a, me, pl.ds(0, ins[a].shape[0]))
            copy(a, 6, whole, whole, sibling).wait_recv()
        for cp in started:
            cp.wait_send()

    return pl.pallas_call(
        body, name=name, in_specs=_any_specs(na), out_specs=_any_specs(na), out_shape=out_shape,
        scratch_shapes=[pltpu.SemaphoreType.DMA((7 * na,)), pltpu.SemaphoreType.DMA((7 * na,))],
        compiler_params=pltpu.CompilerParams(has_side_effects=True),
    )(*shards)


def _remote(src, dst, send_sems, recv_sems, k, to):
    return pltpu.make_async_remote_copy(src_ref=src, dst_ref=dst, send_sem=send_sems.at[k], recv_sem=recv_sems.at[k],
                                        device_id=to, device_id_type=MESH)


LAYER_HOW = ("lead", "rows", "cols", "rows")


def _layer_place(ref, how, chip, shard_shape, start, size):
    r, c = shard_shape
    if how == "lead":
        return ref.at[chip, :, pl.ds(start, size), :]
    if how == "rows":
        return ref.at[:, pl.ds(pl.multiple_of(chip * r + start, HALO), size), :]
    return ref.at[:, pl.ds(start, size), pl.ds(pl.multiple_of(chip * c, LANES), c)]


def weight_rider_ici(shards, hows, layer):
    shapes = [tuple(s.shape[1:]) for s in shards]
    out_shapes = []
    for (r, c), how, s in zip(shapes, hows, shards):
        shp = {"lead": (N_CHIPS, 1, r, c), "rows": (1, N_CHIPS * r, c), "cols": (1, r, N_CHIPS * c)}[how]
        out_shapes.append(jax.ShapeDtypeStruct(shp, s.dtype))

    def copies(ins, outs, send_sems, recv_sems):
        x, y, c = _coords()
        me = 2 * x + y
        chips, chip_ids = _ici_peers(x, y)
        sibling = (x, y, 1 - c)
        pairs = []
        for a, (shape, how) in enumerate(zip(shapes, hows)):
            half = shape[0] // 2
            mine = pl.multiple_of(c * half, HALO)
            src = ins[a].at[pl.ds(layer, 1)]
            for k in range(3):
                to = (*chips[k], c)
                land = _layer_place(outs[a], how, chip_ids[k], shape, mine, half)
                pairs.append((_remote(src.at[:, pl.ds(mine, half), :], _layer_place(outs[a], how, me, shape, mine, half),
                                      send_sems, recv_sems, 4 * a + k, to),
                              _remote(land, land, send_sems, recv_sems, 4 * a + k, to)))
            whole = _layer_place(outs[a], how, me, shape, 0, shape[0])
            pairs.append((_remote(src, whole, send_sems, recv_sems, 4 * a + 3, sibling),
                          _remote(whole, whole, send_sems, recv_sems, 4 * a + 3, sibling)))
        return pairs

    return Rider(list(shards), out_shapes, {}, 4 * len(shards), copies)


def weight_rider_d2d(bufs, shapes, hows):
    def copies(ins, outs, send_sems, recv_sems):
        x, y, c = _coords()
        _, chip_ids = _ici_peers(x, y)
        sibling = (x, y, 1 - c)
        pairs = []
        for a, (shape, how) in enumerate(zip(shapes, hows)):
            half = shape[0] // 2
            mine = pl.multiple_of(c * half, HALO)
            theirs = pl.multiple_of((1 - c) * half, HALO)
            for k in range(3):
                land = _layer_place(outs[a], how, chip_ids[k], shape, theirs, half)
                pairs.append((_remote(_layer_place(ins[a], how, chip_ids[k], shape, mine, half),
                                      _layer_place(outs[a], how, chip_ids[k], shape, mine, half),
                                      send_sems, recv_sems, 3 * a + k, sibling),
                              _remote(land, land, send_sems, recv_sems, 3 * a + k, sibling)))
        return pairs

    return Rider(list(bufs), [jax.ShapeDtypeStruct(b.shape, b.dtype) for b in bufs],
                 {a: a for a in range(len(bufs))}, 3 * len(bufs), copies)


def grads_rider_sibling(arrs):
    def copies(ins, outs, send_sems, recv_sems):
        x, y, c = _coords()
        sibling = (x, y, 1 - c)
        pairs = []
        for a in range(len(arrs)):
            r2 = ins[a].shape[1] // 2
            src = ins[a].at[:, pl.ds(pl.multiple_of((1 - c) * r2, 8), r2), :]
            pairs.append((_remote(src, outs[a], send_sems, recv_sems, a, sibling),
                          _remote(outs[a], outs[a], send_sems, recv_sems, a, sibling)))
        return pairs

    return Rider(list(arrs), [jax.ShapeDtypeStruct((a.shape[0], a.shape[1] // 2, a.shape[2]), a.dtype) for a in arrs],
                 {}, len(arrs), copies)


def chip_sum(g, recv, name):
    nch, r, c = g.shape
    r2 = r // 2
    tr = _tile(r2, 256, 16)
    nb = r2 // tr

    def body(g0_ref, g1_ref, r_ref, o32_ref, o16_ref):
        s = jnp.where(lax.axis_index("c") == 0, g0_ref[...], g1_ref[...]) + r_ref[...]
        o32_ref[...] = s
        o16_ref[...] = s.astype(BF16)

    here = pl.BlockSpec((1, tr, c), lambda i, j: (i, j, 0))
    return pl.pallas_call(
        body, name=name, grid=(nch, nb),
        in_specs=[here, pl.BlockSpec((1, tr, c), lambda i, j: (i, nb + j, 0)), here],
        out_specs=[here, here],
        out_shape=[jax.ShapeDtypeStruct((nch, r2, c), F32), jax.ShapeDtypeStruct((nch, r2, c), BF16)],
        compiler_params=_params(("parallel", "parallel")),
    )(g, g, recv)


def grads_rider_chips(arrs):
    def copies(ins, outs, send_sems, recv_sems):
        x, y, c = _coords()
        chips, chip_ids = _ici_peers(x, y)
        pairs = []
        for a in range(len(arrs)):
            for k in range(3):
                to = (*chips[k], c)
                pairs.append((_remote(ins[a].at[chip_ids[k]], outs[a].at[k], send_sems, recv_sems, 3 * a + k, to),
                              _remote(outs[a].at[k], outs[a].at[k], send_sems, recv_sems, 3 * a + k, to)))
        return pairs

    return Rider(list(arrs), [jax.ShapeDtypeStruct((3, *a.shape[1:]), a.dtype) for a in arrs], {}, 3 * len(arrs),
                 copies)


def grad_sum(p32, recv, layer, nl, buf, name):
    _, r2, c = p32.shape
    tr = _tile(r2, 256, 16)
    nb = r2 // tr

    def body(p0_ref, p1_ref, p2_ref, p3_ref, r0_ref, r1_ref, r2_ref, *rest):
        o_ref = rest[-1]
        x, y, _ = _coords()
        chip = 2 * x + y
        own = jnp.where(chip == 0, p0_ref[...], jnp.where(chip == 1, p1_ref[...],
                                                        jnp.where(chip == 2, p2_ref[...], p3_ref[...])))
        o_ref[...] = own + r0_ref[...].astype(F32) + r1_ref[...].astype(F32) + r2_ref[...].astype(F32)

    slot = lambda k: pl.BlockSpec((1, tr, c), lambda j: (k, j, 0))
    in_specs = [slot(k) for k in range(N_CHIPS)] + [slot(k) for k in range(3)]
    args = [p32] * N_CHIPS + [recv] * 3
    aliases = {}
    if buf is not None:
        in_specs.append(pl.BlockSpec(memory_space=pl.ANY))
        args.append(buf)
        aliases = {len(args) - 1: 0}
    return pl.pallas_call(
        body, name=name, grid=(nb,), in_specs=in_specs,
        out_specs=pl.BlockSpec((1, tr, c), lambda j: (layer, j, 0)),
        out_shape=jax.ShapeDtypeStruct((nl, r2, c), F32),
        input_output_aliases=aliases,
        compiler_params=_params(("parallel",)),
    )(*args)


def grads_rider_exchange(bufs):
    def copies(ins, outs, send_sems, recv_sems):
        x, y, c = _coords()
        sibling = (x, y, 1 - c)
        return [(_remote(ins[a], outs[a], send_sems, recv_sems, a, sibling),
                 _remote(outs[a], outs[a], send_sems, recv_sems, a, sibling)) for a in range(len(bufs))]

    return Rider(list(bufs), [jax.ShapeDtypeStruct(b.shape, b.dtype) for b in bufs], {}, len(bufs), copies)


def allreduce_small(buf, name):
    r, cdim = buf.shape

    def body(x_ref, o_ref, gath, send_sems, recv_sems):
        x, y, c = _coords()
        me, sibling = (x, y, c), (x, y, 1 - c)
        chips, _ = _ici_peers(x, y)

        def slot(px, py, pc):
            return gath.at[4 * px + 2 * py + pc]

        def copy(k, block, to, src=None):
            return pltpu.make_async_remote_copy(
                src_ref=slot(*block) if src is None else src, dst_ref=slot(*block),
                send_sem=send_sems.at[k], recv_sem=recv_sems.at[k], device_id=to, device_id_type=MESH)

        gath[4 * x + 2 * y + c] = x_ref[...]
        first = [copy(0, me, sibling, src=x_ref)]
        first += [copy(1 + j, me, (*chip, c), src=x_ref) for j, chip in enumerate(chips)]
        for cp in first:
            cp.start()
        passed = [copy(4 + j, (*chip, c), sibling) for j, chip in enumerate(chips)]
        for j, chip in enumerate(chips):
            copy(1 + j, (*chip, c), me).wait_recv()
            passed[j].start()
        copy(0, sibling, me).wait_recv()
        for j, chip in enumerate(chips):
            copy(4 + j, (*chip, 1 - c), me).wait_recv()
        for cp in first + passed:
            cp.wait_send()
        acc = gath[0]
        for d in range(1, 8):
            acc = acc + gath[d]
        o_ref[...] = acc

    return pl.pallas_call(
        body, name=name,
        in_specs=[pl.BlockSpec(memory_space=pltpu.VMEM)], out_specs=pl.BlockSpec(memory_space=pltpu.VMEM),
        out_shape=jax.ShapeDtypeStruct((r, cdim), F32),
        scratch_shapes=[pltpu.VMEM((8, r, cdim), F32), pltpu.SemaphoreType.DMA((7,)), pltpu.SemaphoreType.DMA((7,))],
        compiler_params=pltpu.CompilerParams(has_side_effects=True),
    )(buf)


def _expand_heads(v):
    return jnp.repeat(v.astype(F32), HEAD_DIM).reshape(1, -1)


def _pad_lanes(v):
    return jnp.pad(v.astype(F32), (0, LANES - v.shape[0])).reshape(1, LANES)


def local_step(x, tgt, p, comm, cols):
    nl = p["norm_mix_w"].shape[0]
    d = x.shape[1]
    cw = p["short_conv_w"].shape[2]
    si = p["ssd_norm_w"].shape[1]
    ff, npad = comm.ff, comm.npad
    nh = si // HEAD_DIM
    gn = GROUPS * STATE
    dt_off = 3 * cw + si + si + 2 * gn
    assert cols == dt_off + nh and nh <= LANES and dt_off % LANES == 0 and npad == dt_off + LANES
    pieces = [(0, cw), (cw, cw), (2 * cw, cw), (3 * cw, si), (3 * cw + si, si), (3 * cw + 2 * si, 2 * gn),
              (dt_off, LANES)]

    saved = []
    for l in range(nl):
        nw1 = p["norm_mix_w"][l].reshape(1, d)
        nw2 = p["norm_mlp_w"][l].reshape(1, d)
        kw3 = p["short_conv_w"][l]
        kwx, kwb = p["ssd_conv_w"][l][:, :si], p["ssd_conv_w"][l][:, si:]
        bx, bb = p["ssd_conv_b"][l][:si].reshape(1, si), p["ssd_conv_b"][l][si:].reshape(1, 2 * gn)
        dtb = _pad_lanes(p["dt_bias"][l])
        aneg = _pad_lanes(-jnp.exp(p["a_log"][l]))
        dsk = _expand_heads(p["d_skip"][l])
        snw = p["ssd_norm_w"][l].reshape(1, si)
        ssd_args = (kwx, kwb, bx, bb, dtb, aneg, dsk, snw)

        w_in = comm.weight(l, "w_in")
        (proj, h), sent = norm_matmul(x, nw1, w_in, 0, dt_off, 0, BF16, "in_proj", rider=comm.rider("in_proj", l))
        comm.done("in_proj", l, sent)
        (dt_raw,), _ = norm_matmul(x, nw1, w_in, 0, LANES, dt_off // LANES, F32, "dt_proj", emit_h=False)
        y_mix = conv_mixer_fwd(proj, kw3, cw, cw + si, "conv_mixer_fwd")
        (y_mix, *ssd_saved), sent = ssd_fwd(proj, dt_raw, y_mix, *ssd_args, cw, si, "ssd_fwd",
                                            rider=comm.rider("ssd_fwd", l))
        comm.done("ssd_fwd", l, sent)
        x2, _ = matmul(y_mix, comm.weight(l, "w_out"), 0, False, d, F32, "out_proj", residual=x)
        (up, h2), sent = norm_matmul(x2, nw2, comm.weight(l, "w_up"), 0, ff, 0, BF16, "up_proj",
                                     rider=comm.rider("up_proj", l))
        comm.done("up_proj", l, sent)
        x3, sent = matmul(up, comm.weight(l, "w_down"), 0, False, d, F32, "down_proj", lhs_fn=_relu2, residual=x2,
                          rider=comm.rider("down_proj", l))
        comm.done("down_proj", l, sent)
        saved.append((x, h, proj, dt_raw, y_mix, ssd_saved, x2, h2, up, nw1, nw2, kw3, ssd_args))
        x = x3

    dx, dwf, lvec = final_loss(x, p["final_norm_w"].reshape(1, d), tgt, "final_loss")
    loss = jnp.sum(lvec)

    names = ("norm_mix_w", "short_conv_w", "ssd_conv_w", "ssd_conv_b", "dt_bias", "a_log", "d_skip",
             "ssd_norm_w", "norm_mlp_w")
    grads = {k: [None] * nl for k in names}
    for l in reversed(range(nl)):
        x0, h, proj, dt_raw, y_mix, ssd_saved, x2, h2, up, nw1, nw2, kw3, ssd_args = saved[l]
        kwx, kwb, _, _, dtb, aneg, dsk, snw = ssd_args
        dup, sent = matmul(dx, comm.weight(l, "w_down"), 0, True, ff, BF16, "down_bwd", relu_gate=up,
                           rider=comm.rider("down_bwd", l))
        comm.done("down_bwd", l, sent)
        comm.take_gradient(l, "w_down", matmul_tn(up, dx, "down_wgrad", a_fn=_relu2))
        (dx2, dnw2), _ = matmul_normbwd([dup], [(0, ff)], comm.weight(l, "w_up"), 0, x2, nw2, dx, "up_bwd")
        comm.take_gradient(l, "w_up", matmul_tn(h2, dup, "up_wgrad", by_chip=True))
        grads["norm_mlp_w"][l] = dnw2.reshape(d)
        dy, sent = matmul(dx2, comm.weight(l, "w_out"), 0, True, cw + si, BF16, "out_bwd",
                          rider=comm.rider("out_bwd", l))
        comm.done("out_bwd", l, sent)
        comm.take_gradient(l, "w_out", matmul_tn(y_mix, dx2, "out_wgrad"))
        dub, duc, duh, dkw3 = conv_mixer_bwd(proj, dy, kw3, cw, "conv_mixer_bwd")
        (dz, dxs, dbc, ddt, dkwx, dkwb, dbx, dbb, ddtb, da, ddsk, dsnw), sent = ssd_bwd(
            proj, dt_raw, *ssd_saved, dy, kwx, kwb, dtb, aneg, dsk, snw, cw, si, "ssd_bwd",
            rider=comm.rider("ssd_bwd", l))
        comm.done("ssd_bwd", l, sent)
        dpieces = [dub, duc, duh, dz, dxs, dbc, ddt]
        (dxl, dnw1), sent = matmul_normbwd(dpieces, pieces, comm.weight(l, "w_in"), 0, x0, nw1, dx2, "in_bwd",
                                           rider=comm.rider("in_bwd", l))
        comm.done("in_bwd", l, sent)
        comm.take_gradient(l, "w_in", split_to_chips(
            [matmul_tn(h, dp, "in_wgrad_%d" % i) for i, dp in enumerate(dpieces)], cols, "in_wgrad_split"))
        grads["norm_mix_w"][l] = dnw1.reshape(d)
        grads["short_conv_w"][l] = dkw3
        grads["ssd_conv_w"][l] = jnp.concatenate([dkwx, dkwb], axis=1)
        grads["ssd_conv_b"][l] = jnp.concatenate([dbx, dbb], axis=1).reshape(-1)
        grads["dt_bias"][l] = ddtb[0, :nh]
        grads["a_log"][l] = da[0, :nh] * aneg[0, :nh]
        grads["d_skip"][l] = jnp.sum(ddsk.reshape(nh, HEAD_DIM), axis=1)
        grads["ssd_norm_w"][l] = dsnw.reshape(si)
        dx = dxl

    grads = {k: jnp.stack(v) for k, v in grads.items()}
    grads["final_norm_w"] = dwf.reshape(d)
    return loss, dx, grads


BIG = ("w_in", "w_out", "w_up", "w_down")
SMALL_SHARDED = ("short_conv_w", "ssd_conv_w")
SMALL_REPL = ("norm_mix_w", "ssd_conv_b", "dt_bias", "a_log", "d_skip", "ssd_norm_w", "norm_mlp_w", "final_norm_w")
WEIGHTS = ("norm_mix_w", "w_in", "short_conv_w", "ssd_conv_w", "ssd_conv_b", "dt_bias", "a_log", "d_skip",
           "ssd_norm_w", "w_out", "norm_mlp_w", "w_up", "w_down", "final_norm_w")
SMALL_COLS = 1024


def _pack_small(named):
    flat = jnp.concatenate([v.reshape(-1).astype(F32) for v in named])
    n = flat.shape[0]
    rows = -(-n // SMALL_COLS)
    rows = -(-rows // 8) * 8
    return jnp.pad(flat, (0, rows * SMALL_COLS - n)).reshape(rows, SMALL_COLS)


def _unpack_small(buf, like):
    flat = buf.reshape(-1)
    out, off = [], 0
    for v in like:
        out.append(flat[off:off + v.size].reshape(v.shape))
        off += v.size
    return out


class ChipComm:
    IO = ("w_in", "w_out")
    MLP = ("w_up", "w_down")

    def __init__(self, shards, nl, npad):
        self.shards, self.nl, self.npad = shards, nl, npad
        self.how = dict(zip(BIG, LAYER_HOW))
        self.ff = N_CHIPS * shards["w_up"].shape[2]
        self.w = {}
        self.landed = {}
        self.grad = {}
        self.sums = {}
        self.bufs = {k: None for k in BIG}
        first = run_rider(self._ici(self.IO, 0), "gather_first_ici")
        self._gathered(self.IO, 0, run_rider(self._d2d(self.IO, first), "gather_first_d2d"))

    def _ici(self, group, l):
        return weight_rider_ici([self.shards[k] for k in group], [self.how[k] for k in group], l)

    def _d2d(self, group, landed):
        return weight_rider_d2d(landed, [tuple(self.shards[k].shape[1:]) for k in group],
                                [self.how[k] for k in group])

    def _gathered(self, group, l, arrays):
        for k, g in zip(group, arrays):
            self.w[(l, k)] = join_from_chips(g, self.npad, "w_in_join") if k == "w_in" else g

    def _to_sibling(self, group, l):
        return grads_rider_sibling([self.grad[(l, k)] for k in group])

    def _summed(self, group, l, from_sibling):
        self.sums[group] = (l, [chip_sum(self.grad.pop((l, k)), r, "chip_sum") for k, r in zip(group, from_sibling)])

    def _to_chips(self, group):
        return grads_rider_chips([s[1] for s in self.sums[group][1]])

    def _reduced(self, group, from_chips):
        l, sums = self.sums.pop(group)
        for k, s, r in zip(group, sums, from_chips):
            self.bufs[k] = grad_sum(s[0], r, l, self.nl, self.bufs[k], "grad_sum")

    def weight(self, l, name):
        return self.w[(l, name)]

    def take_gradient(self, l, name, g):
        self.grad[(l, name)] = g if g.ndim == 3 else g.reshape(N_CHIPS, g.shape[0] // N_CHIPS, g.shape[1])

    def rider(self, point, l):
        more = l + 1 < self.nl
        if point == "in_proj":
            return self._ici(self.MLP, l)
        if point == "ssd_fwd":
            return self._d2d(self.MLP, self.landed[self.MLP])
        if point == "up_proj":
            return self._ici(self.IO, l + 1) if more else None
        if point == "down_proj":
            return self._d2d(self.IO, self.landed[self.IO]) if more else None
        if point == "down_bwd":
            return self._to_sibling(self.IO, l + 1) if more else None
        if point == "out_bwd":
            return self._to_sibling(self.MLP, l)
        if point == "ssd_bwd":
            return self._to_chips(self.IO) if more else None
        return self._to_chips(self.MLP)

    def done(self, point, l, results):
        if not results:
            return
        if point in ("in_proj", "up_proj"):
            self.landed[self.MLP if point == "in_proj" else self.IO] = results
        elif point == "ssd_fwd":
            self._gathered(self.MLP, l, results)
        elif point == "down_proj":
            self._gathered(self.IO, l + 1, results)
        elif point == "down_bwd":
            self._summed(self.IO, l + 1, results)
        elif point == "out_bwd":
            self._summed(self.MLP, l, results)
        elif point == "ssd_bwd":
            self._reduced(self.IO, results)
        else:
            self._reduced(self.MLP, results)

    def finish(self):
        self._summed(self.IO, 0, run_rider(self._to_sibling(self.IO, 0), "grads_to_sibling"))
        self._reduced(self.IO, run_rider(self._to_chips(self.IO), "grads_to_chips"))
        mine = [self.bufs[k] for k in BIG]
        theirs = run_rider(grads_rider_exchange(mine), "grads_exchange")
        return dict(zip(BIG, zip(mine, theirs)))


def kernel(x, norm_mix_w, w_in, short_conv_w, ssd_conv_w, ssd_conv_b, dt_bias, a_log, d_skip, ssd_norm_w, w_out, norm_mlp_w, w_up, w_down, final_norm_w, loss_target, m_norm_mix_w, m_w_in, m_short_conv_w, m_ssd_conv_w, m_ssd_conv_b, m_dt_bias, m_a_log, m_d_skip, m_ssd_norm_w, m_w_out, m_norm_mlp_w, m_w_up, m_w_down, m_final_norm_w, v_norm_mix_w, v_w_in, v_short_conv_w, v_ssd_conv_w, v_ssd_conv_b, v_dt_bias, v_a_log, v_d_skip, v_ssd_norm_w, v_w_out, v_norm_mlp_w, v_w_up, v_w_down, v_final_norm_w):
    w = dict(norm_mix_w=norm_mix_w, w_in=w_in, short_conv_w=short_conv_w, ssd_conv_w=ssd_conv_w,
             ssd_conv_b=ssd_conv_b, dt_bias=dt_bias, a_log=a_log, d_skip=d_skip, ssd_norm_w=ssd_norm_w, w_out=w_out,
             norm_mlp_w=norm_mlp_w, w_up=w_up, w_down=w_down, final_norm_w=final_norm_w)
    m = dict(norm_mix_w=m_norm_mix_w, w_in=m_w_in, short_conv_w=m_short_conv_w, ssd_conv_w=m_ssd_conv_w,
             ssd_conv_b=m_ssd_conv_b, dt_bias=m_dt_bias, a_log=m_a_log, d_skip=m_d_skip, ssd_norm_w=m_ssd_norm_w,
             w_out=m_w_out, norm_mlp_w=m_norm_mlp_w, w_up=m_w_up, w_down=m_w_down, final_norm_w=m_final_norm_w)
    v = dict(norm_mix_w=v_norm_mix_w, w_in=v_w_in, short_conv_w=v_short_conv_w, ssd_conv_w=v_ssd_conv_w,
             ssd_conv_b=v_ssd_conv_b, dt_bias=v_dt_bias, a_log=v_a_log, d_skip=v_d_skip, ssd_norm_w=v_ssd_norm_w,
             w_out=v_w_out, norm_mlp_w=v_norm_mlp_w, w_up=v_w_up, w_down=v_w_down, final_norm_w=v_final_norm_w)
    xi, yi, ci = lax.axis_index("x"), lax.axis_index("y"), lax.axis_index("c")
    chip = 2 * xi + yi
    nl = w_up.shape[0]
    cols = N_CHIPS * w_in.shape[2]
    npad = cols // LANES * LANES + LANES

    full = dict(w)
    small_gathered = gather_weights([w[k] for k in SMALL_SHARDED], ["lead"] * len(SMALL_SHARDED), "gather_small")
    for k, g4 in zip(SMALL_SHARDED, small_gathered):
        full[k] = jnp.concatenate([g4[j] for j in range(N_CHIPS)], axis=2)
    comm = ChipComm({k: w[k].astype(BF16) for k in BIG}, nl, npad)

    loss, grad_x, grads = local_step(x[0], loss_target[0], full, comm, cols)
    loss = lax.psum(loss, ("x", "y", "c"))
    halves = comm.finish()
    g_shard = {}

    small_names = SMALL_REPL + SMALL_SHARDED
    small_sum = allreduce_small(_pack_small([grads[k] for k in small_names]), "allreduce_small")
    for k, g in zip(small_names, _unpack_small(small_sum, [grads[k] for k in small_names])):
        if k in SMALL_SHARDED:
            width = w[k].shape[2]
            g = lax.dynamic_slice_in_dim(g, chip * width, width, axis=2)
        g_shard[k] = g

    delta, new_m, new_v = {}, {}, {}
    for k in BIG:
        g_shard[k], delta[k], new_m[k], new_v[k] = adamw_halves(w[k], *halves[k], m[k], v[k], "adamw_%s" % k)
    packed = [_pack_small([d_[k] for k in small_names]) for d_ in (w, g_shard, m, v)]
    outs = adamw(*packed, "adamw_small")
    for d_, buf in zip((delta, new_m, new_v), outs):
        for k, val in zip(small_names, _unpack_small(buf, [w[k] for k in small_names])):
            d_[k] = val

    return (loss, grad_x[None], *[g_shard[k] for k in WEIGHTS], *[delta[k] for k in WEIGHTS],
            *[new_m[k] for k in WEIGHTS], *[new_v[k] for k in WEIGHTS])
```

```python
import functools

import jax
import jax.numpy as jnp
from jax import lax
from jax.experimental import pallas as pl
from jax.experimental.pallas import tpu as pltpu

F32 = jnp.float32
BF16 = jnp.bfloat16

EPS = 1e-5
HEAD_DIM = 64
STATE = 128
GROUPS = 2
SHORT_K = 3
SSD_K = 4
LANES = 128
PAIR = LANES // HEAD_DIM
SCAN_CHUNK = 256
HALO = 16
N_CHIPS = 4
VMEM_LIMIT = 56 * 1024 * 1024

ADAM_LR = 0.001
ADAM_B1 = 0.9
ADAM_B2 = 0.999
ADAM_EPS = 1e-08
ADAM_WD = 0.01
ADAM_STEP = 10

MESH = pl.DeviceIdType.MESH


def _params(sem):
    return pltpu.CompilerParams(dimension_semantics=sem, vmem_limit_bytes=VMEM_LIMIT)


def _tile(n, cap, quantum):
    if n <= cap:
        return n
    best = None
    for t in range(quantum, cap + 1, quantum):
        if n % t == 0:
            best = t
    assert best is not None, (n, cap, quantum)
    return best


def _dot(a, b):
    return jnp.dot(a, b, preferred_element_type=F32)


def _dot_nt(a, b):
    return lax.dot_general(a, b, (((1,), (1,)), ((), ())), preferred_element_type=F32)


def _dot_tn(a, b):
    return lax.dot_general(a, b, (((0,), (0,)), ((), ())), preferred_element_type=F32)


def _dot_exact(a, b):
    return jnp.dot(a, b, precision=lax.Precision.HIGHEST, preferred_element_type=F32)


def _sigmoid(x):
    return pl.reciprocal(1.0 + jnp.exp(-x), approx=True)


def _softplus(x):
    return jnp.maximum(x, 0.0) + jnp.log(1.0 + jnp.exp(-jnp.abs(x)))


def _relu2(v):
    return jnp.square(jnp.maximum(v, 0.0))


class Rider:
    def __init__(self, ins, out_shapes, aliases, n_sems, copies):
        self.ins, self.out_shapes, self.aliases, self.n_sems, self.copies = ins, out_shapes, aliases, n_sems, copies


def _any_specs(n):
    return [pl.BlockSpec(memory_space=pl.ANY)] * n


def _ride(body, grid, in_specs, out_specs, out_shape, scratch, args, aliases, rider, sem, name):
    n_in, n_out, n_scr = len(in_specs), len(out_specs), len(scratch)
    if rider is None:
        outs = pl.pallas_call(
            body, name=name, grid=grid, in_specs=in_specs, out_specs=out_specs, out_shape=out_shape,
            scratch_shapes=scratch, input_output_aliases=aliases, compiler_params=_params(sem))(*args)
        return list(outs), []
    ri, ro = len(rider.ins), len(rider.out_shapes)
    last = tuple(g - 1 for g in grid)

    def wrapped(*refs):
        ins = refs[:n_in]
        r_ins = refs[n_in:n_in + ri]
        outs = refs[n_in + ri:n_in + ri + n_out]
        r_outs = refs[n_in + ri + n_out:n_in + ri + n_out + ro]
        scr = refs[n_in + ri + n_out + ro:n_in + ri + n_out + ro + n_scr]
        send_sems, recv_sems = refs[-2:]
        ids = [pl.program_id(a) for a in range(len(grid))]
        at_first = functools.reduce(jnp.logical_and, [i == 0 for i in ids])
        at_last = functools.reduce(jnp.logical_and, [i == e for i, e in zip(ids, last)])

        @pl.when(at_first)
        def _():
            for cp, _ in rider.copies(r_ins, r_outs, send_sems, recv_sems):
                cp.start()

        body(*ins, *outs, *scr)

        @pl.when(at_last)
        def _():
            for cp, landed in rider.copies(r_ins, r_outs, send_sems, recv_sems):
                cp.wait_send()
                landed.wait_recv()

    all_aliases = dict(aliases)
    all_aliases.update({n_in + a: n_out + b for a, b in rider.aliases.items()})
    outs = pl.pallas_call(
        wrapped, name=name, grid=grid, in_specs=list(in_specs) + _any_specs(ri),
        out_specs=list(out_specs) + _any_specs(ro), out_shape=list(out_shape) + list(rider.out_shapes),
        scratch_shapes=list(scratch) + [pltpu.SemaphoreType.DMA((rider.n_sems,)),
                                        pltpu.SemaphoreType.DMA((rider.n_sems,))],
        input_output_aliases=all_aliases, compiler_params=_params(sem))(*args, *rider.ins)
    return list(outs[:n_out]), list(outs[n_out:])


def run_rider(rider, name):
    ri, ro = len(rider.ins), len(rider.out_shapes)

    def body(*refs):
        send_sems, recv_sems = refs[-2:]
        pairs = rider.copies(refs[:ri], refs[ri:ri + ro], send_sems, recv_sems)
        for cp, _ in pairs:
            cp.start()
        for cp, landed in pairs:
            cp.wait_send()
            landed.wait_recv()

    return list(pl.pallas_call(
        body, name=name, in_specs=_any_specs(ri), out_specs=_any_specs(ro), out_shape=list(rider.out_shapes),
        scratch_shapes=[pltpu.SemaphoreType.DMA((rider.n_sems,)), pltpu.SemaphoreType.DMA((rider.n_sems,))],
        input_output_aliases=dict(rider.aliases),
        compiler_params=pltpu.CompilerParams(has_side_effects=True))(*rider.ins))


def norm_matmul(x, nw, w, layer, n, col0, out_dtype, name, emit_h=True, rider=None):
    t, d = x.shape
    mxu_cols = 2 * LANES
    tn = _tile(n, 1536, mxu_cols if n % mxu_cols == 0 else LANES)
    if n % mxu_cols == 0 and tn < 1024 <= n:
        tn = _tile(n, 3072, mxu_cols)
    tm = _tile(t, 512 if tn > 1536 else 1024, 8)
    nj = n // tn

    def body(x_ref, nw_ref, w_ref, o_ref, h_ref):
        @pl.when(pl.program_id(1) == 0)
        def _():
            xf = x_ref[...]
            r = lax.rsqrt(jnp.mean(xf * xf, axis=-1, keepdims=True) + EPS)
            h_ref[...] = (xf * r * nw_ref[...]).astype(BF16)

        o_ref[...] = _dot(h_ref[...], w_ref[...]).astype(out_dtype)

    out_specs = [pl.BlockSpec((tm, tn), lambda i, j: (i, j))]
    out_shape = [jax.ShapeDtypeStruct((t, n), out_dtype)]
    if emit_h:
        out_specs.append(pl.BlockSpec((tm, d), lambda i, j: (i, 0)))
        out_shape.append(jax.ShapeDtypeStruct((t, d), BF16))
    return _ride(
        body, (t // tm, nj),
        [pl.BlockSpec((tm, d), lambda i, j: (i, 0)), pl.BlockSpec((1, d), lambda i, j: (0, 0)),
         pl.BlockSpec((None, d, tn), lambda i, j: (layer, 0, col0 * nj + j))],
        out_specs, out_shape, [] if emit_h else [pltpu.VMEM((tm, d), BF16)], [x, nw, w], {}, rider,
        ("parallel", "arbitrary"), name)


def matmul(lhs, w, layer, transposed, n, out_dtype, name, *, lhs_fn=None, residual=None, relu_gate=None,
           rider=None):
    t, k = lhs.shape
    tm = _tile(t, 512 if k > 2048 else 1024, 8)
    tn = _tile(n, 1024, LANES)
    staged = lhs.dtype != BF16 or lhs_fn is not None
    fn = lhs_fn if lhs_fn is not None else (lambda v: v)
    has_extra = residual is not None or relu_gate is not None
    dot = _dot_nt if transposed else _dot

    def body(*refs):
        a_ref, w_ref = refs[:2]
        extra = refs[2] if has_extra else None
        o_ref = refs[3] if has_extra else refs[2]
        if staged:
            s_ref = refs[-1]

            @pl.when(pl.program_id(1) == 0)
            def _():
                s_ref[...] = fn(a_ref[...].astype(F32)).astype(BF16)

            a_ref = s_ref
        acc = dot(a_ref[...], w_ref[...])
        if residual is not None:
            acc = acc + extra[...]
        if relu_gate is not None:
            acc = acc * (2.0 * jnp.maximum(extra[...].astype(F32), 0.0))
        o_ref[...] = acc.astype(out_dtype)

    if transposed:
        w_spec = pl.BlockSpec((None, tn, k), lambda i, j: (layer, j, 0))
    else:
        w_spec = pl.BlockSpec((None, k, tn), lambda i, j: (layer, 0, j))
    in_specs = [pl.BlockSpec((tm, k), lambda i, j: (i, 0)), w_spec]
    args = [lhs, w]
    if has_extra:
        in_specs.append(pl.BlockSpec((tm, tn), lambda i, j: (i, j)))
        args.append(residual if residual is not None else relu_gate)
    outs, extra = _ride(
        body, (t // tm, n // tn), in_specs, [pl.BlockSpec((tm, tn), lambda i, j: (i, j))],
        [jax.ShapeDtypeStruct((t, n), out_dtype)], [pltpu.VMEM((tm, k), BF16)] if staged else [], args, {},
        rider, ("parallel", "arbitrary"), name)
    return outs[0], extra


def matmul_normbwd(lhs, pieces, w, layer, x, nw, dres, name, rider=None):
    t, d = x.shape
    nl = len(lhs)
    tm = _tile(t, 512, 8)
    for off, width in pieces:
        assert off % width == 0

    def body(*refs):
        lrefs = refs[:nl]
        wrefs = refs[nl:2 * nl]
        x_ref, nw_ref, dres_ref, dx_ref, dnw_ref = refs[2 * nl:]
        dh = _dot_nt(lrefs[0][...].astype(BF16), wrefs[0][...])
        for a_ref, w_ref in zip(lrefs[1:], wrefs[1:]):
            dh = dh + _dot_nt(a_ref[...].astype(BF16), w_ref[...])
        xf = x_ref[...]
        r = lax.rsqrt(jnp.mean(xf * xf, axis=-1, keepdims=True) + EPS)
        nx = xf * r
        dn = dh * nw_ref[...]
        dx = r * (dn - nx * jnp.mean(dn * nx, axis=-1, keepdims=True))
        dx_ref[...] = dres_ref[...] + dx

        @pl.when(pl.program_id(0) == 0)
        def _():
            dnw_ref[...] = jnp.zeros_like(dnw_ref)

        dnw_ref[...] += jnp.sum(dh * nx, axis=0, keepdims=True)

    in_specs = [pl.BlockSpec((tm, width), lambda i: (i, 0)) for _, width in pieces]
    in_specs += [pl.BlockSpec((None, d, width), (lambda blk: (lambda i: (layer, 0, blk)))(off // width),
                              pipeline_mode=pl.Buffered(1))
                 for off, width in pieces]
    in_specs += [pl.BlockSpec((tm, d), lambda i: (i, 0)), pl.BlockSpec((1, d), lambda i: (0, 0)),
                 pl.BlockSpec((tm, d), lambda i: (i, 0))]
    return _ride(
        body, (t // tm,), in_specs,
        [pl.BlockSpec((tm, d), lambda i: (i, 0)), pl.BlockSpec((1, d), lambda i: (0, 0))],
        [jax.ShapeDtypeStruct((t, d), F32), jax.ShapeDtypeStruct((1, d), F32)], [],
        [*lhs, *([w] * nl), x, nw, dres], {}, rider, ("arbitrary",), name)


def matmul_tn(a, b, name, *, a_fn=None, by_chip=False, chip_sums=None):
    t, k = a.shape
    n = b.shape[1]
    tk = _tile(k, 1024, LANES)
    tn = _tile(n // N_CHIPS if by_chip else n, 1024, LANES)
    tt = _tile(t, 1024, 8)
    nt = t // tt
    gn_ = n // tn
    steps = (k // tk) * gn_ * nt
    fn = a_fn if a_fn is not None else (lambda v: v)
    with_sums, chip_sums = chip_sums, list(chip_sums or ())
    for g, _ in chip_sums:
        if (g.shape[0] * g.shape[1] // 2) % (steps * HALO) or steps % g.shape[0]:
            return (matmul_tn(a, b, name, a_fn=a_fn, by_chip=by_chip),
                    [tuple(chip_sum(g_, r_, "chip_sum")) for g_, r_ in chip_sums])
    ns = len(chip_sums)
    step = lambda i, j, s: (i * gn_ + j) * nt + s

    def body(*refs):
        a_ref, b_ref = refs[:2]
        side_in = refs[2:2 + 3 * ns]
        o_ref = refs[2 + 3 * ns]
        side_out = refs[3 + 3 * ns:3 + 5 * ns]
        acc_ref = refs[-1]

        @pl.when(pl.program_id(2) == 0)
        def _():
            acc_ref[...] = jnp.zeros_like(acc_ref)

        av = a_ref[...]
        if a_fn is not None:
            av = fn(av.astype(F32))
        acc_ref[...] += _dot_tn(av.astype(BF16), b_ref[...].astype(BF16))
        for q in range(ns):
            g0_ref, g1_ref, r_ref = side_in[3 * q:3 * q + 3]
            tot = jnp.where(lax.axis_index("c") == 0, g0_ref[...], g1_ref[...]) + r_ref[...]
            side_out[2 * q][...] = tot
            side_out[2 * q + 1][...] = tot.astype(BF16)

        @pl.when(pl.program_id(2) == nt - 1)
        def _():
            o_ref[...] = acc_ref[...]

    if by_chip:
        per = n // N_CHIPS // tn
        out_specs = [pl.BlockSpec((None, tk, tn), lambda i, j, s: (j // per, i, j % per))]
        out_shape = [jax.ShapeDtypeStruct((N_CHIPS, k, n // N_CHIPS), F32)]
    else:
        out_specs = [pl.BlockSpec((tk, tn), lambda i, j, s: (i, j))]
        out_shape = [jax.ShapeDtypeStruct((k, n), F32)]
    in_specs = [pl.BlockSpec((tt, tk), lambda i, j, s: (s, i)), pl.BlockSpec((tt, tn), lambda i, j, s: (s, j))]
    args = [a, b]
    for g, recv in chip_sums:
        nch, r, c = g.shape
        r2 = r // 2
        rows = nch * r2 // steps
        per_chip = r2 // rows
        assert rows % HALO == 0 and r2 % rows == 0
        for half in range(2):
            in_specs.append(pl.BlockSpec(
                (rows, c), (lambda h: (lambda i, j, s: ((step(i, j, s) // per_chip) * 2 * per_chip + h * per_chip
                                                        + step(i, j, s) % per_chip, 0)))(half)))
        flat = pl.BlockSpec((rows, c), lambda i, j, s: (step(i, j, s), 0))
        in_specs.append(flat)
        args += [g.reshape(nch * r, c), g.reshape(nch * r, c), recv.reshape(nch * r2, c)]
        out_specs += [flat, flat]
        out_shape += [jax.ShapeDtypeStruct((nch * r2, c), F32), jax.ShapeDtypeStruct((nch * r2, c), BF16)]
    outs = pl.pallas_call(
        body, name=name, grid=(k // tk, gn_, nt), in_specs=in_specs, out_specs=out_specs, out_shape=out_shape,
        scratch_shapes=[pltpu.VMEM((tk, tn), F32)],
        compiler_params=_params(("parallel", "parallel", "arbitrary")),
    )(*args)
    if with_sums is None:
        return outs[0]
    sums = [(outs[1 + 2 * q].reshape(g.shape[0], g.shape[1] // 2, g.shape[2]),
             outs[2 + 2 * q].reshape(g.shape[0], g.shape[1] // 2, g.shape[2])) for q, (g, _) in enumerate(chip_sums)]
    return outs[0], sums


def split_to_chips(pieces, cols, name):
    d = pieces[0].shape[0]
    widths = [p.shape[1] for p in pieces]
    w = cols // N_CHIPS
    tr = _tile(d, 256, 8)
    npc = len(pieces)

    def body(*refs):
        o_ref, row = refs[npc], refs[npc + 1]
        off = 0
        for r, n in zip(refs[:npc], widths):
            row[:, off:off + n] = r[...]
            off += n
        for j in range(N_CHIPS):
            o_ref[j] = row[:, j * w:(j + 1) * w]

    return pl.pallas_call(
        body, name=name, grid=(d // tr,),
        in_specs=[pl.BlockSpec((tr, n), lambda i: (i, 0)) for n in widths],
        out_specs=pl.BlockSpec((N_CHIPS, tr, w), lambda i: (0, i, 0)),
        out_shape=jax.ShapeDtypeStruct((N_CHIPS, d, w), F32),
        scratch_shapes=[pltpu.VMEM((tr, sum(widths)), F32)],
        compiler_params=_params(("parallel",)),
    )(*pieces)


def join_from_chips(g4, npad, name):
    _, nl, d, w = g4.shape
    tr = _tile(d, 256, HALO)

    def body(g_ref, o_ref):
        for j in range(N_CHIPS):
            o_ref[:, j * w:(j + 1) * w] = g_ref[j]
        o_ref[:, N_CHIPS * w:] = jnp.zeros((tr, npad - N_CHIPS * w), o_ref.dtype)

    return pl.pallas_call(
        body, name=name, grid=(nl, d // tr),
        in_specs=[pl.BlockSpec((N_CHIPS, None, tr, w), lambda l, i: (0, l, i, 0))],
        out_specs=pl.BlockSpec((None, tr, npad), lambda l, i: (l, i, 0)),
        out_shape=jax.ShapeDtypeStruct((nl, d, npad), g4.dtype),
        compiler_params=_params(("parallel", "parallel")),
    )(g4)


def conv_mixer_fwd(proj, kw, cw, out_cols, name):
    t = proj.shape[0]
    tm = _tile(t, 1024, HALO)
    tc = _tile(cw, 1024, LANES)
    nj = cw // tc
    hb = tm // HALO

    def body(ub_ref, uc_ref, uh_ref, ucp_ref, uhp_ref, kw_ref, y_ref):
        i = pl.program_id(0)
        taps = [kw_ref[pl.ds(k, 1), :] for k in range(SHORT_K)]
        row = lax.broadcasted_iota(jnp.int32, (8, tc), 0)
        vp = ucp_ref[...].astype(F32) * uhp_ref[...].astype(F32)

        def conv(block, before):
            acc = taps[SHORT_K - 1] * block
            for k in range(SHORT_K - 1):
                s = SHORT_K - 1 - k
                acc = acc + taps[k] * jnp.where(row >= s, pltpu.roll(block, s, 0), pltpu.roll(before, s, 0))
            return acc

        def strip(s, before):
            rows = pl.ds(pl.multiple_of(s * HALO, HALO), HALO)
            v = uc_ref[rows, :].astype(F32) * uh_ref[rows, :].astype(F32)
            top, bottom = v[0:8], v[8:HALO]
            cv = jnp.concatenate([conv(top, before), conv(bottom, top)], axis=0)
            y_ref[rows, :] = (ub_ref[rows, :].astype(F32) * cv).astype(BF16)
            return bottom

        lax.fori_loop(0, tm // HALO, strip, jnp.where(i > 0, vp[8:HALO], 0.0))

    prev = lambda off: (lambda i, j: (jnp.maximum(i * hb - 1, 0), off + j))
    return pl.pallas_call(
        body, name=name, grid=(t // tm, nj),
        in_specs=[pl.BlockSpec((tm, tc), lambda i, j: (i, j)),
                  pl.BlockSpec((tm, tc), lambda i, j: (i, nj + j)),
                  pl.BlockSpec((tm, tc), lambda i, j: (i, 2 * nj + j)),
                  pl.BlockSpec((HALO, tc), prev(nj)),
                  pl.BlockSpec((HALO, tc), prev(2 * nj)),
                  pl.BlockSpec((SHORT_K, tc), lambda i, j: (0, j))],
        out_specs=pl.BlockSpec((tm, tc), lambda i, j: (i, j)),
        out_shape=jax.ShapeDtypeStruct((t, out_cols), BF16),
        compiler_params=_params(("parallel", "parallel")),
    )(proj, proj, proj, proj, proj, kw)


def conv_mixer_bwd(proj, dy, kw, cw, name):
    t = proj.shape[0]
    tm = _tile(t, 1024, HALO)
    tc = _tile(cw, 1024, LANES)
    nj = cw // tc
    hb = tm // HALO
    ni = t // tm
    last_hb = t // HALO - 1

    def body(ub_ref, uc_ref, uh_ref, dy_ref, ucp_ref, uhp_ref, ubn_ref, dyn_ref, kw_ref,
             dub_ref, duc_ref, duh_ref, dkw_ref):
        i = pl.program_id(1)
        nstrips = tm // HALO
        taps = [kw_ref[pl.ds(k, 1), :] for k in range(SHORT_K)]
        row = lax.broadcasted_iota(jnp.int32, (8, tc), 0)
        vp = ucp_ref[...].astype(F32) * uhp_ref[...].astype(F32)
        dcvn = dyn_ref[...].astype(F32) * ubn_ref[...].astype(F32)

        def shifted(block, before, s):
            return jnp.where(row >= s, pltpu.roll(block, s, 0), pltpu.roll(before, s, 0))

        def lifted(block, after, s):
            return jnp.where(row < 8 - s, pltpu.roll(block, 8 - s, 0), pltpu.roll(after, 8 - s, 0))

        def down(s, carry):
            before, sums = carry
            rows = pl.ds(pl.multiple_of(s * HALO, HALO), HALO)
            v = uc_ref[rows, :].astype(F32) * uh_ref[rows, :].astype(F32)
            dyv = dy_ref[rows, :].astype(F32)
            dcv = dyv * ub_ref[rows, :].astype(F32)
            cvs = []
            sums = list(sums)
            for block, above, dcb in ((v[0:8], before, dcv[0:8]), (v[8:HALO], v[0:8], dcv[8:HALO])):
                moved = [shifted(block, above, SHORT_K - 1 - k) for k in range(SHORT_K - 1)] + [block]
                cvs.append(sum(taps[k] * moved[k] for k in range(SHORT_K)))
                sums = [sums[k] + dcb * moved[k] for k in range(SHORT_K)]
            dub_ref[rows, :] = (dyv * jnp.concatenate(cvs, axis=0)).astype(BF16)
            return v[8:HALO], tuple(sums)

        zero = jnp.zeros((8, tc), F32)
        _, sums = lax.fori_loop(0, nstrips, down, (jnp.where(i > 0, vp[8:HALO], 0.0), (zero,) * SHORT_K))

        def up(n, after):
            rows = pl.ds(pl.multiple_of((nstrips - 1 - n) * HALO, HALO), HALO)
            uc = uc_ref[rows, :].astype(F32)
            uh = uh_ref[rows, :].astype(F32)
            dcv = dy_ref[rows, :].astype(F32) * ub_ref[rows, :].astype(F32)
            dvs = []
            for block, below in ((dcv[0:8], dcv[8:HALO]), (dcv[8:HALO], after)):
                dvs.append(taps[SHORT_K - 1] * block
                           + sum(taps[k] * lifted(block, below, SHORT_K - 1 - k) for k in range(SHORT_K - 1)))
            dv = jnp.concatenate(dvs, axis=0)
            duc_ref[rows, :] = (dv * uh).astype(BF16)
            duh_ref[rows, :] = (dv * uc).astype(BF16)
            return dcv[0:8]

        lax.fori_loop(0, nstrips, up, jnp.where(i < ni - 1, dcvn[0:8], 0.0))

        @pl.when(i == 0)
        def _():
            dkw_ref[...] = jnp.zeros_like(dkw_ref)

        for k in range(SHORT_K):
            dkw_ref[pl.ds(k, 1), :] += jnp.sum(sums[k], axis=0, keepdims=True)

    prev = lambda off: (lambda j, i: (jnp.maximum(i * hb - 1, 0), off + j))
    nxt = lambda off: (lambda j, i: (jnp.minimum((i + 1) * hb, last_hb), off + j))
    cur = lambda off: (lambda j, i: (i, off + j))
    return pl.pallas_call(
        body, name=name, grid=(nj, ni),
        in_specs=[pl.BlockSpec((tm, tc), cur(0)), pl.BlockSpec((tm, tc), cur(nj)),
                  pl.BlockSpec((tm, tc), cur(2 * nj)), pl.BlockSpec((tm, tc), cur(0)),
                  pl.BlockSpec((HALO, tc), prev(nj)), pl.BlockSpec((HALO, tc), prev(2 * nj)),
                  pl.BlockSpec((HALO, tc), nxt(0)), pl.BlockSpec((HALO, tc), nxt(0)),
                  pl.BlockSpec((SHORT_K, tc), lambda j, i: (0, j))],
        out_specs=[pl.BlockSpec((tm, tc), cur(0)), pl.BlockSpec((tm, tc), cur(0)),
                   pl.BlockSpec((tm, tc), cur(0)), pl.BlockSpec((SHORT_K, tc), lambda j, i: (0, j))],
        out_shape=[jax.ShapeDtypeStruct((t, cw), BF16)] * 3 + [jax.ShapeDtypeStruct((SHORT_K, cw), F32)],
        compiler_params=_params(("parallel", "arbitrary")),
    )(proj, proj, proj, dy, proj, proj, proj, dy, kw)


def _head_column(mat, lane, h):
    return jnp.sum(jnp.where(lane == h, mat, 0.0), axis=-1, keepdims=True)


def _ssd_common(dt_raw_ref, dtb_ref, aneg_ref, cum_s, cumt_s, chunk):
    dt = _softplus(dt_raw_ref[...] + dtb_ref[...])
    al = dt * aneg_ref[...]
    ri = lax.broadcasted_iota(jnp.int32, (chunk, chunk), 0)
    ci = lax.broadcasted_iota(jnp.int32, (chunk, chunk), 1)
    cum = _dot_exact((ri >= ci).astype(F32), al)
    cum_s[...] = cum
    cumt_s[...] = cum.T
    return dt, cum, ri >= ci


EDGE = 16


def _shift_matrices(shift_s, chunk, kk, up):
    ri = lax.broadcasted_iota(jnp.int32, (chunk, chunk), 0)
    ci = lax.broadcasted_iota(jnp.int32, (chunk, chunk), 1)
    for k in range(kk - 1):
        s = kk - 1 - k
        shift_s[k] = ((ci - ri if up else ri - ci) == s).astype(BF16)


def _causal_conv(cur, head, kw_ref, b_ref, shift_s, kk):
    acc = b_ref[...] + kw_ref[pl.ds(kk - 1, 1), :] * cur.astype(F32)
    top = b_ref[...] + kw_ref[pl.ds(kk - 1, 1), :] * head[pl.ds(8, EDGE), :]
    for k in range(kk - 1):
        acc = acc + kw_ref[pl.ds(k, 1), :] * _dot(shift_s[k], cur)
        top = top + kw_ref[pl.ds(k, 1), :] * head[pl.ds(8 - (kk - 1) + k, EDGE), :]
    return acc, top


def ssd_fwd(proj, dt_raw, y_mix, kw_xs, kw_bc, b_xs, b_bc, dtb, aneg, dskip, normw, cw, si, name, rider=None):
    t = proj.shape[0]
    ch = min(SCAN_CHUNK, t)
    nc = t // ch
    npair = si // LANES
    ppg = npair // GROUPS
    gn = GROUPS * STATE
    gw = si // GROUPS
    assert cw == si and (3 * cw + 2 * si) % (2 * gn) == 0
    zblk = 3 * cw // si
    xsblk = zblk + 1
    bcblk = (3 * cw + 2 * si) // (2 * gn)

    def body(z_ref, xs_ref, bc_ref, dtr_ref, ymix_ref, kwx_ref, kwb_ref, bx_ref, bb_ref, dtb_ref, aneg_ref, dsk_ref,
             nw_ref, yb_ref, ys_ref, hs_ref, xcx_ref, xcb_ref,
             headx, headb, shift_s, xs_s, bc_s, h_s, gated_s, s_s, cum_s, cumt_s):
        del ymix_ref
        c = pl.program_id(0)

        @pl.when(c == 0)
        def _():
            h_s[...] = jnp.zeros_like(h_s)
            headx[0:8, :] = jnp.zeros((8, si), F32)
            headb[0:8, :] = jnp.zeros((8, 2 * gn), F32)
            _shift_matrices(shift_s, ch, SSD_K, up=False)

        for raw_ref, head, kw_ref, b_ref, pre_ref, act_s in ((xs_ref, headx, kwx_ref, bx_ref, xcx_ref, xs_s),
                                                           (bc_ref, headb, kwb_ref, bb_ref, xcb_ref, bc_s)):
            head[8:8 + EDGE, :] = raw_ref[0:EDGE, :].astype(F32)
            pre, top = _causal_conv(raw_ref[...], head, kw_ref, b_ref, shift_s, SSD_K)
            head[0:8, :] = raw_ref[ch - EDGE:ch, :].astype(F32)[EDGE - 8:EDGE]
            pre_ref[...] = pre.astype(BF16)
            pre_ref[0:EDGE, :] = top.astype(BF16)
            act_s[...] = (pre * _sigmoid(pre)).astype(act_s.dtype)
            act_s[0:EDGE, :] = (top * _sigmoid(top)).astype(act_s.dtype)

        dt, cum, tril = _ssd_common(dtr_ref, dtb_ref, aneg_ref, cum_s, cumt_s, ch)
        lane = lax.broadcasted_iota(jnp.int32, (ch, LANES), 1)
        lane1 = lax.broadcasted_iota(jnp.int32, (1, LANES), 1)
        low = lane < HEAD_DIM
        clast = cum_s[pl.ds(ch - 1, 1), :]

        for p in range(npair):
            g = p // ppg
            col = slice(p * LANES, (p + 1) * LANES)
            bg = bc_s[:, g * STATE:(g + 1) * STATE]
            cg = bc_s[:, gn + g * STATE:gn + (g + 1) * STATE]
            if p % ppg == 0:
                s_s[...] = _dot_nt(cg, bg)
            heads = (PAIR * p, PAIR * p + 1)
            ccol = [_head_column(cum, lane, h) for h in heads]
            dcol = [_head_column(dt, lane, h) for h in heads]
            cl = [jnp.sum(jnp.where(lane1 == h, clast, 0.0), axis=-1, keepdims=True) for h in heads]
            cum_px = jnp.where(low, ccol[0], ccol[1])
            dt_px = jnp.where(low, dcol[0], dcol[1])
            cl_px = jnp.where(lane1 < HEAD_DIM, cl[0], cl[1])
            xs_p = xs_s[:, col]
            xdt = xs_p * dt_px
            y = dsk_ref[:, col] * xs_p
            for hi, h in enumerate(heads):
                dec = jnp.exp(jnp.where(tril, ccol[hi] - cumt_s[pl.ds(h, 1), :], -jnp.inf))
                wm = (s_s[...] * dec).astype(BF16)
                xm = jnp.where(low if hi == 0 else jnp.logical_not(low), xdt, 0.0).astype(BF16)
                y = y + _dot(wm, xm)
            hp = h_s[p]
            hs_ref[0, p] = hp
            y = y + _dot(cg, hp.astype(BF16)) * jnp.exp(cum_px)
            st = _dot_tn(bg, (xdt * jnp.exp(cl_px - cum_px)).astype(BF16))
            h_s[p] = jnp.exp(cl_px) * hp + st
            ys_ref[:, col] = y.astype(BF16)
            zp = z_ref[:, col].astype(F32)
            gated_s[:, col] = y * zp * _sigmoid(zp)

        for g in range(GROUPS):
            col = slice(g * gw, (g + 1) * gw)
            gg = gated_s[:, col]
            r = lax.rsqrt(jnp.mean(gg * gg, axis=-1, keepdims=True) + EPS)
            yb_ref[:, col] = (gg * r * nw_ref[:, col]).astype(BF16)

    full = lambda shape: pl.BlockSpec(shape, lambda c: tuple(0 for _ in shape))
    return _ride(
        body, (nc,),
        [pl.BlockSpec((ch, si), lambda c: (c, zblk)),
         pl.BlockSpec((ch, si), lambda c: (c, xsblk)),
         pl.BlockSpec((ch, 2 * gn), lambda c: (c, bcblk)),
         pl.BlockSpec((ch, LANES), lambda c: (c, 0)),
         pl.BlockSpec(memory_space=pl.ANY),
         full((SSD_K, si)), full((SSD_K, 2 * gn)), full((1, si)), full((1, 2 * gn)),
         full((1, LANES)), full((1, LANES)), full((1, si)), full((1, si))],
        [pl.BlockSpec((ch, si), lambda c: (c, cw // si)),
         pl.BlockSpec((ch, si), lambda c: (c, 0)),
         pl.BlockSpec((1, npair, STATE, LANES), lambda c: (c, 0, 0, 0)),
         pl.BlockSpec((ch, si), lambda c: (c, 0)), pl.BlockSpec((ch, 2 * gn), lambda c: (c, 0))],
        [jax.ShapeDtypeStruct(y_mix.shape, BF16), jax.ShapeDtypeStruct((t, si), BF16),
         jax.ShapeDtypeStruct((nc, npair, STATE, LANES), F32),
         jax.ShapeDtypeStruct((t, si), BF16), jax.ShapeDtypeStruct((t, 2 * gn), BF16)],
        [pltpu.VMEM((8 + EDGE, si), F32), pltpu.VMEM((8 + EDGE, 2 * gn), F32),
         pltpu.VMEM((SSD_K - 1, ch, ch), BF16),
         pltpu.VMEM((ch, si), F32), pltpu.VMEM((ch, 2 * gn), BF16),
         pltpu.VMEM((npair, STATE, LANES), F32), pltpu.VMEM((ch, si), F32),
         pltpu.VMEM((ch, ch), F32), pltpu.VMEM((ch, LANES), F32), pltpu.VMEM((LANES, ch), F32)],
        [proj, proj, proj, dt_raw, y_mix, kw_xs, kw_bc, b_xs, b_bc, dtb, aneg, dskip, normw], {4: 0}, rider,
        ("arbitrary",), name)


def ssd_bwd(proj, dt_raw, ys, hsave, pre_xs, pre_bc, dy, kw_xs, kw_bc, dtb, aneg, dskip, normw, cw, si, name,
            rider=None):
    t = proj.shape[0]
    ch = min(SCAN_CHUNK, t)
    nc = t // ch
    npair = si // LANES
    ppg = npair // GROUPS
    gn = GROUPS * STATE
    gw = si // GROUPS
    zblk = 3 * cw // si
    xsblk = zblk + 1
    bcblk = (3 * cw + 2 * si) // (2 * gn)

    def body(z_ref, xs_ref, bc_ref, xcx_ref, xcb_ref, dtr_ref, ys_ref, hs_ref, dyb_ref,
             kwx_ref, kwb_ref, dtb_ref, aneg_ref, dsk_ref, nw_ref,
             dz_ref, dxs_ref, dbc_ref, ddt_ref, dkwx_ref, dkwb_ref, dbx_ref, dbb_ref, ddtb_ref, da_ref, ddsk_ref,
             dnw_ref,
             tailx, tailb, shift_s, xs_s, bc_s, dsx_s, dsb_s, dy_s, dxs_s, dbc_s, dh_s, s_s, ds_s,
             cum_s, cumt_s, dccol_s, dcrow_s, ddtcol_s, dcl_s):
        i = pl.program_id(0)

        @pl.when(i == 0)
        def _():
            dh_s[...] = jnp.zeros_like(dh_s)
            tailx[EDGE:EDGE + 8, :] = jnp.zeros((8, si), F32)
            tailb[EDGE:EDGE + 8, :] = jnp.zeros((8, 2 * gn), F32)
            _shift_matrices(shift_s, ch, SSD_K, up=True)
            for r in (dkwx_ref, dkwb_ref, dbx_ref, dbb_ref, ddtb_ref, da_ref, ddsk_ref, dnw_ref):
                r[...] = jnp.zeros_like(r)

        xc = xcx_ref[...].astype(F32)
        sg = _sigmoid(xc)
        xs_s[...] = xc * sg
        dsx_s[...] = sg * (1.0 + xc * (1.0 - sg))
        bcc = xcb_ref[...].astype(F32)
        sgb = _sigmoid(bcc)
        bc_s[...] = (bcc * sgb).astype(BF16)
        dsb_s[...] = sgb * (1.0 + bcc * (1.0 - sgb))

        dt, cum, tril = _ssd_common(dtr_ref, dtb_ref, aneg_ref, cum_s, cumt_s, ch)
        lane = lax.broadcasted_iota(jnp.int32, (ch, LANES), 1)
        lane1 = lax.broadcasted_iota(jnp.int32, (1, LANES), 1)
        low = lane < HEAD_DIM
        low1 = lane1 < HEAD_DIM
        clast = cum_s[pl.ds(ch - 1, 1), :]

        for g in range(GROUPS):
            col = slice(g * gw, (g + 1) * gw)
            ysf = ys_ref[:, col].astype(F32)
            zf = z_ref[:, col].astype(F32)
            sz = _sigmoid(zf)
            silz = zf * sz
            gg = ysf * silz
            r = lax.rsqrt(jnp.mean(gg * gg, axis=-1, keepdims=True) + EPS)
            nrm = gg * r
            dyb = dyb_ref[:, col].astype(F32)
            dnw_ref[:, col] += jnp.sum(dyb * nrm, axis=0, keepdims=True)
            dn = dyb * nw_ref[:, col]
            dgg = r * (dn - nrm * jnp.mean(dn * nrm, axis=-1, keepdims=True))
            dy_s[:, col] = dgg * silz
            dz_ref[:, col] = (dgg * ysf * (sz * (1.0 + zf * (1.0 - sz)))).astype(BF16)

        dccol_s[...] = jnp.zeros_like(dccol_s)
        dcrow_s[...] = jnp.zeros_like(dcrow_s)
        ddtcol_s[...] = jnp.zeros_like(ddtcol_s)
        dcl_s[...] = jnp.zeros_like(dcl_s)
        dbc_s[...] = jnp.zeros_like(dbc_s)

        for p in range(npair):
            g = p // ppg
            col = slice(p * LANES, (p + 1) * LANES)
            bcol = slice(g * STATE, (g + 1) * STATE)
            ccolg = slice(gn + g * STATE, gn + (g + 1) * STATE)
            bg = bc_s[:, bcol]
            cg = bc_s[:, ccolg]
            if p % ppg == 0:
                s_s[...] = _dot_nt(cg, bg)
                ds_s[...] = jnp.zeros_like(ds_s)
            heads = (PAIR * p, PAIR * p + 1)
            masks = (low, jnp.logical_not(low))
            masks1 = (low1, jnp.logical_not(low1))
            ccol = [_head_column(cum, lane, h) for h in heads]
            dcol = [_head_column(dt, lane, h) for h in heads]
            cl = [jnp.sum(jnp.where(lane1 == h, clast, 0.0), axis=-1, keepdims=True) for h in heads]
            cum_px = jnp.where(low, ccol[0], ccol[1])
            dt_px = jnp.where(low, dcol[0], dcol[1])
            cl_px = jnp.where(low1, cl[0], cl[1])
            e_px = jnp.exp(cum_px)
            dec_end = jnp.exp(cl_px - cum_px)
            gdec = jnp.exp(cl_px)
            xs_p = xs_s[:, col]
            xdt = xs_p * dt_px
            dyp = dy_s[:, col]
            hc = hs_ref[0, p]
            hcb = hc.astype(BF16)
            dhn = dh_s[p]
            dhnb = dhn.astype(BF16)

            ddsk_ref[:, col] += jnp.sum(dyp * xs_p, axis=0, keepdims=True)
            dxs_acc = dsk_ref[:, col] * dyp
            dye = dyp * e_px
            dyeb = dye.astype(BF16)
            dbc_s[:, ccolg] += _dot_nt(dyeb, hcb)
            dcum_lane = dye * _dot(cg, hcb)
            dh_from_y = _dot_tn(cg, dyeb)
            xd = xdt * dec_end
            dxd = _dot(bg, dhnb)
            dbc_s[:, bcol] += _dot_nt(xd.astype(BF16), dhnb)
            dxdt = dxd * dec_end
            t1 = dxd * xd
            dcum_lane = dcum_lane - t1
            dcl_lane = jnp.sum(t1, axis=0, keepdims=True) + jnp.sum(dhn * hc, axis=0, keepdims=True) * gdec
            dh_s[p] = gdec * dhn + dh_from_y
            xdtb = xdt.astype(BF16)
            for hi, h in enumerate(heads):
                dym = jnp.where(masks[hi], dyp, 0.0).astype(BF16)
                dw = _dot_nt(dym, xdtb)
                dec = jnp.exp(jnp.where(tril, ccol[hi] - cumt_s[pl.ds(h, 1), :], -jnp.inf))
                wm = s_s[...] * dec
                dxdt = dxdt + _dot_tn(wm.astype(BF16), dym)
                ds_s[...] += dw * dec
                gm = dw * wm
                rowsum = jnp.sum(gm, axis=-1, keepdims=True)
                lanesum = jnp.sum(jnp.where(masks[hi], dcum_lane, 0.0), axis=-1, keepdims=True)
                dccol_s[...] += jnp.where(lane == h, rowsum + lanesum, 0.0)
                dcrow_s[pl.ds(h, 1), :] = jnp.sum(gm, axis=0, keepdims=True)
                dcl_h = jnp.sum(jnp.where(masks1[hi], dcl_lane, 0.0), axis=-1, keepdims=True)
                dcl_s[...] += jnp.where(lane1 == h, dcl_h, 0.0)
            ddt_lane = dxdt * xs_p
            for hi, h in enumerate(heads):
                s = jnp.sum(jnp.where(masks[hi], ddt_lane, 0.0), axis=-1, keepdims=True)
                ddtcol_s[...] += jnp.where(lane == h, s, 0.0)
            dxs_s[:, col] = dxs_acc + dxdt * dt_px
            if p % ppg == ppg - 1:
                dsb = ds_s[...].astype(BF16)
                dbc_s[:, ccolg] += _dot(dsb, bg)
                dbc_s[:, bcol] += _dot_tn(dsb, cg)

        rowi = lax.broadcasted_iota(jnp.int32, (ch, LANES), 0)
        dcum = dccol_s[...] - dcrow_s[...].T + jnp.where(rowi == ch - 1, dcl_s[...], 0.0)
        ri = lax.broadcasted_iota(jnp.int32, (ch, ch), 0)
        ci = lax.broadcasted_iota(jnp.int32, (ch, ch), 1)
        dal = _dot_exact((ri <= ci).astype(F32), dcum)
        ddt = dal * aneg_ref[...] + ddtcol_s[...]
        da_ref[...] += jnp.sum(dal * dt, axis=0, keepdims=True)
        ddtr = ddt * _sigmoid(dtr_ref[...] + dtb_ref[...])
        ddt_ref[...] = ddtr
        ddtb_ref[...] += jnp.sum(ddtr, axis=0, keepdims=True)

        for (dpost, dsl, tail, raw_ref, kw_ref, dkw_ref, db_ref, out_ref) in (
                (dxs_s, dsx_s, tailx, xs_ref, kwx_ref, dkwx_ref, dbx_ref, dxs_ref),
                (dbc_s, dsb_s, tailb, bc_ref, kwb_ref, dkwb_ref, dbb_ref, dbc_ref)):
            dxc = dpost[...] * dsl[...]
            dxcb = dxc.astype(BF16)
            raw = raw_ref[...].astype(F32)
            raw_end = raw_ref[ch - EDGE:ch, :].astype(F32)
            tail[0:EDGE, :] = dxcb[ch - EDGE:ch].astype(F32)
            db_ref[...] += jnp.sum(dxc, axis=0, keepdims=True)
            draw = kw_ref[pl.ds(SSD_K - 1, 1), :] * dxc
            dkw_ref[pl.ds(SSD_K - 1, 1), :] += jnp.sum(dxc * raw, axis=0, keepdims=True)
            fix = jnp.zeros((EDGE, dxc.shape[1]), F32)
            for k in range(SSD_K - 1):
                moved = _dot(shift_s[k], dxcb)
                miss = tail[pl.ds(SSD_K - 1 - k, EDGE), :] - moved[ch - EDGE:ch]
                draw = draw + kw_ref[pl.ds(k, 1), :] * moved
                fix = fix + kw_ref[pl.ds(k, 1), :] * miss
                dkw_ref[pl.ds(k, 1), :] += (jnp.sum(moved * raw, axis=0, keepdims=True)
                                            + jnp.sum(miss * raw_end, axis=0, keepdims=True))
            out_ref[...] = draw.astype(BF16)
            out_ref[ch - EDGE:ch, :] = (draw[ch - EDGE:ch] + fix).astype(BF16)
            tail[EDGE:EDGE + 8, :] = dxcb[0:EDGE].astype(F32)[0:8]

    full = lambda shape: pl.BlockSpec(shape, lambda i: tuple(0 for _ in shape))
    rev = lambda blk: (lambda i: (nc - 1 - i, blk))
    small_in = [(SSD_K, si), (SSD_K, 2 * gn), (1, LANES), (1, LANES), (1, si), (1, si)]
    small = [(SSD_K, si), (SSD_K, 2 * gn), (1, si), (1, 2 * gn), (1, LANES), (1, LANES), (1, si), (1, si)]
    return _ride(
        body, (nc,),
        [pl.BlockSpec((ch, si), rev(zblk)), pl.BlockSpec((ch, si), rev(xsblk)),
         pl.BlockSpec((ch, 2 * gn), rev(bcblk)),
         pl.BlockSpec((ch, si), rev(0)), pl.BlockSpec((ch, 2 * gn), rev(0)),
         pl.BlockSpec((ch, LANES), rev(0)), pl.BlockSpec((ch, si), rev(0)),
         pl.BlockSpec((1, npair, STATE, LANES), lambda i: (nc - 1 - i, 0, 0, 0)),
         pl.BlockSpec((ch, si), rev(cw // si))] + [full(s) for s in small_in],
        [pl.BlockSpec((ch, si), rev(0)), pl.BlockSpec((ch, si), rev(0)),
         pl.BlockSpec((ch, 2 * gn), rev(0)), pl.BlockSpec((ch, LANES), rev(0))] + [full(s) for s in small],
        [jax.ShapeDtypeStruct((t, si), BF16), jax.ShapeDtypeStruct((t, si), BF16),
         jax.ShapeDtypeStruct((t, 2 * gn), BF16), jax.ShapeDtypeStruct((t, LANES), F32)]
        + [jax.ShapeDtypeStruct(s, F32) for s in small],
        [pltpu.VMEM((EDGE + 8, si), F32), pltpu.VMEM((EDGE + 8, 2 * gn), F32),
         pltpu.VMEM((SSD_K - 1, ch, ch), BF16),
         pltpu.VMEM((ch, si), F32), pltpu.VMEM((ch, 2 * gn), BF16),
         pltpu.VMEM((ch, si), F32), pltpu.VMEM((ch, 2 * gn), F32),
         pltpu.VMEM((ch, si), F32), pltpu.VMEM((ch, si), F32), pltpu.VMEM((ch, 2 * gn), F32),
         pltpu.VMEM((npair, STATE, LANES), F32),
         pltpu.VMEM((ch, ch), F32), pltpu.VMEM((ch, ch), F32),
         pltpu.VMEM((ch, LANES), F32), pltpu.VMEM((LANES, ch), F32),
         pltpu.VMEM((ch, LANES), F32), pltpu.VMEM((LANES, ch), F32),
         pltpu.VMEM((ch, LANES), F32), pltpu.VMEM((1, LANES), F32)],
        [proj, proj, proj, pre_xs, pre_bc, dt_raw, ys, hsave, dy, kw_xs, kw_bc, dtb, aneg, dskip, normw],
        {}, rider, ("arbitrary",), name)


def final_loss(x, nw, tgt, name):
    t, d = x.shape
    tm = _tile(t, 512, 8)

    def body(x_ref, nw_ref, t_ref, dx_ref, dnw_ref, ls_ref):
        xf = x_ref[...]
        r = lax.rsqrt(jnp.mean(xf * xf, axis=-1, keepdims=True) + EPS)
        nx = xf * r
        e = nx * nw_ref[...] - t_ref[...]
        dyv = e * (1.0 / d)
        dn = dyv * nw_ref[...]
        dx_ref[...] = r * (dn - nx * jnp.mean(dn * nx, axis=-1, keepdims=True))

        @pl.when(pl.program_id(0) == 0)
        def _():
            dnw_ref[...] = jnp.zeros_like(dnw_ref)
            ls_ref[...] = jnp.zeros_like(ls_ref)

        dnw_ref[...] += jnp.sum(dyv * nx, axis=0, keepdims=True)
        ls_ref[...] += jnp.sum(e * e, axis=0, keepdims=True) * (0.5 / d)

    return pl.pallas_call(
        body, name=name, grid=(t // tm,),
        in_specs=[pl.BlockSpec((tm, d), lambda i: (i, 0)), pl.BlockSpec((1, d), lambda i: (0, 0)),
                  pl.BlockSpec((tm, d), lambda i: (i, 0))],
        out_specs=[pl.BlockSpec((tm, d), lambda i: (i, 0)), pl.BlockSpec((1, d), lambda i: (0, 0)),
                   pl.BlockSpec((1, d), lambda i: (0, 0))],
        out_shape=[jax.ShapeDtypeStruct((t, d), F32), jax.ShapeDtypeStruct((1, d), F32),
                   jax.ShapeDtypeStruct((1, d), F32)],
        compiler_params=_params(("arbitrary",)),
    )(x, nw, tgt)


def _rows3(a):
    if a.ndim == 1:
        return a.reshape(1, 1, a.shape[0])
    if a.ndim == 2:
        return a.reshape(1, *a.shape)
    return a.reshape(-1, a.shape[-2], a.shape[-1])


def adamw(w, g, m, v, name):
    shape = w.shape
    views = [_rows3(a) for a in (w, g, m, v)]
    b, r, c = views[0].shape
    tr = _tile(r, 256, 16) if r % 16 == 0 else r

    def body(w_ref, g_ref, m_ref, v_ref, d_ref, nm_ref, nv_ref):
        g = g_ref[...]
        m = ADAM_B1 * m_ref[...] + (1.0 - ADAM_B1) * g
        v = ADAM_B2 * v_ref[...] + (1.0 - ADAM_B2) * (g * g)
        m_hat = m / (1.0 - ADAM_B1 ** ADAM_STEP)
        v_hat = v / (1.0 - ADAM_B2 ** ADAM_STEP)
        d_ref[...] = -ADAM_LR * (m_hat / (jnp.sqrt(v_hat) + ADAM_EPS) + ADAM_WD * w_ref[...])
        nm_ref[...] = m
        nv_ref[...] = v

    spec = pl.BlockSpec((1, tr, c), lambda i, j: (i, j, 0))
    outs = pl.pallas_call(
        body, name=name, grid=(b, r // tr), in_specs=[spec] * 4, out_specs=[spec] * 3,
        out_shape=[jax.ShapeDtypeStruct((b, r, c), F32)] * 3,
        compiler_params=_params(("parallel", "parallel")),
    )(*views)
    return [o.reshape(shape) for o in outs]


def adamw_halves(w, g_mine, g_theirs, m, v, name):
    nl, r, c = w.shape
    r2 = r // 2
    tr = _tile(r2, 256, 16)
    nb = r2 // tr

    def body(w_ref, gm_ref, gt_ref, m_ref, v_ref, g_ref, d_ref, nm_ref, nv_ref):
        mine = (pl.program_id(1) // nb) == lax.axis_index("c")
        g = jnp.where(mine, gm_ref[...], gt_ref[...])
        m = ADAM_B1 * m_ref[...] + (1.0 - ADAM_B1) * g
        v = ADAM_B2 * v_ref[...] + (1.0 - ADAM_B2) * (g * g)
        m_hat = m / (1.0 - ADAM_B1 ** ADAM_STEP)
        v_hat = v / (1.0 - ADAM_B2 ** ADAM_STEP)
        g_ref[...] = g
        d_ref[...] = -ADAM_LR * (m_hat / (jnp.sqrt(v_hat) + ADAM_EPS) + ADAM_WD * w_ref[...])
        nm_ref[...] = m
        nv_ref[...] = v

    whole = pl.BlockSpec((1, tr, c), lambda l, i: (l, i, 0))
    half = pl.BlockSpec((1, tr, c), lambda l, i: (l, i % nb, 0))
    return pl.pallas_call(
        body, name=name, grid=(nl, 2 * nb), in_specs=[whole, half, half, whole, whole], out_specs=[whole] * 4,
        out_shape=[jax.ShapeDtypeStruct((nl, r, c), F32)] * 4,
        compiler_params=_params(("parallel", "parallel")),
    )(w, g_mine, g_theirs, m, v)


def _coords():
    return lax.axis_index("x"), lax.axis_index("y"), lax.axis_index("c")


def _ici_peers(x, y):
    chips = [(1 - x, y), (x, 1 - y), (1 - x, 1 - y)]
    return chips, [2 * cx + cy for cx, cy in chips]


def _place(ref, how, chip, layers, per):
    if how == "lead":
        return ref.at[chip, layers]
    start = pl.multiple_of(chip * per, per)
    if how == "rows":
        return ref.at[layers, pl.ds(start, per), :]
    return ref.at[layers, :, pl.ds(start, per)]


def gather_weights(shards, hows, name):
    na = len(shards)
    out_shape = []
    for s, how in zip(shards, hows):
        assert s.shape[0] % 2 == 0
        if how == "lead":
            shp = (N_CHIPS, *s.shape)
        elif how == "rows":
            shp = (s.shape[0], N_CHIPS * s.shape[1], s.shape[2])
        else:
            shp = (s.shape[0], s.shape[1], N_CHIPS * s.shape[2])
        out_shape.append(jax.ShapeDtypeStruct(shp, s.dtype))

    def body(*refs):
        ins = refs[:na]
        outs = refs[na:2 * na]
        send_sems, recv_sems = refs[2 * na:]
        x, y, c = _coords()
        me = 2 * x + y
        chips, chip_ids = _ici_peers(x, y)
        sibling = (x, y, 1 - c)

        def dst(a, chip, layers):
            per = {"lead": 0, "rows": ins[a].shape[1], "cols": ins[a].shape[-1]}[hows[a]]
            return _place(outs[a], hows[a], chip, layers, per)

        def copy(a, k, src, dst_ref, to):
            return pltpu.make_async_remote_copy(
                src_ref=src, dst_ref=dst_ref, send_sem=send_sems.at[7 * a + k], recv_sem=recv_sems.at[7 * a + k],
                device_id=to, device_id_type=MESH)

        started = []
        halves = []
        for a in range(na):
            nl = ins[a].shape[0]
            hl = nl // 2
            mine = pl.ds(c * hl, hl)
            theirs = pl.ds((1 - c) * hl, hl)
            halves.append((mine, theirs))
            for k in range(3):
                cp = copy(a, k, ins[a].at[mine], dst(a, me, mine), (*chips[k], c))
                cp.start()
                started.append(cp)
            own = copy(a, 6, ins[a], dst(a, me, pl.ds(0, nl)), sibling)
            own.start()
            started.append(own)
        for a in range(na):
            mine, _ = halves[a]
            for k in range(3):
                landed = dst(a, chip_ids[k], mine)
                copy(a, k, landed, landed, (*chips[k], c)).wait_recv()
                fw = copy(a, 3 + k, landed, landed, sibling)
                fw.start()
                started.append(fw)
        for a in range(na):
            _, theirs = halves[a]
            for k in range(3):
                got = dst(a, chip_ids[k], theirs)
                copy(a, 3 + k, got, got, sibling).wait_recv()
            whole = dst(a, me, pl.ds(0, ins[a].shape[0]))
            copy(a, 6, whole, whole, sibling).wait_recv()
        for cp in started:
            cp.wait_send()

    return pl.pallas_call(
        body, name=name, in_specs=_any_specs(na), out_specs=_any_specs(na), out_shape=out_shape,
        scratch_shapes=[pltpu.SemaphoreType.DMA((7 * na,)), pltpu.SemaphoreType.DMA((7 * na,))],
        compiler_params=pltpu.CompilerParams(has_side_effects=True),
    )(*shards)


def _remote(src, dst, send_sems, recv_sems, k, to):
    return pltpu.make_async_remote_copy(src_ref=src, dst_ref=dst, send_sem=send_sems.at[k], recv_sem=recv_sems.at[k],
                                        device_id=to, device_id_type=MESH)


LAYER_HOW = ("lead", "rows", "cols", "rows")


def _layer_place(ref, how, chip, shard_shape, start, size):
    r, c = shard_shape
    if how == "lead":
        return ref.at[chip, :, pl.ds(start, size), :]
    if how == "rows":
        return ref.at[:, pl.ds(pl.multiple_of(chip * r + start, HALO), size), :]
    return ref.at[:, pl.ds(start, size), pl.ds(pl.multiple_of(chip * c, LANES), c)]


def weight_rider_ici(shards, hows, layer):
    shapes = [tuple(s.shape[1:]) for s in shards]
    out_shapes = []
    for (r, c), how, s in zip(shapes, hows, shards):
        shp = {"lead": (N_CHIPS, 1, r, c), "rows": (1, N_CHIPS * r, c), "cols": (1, r, N_CHIPS * c)}[how]
        out_shapes.append(jax.ShapeDtypeStruct(shp, s.dtype))

    def copies(ins, outs, send_sems, recv_sems):
        x, y, c = _coords()
        me = 2 * x + y
        chips, chip_ids = _ici_peers(x, y)
        sibling = (x, y, 1 - c)
        pairs = []
        for a, (shape, how) in enumerate(zip(shapes, hows)):
            half = shape[0] // 2
            mine = pl.multiple_of(c * half, HALO)
            src = ins[a].at[pl.ds(layer, 1)]
            for k in range(3):
                to = (*chips[k], c)
                land = _layer_place(outs[a], how, chip_ids[k], shape, mine, half)
                pairs.append((_remote(src.at[:, pl.ds(mine, half), :], _layer_place(outs[a], how, me, shape, mine, half),
                                      send_sems, recv_sems, 4 * a + k, to),
                              _remote(land, land, send_sems, recv_sems, 4 * a + k, to)))
            whole = _layer_place(outs[a], how, me, shape, 0, shape[0])
            pairs.append((_remote(src, whole, send_sems, recv_sems, 4 * a + 3, sibling),
                          _remote(whole, whole, send_sems, recv_sems, 4 * a + 3, sibling)))
        return pairs

    return Rider(list(shards), out_shapes, {}, 4 * len(shards), copies)


def weight_rider_d2d(bufs, shapes, hows):
    def copies(ins, outs, send_sems, recv_sems):
        x, y, c = _coords()
        _, chip_ids = _ici_peers(x, y)
        sibling = (x, y, 1 - c)
        pairs = []
        for a, (shape, how) in enumerate(zip(shapes, hows)):
            half = shape[0] // 2
            mine = pl.multiple_of(c * half, HALO)
            theirs = pl.multiple_of((1 - c) * half, HALO)
            for k in range(3):
                land = _layer_place(outs[a], how, chip_ids[k], shape, theirs, half)
                pairs.append((_remote(_layer_place(ins[a], how, chip_ids[k], shape, mine, half),
                                      _layer_place(outs[a], how, chip_ids[k], shape, mine, half),
                                      send_sems, recv_sems, 3 * a + k, sibling),
                              _remote(land, land, send_sems, recv_sems, 3 * a + k, sibling)))
        return pairs

    return Rider(list(bufs), [jax.ShapeDtypeStruct(b.shape, b.dtype) for b in bufs],
                 {a: a for a in range(len(bufs))}, 3 * len(bufs), copies)


def grads_rider_sibling(arrs):
    def copies(ins, outs, send_sems, recv_sems):
        x, y, c = _coords()
        sibling = (x, y, 1 - c)
        pairs = []
        for a in range(len(arrs)):
            r2 = ins[a].shape[1] // 2
            src = ins[a].at[:, pl.ds(pl.multiple_of((1 - c) * r2, 8), r2), :]
            pairs.append((_remote(src, outs[a], send_sems, recv_sems, a, sibling),
                          _remote(outs[a], outs[a], send_sems, recv_sems, a, sibling)))
        return pairs

    return Rider(list(arrs), [jax.ShapeDtypeStruct((a.shape[0], a.shape[1] // 2, a.shape[2]), a.dtype) for a in arrs],
                 {}, len(arrs), copies)


def chip_sum(g, recv, name):
    nch, r, c = g.shape
    r2 = r // 2
    tr = _tile(r2, 256, 16)
    nb = r2 // tr

    def body(g0_ref, g1_ref, r_ref, o32_ref, o16_ref):
        s = jnp.where(lax.axis_index("c") == 0, g0_ref[...], g1_ref[...]) + r_ref[...]
        o32_ref[...] = s
        o16_ref[...] = s.astype(BF16)

    here = pl.BlockSpec((1, tr, c), lambda i, j: (i, j, 0))
    return pl.pallas_call(
        body, name=name, grid=(nch, nb),
        in_specs=[here, pl.BlockSpec((1, tr, c), lambda i, j: (i, nb + j, 0)), here],
        out_specs=[here, here],
        out_shape=[jax.ShapeDtypeStruct((nch, r2, c), F32), jax.ShapeDtypeStruct((nch, r2, c), BF16)],
        compiler_params=_params(("parallel", "parallel")),
    )(g, g, recv)


def grads_rider_chips(arrs):
    def copies(ins, outs, send_sems, recv_sems):
        x, y, c = _coords()
        chips, chip_ids = _ici_peers(x, y)
        pairs = []
        for a in range(len(arrs)):
            for k in range(3):
                to = (*chips[k], c)
                pairs.append((_remote(ins[a].at[chip_ids[k]], outs[a].at[k], send_sems, recv_sems, 3 * a + k, to),
                              _remote(outs[a].at[k], outs[a].at[k], send_sems, recv_sems, 3 * a + k, to)))
        return pairs

    return Rider(list(arrs), [jax.ShapeDtypeStruct((3, *a.shape[1:]), a.dtype) for a in arrs], {}, 3 * len(arrs),
                 copies)


def grad_sum(p32, recv, layer, nl, buf, name):
    _, r2, c = p32.shape
    tr = _tile(r2, 256, 16)
    nb = r2 // tr

    def body(p0_ref, p1_ref, p2_ref, p3_ref, r0_ref, r1_ref, r2_ref, *rest):
        o_ref = rest[-1]
        x, y, _ = _coords()
        chip = 2 * x + y
        own = jnp.where(chip == 0, p0_ref[...], jnp.where(chip == 1, p1_ref[...],
                                                        jnp.where(chip == 2, p2_ref[...], p3_ref[...])))
        o_ref[...] = own + r0_ref[...].astype(F32) + r1_ref[...].astype(F32) + r2_ref[...].astype(F32)

    slot = lambda k: pl.BlockSpec((1, tr, c), lambda j: (k, j, 0))
    in_specs = [slot(k) for k in range(N_CHIPS)] + [slot(k) for k in range(3)]
    args = [p32] * N_CHIPS + [recv] * 3
    aliases = {}
    if buf is not None:
        in_specs.append(pl.BlockSpec(memory_space=pl.ANY))
        args.append(buf)
        aliases = {len(args) - 1: 0}
    return pl.pallas_call(
        body, name=name, grid=(nb,), in_specs=in_specs,
        out_specs=pl.BlockSpec((1, tr, c), lambda j: (layer, j, 0)),
        out_shape=jax.ShapeDtypeStruct((nl, r2, c), F32),
        input_output_aliases=aliases,
        compiler_params=_params(("parallel",)),
    )(*args)


def grads_rider_exchange(bufs):
    def copies(ins, outs, send_sems, recv_sems):
        x, y, c = _coords()
        sibling = (x, y, 1 - c)
        return [(_remote(ins[a], outs[a], send_sems, recv_sems, a, sibling),
                 _remote(outs[a], outs[a], send_sems, recv_sems, a, sibling)) for a in range(len(bufs))]

    return Rider(list(bufs), [jax.ShapeDtypeStruct(b.shape, b.dtype) for b in bufs], {}, len(bufs), copies)


def allreduce_small(buf, name):
    r, cdim = buf.shape

    def body(x_ref, o_ref, gath, send_sems, recv_sems):
        x, y, c = _coords()
        me, sibling = (x, y, c), (x, y, 1 - c)
        chips, _ = _ici_peers(x, y)

        def slot(px, py, pc):
            return gath.at[4 * px + 2 * py + pc]

        def copy(k, block, to, src=None):
            return pltpu.make_async_remote_copy(
                src_ref=slot(*block) if src is None else src, dst_ref=slot(*block),
                send_sem=send_sems.at[k], recv_sem=recv_sems.at[k], device_id=to, device_id_type=MESH)

        gath[4 * x + 2 * y + c] = x_ref[...]
        first = [copy(0, me, sibling, src=x_ref)]
        first += [copy(1 + j, me, (*chip, c), src=x_ref) for j, chip in enumerate(chips)]
        for cp in first:
            cp.start()
        passed = [copy(4 + j, (*chip, c), sibling) for j, chip in enumerate(chips)]
        for j, chip in enumerate(chips):
            copy(1 + j, (*chip, c), me).wait_recv()
            passed[j].start()
        copy(0, sibling, me).wait_recv()
        for j, chip in enumerate(chips):
            copy(4 + j, (*chip, 1 - c), me).wait_recv()
        for cp in first + passed:
            cp.wait_send()
        acc = gath[0]
        for d in range(1, 8):
            acc = acc + gath[d]
        o_ref[...] = acc

    return pl.pallas_call(
        body, name=name,
        in_specs=[pl.BlockSpec(memory_space=pltpu.VMEM)], out_specs=pl.BlockSpec(memory_space=pltpu.VMEM),
        out_shape=jax.ShapeDtypeStruct((r, cdim), F32),
        scratch_shapes=[pltpu.VMEM((8, r, cdim), F32), pltpu.SemaphoreType.DMA((7,)), pltpu.SemaphoreType.DMA((7,))],
        compiler_params=pltpu.CompilerParams(has_side_effects=True),
    )(buf)


def _expand_heads(v):
    return jnp.repeat(v.astype(F32), HEAD_DIM).reshape(1, -1)


def _pad_lanes(v):
    return jnp.pad(v.astype(F32), (0, LANES - v.shape[0])).reshape(1, LANES)


def local_step(x, tgt, p, comm, cols):
    nl = p["norm_mix_w"].shape[0]
    d = x.shape[1]
    cw = p["short_conv_w"].shape[2]
    si = p["ssd_norm_w"].shape[1]
    ff, npad = comm.ff, comm.npad
    nh = si // HEAD_DIM
    gn = GROUPS * STATE
    dt_off = 3 * cw + si + si + 2 * gn
    assert cols == dt_off + nh and nh <= LANES and dt_off % LANES == 0 and npad == dt_off + LANES
    pieces = [(0, cw), (cw, cw), (2 * cw, cw), (3 * cw, si), (3 * cw + si, si), (3 * cw + 2 * si, 2 * gn),
              (dt_off, LANES)]

    saved = []
    for l in range(nl):
        nw1 = p["norm_mix_w"][l].reshape(1, d)
        nw2 = p["norm_mlp_w"][l].reshape(1, d)
        kw3 = p["short_conv_w"][l]
        kwx, kwb = p["ssd_conv_w"][l][:, :si], p["ssd_conv_w"][l][:, si:]
        bx, bb = p["ssd_conv_b"][l][:si].reshape(1, si), p["ssd_conv_b"][l][si:].reshape(1, 2 * gn)
        dtb = _pad_lanes(p["dt_bias"][l])
        aneg = _pad_lanes(-jnp.exp(p["a_log"][l]))
        dsk = _expand_heads(p["d_skip"][l])
        snw = p["ssd_norm_w"][l].reshape(1, si)
        ssd_args = (kwx, kwb, bx, bb, dtb, aneg, dsk, snw)

        w_in = comm.weight(l, "w_in")
        (proj, h), sent = norm_matmul(x, nw1, w_in, 0, dt_off, 0, BF16, "in_proj", rider=comm.rider("in_proj", l))
        comm.done("in_proj", l, sent)
        (dt_raw,), _ = norm_matmul(x, nw1, w_in, 0, LANES, dt_off // LANES, F32, "dt_proj", emit_h=False)
        y_mix = conv_mixer_fwd(proj, kw3, cw, cw + si, "conv_mixer_fwd")
        (y_mix, *ssd_saved), sent = ssd_fwd(proj, dt_raw, y_mix, *ssd_args, cw, si, "ssd_fwd",
                                            rider=comm.rider("ssd_fwd", l))
        comm.done("ssd_fwd", l, sent)
        x2, _ = matmul(y_mix, comm.weight(l, "w_out"), 0, False, d, F32, "out_proj", residual=x)
        (up, h2), sent = norm_matmul(x2, nw2, comm.weight(l, "w_up"), 0, ff, 0, BF16, "up_proj",
                                     rider=comm.rider("up_proj", l))
        comm.done("up_proj", l, sent)
        x3, sent = matmul(up, comm.weight(l, "w_down"), 0, False, d, F32, "down_proj", lhs_fn=_relu2, residual=x2,
                          rider=comm.rider("down_proj", l))
        comm.done("down_proj", l, sent)
        saved.append((x, h, proj, dt_raw, y_mix, ssd_saved, x2, h2, up, nw1, nw2, kw3, ssd_args))
        x = x3

    dx, dwf, lvec = final_loss(x, p["final_norm_w"].reshape(1, d), tgt, "final_loss")
    loss = jnp.sum(lvec)

    names = ("norm_mix_w", "short_conv_w", "ssd_conv_w", "ssd_conv_b", "dt_bias", "a_log", "d_skip",
             "ssd_norm_w", "norm_mlp_w")
    grads = {k: [None] * nl for k in names}
    for l in reversed(range(nl)):
        x0, h, proj, dt_raw, y_mix, ssd_saved, x2, h2, up, nw1, nw2, kw3, ssd_args = saved[l]
        kwx, kwb, _, _, dtb, aneg, dsk, snw = ssd_args
        dup, sent = matmul(dx, comm.weight(l, "w_down"), 0, True, ff, BF16, "down_bwd", relu_gate=up,
                           rider=comm.rider("down_bwd", l))
        comm.done("down_bwd", l, sent)
        g_down, sums = matmul_tn(up, dx, "down_wgrad", a_fn=_relu2, chip_sums=comm.side("down_wgrad", l))
        comm.side_done("down_wgrad", l, sums)
        comm.take_gradient(l, "w_down", g_down)
        (dx2, dnw2), _ = matmul_normbwd([dup], [(0, ff)], comm.weight(l, "w_up"), 0, x2, nw2, dx, "up_bwd")
        comm.take_gradient(l, "w_up", matmul_tn(h2, dup, "up_wgrad", by_chip=True))
        grads["norm_mlp_w"][l] = dnw2.reshape(d)
        dy, sent = matmul(dx2, comm.weight(l, "w_out"), 0, True, cw + si, BF16, "out_bwd",
                          rider=comm.rider("out_bwd", l))
        comm.done("out_bwd", l, sent)
        g_out, sums = matmul_tn(y_mix, dx2, "out_wgrad", chip_sums=comm.side("out_wgrad", l))
        comm.side_done("out_wgrad", l, sums)
        comm.take_gradient(l, "w_out", g_out)
        dub, duc, duh, dkw3 = conv_mixer_bwd(proj, dy, kw3, cw, "conv_mixer_bwd")
        (dz, dxs, dbc, ddt, dkwx, dkwb, dbx, dbb, ddtb, da, ddsk, dsnw), sent = ssd_bwd(
            proj, dt_raw, *ssd_saved, dy, kwx, kwb, dtb, aneg, dsk, snw, cw, si, "ssd_bwd",
            rider=comm.rider("ssd_bwd", l))
        comm.done("ssd_bwd", l, sent)
        dpieces = [dub, duc, duh, dz, dxs, dbc, ddt]
        (dxl, dnw1), sent = matmul_normbwd(dpieces, pieces, comm.weight(l, "w_in"), 0, x0, nw1, dx2, "in_bwd",
                                           rider=comm.rider("in_bwd", l))
        comm.done("in_bwd", l, sent)
        comm.take_gradient(l, "w_in", split_to_chips(
            [matmul_tn(h, dp, "in_wgrad_%d" % i) for i, dp in enumerate(dpieces)], cols, "in_wgrad_split"))
        grads["norm_mix_w"][l] = dnw1.reshape(d)
        grads["short_conv_w"][l] = dkw3
        grads["ssd_conv_w"][l] = jnp.concatenate([dkwx, dkwb], axis=1)
        grads["ssd_conv_b"][l] = jnp.concatenate([dbx, dbb], axis=1).reshape(-1)
        grads["dt_bias"][l] = ddtb[0, :nh]
        grads["a_log"][l] = da[0, :nh] * aneg[0, :nh]
        grads["d_skip"][l] = jnp.sum(ddsk.reshape(nh, HEAD_DIM), axis=1)
        grads["ssd_norm_w"][l] = dsnw.reshape(si)
        dx = dxl

    grads = {k: jnp.stack(v) for k, v in grads.items()}
    grads["final_norm_w"] = dwf.reshape(d)
    return loss, dx, grads


BIG = ("w_in", "w_out", "w_up", "w_down")
SMALL_SHARDED = ("short_conv_w", "ssd_conv_w")
SMALL_REPL = ("norm_mix_w", "ssd_conv_b", "dt_bias", "a_log", "d_skip", "ssd_norm_w", "norm_mlp_w", "final_norm_w")
WEIGHTS = ("norm_mix_w", "w_in", "short_conv_w", "ssd_conv_w", "ssd_conv_b", "dt_bias", "a_log", "d_skip",
           "ssd_norm_w", "w_out", "norm_mlp_w", "w_up", "w_down", "final_norm_w")
SMALL_COLS = 1024


def _pack_small(named):
    flat = jnp.concatenate([v.reshape(-1).astype(F32) for v in named])
    n = flat.shape[0]
    rows = -(-n // SMALL_COLS)
    rows = -(-rows // 8) * 8
    return jnp.pad(flat, (0, rows * SMALL_COLS - n)).reshape(rows, SMALL_COLS)


def _unpack_small(buf, like):
    flat = buf.reshape(-1)
    out, off = [], 0
    for v in like:
        out.append(flat[off:off + v.size].reshape(v.shape))
        off += v.size
    return out


class ChipComm:
    IO = ("w_in", "w_out")
    MLP = ("w_up", "w_down")

    def __init__(self, shards, nl, npad):
        self.shards, self.nl, self.npad = shards, nl, npad
        self.how = dict(zip(BIG, LAYER_HOW))
        self.ff = N_CHIPS * shards["w_up"].shape[2]
        self.w = {}
        self.landed = {}
        self.grad = {}
        self.sums = {}
        self.from_sibling = {}
        self.bufs = {k: None for k in BIG}
        first = run_rider(self._ici(self.IO, 0), "gather_first_ici")
        self._gathered(self.IO, 0, run_rider(self._d2d(self.IO, first), "gather_first_d2d"))

    def _ici(self, group, l):
        return weight_rider_ici([self.shards[k] for k in group], [self.how[k] for k in group], l)

    def _d2d(self, group, landed):
        return weight_rider_d2d(landed, [tuple(self.shards[k].shape[1:]) for k in group],
                                [self.how[k] for k in group])

    def _gathered(self, group, l, arrays):
        for k, g in zip(group, arrays):
            self.w[(l, k)] = join_from_chips(g, self.npad, "w_in_join") if k == "w_in" else g

    def _to_sibling(self, group, l):
        return grads_rider_sibling([self.grad[(l, k)] for k in group])

    def _summed(self, group, l, from_sibling):
        self.sums[group] = (l, [chip_sum(self.grad.pop((l, k)), r, "chip_sum") for k, r in zip(group, from_sibling)])

    def side(self, point, l):
        group = self.IO if point == "down_wgrad" else self.MLP
        if group not in self.from_sibling:
            return ()
        layer, parts = self.from_sibling[group]
        return [(self.grad[(layer, k)], r) for k, r in zip(group, parts)]

    def side_done(self, point, l, sums):
        group = self.IO if point == "down_wgrad" else self.MLP
        if sums:
            layer, _ = self.from_sibling.pop(group)
            for k in group:
                del self.grad[(layer, k)]
            self.sums[group] = (layer, sums)

    def _to_chips(self, group):
        return grads_rider_chips([s[1] for s in self.sums[group][1]])

    def _reduced(self, group, from_chips):
        l, sums = self.sums.pop(group)
        for k, s, r in zip(group, sums, from_chips):
            self.bufs[k] = grad_sum(s[0], r, l, self.nl, self.bufs[k], "grad_sum")

    def weight(self, l, name):
        return self.w[(l, name)]

    def take_gradient(self, l, name, g):
        self.grad[(l, name)] = g if g.ndim == 3 else g.reshape(N_CHIPS, g.shape[0] // N_CHIPS, g.shape[1])

    def rider(self, point, l):
        more = l + 1 < self.nl
        if point == "in_proj":
            return self._ici(self.MLP, l)
        if point == "ssd_fwd":
            return self._d2d(self.MLP, self.landed[self.MLP])
        if point == "up_proj":
            return self._ici(self.IO, l + 1) if more else None
        if point == "down_proj":
            return self._d2d(self.IO, self.landed[self.IO]) if more else None
        if point == "down_bwd":
            return self._to_sibling(self.IO, l + 1) if more else None
        if point == "out_bwd":
            return self._to_sibling(self.MLP, l)
        if point == "ssd_bwd":
            return self._to_chips(self.IO) if more else None
        return self._to_chips(self.MLP)

    def done(self, point, l, results):
        if not results:
            return
        if point in ("in_proj", "up_proj"):
            self.landed[self.MLP if point == "in_proj" else self.IO] = results
        elif point == "ssd_fwd":
            self._gathered(self.MLP, l, results)
        elif point == "down_proj":
            self._gathered(self.IO, l + 1, results)
        elif point == "down_bwd":
            self.from_sibling[self.IO] = (l + 1, results)
        elif point == "out_bwd":
            self.from_sibling[self.MLP] = (l, results)
        elif point == "ssd_bwd":
            self._reduced(self.IO, results)
        else:
            self._reduced(self.MLP, results)

    def finish(self):
        self._summed(self.IO, 0, run_rider(self._to_sibling(self.IO, 0), "grads_to_sibling"))
        self._reduced(self.IO, run_rider(self._to_chips(self.IO), "grads_to_chips"))
        mine = [self.bufs[k] for k in BIG]
        theirs = run_rider(grads_rider_exchange(mine), "grads_exchange")
        return dict(zip(BIG, zip(mine, theirs)))


def kernel(x, norm_mix_w, w_in, short_conv_w, ssd_conv_w, ssd_conv_b, dt_bias, a_log, d_skip, ssd_norm_w, w_out, norm_mlp_w, w_up, w_down, final_norm_w, loss_target, m_norm_mix_w, m_w_in, m_short_conv_w, m_ssd_conv_w, m_ssd_conv_b, m_dt_bias, m_a_log, m_d_skip, m_ssd_norm_w, m_w_out, m_norm_mlp_w, m_w_up, m_w_down, m_final_norm_w, v_norm_mix_w, v_w_in, v_short_conv_w, v_ssd_conv_w, v_ssd_conv_b, v_dt_bias, v_a_log, v_d_skip, v_ssd_norm_w, v_w_out, v_norm_mlp_w, v_w_up, v_w_down, v_final_norm_w):
    w = dict(norm_mix_w=norm_mix_w, w_in=w_in, short_conv_w=short_conv_w, ssd_conv_w=ssd_conv_w,
             ssd_conv_b=ssd_conv_b, dt_bias=dt_bias, a_log=a_log, d_skip=d_skip, ssd_norm_w=ssd_norm_w, w_out=w_out,
             norm_mlp_w=norm_mlp_w, w_up=w_up, w_down=w_down, final_norm_w=final_norm_w)
    m = dict(norm_mix_w=m_norm_mix_w, w_in=m_w_in, short_conv_w=m_short_conv_w, ssd_conv_w=m_ssd_conv_w,
             ssd_conv_b=m_ssd_conv_b, dt_bias=m_dt_bias, a_log=m_a_log, d_skip=m_d_skip, ssd_norm_w=m_ssd_norm_w,
             w_out=m_w_out, norm_mlp_w=m_norm_mlp_w, w_up=m_w_up, w_down=m_w_down, final_norm_w=m_final_norm_w)
    v = dict(norm_mix_w=v_norm_mix_w, w_in=v_w_in, short_conv_w=v_short_conv_w, ssd_conv_w=v_ssd_conv_w,
             ssd_conv_b=v_ssd_conv_b, dt_bias=v_dt_bias, a_log=v_a_log, d_skip=v_d_skip, ssd_norm_w=v_ssd_norm_w,
             w_out=v_w_out, norm_mlp_w=v_norm_mlp_w, w_up=v_w_up, w_down=v_w_down, final_norm_w=v_final_norm_w)
    xi, yi, ci = lax.axis_index("x"), lax.axis_index("y"), lax.axis_index("c")
    chip = 2 * xi + yi
    nl = w_up.shape[0]
    cols = N_CHIPS * w_in.shape[2]
    npad = cols // LANES * LANES + LANES

    full = dict(w)
    small_gathered = gather_weights([w[k] for k in SMALL_SHARDED], ["lead"] * len(SMALL_SHARDED), "gather_small")
    for k, g4 in zip(SMALL_SHARDED, small_gathered):
        full[k] = jnp.concatenate([g4[j] for j in range(N_CHIPS)], axis=2)
    comm = ChipComm({k: w[k].astype(BF16) for k in BIG}, nl, npad)

    loss, grad_x, grads = local_step(x[0], loss_target[0], full, comm, cols)
    loss = lax.psum(loss, ("x", "y", "c"))
    halves = comm.finish()
    g_shard = {}

    small_names = SMALL_REPL + SMALL_SHARDED
    small_sum = allreduce_small(_pack_small([grads[k] for k in small_names]), "allreduce_small")
    for k, g in zip(small_names, _unpack_small(small_sum, [grads[k] for k in small_names])):
        if k in SMALL_SHARDED:
            width = w[k].shape[2]
            g = lax.dynamic_slice_in_dim(g, chip * width, width, axis=2)
        g_shard[k] = g

    delta, new_m, new_v = {}, {}, {}
    for k in BIG:
        g_shard[k], delta[k], new_m[k], new_v[k] = adamw_halves(w[k], *halves[k], m[k], v[k], "adamw_%s" % k)
    packed = [_pack_small([d_[k] for k in small_names]) for d_ in (w, g_shard, m, v)]
    outs = adamw(*packed, "adamw_small")
    for d_, buf in zip((delta, new_m, new_v), outs):
        for k, val in zip(small_names, _unpack_small(buf, [w[k] for k in small_names])):
            d_[k] = val

    return (loss, grad_x[None], *[g_shard[k] for k in WEIGHTS], *[delta[k] for k in WEIGHTS],
            *[new_m[k] for k in WEIGHTS], *[new_v[k] for k in WEIGHTS])
```

```python
import functools

import jax
import jax.numpy as jnp
from jax import lax
from jax.experimental import pallas as pl
from jax.experimental.pallas import tpu as pltpu

F32 = jnp.float32
BF16 = jnp.bfloat16

EPS = 1e-5
HEAD_DIM = 64
STATE = 128
GROUPS = 2
SHORT_K = 3
SSD_K = 4
LANES = 128
PAIR = LANES // HEAD_DIM
SCAN_CHUNK = 256
HALO = 16
N_CHIPS = 4
VMEM_LIMIT = 56 * 1024 * 1024

ADAM_LR = 0.001
ADAM_B1 = 0.9
ADAM_B2 = 0.999
ADAM_EPS = 1e-08
ADAM_WD = 0.01
ADAM_STEP = 10

MESH = pl.DeviceIdType.MESH


def _params(sem):
    return pltpu.CompilerParams(dimension_semantics=sem, vmem_limit_bytes=VMEM_LIMIT)


def _tile(n, cap, quantum):
    if n <= cap:
        return n
    best = None
    for t in range(quantum, cap + 1, quantum):
        if n % t == 0:
            best = t
    assert best is not None, (n, cap, quantum)
    return best


def _dot(a, b):
    return jnp.dot(a, b, preferred_element_type=F32)


def _dot_nt(a, b):
    return lax.dot_general(a, b, (((1,), (1,)), ((), ())), preferred_element_type=F32)


def _dot_tn(a, b):
    return lax.dot_general(a, b, (((0,), (0,)), ((), ())), preferred_element_type=F32)


def _dot_exact(a, b):
    return jnp.dot(a, b, precision=lax.Precision.HIGHEST, preferred_element_type=F32)


def _sigmoid(x):
    return pl.reciprocal(1.0 + jnp.exp(-x), approx=True)


def _softplus(x):
    return jnp.maximum(x, 0.0) + jnp.log(1.0 + jnp.exp(-jnp.abs(x)))


def _relu2(v):
    return jnp.square(jnp.maximum(v, 0.0))


class Rider:
    def __init__(self, ins, out_shapes, aliases, n_sems, copies):
        self.ins, self.out_shapes, self.aliases, self.n_sems, self.copies = ins, out_shapes, aliases, n_sems, copies


def _any_specs(n):
    return [pl.BlockSpec(memory_space=pl.ANY)] * n


def _ride(body, grid, in_specs, out_specs, out_shape, scratch, args, aliases, rider, sem, name):
    n_in, n_out, n_scr = len(in_specs), len(out_specs), len(scratch)
    if rider is None:
        outs = pl.pallas_call(
            body, name=name, grid=grid, in_specs=in_specs, out_specs=out_specs, out_shape=out_shape,
            scratch_shapes=scratch, input_output_aliases=aliases, compiler_params=_params(sem))(*args)
        return list(outs), []
    ri, ro = len(rider.ins), len(rider.out_shapes)
    last = tuple(g - 1 for g in grid)

    def wrapped(*refs):
        ins = refs[:n_in]
        r_ins = refs[n_in:n_in + ri]
        outs = refs[n_in + ri:n_in + ri + n_out]
        r_outs = refs[n_in + ri + n_out:n_in + ri + n_out + ro]
        scr = refs[n_in + ri + n_out + ro:n_in + ri + n_out + ro + n_scr]
        send_sems, recv_sems = refs[-2:]
        ids = [pl.program_id(a) for a in range(len(grid))]
        at_first = functools.reduce(jnp.logical_and, [i == 0 for i in ids])
        at_last = functools.reduce(jnp.logical_and, [i == e for i, e in zip(ids, last)])

        @pl.when(at_first)
        def _():
            for cp, _ in rider.copies(r_ins, r_outs, send_sems, recv_sems):
                cp.start()

        body(*ins, *outs, *scr)

        @pl.when(at_last)
        def _():
            for cp, landed in rider.copies(r_ins, r_outs, send_sems, recv_sems):
                cp.wait_send()
                landed.wait_recv()

    all_aliases = dict(aliases)
    all_aliases.update({n_in + a: n_out + b for a, b in rider.aliases.items()})
    outs = pl.pallas_call(
        wrapped, name=name, grid=grid, in_specs=list(in_specs) + _any_specs(ri),
        out_specs=list(out_specs) + _any_specs(ro), out_shape=list(out_shape) + list(rider.out_shapes),
        scratch_shapes=list(scratch) + [pltpu.SemaphoreType.DMA((rider.n_sems,)),
                                        pltpu.SemaphoreType.DMA((rider.n_sems,))],
        input_output_aliases=all_aliases, compiler_params=_params(sem))(*args, *rider.ins)
    return list(outs[:n_out]), list(outs[n_out:])


def run_rider(rider, name):
    ri, ro = len(rider.ins), len(rider.out_shapes)

    def body(*refs):
        send_sems, recv_sems = refs[-2:]
        pairs = rider.copies(refs[:ri], refs[ri:ri + ro], send_sems, recv_sems)
        for cp, _ in pairs:
            cp.start()
        for cp, landed in pairs:
            cp.wait_send()
            landed.wait_recv()

    return list(pl.pallas_call(
        body, name=name, in_specs=_any_specs(ri), out_specs=_any_specs(ro), out_shape=list(rider.out_shapes),
        scratch_shapes=[pltpu.SemaphoreType.DMA((rider.n_sems,)), pltpu.SemaphoreType.DMA((rider.n_sems,))],
        input_output_aliases=dict(rider.aliases),
        compiler_params=pltpu.CompilerParams(has_side_effects=True))(*rider.ins))


def norm_matmul(x, nw, w, layer, n, out_dtype, name, tail_block=None, rider=None):
    t, d = x.shape
    mxu_cols = 2 * LANES
    tn = _tile(n, 1536, mxu_cols if n % mxu_cols == 0 else LANES)
    if n % mxu_cols == 0 and tn < 1024 <= n:
        tn = _tile(n, 3072, mxu_cols)
    tm = _tile(t, 512 if tn > 1536 else 1024, 8)
    nj = n // tn

    def body(x_ref, nw_ref, w_ref, *rest):
        if tail_block is None:
            o_ref, h_ref = rest
        else:
            wt_ref, o_ref, h_ref, tail_ref = rest

        @pl.when(pl.program_id(1) == 0)
        def _():
            xf = x_ref[...]
            r = lax.rsqrt(jnp.mean(xf * xf, axis=-1, keepdims=True) + EPS)
            h_ref[...] = (xf * r * nw_ref[...]).astype(BF16)

        o_ref[...] = _dot(h_ref[...], w_ref[...]).astype(out_dtype)
        if tail_block is not None:
            @pl.when(pl.program_id(1) == nj - 1)
            def _():
                tail_ref[...] = _dot(h_ref[...], wt_ref[...])

    in_specs = [pl.BlockSpec((tm, d), lambda i, j: (i, 0)), pl.BlockSpec((1, d), lambda i, j: (0, 0)),
                pl.BlockSpec((None, d, tn), lambda i, j: (layer, 0, j))]
    out_specs = [pl.BlockSpec((tm, tn), lambda i, j: (i, j)), pl.BlockSpec((tm, d), lambda i, j: (i, 0))]
    out_shape = [jax.ShapeDtypeStruct((t, n), out_dtype), jax.ShapeDtypeStruct((t, d), BF16)]
    args = [x, nw, w]
    if tail_block is not None:
        in_specs.append(pl.BlockSpec((None, d, LANES), lambda i, j: (layer, 0, tail_block)))
        out_specs.append(pl.BlockSpec((tm, LANES), lambda i, j: (i, 0)))
        out_shape.append(jax.ShapeDtypeStruct((t, LANES), F32))
        args.append(w)
    return _ride(body, (t // tm, nj), in_specs, out_specs, out_shape, [], args, {}, rider,
                 ("parallel", "arbitrary"), name)


def matmul(lhs, w, layer, transposed, n, out_dtype, name, *, lhs_fn=None, residual=None, relu_gate=None,
           rider=None):
    t, k = lhs.shape
    tm = _tile(t, 512 if k > 2048 else 1024, 8)
    tn = _tile(n, 1024, LANES)
    staged = lhs.dtype != BF16 or lhs_fn is not None
    fn = lhs_fn if lhs_fn is not None else (lambda v: v)
    has_extra = residual is not None or relu_gate is not None
    dot = _dot_nt if transposed else _dot

    def body(*refs):
        a_ref, w_ref = refs[:2]
        extra = refs[2] if has_extra else None
        o_ref = refs[3] if has_extra else refs[2]
        if staged:
            s_ref = refs[-1]

            @pl.when(pl.program_id(1) == 0)
            def _():
                s_ref[...] = fn(a_ref[...].astype(F32)).astype(BF16)

            a_ref = s_ref
        acc = dot(a_ref[...], w_ref[...])
        if residual is not None:
            acc = acc + extra[...]
        if relu_gate is not None:
            acc = acc * (2.0 * jnp.maximum(extra[...].astype(F32), 0.0))
        o_ref[...] = acc.astype(out_dtype)

    if transposed:
        w_spec = pl.BlockSpec((None, tn, k), lambda i, j: (layer, j, 0))
    else:
        w_spec = pl.BlockSpec((None, k, tn), lambda i, j: (layer, 0, j))
    in_specs = [pl.BlockSpec((tm, k), lambda i, j: (i, 0)), w_spec]
    args = [lhs, w]
    if has_extra:
        in_specs.append(pl.BlockSpec((tm, tn), lambda i, j: (i, j)))
        args.append(residual if residual is not None else relu_gate)
    outs, extra = _ride(
        body, (t // tm, n // tn), in_specs, [pl.BlockSpec((tm, tn), lambda i, j: (i, j))],
        [jax.ShapeDtypeStruct((t, n), out_dtype)], [pltpu.VMEM((tm, k), BF16)] if staged else [], args, {},
        rider, ("parallel", "arbitrary"), name)
    return outs[0], extra


def matmul_normbwd(lhs, pieces, w, layer, x, nw, dres, name, rider=None):
    t, d = x.shape
    nl = len(lhs)
    tm = _tile(t, 512, 8)
    for off, width in pieces:
        assert off % width == 0

    def body(*refs):
        lrefs = refs[:nl]
        wrefs = refs[nl:2 * nl]
        x_ref, nw_ref, dres_ref, dx_ref, dnw_ref = refs[2 * nl:]
        dh = _dot_nt(lrefs[0][...].astype(BF16), wrefs[0][...])
        for a_ref, w_ref in zip(lrefs[1:], wrefs[1:]):
            dh = dh + _dot_nt(a_ref[...].astype(BF16), w_ref[...])
        xf = x_ref[...]
        r = lax.rsqrt(jnp.mean(xf * xf, axis=-1, keepdims=True) + EPS)
        nx = xf * r
        dn = dh * nw_ref[...]
        dx = r * (dn - nx * jnp.mean(dn * nx, axis=-1, keepdims=True))
        dx_ref[...] = dres_ref[...] + dx

        @pl.when(pl.program_id(0) == 0)
        def _():
            dnw_ref[...] = jnp.zeros_like(dnw_ref)

        dnw_ref[...] += jnp.sum(dh * nx, axis=0, keepdims=True)

    in_specs = [pl.BlockSpec((tm, width), (lambda blk: (lambda i: (i, blk)))(blk))
                for (_, blk), (_, width) in zip(lhs, pieces)]
    in_specs += [pl.BlockSpec((None, d, width), (lambda blk: (lambda i: (layer, 0, blk)))(off // width),
                              pipeline_mode=pl.Buffered(1))
                 for off, width in pieces]
    in_specs += [pl.BlockSpec((tm, d), lambda i: (i, 0)), pl.BlockSpec((1, d), lambda i: (0, 0)),
                 pl.BlockSpec((tm, d), lambda i: (i, 0))]
    return _ride(
        body, (t // tm,), in_specs,
        [pl.BlockSpec((tm, d), lambda i: (i, 0)), pl.BlockSpec((1, d), lambda i: (0, 0))],
        [jax.ShapeDtypeStruct((t, d), F32), jax.ShapeDtypeStruct((1, d), F32)], [],
        [*[a for a, _ in lhs], *([w] * nl), x, nw, dres], {}, rider, ("arbitrary",), name)


def matmul_tn(a, b, name, *, a_fn=None, by_chip=False, chip_sums=None):
    t, k = a.shape
    n = b.shape[1]
    tk = _tile(k, 1024, LANES)
    nn = n // N_CHIPS if by_chip else n
    tn = _tile(nn, 1024, 2 * LANES if nn % (2 * LANES) == 0 else LANES)
    tt = _tile(t, 1024, 8)
    nt = t // tt
    gn_ = n // tn
    steps = (k // tk) * gn_ * nt
    fn = a_fn if a_fn is not None else (lambda v: v)
    with_sums, chip_sums = chip_sums, list(chip_sums or ())
    for g, _ in chip_sums:
        if (g.shape[0] * g.shape[1] // 2) % (steps * HALO) or steps % g.shape[0]:
            return (matmul_tn(a, b, name, a_fn=a_fn, by_chip=by_chip),
                    [tuple(chip_sum(g_, r_, "chip_sum")) for g_, r_ in chip_sums])
    ns = len(chip_sums)
    step = lambda i, j, s: (i * gn_ + j) * nt + s

    def body(*refs):
        a_ref, b_ref = refs[:2]
        side_in = refs[2:2 + 3 * ns]
        o_ref = refs[2 + 3 * ns]
        side_out = refs[3 + 3 * ns:3 + 5 * ns]
        acc_ref = refs[-1]

        @pl.when(pl.program_id(2) == 0)
        def _():
            acc_ref[...] = jnp.zeros_like(acc_ref)

        av = a_ref[...]
        if a_fn is not None:
            av = fn(av.astype(F32))
        acc_ref[...] += _dot_tn(av.astype(BF16), b_ref[...].astype(BF16))
        for q in range(ns):
            g0_ref, g1_ref, r_ref = side_in[3 * q:3 * q + 3]
            tot = jnp.where(lax.axis_index("c") == 0, g0_ref[...], g1_ref[...]) + r_ref[...]
            side_out[2 * q][...] = tot
            side_out[2 * q + 1][...] = tot.astype(BF16)

        @pl.when(pl.program_id(2) == nt - 1)
        def _():
            o_ref[...] = acc_ref[...]

    if by_chip:
        per = n // N_CHIPS // tn
        out_specs = [pl.BlockSpec((None, tk, tn), lambda i, j, s: (j // per, i, j % per))]
        out_shape = [jax.ShapeDtypeStruct((N_CHIPS, k, n // N_CHIPS), F32)]
    else:
        out_specs = [pl.BlockSpec((tk, tn), lambda i, j, s: (i, j))]
        out_shape = [jax.ShapeDtypeStruct((k, n), F32)]
    in_specs = [pl.BlockSpec((tt, tk), lambda i, j, s: (s, i)), pl.BlockSpec((tt, tn), lambda i, j, s: (s, j))]
    args = [a, b]
    for g, recv in chip_sums:
        nch, r, c = g.shape
        r2 = r // 2
        rows = nch * r2 // steps
        per_chip = r2 // rows
        assert rows % HALO == 0 and r2 % rows == 0
        for half in range(2):
            in_specs.append(pl.BlockSpec(
                (rows, c), (lambda h: (lambda i, j, s: ((step(i, j, s) // per_chip) * 2 * per_chip + h * per_chip
                                                        + step(i, j, s) % per_chip, 0)))(half)))
        flat = pl.BlockSpec((rows, c), lambda i, j, s: (step(i, j, s), 0))
        in_specs.append(flat)
        args += [g.reshape(nch * r, c), g.reshape(nch * r, c), recv.reshape(nch * r2, c)]
        out_specs += [flat, flat]
        out_shape += [jax.ShapeDtypeStruct((nch * r2, c), F32), jax.ShapeDtypeStruct((nch * r2, c), BF16)]
    outs = pl.pallas_call(
        body, name=name, grid=(k // tk, gn_, nt), in_specs=in_specs, out_specs=out_specs, out_shape=out_shape,
        scratch_shapes=[pltpu.VMEM((tk, tn), F32)],
        compiler_params=_params(("parallel", "parallel", "arbitrary")),
    )(*args)
    if with_sums is None:
        return outs[0]
    sums = [(outs[1 + 2 * q].reshape(g.shape[0], g.shape[1] // 2, g.shape[2]),
             outs[2 + 2 * q].reshape(g.shape[0], g.shape[1] // 2, g.shape[2])) for q, (g, _) in enumerate(chip_sums)]
    return outs[0], sums


def split_to_chips(pieces, cols, name):
    d = pieces[0].shape[0]
    widths = [p.shape[1] for p in pieces]
    w = cols // N_CHIPS
    tr = _tile(d, 256, 8)
    npc = len(pieces)

    def body(*refs):
        o_ref, row = refs[npc], refs[npc + 1]
        off = 0
        for r, n in zip(refs[:npc], widths):
            row[:, off:off + n] = r[...]
            off += n
        for j in range(N_CHIPS):
            o_ref[j] = row[:, j * w:(j + 1) * w]

    return pl.pallas_call(
        body, name=name, grid=(d // tr,),
        in_specs=[pl.BlockSpec((tr, n), lambda i: (i, 0)) for n in widths],
        out_specs=pl.BlockSpec((N_CHIPS, tr, w), lambda i: (0, i, 0)),
        out_shape=jax.ShapeDtypeStruct((N_CHIPS, d, w), F32),
        scratch_shapes=[pltpu.VMEM((tr, sum(widths)), F32)],
        compiler_params=_params(("parallel",)),
    )(*pieces)


def join_from_chips(g4, npad, name):
    _, nl, d, w = g4.shape
    tr = _tile(d, 256, HALO)

    def body(g_ref, o_ref):
        for j in range(N_CHIPS):
            o_ref[:, j * w:(j + 1) * w] = g_ref[j]
        o_ref[:, N_CHIPS * w:] = jnp.zeros((tr, npad - N_CHIPS * w), o_ref.dtype)

    return pl.pallas_call(
        body, name=name, grid=(nl, d // tr),
        in_specs=[pl.BlockSpec((N_CHIPS, None, tr, w), lambda l, i: (0, l, i, 0))],
        out_specs=pl.BlockSpec((None, tr, npad), lambda l, i: (l, i, 0)),
        out_shape=jax.ShapeDtypeStruct((nl, d, npad), g4.dtype),
        compiler_params=_params(("parallel", "parallel")),
    )(g4)


def conv_mixer_fwd(proj, kw, cw, out_cols, name):
    t = proj.shape[0]
    tm = _tile(t, 1024, HALO)
    tc = _tile(cw, 1024, LANES)
    nj = cw // tc
    hb = tm // HALO

    def body(ub_ref, uc_ref, uh_ref, ucp_ref, uhp_ref, kw_ref, y_ref):
        i = pl.program_id(0)
        taps = [kw_ref[pl.ds(k, 1), :] for k in range(SHORT_K)]
        row = lax.broadcasted_iota(jnp.int32, (8, tc), 0)
        vp = ucp_ref[...].astype(F32) * uhp_ref[...].astype(F32)

        def conv(block, before):
            acc = taps[SHORT_K - 1] * block
            for k in range(SHORT_K - 1):
                s = SHORT_K - 1 - k
                acc = acc + taps[k] * jnp.where(row >= s, pltpu.roll(block, s, 0), pltpu.roll(before, s, 0))
            return acc

        def strip(s, before):
            rows = pl.ds(pl.multiple_of(s * HALO, HALO), HALO)
            v = uc_ref[rows, :].astype(F32) * uh_ref[rows, :].astype(F32)
            top, bottom = v[0:8], v[8:HALO]
            cv = jnp.concatenate([conv(top, before), conv(bottom, top)], axis=0)
            y_ref[rows, :] = (ub_ref[rows, :].astype(F32) * cv).astype(BF16)
            return bottom

        lax.fori_loop(0, tm // HALO, strip, jnp.where(i > 0, vp[8:HALO], 0.0))

    prev = lambda off: (lambda i, j: (jnp.maximum(i * hb - 1, 0), off + j))
    return pl.pallas_call(
        body, name=name, grid=(t // tm, nj),
        in_specs=[pl.BlockSpec((tm, tc), lambda i, j: (i, j)),
                  pl.BlockSpec((tm, tc), lambda i, j: (i, nj + j)),
                  pl.BlockSpec((tm, tc), lambda i, j: (i, 2 * nj + j)),
                  pl.BlockSpec((HALO, tc), prev(nj)),
                  pl.BlockSpec((HALO, tc), prev(2 * nj)),
                  pl.BlockSpec((SHORT_K, tc), lambda i, j: (0, j))],
        out_specs=pl.BlockSpec((tm, tc), lambda i, j: (i, j)),
        out_shape=jax.ShapeDtypeStruct((t, out_cols), BF16),
        compiler_params=_params(("parallel", "parallel")),
    )(proj, proj, proj, proj, proj, kw)


def conv_mixer_bwd(proj, dy, kw, cw, name):
    t = proj.shape[0]
    tm = _tile(t, 1024, HALO)
    tc = cw
    nj = cw // tc
    hb = tm // HALO
    ni = t // tm
    last_hb = t // HALO - 1

    def body(ub_ref, uc_ref, uh_ref, dy_ref, ucp_ref, uhp_ref, ubn_ref, dyn_ref, kw_ref,
             du_ref, dkw_ref):
        i = pl.program_id(1)
        nstrips = tm // HALO
        taps = [kw_ref[pl.ds(k, 1), :] for k in range(SHORT_K)]
        row = lax.broadcasted_iota(jnp.int32, (8, tc), 0)
        vp = ucp_ref[...].astype(F32) * uhp_ref[...].astype(F32)
        dcvn = dyn_ref[...].astype(F32) * ubn_ref[...].astype(F32)

        def shifted(block, before, s):
            return jnp.where(row >= s, pltpu.roll(block, s, 0), pltpu.roll(before, s, 0))

        def lifted(block, after, s):
            return jnp.where(row < 8 - s, pltpu.roll(block, 8 - s, 0), pltpu.roll(after, 8 - s, 0))

        def down(s, carry):
            before, sums = carry
            rows = pl.ds(pl.multiple_of(s * HALO, HALO), HALO)
            v = uc_ref[rows, :].astype(F32) * uh_ref[rows, :].astype(F32)
            dyv = dy_ref[rows, :].astype(F32)
            dcv = dyv * ub_ref[rows, :].astype(F32)
            cvs = []
            sums = list(sums)
            for block, above, dcb in ((v[0:8], before, dcv[0:8]), (v[8:HALO], v[0:8], dcv[8:HALO])):
                moved = [shifted(block, above, SHORT_K - 1 - k) for k in range(SHORT_K - 1)] + [block]
                cvs.append(sum(taps[k] * moved[k] for k in range(SHORT_K)))
                sums = [sums[k] + dcb * moved[k] for k in range(SHORT_K)]
            du_ref[rows, 0:cw] = (dyv * jnp.concatenate(cvs, axis=0)).astype(BF16)
            return v[8:HALO], tuple(sums)

        zero = jnp.zeros((8, tc), F32)
        _, sums = lax.fori_loop(0, nstrips, down, (jnp.where(i > 0, vp[8:HALO], 0.0), (zero,) * SHORT_K))

        def up(n, after):
            rows = pl.ds(pl.multiple_of((nstrips - 1 - n) * HALO, HALO), HALO)
            uc = uc_ref[rows, :].astype(F32)
            uh = uh_ref[rows, :].astype(F32)
            dcv = dy_ref[rows, :].astype(F32) * ub_ref[rows, :].astype(F32)
            dvs = []
            for block, below in ((dcv[0:8], dcv[8:HALO]), (dcv[8:HALO], after)):
                dvs.append(taps[SHORT_K - 1] * block
                           + sum(taps[k] * lifted(block, below, SHORT_K - 1 - k) for k in range(SHORT_K - 1)))
            dv = jnp.concatenate(dvs, axis=0)
            du_ref[rows, cw:2 * cw] = (dv * uh).astype(BF16)
            du_ref[rows, 2 * cw:3 * cw] = (dv * uc).astype(BF16)
            return dcv[0:8]

        lax.fori_loop(0, nstrips, up, jnp.where(i < ni - 1, dcvn[0:8], 0.0))

        @pl.when(i == 0)
        def _():
            dkw_ref[...] = jnp.zeros_like(dkw_ref)

        for k in range(SHORT_K):
            dkw_ref[pl.ds(k, 1), :] += jnp.sum(sums[k], axis=0, keepdims=True)

    prev = lambda off: (lambda j, i: (jnp.maximum(i * hb - 1, 0), off + j))
    nxt = lambda off: (lambda j, i: (jnp.minimum((i + 1) * hb, last_hb), off + j))
    cur = lambda off: (lambda j, i: (i, off + j))
    return pl.pallas_call(
        body, name=name, grid=(nj, ni),
        in_specs=[pl.BlockSpec((tm, tc), cur(0)), pl.BlockSpec((tm, tc), cur(nj)),
                  pl.BlockSpec((tm, tc), cur(2 * nj)), pl.BlockSpec((tm, tc), cur(0)),
                  pl.BlockSpec((HALO, tc), prev(nj)), pl.BlockSpec((HALO, tc), prev(2 * nj)),
                  pl.BlockSpec((HALO, tc), nxt(0)), pl.BlockSpec((HALO, tc), nxt(0)),
                  pl.BlockSpec((SHORT_K, tc), lambda j, i: (0, j))],
        out_specs=[pl.BlockSpec((tm, 3 * cw), lambda j, i: (i, 0)),
                   pl.BlockSpec((SHORT_K, tc), lambda j, i: (0, j))],
        out_shape=[jax.ShapeDtypeStruct((t, 3 * cw), BF16), jax.ShapeDtypeStruct((SHORT_K, cw), F32)],
        compiler_params=_params(("parallel", "arbitrary")),
    )(proj, proj, proj, dy, proj, proj, proj, dy, kw)


def _head_column(mat, lane, h):
    return jnp.sum(jnp.where(lane == h, mat, 0.0), axis=-1, keepdims=True)


def _ssd_common(dt_raw_ref, dtb_ref, aneg_ref, cum_s, cumt_s, chunk):
    dt = _softplus(dt_raw_ref[...] + dtb_ref[...])
    al = dt * aneg_ref[...]
    ri = lax.broadcasted_iota(jnp.int32, (chunk, chunk), 0)
    ci = lax.broadcasted_iota(jnp.int32, (chunk, chunk), 1)
    cum = _dot_exact((ri >= ci).astype(F32), al)
    cum_s[...] = cum
    cumt_s[...] = cum.T
    return dt, cum, ri >= ci


EDGE = 16


def _shift_matrices(shift_s, chunk, kk, up):
    ri = lax.broadcasted_iota(jnp.int32, (chunk, chunk), 0)
    ci = lax.broadcasted_iota(jnp.int32, (chunk, chunk), 1)
    for k in range(kk - 1):
        s = kk - 1 - k
        shift_s[k] = ((ci - ri if up else ri - ci) == s).astype(BF16)


def _causal_conv(cur, head, kw_ref, b_ref, shift_s, kk):
    acc = b_ref[...] + kw_ref[pl.ds(kk - 1, 1), :] * cur.astype(F32)
    top = b_ref[...] + kw_ref[pl.ds(kk - 1, 1), :] * head[pl.ds(8, EDGE), :]
    for k in range(kk - 1):
        acc = acc + kw_ref[pl.ds(k, 1), :] * _dot(shift_s[k], cur)
        top = top + kw_ref[pl.ds(k, 1), :] * head[pl.ds(8 - (kk - 1) + k, EDGE), :]
    return acc, top


def ssd_fwd(proj, dt_raw, y_mix, kw_xs, kw_bc, b_xs, b_bc, dtb, aneg, dskip, normw, cw, si, name, rider=None):
    t = proj.shape[0]
    ch = min(SCAN_CHUNK, t)
    nc = t // ch
    npair = si // LANES
    ppg = npair // GROUPS
    gn = GROUPS * STATE
    gw = si // GROUPS
    assert cw == si and (3 * cw + 2 * si) % (2 * gn) == 0
    zblk = 3 * cw // si
    xsblk = zblk + 1
    bcblk = (3 * cw + 2 * si) // (2 * gn)

    def body(z_ref, xs_ref, bc_ref, dtr_ref, ymix_ref, kwx_ref, kwb_ref, bx_ref, bb_ref, dtb_ref, aneg_ref, dsk_ref,
             nw_ref, yb_ref, ys_ref, hs_ref, xcx_ref, xcb_ref,
             headx, headb, shift_s, xs_s, bc_s, h_s, gated_s, s_s, cum_s, cumt_s):
        del ymix_ref
        c = pl.program_id(0)

        @pl.when(c == 0)
        def _():
            h_s[...] = jnp.zeros_like(h_s)
            headx[0:8, :] = jnp.zeros((8, si), F32)
            headb[0:8, :] = jnp.zeros((8, 2 * gn), F32)
            _shift_matrices(shift_s, ch, SSD_K, up=False)

        for raw_ref, head, kw_ref, b_ref, pre_ref, act_s in ((xs_ref, headx, kwx_ref, bx_ref, xcx_ref, xs_s),
                                                           (bc_ref, headb, kwb_ref, bb_ref, xcb_ref, bc_s)):
            head[8:8 + EDGE, :] = raw_ref[0:EDGE, :].astype(F32)
            pre, top = _causal_conv(raw_ref[...], head, kw_ref, b_ref, shift_s, SSD_K)
            head[0:8, :] = raw_ref[ch - EDGE:ch, :].astype(F32)[EDGE - 8:EDGE]
            pre_ref[...] = pre.astype(BF16)
            pre_ref[0:EDGE, :] = top.astype(BF16)
            act_s[...] = (pre * _sigmoid(pre)).astype(act_s.dtype)
            act_s[0:EDGE, :] = (top * _sigmoid(top)).astype(act_s.dtype)

        dt, cum, tril = _ssd_common(dtr_ref, dtb_ref, aneg_ref, cum_s, cumt_s, ch)
        lane = lax.broadcasted_iota(jnp.int32, (ch, LANES), 1)
        lane1 = lax.broadcasted_iota(jnp.int32, (1, LANES), 1)
        low = lane < HEAD_DIM
        clast = cum_s[pl.ds(ch - 1, 1), :]

        for p in range(npair):
            g = p // ppg
            col = slice(p * LANES, (p + 1) * LANES)
            bg = bc_s[:, g * STATE:(g + 1) * STATE]
            cg = bc_s[:, gn + g * STATE:gn + (g + 1) * STATE]
            if p % ppg == 0:
                s_s[...] = _dot_nt(cg, bg)
            heads = (PAIR * p, PAIR * p + 1)
            ccol = [_head_column(cum, lane, h) for h in heads]
            dcol = [_head_column(dt, lane, h) for h in heads]
            cl = [jnp.sum(jnp.where(lane1 == h, clast, 0.0), axis=-1, keepdims=True) for h in heads]
            cum_px = jnp.where(low, ccol[0], ccol[1])
            dt_px = jnp.where(low, dcol[0], dcol[1])
            cl_px = jnp.where(lane1 < HEAD_DIM, cl[0], cl[1])
            xs_p = xs_s[:, col]
            xdt = xs_p * dt_px
            y = dsk_ref[:, col] * xs_p
            for hi, h in enumerate(heads):
                dec = jnp.exp(jnp.where(tril, ccol[hi] - cumt_s[pl.ds(h, 1), :], -jnp.inf))
                wm = (s_s[...] * dec).astype(BF16)
                xm = jnp.where(low if hi == 0 else jnp.logical_not(low), xdt, 0.0).astype(BF16)
                y = y + _dot(wm, xm)
            hp = h_s[p]
            hs_ref[0, p] = hp
            y = y + _dot(cg, hp.astype(BF16)) * jnp.exp(cum_px)
            st = _dot_tn(bg, (xdt * jnp.exp(cl_px - cum_px)).astype(BF16))
            h_s[p] = jnp.exp(cl_px) * hp + st
            ys_ref[:, col] = y.astype(BF16)
            zp = z_ref[:, col].astype(F32)
            gated_s[:, col] = y * zp * _sigmoid(zp)

        for g in range(GROUPS):
            col = slice(g * gw, (g + 1) * gw)
            gg = gated_s[:, col]
            r = lax.rsqrt(jnp.mean(gg * gg, axis=-1, keepdims=True) + EPS)
            yb_ref[:, col] = (gg * r * nw_ref[:, col]).astype(BF16)

    full = lambda shape: pl.BlockSpec(shape, lambda c: tuple(0 for _ in shape))
    return _ride(
        body, (nc,),
        [pl.BlockSpec((ch, si), lambda c: (c, zblk)),
         pl.BlockSpec((ch, si), lambda c: (c, xsblk)),
         pl.BlockSpec((ch, 2 * gn), lambda c: (c, bcblk)),
         pl.BlockSpec((ch, LANES), lambda c: (c, 0)),
         pl.BlockSpec(memory_space=pl.ANY),
         full((SSD_K, si)), full((SSD_K, 2 * gn)), full((1, si)), full((1, 2 * gn)),
         full((1, LANES)), full((1, LANES)), full((1, si)), full((1, si))],
        [pl.BlockSpec((ch, si), lambda c: (c, cw // si)),
         pl.BlockSpec((ch, si), lambda c: (c, 0)),
         pl.BlockSpec((1, npair, STATE, LANES), lambda c: (c, 0, 0, 0)),
         pl.BlockSpec((ch, si), lambda c: (c, 0)), pl.BlockSpec((ch, 2 * gn), lambda c: (c, 0))],
        [jax.ShapeDtypeStruct(y_mix.shape, BF16), jax.ShapeDtypeStruct((t, si), BF16),
         jax.ShapeDtypeStruct((nc, npair, STATE, LANES), F32),
         jax.ShapeDtypeStruct((t, si), BF16), jax.ShapeDtypeStruct((t, 2 * gn), BF16)],
        [pltpu.VMEM((8 + EDGE, si), F32), pltpu.VMEM((8 + EDGE, 2 * gn), F32),
         pltpu.VMEM((SSD_K - 1, ch, ch), BF16),
         pltpu.VMEM((ch, si), F32), pltpu.VMEM((ch, 2 * gn), BF16),
         pltpu.VMEM((npair, STATE, LANES), F32), pltpu.VMEM((ch, si), F32),
         pltpu.VMEM((ch, ch), F32), pltpu.VMEM((ch, LANES), F32), pltpu.VMEM((LANES, ch), F32)],
        [proj, proj, proj, dt_raw, y_mix, kw_xs, kw_bc, b_xs, b_bc, dtb, aneg, dskip, normw], {4: 0}, rider,
        ("arbitrary",), name)


def ssd_bwd(proj, dt_raw, ys, hsave, pre_xs, pre_bc, dy, kw_xs, kw_bc, dtb, aneg, dskip, normw, cw, si, name,
            rider=None):
    t = proj.shape[0]
    ch = min(SCAN_CHUNK, t)
    nc = t // ch
    npair = si // LANES
    ppg = npair // GROUPS
    gn = GROUPS * STATE
    gw = si // GROUPS
    zblk = 3 * cw // si
    xsblk = zblk + 1
    bcblk = (3 * cw + 2 * si) // (2 * gn)

    def body(z_ref, xs_ref, bc_ref, xcx_ref, xcb_ref, dtr_ref, ys_ref, hs_ref, dyb_ref,
             kwx_ref, kwb_ref, dtb_ref, aneg_ref, dsk_ref, nw_ref,
             dp_ref, ddt_ref, dkwx_ref, dkwb_ref, dbx_ref, dbb_ref, ddtb_ref, da_ref, ddsk_ref,
             dnw_ref,
             tailx, tailb, shift_s, xs_s, bc_s, dsx_s, dsb_s, dy_s, dxs_s, dbc_s, dh_s, s_s, ds_s,
             cum_s, cumt_s, dccol_s, dcrow_s, ddtcol_s, dcl_s):
        i = pl.program_id(0)

        @pl.when(i == 0)
        def _():
            dh_s[...] = jnp.zeros_like(dh_s)
            tailx[EDGE:EDGE + 8, :] = jnp.zeros((8, si), F32)
            tailb[EDGE:EDGE + 8, :] = jnp.zeros((8, 2 * gn), F32)
            _shift_matrices(shift_s, ch, SSD_K, up=True)
            for r in (dkwx_ref, dkwb_ref, dbx_ref, dbb_ref, ddtb_ref, da_ref, ddsk_ref, dnw_ref):
                r[...] = jnp.zeros_like(r)

        xc = xcx_ref[...].astype(F32)
        sg = _sigmoid(xc)
        xs_s[...] = xc * sg
        dsx_s[...] = sg * (1.0 + xc * (1.0 - sg))
        bcc = xcb_ref[...].astype(F32)
        sgb = _sigmoid(bcc)
        bc_s[...] = (bcc * sgb).astype(BF16)
        dsb_s[...] = sgb * (1.0 + bcc * (1.0 - sgb))

        dt, cum, tril = _ssd_common(dtr_ref, dtb_ref, aneg_ref, cum_s, cumt_s, ch)
        lane = lax.broadcasted_iota(jnp.int32, (ch, LANES), 1)
        lane1 = lax.broadcasted_iota(jnp.int32, (1, LANES), 1)
        low = lane < HEAD_DIM
        low1 = lane1 < HEAD_DIM
        clast = cum_s[pl.ds(ch - 1, 1), :]

        for g in range(GROUPS):
            col = slice(g * gw, (g + 1) * gw)
            ysf = ys_ref[:, col].astype(F32)
            zf = z_ref[:, col].astype(F32)
            sz = _sigmoid(zf)
            silz = zf * sz
            gg = ysf * silz
            r = lax.rsqrt(jnp.mean(gg * gg, axis=-1, keepdims=True) + EPS)
            nrm = gg * r
            dyb = dyb_ref[:, col].astype(F32)
            dnw_ref[:, col] += jnp.sum(dyb * nrm, axis=0, keepdims=True)
            dn = dyb * nw_ref[:, col]
            dgg = r * (dn - nrm * jnp.mean(dn * nrm, axis=-1, keepdims=True))
            dy_s[:, col] = dgg * silz
            dp_ref[:, col] = (dgg * ysf * (sz * (1.0 + zf * (1.0 - sz)))).astype(BF16)

        dccol_s[...] = jnp.zeros_like(dccol_s)
        dcrow_s[...] = jnp.zeros_like(dcrow_s)
        ddtcol_s[...] = jnp.zeros_like(ddtcol_s)
        dcl_s[...] = jnp.zeros_like(dcl_s)
        dbc_s[...] = jnp.zeros_like(dbc_s)

        for p in range(npair):
            g = p // ppg
            col = slice(p * LANES, (p + 1) * LANES)
            bcol = slice(g * STATE, (g + 1) * STATE)
            ccolg = slice(gn + g * STATE, gn + (g + 1) * STATE)
            bg = bc_s[:, bcol]
            cg = bc_s[:, ccolg]
            if p % ppg == 0:
                s_s[...] = _dot_nt(cg, bg)
                ds_s[...] = jnp.zeros_like(ds_s)
            heads = (PAIR * p, PAIR * p + 1)
            masks = (low, jnp.logical_not(low))
            masks1 = (low1, jnp.logical_not(low1))
            ccol = [_head_column(cum, lane, h) for h in heads]
            dcol = [_head_column(dt, lane, h) for h in heads]
            cl = [jnp.sum(jnp.where(lane1 == h, clast, 0.0), axis=-1, keepdims=True) for h in heads]
            cum_px = jnp.where(low, ccol[0], ccol[1])
            dt_px = jnp.where(low, dcol[0], dcol[1])
            cl_px = jnp.where(low1, cl[0], cl[1])
            e_px = jnp.exp(cum_px)
            dec_end = jnp.exp(cl_px - cum_px)
            gdec = jnp.exp(cl_px)
            xs_p = xs_s[:, col]
            xdt = xs_p * dt_px
            dyp = dy_s[:, col]
            hc = hs_ref[0, p]
            hcb = hc.astype(BF16)
            dhn = dh_s[p]
            dhnb = dhn.astype(BF16)

            ddsk_ref[:, col] += jnp.sum(dyp * xs_p, axis=0, keepdims=True)
            dxs_acc = dsk_ref[:, col] * dyp
            dye = dyp * e_px
            dyeb = dye.astype(BF16)
            dbc_s[:, ccolg] += _dot_nt(dyeb, hcb)
            dcum_lane = dye * _dot(cg, hcb)
            dh_from_y = _dot_tn(cg, dyeb)
            xd = xdt * dec_end
            dxd = _dot(bg, dhnb)
            dbc_s[:, bcol] += _dot_nt(xd.astype(BF16), dhnb)
            dxdt = dxd * dec_end
            t1 = dxd * xd
            dcum_lane = dcum_lane - t1
            dcl_lane = jnp.sum(t1, axis=0, keepdims=True) + jnp.sum(dhn * hc, axis=0, keepdims=True) * gdec
            dh_s[p] = gdec * dhn + dh_from_y
            xdtb = xdt.astype(BF16)
            for hi, h in enumerate(heads):
                dym = jnp.where(masks[hi], dyp, 0.0).astype(BF16)
                dw = _dot_nt(dym, xdtb)
                dec = jnp.exp(jnp.where(tril, ccol[hi] - cumt_s[pl.ds(h, 1), :], -jnp.inf))
                wm = s_s[...] * dec
                dxdt = dxdt + _dot_tn(wm.astype(BF16), dym)
                ds_s[...] += dw * dec
                gm = dw * wm
                rowsum = jnp.sum(gm, axis=-1, keepdims=True)
                lanesum = jnp.sum(jnp.where(masks[hi], dcum_lane, 0.0), axis=-1, keepdims=True)
                dccol_s[...] += jnp.where(lane == h, rowsum + lanesum, 0.0)
                dcrow_s[pl.ds(h, 1), :] = jnp.sum(gm, axis=0, keepdims=True)
                dcl_h = jnp.sum(jnp.where(masks1[hi], dcl_lane, 0.0), axis=-1, keepdims=True)
                dcl_s[...] += jnp.where(lane1 == h, dcl_h, 0.0)
            ddt_lane = dxdt * xs_p
            for hi, h in enumerate(heads):
                s = jnp.sum(jnp.where(masks[hi], ddt_lane, 0.0), axis=-1, keepdims=True)
                ddtcol_s[...] += jnp.where(lane == h, s, 0.0)
            dxs_s[:, col] = dxs_acc + dxdt * dt_px
            if p % ppg == ppg - 1:
                dsb = ds_s[...].astype(BF16)
                dbc_s[:, ccolg] += _dot(dsb, bg)
                dbc_s[:, bcol] += _dot_tn(dsb, cg)

        rowi = lax.broadcasted_iota(jnp.int32, (ch, LANES), 0)
        dcum = dccol_s[...] - dcrow_s[...].T + jnp.where(rowi == ch - 1, dcl_s[...], 0.0)
        ri = lax.broadcasted_iota(jnp.int32, (ch, ch), 0)
        ci = lax.broadcasted_iota(jnp.int32, (ch, ch), 1)
        dal = _dot_exact((ri <= ci).astype(F32), dcum)
        ddt = dal * aneg_ref[...] + ddtcol_s[...]
        da_ref[...] += jnp.sum(dal * dt, axis=0, keepdims=True)
        ddtr = ddt * _sigmoid(dtr_ref[...] + dtb_ref[...])
        ddt_ref[...] = ddtr
        ddtb_ref[...] += jnp.sum(ddtr, axis=0, keepdims=True)

        for (dpost, dsl, tail, raw_ref, kw_ref, dkw_ref, db_ref, out) in (
                (dxs_s, dsx_s, tailx, xs_ref, kwx_ref, dkwx_ref, dbx_ref, slice(si, 2 * si)),
                (dbc_s, dsb_s, tailb, bc_ref, kwb_ref, dkwb_ref, dbb_ref, slice(2 * si, 2 * si + 2 * gn))):
            dxc = dpost[...] * dsl[...]
            dxcb = dxc.astype(BF16)
            raw = raw_ref[...].astype(F32)
            raw_end = raw_ref[ch - EDGE:ch, :].astype(F32)
            tail[0:EDGE, :] = dxcb[ch - EDGE:ch].astype(F32)
            db_ref[...] += jnp.sum(dxc, axis=0, keepdims=True)
            draw = kw_ref[pl.ds(SSD_K - 1, 1), :] * dxc
            dkw_ref[pl.ds(SSD_K - 1, 1), :] += jnp.sum(dxc * raw, axis=0, keepdims=True)
            fix = jnp.zeros((EDGE, dxc.shape[1]), F32)
            for k in range(SSD_K - 1):
                moved = _dot(shift_s[k], dxcb)
                miss = tail[pl.ds(SSD_K - 1 - k, EDGE), :] - moved[ch - EDGE:ch]
                draw = draw + kw_ref[pl.ds(k, 1), :] * moved
                fix = fix + kw_ref[pl.ds(k, 1), :] * miss
                dkw_ref[pl.ds(k, 1), :] += (jnp.sum(moved * raw, axis=0, keepdims=True)
                                            + jnp.sum(miss * raw_end, axis=0, keepdims=True))
            dp_ref[:, out] = draw.astype(BF16)
            dp_ref[ch - EDGE:ch, out] = (draw[ch - EDGE:ch] + fix).astype(BF16)
            tail[EDGE:EDGE + 8, :] = dxcb[0:EDGE].astype(F32)[0:8]

    full = lambda shape: pl.BlockSpec(shape, lambda i: tuple(0 for _ in shape))
    rev = lambda blk: (lambda i: (nc - 1 - i, blk))
    small_in = [(SSD_K, si), (SSD_K, 2 * gn), (1, LANES), (1, LANES), (1, si), (1, si)]
    small = [(SSD_K, si), (SSD_K, 2 * gn), (1, si), (1, 2 * gn), (1, LANES), (1, LANES), (1, si), (1, si)]
    return _ride(
        body, (nc,),
        [pl.BlockSpec((ch, si), rev(zblk)), pl.BlockSpec((ch, si), rev(xsblk)),
         pl.BlockSpec((ch, 2 * gn), rev(bcblk)),
         pl.BlockSpec((ch, si), rev(0)), pl.BlockSpec((ch, 2 * gn), rev(0)),
         pl.BlockSpec((ch, LANES), rev(0)), pl.BlockSpec((ch, si), rev(0)),
         pl.BlockSpec((1, npair, STATE, LANES), lambda i: (nc - 1 - i, 0, 0, 0)),
         pl.BlockSpec((ch, si), rev(cw // si))] + [full(s) for s in small_in],
        [pl.BlockSpec((ch, 2 * si + 2 * gn), rev(0)), pl.BlockSpec((ch, LANES), rev(0))] + [full(s) for s in small],
        [jax.ShapeDtypeStruct((t, 2 * si + 2 * gn), BF16), jax.ShapeDtypeStruct((t, LANES), F32)]
        + [jax.ShapeDtypeStruct(s, F32) for s in small],
        [pltpu.VMEM((EDGE + 8, si), F32), pltpu.VMEM((EDGE + 8, 2 * gn), F32),
         pltpu.VMEM((SSD_K - 1, ch, ch), BF16),
         pltpu.VMEM((ch, si), F32), pltpu.VMEM((ch, 2 * gn), BF16),
         pltpu.VMEM((ch, si), F32), pltpu.VMEM((ch, 2 * gn), F32),
         pltpu.VMEM((ch, si), F32), pltpu.VMEM((ch, si), F32), pltpu.VMEM((ch, 2 * gn), F32),
         pltpu.VMEM((npair, STATE, LANES), F32),
         pltpu.VMEM((ch, ch), F32), pltpu.VMEM((ch, ch), F32),
         pltpu.VMEM((ch, LANES), F32), pltpu.VMEM((LANES, ch), F32),
         pltpu.VMEM((ch, LANES), F32), pltpu.VMEM((LANES, ch), F32),
         pltpu.VMEM((ch, LANES), F32), pltpu.VMEM((1, LANES), F32)],
        [proj, proj, proj, pre_xs, pre_bc, dt_raw, ys, hsave, dy, kw_xs, kw_bc, dtb, aneg, dskip, normw],
        {}, rider, ("arbitrary",), name)


def final_loss(x, nw, tgt, name):
    t, d = x.shape
    tm = _tile(t, 512, 8)

    def body(x_ref, nw_ref, t_ref, dx_ref, dnw_ref, ls_ref):
        xf = x_ref[...]
        r = lax.rsqrt(jnp.mean(xf * xf, axis=-1, keepdims=True) + EPS)
        nx = xf * r
        e = nx * nw_ref[...] - t_ref[...]
        dyv = e * (1.0 / d)
        dn = dyv * nw_ref[...]
        dx_ref[...] = r * (dn - nx * jnp.mean(dn * nx, axis=-1, keepdims=True))

        @pl.when(pl.program_id(0) == 0)
        def _():
            dnw_ref[...] = jnp.zeros_like(dnw_ref)
            ls_ref[...] = jnp.zeros_like(ls_ref)

        dnw_ref[...] += jnp.sum(dyv * nx, axis=0, keepdims=True)
        ls_ref[...] += jnp.sum(e * e, axis=0, keepdims=True) * (0.5 / d)

    return pl.pallas_call(
        body, name=name, grid=(t // tm,),
        in_specs=[pl.BlockSpec((tm, d), lambda i: (i, 0)), pl.BlockSpec((1, d), lambda i: (0, 0)),
                  pl.BlockSpec((tm, d), lambda i: (i, 0))],
        out_specs=[pl.BlockSpec((tm, d), lambda i: (i, 0)), pl.BlockSpec((1, d), lambda i: (0, 0)),
                   pl.BlockSpec((1, d), lambda i: (0, 0))],
        out_shape=[jax.ShapeDtypeStruct((t, d), F32), jax.ShapeDtypeStruct((1, d), F32),
                   jax.ShapeDtypeStruct((1, d), F32)],
        compiler_params=_params(("arbitrary",)),
    )(x, nw, tgt)


def _rows3(a):
    if a.ndim == 1:
        return a.reshape(1, 1, a.shape[0])
    if a.ndim == 2:
        return a.reshape(1, *a.shape)
    return a.reshape(-1, a.shape[-2], a.shape[-1])


def adamw(w, g, m, v, name):
    shape = w.shape
    views = [_rows3(a) for a in (w, g, m, v)]
    b, r, c = views[0].shape
    tr = _tile(r, 256, 16) if r % 16 == 0 else r

    def body(w_ref, g_ref, m_ref, v_ref, d_ref, nm_ref, nv_ref):
        g = g_ref[...]
        m = ADAM_B1 * m_ref[...] + (1.0 - ADAM_B1) * g
        v = ADAM_B2 * v_ref[...] + (1.0 - ADAM_B2) * (g * g)
        m_hat = m / (1.0 - ADAM_B1 ** ADAM_STEP)
        v_hat = v / (1.0 - ADAM_B2 ** ADAM_STEP)
        d_ref[...] = -ADAM_LR * (m_hat / (jnp.sqrt(v_hat) + ADAM_EPS) + ADAM_WD * w_ref[...])
        nm_ref[...] = m
        nv_ref[...] = v

    spec = pl.BlockSpec((1, tr, c), lambda i, j: (i, j, 0))
    outs = pl.pallas_call(
        body, name=name, grid=(b, r // tr), in_specs=[spec] * 4, out_specs=[spec] * 3,
        out_shape=[jax.ShapeDtypeStruct((b, r, c), F32)] * 3,
        compiler_params=_params(("parallel", "parallel")),
    )(*views)
    return [o.reshape(shape) for o in outs]


def adamw_halves(w, g_mine, g_theirs, m, v, name):
    nl, r, c = w.shape
    r2 = r // 2
    tr = _tile(r2, 256, 16)
    nb = r2 // tr

    def body(w_ref, gm_ref, gt_ref, m_ref, v_ref, g_ref, d_ref, nm_ref, nv_ref):
        mine = (pl.program_id(1) // nb) == lax.axis_index("c")
        g = jnp.where(mine, gm_ref[...], gt_ref[...])
        m = ADAM_B1 * m_ref[...] + (1.0 - ADAM_B1) * g
        v = ADAM_B2 * v_ref[...] + (1.0 - ADAM_B2) * (g * g)
        m_hat = m / (1.0 - ADAM_B1 ** ADAM_STEP)
        v_hat = v / (1.0 - ADAM_B2 ** ADAM_STEP)
        g_ref[...] = g
        d_ref[...] = -ADAM_LR * (m_hat / (jnp.sqrt(v_hat) + ADAM_EPS) + ADAM_WD * w_ref[...])
        nm_ref[...] = m
        nv_ref[...] = v

    whole = pl.BlockSpec((1, tr, c), lambda l, i: (l, i, 0))
    half = pl.BlockSpec((1, tr, c), lambda l, i: (l, i % nb, 0))
    return pl.pallas_call(
        body, name=name, grid=(nl, 2 * nb), in_specs=[whole, half, half, whole, whole], out_specs=[whole] * 4,
        out_shape=[jax.ShapeDtypeStruct((nl, r, c), F32)] * 4,
        compiler_params=_params(("parallel", "parallel")),
    )(w, g_mine, g_theirs, m, v)


def _coords():
    return lax.axis_index("x"), lax.axis_index("y"), lax.axis_index("c")


def _ici_peers(x, y):
    chips = [(1 - x, y), (x, 1 - y), (1 - x, 1 - y)]
    return chips, [2 * cx + cy for cx, cy in chips]


def _place(ref, how, chip, layers, per):
    if how == "lead":
        return ref.at[chip, layers]
    start = pl.multiple_of(chip * per, per)
    if how == "rows":
        return ref.at[layers, pl.ds(start, per), :]
    return ref.at[layers, :, pl.ds(start, per)]


def gather_weights(shards, hows, name):
    na = len(shards)
    out_shape = []
    for s, how in zip(shards, hows):
        assert s.shape[0] % 2 == 0
        if how == "lead":
            shp = (N_CHIPS, *s.shape)
        elif how == "rows":
            shp = (s.shape[0], N_CHIPS * s.shape[1], s.shape[2])
        else:
            shp = (s.shape[0], s.shape[1], N_CHIPS * s.shape[2])
        out_shape.append(jax.ShapeDtypeStruct(shp, s.dtype))

    def body(*refs):
        ins = refs[:na]
        outs = refs[na:2 * na]
        send_sems, recv_sems = refs[2 * na:]
        x, y, c = _coords()
        me = 2 * x + y
        chips, chip_ids = _ici_peers(x, y)
        sibling = (x, y, 1 - c)

        def dst(a, chip, layers):
            per = {"lead": 0, "rows": ins[a].shape[1], "cols": ins[a].shape[-1]}[hows[a]]
            return _place(outs[a], hows[a], chip, layers, per)

        def copy(a, k, src, dst_ref, to):
            return pltpu.make_async_remote_copy(
                src_ref=src, dst_ref=dst_ref, send_sem=send_sems.at[7 * a + k], recv_sem=recv_sems.at[7 * a + k],
                device_id=to, device_id_type=MESH)

        started = []
        halves = []
        for a in range(na):
            nl = ins[a].shape[0]
            hl = nl // 2
            mine = pl.ds(c * hl, hl)
            theirs = pl.ds((1 - c) * hl, hl)
            halves.append((mine, theirs))
            for k in range(3):
                cp = copy(a, k, ins[a].at[mine], dst(a, me, mine), (*chips[k], c))
                cp.start()
                started.append(cp)
            own = copy(a, 6, ins[a], dst(a, me, pl.ds(0, nl)), sibling)
            own.start()
            started.append(own)
        for a in range(na):
            mine, _ = halves[a]
            for k in range(3):
                landed = dst(a, chip_ids[k], mine)
                copy(a, k, landed, landed, (*chips[k], c)).wait_recv()
                fw = copy(a, 3 + k, landed, landed, sibling)
                fw.start()
                started.append(fw)
        for a in range(na):
            _, theirs = halves[a]
            for k in range(3):
                got = dst(a, chip_ids[k], theirs)
                copy(a, 3 + k, got, got, sibling).wait_recv()
            whole = dst(a, me, pl.ds(0, ins[a].shape[0]))
            copy(a, 6, whole, whole, sibling).wait_recv()
        for cp in started:
            cp.wait_send()

    return pl.pallas_call(
        body, name=name, in_specs=_any_specs(na), out_specs=_any_specs(na), out_shape=out_shape,
        scratch_shapes=[pltpu.SemaphoreType.DMA((7 * na,)), pltpu.SemaphoreType.DMA((7 * na,))],
        compiler_params=pltpu.CompilerParams(has_side_effects=True),
    )(*shards)


def _remote(src, dst, send_sems, recv_sems, k, to):
    return pltpu.make_async_remote_copy(src_ref=src, dst_ref=dst, send_sem=send_sems.at[k], recv_sem=recv_sems.at[k],
                                        device_id=to, device_id_type=MESH)


LAYER_HOW = ("lead", "rows", "cols", "rows")


def _layer_place(ref, how, chip, shard_shape, start, size):
    r, c = shard_shape
    if how == "lead":
        return ref.at[chip, :, pl.ds(start, size), :]
    if how == "rows":
        return ref.at[:, pl.ds(pl.multiple_of(chip * r + start, HALO), size), :]
    return ref.at[:, pl.ds(start, size), pl.ds(pl.multiple_of(chip * c, LANES), c)]


def weight_rider_ici(shards, hows, layer):
    shapes = [tuple(s.shape[1:]) for s in shards]
    out_shapes = []
    for (r, c), how, s in zip(shapes, hows, shards):
        shp = {"lead": (N_CHIPS, 1, r, c), "rows": (1, N_CHIPS * r, c), "cols": (1, r, N_CHIPS * c)}[how]
        out_shapes.append(jax.ShapeDtypeStruct(shp, s.dtype))

    def copies(ins, outs, send_sems, recv_sems):
        x, y, c = _coords()
        me = 2 * x + y
        chips, chip_ids = _ici_peers(x, y)
        sibling = (x, y, 1 - c)
        pairs = []
        for a, (shape, how) in enumerate(zip(shapes, hows)):
            half = shape[0] // 2
            mine = pl.multiple_of(c * half, HALO)
            src = ins[a].at[pl.ds(layer, 1)]
            for k in range(3):
                to = (*chips[k], c)
                land = _layer_place(outs[a], how, chip_ids[k], shape, mine, half)
                pairs.append((_remote(src.at[:, pl.ds(mine, half), :], _layer_place(outs[a], how, me, shape, mine, half),
                                      send_sems, recv_sems, 4 * a + k, to),
                              _remote(land, land, send_sems, recv_sems, 4 * a + k, to)))
            whole = _layer_place(outs[a], how, me, shape, 0, shape[0])
            pairs.append((_remote(src, whole, send_sems, recv_sems, 4 * a + 3, sibling),
                          _remote(whole, whole, send_sems, recv_sems, 4 * a + 3, sibling)))
        return pairs

    return Rider(list(shards), out_shapes, {}, 4 * len(shards), copies)


def weight_rider_d2d(bufs, shapes, hows):
    def copies(ins, outs, send_sems, recv_sems):
        x, y, c = _coords()
        _, chip_ids = _ici_peers(x, y)
        sibling = (x, y, 1 - c)
        pairs = []
        for a, (shape, how) in enumerate(zip(shapes, hows)):
            half = shape[0] // 2
            mine = pl.multiple_of(c * half, HALO)
            theirs = pl.multiple_of((1 - c) * half, HALO)
            for k in range(3):
                land = _layer_place(outs[a], how, chip_ids[k], shape, theirs, half)
                pairs.append((_remote(_layer_place(ins[a], how, chip_ids[k], shape, mine, half),
                                      _layer_place(outs[a], how, chip_ids[k], shape, mine, half),
                                      send_sems, recv_sems, 3 * a + k, sibling),
                              _remote(land, land, send_sems, recv_sems, 3 * a + k, sibling)))
        return pairs

    return Rider(list(bufs), [jax.ShapeDtypeStruct(b.shape, b.dtype) for b in bufs],
                 {a: a for a in range(len(bufs))}, 3 * len(bufs), copies)


def grads_rider_sibling(arrs):
    def copies(ins, outs, send_sems, recv_sems):
        x, y, c = _coords()
        sibling = (x, y, 1 - c)
        pairs = []
        for a in range(len(arrs)):
            r2 = ins[a].shape[1] // 2
            src = ins[a].at[:, pl.ds(pl.multiple_of((1 - c) * r2, 8), r2), :]
            pairs.append((_remote(src, outs[a], send_sems, recv_sems, a, sibling),
                          _remote(outs[a], outs[a], send_sems, recv_sems, a, sibling)))
        return pairs

    return Rider(list(arrs), [jax.ShapeDtypeStruct((a.shape[0], a.shape[1] // 2, a.shape[2]), a.dtype) for a in arrs],
                 {}, len(arrs), copies)


def chip_sum(g, recv, name):
    nch, r, c = g.shape
    r2 = r // 2
    tr = _tile(r2, 256, 16)
    nb = r2 // tr

    def body(g0_ref, g1_ref, r_ref, o32_ref, o16_ref):
        s = jnp.where(lax.axis_index("c") == 0, g0_ref[...], g1_ref[...]) + r_ref[...]
        o32_ref[...] = s
        o16_ref[...] = s.astype(BF16)

    here = pl.BlockSpec((1, tr, c), lambda i, j: (i, j, 0))
    return pl.pallas_call(
        body, name=name, grid=(nch, nb),
        in_specs=[here, pl.BlockSpec((1, tr, c), lambda i, j: (i, nb + j, 0)), here],
        out_specs=[here, here],
        out_shape=[jax.ShapeDtypeStruct((nch, r2, c), F32), jax.ShapeDtypeStruct((nch, r2, c), BF16)],
        compiler_params=_params(("parallel", "parallel")),
    )(g, g, recv)


def grads_rider_chips(arrs):
    def copies(ins, outs, send_sems, recv_sems):
        x, y, c = _coords()
        chips, chip_ids = _ici_peers(x, y)
        pairs = []
        for a in range(len(arrs)):
            for k in range(3):
                to = (*chips[k], c)
                pairs.append((_remote(ins[a].at[chip_ids[k]], outs[a].at[k], send_sems, recv_sems, 3 * a + k, to),
                              _remote(outs[a].at[k], outs[a].at[k], send_sems, recv_sems, 3 * a + k, to)))
        return pairs

    return Rider(list(arrs), [jax.ShapeDtypeStruct((3, *a.shape[1:]), a.dtype) for a in arrs], {}, 3 * len(arrs),
                 copies)


def grad_sum(p32, recv, layer, nl, buf, name):
    _, r2, c = p32.shape
    tr = _tile(r2, 256, 16)
    nb = r2 // tr

    def body(p0_ref, p1_ref, p2_ref, p3_ref, r0_ref, r1_ref, r2_ref, *rest):
        o_ref = rest[-1]
        x, y, _ = _coords()
        chip = 2 * x + y
        own = jnp.where(chip == 0, p0_ref[...], jnp.where(chip == 1, p1_ref[...],
                                                        jnp.where(chip == 2, p2_ref[...], p3_ref[...])))
        o_ref[...] = own + r0_ref[...].astype(F32) + r1_ref[...].astype(F32) + r2_ref[...].astype(F32)

    slot = lambda k: pl.BlockSpec((1, tr, c), lambda j: (k, j, 0))
    in_specs = [slot(k) for k in range(N_CHIPS)] + [slot(k) for k in range(3)]
    args = [p32] * N_CHIPS + [recv] * 3
    aliases = {}
    if buf is not None:
        in_specs.append(pl.BlockSpec(memory_space=pl.ANY))
        args.append(buf)
        aliases = {len(args) - 1: 0}
    return pl.pallas_call(
        body, name=name, grid=(nb,), in_specs=in_specs,
        out_specs=pl.BlockSpec((1, tr, c), lambda j: (layer, j, 0)),
        out_shape=jax.ShapeDtypeStruct((nl, r2, c), F32),
        input_output_aliases=aliases,
        compiler_params=_params(("parallel",)),
    )(*args)


def grads_rider_exchange(bufs):
    def copies(ins, outs, send_sems, recv_sems):
        x, y, c = _coords()
        sibling = (x, y, 1 - c)
        return [(_remote(ins[a], outs[a], send_sems, recv_sems, a, sibling),
                 _remote(outs[a], outs[a], send_sems, recv_sems, a, sibling)) for a in range(len(bufs))]

    return Rider(list(bufs), [jax.ShapeDtypeStruct(b.shape, b.dtype) for b in bufs], {}, len(bufs), copies)


def allreduce_small(buf, name):
    r, cdim = buf.shape

    def body(x_ref, o_ref, gath, send_sems, recv_sems):
        x, y, c = _coords()
        me, sibling = (x, y, c), (x, y, 1 - c)
        chips, _ = _ici_peers(x, y)

        def slot(px, py, pc):
            return gath.at[4 * px + 2 * py + pc]

        def copy(k, block, to, src=None):
            return pltpu.make_async_remote_copy(
                src_ref=slot(*block) if src is None else src, dst_ref=slot(*block),
                send_sem=send_sems.at[k], recv_sem=recv_sems.at[k], device_id=to, device_id_type=MESH)

        gath[4 * x + 2 * y + c] = x_ref[...]
        first = [copy(0, me, sibling, src=x_ref)]
        first += [copy(1 + j, me, (*chip, c), src=x_ref) for j, chip in enumerate(chips)]
        for cp in first:
            cp.start()
        passed = [copy(4 + j, (*chip, c), sibling) for j, chip in enumerate(chips)]
        for j, chip in enumerate(chips):
            copy(1 + j, (*chip, c), me).wait_recv()
            passed[j].start()
        copy(0, sibling, me).wait_recv()
        for j, chip in enumerate(chips):
            copy(4 + j, (*chip, 1 - c), me).wait_recv()
        for cp in first + passed:
            cp.wait_send()
        acc = gath[0]
        for d in range(1, 8):
            acc = acc + gath[d]
        o_ref[...] = acc

    return pl.pallas_call(
        body, name=name,
        in_specs=[pl.BlockSpec(memory_space=pltpu.VMEM)], out_specs=pl.BlockSpec(memory_space=pltpu.VMEM),
        out_shape=jax.ShapeDtypeStruct((r, cdim), F32),
        scratch_shapes=[pltpu.VMEM((8, r, cdim), F32), pltpu.SemaphoreType.DMA((7,)), pltpu.SemaphoreType.DMA((7,))],
        compiler_params=pltpu.CompilerParams(has_side_effects=True),
    )(buf)


def _expand_heads(v):
    return jnp.repeat(v.astype(F32), HEAD_DIM).reshape(1, -1)


def _pad_lanes(v):
    return jnp.pad(v.astype(F32), (0, LANES - v.shape[0])).reshape(1, LANES)


def local_step(x, tgt, p, comm, cols):
    nl = p["norm_mix_w"].shape[0]
    d = x.shape[1]
    cw = p["short_conv_w"].shape[2]
    si = p["ssd_norm_w"].shape[1]
    ff, npad = comm.ff, comm.npad
    nh = si // HEAD_DIM
    gn = GROUPS * STATE
    dt_off = 3 * cw + si + si + 2 * gn
    assert cols == dt_off + nh and nh <= LANES and dt_off % LANES == 0 and npad == dt_off + LANES
    pieces = [(0, cw), (cw, cw), (2 * cw, cw), (3 * cw, si), (3 * cw + si, si), (3 * cw + 2 * si, 2 * gn),
              (dt_off, LANES)]

    saved = []
    for l in range(nl):
        nw1 = p["norm_mix_w"][l].reshape(1, d)
        nw2 = p["norm_mlp_w"][l].reshape(1, d)
        kw3 = p["short_conv_w"][l]
        kwx, kwb = p["ssd_conv_w"][l][:, :si], p["ssd_conv_w"][l][:, si:]
        bx, bb = p["ssd_conv_b"][l][:si].reshape(1, si), p["ssd_conv_b"][l][si:].reshape(1, 2 * gn)
        dtb = _pad_lanes(p["dt_bias"][l])
        aneg = _pad_lanes(-jnp.exp(p["a_log"][l]))
        dsk = _expand_heads(p["d_skip"][l])
        snw = p["ssd_norm_w"][l].reshape(1, si)
        ssd_args = (kwx, kwb, bx, bb, dtb, aneg, dsk, snw)

        w_in = comm.weight(l, "w_in")
        (proj, h, dt_raw), sent = norm_matmul(x, nw1, w_in, 0, dt_off, BF16, "in_proj", tail_block=dt_off // LANES,
                                              rider=comm.rider("in_proj", l))
        comm.done("in_proj", l, sent)
        y_mix = conv_mixer_fwd(proj, kw3, cw, cw + si, "conv_mixer_fwd")
        (y_mix, *ssd_saved), sent = ssd_fwd(proj, dt_raw, y_mix, *ssd_args, cw, si, "ssd_fwd",
                                            rider=comm.rider("ssd_fwd", l))
        comm.done("ssd_fwd", l, sent)
        x2, _ = matmul(y_mix, comm.weight(l, "w_out"), 0, False, d, F32, "out_proj", residual=x)
        (up, h2), sent = norm_matmul(x2, nw2, comm.weight(l, "w_up"), 0, ff, BF16, "up_proj",
                                     rider=comm.rider("up_proj", l))
        comm.done("up_proj", l, sent)
        x3, sent = matmul(up, comm.weight(l, "w_down"), 0, False, d, F32, "down_proj", lhs_fn=_relu2, residual=x2,
                          rider=comm.rider("down_proj", l))
        comm.done("down_proj", l, sent)
        saved.append((x, h, proj, dt_raw, y_mix, ssd_saved, x2, h2, up, nw1, nw2, kw3, ssd_args))
        x = x3

    dx, dwf, lvec = final_loss(x, p["final_norm_w"].reshape(1, d), tgt, "final_loss")
    loss = jnp.sum(lvec)

    names = ("norm_mix_w", "short_conv_w", "ssd_conv_w", "ssd_conv_b", "dt_bias", "a_log", "d_skip",
             "ssd_norm_w", "norm_mlp_w")
    grads = {k: [None] * nl for k in names}
    for l in reversed(range(nl)):
        x0, h, proj, dt_raw, y_mix, ssd_saved, x2, h2, up, nw1, nw2, kw3, ssd_args = saved[l]
        kwx, kwb, _, _, dtb, aneg, dsk, snw = ssd_args
        dup, sent = matmul(dx, comm.weight(l, "w_down"), 0, True, ff, BF16, "down_bwd", relu_gate=up,
                           rider=comm.rider("down_bwd", l))
        comm.done("down_bwd", l, sent)
        g_down, sums = matmul_tn(up, dx, "down_wgrad", a_fn=_relu2, chip_sums=comm.side("down_wgrad", l))
        comm.side_done("down_wgrad", l, sums)
        comm.take_gradient(l, "w_down", g_down)
        (dx2, dnw2), _ = matmul_normbwd([(dup, 0)], [(0, ff)], comm.weight(l, "w_up"), 0, x2, nw2, dx, "up_bwd")
        comm.take_gradient(l, "w_up", matmul_tn(h2, dup, "up_wgrad", by_chip=True))
        grads["norm_mlp_w"][l] = dnw2.reshape(d)
        dy, sent = matmul(dx2, comm.weight(l, "w_out"), 0, True, cw + si, BF16, "out_bwd",
                          rider=comm.rider("out_bwd", l))
        comm.done("out_bwd", l, sent)
        g_out, sums = matmul_tn(y_mix, dx2, "out_wgrad", chip_sums=comm.side("out_wgrad", l))
        comm.side_done("out_wgrad", l, sums)
        comm.take_gradient(l, "w_out", g_out)
        du, dkw3 = conv_mixer_bwd(proj, dy, kw3, cw, "conv_mixer_bwd")
        (dssd, ddt, dkwx, dkwb, dbx, dbb, ddtb, da, ddsk, dsnw), sent = ssd_bwd(
            proj, dt_raw, *ssd_saved, dy, kwx, kwb, dtb, aneg, dsk, snw, cw, si, "ssd_bwd",
            rider=comm.rider("ssd_bwd", l))
        comm.done("ssd_bwd", l, sent)
        views = [(du, 0), (du, 1), (du, 2), (dssd, 0), (dssd, 1), (dssd, 2 * si // (2 * gn)), (ddt, 0)]
        (dxl, dnw1), sent = matmul_normbwd(views, pieces, comm.weight(l, "w_in"), 0, x0, nw1, dx2, "in_bwd",
                                           rider=comm.rider("in_bwd", l))
        comm.done("in_bwd", l, sent)
        comm.take_gradient(l, "w_in", split_to_chips(
            [matmul_tn(h, dp, "in_wgrad_%d" % i) for i, dp in enumerate((du, dssd, ddt))], cols, "in_wgrad_split"))
        grads["norm_mix_w"][l] = dnw1.reshape(d)
        grads["short_conv_w"][l] = dkw3
        grads["ssd_conv_w"][l] = jnp.concatenate([dkwx, dkwb], axis=1)
        grads["ssd_conv_b"][l] = jnp.concatenate([dbx, dbb], axis=1).reshape(-1)
        grads["dt_bias"][l] = ddtb[0, :nh]
        grads["a_log"][l] = da[0, :nh] * aneg[0, :nh]
        grads["d_skip"][l] = jnp.sum(ddsk.reshape(nh, HEAD_DIM), axis=1)
        grads["ssd_norm_w"][l] = dsnw.reshape(si)
        dx = dxl

    grads = {k: jnp.stack(v) for k, v in grads.items()}
    grads["final_norm_w"] = dwf.reshape(d)
    return loss, dx, grads


BIG = ("w_in", "w_out", "w_up", "w_down")
SMALL_SHARDED = ("short_conv_w", "ssd_conv_w")
SMALL_REPL = ("norm_mix_w", "ssd_conv_b", "dt_bias", "a_log", "d_skip", "ssd_norm_w", "norm_mlp_w", "final_norm_w")
WEIGHTS = ("norm_mix_w", "w_in", "short_conv_w", "ssd_conv_w", "ssd_conv_b", "dt_bias", "a_log", "d_skip",
           "ssd_norm_w", "w_out", "norm_mlp_w", "w_up", "w_down", "final_norm_w")
SMALL_COLS = 1024


def _pack_small(named):
    flat = jnp.concatenate([v.reshape(-1).astype(F32) for v in named])
    n = flat.shape[0]
    rows = -(-n // SMALL_COLS)
    rows = -(-rows // 8) * 8
    return jnp.pad(flat, (0, rows * SMALL_COLS - n)).reshape(rows, SMALL_COLS)


def _unpack_small(buf, like):
    flat = buf.reshape(-1)
    out, off = [], 0
    for v in like:
        out.append(flat[off:off + v.size].reshape(v.shape))
        off += v.size
    return out


class ChipComm:
    IO = ("w_in", "w_out")
    MLP = ("w_up", "w_down")

    def __init__(self, shards, nl, npad):
        self.shards, self.nl, self.npad = shards, nl, npad
        self.how = dict(zip(BIG, LAYER_HOW))
        self.ff = N_CHIPS * shards["w_up"].shape[2]
        self.w = {}
        self.landed = {}
        self.grad = {}
        self.sums = {}
        self.from_sibling = {}
        self.bufs = {k: None for k in BIG}
        first = run_rider(self._ici(self.IO, 0), "gather_first_ici")
        self._gathered(self.IO, 0, run_rider(self._d2d(self.IO, first), "gather_first_d2d"))

    def _ici(self, group, l):
        return weight_rider_ici([self.shards[k] for k in group], [self.how[k] for k in group], l)

    def _d2d(self, group, landed):
        return weight_rider_d2d(landed, [tuple(self.shards[k].shape[1:]) for k in group],
                                [self.how[k] for k in group])

    def _gathered(self, group, l, arrays):
        for k, g in zip(group, arrays):
            self.w[(l, k)] = join_from_chips(g, self.npad, "w_in_join") if k == "w_in" else g

    def _to_sibling(self, group, l):
        return grads_rider_sibling([self.grad[(l, k)] for k in group])

    def _summed(self, group, l, from_sibling):
        self.sums[group] = (l, [chip_sum(self.grad.pop((l, k)), r, "chip_sum") for k, r in zip(group, from_sibling)])

    def side(self, point, l):
        group = self.IO if point == "down_wgrad" else self.MLP
        if group not in self.from_sibling:
            return ()
        layer, parts = self.from_sibling[group]
        return [(self.grad[(layer, k)], r) for k, r in zip(group, parts)]

    def side_done(self, point, l, sums):
        group = self.IO if point == "down_wgrad" else self.MLP
        if sums:
            layer, _ = self.from_sibling.pop(group)
            for k in group:
                del self.grad[(layer, k)]
            self.sums[group] = (layer, sums)

    def _to_chips(self, group):
        return grads_rider_chips([s[1] for s in self.sums[group][1]])

    def _reduced(self, group, from_chips):
        l, sums = self.sums.pop(group)
        for k, s, r in zip(group, sums, from_chips):
            self.bufs[k] = grad_sum(s[0], r, l, self.nl, self.bufs[k], "grad_sum")

    def weight(self, l, name):
        return self.w[(l, name)]

    def take_gradient(self, l, name, g):
        self.grad[(l, name)] = g if g.ndim == 3 else g.reshape(N_CHIPS, g.shape[0] // N_CHIPS, g.shape[1])

    def rider(self, point, l):
        more = l + 1 < self.nl
        if point == "in_proj":
            return self._ici(self.MLP, l)
        if point == "ssd_fwd":
            return self._d2d(self.MLP, self.landed[self.MLP])
        if point == "up_proj":
            return self._ici(self.IO, l + 1) if more else None
        if point == "down_proj":
            return self._d2d(self.IO, self.landed[self.IO]) if more else None
        if point == "down_bwd":
            return self._to_sibling(self.IO, l + 1) if more else None
        if point == "out_bwd":
            return self._to_sibling(self.MLP, l)
        if point == "ssd_bwd":
            return self._to_chips(self.IO) if more else None
        return self._to_chips(self.MLP)

    def done(self, point, l, results):
        if not results:
            return
        if point in ("in_proj", "up_proj"):
            self.landed[self.MLP if point == "in_proj" else self.IO] = results
        elif point == "ssd_fwd":
            self._gathered(self.MLP, l, results)
        elif point == "down_proj":
            self._gathered(self.IO, l + 1, results)
        elif point == "down_bwd":
            self.from_sibling[self.IO] = (l + 1, results)
        elif point == "out_bwd":
            self.from_sibling[self.MLP] = (l, results)
        elif point == "ssd_bwd":
            self._reduced(self.IO, results)
        else:
            self._reduced(self.MLP, results)

    def finish(self):
        self._summed(self.IO, 0, run_rider(self._to_sibling(self.IO, 0), "grads_to_sibling"))
        self._reduced(self.IO, run_rider(self._to_chips(self.IO), "grads_to_chips"))
        mine = [self.bufs[k] for k in BIG]
        theirs = run_rider(grads_rider_exchange(mine), "grads_exchange")
        return dict(zip(BIG, zip(mine, theirs)))


def kernel(x, norm_mix_w, w_in, short_conv_w, ssd_conv_w, ssd_conv_b, dt_bias, a_log, d_skip, ssd_norm_w, w_out, norm_mlp_w, w_up, w_down, final_norm_w, loss_target, m_norm_mix_w, m_w_in, m_short_conv_w, m_ssd_conv_w, m_ssd_conv_b, m_dt_bias, m_a_log, m_d_skip, m_ssd_norm_w, m_w_out, m_norm_mlp_w, m_w_up, m_w_down, m_final_norm_w, v_norm_mix_w, v_w_in, v_short_conv_w, v_ssd_conv_w, v_ssd_conv_b, v_dt_bias, v_a_log, v_d_skip, v_ssd_norm_w, v_w_out, v_norm_mlp_w, v_w_up, v_w_down, v_final_norm_w):
    w = dict(norm_mix_w=norm_mix_w, w_in=w_in, short_conv_w=short_conv_w, ssd_conv_w=ssd_conv_w,
             ssd_conv_b=ssd_conv_b, dt_bias=dt_bias, a_log=a_log, d_skip=d_skip, ssd_norm_w=ssd_norm_w, w_out=w_out,
             norm_mlp_w=norm_mlp_w, w_up=w_up, w_down=w_down, final_norm_w=final_norm_w)
    m = dict(norm_mix_w=m_norm_mix_w, w_in=m_w_in, short_conv_w=m_short_conv_w, ssd_conv_w=m_ssd_conv_w,
             ssd_conv_b=m_ssd_conv_b, dt_bias=m_dt_bias, a_log=m_a_log, d_skip=m_d_skip, ssd_norm_w=m_ssd_norm_w,
             w_out=m_w_out, norm_mlp_w=m_norm_mlp_w, w_up=m_w_up, w_down=m_w_down, final_norm_w=m_final_norm_w)
    v = dict(norm_mix_w=v_norm_mix_w, w_in=v_w_in, short_conv_w=v_short_conv_w, ssd_conv_w=v_ssd_conv_w,
             ssd_conv_b=v_ssd_conv_b, dt_bias=v_dt_bias, a_log=v_a_log, d_skip=v_d_skip, ssd_norm_w=v_ssd_norm_w,
             w_out=v_w_out, norm_mlp_w=v_norm_mlp_w, w_up=v_w_up, w_down=v_w_down, final_norm_w=v_final_norm_w)
    xi, yi, ci = lax.axis_index("x"), lax.axis_index("y"), lax.axis_index("c")
    chip = 2 * xi + yi
    nl = w_up.shape[0]
    cols = N_CHIPS * w_in.shape[2]
    npad = cols // LANES * LANES + LANES

    full = dict(w)
    small_gathered = gather_weights([w[k] for k in SMALL_SHARDED], ["lead"] * len(SMALL_SHARDED), "gather_small")
    for k, g4 in zip(SMALL_SHARDED, small_gathered):
        full[k] = jnp.concatenate([g4[j] for j in range(N_CHIPS)], axis=2)
    comm = ChipComm({k: w[k].astype(BF16) for k in BIG}, nl, npad)

    loss, grad_x, grads = local_step(x[0], loss_target[0], full, comm, cols)
    loss = lax.psum(loss, ("x", "y", "c"))
    halves = comm.finish()
    g_shard = {}

    small_names = SMALL_REPL + SMALL_SHARDED
    small_sum = allreduce_small(_pack_small([grads[k] for k in small_names]), "allreduce_small")
    for k, g in zip(small_names, _unpack_small(small_sum, [grads[k] for k in small_names])):
        if k in SMALL_SHARDED:
            width = w[k].shape[2]
            g = lax.dynamic_slice_in_dim(g, chip * width, width, axis=2)
        g_shard[k] = g

    delta, new_m, new_v = {}, {}, {}
    for k in BIG:
        g_shard[k], delta[k], new_m[k], new_v[k] = adamw_halves(w[k], *halves[k], m[k], v[k], "adamw_%s" % k)
    packed = [_pack_small([d_[k] for k in small_names]) for d_ in (w, g_shard, m, v)]
    outs = adamw(*packed, "adamw_small")
    for d_, buf in zip((delta, new_m, new_v), outs):
        for k, val in zip(small_names, _unpack_small(buf, [w[k] for k in small_names])):
            d_[k] = val

    return (loss, grad_x[None], *[g_shard[k] for k in WEIGHTS], *[delta[k] for k in WEIGHTS],
            *[new_m[k] for k in WEIGHTS], *[new_v[k] for k in WEIGHTS])
```

```python
import functools

import jax
import jax.numpy as jnp
from jax import lax
from jax.experimental import pallas as pl
from jax.experimental.pallas import tpu as pltpu

F32 = jnp.float32
BF16 = jnp.bfloat16

EPS = 1e-5
HEAD_DIM = 64
STATE = 128
GROUPS = 2
SHORT_K = 3
SSD_K = 4
LANES = 128
PAIR = LANES // HEAD_DIM
SCAN_CHUNK = 256
HALO = 16
N_CHIPS = 4
VMEM_LIMIT = 56 * 1024 * 1024

ADAM_LR = 0.001
ADAM_B1 = 0.9
ADAM_B2 = 0.999
ADAM_EPS = 1e-08
ADAM_WD = 0.01
ADAM_STEP = 10

MESH = pl.DeviceIdType.MESH


def _params(sem):
    return pltpu.CompilerParams(dimension_semantics=sem, vmem_limit_bytes=VMEM_LIMIT)


def _tile(n, cap, quantum):
    if n <= cap:
        return n
    best = None
    for t in range(quantum, cap + 1, quantum):
        if n % t == 0:
            best = t
    assert best is not None, (n, cap, quantum)
    return best


def _dot(a, b):
    return jnp.dot(a, b, preferred_element_type=F32)


def _dot_nt(a, b):
    return lax.dot_general(a, b, (((1,), (1,)), ((), ())), preferred_element_type=F32)


def _dot_tn(a, b):
    return lax.dot_general(a, b, (((0,), (0,)), ((), ())), preferred_element_type=F32)


def _dot_exact(a, b):
    return jnp.dot(a, b, precision=lax.Precision.HIGHEST, preferred_element_type=F32)


def _sigmoid(x):
    return pl.reciprocal(1.0 + jnp.exp(-x), approx=True)


def _softplus(x):
    return jnp.maximum(x, 0.0) + jnp.log(1.0 + jnp.exp(-jnp.abs(x)))


def _relu2(v):
    return jnp.square(jnp.maximum(v, 0.0))


class Rider:
    def __init__(self, ins, out_shapes, aliases, n_sems, copies):
        self.ins, self.out_shapes, self.aliases, self.n_sems, self.copies = ins, out_shapes, aliases, n_sems, copies


def _any_specs(n):
    return [pl.BlockSpec(memory_space=pl.ANY)] * n


def _ride(body, grid, in_specs, out_specs, out_shape, scratch, args, aliases, rider, sem, name):
    n_in, n_out, n_scr = len(in_specs), len(out_specs), len(scratch)
    if rider is None:
        outs = pl.pallas_call(
            body, name=name, grid=grid, in_specs=in_specs, out_specs=out_specs, out_shape=out_shape,
            scratch_shapes=scratch, input_output_aliases=aliases, compiler_params=_params(sem))(*args)
        return list(outs), []
    ri, ro = len(rider.ins), len(rider.out_shapes)
    last = tuple(g - 1 for g in grid)

    def wrapped(*refs):
        ins = refs[:n_in]
        r_ins = refs[n_in:n_in + ri]
        outs = refs[n_in + ri:n_in + ri + n_out]
        r_outs = refs[n_in + ri + n_out:n_in + ri + n_out + ro]
        scr = refs[n_in + ri + n_out + ro:n_in + ri + n_out + ro + n_scr]
        send_sems, recv_sems = refs[-2:]
        ids = [pl.program_id(a) for a in range(len(grid))]
        at_first = functools.reduce(jnp.logical_and, [i == 0 for i in ids])
        at_last = functools.reduce(jnp.logical_and, [i == e for i, e in zip(ids, last)])

        @pl.when(at_first)
        def _():
            for cp, _ in rider.copies(r_ins, r_outs, send_sems, recv_sems):
                cp.start()

        body(*ins, *outs, *scr)

        @pl.when(at_last)
        def _():
            for cp, landed in rider.copies(r_ins, r_outs, send_sems, recv_sems):
                cp.wait_send()
                landed.wait_recv()

    all_aliases = dict(aliases)
    all_aliases.update({n_in + a: n_out + b for a, b in rider.aliases.items()})
    outs = pl.pallas_call(
        wrapped, name=name, grid=grid, in_specs=list(in_specs) + _any_specs(ri),
        out_specs=list(out_specs) + _any_specs(ro), out_shape=list(out_shape) + list(rider.out_shapes),
        scratch_shapes=list(scratch) + [pltpu.SemaphoreType.DMA((rider.n_sems,)),
                                        pltpu.SemaphoreType.DMA((rider.n_sems,))],
        input_output_aliases=all_aliases, compiler_params=_params(sem))(*args, *rider.ins)
    return list(outs[:n_out]), list(outs[n_out:])


def run_rider(rider, name):
    ri, ro = len(rider.ins), len(rider.out_shapes)

    def body(*refs):
        send_sems, recv_sems = refs[-2:]
        pairs = rider.copies(refs[:ri], refs[ri:ri + ro], send_sems, recv_sems)
        for cp, _ in pairs:
            cp.start()
        for cp, landed in pairs:
            cp.wait_send()
            landed.wait_recv()

    return list(pl.pallas_call(
        body, name=name, in_specs=_any_specs(ri), out_specs=_any_specs(ro), out_shape=list(rider.out_shapes),
        scratch_shapes=[pltpu.SemaphoreType.DMA((rider.n_sems,)), pltpu.SemaphoreType.DMA((rider.n_sems,))],
        input_output_aliases=dict(rider.aliases),
        compiler_params=pltpu.CompilerParams(has_side_effects=True))(*rider.ins))


def norm_matmul(x, nw, w, layer, n, out_dtype, name, tail_block=None, rider=None):
    t, d = x.shape
    mxu_cols = 2 * LANES
    tn = _tile(n, 1536, mxu_cols if n % mxu_cols == 0 else LANES)
    if n % mxu_cols == 0 and tn < 1024 <= n:
        tn = _tile(n, 3072, mxu_cols)
    tm = _tile(t, 512 if tn > 1536 else 1024, 8)
    nj = n // tn

    def body(x_ref, nw_ref, w_ref, *rest):
        if tail_block is None:
            o_ref, h_ref = rest
        else:
            wt_ref, o_ref, h_ref, tail_ref = rest

        @pl.when(pl.program_id(1) == 0)
        def _():
            xf = x_ref[...]
            r = lax.rsqrt(jnp.mean(xf * xf, axis=-1, keepdims=True) + EPS)
            h_ref[...] = (xf * r * nw_ref[...]).astype(BF16)

        o_ref[...] = _dot(h_ref[...], w_ref[...]).astype(out_dtype)
        if tail_block is not None:
            @pl.when(pl.program_id(1) == nj - 1)
            def _():
                tail_ref[...] = _dot(h_ref[...], wt_ref[...])

    in_specs = [pl.BlockSpec((tm, d), lambda i, j: (i, 0)), pl.BlockSpec((1, d), lambda i, j: (0, 0)),
                pl.BlockSpec((None, d, tn), lambda i, j: (layer, 0, j))]
    out_specs = [pl.BlockSpec((tm, tn), lambda i, j: (i, j)), pl.BlockSpec((tm, d), lambda i, j: (i, 0))]
    out_shape = [jax.ShapeDtypeStruct((t, n), out_dtype), jax.ShapeDtypeStruct((t, d), BF16)]
    args = [x, nw, w]
    if tail_block is not None:
        in_specs.append(pl.BlockSpec((None, d, LANES), lambda i, j: (layer, 0, tail_block)))
        out_specs.append(pl.BlockSpec((tm, LANES), lambda i, j: (i, 0)))
        out_shape.append(jax.ShapeDtypeStruct((t, LANES), F32))
        args.append(w)
    return _ride(body, (t // tm, nj), in_specs, out_specs, out_shape, [], args, {}, rider,
                 ("parallel", "arbitrary"), name)


def matmul(lhs, w, layer, transposed, n, out_dtype, name, *, lhs_fn=None, residual=None, relu_gate=None,
           rider=None):
    t, k = lhs.shape
    tm = _tile(t, 512 if k > 2048 else 1024, 8)
    tn = _tile(n, 1024, LANES)
    staged = lhs.dtype != BF16 or lhs_fn is not None
    fn = lhs_fn if lhs_fn is not None else (lambda v: v)
    has_extra = residual is not None or relu_gate is not None
    dot = _dot_nt if transposed else _dot

    def body(*refs):
        a_ref, w_ref = refs[:2]
        extra = refs[2] if has_extra else None
        o_ref = refs[3] if has_extra else refs[2]
        if staged:
            s_ref = refs[-1]

            @pl.when(pl.program_id(1) == 0)
            def _():
                s_ref[...] = fn(a_ref[...].astype(F32)).astype(BF16)

            a_ref = s_ref
        acc = dot(a_ref[...], w_ref[...])
        if residual is not None:
            acc = acc + extra[...]
        if relu_gate is not None:
            acc = acc * (2.0 * jnp.maximum(extra[...].astype(F32), 0.0))
        o_ref[...] = acc.astype(out_dtype)

    if transposed:
        w_spec = pl.BlockSpec((None, tn, k), lambda i, j: (layer, j, 0))
    else:
        w_spec = pl.BlockSpec((None, k, tn), lambda i, j: (layer, 0, j))
    in_specs = [pl.BlockSpec((tm, k), lambda i, j: (i, 0)), w_spec]
    args = [lhs, w]
    if has_extra:
        in_specs.append(pl.BlockSpec((tm, tn), lambda i, j: (i, j)))
        args.append(residual if residual is not None else relu_gate)
    outs, extra = _ride(
        body, (t // tm, n // tn), in_specs, [pl.BlockSpec((tm, tn), lambda i, j: (i, j))],
        [jax.ShapeDtypeStruct((t, n), out_dtype)], [pltpu.VMEM((tm, k), BF16)] if staged else [], args, {},
        rider, ("parallel", "arbitrary"), name)
    return outs[0], extra


def matmul_normbwd(lhs, pieces, w, layer, x, nw, dres, name, rider=None):
    t, d = x.shape
    nl = len(lhs)
    tm = _tile(t, 512, 8)
    for off, width in pieces:
        assert off % width == 0

    def body(*refs):
        lrefs = refs[:nl]
        wrefs = refs[nl:2 * nl]
        x_ref, nw_ref, dres_ref, dx_ref, dnw_ref = refs[2 * nl:]
        dh = _dot_nt(lrefs[0][...].astype(BF16), wrefs[0][...])
        for a_ref, w_ref in zip(lrefs[1:], wrefs[1:]):
            dh = dh + _dot_nt(a_ref[...].astype(BF16), w_ref[...])
        xf = x_ref[...]
        r = lax.rsqrt(jnp.mean(xf * xf, axis=-1, keepdims=True) + EPS)
        nx = xf * r
        dn = dh * nw_ref[...]
        dx = r * (dn - nx * jnp.mean(dn * nx, axis=-1, keepdims=True))
        dx_ref[...] = dres_ref[...] + dx

        @pl.when(pl.program_id(0) == 0)
        def _():
            dnw_ref[...] = jnp.zeros_like(dnw_ref)

        dnw_ref[...] += jnp.sum(dh * nx, axis=0, keepdims=True)

    in_specs = [pl.BlockSpec((tm, width), (lambda blk: (lambda i: (i, blk)))(blk))
                for (_, blk), (_, width) in zip(lhs, pieces)]
    in_specs += [pl.BlockSpec((None, d, width), (lambda blk: (lambda i: (layer, 0, blk)))(off // width),
                              pipeline_mode=pl.Buffered(1))
                 for off, width in pieces]
    in_specs += [pl.BlockSpec((tm, d), lambda i: (i, 0)), pl.BlockSpec((1, d), lambda i: (0, 0)),
                 pl.BlockSpec((tm, d), lambda i: (i, 0))]
    return _ride(
        body, (t // tm,), in_specs,
        [pl.BlockSpec((tm, d), lambda i: (i, 0)), pl.BlockSpec((1, d), lambda i: (0, 0))],
        [jax.ShapeDtypeStruct((t, d), F32), jax.ShapeDtypeStruct((1, d), F32)], [],
        [*[a for a, _ in lhs], *([w] * nl), x, nw, dres], {}, rider, ("arbitrary",), name)


def matmul_tn(a, b, name, *, a_fn=None, by_chip=False, chip_sums=None):
    t, k = a.shape
    n = b.shape[1]
    tk = _tile(k, 1024, LANES)
    nn = n // N_CHIPS if by_chip else n
    tn = _tile(nn, 1536, 2 * LANES if nn % (2 * LANES) == 0 else LANES)
    tt = _tile(t, 1024, 8)
    nt = t // tt
    gn_ = n // tn
    steps = (k // tk) * gn_ * nt
    fn = a_fn if a_fn is not None else (lambda v: v)
    with_sums, chip_sums = chip_sums, list(chip_sums or ())
    for g, _ in chip_sums:
        if (g.shape[0] * g.shape[1] // 2) % (steps * HALO) or steps % g.shape[0]:
            return (matmul_tn(a, b, name, a_fn=a_fn, by_chip=by_chip),
                    [tuple(chip_sum(g_, r_, "chip_sum")) for g_, r_ in chip_sums])
    ns = len(chip_sums)
    step = lambda i, j, s: (i * gn_ + j) * nt + s

    def body(*refs):
        a_ref, b_ref = refs[:2]
        side_in = refs[2:2 + 3 * ns]
        o_ref = refs[2 + 3 * ns]
        side_out = refs[3 + 3 * ns:3 + 5 * ns]
        acc_ref = refs[-1]

        @pl.when(pl.program_id(2) == 0)
        def _():
            acc_ref[...] = jnp.zeros_like(acc_ref)

        av = a_ref[...]
        if a_fn is not None:
            av = fn(av.astype(F32))
        acc_ref[...] += _dot_tn(av.astype(BF16), b_ref[...].astype(BF16))
        for q in range(ns):
            g0_ref, g1_ref, r_ref = side_in[3 * q:3 * q + 3]
            tot = jnp.where(lax.axis_index("c") == 0, g0_ref[...], g1_ref[...]) + r_ref[...]
            side_out[2 * q][...] = tot
            side_out[2 * q + 1][...] = tot.astype(BF16)

        @pl.when(pl.program_id(2) == nt - 1)
        def _():
            o_ref[...] = acc_ref[...]

    if by_chip:
        per = n // N_CHIPS // tn
        out_specs = [pl.BlockSpec((None, tk, tn), lambda i, j, s: (j // per, i, j % per))]
        out_shape = [jax.ShapeDtypeStruct((N_CHIPS, k, n // N_CHIPS), F32)]
    else:
        out_specs = [pl.BlockSpec((tk, tn), lambda i, j, s: (i, j))]
        out_shape = [jax.ShapeDtypeStruct((k, n), F32)]
    in_specs = [pl.BlockSpec((tt, tk), lambda i, j, s: (s, i)), pl.BlockSpec((tt, tn), lambda i, j, s: (s, j))]
    args = [a, b]
    for g, recv in chip_sums:
        nch, r, c = g.shape
        r2 = r // 2
        rows = nch * r2 // steps
        per_chip = r2 // rows
        assert rows % HALO == 0 and r2 % rows == 0
        for half in range(2):
            in_specs.append(pl.BlockSpec(
                (rows, c), (lambda h: (lambda i, j, s: ((step(i, j, s) // per_chip) * 2 * per_chip + h * per_chip
                                                        + step(i, j, s) % per_chip, 0)))(half)))
        flat = pl.BlockSpec((rows, c), lambda i, j, s: (step(i, j, s), 0))
        in_specs.append(flat)
        args += [g.reshape(nch * r, c), g.reshape(nch * r, c), recv.reshape(nch * r2, c)]
        out_specs += [flat, flat]
        out_shape += [jax.ShapeDtypeStruct((nch * r2, c), F32), jax.ShapeDtypeStruct((nch * r2, c), BF16)]
    outs = pl.pallas_call(
        body, name=name, grid=(k // tk, gn_, nt), in_specs=in_specs, out_specs=out_specs, out_shape=out_shape,
        scratch_shapes=[pltpu.VMEM((tk, tn), F32)],
        compiler_params=_params(("parallel", "parallel", "arbitrary")),
    )(*args)
    if with_sums is None:
        return outs[0]
    sums = [(outs[1 + 2 * q].reshape(g.shape[0], g.shape[1] // 2, g.shape[2]),
             outs[2 + 2 * q].reshape(g.shape[0], g.shape[1] // 2, g.shape[2])) for q, (g, _) in enumerate(chip_sums)]
    return outs[0], sums


def split_to_chips(pieces, cols, name):
    d = pieces[0].shape[0]
    widths = [p.shape[1] for p in pieces]
    w = cols // N_CHIPS
    tr = _tile(d, 256, 8)
    npc = len(pieces)

    def body(*refs):
        o_ref, row = refs[npc], refs[npc + 1]
        off = 0
        for r, n in zip(refs[:npc], widths):
            row[:, off:off + n] = r[...]
            off += n
        for j in range(N_CHIPS):
            o_ref[j] = row[:, j * w:(j + 1) * w]

    return pl.pallas_call(
        body, name=name, grid=(d // tr,),
        in_specs=[pl.BlockSpec((tr, n), lambda i: (i, 0)) for n in widths],
        out_specs=pl.BlockSpec((N_CHIPS, tr, w), lambda i: (0, i, 0)),
        out_shape=jax.ShapeDtypeStruct((N_CHIPS, d, w), F32),
        scratch_shapes=[pltpu.VMEM((tr, sum(widths)), F32)],
        compiler_params=_params(("parallel",)),
    )(*pieces)


def join_from_chips(g4, npad, name):
    _, nl, d, w = g4.shape
    tr = _tile(d, 256, HALO)

    def body(g_ref, o_ref):
        for j in range(N_CHIPS):
            o_ref[:, j * w:(j + 1) * w] = g_ref[j]
        o_ref[:, N_CHIPS * w:] = jnp.zeros((tr, npad - N_CHIPS * w), o_ref.dtype)

    return pl.pallas_call(
        body, name=name, grid=(nl, d // tr),
        in_specs=[pl.BlockSpec((N_CHIPS, None, tr, w), lambda l, i: (0, l, i, 0))],
        out_specs=pl.BlockSpec((None, tr, npad), lambda l, i: (l, i, 0)),
        out_shape=jax.ShapeDtypeStruct((nl, d, npad), g4.dtype),
        compiler_params=_params(("parallel", "parallel")),
    )(g4)


def conv_mixer_fwd(proj, kw, cw, out_cols, name):
    t = proj.shape[0]
    tm = _tile(t, 1024, HALO)
    tc = _tile(cw, 1024, LANES)
    nj = cw // tc
    hb = tm // HALO

    def body(ub_ref, uc_ref, uh_ref, ucp_ref, uhp_ref, kw_ref, y_ref):
        i = pl.program_id(0)
        taps = [kw_ref[pl.ds(k, 1), :] for k in range(SHORT_K)]
        row = lax.broadcasted_iota(jnp.int32, (8, tc), 0)
        vp = ucp_ref[...].astype(F32) * uhp_ref[...].astype(F32)

        def conv(block, before):
            acc = taps[SHORT_K - 1] * block
            for k in range(SHORT_K - 1):
                s = SHORT_K - 1 - k
                acc = acc + taps[k] * jnp.where(row >= s, pltpu.roll(block, s, 0), pltpu.roll(before, s, 0))
            return acc

        def strip(s, before):
            rows = pl.ds(pl.multiple_of(s * HALO, HALO), HALO)
            v = uc_ref[rows, :].astype(F32) * uh_ref[rows, :].astype(F32)
            top, bottom = v[0:8], v[8:HALO]
            cv = jnp.concatenate([conv(top, before), conv(bottom, top)], axis=0)
            y_ref[rows, :] = (ub_ref[rows, :].astype(F32) * cv).astype(BF16)
            return bottom

        lax.fori_loop(0, tm // HALO, strip, jnp.where(i > 0, vp[8:HALO], 0.0))

    prev = lambda off: (lambda i, j: (jnp.maximum(i * hb - 1, 0), off + j))
    return pl.pallas_call(
        body, name=name, grid=(t // tm, nj),
        in_specs=[pl.BlockSpec((tm, tc), lambda i, j: (i, j)),
                  pl.BlockSpec((tm, tc), lambda i, j: (i, nj + j)),
                  pl.BlockSpec((tm, tc), lambda i, j: (i, 2 * nj + j)),
                  pl.BlockSpec((HALO, tc), prev(nj)),
                  pl.BlockSpec((HALO, tc), prev(2 * nj)),
                  pl.BlockSpec((SHORT_K, tc), lambda i, j: (0, j))],
        out_specs=pl.BlockSpec((tm, tc), lambda i, j: (i, j)),
        out_shape=jax.ShapeDtypeStruct((t, out_cols), BF16),
        compiler_params=_params(("parallel", "parallel")),
    )(proj, proj, proj, proj, proj, kw)


def conv_mixer_bwd(proj, dy, kw, cw, name):
    t = proj.shape[0]
    tm = _tile(t, 1024, HALO)
    tc = cw
    nj = cw // tc
    hb = tm // HALO
    ni = t // tm
    last_hb = t // HALO - 1

    def body(ub_ref, uc_ref, uh_ref, dy_ref, ucp_ref, uhp_ref, ubn_ref, dyn_ref, kw_ref,
             du_ref, dkw_ref):
        i = pl.program_id(1)
        nstrips = tm // HALO
        taps = [kw_ref[pl.ds(k, 1), :] for k in range(SHORT_K)]
        row = lax.broadcasted_iota(jnp.int32, (8, tc), 0)
        vp = ucp_ref[...].astype(F32) * uhp_ref[...].astype(F32)
        dcvn = dyn_ref[...].astype(F32) * ubn_ref[...].astype(F32)

        def shifted(block, before, s):
            return jnp.where(row >= s, pltpu.roll(block, s, 0), pltpu.roll(before, s, 0))

        def lifted(block, after, s):
            return jnp.where(row < 8 - s, pltpu.roll(block, 8 - s, 0), pltpu.roll(after, 8 - s, 0))

        def down(s, carry):
            before, sums = carry
            rows = pl.ds(pl.multiple_of(s * HALO, HALO), HALO)
            v = uc_ref[rows, :].astype(F32) * uh_ref[rows, :].astype(F32)
            dyv = dy_ref[rows, :].astype(F32)
            dcv = dyv * ub_ref[rows, :].astype(F32)
            cvs = []
            sums = list(sums)
            for block, above, dcb in ((v[0:8], before, dcv[0:8]), (v[8:HALO], v[0:8], dcv[8:HALO])):
                moved = [shifted(block, above, SHORT_K - 1 - k) for k in range(SHORT_K - 1)] + [block]
                cvs.append(sum(taps[k] * moved[k] for k in range(SHORT_K)))
                sums = [sums[k] + dcb * moved[k] for k in range(SHORT_K)]
            du_ref[rows, 0:cw] = (dyv * jnp.concatenate(cvs, axis=0)).astype(BF16)
            return v[8:HALO], tuple(sums)

        zero = jnp.zeros((8, tc), F32)
        _, sums = lax.fori_loop(0, nstrips, down, (jnp.where(i > 0, vp[8:HALO], 0.0), (zero,) * SHORT_K))

        def up(n, after):
            rows = pl.ds(pl.multiple_of((nstrips - 1 - n) * HALO, HALO), HALO)
            uc = uc_ref[rows, :].astype(F32)
            uh = uh_ref[rows, :].astype(F32)
            dcv = dy_ref[rows, :].astype(F32) * ub_ref[rows, :].astype(F32)
            dvs = []
            for block, below in ((dcv[0:8], dcv[8:HALO]), (dcv[8:HALO], after)):
                dvs.append(taps[SHORT_K - 1] * block
                           + sum(taps[k] * lifted(block, below, SHORT_K - 1 - k) for k in range(SHORT_K - 1)))
            dv = jnp.concatenate(dvs, axis=0)
            du_ref[rows, cw:2 * cw] = (dv * uh).astype(BF16)
            du_ref[rows, 2 * cw:3 * cw] = (dv * uc).astype(BF16)
            return dcv[0:8]

        lax.fori_loop(0, nstrips, up, jnp.where(i < ni - 1, dcvn[0:8], 0.0))

        @pl.when(i == 0)
        def _():
            dkw_ref[...] = jnp.zeros_like(dkw_ref)

        for k in range(SHORT_K):
            dkw_ref[pl.ds(k, 1), :] += jnp.sum(sums[k], axis=0, keepdims=True)

    prev = lambda off: (lambda j, i: (jnp.maximum(i * hb - 1, 0), off + j))
    nxt = lambda off: (lambda j, i: (jnp.minimum((i + 1) * hb, last_hb), off + j))
    cur = lambda off: (lambda j, i: (i, off + j))
    return pl.pallas_call(
        body, name=name, grid=(nj, ni),
        in_specs=[pl.BlockSpec((tm, tc), cur(0)), pl.BlockSpec((tm, tc), cur(nj)),
                  pl.BlockSpec((tm, tc), cur(2 * nj)), pl.BlockSpec((tm, tc), cur(0)),
                  pl.BlockSpec((HALO, tc), prev(nj)), pl.BlockSpec((HALO, tc), prev(2 * nj)),
                  pl.BlockSpec((HALO, tc), nxt(0)), pl.BlockSpec((HALO, tc), nxt(0)),
                  pl.BlockSpec((SHORT_K, tc), lambda j, i: (0, j))],
        out_specs=[pl.BlockSpec((tm, 3 * cw), lambda j, i: (i, 0)),
                   pl.BlockSpec((SHORT_K, tc), lambda j, i: (0, j))],
        out_shape=[jax.ShapeDtypeStruct((t, 3 * cw), BF16), jax.ShapeDtypeStruct((SHORT_K, cw), F32)],
        compiler_params=_params(("parallel", "arbitrary")),
    )(proj, proj, proj, dy, proj, proj, proj, dy, kw)


def _head_column(mat, lane, h):
    return jnp.sum(jnp.where(lane == h, mat, 0.0), axis=-1, keepdims=True)


def _ssd_common(dt_raw_ref, dtb_ref, aneg_ref, cum_s, cumt_s, chunk):
    dt = _softplus(dt_raw_ref[...] + dtb_ref[...])
    al = dt * aneg_ref[...]
    ri = lax.broadcasted_iota(jnp.int32, (chunk, chunk), 0)
    ci = lax.broadcasted_iota(jnp.int32, (chunk, chunk), 1)
    cum = _dot_exact((ri >= ci).astype(F32), al)
    cum_s[...] = cum
    cumt_s[...] = cum.T
    return dt, cum, ri >= ci


EDGE = 16


def _shift_matrices(shift_s, chunk, kk, up):
    ri = lax.broadcasted_iota(jnp.int32, (chunk, chunk), 0)
    ci = lax.broadcasted_iota(jnp.int32, (chunk, chunk), 1)
    for k in range(kk - 1):
        s = kk - 1 - k
        shift_s[k] = ((ci - ri if up else ri - ci) == s).astype(BF16)


def _causal_conv(cur, head, kw_ref, b_ref, shift_s, kk):
    acc = b_ref[...] + kw_ref[pl.ds(kk - 1, 1), :] * cur.astype(F32)
    top = b_ref[...] + kw_ref[pl.ds(kk - 1, 1), :] * head[pl.ds(8, EDGE), :]
    for k in range(kk - 1):
        acc = acc + kw_ref[pl.ds(k, 1), :] * _dot(shift_s[k], cur)
        top = top + kw_ref[pl.ds(k, 1), :] * head[pl.ds(8 - (kk - 1) + k, EDGE), :]
    return acc, top


def ssd_fwd(proj, dt_raw, y_mix, kw_xs, kw_bc, b_xs, b_bc, dtb, aneg, dskip, normw, cw, si, name, rider=None):
    t = proj.shape[0]
    ch = min(SCAN_CHUNK, t)
    nc = t // ch
    npair = si // LANES
    ppg = npair // GROUPS
    gn = GROUPS * STATE
    gw = si // GROUPS
    assert cw == si and (3 * cw + 2 * si) % (2 * gn) == 0
    zblk = 3 * cw // si
    xsblk = zblk + 1
    bcblk = (3 * cw + 2 * si) // (2 * gn)

    def body(z_ref, xs_ref, bc_ref, dtr_ref, ymix_ref, kwx_ref, kwb_ref, bx_ref, bb_ref, dtb_ref, aneg_ref, dsk_ref,
             nw_ref, yb_ref, ys_ref, hs_ref, xcx_ref, xcb_ref,
             headx, headb, shift_s, xs_s, bc_s, h_s, gated_s, s_s, cum_s, cumt_s):
        del ymix_ref
        c = pl.program_id(0)

        @pl.when(c == 0)
        def _():
            h_s[...] = jnp.zeros_like(h_s)
            headx[0:8, :] = jnp.zeros((8, si), F32)
            headb[0:8, :] = jnp.zeros((8, 2 * gn), F32)
            _shift_matrices(shift_s, ch, SSD_K, up=False)

        for raw_ref, head, kw_ref, b_ref, pre_ref, act_s in ((xs_ref, headx, kwx_ref, bx_ref, xcx_ref, xs_s),
                                                           (bc_ref, headb, kwb_ref, bb_ref, xcb_ref, bc_s)):
            head[8:8 + EDGE, :] = raw_ref[0:EDGE, :].astype(F32)
            pre, top = _causal_conv(raw_ref[...], head, kw_ref, b_ref, shift_s, SSD_K)
            head[0:8, :] = raw_ref[ch - EDGE:ch, :].astype(F32)[EDGE - 8:EDGE]
            pre_ref[...] = pre.astype(BF16)
            pre_ref[0:EDGE, :] = top.astype(BF16)
            act_s[...] = (pre * _sigmoid(pre)).astype(act_s.dtype)
            act_s[0:EDGE, :] = (top * _sigmoid(top)).astype(act_s.dtype)

        dt, cum, tril = _ssd_common(dtr_ref, dtb_ref, aneg_ref, cum_s, cumt_s, ch)
        lane = lax.broadcasted_iota(jnp.int32, (ch, LANES), 1)
        lane1 = lax.broadcasted_iota(jnp.int32, (1, LANES), 1)
        low = lane < HEAD_DIM
        clast = cum_s[pl.ds(ch - 1, 1), :]

        for p in range(npair):
            g = p // ppg
            col = slice(p * LANES, (p + 1) * LANES)
            bg = bc_s[:, g * STATE:(g + 1) * STATE]
            cg = bc_s[:, gn + g * STATE:gn + (g + 1) * STATE]
            if p % ppg == 0:
                s_s[...] = _dot_nt(cg, bg)
            heads = (PAIR * p, PAIR * p + 1)
            ccol = [_head_column(cum, lane, h) for h in heads]
            dcol = [_head_column(dt, lane, h) for h in heads]
            cl = [jnp.sum(jnp.where(lane1 == h, clast, 0.0), axis=-1, keepdims=True) for h in heads]
            cum_px = jnp.where(low, ccol[0], ccol[1])
            dt_px = jnp.where(low, dcol[0], dcol[1])
            cl_px = jnp.where(lane1 < HEAD_DIM, cl[0], cl[1])
            xs_p = xs_s[:, col]
            xdt = xs_p * dt_px
            y = dsk_ref[:, col] * xs_p
            for hi, h in enumerate(heads):
                dec = jnp.exp(jnp.where(tril, ccol[hi] - cumt_s[pl.ds(h, 1), :], -jnp.inf))
                wm = (s_s[...] * dec).astype(BF16)
                xm = jnp.where(low if hi == 0 else jnp.logical_not(low), xdt, 0.0).astype(BF16)
                y = y + _dot(wm, xm)
            hp = h_s[p]
            hs_ref[0, p] = hp
            y = y + _dot(cg, hp.astype(BF16)) * jnp.exp(cum_px)
            st = _dot_tn(bg, (xdt * jnp.exp(cl_px - cum_px)).astype(BF16))
            h_s[p] = jnp.exp(cl_px) * hp + st
            ys_ref[:, col] = y.astype(BF16)
            zp = z_ref[:, col].astype(F32)
            gated_s[:, col] = y * zp * _sigmoid(zp)

        for g in range(GROUPS):
            col = slice(g * gw, (g + 1) * gw)
            gg = gated_s[:, col]
            r = lax.rsqrt(jnp.mean(gg * gg, axis=-1, keepdims=True) + EPS)
            yb_ref[:, col] = (gg * r * nw_ref[:, col]).astype(BF16)

    full = lambda shape: pl.BlockSpec(shape, lambda c: tuple(0 for _ in shape))
    return _ride(
        body, (nc,),
        [pl.BlockSpec((ch, si), lambda c: (c, zblk)),
         pl.BlockSpec((ch, si), lambda c: (c, xsblk)),
         pl.BlockSpec((ch, 2 * gn), lambda c: (c, bcblk)),
         pl.BlockSpec((ch, LANES), lambda c: (c, 0)),
         pl.BlockSpec(memory_space=pl.ANY),
         full((SSD_K, si)), full((SSD_K, 2 * gn)), full((1, si)), full((1, 2 * gn)),
         full((1, LANES)), full((1, LANES)), full((1, si)), full((1, si))],
        [pl.BlockSpec((ch, si), lambda c: (c, cw // si)),
         pl.BlockSpec((ch, si), lambda c: (c, 0)),
         pl.BlockSpec((1, npair, STATE, LANES), lambda c: (c, 0, 0, 0)),
         pl.BlockSpec((ch, si), lambda c: (c, 0)), pl.BlockSpec((ch, 2 * gn), lambda c: (c, 0))],
        [jax.ShapeDtypeStruct(y_mix.shape, BF16), jax.ShapeDtypeStruct((t, si), BF16),
         jax.ShapeDtypeStruct((nc, npair, STATE, LANES), F32),
         jax.ShapeDtypeStruct((t, si), BF16), jax.ShapeDtypeStruct((t, 2 * gn), BF16)],
        [pltpu.VMEM((8 + EDGE, si), F32), pltpu.VMEM((8 + EDGE, 2 * gn), F32),
         pltpu.VMEM((SSD_K - 1, ch, ch), BF16),
         pltpu.VMEM((ch, si), F32), pltpu.VMEM((ch, 2 * gn), BF16),
         pltpu.VMEM((npair, STATE, LANES), F32), pltpu.VMEM((ch, si), F32),
         pltpu.VMEM((ch, ch), F32), pltpu.VMEM((ch, LANES), F32), pltpu.VMEM((LANES, ch), F32)],
        [proj, proj, proj, dt_raw, y_mix, kw_xs, kw_bc, b_xs, b_bc, dtb, aneg, dskip, normw], {4: 0}, rider,
        ("arbitrary",), name)


def ssd_bwd(proj, dt_raw, ys, hsave, pre_xs, pre_bc, dy, kw_xs, kw_bc, dtb, aneg, dskip, normw, cw, si, name,
            rider=None):
    t = proj.shape[0]
    ch = min(SCAN_CHUNK, t)
    nc = t // ch
    npair = si // LANES
    ppg = npair // GROUPS
    gn = GROUPS * STATE
    gw = si // GROUPS
    zblk = 3 * cw // si
    xsblk = zblk + 1
    bcblk = (3 * cw + 2 * si) // (2 * gn)

    def body(z_ref, xs_ref, bc_ref, xcx_ref, xcb_ref, dtr_ref, ys_ref, hs_ref, dyb_ref,
             kwx_ref, kwb_ref, dtb_ref, aneg_ref, dsk_ref, nw_ref,
             dp_ref, ddt_ref, dkwx_ref, dkwb_ref, dbx_ref, dbb_ref, ddtb_ref, da_ref, ddsk_ref,
             dnw_ref,
             tailx, tailb, shift_s, xs_s, bc_s, dsx_s, dsb_s, dy_s, dxs_s, dbc_s, dh_s, s_s, ds_s,
             cum_s, cumt_s, dccol_s, dcrow_s, ddtcol_s, dcl_s):
        i = pl.program_id(0)

        @pl.when(i == 0)
        def _():
            dh_s[...] = jnp.zeros_like(dh_s)
            tailx[EDGE:EDGE + 8, :] = jnp.zeros((8, si), F32)
            tailb[EDGE:EDGE + 8, :] = jnp.zeros((8, 2 * gn), F32)
            _shift_matrices(shift_s, ch, SSD_K, up=True)
            for r in (dkwx_ref, dkwb_ref, dbx_ref, dbb_ref, ddtb_ref, da_ref, ddsk_ref, dnw_ref):
                r[...] = jnp.zeros_like(r)

        xc = xcx_ref[...].astype(F32)
        sg = _sigmoid(xc)
        xs_s[...] = xc * sg
        dsx_s[...] = sg * (1.0 + xc * (1.0 - sg))
        bcc = xcb_ref[...].astype(F32)
        sgb = _sigmoid(bcc)
        bc_s[...] = (bcc * sgb).astype(BF16)
        dsb_s[...] = sgb * (1.0 + bcc * (1.0 - sgb))

        dt, cum, tril = _ssd_common(dtr_ref, dtb_ref, aneg_ref, cum_s, cumt_s, ch)
        lane = lax.broadcasted_iota(jnp.int32, (ch, LANES), 1)
        lane1 = lax.broadcasted_iota(jnp.int32, (1, LANES), 1)
        low = lane < HEAD_DIM
        low1 = lane1 < HEAD_DIM
        clast = cum_s[pl.ds(ch - 1, 1), :]

        for g in range(GROUPS):
            col = slice(g * gw, (g + 1) * gw)
            ysf = ys_ref[:, col].astype(F32)
            zf = z_ref[:, col].astype(F32)
            sz = _sigmoid(zf)
            silz = zf * sz
            gg = ysf * silz
            r = lax.rsqrt(jnp.mean(gg * gg, axis=-1, keepdims=True) + EPS)
            nrm = gg * r
            dyb = dyb_ref[:, col].astype(F32)
            dnw_ref[:, col] += jnp.sum(dyb * nrm, axis=0, keepdims=True)
            dn = dyb * nw_ref[:, col]
            dgg = r * (dn - nrm * jnp.mean(dn * nrm, axis=-1, keepdims=True))
            dy_s[:, col] = dgg * silz
            dp_ref[:, col] = (dgg * ysf * (sz * (1.0 + zf * (1.0 - sz)))).astype(BF16)

        dccol_s[...] = jnp.zeros_like(dccol_s)
        dcrow_s[...] = jnp.zeros_like(dcrow_s)
        ddtcol_s[...] = jnp.zeros_like(ddtcol_s)
        dcl_s[...] = jnp.zeros_like(dcl_s)
        dbc_s[...] = jnp.zeros_like(dbc_s)

        for p in range(npair):
            g = p // ppg
            col = slice(p * LANES, (p + 1) * LANES)
            bcol = slice(g * STATE, (g + 1) * STATE)
            ccolg = slice(gn + g * STATE, gn + (g + 1) * STATE)
            bg = bc_s[:, bcol]
            cg = bc_s[:, ccolg]
            if p % ppg == 0:
                s_s[...] = _dot_nt(cg, bg)
                ds_s[...] = jnp.zeros_like(ds_s)
            heads = (PAIR * p, PAIR * p + 1)
            masks = (low, jnp.logical_not(low))
            masks1 = (low1, jnp.logical_not(low1))
            ccol = [_head_column(cum, lane, h) for h in heads]
            dcol = [_head_column(dt, lane, h) for h in heads]
            cl = [jnp.sum(jnp.where(lane1 == h, clast, 0.0), axis=-1, keepdims=True) for h in heads]
            cum_px = jnp.where(low, ccol[0], ccol[1])
            dt_px = jnp.where(low, dcol[0], dcol[1])
            cl_px = jnp.where(low1, cl[0], cl[1])
            e_px = jnp.exp(cum_px)
            dec_end = jnp.exp(cl_px - cum_px)
            gdec = jnp.exp(cl_px)
            xs_p = xs_s[:, col]
            xdt = xs_p * dt_px
            dyp = dy_s[:, col]
            hc = hs_ref[0, p]
            hcb = hc.astype(BF16)
            dhn = dh_s[p]
            dhnb = dhn.astype(BF16)

            ddsk_ref[:, col] += jnp.sum(dyp * xs_p, axis=0, keepdims=True)
            dxs_acc = dsk_ref[:, col] * dyp
            dye = dyp * e_px
            dyeb = dye.astype(BF16)
            dbc_s[:, ccolg] += _dot_nt(dyeb, hcb)
            dcum_lane = dye * _dot(cg, hcb)
            dh_from_y = _dot_tn(cg, dyeb)
            xd = xdt * dec_end
            dxd = _dot(bg, dhnb)
            dbc_s[:, bcol] += _dot_nt(xd.astype(BF16), dhnb)
            dxdt = dxd * dec_end
            t1 = dxd * xd
            dcum_lane = dcum_lane - t1
            dcl_lane = jnp.sum(t1, axis=0, keepdims=True) + jnp.sum(dhn * hc, axis=0, keepdims=True) * gdec
            dh_s[p] = gdec * dhn + dh_from_y
            xdtb = xdt.astype(BF16)
            for hi, h in enumerate(heads):
                dym = jnp.where(masks[hi], dyp, 0.0).astype(BF16)
                dw = _dot_nt(dym, xdtb)
                dec = jnp.exp(jnp.where(tril, ccol[hi] - cumt_s[pl.ds(h, 1), :], -jnp.inf))
                wm = s_s[...] * dec
                dxdt = dxdt + _dot_tn(wm.astype(BF16), dym)
                ds_s[...] += dw * dec
                gm = dw * wm
                rowsum = jnp.sum(gm, axis=-1, keepdims=True)
                lanesum = jnp.sum(jnp.where(masks[hi], dcum_lane, 0.0), axis=-1, keepdims=True)
                dccol_s[...] += jnp.where(lane == h, rowsum + lanesum, 0.0)
                dcrow_s[pl.ds(h, 1), :] = jnp.sum(gm, axis=0, keepdims=True)
                dcl_h = jnp.sum(jnp.where(masks1[hi], dcl_lane, 0.0), axis=-1, keepdims=True)
                dcl_s[...] += jnp.where(lane1 == h, dcl_h, 0.0)
            ddt_lane = dxdt * xs_p
            for hi, h in enumerate(heads):
                s = jnp.sum(jnp.where(masks[hi], ddt_lane, 0.0), axis=-1, keepdims=True)
                ddtcol_s[...] += jnp.where(lane == h, s, 0.0)
            dxs_s[:, col] = dxs_acc + dxdt * dt_px
            if p % ppg == ppg - 1:
                dsb = ds_s[...].astype(BF16)
                dbc_s[:, ccolg] += _dot(dsb, bg)
                dbc_s[:, bcol] += _dot_tn(dsb, cg)

        rowi = lax.broadcasted_iota(jnp.int32, (ch, LANES), 0)
        dcum = dccol_s[...] - dcrow_s[...].T + jnp.where(rowi == ch - 1, dcl_s[...], 0.0)
        ri = lax.broadcasted_iota(jnp.int32, (ch, ch), 0)
        ci = lax.broadcasted_iota(jnp.int32, (ch, ch), 1)
        dal = _dot_exact((ri <= ci).astype(F32), dcum)
        ddt = dal * aneg_ref[...] + ddtcol_s[...]
        da_ref[...] += jnp.sum(dal * dt, axis=0, keepdims=True)
        ddtr = ddt * _sigmoid(dtr_ref[...] + dtb_ref[...])
        ddt_ref[...] = ddtr
        ddtb_ref[...] += jnp.sum(ddtr, axis=0, keepdims=True)

        for (dpost, dsl, tail, raw_ref, kw_ref, dkw_ref, db_ref, out) in (
                (dxs_s, dsx_s, tailx, xs_ref, kwx_ref, dkwx_ref, dbx_ref, slice(si, 2 * si)),
                (dbc_s, dsb_s, tailb, bc_ref, kwb_ref, dkwb_ref, dbb_ref, slice(2 * si, 2 * si + 2 * gn))):
            dxc = dpost[...] * dsl[...]
            dxcb = dxc.astype(BF16)
            raw = raw_ref[...].astype(F32)
            raw_end = raw_ref[ch - EDGE:ch, :].astype(F32)
            tail[0:EDGE, :] = dxcb[ch - EDGE:ch].astype(F32)
            db_ref[...] += jnp.sum(dxc, axis=0, keepdims=True)
            draw = kw_ref[pl.ds(SSD_K - 1, 1), :] * dxc
            dkw_ref[pl.ds(SSD_K - 1, 1), :] += jnp.sum(dxc * raw, axis=0, keepdims=True)
            fix = jnp.zeros((EDGE, dxc.shape[1]), F32)
            for k in range(SSD_K - 1):
                moved = _dot(shift_s[k], dxcb)
                miss = tail[pl.ds(SSD_K - 1 - k, EDGE), :] - moved[ch - EDGE:ch]
                draw = draw + kw_ref[pl.ds(k, 1), :] * moved
                fix = fix + kw_ref[pl.ds(k, 1), :] * miss
                dkw_ref[pl.ds(k, 1), :] += (jnp.sum(moved * raw, axis=0, keepdims=True)
                                            + jnp.sum(miss * raw_end, axis=0, keepdims=True))
            dp_ref[:, out] = draw.astype(BF16)
            dp_ref[ch - EDGE:ch, out] = (draw[ch - EDGE:ch] + fix).astype(BF16)
            tail[EDGE:EDGE + 8, :] = dxcb[0:EDGE].astype(F32)[0:8]

    full = lambda shape: pl.BlockSpec(shape, lambda i: tuple(0 for _ in shape))
    rev = lambda blk: (lambda i: (nc - 1 - i, blk))
    small_in = [(SSD_K, si), (SSD_K, 2 * gn), (1, LANES), (1, LANES), (1, si), (1, si)]
    small = [(SSD_K, si), (SSD_K, 2 * gn), (1, si), (1, 2 * gn), (1, LANES), (1, LANES), (1, si), (1, si)]
    return _ride(
        body, (nc,),
        [pl.BlockSpec((ch, si), rev(zblk)), pl.BlockSpec((ch, si), rev(xsblk)),
         pl.BlockSpec((ch, 2 * gn), rev(bcblk)),
         pl.BlockSpec((ch, si), rev(0)), pl.BlockSpec((ch, 2 * gn), rev(0)),
         pl.BlockSpec((ch, LANES), rev(0)), pl.BlockSpec((ch, si), rev(0)),
         pl.BlockSpec((1, npair, STATE, LANES), lambda i: (nc - 1 - i, 0, 0, 0)),
         pl.BlockSpec((ch, si), rev(cw // si))] + [full(s) for s in small_in],
        [pl.BlockSpec((ch, 2 * si + 2 * gn), rev(0)), pl.BlockSpec((ch, LANES), rev(0))] + [full(s) for s in small],
        [jax.ShapeDtypeStruct((t, 2 * si + 2 * gn), BF16), jax.ShapeDtypeStruct((t, LANES), F32)]
        + [jax.ShapeDtypeStruct(s, F32) for s in small],
        [pltpu.VMEM((EDGE + 8, si), F32), pltpu.VMEM((EDGE + 8, 2 * gn), F32),
         pltpu.VMEM((SSD_K - 1, ch, ch), BF16),
         pltpu.VMEM((ch, si), F32), pltpu.VMEM((ch, 2 * gn), BF16),
         pltpu.VMEM((ch, si), F32), pltpu.VMEM((ch, 2 * gn), F32),
         pltpu.VMEM((ch, si), F32), pltpu.VMEM((ch, si), F32), pltpu.VMEM((ch, 2 * gn), F32),
         pltpu.VMEM((npair, STATE, LANES), F32),
         pltpu.VMEM((ch, ch), F32), pltpu.VMEM((ch, ch), F32),
         pltpu.VMEM((ch, LANES), F32), pltpu.VMEM((LANES, ch), F32),
         pltpu.VMEM((ch, LANES), F32), pltpu.VMEM((LANES, ch), F32),
         pltpu.VMEM((ch, LANES), F32), pltpu.VMEM((1, LANES), F32)],
        [proj, proj, proj, pre_xs, pre_bc, dt_raw, ys, hsave, dy, kw_xs, kw_bc, dtb, aneg, dskip, normw],
        {}, rider, ("arbitrary",), name)


def final_loss(x, nw, tgt, name):
    t, d = x.shape
    tm = _tile(t, 512, 8)

    def body(x_ref, nw_ref, t_ref, dx_ref, dnw_ref, ls_ref):
        xf = x_ref[...]
        r = lax.rsqrt(jnp.mean(xf * xf, axis=-1, keepdims=True) + EPS)
        nx = xf * r
        e = nx * nw_ref[...] - t_ref[...]
        dyv = e * (1.0 / d)
        dn = dyv * nw_ref[...]
        dx_ref[...] = r * (dn - nx * jnp.mean(dn * nx, axis=-1, keepdims=True))

        @pl.when(pl.program_id(0) == 0)
        def _():
            dnw_ref[...] = jnp.zeros_like(dnw_ref)
            ls_ref[...] = jnp.zeros_like(ls_ref)

        dnw_ref[...] += jnp.sum(dyv * nx, axis=0, keepdims=True)
        ls_ref[...] += jnp.sum(e * e, axis=0, keepdims=True) * (0.5 / d)

    return pl.pallas_call(
        body, name=name, grid=(t // tm,),
        in_specs=[pl.BlockSpec((tm, d), lambda i: (i, 0)), pl.BlockSpec((1, d), lambda i: (0, 0)),
                  pl.BlockSpec((tm, d), lambda i: (i, 0))],
        out_specs=[pl.BlockSpec((tm, d), lambda i: (i, 0)), pl.BlockSpec((1, d), lambda i: (0, 0)),
                   pl.BlockSpec((1, d), lambda i: (0, 0))],
        out_shape=[jax.ShapeDtypeStruct((t, d), F32), jax.ShapeDtypeStruct((1, d), F32),
                   jax.ShapeDtypeStruct((1, d), F32)],
        compiler_params=_params(("arbitrary",)),
    )(x, nw, tgt)


def _rows3(a):
    if a.ndim == 1:
        return a.reshape(1, 1, a.shape[0])
    if a.ndim == 2:
        return a.reshape(1, *a.shape)
    return a.reshape(-1, a.shape[-2], a.shape[-1])


def adamw(w, g, m, v, name):
    shape = w.shape
    views = [_rows3(a) for a in (w, g, m, v)]
    b, r, c = views[0].shape
    tr = _tile(r, 256, 16) if r % 16 == 0 else r

    def body(w_ref, g_ref, m_ref, v_ref, d_ref, nm_ref, nv_ref):
        g = g_ref[...]
        m = ADAM_B1 * m_ref[...] + (1.0 - ADAM_B1) * g
        v = ADAM_B2 * v_ref[...] + (1.0 - ADAM_B2) * (g * g)
        m_hat = m / (1.0 - ADAM_B1 ** ADAM_STEP)
        v_hat = v / (1.0 - ADAM_B2 ** ADAM_STEP)
        d_ref[...] = -ADAM_LR * (m_hat / (jnp.sqrt(v_hat) + ADAM_EPS) + ADAM_WD * w_ref[...])
        nm_ref[...] = m
        nv_ref[...] = v

    spec = pl.BlockSpec((1, tr, c), lambda i, j: (i, j, 0))
    outs = pl.pallas_call(
        body, name=name, grid=(b, r // tr), in_specs=[spec] * 4, out_specs=[spec] * 3,
        out_shape=[jax.ShapeDtypeStruct((b, r, c), F32)] * 3,
        compiler_params=_params(("parallel", "parallel")),
    )(*views)
    return [o.reshape(shape) for o in outs]


def adamw_halves(w, g_mine, g_theirs, m, v, name):
    nl, r, c = w.shape
    r2 = r // 2
    tr = _tile(r2, 256, 16)
    nb = r2 // tr

    def body(w_ref, gm_ref, gt_ref, m_ref, v_ref, g_ref, d_ref, nm_ref, nv_ref):
        mine = (pl.program_id(1) // nb) == lax.axis_index("c")
        g = jnp.where(mine, gm_ref[...], gt_ref[...])
        m = ADAM_B1 * m_ref[...] + (1.0 - ADAM_B1) * g
        v = ADAM_B2 * v_ref[...] + (1.0 - ADAM_B2) * (g * g)
        m_hat = m / (1.0 - ADAM_B1 ** ADAM_STEP)
        v_hat = v / (1.0 - ADAM_B2 ** ADAM_STEP)
        g_ref[...] = g
        d_ref[...] = -ADAM_LR * (m_hat / (jnp.sqrt(v_hat) + ADAM_EPS) + ADAM_WD * w_ref[...])
        nm_ref[...] = m
        nv_ref[...] = v

    whole = pl.BlockSpec((1, tr, c), lambda l, i: (l, i, 0))
    half = pl.BlockSpec((1, tr, c), lambda l, i: (l, i % nb, 0))
    return pl.pallas_call(
        body, name=name, grid=(nl, 2 * nb), in_specs=[whole, half, half, whole, whole], out_specs=[whole] * 4,
        out_shape=[jax.ShapeDtypeStruct((nl, r, c), F32)] * 4,
        compiler_params=_params(("parallel", "parallel")),
    )(w, g_mine, g_theirs, m, v)


def _coords():
    return lax.axis_index("x"), lax.axis_index("y"), lax.axis_index("c")


def _ici_peers(x, y):
    chips = [(1 - x, y), (x, 1 - y), (1 - x, 1 - y)]
    return chips, [2 * cx + cy for cx, cy in chips]


def _place(ref, how, chip, layers, per):
    if how == "lead":
        return ref.at[chip, layers]
    start = pl.multiple_of(chip * per, per)
    if how == "rows":
        return ref.at[layers, pl.ds(start, per), :]
    return ref.at[layers, :, pl.ds(start, per)]


def gather_weights(shards, hows, name):
    na = len(shards)
    out_shape = []
    for s, how in zip(shards, hows):
        assert s.shape[0] % 2 == 0
        if how == "lead":
            shp = (N_CHIPS, *s.shape)
        elif how == "rows":
            shp = (s.shape[0], N_CHIPS * s.shape[1], s.shape[2])
        else:
            shp = (s.shape[0], s.shape[1], N_CHIPS * s.shape[2])
        out_shape.append(jax.ShapeDtypeStruct(shp, s.dtype))

    def body(*refs):
        ins = refs[:na]
        outs = refs[na:2 * na]
        send_sems, recv_sems = refs[2 * na:]
        x, y, c = _coords()
        me = 2 * x + y
        chips, chip_ids = _ici_peers(x, y)
        sibling = (x, y, 1 - c)

        def dst(a, chip, layers):
            per = {"lead": 0, "rows": ins[a].shape[1], "cols": ins[a].shape[-1]}[hows[a]]
            return _place(outs[a], hows[a], chip, layers, per)

        def copy(a, k, src, dst_ref, to):
            return pltpu.make_async_remote_copy(
                src_ref=src, dst_ref=dst_ref, send_sem=send_sems.at[7 * a + k], recv_sem=recv_sems.at[7 * a + k],
                device_id=to, device_id_type=MESH)

        started = []
        halves = []
        for a in range(na):
            nl = ins[a].shape[0]
            hl = nl // 2
            mine = pl.ds(c * hl, hl)
            theirs = pl.ds((1 - c) * hl, hl)
            halves.append((mine, theirs))
            for k in range(3):
                cp = copy(a, k, ins[a].at[mine], dst(a, me, mine), (*chips[k], c))
                cp.start()
                started.append(cp)
            own = copy(a, 6, ins[a], dst(a, me, pl.ds(0, nl)), sibling)
            own.start()
            started.append(own)
        for a in range(na):
            mine, _ = halves[a]
            for k in range(3):
                landed = dst(a, chip_ids[k], mine)
                copy(a, k, landed, landed, (*chips[k], c)).wait_recv()
                fw = copy(a, 3 + k, landed, landed, sibling)
                fw.start()
                started.append(fw)
        for a in range(na):
            _, theirs = halves[a]
            for k in range(3):
                got = dst(a, chip_ids[k], theirs)
                copy(a, 3 + k, got, got, sibling).wait_recv()
            whole = dst(a, me, pl.ds(0, ins[a].shape[0]))
            copy(a, 6, whole, whole, sibling).wait_recv()
        for cp in started:
            cp.wait_send()

    return pl.pallas_call(
        body, name=name, in_specs=_any_specs(na), out_specs=_any_specs(na), out_shape=out_shape,
        scratch_shapes=[pltpu.SemaphoreType.DMA((7 * na,)), pltpu.SemaphoreType.DMA((7 * na,))],
        compiler_params=pltpu.CompilerParams(has_side_effects=True),
    )(*shards)


def _remote(src, dst, send_sems, recv_sems, k, to):
    return pltpu.make_async_remote_copy(src_ref=src, dst_ref=dst, send_sem=send_sems.at[k], recv_sem=recv_sems.at[k],
                                        device_id=to, device_id_type=MESH)


LAYER_HOW = ("lead", "rows", "cols", "rows")


def _layer_place(ref, how, chip, shard_shape, start, size):
    r, c = shard_shape
    if how == "lead":
        return ref.at[chip, :, pl.ds(start, size), :]
    if how == "rows":
        return ref.at[:, pl.ds(pl.multiple_of(chip * r + start, HALO), size), :]
    return ref.at[:, pl.ds(start, size), pl.ds(pl.multiple_of(chip * c, LANES), c)]


def weight_rider_ici(shards, hows, layer):
    shapes = [tuple(s.shape[1:]) for s in shards]
    out_shapes = []
    for (r, c), how, s in zip(shapes, hows, shards):
        shp = {"lead": (N_CHIPS, 1, r, c), "rows": (1, N_CHIPS * r, c), "cols": (1, r, N_CHIPS * c)}[how]
        out_shapes.append(jax.ShapeDtypeStruct(shp, s.dtype))

    def copies(ins, outs, send_sems, recv_sems):
        x, y, c = _coords()
        me = 2 * x + y
        chips, chip_ids = _ici_peers(x, y)
        sibling = (x, y, 1 - c)
        pairs = []
        for a, (shape, how) in enumerate(zip(shapes, hows)):
            half = shape[0] // 2
            mine = pl.multiple_of(c * half, HALO)
            src = ins[a].at[pl.ds(layer, 1)]
            for k in range(3):
                to = (*chips[k], c)
                land = _layer_place(outs[a], how, chip_ids[k], shape, mine, half)
                pairs.append((_remote(src.at[:, pl.ds(mine, half), :], _layer_place(outs[a], how, me, shape, mine, half),
                                      send_sems, recv_sems, 4 * a + k, to),
                              _remote(land, land, send_sems, recv_sems, 4 * a + k, to)))
            whole = _layer_place(outs[a], how, me, shape, 0, shape[0])
            pairs.append((_remote(src, whole, send_sems, recv_sems, 4 * a + 3, sibling),
                          _remote(whole, whole, send_sems, recv_sems, 4 * a + 3, sibling)))
        return pairs

    return Rider(list(shards), out_shapes, {}, 4 * len(shards), copies)


def weight_rider_d2d(bufs, shapes, hows):
    def copies(ins, outs, send_sems, recv_sems):
        x, y, c = _coords()
        _, chip_ids = _ici_peers(x, y)
        sibling = (x, y, 1 - c)
        pairs = []
        for a, (shape, how) in enumerate(zip(shapes, hows)):
            half = shape[0] // 2
            mine = pl.multiple_of(c * half, HALO)
            theirs = pl.multiple_of((1 - c) * half, HALO)
            for k in range(3):
                land = _layer_place(outs[a], how, chip_ids[k], shape, theirs, half)
                pairs.append((_remote(_layer_place(ins[a], how, chip_ids[k], shape, mine, half),
                                      _layer_place(outs[a], how, chip_ids[k], shape, mine, half),
                                      send_sems, recv_sems, 3 * a + k, sibling),
                              _remote(land, land, send_sems, recv_sems, 3 * a + k, sibling)))
        return pairs

    return Rider(list(bufs), [jax.ShapeDtypeStruct(b.shape, b.dtype) for b in bufs],
                 {a: a for a in range(len(bufs))}, 3 * len(bufs), copies)


def grads_rider_sibling(arrs):
    def copies(ins, outs, send_sems, recv_sems):
        x, y, c = _coords()
        sibling = (x, y, 1 - c)
        pairs = []
        for a in range(len(arrs)):
            r2 = ins[a].shape[1] // 2
            src = ins[a].at[:, pl.ds(pl.multiple_of((1 - c) * r2, 8), r2), :]
            pairs.append((_remote(src, outs[a], send_sems, recv_sems, a, sibling),
                          _remote(outs[a], outs[a], send_sems, recv_sems, a, sibling)))
        return pairs

    return Rider(list(arrs), [jax.ShapeDtypeStruct((a.shape[0], a.shape[1] // 2, a.shape[2]), a.dtype) for a in arrs],
                 {}, len(arrs), copies)


def chip_sum(g, recv, name):
    nch, r, c = g.shape
    r2 = r // 2
    tr = _tile(r2, 256, 16)
    nb = r2 // tr

    def body(g0_ref, g1_ref, r_ref, o32_ref, o16_ref):
        s = jnp.where(lax.axis_index("c") == 0, g0_ref[...], g1_ref[...]) + r_ref[...]
        o32_ref[...] = s
        o16_ref[...] = s.astype(BF16)

    here = pl.BlockSpec((1, tr, c), lambda i, j: (i, j, 0))
    return pl.pallas_call(
        body, name=name, grid=(nch, nb),
        in_specs=[here, pl.BlockSpec((1, tr, c), lambda i, j: (i, nb + j, 0)), here],
        out_specs=[here, here],
        out_shape=[jax.ShapeDtypeStruct((nch, r2, c), F32), jax.ShapeDtypeStruct((nch, r2, c), BF16)],
        compiler_params=_params(("parallel", "parallel")),
    )(g, g, recv)


def grads_rider_chips(arrs):
    def copies(ins, outs, send_sems, recv_sems):
        x, y, c = _coords()
        chips, chip_ids = _ici_peers(x, y)
        pairs = []
        for a in range(len(arrs)):
            for k in range(3):
                to = (*chips[k], c)
                pairs.append((_remote(ins[a].at[chip_ids[k]], outs[a].at[k], send_sems, recv_sems, 3 * a + k, to),
                              _remote(outs[a].at[k], outs[a].at[k], send_sems, recv_sems, 3 * a + k, to)))
        return pairs

    return Rider(list(arrs), [jax.ShapeDtypeStruct((3, *a.shape[1:]), a.dtype) for a in arrs], {}, 3 * len(arrs),
                 copies)


def grad_sum(p32, recv, layer, nl, buf, name):
    _, r2, c = p32.shape
    tr = _tile(r2, 256, 16)
    nb = r2 // tr

    def body(p0_ref, p1_ref, p2_ref, p3_ref, r0_ref, r1_ref, r2_ref, *rest):
        o_ref = rest[-1]
        x, y, _ = _coords()
        chip = 2 * x + y
        own = jnp.where(chip == 0, p0_ref[...], jnp.where(chip == 1, p1_ref[...],
                                                        jnp.where(chip == 2, p2_ref[...], p3_ref[...])))
        o_ref[...] = own + r0_ref[...].astype(F32) + r1_ref[...].astype(F32) + r2_ref[...].astype(F32)

    slot = lambda k: pl.BlockSpec((1, tr, c), lambda j: (k, j, 0))
    in_specs = [slot(k) for k in range(N_CHIPS)] + [slot(k) for k in range(3)]
    args = [p32] * N_CHIPS + [recv] * 3
    aliases = {}
    if buf is not None:
        in_specs.append(pl.BlockSpec(memory_space=pl.ANY))
        args.append(buf)
        aliases = {len(args) - 1: 0}
    return pl.pallas_call(
        body, name=name, grid=(nb,), in_specs=in_specs,
        out_specs=pl.BlockSpec((1, tr, c), lambda j: (layer, j, 0)),
        out_shape=jax.ShapeDtypeStruct((nl, r2, c), F32),
        input_output_aliases=aliases,
        compiler_params=_params(("parallel",)),
    )(*args)


def grads_rider_exchange(bufs):
    def copies(ins, outs, send_sems, recv_sems):
        x, y, c = _coords()
        sibling = (x, y, 1 - c)
        return [(_remote(ins[a], outs[a], send_sems, recv_sems, a, sibling),
                 _remote(outs[a], outs[a], send_sems, recv_sems, a, sibling)) for a in range(len(bufs))]

    return Rider(list(bufs), [jax.ShapeDtypeStruct(b.shape, b.dtype) for b in bufs], {}, len(bufs), copies)


def allreduce_small(buf, name):
    r, cdim = buf.shape

    def body(x_ref, o_ref, gath, send_sems, recv_sems):
        x, y, c = _coords()
        me, sibling = (x, y, c), (x, y, 1 - c)
        chips, _ = _ici_peers(x, y)

        def slot(px, py, pc):
            return gath.at[4 * px + 2 * py + pc]

        def copy(k, block, to, src=None):
            return pltpu.make_async_remote_copy(
                src_ref=slot(*block) if src is None else src, dst_ref=slot(*block),
                send_sem=send_sems.at[k], recv_sem=recv_sems.at[k], device_id=to, device_id_type=MESH)

        gath[4 * x + 2 * y + c] = x_ref[...]
        first = [copy(0, me, sibling, src=x_ref)]
        first += [copy(1 + j, me, (*chip, c), src=x_ref) for j, chip in enumerate(chips)]
        for cp in first:
            cp.start()
        passed = [copy(4 + j, (*chip, c), sibling) for j, chip in enumerate(chips)]
        for j, chip in enumerate(chips):
            copy(1 + j, (*chip, c), me).wait_recv()
            passed[j].start()
        copy(0, sibling, me).wait_recv()
        for j, chip in enumerate(chips):
            copy(4 + j, (*chip, 1 - c), me).wait_recv()
        for cp in first + passed:
            cp.wait_send()
        acc = gath[0]
        for d in range(1, 8):
            acc = acc + gath[d]
        o_ref[...] = acc

    return pl.pallas_call(
        body, name=name,
        in_specs=[pl.BlockSpec(memory_space=pltpu.VMEM)], out_specs=pl.BlockSpec(memory_space=pltpu.VMEM),
        out_shape=jax.ShapeDtypeStruct((r, cdim), F32),
        scratch_shapes=[pltpu.VMEM((8, r, cdim), F32), pltpu.SemaphoreType.DMA((7,)), pltpu.SemaphoreType.DMA((7,))],
        compiler_params=pltpu.CompilerParams(has_side_effects=True),
    )(buf)


def _expand_heads(v):
    return jnp.repeat(v.astype(F32), HEAD_DIM).reshape(1, -1)


def _pad_lanes(v):
    return jnp.pad(v.astype(F32), (0, LANES - v.shape[0])).reshape(1, LANES)


def local_step(x, tgt, p, comm, cols):
    nl = p["norm_mix_w"].shape[0]
    d = x.shape[1]
    cw = p["short_conv_w"].shape[2]
    si = p["ssd_norm_w"].shape[1]
    ff, npad = comm.ff, comm.npad
    nh = si // HEAD_DIM
    gn = GROUPS * STATE
    dt_off = 3 * cw + si + si + 2 * gn
    assert cols == dt_off + nh and nh <= LANES and dt_off % LANES == 0 and npad == dt_off + LANES
    pieces = [(0, cw), (cw, cw), (2 * cw, cw), (3 * cw, si), (3 * cw + si, si), (3 * cw + 2 * si, 2 * gn),
              (dt_off, LANES)]

    saved = []
    for l in range(nl):
        nw1 = p["norm_mix_w"][l].reshape(1, d)
        nw2 = p["norm_mlp_w"][l].reshape(1, d)
        kw3 = p["short_conv_w"][l]
        kwx, kwb = p["ssd_conv_w"][l][:, :si], p["ssd_conv_w"][l][:, si:]
        bx, bb = p["ssd_conv_b"][l][:si].reshape(1, si), p["ssd_conv_b"][l][si:].reshape(1, 2 * gn)
        dtb = _pad_lanes(p["dt_bias"][l])
        aneg = _pad_lanes(-jnp.exp(p["a_log"][l]))
        dsk = _expand_heads(p["d_skip"][l])
        snw = p["ssd_norm_w"][l].reshape(1, si)
        ssd_args = (kwx, kwb, bx, bb, dtb, aneg, dsk, snw)

        w_in = comm.weight(l, "w_in")
        (proj, h, dt_raw), sent = norm_matmul(x, nw1, w_in, 0, dt_off, BF16, "in_proj", tail_block=dt_off // LANES,
                                              rider=comm.rider("in_proj", l))
        comm.done("in_proj", l, sent)
        y_mix = conv_mixer_fwd(proj, kw3, cw, cw + si, "conv_mixer_fwd")
        (y_mix, *ssd_saved), sent = ssd_fwd(proj, dt_raw, y_mix, *ssd_args, cw, si, "ssd_fwd",
                                            rider=comm.rider("ssd_fwd", l))
        comm.done("ssd_fwd", l, sent)
        x2, _ = matmul(y_mix, comm.weight(l, "w_out"), 0, False, d, F32, "out_proj", residual=x)
        (up, h2), sent = norm_matmul(x2, nw2, comm.weight(l, "w_up"), 0, ff, BF16, "up_proj",
                                     rider=comm.rider("up_proj", l))
        comm.done("up_proj", l, sent)
        x3, sent = matmul(up, comm.weight(l, "w_down"), 0, False, d, F32, "down_proj", lhs_fn=_relu2, residual=x2,
                          rider=comm.rider("down_proj", l))
        comm.done("down_proj", l, sent)
        saved.append((x, h, proj, dt_raw, y_mix, ssd_saved, x2, h2, up, nw1, nw2, kw3, ssd_args))
        x = x3

    dx, dwf, lvec = final_loss(x, p["final_norm_w"].reshape(1, d), tgt, "final_loss")
    loss = jnp.sum(lvec)

    names = ("norm_mix_w", "short_conv_w", "ssd_conv_w", "ssd_conv_b", "dt_bias", "a_log", "d_skip",
             "ssd_norm_w", "norm_mlp_w")
    grads = {k: [None] * nl for k in names}
    for l in reversed(range(nl)):
        x0, h, proj, dt_raw, y_mix, ssd_saved, x2, h2, up, nw1, nw2, kw3, ssd_args = saved[l]
        kwx, kwb, _, _, dtb, aneg, dsk, snw = ssd_args
        dup, sent = matmul(dx, comm.weight(l, "w_down"), 0, True, ff, BF16, "down_bwd", relu_gate=up,
                           rider=comm.rider("down_bwd", l))
        comm.done("down_bwd", l, sent)
        g_down, sums = matmul_tn(up, dx, "down_wgrad", a_fn=_relu2, chip_sums=comm.side("down_wgrad", l))
        comm.side_done("down_wgrad", l, sums)
        comm.take_gradient(l, "w_down", g_down)
        (dx2, dnw2), _ = matmul_normbwd([(dup, 0)], [(0, ff)], comm.weight(l, "w_up"), 0, x2, nw2, dx, "up_bwd")
        comm.take_gradient(l, "w_up", matmul_tn(h2, dup, "up_wgrad", by_chip=True))
        grads["norm_mlp_w"][l] = dnw2.reshape(d)
        dy, sent = matmul(dx2, comm.weight(l, "w_out"), 0, True, cw + si, BF16, "out_bwd",
                          rider=comm.rider("out_bwd", l))
        comm.done("out_bwd", l, sent)
        g_out, sums = matmul_tn(y_mix, dx2, "out_wgrad", chip_sums=comm.side("out_wgrad", l))
        comm.side_done("out_wgrad", l, sums)
        comm.take_gradient(l, "w_out", g_out)
        du, dkw3 = conv_mixer_bwd(proj, dy, kw3, cw, "conv_mixer_bwd")
        (dssd, ddt, dkwx, dkwb, dbx, dbb, ddtb, da, ddsk, dsnw), sent = ssd_bwd(
            proj, dt_raw, *ssd_saved, dy, kwx, kwb, dtb, aneg, dsk, snw, cw, si, "ssd_bwd",
            rider=comm.rider("ssd_bwd", l))
        comm.done("ssd_bwd", l, sent)
        views = [(du, 0), (du, 1), (du, 2), (dssd, 0), (dssd, 1), (dssd, 2 * si // (2 * gn)), (ddt, 0)]
        (dxl, dnw1), sent = matmul_normbwd(views, pieces, comm.weight(l, "w_in"), 0, x0, nw1, dx2, "in_bwd",
                                           rider=comm.rider("in_bwd", l))
        comm.done("in_bwd", l, sent)
        comm.take_gradient(l, "w_in", split_to_chips(
            [matmul_tn(h, dp, "in_wgrad_%d" % i) for i, dp in enumerate((du, dssd, ddt))], cols, "in_wgrad_split"))
        grads["norm_mix_w"][l] = dnw1.reshape(d)
        grads["short_conv_w"][l] = dkw3
        grads["ssd_conv_w"][l] = jnp.concatenate([dkwx, dkwb], axis=1)
        grads["ssd_conv_b"][l] = jnp.concatenate([dbx, dbb], axis=1).reshape(-1)
        grads["dt_bias"][l] = ddtb[0, :nh]
        grads["a_log"][l] = da[0, :nh] * aneg[0, :nh]
        grads["d_skip"][l] = jnp.sum(ddsk.reshape(nh, HEAD_DIM), axis=1)
        grads["ssd_norm_w"][l] = dsnw.reshape(si)
        dx = dxl

    grads = {k: jnp.stack(v) for k, v in grads.items()}
    grads["final_norm_w"] = dwf.reshape(d)
    return loss, dx, grads


BIG = ("w_in", "w_out", "w_up", "w_down")
SMALL_SHARDED = ("short_conv_w", "ssd_conv_w")
SMALL_REPL = ("norm_mix_w", "ssd_conv_b", "dt_bias", "a_log", "d_skip", "ssd_norm_w", "norm_mlp_w", "final_norm_w")
WEIGHTS = ("norm_mix_w", "w_in", "short_conv_w", "ssd_conv_w", "ssd_conv_b", "dt_bias", "a_log", "d_skip",
           "ssd_norm_w", "w_out", "norm_mlp_w", "w_up", "w_down", "final_norm_w")
SMALL_COLS = 1024


def _pack_small(named):
    flat = jnp.concatenate([v.reshape(-1).astype(F32) for v in named])
    n = flat.shape[0]
    rows = -(-n // SMALL_COLS)
    rows = -(-rows // 8) * 8
    return jnp.pad(flat, (0, rows * SMALL_COLS - n)).reshape(rows, SMALL_COLS)


def _unpack_small(buf, like):
    flat = buf.reshape(-1)
    out, off = [], 0
    for v in like:
        out.append(flat[off:off + v.size].reshape(v.shape))
        off += v.size
    return out


class ChipComm:
    IO = ("w_in", "w_out")
    MLP = ("w_up", "w_down")

    def __init__(self, shards, nl, npad):
        self.shards, self.nl, self.npad = shards, nl, npad
        self.how = dict(zip(BIG, LAYER_HOW))
        self.ff = N_CHIPS * shards["w_up"].shape[2]
        self.w = {}
        self.landed = {}
        self.grad = {}
        self.sums = {}
        self.from_sibling = {}
        self.bufs = {k: None for k in BIG}
        first = run_rider(self._ici(self.IO, 0), "gather_first_ici")
        self._gathered(self.IO, 0, run_rider(self._d2d(self.IO, first), "gather_first_d2d"))

    def _ici(self, group, l):
        return weight_rider_ici([self.shards[k] for k in group], [self.how[k] for k in group], l)

    def _d2d(self, group, landed):
        return weight_rider_d2d(landed, [tuple(self.shards[k].shape[1:]) for k in group],
                                [self.how[k] for k in group])

    def _gathered(self, group, l, arrays):
        for k, g in zip(group, arrays):
            self.w[(l, k)] = join_from_chips(g, self.npad, "w_in_join") if k == "w_in" else g

    def _to_sibling(self, group, l):
        return grads_rider_sibling([self.grad[(l, k)] for k in group])

    def _summed(self, group, l, from_sibling):
        self.sums[group] = (l, [chip_sum(self.grad.pop((l, k)), r, "chip_sum") for k, r in zip(group, from_sibling)])

    def side(self, point, l):
        group = self.IO if point == "down_wgrad" else self.MLP
        if group not in self.from_sibling:
            return ()
        layer, parts = self.from_sibling[group]
        return [(self.grad[(layer, k)], r) for k, r in zip(group, parts)]

    def side_done(self, point, l, sums):
        group = self.IO if point == "down_wgrad" else self.MLP
        if sums:
            layer, _ = self.from_sibling.pop(group)
            for k in group:
                del self.grad[(layer, k)]
            self.sums[group] = (layer, sums)

    def _to_chips(self, group):
        return grads_rider_chips([s[1] for s in self.sums[group][1]])

    def _reduced(self, group, from_chips):
        l, sums = self.sums.pop(group)
        for k, s, r in zip(group, sums, from_chips):
            self.bufs[k] = grad_sum(s[0], r, l, self.nl, self.bufs[k], "grad_sum")

    def weight(self, l, name):
        return self.w[(l, name)]

    def take_gradient(self, l, name, g):
        self.grad[(l, name)] = g if g.ndim == 3 else g.reshape(N_CHIPS, g.shape[0] // N_CHIPS, g.shape[1])

    def rider(self, point, l):
        more = l + 1 < self.nl
        if point == "in_proj":
            return self._ici(self.MLP, l)
        if point == "ssd_fwd":
            return self._d2d(self.MLP, self.landed[self.MLP])
        if point == "up_proj":
            return self._ici(self.IO, l + 1) if more else None
        if point == "down_proj":
            return self._d2d(self.IO, self.landed[self.IO]) if more else None
        if point == "down_bwd":
            return self._to_sibling(self.IO, l + 1) if more else None
        if point == "out_bwd":
            return self._to_sibling(self.MLP, l)
        if point == "ssd_bwd":
            return self._to_chips(self.IO) if more else None
        return self._to_chips(self.MLP)

    def done(self, point, l, results):
        if not results:
            return
        if point in ("in_proj", "up_proj"):
            self.landed[self.MLP if point == "in_proj" else self.IO] = results
        elif point == "ssd_fwd":
            self._gathered(self.MLP, l, results)
        elif point == "down_proj":
            self._gathered(self.IO, l + 1, results)
        elif point == "down_bwd":
            self.from_sibling[self.IO] = (l + 1, results)
        elif point == "out_bwd":
            self.from_sibling[self.MLP] = (l, results)
        elif point == "ssd_bwd":
            self._reduced(self.IO, results)
        else:
            self._reduced(self.MLP, results)

    def finish(self):
        self._summed(self.IO, 0, run_rider(self._to_sibling(self.IO, 0), "grads_to_sibling"))
        self._reduced(self.IO, run_rider(self._to_chips(self.IO), "grads_to_chips"))
        mine = [self.bufs[k] for k in BIG]
        theirs = run_rider(grads_rider_exchange(mine), "grads_exchange")
        return dict(zip(BIG, zip(mine, theirs)))


def kernel(x, norm_mix_w, w_in, short_conv_w, ssd_conv_w, ssd_conv_b, dt_bias, a_log, d_skip, ssd_norm_w, w_out, norm_mlp_w, w_up, w_down, final_norm_w, loss_target, m_norm_mix_w, m_w_in, m_short_conv_w, m_ssd_conv_w, m_ssd_conv_b, m_dt_bias, m_a_log, m_d_skip, m_ssd_norm_w, m_w_out, m_norm_mlp_w, m_w_up, m_w_down, m_final_norm_w, v_norm_mix_w, v_w_in, v_short_conv_w, v_ssd_conv_w, v_ssd_conv_b, v_dt_bias, v_a_log, v_d_skip, v_ssd_norm_w, v_w_out, v_norm_mlp_w, v_w_up, v_w_down, v_final_norm_w):
    w = dict(norm_mix_w=norm_mix_w, w_in=w_in, short_conv_w=short_conv_w, ssd_conv_w=ssd_conv_w,
             ssd_conv_b=ssd_conv_b, dt_bias=dt_bias, a_log=a_log, d_skip=d_skip, ssd_norm_w=ssd_norm_w, w_out=w_out,
             norm_mlp_w=norm_mlp_w, w_up=w_up, w_down=w_down, final_norm_w=final_norm_w)
    m = dict(norm_mix_w=m_norm_mix_w, w_in=m_w_in, short_conv_w=m_short_conv_w, ssd_conv_w=m_ssd_conv_w,
             ssd_conv_b=m_ssd_conv_b, dt_bias=m_dt_bias, a_log=m_a_log, d_skip=m_d_skip, ssd_norm_w=m_ssd_norm_w,
             w_out=m_w_out, norm_mlp_w=m_norm_mlp_w, w_up=m_w_up, w_down=m_w_down, final_norm_w=m_final_norm_w)
    v = dict(norm_mix_w=v_norm_mix_w, w_in=v_w_in, short_conv_w=v_short_conv_w, ssd_conv_w=v_ssd_conv_w,
             ssd_conv_b=v_ssd_conv_b, dt_bias=v_dt_bias, a_log=v_a_log, d_skip=v_d_skip, ssd_norm_w=v_ssd_norm_w,
             w_out=v_w_out, norm_mlp_w=v_norm_mlp_w, w_up=v_w_up, w_down=v_w_down, final_norm_w=v_final_norm_w)
    xi, yi, ci = lax.axis_index("x"), lax.axis_index("y"), lax.axis_index("c")
    chip = 2 * xi + yi
    nl = w_up.shape[0]
    cols = N_CHIPS * w_in.shape[2]
    npad = cols // LANES * LANES + LANES

    full = dict(w)
    small_gathered = gather_weights([w[k] for k in SMALL_SHARDED], ["lead"] * len(SMALL_SHARDED), "gather_small")
    for k, g4 in zip(SMALL_SHARDED, small_gathered):
        full[k] = jnp.concatenate([g4[j] for j in range(N_CHIPS)], axis=2)
    comm = ChipComm({k: w[k].astype(BF16) for k in BIG}, nl, npad)

    loss, grad_x, grads = local_step(x[0], loss_target[0], full, comm, cols)
    loss = lax.psum(loss, ("x", "y", "c"))
    halves = comm.finish()
    g_shard = {}

    small_names = SMALL_REPL + SMALL_SHARDED
    small_sum = allreduce_small(_pack_small([grads[k] for k in small_names]), "allreduce_small")
    for k, g in zip(small_names, _unpack_small(small_sum, [grads[k] for k in small_names])):
        if k in SMALL_SHARDED:
            width = w[k].shape[2]
            g = lax.dynamic_slice_in_dim(g, chip * width, width, axis=2)
        g_shard[k] = g

    delta, new_m, new_v = {}, {}, {}
    for k in BIG:
        g_shard[k], delta[k], new_m[k], new_v[k] = adamw_halves(w[k], *halves[k], m[k], v[k], "adamw_%s" % k)
    packed = [_pack_small([d_[k] for k in small_names]) for d_ in (w, g_shard, m, v)]
    outs = adamw(*packed, "adamw_small")
    for d_, buf in zip((delta, new_m, new_v), outs):
        for k, val in zip(small_names, _unpack_small(buf, [w[k] for k in small_names])):
            d_[k] = val

    return (loss, grad_x[None], *[g_shard[k] for k in WEIGHTS], *[delta[k] for k in WEIGHTS],
            *[new_m[k] for k in WEIGHTS], *[new_v[k] for k in WEIGHTS])
```

```python
import functools

import jax
import jax.numpy as jnp
from jax import lax
from jax.experimental import pallas as pl
from jax.experimental.pallas import tpu as pltpu

F32 = jnp.float32
BF16 = jnp.bfloat16

EPS = 1e-5
HEAD_DIM = 64
STATE = 128
GROUPS = 2
SHORT_K = 3
SSD_K = 4
LANES = 128
PAIR = LANES // HEAD_DIM
SCAN_CHUNK = 256
HALO = 16
N_CHIPS = 4
VMEM_LIMIT = 56 * 1024 * 1024

ADAM_LR = 0.001
ADAM_B1 = 0.9
ADAM_B2 = 0.999
ADAM_EPS = 1e-08
ADAM_WD = 0.01
ADAM_STEP = 10

MESH = pl.DeviceIdType.MESH


def _params(sem):
    return pltpu.CompilerParams(dimension_semantics=sem, vmem_limit_bytes=VMEM_LIMIT)


def _tile(n, cap, quantum):
    if n <= cap:
        return n
    best = None
    for t in range(quantum, cap + 1, quantum):
        if n % t == 0:
            best = t
    assert best is not None, (n, cap, quantum)
    return best


def _dot(a, b):
    return jnp.dot(a, b, preferred_element_type=F32)


def _dot_nt(a, b):
    return lax.dot_general(a, b, (((1,), (1,)), ((), ())), preferred_element_type=F32)


def _dot_tn(a, b):
    return lax.dot_general(a, b, (((0,), (0,)), ((), ())), preferred_element_type=F32)


def _dot_exact(a, b):
    return jnp.dot(a, b, precision=lax.Precision.HIGHEST, preferred_element_type=F32)


def _sigmoid(x):
    return pl.reciprocal(1.0 + jnp.exp(-x), approx=True)


def _softplus(x):
    return jnp.maximum(x, 0.0) + jnp.log(1.0 + jnp.exp(-jnp.abs(x)))


def _relu2(v):
    return jnp.square(jnp.maximum(v, 0.0))


class Rider:
    def __init__(self, ins, out_shapes, aliases, n_sems, copies):
        self.ins, self.out_shapes, self.aliases, self.n_sems, self.copies = ins, out_shapes, aliases, n_sems, copies


def _any_specs(n):
    return [pl.BlockSpec(memory_space=pl.ANY)] * n


def _ride(body, grid, in_specs, out_specs, out_shape, scratch, args, aliases, rider, sem, name):
    n_in, n_out, n_scr = len(in_specs), len(out_specs), len(scratch)
    if rider is None:
        outs = pl.pallas_call(
            body, name=name, grid=grid, in_specs=in_specs, out_specs=out_specs, out_shape=out_shape,
            scratch_shapes=scratch, input_output_aliases=aliases, compiler_params=_params(sem))(*args)
        return list(outs), []
    ri, ro = len(rider.ins), len(rider.out_shapes)
    last = tuple(g - 1 for g in grid)

    def wrapped(*refs):
        ins = refs[:n_in]
        r_ins = refs[n_in:n_in + ri]
        outs = refs[n_in + ri:n_in + ri + n_out]
        r_outs = refs[n_in + ri + n_out:n_in + ri + n_out + ro]
        scr = refs[n_in + ri + n_out + ro:n_in + ri + n_out + ro + n_scr]
        send_sems, recv_sems = refs[-2:]
        ids = [pl.program_id(a) for a in range(len(grid))]
        at_first = functools.reduce(jnp.logical_and, [i == 0 for i in ids])
        at_last = functools.reduce(jnp.logical_and, [i == e for i, e in zip(ids, last)])

        @pl.when(at_first)
        def _():
            for cp, _ in rider.copies(r_ins, r_outs, send_sems, recv_sems):
                cp.start()

        body(*ins, *outs, *scr)

        @pl.when(at_last)
        def _():
            for cp, landed in rider.copies(r_ins, r_outs, send_sems, recv_sems):
                cp.wait_send()
                landed.wait_recv()

    all_aliases = dict(aliases)
    all_aliases.update({n_in + a: n_out + b for a, b in rider.aliases.items()})
    outs = pl.pallas_call(
        wrapped, name=name, grid=grid, in_specs=list(in_specs) + _any_specs(ri),
        out_specs=list(out_specs) + _any_specs(ro), out_shape=list(out_shape) + list(rider.out_shapes),
        scratch_shapes=list(scratch) + [pltpu.SemaphoreType.DMA((rider.n_sems,)),
                                        pltpu.SemaphoreType.DMA((rider.n_sems,))],
        input_output_aliases=all_aliases, compiler_params=_params(sem))(*args, *rider.ins)
    return list(outs[:n_out]), list(outs[n_out:])


def run_rider(rider, name):
    ri, ro = len(rider.ins), len(rider.out_shapes)

    def body(*refs):
        send_sems, recv_sems = refs[-2:]
        pairs = rider.copies(refs[:ri], refs[ri:ri + ro], send_sems, recv_sems)
        for cp, _ in pairs:
            cp.start()
        for cp, landed in pairs:
            cp.wait_send()
            landed.wait_recv()

    return list(pl.pallas_call(
        body, name=name, in_specs=_any_specs(ri), out_specs=_any_specs(ro), out_shape=list(rider.out_shapes),
        scratch_shapes=[pltpu.SemaphoreType.DMA((rider.n_sems,)), pltpu.SemaphoreType.DMA((rider.n_sems,))],
        input_output_aliases=dict(rider.aliases),
        compiler_params=pltpu.CompilerParams(has_side_effects=True))(*rider.ins))


def norm_matmul(x, nw, w, layer, n, out_dtype, name, tail_block=None, rider=None):
    t, d = x.shape
    mxu_cols = 2 * LANES
    tn = _tile(n, 1536, mxu_cols if n % mxu_cols == 0 else LANES)
    if n % mxu_cols == 0 and tn < 1024 <= n:
        tn = _tile(n, 3072, mxu_cols)
    tm = _tile(t, 512 if tn > 1536 else 1024, 8)
    nj = n // tn

    def body(x_ref, nw_ref, w_ref, *rest):
        if tail_block is None:
            o_ref, h_ref = rest
        else:
            wt_ref, o_ref, h_ref, tail_ref = rest

        @pl.when(pl.program_id(1) == 0)
        def _():
            xf = x_ref[...]
            r = lax.rsqrt(jnp.mean(xf * xf, axis=-1, keepdims=True) + EPS)
            h_ref[...] = (xf * r * nw_ref[...]).astype(BF16)

        o_ref[...] = _dot(h_ref[...], w_ref[...]).astype(out_dtype)
        if tail_block is not None:
            @pl.when(pl.program_id(1) == nj - 1)
            def _():
                tail_ref[...] = _dot(h_ref[...], wt_ref[...])

    in_specs = [pl.BlockSpec((tm, d), lambda i, j: (i, 0)), pl.BlockSpec((1, d), lambda i, j: (0, 0)),
                pl.BlockSpec((None, d, tn), lambda i, j: (layer, 0, j))]
    out_specs = [pl.BlockSpec((tm, tn), lambda i, j: (i, j)), pl.BlockSpec((tm, d), lambda i, j: (i, 0))]
    out_shape = [jax.ShapeDtypeStruct((t, n), out_dtype), jax.ShapeDtypeStruct((t, d), BF16)]
    args = [x, nw, w]
    if tail_block is not None:
        in_specs.append(pl.BlockSpec((None, d, LANES), lambda i, j: (layer, 0, tail_block)))
        out_specs.append(pl.BlockSpec((tm, LANES), lambda i, j: (i, 0)))
        out_shape.append(jax.ShapeDtypeStruct((t, LANES), F32))
        args.append(w)
    return _ride(body, (t // tm, nj), in_specs, out_specs, out_shape, [], args, {}, rider,
                 ("parallel", "arbitrary"), name)


def matmul(lhs, w, layer, transposed, n, out_dtype, name, *, lhs_fn=None, residual=None, relu_gate=None,
           rider=None):
    t, k = lhs.shape
    tm = _tile(t, 512 if k > 2048 else 1024, 8)
    tn = _tile(n, 1024, LANES)
    staged = lhs.dtype != BF16 or lhs_fn is not None
    fn = lhs_fn if lhs_fn is not None else (lambda v: v)
    has_extra = residual is not None or relu_gate is not None
    dot = _dot_nt if transposed else _dot

    def body(*refs):
        a_ref, w_ref = refs[:2]
        extra = refs[2] if has_extra else None
        o_ref = refs[3] if has_extra else refs[2]
        if staged:
            s_ref = refs[-1]

            @pl.when(pl.program_id(1) == 0)
            def _():
                s_ref[...] = fn(a_ref[...].astype(F32)).astype(BF16)

            a_ref = s_ref
        acc = dot(a_ref[...], w_ref[...])
        if residual is not None:
            acc = acc + extra[...]
        if relu_gate is not None:
            acc = acc * (2.0 * jnp.maximum(extra[...].astype(F32), 0.0))
        o_ref[...] = acc.astype(out_dtype)

    if transposed:
        w_spec = pl.BlockSpec((None, tn, k), lambda i, j: (layer, j, 0))
    else:
        w_spec = pl.BlockSpec((None, k, tn), lambda i, j: (layer, 0, j))
    in_specs = [pl.BlockSpec((tm, k), lambda i, j: (i, 0)), w_spec]
    args = [lhs, w]
    if has_extra:
        in_specs.append(pl.BlockSpec((tm, tn), lambda i, j: (i, j)))
        args.append(residual if residual is not None else relu_gate)
    outs, extra = _ride(
        body, (t // tm, n // tn), in_specs, [pl.BlockSpec((tm, tn), lambda i, j: (i, j))],
        [jax.ShapeDtypeStruct((t, n), out_dtype)], [pltpu.VMEM((tm, k), BF16)] if staged else [], args, {},
        rider, ("parallel", "arbitrary"), name)
    return outs[0], extra


def matmul_normbwd(lhs, pieces, w, layer, x, nw, dres, name, rider=None):
    t, d = x.shape
    nl = len(lhs)
    tm = _tile(t, 512, 8)
    for off, width in pieces:
        assert off % width == 0

    def body(*refs):
        lrefs = refs[:nl]
        wrefs = refs[nl:2 * nl]
        x_ref, nw_ref, dres_ref, dx_ref, dnw_ref = refs[2 * nl:]
        dh = _dot_nt(lrefs[0][...].astype(BF16), wrefs[0][...])
        for a_ref, w_ref in zip(lrefs[1:], wrefs[1:]):
            dh = dh + _dot_nt(a_ref[...].astype(BF16), w_ref[...])
        xf = x_ref[...]
        r = lax.rsqrt(jnp.mean(xf * xf, axis=-1, keepdims=True) + EPS)
        nx = xf * r
        dn = dh * nw_ref[...]
        dx = r * (dn - nx * jnp.mean(dn * nx, axis=-1, keepdims=True))
        dx_ref[...] = dres_ref[...] + dx

        @pl.when(pl.program_id(0) == 0)
        def _():
            dnw_ref[...] = jnp.zeros_like(dnw_ref)

        dnw_ref[...] += jnp.sum(dh * nx, axis=0, keepdims=True)

    in_specs = [pl.BlockSpec((tm, width), (lambda blk: (lambda i: (i, blk)))(blk))
                for (_, blk), (_, width) in zip(lhs, pieces)]
    in_specs += [pl.BlockSpec((None, d, width), (lambda blk: (lambda i: (layer, 0, blk)))(off // width),
                              pipeline_mode=pl.Buffered(1))
                 for off, width in pieces]
    in_specs += [pl.BlockSpec((tm, d), lambda i: (i, 0)), pl.BlockSpec((1, d), lambda i: (0, 0)),
                 pl.BlockSpec((tm, d), lambda i: (i, 0))]
    return _ride(
        body, (t // tm,), in_specs,
        [pl.BlockSpec((tm, d), lambda i: (i, 0)), pl.BlockSpec((1, d), lambda i: (0, 0))],
        [jax.ShapeDtypeStruct((t, d), F32), jax.ShapeDtypeStruct((1, d), F32)], [],
        [*[a for a, _ in lhs], *([w] * nl), x, nw, dres], {}, rider, ("arbitrary",), name)


def matmul_tn(a, b, name, *, a_fn=None, by_chip=False, chip_sums=None, grad_sums=None):
    t, k = a.shape
    n = b.shape[1]
    tk = _tile(k, 1024, LANES)
    nn = n // N_CHIPS if by_chip else n
    tn = _tile(nn, 1536, 2 * LANES if nn % (2 * LANES) == 0 else LANES)
    tt = _tile(t, 1024, 8)
    nt = t // tt
    gn_ = n // tn
    steps = (k // tk) * gn_ * nt
    fn = a_fn if a_fn is not None else (lambda v: v)
    with_sums, chip_sums = chip_sums, list(chip_sums or ())
    for g, _ in chip_sums:
        if (g.shape[0] * g.shape[1] // 2) % (steps * HALO) or steps % g.shape[0]:
            return (matmul_tn(a, b, name, a_fn=a_fn, by_chip=by_chip),
                    [tuple(chip_sum(g_, r_, "chip_sum")) for g_, r_ in chip_sums])
    with_grads, grad_sums = grad_sums, list(grad_sums or ())
    assert not (chip_sums and grad_sums)
    for p32, *_ in grad_sums:
        if p32.shape[1] % (steps * HALO):
            return (matmul_tn(a, b, name, a_fn=a_fn, by_chip=by_chip),
                    [grad_sum(*item, "grad_sum") for item in grad_sums])
    ns = len(chip_sums)
    ng = len(grad_sums)
    n_gin = [N_CHIPS + 3 + (item[4] is not None) for item in grad_sums]
    step = lambda i, j, s: (i * gn_ + j) * nt + s

    def body(*refs):
        a_ref, b_ref = refs[:2]
        side_in = refs[2:2 + 3 * ns]
        grad_in = refs[2 + 3 * ns:2 + 3 * ns + sum(n_gin)]
        o_ref = refs[2 + 3 * ns + sum(n_gin)]
        side_out = refs[3 + 3 * ns + sum(n_gin):3 + 5 * ns + sum(n_gin)]
        grad_out = refs[3 + 5 * ns + sum(n_gin):3 + 5 * ns + sum(n_gin) + ng]
        acc_ref = refs[-1]

        @pl.when(pl.program_id(2) == 0)
        def _():
            acc_ref[...] = jnp.zeros_like(acc_ref)

        av = a_ref[...]
        if a_fn is not None:
            av = fn(av.astype(F32))
        acc_ref[...] += _dot_tn(av.astype(BF16), b_ref[...].astype(BF16))
        for q in range(ns):
            g0_ref, g1_ref, r_ref = side_in[3 * q:3 * q + 3]
            tot = jnp.where(lax.axis_index("c") == 0, g0_ref[...], g1_ref[...]) + r_ref[...]
            side_out[2 * q][...] = tot
            side_out[2 * q + 1][...] = tot.astype(BF16)
        pos = 0
        for q in range(ng):
            p = grad_in[pos:pos + N_CHIPS]
            r = grad_in[pos + N_CHIPS:pos + N_CHIPS + 3]
            pos += n_gin[q]
            x_, y_, _ = _coords()
            chip = 2 * x_ + y_
            own = jnp.where(chip == 0, p[0][...], jnp.where(chip == 1, p[1][...],
                                                           jnp.where(chip == 2, p[2][...], p[3][...])))
            grad_out[q][...] = (own + r[0][...].astype(F32) + r[1][...].astype(F32)
                                + r[2][...].astype(F32))[None]

        @pl.when(pl.program_id(2) == nt - 1)
        def _():
            o_ref[...] = acc_ref[...]

    if by_chip:
        per = n // N_CHIPS // tn
        out_specs = [pl.BlockSpec((None, tk, tn), lambda i, j, s: (j // per, i, j % per))]
        out_shape = [jax.ShapeDtypeStruct((N_CHIPS, k, n // N_CHIPS), F32)]
    else:
        out_specs = [pl.BlockSpec((tk, tn), lambda i, j, s: (i, j))]
        out_shape = [jax.ShapeDtypeStruct((k, n), F32)]
    in_specs = [pl.BlockSpec((tt, tk), lambda i, j, s: (s, i)), pl.BlockSpec((tt, tn), lambda i, j, s: (s, j))]
    args = [a, b]
    for g, recv in chip_sums:
        nch, r, c = g.shape
        r2 = r // 2
        rows = nch * r2 // steps
        per_chip = r2 // rows
        assert rows % HALO == 0 and r2 % rows == 0
        for half in range(2):
            in_specs.append(pl.BlockSpec(
                (rows, c), (lambda h: (lambda i, j, s: ((step(i, j, s) // per_chip) * 2 * per_chip + h * per_chip
                                                        + step(i, j, s) % per_chip, 0)))(half)))
        flat = pl.BlockSpec((rows, c), lambda i, j, s: (step(i, j, s), 0))
        in_specs.append(flat)
        args += [g.reshape(nch * r, c), g.reshape(nch * r, c), recv.reshape(nch * r2, c)]
        out_specs += [flat, flat]
        out_shape += [jax.ShapeDtypeStruct((nch * r2, c), F32), jax.ShapeDtypeStruct((nch * r2, c), BF16)]
    aliases = {}
    for p32, recv, layer, nl, buf in grad_sums:
        _, r2, c = p32.shape
        rows = r2 // steps
        slab = lambda kk: pl.BlockSpec((rows, c), lambda i, j, s: (kk * steps + step(i, j, s), 0))
        in_specs += [slab(kk) for kk in range(N_CHIPS)] + [slab(kk) for kk in range(3)]
        args += [p32.reshape(N_CHIPS * r2, c)] * N_CHIPS + [recv.reshape(3 * r2, c)] * 3
        if buf is not None:
            aliases[len(args)] = len(out_specs)
            in_specs.append(pl.BlockSpec(memory_space=pl.ANY))
            args.append(buf)
        out_specs.append(pl.BlockSpec((1, rows, c), (lambda ly: (lambda i, j, s: (ly, step(i, j, s), 0)))(layer)))
        out_shape.append(jax.ShapeDtypeStruct((nl, r2, c), F32))
    outs = pl.pallas_call(
        body, name=name, grid=(k // tk, gn_, nt), in_specs=in_specs, out_specs=out_specs, out_shape=out_shape,
        scratch_shapes=[pltpu.VMEM((tk, tn), F32)], input_output_aliases=aliases,
        compiler_params=_params(("parallel", "parallel", "arbitrary")),
    )(*args)
    if with_grads is not None:
        return outs[0], list(outs[1:])
    if with_sums is None:
        return outs[0]
    sums = [(outs[1 + 2 * q].reshape(g.shape[0], g.shape[1] // 2, g.shape[2]),
             outs[2 + 2 * q].reshape(g.shape[0], g.shape[1] // 2, g.shape[2])) for q, (g, _) in enumerate(chip_sums)]
    return outs[0], sums


def split_to_chips(pieces, cols, name):
    d = pieces[0].shape[0]
    widths = [p.shape[1] for p in pieces]
    w = cols // N_CHIPS
    tr = _tile(d, 256, 8)
    npc = len(pieces)

    def body(*refs):
        o_ref, row = refs[npc], refs[npc + 1]
        off = 0
        for r, n in zip(refs[:npc], widths):
            row[:, off:off + n] = r[...]
            off += n
        for j in range(N_CHIPS):
            o_ref[j] = row[:, j * w:(j + 1) * w]

    return pl.pallas_call(
        body, name=name, grid=(d // tr,),
        in_specs=[pl.BlockSpec((tr, n), lambda i: (i, 0)) for n in widths],
        out_specs=pl.BlockSpec((N_CHIPS, tr, w), lambda i: (0, i, 0)),
        out_shape=jax.ShapeDtypeStruct((N_CHIPS, d, w), F32),
        scratch_shapes=[pltpu.VMEM((tr, sum(widths)), F32)],
        compiler_params=_params(("parallel",)),
    )(*pieces)


def join_from_chips(g4, npad, name):
    _, nl, d, w = g4.shape
    tr = _tile(d, 256, HALO)

    def body(g_ref, o_ref):
        for j in range(N_CHIPS):
            o_ref[:, j * w:(j + 1) * w] = g_ref[j]
        o_ref[:, N_CHIPS * w:] = jnp.zeros((tr, npad - N_CHIPS * w), o_ref.dtype)

    return pl.pallas_call(
        body, name=name, grid=(nl, d // tr),
        in_specs=[pl.BlockSpec((N_CHIPS, None, tr, w), lambda l, i: (0, l, i, 0))],
        out_specs=pl.BlockSpec((None, tr, npad), lambda l, i: (l, i, 0)),
        out_shape=jax.ShapeDtypeStruct((nl, d, npad), g4.dtype),
        compiler_params=_params(("parallel", "parallel")),
    )(g4)


def conv_mixer_fwd(proj, kw, cw, out_cols, name):
    t = proj.shape[0]
    tm = _tile(t, 1024, HALO)
    tc = _tile(cw, 1024, LANES)
    nj = cw // tc
    hb = tm // HALO

    def body(ub_ref, uc_ref, uh_ref, ucp_ref, uhp_ref, kw_ref, y_ref):
        i = pl.program_id(0)
        taps = [kw_ref[pl.ds(k, 1), :] for k in range(SHORT_K)]
        row = lax.broadcasted_iota(jnp.int32, (8, tc), 0)
        vp = ucp_ref[...].astype(F32) * uhp_ref[...].astype(F32)

        def conv(block, before):
            acc = taps[SHORT_K - 1] * block
            for k in range(SHORT_K - 1):
                s = SHORT_K - 1 - k
                acc = acc + taps[k] * jnp.where(row >= s, pltpu.roll(block, s, 0), pltpu.roll(before, s, 0))
            return acc

        def strip(s, before):
            rows = pl.ds(pl.multiple_of(s * HALO, HALO), HALO)
            v = uc_ref[rows, :].astype(F32) * uh_ref[rows, :].astype(F32)
            top, bottom = v[0:8], v[8:HALO]
            cv = jnp.concatenate([conv(top, before), conv(bottom, top)], axis=0)
            y_ref[rows, :] = (ub_ref[rows, :].astype(F32) * cv).astype(BF16)
            return bottom

        lax.fori_loop(0, tm // HALO, strip, jnp.where(i > 0, vp[8:HALO], 0.0))

    prev = lambda off: (lambda i, j: (jnp.maximum(i * hb - 1, 0), off + j))
    return pl.pallas_call(
        body, name=name, grid=(t // tm, nj),
        in_specs=[pl.BlockSpec((tm, tc), lambda i, j: (i, j)),
                  pl.BlockSpec((tm, tc), lambda i, j: (i, nj + j)),
                  pl.BlockSpec((tm, tc), lambda i, j: (i, 2 * nj + j)),
                  pl.BlockSpec((HALO, tc), prev(nj)),
                  pl.BlockSpec((HALO, tc), prev(2 * nj)),
                  pl.BlockSpec((SHORT_K, tc), lambda i, j: (0, j))],
        out_specs=pl.BlockSpec((tm, tc), lambda i, j: (i, j)),
        out_shape=jax.ShapeDtypeStruct((t, out_cols), BF16),
        compiler_params=_params(("parallel", "parallel")),
    )(proj, proj, proj, proj, proj, kw)


def conv_mixer_bwd(proj, dy, kw, cw, name):
    t = proj.shape[0]
    tm = _tile(t, 1024, HALO)
    tc = cw
    nj = cw // tc
    hb = tm // HALO
    ni = t // tm
    last_hb = t // HALO - 1

    def body(ub_ref, uc_ref, uh_ref, dy_ref, ucp_ref, uhp_ref, ubn_ref, dyn_ref, kw_ref,
             du_ref, dkw_ref):
        i = pl.program_id(1)
        nstrips = tm // HALO
        taps = [kw_ref[pl.ds(k, 1), :] for k in range(SHORT_K)]
        row = lax.broadcasted_iota(jnp.int32, (8, tc), 0)
        vp = ucp_ref[...].astype(F32) * uhp_ref[...].astype(F32)
        dcvn = dyn_ref[...].astype(F32) * ubn_ref[...].astype(F32)

        def shifted(block, before, s):
            return jnp.where(row >= s, pltpu.roll(block, s, 0), pltpu.roll(before, s, 0))

        def lifted(block, after, s):
            return jnp.where(row < 8 - s, pltpu.roll(block, 8 - s, 0), pltpu.roll(after, 8 - s, 0))

        def down(s, carry):
            before, sums = carry
            rows = pl.ds(pl.multiple_of(s * HALO, HALO), HALO)
            v = uc_ref[rows, :].astype(F32) * uh_ref[rows, :].astype(F32)
            dyv = dy_ref[rows, :].astype(F32)
            dcv = dyv * ub_ref[rows, :].astype(F32)
            cvs = []
            sums = list(sums)
            for block, above, dcb in ((v[0:8], before, dcv[0:8]), (v[8:HALO], v[0:8], dcv[8:HALO])):
                moved = [shifted(block, above, SHORT_K - 1 - k) for k in range(SHORT_K - 1)] + [block]
                cvs.append(sum(taps[k] * moved[k] for k in range(SHORT_K)))
                sums = [sums[k] + dcb * moved[k] for k in range(SHORT_K)]
            du_ref[rows, 0:cw] = (dyv * jnp.concatenate(cvs, axis=0)).astype(BF16)
            return v[8:HALO], tuple(sums)

        zero = jnp.zeros((8, tc), F32)
        _, sums = lax.fori_loop(0, nstrips, down, (jnp.where(i > 0, vp[8:HALO], 0.0), (zero,) * SHORT_K))

        def up(n, after):
            rows = pl.ds(pl.multiple_of((nstrips - 1 - n) * HALO, HALO), HALO)
            uc = uc_ref[rows, :].astype(F32)
            uh = uh_ref[rows, :].astype(F32)
            dcv = dy_ref[rows, :].astype(F32) * ub_ref[rows, :].astype(F32)
            dvs = []
            for block, below in ((dcv[0:8], dcv[8:HALO]), (dcv[8:HALO], after)):
                dvs.append(taps[SHORT_K - 1] * block
                           + sum(taps[k] * lifted(block, below, SHORT_K - 1 - k) for k in range(SHORT_K - 1)))
            dv = jnp.concatenate(dvs, axis=0)
            du_ref[rows, cw:2 * cw] = (dv * uh).astype(BF16)
            du_ref[rows, 2 * cw:3 * cw] = (dv * uc).astype(BF16)
            return dcv[0:8]

        lax.fori_loop(0, nstrips, up, jnp.where(i < ni - 1, dcvn[0:8], 0.0))

        @pl.when(i == 0)
        def _():
            dkw_ref[...] = jnp.zeros_like(dkw_ref)

        for k in range(SHORT_K):
            dkw_ref[pl.ds(k, 1), :] += jnp.sum(sums[k], axis=0, keepdims=True)

    prev = lambda off: (lambda j, i: (jnp.maximum(i * hb - 1, 0), off + j))
    nxt = lambda off: (lambda j, i: (jnp.minimum((i + 1) * hb, last_hb), off + j))
    cur = lambda off: (lambda j, i: (i, off + j))
    return pl.pallas_call(
        body, name=name, grid=(nj, ni),
        in_specs=[pl.BlockSpec((tm, tc), cur(0)), pl.BlockSpec((tm, tc), cur(nj)),
                  pl.BlockSpec((tm, tc), cur(2 * nj)), pl.BlockSpec((tm, tc), cur(0)),
                  pl.BlockSpec((HALO, tc), prev(nj)), pl.BlockSpec((HALO, tc), prev(2 * nj)),
                  pl.BlockSpec((HALO, tc), nxt(0)), pl.BlockSpec((HALO, tc), nxt(0)),
                  pl.BlockSpec((SHORT_K, tc), lambda j, i: (0, j))],
        out_specs=[pl.BlockSpec((tm, 3 * cw), lambda j, i: (i, 0)),
                   pl.BlockSpec((SHORT_K, tc), lambda j, i: (0, j))],
        out_shape=[jax.ShapeDtypeStruct((t, 3 * cw), BF16), jax.ShapeDtypeStruct((SHORT_K, cw), F32)],
        compiler_params=_params(("parallel", "arbitrary")),
    )(proj, proj, proj, dy, proj, proj, proj, dy, kw)


def _head_column(mat, lane, h):
    return jnp.sum(jnp.where(lane == h, mat, 0.0), axis=-1, keepdims=True)


def _ssd_common(dt_raw_ref, dtb_ref, aneg_ref, cum_s, cumt_s, chunk):
    dt = _softplus(dt_raw_ref[...] + dtb_ref[...])
    al = dt * aneg_ref[...]
    ri = lax.broadcasted_iota(jnp.int32, (chunk, chunk), 0)
    ci = lax.broadcasted_iota(jnp.int32, (chunk, chunk), 1)
    cum = _dot_exact((ri >= ci).astype(F32), al)
    cum_s[...] = cum
    cumt_s[...] = cum.T
    return dt, cum, ri >= ci


EDGE = 16


def _shift_matrices(shift_s, chunk, kk, up):
    ri = lax.broadcasted_iota(jnp.int32, (chunk, chunk), 0)
    ci = lax.broadcasted_iota(jnp.int32, (chunk, chunk), 1)
    for k in range(kk - 1):
        s = kk - 1 - k
        shift_s[k] = ((ci - ri if up else ri - ci) == s).astype(BF16)


def _causal_conv(cur, head, kw_ref, b_ref, shift_s, kk):
    acc = b_ref[...] + kw_ref[pl.ds(kk - 1, 1), :] * cur.astype(F32)
    top = b_ref[...] + kw_ref[pl.ds(kk - 1, 1), :] * head[pl.ds(8, EDGE), :]
    for k in range(kk - 1):
        acc = acc + kw_ref[pl.ds(k, 1), :] * _dot(shift_s[k], cur)
        top = top + kw_ref[pl.ds(k, 1), :] * head[pl.ds(8 - (kk - 1) + k, EDGE), :]
    return acc, top


def ssd_fwd(proj, dt_raw, y_mix, kw_xs, kw_bc, b_xs, b_bc, dtb, aneg, dskip, normw, cw, si, name, rider=None):
    t = proj.shape[0]
    ch = min(SCAN_CHUNK, t)
    nc = t // ch
    npair = si // LANES
    ppg = npair // GROUPS
    gn = GROUPS * STATE
    gw = si // GROUPS
    assert cw == si and (3 * cw + 2 * si) % (2 * gn) == 0
    zblk = 3 * cw // si
    xsblk = zblk + 1
    bcblk = (3 * cw + 2 * si) // (2 * gn)

    def body(z_ref, xs_ref, bc_ref, dtr_ref, ymix_ref, kwx_ref, kwb_ref, bx_ref, bb_ref, dtb_ref, aneg_ref, dsk_ref,
             nw_ref, yb_ref, ys_ref, hs_ref, xcx_ref, xcb_ref,
             headx, headb, shift_s, xs_s, bc_s, h_s, gated_s, s_s, cum_s, cumt_s):
        del ymix_ref
        c = pl.program_id(0)

        @pl.when(c == 0)
        def _():
            h_s[...] = jnp.zeros_like(h_s)
            headx[0:8, :] = jnp.zeros((8, si), F32)
            headb[0:8, :] = jnp.zeros((8, 2 * gn), F32)
            _shift_matrices(shift_s, ch, SSD_K, up=False)

        for raw_ref, head, kw_ref, b_ref, pre_ref, act_s in ((xs_ref, headx, kwx_ref, bx_ref, xcx_ref, xs_s),
                                                           (bc_ref, headb, kwb_ref, bb_ref, xcb_ref, bc_s)):
            head[8:8 + EDGE, :] = raw_ref[0:EDGE, :].astype(F32)
            pre, top = _causal_conv(raw_ref[...], head, kw_ref, b_ref, shift_s, SSD_K)
            head[0:8, :] = raw_ref[ch - EDGE:ch, :].astype(F32)[EDGE - 8:EDGE]
            pre_ref[...] = pre.astype(BF16)
            pre_ref[0:EDGE, :] = top.astype(BF16)
            act_s[...] = (pre * _sigmoid(pre)).astype(act_s.dtype)
            act_s[0:EDGE, :] = (top * _sigmoid(top)).astype(act_s.dtype)

        dt, cum, tril = _ssd_common(dtr_ref, dtb_ref, aneg_ref, cum_s, cumt_s, ch)
        lane = lax.broadcasted_iota(jnp.int32, (ch, LANES), 1)
        lane1 = lax.broadcasted_iota(jnp.int32, (1, LANES), 1)
        low = lane < HEAD_DIM
        clast = cum_s[pl.ds(ch - 1, 1), :]

        for p in range(npair):
            g = p // ppg
            col = slice(p * LANES, (p + 1) * LANES)
            bg = bc_s[:, g * STATE:(g + 1) * STATE]
            cg = bc_s[:, gn + g * STATE:gn + (g + 1) * STATE]
            if p % ppg == 0:
                s_s[...] = _dot_nt(cg, bg)
            heads = (PAIR * p, PAIR * p + 1)
            ccol = [_head_column(cum, lane, h) for h in heads]
            dcol = [_head_column(dt, lane, h) for h in heads]
            cl = [jnp.sum(jnp.where(lane1 == h, clast, 0.0), axis=-1, keepdims=True) for h in heads]
            cum_px = jnp.where(low, ccol[0], ccol[1])
            dt_px = jnp.where(low, dcol[0], dcol[1])
            cl_px = jnp.where(lane1 < HEAD_DIM, cl[0], cl[1])
            xs_p = xs_s[:, col]
            xdt = xs_p * dt_px
            y = dsk_ref[:, col] * xs_p
            for hi, h in enumerate(heads):
                dec = jnp.exp(jnp.where(tril, ccol[hi] - cumt_s[pl.ds(h, 1), :], -jnp.inf))
                wm = (s_s[...] * dec).astype(BF16)
                xm = jnp.where(low if hi == 0 else jnp.logical_not(low), xdt, 0.0).astype(BF16)
                y = y + _dot(wm, xm)
            hp = h_s[p]
            hs_ref[0, p] = hp
            y = y + _dot(cg, hp.astype(BF16)) * jnp.exp(cum_px)
            st = _dot_tn(bg, (xdt * jnp.exp(cl_px - cum_px)).astype(BF16))
            h_s[p] = jnp.exp(cl_px) * hp + st
            ys_ref[:, col] = y.astype(BF16)
            zp = z_ref[:, col].astype(F32)
            gated_s[:, col] = y * zp * _sigmoid(zp)

        for g in range(GROUPS):
            col = slice(g * gw, (g + 1) * gw)
            gg = gated_s[:, col]
            r = lax.rsqrt(jnp.mean(gg * gg, axis=-1, keepdims=True) + EPS)
            yb_ref[:, col] = (gg * r * nw_ref[:, col]).astype(BF16)

    full = lambda shape: pl.BlockSpec(shape, lambda c: tuple(0 for _ in shape))
    return _ride(
        body, (nc,),
        [pl.BlockSpec((ch, si), lambda c: (c, zblk)),
         pl.BlockSpec((ch, si), lambda c: (c, xsblk)),
         pl.BlockSpec((ch, 2 * gn), lambda c: (c, bcblk)),
         pl.BlockSpec((ch, LANES), lambda c: (c, 0)),
         pl.BlockSpec(memory_space=pl.ANY),
         full((SSD_K, si)), full((SSD_K, 2 * gn)), full((1, si)), full((1, 2 * gn)),
         full((1, LANES)), full((1, LANES)), full((1, si)), full((1, si))],
        [pl.BlockSpec((ch, si), lambda c: (c, cw // si)),
         pl.BlockSpec((ch, si), lambda c: (c, 0)),
         pl.BlockSpec((1, npair, STATE, LANES), lambda c: (c, 0, 0, 0)),
         pl.BlockSpec((ch, si), lambda c: (c, 0)), pl.BlockSpec((ch, 2 * gn), lambda c: (c, 0))],
        [jax.ShapeDtypeStruct(y_mix.shape, BF16), jax.ShapeDtypeStruct((t, si), BF16),
         jax.ShapeDtypeStruct((nc, npair, STATE, LANES), F32),
         jax.ShapeDtypeStruct((t, si), BF16), jax.ShapeDtypeStruct((t, 2 * gn), BF16)],
        [pltpu.VMEM((8 + EDGE, si), F32), pltpu.VMEM((8 + EDGE, 2 * gn), F32),
         pltpu.VMEM((SSD_K - 1, ch, ch), BF16),
         pltpu.VMEM((ch, si), F32), pltpu.VMEM((ch, 2 * gn), BF16),
         pltpu.VMEM((npair, STATE, LANES), F32), pltpu.VMEM((ch, si), F32),
         pltpu.VMEM((ch, ch), F32), pltpu.VMEM((ch, LANES), F32), pltpu.VMEM((LANES, ch), F32)],
        [proj, proj, proj, dt_raw, y_mix, kw_xs, kw_bc, b_xs, b_bc, dtb, aneg, dskip, normw], {4: 0}, rider,
        ("arbitrary",), name)


def ssd_bwd(proj, dt_raw, ys, hsave, pre_xs, pre_bc, dy, kw_xs, kw_bc, dtb, aneg, dskip, normw, cw, si, name,
            rider=None):
    t = proj.shape[0]
    ch = min(SCAN_CHUNK, t)
    nc = t // ch
    npair = si // LANES
    ppg = npair // GROUPS
    gn = GROUPS * STATE
    gw = si // GROUPS
    zblk = 3 * cw // si
    xsblk = zblk + 1
    bcblk = (3 * cw + 2 * si) // (2 * gn)

    def body(z_ref, xs_ref, bc_ref, xcx_ref, xcb_ref, dtr_ref, ys_ref, hs_ref, dyb_ref,
             kwx_ref, kwb_ref, dtb_ref, aneg_ref, dsk_ref, nw_ref,
             dp_ref, ddt_ref, dkwx_ref, dkwb_ref, dbx_ref, dbb_ref, ddtb_ref, da_ref, ddsk_ref,
             dnw_ref,
             tailx, tailb, shift_s, xs_s, bc_s, dsx_s, dsb_s, dy_s, dxs_s, dbc_s, dh_s, s_s, ds_s,
             cum_s, cumt_s, dccol_s, dcrow_s, ddtcol_s, dcl_s):
        i = pl.program_id(0)

        @pl.when(i == 0)
        def _():
            dh_s[...] = jnp.zeros_like(dh_s)
            tailx[EDGE:EDGE + 8, :] = jnp.zeros((8, si), F32)
            tailb[EDGE:EDGE + 8, :] = jnp.zeros((8, 2 * gn), F32)
            _shift_matrices(shift_s, ch, SSD_K, up=True)
            for r in (dkwx_ref, dkwb_ref, dbx_ref, dbb_ref, ddtb_ref, da_ref, ddsk_ref, dnw_ref):
                r[...] = jnp.zeros_like(r)

        xc = xcx_ref[...].astype(F32)
        sg = _sigmoid(xc)
        xs_s[...] = xc * sg
        dsx_s[...] = sg * (1.0 + xc * (1.0 - sg))
        bcc = xcb_ref[...].astype(F32)
        sgb = _sigmoid(bcc)
        bc_s[...] = (bcc * sgb).astype(BF16)
        dsb_s[...] = sgb * (1.0 + bcc * (1.0 - sgb))

        dt, cum, tril = _ssd_common(dtr_ref, dtb_ref, aneg_ref, cum_s, cumt_s, ch)
        lane = lax.broadcasted_iota(jnp.int32, (ch, LANES), 1)
        lane1 = lax.broadcasted_iota(jnp.int32, (1, LANES), 1)
        low = lane < HEAD_DIM
        low1 = lane1 < HEAD_DIM
        clast = cum_s[pl.ds(ch - 1, 1), :]

        for g in range(GROUPS):
            col = slice(g * gw, (g + 1) * gw)
            ysf = ys_ref[:, col].astype(F32)
            zf = z_ref[:, col].astype(F32)
            sz = _sigmoid(zf)
            silz = zf * sz
            gg = ysf * silz
            r = lax.rsqrt(jnp.mean(gg * gg, axis=-1, keepdims=True) + EPS)
            nrm = gg * r
            dyb = dyb_ref[:, col].astype(F32)
            dnw_ref[:, col] += jnp.sum(dyb * nrm, axis=0, keepdims=True)
            dn = dyb * nw_ref[:, col]
            dgg = r * (dn - nrm * jnp.mean(dn * nrm, axis=-1, keepdims=True))
            dy_s[:, col] = dgg * silz
            dp_ref[:, col] = (dgg * ysf * (sz * (1.0 + zf * (1.0 - sz)))).astype(BF16)

        dccol_s[...] = jnp.zeros_like(dccol_s)
        dcrow_s[...] = jnp.zeros_like(dcrow_s)
        ddtcol_s[...] = jnp.zeros_like(ddtcol_s)
        dcl_s[...] = jnp.zeros_like(dcl_s)
        dbc_s[...] = jnp.zeros_like(dbc_s)

        for p in range(npair):
            g = p // ppg
            col = slice(p * LANES, (p + 1) * LANES)
            bcol = slice(g * STATE, (g + 1) * STATE)
            ccolg = slice(gn + g * STATE, gn + (g + 1) * STATE)
            bg = bc_s[:, bcol]
            cg = bc_s[:, ccolg]
            if p % ppg == 0:
                s_s[...] = _dot_nt(cg, bg)
                ds_s[...] = jnp.zeros_like(ds_s)
            heads = (PAIR * p, PAIR * p + 1)
            masks = (low, jnp.logical_not(low))
            masks1 = (low1, jnp.logical_not(low1))
            ccol = [_head_column(cum, lane, h) for h in heads]
            dcol = [_head_column(dt, lane, h) for h in heads]
            cl = [jnp.sum(jnp.where(lane1 == h, clast, 0.0), axis=-1, keepdims=True) for h in heads]
            cum_px = jnp.where(low, ccol[0], ccol[1])
            dt_px = jnp.where(low, dcol[0], dcol[1])
            cl_px = jnp.where(low1, cl[0], cl[1])
            e_px = jnp.exp(cum_px)
            dec_end = jnp.exp(cl_px - cum_px)
            gdec = jnp.exp(cl_px)
            xs_p = xs_s[:, col]
            xdt = xs_p * dt_px
            dyp = dy_s[:, col]
            hc = hs_ref[0, p]
            hcb = hc.astype(BF16)
            dhn = dh_s[p]
            dhnb = dhn.astype(BF16)

            ddsk_ref[:, col] += jnp.sum(dyp * xs_p, axis=0, keepdims=True)
            dxs_acc = dsk_ref[:, col] * dyp
            dye = dyp * e_px
            dyeb = dye.astype(BF16)
            dbc_s[:, ccolg] += _dot_nt(dyeb, hcb)
            dcum_lane = dye * _dot(cg, hcb)
            dh_from_y = _dot_tn(cg, dyeb)
            xd = xdt * dec_end
            dxd = _dot(bg, dhnb)
            dbc_s[:, bcol] += _dot_nt(xd.astype(BF16), dhnb)
            dxdt = dxd * dec_end
            t1 = dxd * xd
            dcum_lane = dcum_lane - t1
            dcl_lane = jnp.sum(t1, axis=0, keepdims=True) + jnp.sum(dhn * hc, axis=0, keepdims=True) * gdec
            dh_s[p] = gdec * dhn + dh_from_y
            xdtb = xdt.astype(BF16)
            for hi, h in enumerate(heads):
                dym = jnp.where(masks[hi], dyp, 0.0).astype(BF16)
                dw = _dot_nt(dym, xdtb)
                dec = jnp.exp(jnp.where(tril, ccol[hi] - cumt_s[pl.ds(h, 1), :], -jnp.inf))
                wm = s_s[...] * dec
                dxdt = dxdt + _dot_tn(wm.astype(BF16), dym)
                ds_s[...] += dw * dec
                gm = dw * wm
                rowsum = jnp.sum(gm, axis=-1, keepdims=True)
                lanesum = jnp.sum(jnp.where(masks[hi], dcum_lane, 0.0), axis=-1, keepdims=True)
                dccol_s[...] += jnp.where(lane == h, rowsum + lanesum, 0.0)
                dcrow_s[pl.ds(h, 1), :] = jnp.sum(gm, axis=0, keepdims=True)
                dcl_h = jnp.sum(jnp.where(masks1[hi], dcl_lane, 0.0), axis=-1, keepdims=True)
                dcl_s[...] += jnp.where(lane1 == h, dcl_h, 0.0)
            ddt_lane = dxdt * xs_p
            for hi, h in enumerate(heads):
                s = jnp.sum(jnp.where(masks[hi], ddt_lane, 0.0), axis=-1, keepdims=True)
                ddtcol_s[...] += jnp.where(lane == h, s, 0.0)
            dxs_s[:, col] = dxs_acc + dxdt * dt_px
            if p % ppg == ppg - 1:
                dsb = ds_s[...].astype(BF16)
                dbc_s[:, ccolg] += _dot(dsb, bg)
                dbc_s[:, bcol] += _dot_tn(dsb, cg)

        rowi = lax.broadcasted_iota(jnp.int32, (ch, LANES), 0)
        dcum = dccol_s[...] - dcrow_s[...].T + jnp.where(rowi == ch - 1, dcl_s[...], 0.0)
        ri = lax.broadcasted_iota(jnp.int32, (ch, ch), 0)
        ci = lax.broadcasted_iota(jnp.int32, (ch, ch), 1)
        dal = _dot_exact((ri <= ci).astype(F32), dcum)
        ddt = dal * aneg_ref[...] + ddtcol_s[...]
        da_ref[...] += jnp.sum(dal * dt, axis=0, keepdims=True)
        ddtr = ddt * _sigmoid(dtr_ref[...] + dtb_ref[...])
        ddt_ref[...] = ddtr
        ddtb_ref[...] += jnp.sum(ddtr, axis=0, keepdims=True)

        for (dpost, dsl, tail, raw_ref, kw_ref, dkw_ref, db_ref, out) in (
                (dxs_s, dsx_s, tailx, xs_ref, kwx_ref, dkwx_ref, dbx_ref, slice(si, 2 * si)),
                (dbc_s, dsb_s, tailb, bc_ref, kwb_ref, dkwb_ref, dbb_ref, slice(2 * si, 2 * si + 2 * gn))):
            dxc = dpost[...] * dsl[...]
            dxcb = dxc.astype(BF16)
            raw = raw_ref[...].astype(F32)
            raw_end = raw_ref[ch - EDGE:ch, :].astype(F32)
            tail[0:EDGE, :] = dxcb[ch - EDGE:ch].astype(F32)
            db_ref[...] += jnp.sum(dxc, axis=0, keepdims=True)
            draw = kw_ref[pl.ds(SSD_K - 1, 1), :] * dxc
            dkw_ref[pl.ds(SSD_K - 1, 1), :] += jnp.sum(dxc * raw, axis=0, keepdims=True)
            fix = jnp.zeros((EDGE, dxc.shape[1]), F32)
            for k in range(SSD_K - 1):
                moved = _dot(shift_s[k], dxcb)
                miss = tail[pl.ds(SSD_K - 1 - k, EDGE), :] - moved[ch - EDGE:ch]
                draw = draw + kw_ref[pl.ds(k, 1), :] * moved
                fix = fix + kw_ref[pl.ds(k, 1), :] * miss
                dkw_ref[pl.ds(k, 1), :] += (jnp.sum(moved * raw, axis=0, keepdims=True)
                                            + jnp.sum(miss * raw_end, axis=0, keepdims=True))
            dp_ref[:, out] = draw.astype(BF16)
            dp_ref[ch - EDGE:ch, out] = (draw[ch - EDGE:ch] + fix).astype(BF16)
            tail[EDGE:EDGE + 8, :] = dxcb[0:EDGE].astype(F32)[0:8]

    full = lambda shape: pl.BlockSpec(shape, lambda i: tuple(0 for _ in shape))
    rev = lambda blk: (lambda i: (nc - 1 - i, blk))
    small_in = [(SSD_K, si), (SSD_K, 2 * gn), (1, LANES), (1, LANES), (1, si), (1, si)]
    small = [(SSD_K, si), (SSD_K, 2 * gn), (1, si), (1, 2 * gn), (1, LANES), (1, LANES), (1, si), (1, si)]
    return _ride(
        body, (nc,),
        [pl.BlockSpec((ch, si), rev(zblk)), pl.BlockSpec((ch, si), rev(xsblk)),
         pl.BlockSpec((ch, 2 * gn), rev(bcblk)),
         pl.BlockSpec((ch, si), rev(0)), pl.BlockSpec((ch, 2 * gn), rev(0)),
         pl.BlockSpec((ch, LANES), rev(0)), pl.BlockSpec((ch, si), rev(0)),
         pl.BlockSpec((1, npair, STATE, LANES), lambda i: (nc - 1 - i, 0, 0, 0)),
         pl.BlockSpec((ch, si), rev(cw // si))] + [full(s) for s in small_in],
        [pl.BlockSpec((ch, 2 * si + 2 * gn), rev(0)), pl.BlockSpec((ch, LANES), rev(0))] + [full(s) for s in small],
        [jax.ShapeDtypeStruct((t, 2 * si + 2 * gn), BF16), jax.ShapeDtypeStruct((t, LANES), F32)]
        + [jax.ShapeDtypeStruct(s, F32) for s in small],
        [pltpu.VMEM((EDGE + 8, si), F32), pltpu.VMEM((EDGE + 8, 2 * gn), F32),
         pltpu.VMEM((SSD_K - 1, ch, ch), BF16),
         pltpu.VMEM((ch, si), F32), pltpu.VMEM((ch, 2 * gn), BF16),
         pltpu.VMEM((ch, si), F32), pltpu.VMEM((ch, 2 * gn), F32),
         pltpu.VMEM((ch, si), F32), pltpu.VMEM((ch, si), F32), pltpu.VMEM((ch, 2 * gn), F32),
         pltpu.VMEM((npair, STATE, LANES), F32),
         pltpu.VMEM((ch, ch), F32), pltpu.VMEM((ch, ch), F32),
         pltpu.VMEM((ch, LANES), F32), pltpu.VMEM((LANES, ch), F32),
         pltpu.VMEM((ch, LANES), F32), pltpu.VMEM((LANES, ch), F32),
         pltpu.VMEM((ch, LANES), F32), pltpu.VMEM((1, LANES), F32)],
        [proj, proj, proj, pre_xs, pre_bc, dt_raw, ys, hsave, dy, kw_xs, kw_bc, dtb, aneg, dskip, normw],
        {}, rider, ("arbitrary",), name)


def final_loss(x, nw, tgt, name):
    t, d = x.shape
    tm = _tile(t, 512, 8)

    def body(x_ref, nw_ref, t_ref, dx_ref, dnw_ref, ls_ref):
        xf = x_ref[...]
        r = lax.rsqrt(jnp.mean(xf * xf, axis=-1, keepdims=True) + EPS)
        nx = xf * r
        e = nx * nw_ref[...] - t_ref[...]
        dyv = e * (1.0 / d)
        dn = dyv * nw_ref[...]
        dx_ref[...] = r * (dn - nx * jnp.mean(dn * nx, axis=-1, keepdims=True))

        @pl.when(pl.program_id(0) == 0)
        def _():
            dnw_ref[...] = jnp.zeros_like(dnw_ref)
            ls_ref[...] = jnp.zeros_like(ls_ref)

        dnw_ref[...] += jnp.sum(dyv * nx, axis=0, keepdims=True)
        ls_ref[...] += jnp.sum(e * e, axis=0, keepdims=True) * (0.5 / d)

    return pl.pallas_call(
        body, name=name, grid=(t // tm,),
        in_specs=[pl.BlockSpec((tm, d), lambda i: (i, 0)), pl.BlockSpec((1, d), lambda i: (0, 0)),
                  pl.BlockSpec((tm, d), lambda i: (i, 0))],
        out_specs=[pl.BlockSpec((tm, d), lambda i: (i, 0)), pl.BlockSpec((1, d), lambda i: (0, 0)),
                   pl.BlockSpec((1, d), lambda i: (0, 0))],
        out_shape=[jax.ShapeDtypeStruct((t, d), F32), jax.ShapeDtypeStruct((1, d), F32),
                   jax.ShapeDtypeStruct((1, d), F32)],
        compiler_params=_params(("arbitrary",)),
    )(x, nw, tgt)


def _rows3(a):
    if a.ndim == 1:
        return a.reshape(1, 1, a.shape[0])
    if a.ndim == 2:
        return a.reshape(1, *a.shape)
    return a.reshape(-1, a.shape[-2], a.shape[-1])


def adamw(w, g, m, v, name):
    shape = w.shape
    views = [_rows3(a) for a in (w, g, m, v)]
    b, r, c = views[0].shape
    tr = _tile(r, 256, 16) if r % 16 == 0 else r

    def body(w_ref, g_ref, m_ref, v_ref, d_ref, nm_ref, nv_ref):
        g = g_ref[...]
        m = ADAM_B1 * m_ref[...] + (1.0 - ADAM_B1) * g
        v = ADAM_B2 * v_ref[...] + (1.0 - ADAM_B2) * (g * g)
        m_hat = m / (1.0 - ADAM_B1 ** ADAM_STEP)
        v_hat = v / (1.0 - ADAM_B2 ** ADAM_STEP)
        d_ref[...] = -ADAM_LR * (m_hat / (jnp.sqrt(v_hat) + ADAM_EPS) + ADAM_WD * w_ref[...])
        nm_ref[...] = m
        nv_ref[...] = v

    spec = pl.BlockSpec((1, tr, c), lambda i, j: (i, j, 0))
    outs = pl.pallas_call(
        body, name=name, grid=(b, r // tr), in_specs=[spec] * 4, out_specs=[spec] * 3,
        out_shape=[jax.ShapeDtypeStruct((b, r, c), F32)] * 3,
        compiler_params=_params(("parallel", "parallel")),
    )(*views)
    return [o.reshape(shape) for o in outs]


def adamw_halves(w, g_mine, g_theirs, m, v, name):
    nl, r, c = w.shape
    r2 = r // 2
    tr = _tile(r2, 256, 16)
    nb = r2 // tr

    def body(w_ref, gm_ref, gt_ref, m_ref, v_ref, g_ref, d_ref, nm_ref, nv_ref):
        mine = (pl.program_id(1) // nb) == lax.axis_index("c")
        g = jnp.where(mine, gm_ref[...], gt_ref[...])
        m = ADAM_B1 * m_ref[...] + (1.0 - ADAM_B1) * g
        v = ADAM_B2 * v_ref[...] + (1.0 - ADAM_B2) * (g * g)
        m_hat = m / (1.0 - ADAM_B1 ** ADAM_STEP)
        v_hat = v / (1.0 - ADAM_B2 ** ADAM_STEP)
        g_ref[...] = g
        d_ref[...] = -ADAM_LR * (m_hat / (jnp.sqrt(v_hat) + ADAM_EPS) + ADAM_WD * w_ref[...])
        nm_ref[...] = m
        nv_ref[...] = v

    whole = pl.BlockSpec((1, tr, c), lambda l, i: (l, i, 0))
    half = pl.BlockSpec((1, tr, c), lambda l, i: (l, i % nb, 0))
    return pl.pallas_call(
        body, name=name, grid=(nl, 2 * nb), in_specs=[whole, half, half, whole, whole], out_specs=[whole] * 4,
        out_shape=[jax.ShapeDtypeStruct((nl, r, c), F32)] * 4,
        compiler_params=_params(("parallel", "parallel")),
    )(w, g_mine, g_theirs, m, v)


def _coords():
    return lax.axis_index("x"), lax.axis_index("y"), lax.axis_index("c")


def _ici_peers(x, y):
    chips = [(1 - x, y), (x, 1 - y), (1 - x, 1 - y)]
    return chips, [2 * cx + cy for cx, cy in chips]


def _place(ref, how, chip, layers, per):
    if how == "lead":
        return ref.at[chip, layers]
    start = pl.multiple_of(chip * per, per)
    if how == "rows":
        return ref.at[layers, pl.ds(start, per), :]
    return ref.at[layers, :, pl.ds(start, per)]


def gather_weights(shards, hows, name):
    na = len(shards)
    out_shape = []
    for s, how in zip(shards, hows):
        assert s.shape[0] % 2 == 0
        if how == "lead":
            shp = (N_CHIPS, *s.shape)
        elif how == "rows":
            shp = (s.shape[0], N_CHIPS * s.shape[1], s.shape[2])
        else:
            shp = (s.shape[0], s.shape[1], N_CHIPS * s.shape[2])
        out_shape.append(jax.ShapeDtypeStruct(shp, s.dtype))

    def body(*refs):
        ins = refs[:na]
        outs = refs[na:2 * na]
        send_sems, recv_sems = refs[2 * na:]
        x, y, c = _coords()
        me = 2 * x + y
        chips, chip_ids = _ici_peers(x, y)
        sibling = (x, y, 1 - c)

        def dst(a, chip, layers):
            per = {"lead": 0, "rows": ins[a].shape[1], "cols": ins[a].shape[-1]}[hows[a]]
            return _place(outs[a], hows[a], chip, layers, per)

        def copy(a, k, src, dst_ref, to):
            return pltpu.make_async_remote_copy(
                src_ref=src, dst_ref=dst_ref, send_sem=send_sems.at[7 * a + k], recv_sem=recv_sems.at[7 * a + k],
                device_id=to, device_id_type=MESH)

        started = []
        halves = []
        for a in range(na):
            nl = ins[a].shape[0]
            hl = nl // 2
            mine = pl.ds(c * hl, hl)
            theirs = pl.ds((1 - c) * hl, hl)
            halves.append((mine, theirs))
            for k in range(3):
                cp = copy(a, k, ins[a].at[mine], dst(a, me, mine), (*chips[k], c))
                cp.start()
                started.append(cp)
            own = copy(a, 6, ins[a], dst(a, me, pl.ds(0, nl)), sibling)
            own.start()
            started.append(own)
        for a in range(na):
            mine, _ = halves[a]
            for k in range(3):
                landed = dst(a, chip_ids[k], mine)
                copy(a, k, landed, landed, (*chips[k], c)).wait_recv()
                fw = copy(a, 3 + k, landed, landed, sibling)
                fw.start()
                started.append(fw)
        for a in range(na):
            _, theirs = halves[a]
            for k in range(3):
                got = dst(a, chip_ids[k], theirs)
                copy(a, 3 + k, got, got, sibling).wait_recv()
            whole = dst(a, me, pl.ds(0, ins[a].shape[0]))
            copy(a, 6, whole, whole, sibling).wait_recv()
        for cp in started:
            cp.wait_send()

    return pl.pallas_call(
        body, name=name, in_specs=_any_specs(na), out_specs=_any_specs(na), out_shape=out_shape,
        scratch_shapes=[pltpu.SemaphoreType.DMA((7 * na,)), pltpu.SemaphoreType.DMA((7 * na,))],
        compiler_params=pltpu.CompilerParams(has_side_effects=True),
    )(*shards)


def _remote(src, dst, send_sems, recv_sems, k, to):
    return pltpu.make_async_remote_copy(src_ref=src, dst_ref=dst, send_sem=send_sems.at[k], recv_sem=recv_sems.at[k],
                                        device_id=to, device_id_type=MESH)


LAYER_HOW = ("lead", "rows", "cols", "rows")


def _layer_place(ref, how, chip, shard_shape, start, size):
    r, c = shard_shape
    if how == "lead":
        return ref.at[chip, :, pl.ds(start, size), :]
    if how == "rows":
        return ref.at[:, pl.ds(pl.multiple_of(chip * r + start, HALO), size), :]
    return ref.at[:, pl.ds(start, size), pl.ds(pl.multiple_of(chip * c, LANES), c)]


def weight_rider_ici(shards, hows, layer):
    shapes = [tuple(s.shape[1:]) for s in shards]
    out_shapes = []
    for (r, c), how, s in zip(shapes, hows, shards):
        shp = {"lead": (N_CHIPS, 1, r, c), "rows": (1, N_CHIPS * r, c), "cols": (1, r, N_CHIPS * c)}[how]
        out_shapes.append(jax.ShapeDtypeStruct(shp, s.dtype))

    def copies(ins, outs, send_sems, recv_sems):
        x, y, c = _coords()
        me = 2 * x + y
        chips, chip_ids = _ici_peers(x, y)
        sibling = (x, y, 1 - c)
        pairs = []
        for a, (shape, how) in enumerate(zip(shapes, hows)):
            half = shape[0] // 2
            mine = pl.multiple_of(c * half, HALO)
            src = ins[a].at[pl.ds(layer, 1)]
            for k in range(3):
                to = (*chips[k], c)
                land = _layer_place(outs[a], how, chip_ids[k], shape, mine, half)
                pairs.append((_remote(src.at[:, pl.ds(mine, half), :], _layer_place(outs[a], how, me, shape, mine, half),
                                      send_sems, recv_sems, 4 * a + k, to),
                              _remote(land, land, send_sems, recv_sems, 4 * a + k, to)))
            whole = _layer_place(outs[a], how, me, shape, 0, shape[0])
            pairs.append((_remote(src, whole, send_sems, recv_sems, 4 * a + 3, sibling),
                          _remote(whole, whole, send_sems, recv_sems, 4 * a + 3, sibling)))
        return pairs

    return Rider(list(shards), out_shapes, {}, 4 * len(shards), copies)


def weight_rider_d2d(bufs, shapes, hows):
    def copies(ins, outs, send_sems, recv_sems):
        x, y, c = _coords()
        _, chip_ids = _ici_peers(x, y)
        sibling = (x, y, 1 - c)
        pairs = []
        for a, (shape, how) in enumerate(zip(shapes, hows)):
            half = shape[0] // 2
            mine = pl.multiple_of(c * half, HALO)
            theirs = pl.multiple_of((1 - c) * half, HALO)
            for k in range(3):
                land = _layer_place(outs[a], how, chip_ids[k], shape, theirs, half)
                pairs.append((_remote(_layer_place(ins[a], how, chip_ids[k], shape, mine, half),
                                      _layer_place(outs[a], how, chip_ids[k], shape, mine, half),
                                      send_sems, recv_sems, 3 * a + k, sibling),
                              _remote(land, land, send_sems, recv_sems, 3 * a + k, sibling)))
        return pairs

    return Rider(list(bufs), [jax.ShapeDtypeStruct(b.shape, b.dtype) for b in bufs],
                 {a: a for a in range(len(bufs))}, 3 * len(bufs), copies)


def grads_rider_sibling(arrs):
    def copies(ins, outs, send_sems, recv_sems):
        x, y, c = _coords()
        sibling = (x, y, 1 - c)
        pairs = []
        for a in range(len(arrs)):
            r2 = ins[a].shape[1] // 2
            src = ins[a].at[:, pl.ds(pl.multiple_of((1 - c) * r2, 8), r2), :]
            pairs.append((_remote(src, outs[a], send_sems, recv_sems, a, sibling),
                          _remote(outs[a], outs[a], send_sems, recv_sems, a, sibling)))
        return pairs

    return Rider(list(arrs), [jax.ShapeDtypeStruct((a.shape[0], a.shape[1] // 2, a.shape[2]), a.dtype) for a in arrs],
                 {}, len(arrs), copies)


def chip_sum(g, recv, name):
    nch, r, c = g.shape
    r2 = r // 2
    tr = _tile(r2, 256, 16)
    nb = r2 // tr

    def body(g0_ref, g1_ref, r_ref, o32_ref, o16_ref):
        s = jnp.where(lax.axis_index("c") == 0, g0_ref[...], g1_ref[...]) + r_ref[...]
        o32_ref[...] = s
        o16_ref[...] = s.astype(BF16)

    here = pl.BlockSpec((1, tr, c), lambda i, j: (i, j, 0))
    return pl.pallas_call(
        body, name=name, grid=(nch, nb),
        in_specs=[here, pl.BlockSpec((1, tr, c), lambda i, j: (i, nb + j, 0)), here],
        out_specs=[here, here],
        out_shape=[jax.ShapeDtypeStruct((nch, r2, c), F32), jax.ShapeDtypeStruct((nch, r2, c), BF16)],
        compiler_params=_params(("parallel", "parallel")),
    )(g, g, recv)


def grads_rider_chips(arrs):
    def copies(ins, outs, send_sems, recv_sems):
        x, y, c = _coords()
        chips, chip_ids = _ici_peers(x, y)
        pairs = []
        for a in range(len(arrs)):
            for k in range(3):
                to = (*chips[k], c)
                pairs.append((_remote(ins[a].at[chip_ids[k]], outs[a].at[k], send_sems, recv_sems, 3 * a + k, to),
                              _remote(outs[a].at[k], outs[a].at[k], send_sems, recv_sems, 3 * a + k, to)))
        return pairs

    return Rider(list(arrs), [jax.ShapeDtypeStruct((3, *a.shape[1:]), a.dtype) for a in arrs], {}, 3 * len(arrs),
                 copies)


def grad_sum(p32, recv, layer, nl, buf, name):
    _, r2, c = p32.shape
    tr = _tile(r2, 256, 16)
    nb = r2 // tr

    def body(p0_ref, p1_ref, p2_ref, p3_ref, r0_ref, r1_ref, r2_ref, *rest):
        o_ref = rest[-1]
        x, y, _ = _coords()
        chip = 2 * x + y
        own = jnp.where(chip == 0, p0_ref[...], jnp.where(chip == 1, p1_ref[...],
                                                        jnp.where(chip == 2, p2_ref[...], p3_ref[...])))
        o_ref[...] = own + r0_ref[...].astype(F32) + r1_ref[...].astype(F32) + r2_ref[...].astype(F32)

    slot = lambda k: pl.BlockSpec((1, tr, c), lambda j: (k, j, 0))
    in_specs = [slot(k) for k in range(N_CHIPS)] + [slot(k) for k in range(3)]
    args = [p32] * N_CHIPS + [recv] * 3
    aliases = {}
    if buf is not None:
        in_specs.append(pl.BlockSpec(memory_space=pl.ANY))
        args.append(buf)
        aliases = {len(args) - 1: 0}
    return pl.pallas_call(
        body, name=name, grid=(nb,), in_specs=in_specs,
        out_specs=pl.BlockSpec((1, tr, c), lambda j: (layer, j, 0)),
        out_shape=jax.ShapeDtypeStruct((nl, r2, c), F32),
        input_output_aliases=aliases,
        compiler_params=_params(("parallel",)),
    )(*args)


def grads_rider_exchange(bufs):
    def copies(ins, outs, send_sems, recv_sems):
        x, y, c = _coords()
        sibling = (x, y, 1 - c)
        return [(_remote(ins[a], outs[a], send_sems, recv_sems, a, sibling),
                 _remote(outs[a], outs[a], send_sems, recv_sems, a, sibling)) for a in range(len(bufs))]

    return Rider(list(bufs), [jax.ShapeDtypeStruct(b.shape, b.dtype) for b in bufs], {}, len(bufs), copies)


def allreduce_small(buf, name):
    r, cdim = buf.shape

    def body(x_ref, o_ref, gath, send_sems, recv_sems):
        x, y, c = _coords()
        me, sibling = (x, y, c), (x, y, 1 - c)
        chips, _ = _ici_peers(x, y)

        def slot(px, py, pc):
            return gath.at[4 * px + 2 * py + pc]

        def copy(k, block, to, src=None):
            return pltpu.make_async_remote_copy(
                src_ref=slot(*block) if src is None else src, dst_ref=slot(*block),
                send_sem=send_sems.at[k], recv_sem=recv_sems.at[k], device_id=to, device_id_type=MESH)

        gath[4 * x + 2 * y + c] = x_ref[...]
        first = [copy(0, me, sibling, src=x_ref)]
        first += [copy(1 + j, me, (*chip, c), src=x_ref) for j, chip in enumerate(chips)]
        for cp in first:
            cp.start()
        passed = [copy(4 + j, (*chip, c), sibling) for j, chip in enumerate(chips)]
        for j, chip in enumerate(chips):
            copy(1 + j, (*chip, c), me).wait_recv()
            passed[j].start()
        copy(0, sibling, me).wait_recv()
        for j, chip in enumerate(chips):
            copy(4 + j, (*chip, 1 - c), me).wait_recv()
        for cp in first + passed:
            cp.wait_send()
        acc = gath[0]
        for d in range(1, 8):
            acc = acc + gath[d]
        o_ref[...] = acc

    return pl.pallas_call(
        body, name=name,
        in_specs=[pl.BlockSpec(memory_space=pltpu.VMEM)], out_specs=pl.BlockSpec(memory_space=pltpu.VMEM),
        out_shape=jax.ShapeDtypeStruct((r, cdim), F32),
        scratch_shapes=[pltpu.VMEM((8, r, cdim), F32), pltpu.SemaphoreType.DMA((7,)), pltpu.SemaphoreType.DMA((7,))],
        compiler_params=pltpu.CompilerParams(has_side_effects=True),
    )(buf)


def _expand_heads(v):
    return jnp.repeat(v.astype(F32), HEAD_DIM).reshape(1, -1)


def _pad_lanes(v):
    return jnp.pad(v.astype(F32), (0, LANES - v.shape[0])).reshape(1, LANES)


def local_step(x, tgt, p, comm, cols):
    nl = p["norm_mix_w"].shape[0]
    d = x.shape[1]
    cw = p["short_conv_w"].shape[2]
    si = p["ssd_norm_w"].shape[1]
    ff, npad = comm.ff, comm.npad
    nh = si // HEAD_DIM
    gn = GROUPS * STATE
    dt_off = 3 * cw + si + si + 2 * gn
    assert cols == dt_off + nh and nh <= LANES and dt_off % LANES == 0 and npad == dt_off + LANES
    pieces = [(0, cw), (cw, cw), (2 * cw, cw), (3 * cw, si), (3 * cw + si, si), (3 * cw + 2 * si, 2 * gn),
              (dt_off, LANES)]

    saved = []
    for l in range(nl):
        nw1 = p["norm_mix_w"][l].reshape(1, d)
        nw2 = p["norm_mlp_w"][l].reshape(1, d)
        kw3 = p["short_conv_w"][l]
        kwx, kwb = p["ssd_conv_w"][l][:, :si], p["ssd_conv_w"][l][:, si:]
        bx, bb = p["ssd_conv_b"][l][:si].reshape(1, si), p["ssd_conv_b"][l][si:].reshape(1, 2 * gn)
        dtb = _pad_lanes(p["dt_bias"][l])
        aneg = _pad_lanes(-jnp.exp(p["a_log"][l]))
        dsk = _expand_heads(p["d_skip"][l])
        snw = p["ssd_norm_w"][l].reshape(1, si)
        ssd_args = (kwx, kwb, bx, bb, dtb, aneg, dsk, snw)

        w_in = comm.weight(l, "w_in")
        (proj, h, dt_raw), sent = norm_matmul(x, nw1, w_in, 0, dt_off, BF16, "in_proj", tail_block=dt_off // LANES,
                                              rider=comm.rider("in_proj", l))
        comm.done("in_proj", l, sent)
        y_mix = conv_mixer_fwd(proj, kw3, cw, cw + si, "conv_mixer_fwd")
        (y_mix, *ssd_saved), sent = ssd_fwd(proj, dt_raw, y_mix, *ssd_args, cw, si, "ssd_fwd",
                                            rider=comm.rider("ssd_fwd", l))
        comm.done("ssd_fwd", l, sent)
        x2, _ = matmul(y_mix, comm.weight(l, "w_out"), 0, False, d, F32, "out_proj", residual=x)
        (up, h2), sent = norm_matmul(x2, nw2, comm.weight(l, "w_up"), 0, ff, BF16, "up_proj",
                                     rider=comm.rider("up_proj", l))
        comm.done("up_proj", l, sent)
        x3, sent = matmul(up, comm.weight(l, "w_down"), 0, False, d, F32, "down_proj", lhs_fn=_relu2, residual=x2,
                          rider=comm.rider("down_proj", l))
        comm.done("down_proj", l, sent)
        saved.append((x, h, proj, dt_raw, y_mix, ssd_saved, x2, h2, up, nw1, nw2, kw3, ssd_args))
        x = x3

    dx, dwf, lvec = final_loss(x, p["final_norm_w"].reshape(1, d), tgt, "final_loss")
    loss = jnp.sum(lvec)

    names = ("norm_mix_w", "short_conv_w", "ssd_conv_w", "ssd_conv_b", "dt_bias", "a_log", "d_skip",
             "ssd_norm_w", "norm_mlp_w")
    grads = {k: [None] * nl for k in names}
    for l in reversed(range(nl)):
        x0, h, proj, dt_raw, y_mix, ssd_saved, x2, h2, up, nw1, nw2, kw3, ssd_args = saved[l]
        kwx, kwb, _, _, dtb, aneg, dsk, snw = ssd_args
        dup, sent = matmul(dx, comm.weight(l, "w_down"), 0, True, ff, BF16, "down_bwd", relu_gate=up,
                           rider=comm.rider("down_bwd", l))
        comm.done("down_bwd", l, sent)
        g_down, sums = matmul_tn(up, dx, "down_wgrad", a_fn=_relu2, chip_sums=comm.side("down_wgrad", l))
        comm.side_done("down_wgrad", l, sums)
        comm.take_gradient(l, "w_down", g_down)
        (dx2, dnw2), _ = matmul_normbwd([(dup, 0)], [(0, ff)], comm.weight(l, "w_up"), 0, x2, nw2, dx, "up_bwd")
        comm.take_gradient(l, "w_up", matmul_tn(h2, dup, "up_wgrad", by_chip=True))
        grads["norm_mlp_w"][l] = dnw2.reshape(d)
        dy, sent = matmul(dx2, comm.weight(l, "w_out"), 0, True, cw + si, BF16, "out_bwd",
                          rider=comm.rider("out_bwd", l))
        comm.done("out_bwd", l, sent)
        g_out, sums = matmul_tn(y_mix, dx2, "out_wgrad", chip_sums=comm.side("out_wgrad", l))
        comm.side_done("out_wgrad", l, sums)
        comm.take_gradient(l, "w_out", g_out)
        du, dkw3 = conv_mixer_bwd(proj, dy, kw3, cw, "conv_mixer_bwd")
        (dssd, ddt, dkwx, dkwb, dbx, dbb, ddtb, da, ddsk, dsnw), sent = ssd_bwd(
            proj, dt_raw, *ssd_saved, dy, kwx, kwb, dtb, aneg, dsk, snw, cw, si, "ssd_bwd",
            rider=comm.rider("ssd_bwd", l))
        comm.done("ssd_bwd", l, sent)
        views = [(du, 0), (du, 1), (du, 2), (dssd, 0), (dssd, 1), (dssd, 2 * si // (2 * gn)), (ddt, 0)]
        (dxl, dnw1), sent = matmul_normbwd(views, pieces, comm.weight(l, "w_in"), 0, x0, nw1, dx2, "in_bwd",
                                           rider=comm.rider("in_bwd", l))
        comm.done("in_bwd", l, sent)
        parts = []
        for i, dp in enumerate((du, dssd, ddt)):
            part, bufs = matmul_tn(h, dp, "in_wgrad_%d" % i, grad_sums=comm.reductions("in_wgrad_%d" % i))
            comm.reductions_done("in_wgrad_%d" % i, bufs)
            parts.append(part)
        comm.take_gradient(l, "w_in", split_to_chips(parts, cols, "in_wgrad_split"))
        grads["norm_mix_w"][l] = dnw1.reshape(d)
        grads["short_conv_w"][l] = dkw3
        grads["ssd_conv_w"][l] = jnp.concatenate([dkwx, dkwb], axis=1)
        grads["ssd_conv_b"][l] = jnp.concatenate([dbx, dbb], axis=1).reshape(-1)
        grads["dt_bias"][l] = ddtb[0, :nh]
        grads["a_log"][l] = da[0, :nh] * aneg[0, :nh]
        grads["d_skip"][l] = jnp.sum(ddsk.reshape(nh, HEAD_DIM), axis=1)
        grads["ssd_norm_w"][l] = dsnw.reshape(si)
        dx = dxl

    grads = {k: jnp.stack(v) for k, v in grads.items()}
    grads["final_norm_w"] = dwf.reshape(d)
    return loss, dx, grads


BIG = ("w_in", "w_out", "w_up", "w_down")
SMALL_SHARDED = ("short_conv_w", "ssd_conv_w")
SMALL_REPL = ("norm_mix_w", "ssd_conv_b", "dt_bias", "a_log", "d_skip", "ssd_norm_w", "norm_mlp_w", "final_norm_w")
WEIGHTS = ("norm_mix_w", "w_in", "short_conv_w", "ssd_conv_w", "ssd_conv_b", "dt_bias", "a_log", "d_skip",
           "ssd_norm_w", "w_out", "norm_mlp_w", "w_up", "w_down", "final_norm_w")
SMALL_COLS = 1024


def _pack_small(named):
    flat = jnp.concatenate([v.reshape(-1).astype(F32) for v in named])
    n = flat.shape[0]
    rows = -(-n // SMALL_COLS)
    rows = -(-rows // 8) * 8
    return jnp.pad(flat, (0, rows * SMALL_COLS - n)).reshape(rows, SMALL_COLS)


def _unpack_small(buf, like):
    flat = buf.reshape(-1)
    out, off = [], 0
    for v in like:
        out.append(flat[off:off + v.size].reshape(v.shape))
        off += v.size
    return out


class ChipComm:
    IO = ("w_in", "w_out")
    MLP = ("w_up", "w_down")

    def __init__(self, shards, nl, npad):
        self.shards, self.nl, self.npad = shards, nl, npad
        self.how = dict(zip(BIG, LAYER_HOW))
        self.ff = N_CHIPS * shards["w_up"].shape[2]
        self.w = {}
        self.landed = {}
        self.grad = {}
        self.sums = {}
        self.from_sibling = {}
        self.from_chips = {}
        self.bufs = {k: None for k in BIG}
        first = run_rider(self._ici(self.IO, 0), "gather_first_ici")
        self._gathered(self.IO, 0, run_rider(self._d2d(self.IO, first), "gather_first_d2d"))

    def _ici(self, group, l):
        return weight_rider_ici([self.shards[k] for k in group], [self.how[k] for k in group], l)

    def _d2d(self, group, landed):
        return weight_rider_d2d(landed, [tuple(self.shards[k].shape[1:]) for k in group],
                                [self.how[k] for k in group])

    def _gathered(self, group, l, arrays):
        for k, g in zip(group, arrays):
            self.w[(l, k)] = join_from_chips(g, self.npad, "w_in_join") if k == "w_in" else g

    def _to_sibling(self, group, l):
        return grads_rider_sibling([self.grad[(l, k)] for k in group])

    def _summed(self, group, l, from_sibling):
        self.sums[group] = (l, [chip_sum(self.grad.pop((l, k)), r, "chip_sum") for k, r in zip(group, from_sibling)])

    def side(self, point, l):
        group = self.IO if point == "down_wgrad" else self.MLP
        if group not in self.from_sibling:
            return ()
        layer, parts = self.from_sibling[group]
        return [(self.grad[(layer, k)], r) for k, r in zip(group, parts)]

    def side_done(self, point, l, sums):
        group = self.IO if point == "down_wgrad" else self.MLP
        if sums:
            layer, _ = self.from_sibling.pop(group)
            for k in group:
                del self.grad[(layer, k)]
            self.sums[group] = (layer, sums)

    def _to_chips(self, group):
        return grads_rider_chips([s[1] for s in self.sums[group][1]])

    def _reduced(self, group, from_chips):
        l, sums = self.sums.pop(group)
        for k, s, r in zip(group, sums, from_chips):
            self.bufs[k] = grad_sum(s[0], r, l, self.nl, self.bufs[k], "grad_sum")

    def weight(self, l, name):
        return self.w[(l, name)]

    def take_gradient(self, l, name, g):
        self.grad[(l, name)] = g if g.ndim == 3 else g.reshape(N_CHIPS, g.shape[0] // N_CHIPS, g.shape[1])

    def rider(self, point, l):
        more = l + 1 < self.nl
        if point == "in_proj":
            return self._ici(self.MLP, l)
        if point == "ssd_fwd":
            return self._d2d(self.MLP, self.landed[self.MLP])
        if point == "up_proj":
            return self._ici(self.IO, l + 1) if more else None
        if point == "down_proj":
            return self._d2d(self.IO, self.landed[self.IO]) if more else None
        if point == "down_bwd":
            return self._to_sibling(self.IO, l + 1) if more else None
        if point == "out_bwd":
            return self._to_sibling(self.MLP, l)
        if point == "ssd_bwd":
            return self._to_chips(self.IO) if more else None
        return self._to_chips(self.MLP)

    def done(self, point, l, results):
        if not results:
            return
        if point in ("in_proj", "up_proj"):
            self.landed[self.MLP if point == "in_proj" else self.IO] = results
        elif point == "ssd_fwd":
            self._gathered(self.MLP, l, results)
        elif point == "down_proj":
            self._gathered(self.IO, l + 1, results)
        elif point == "down_bwd":
            self.from_sibling[self.IO] = (l + 1, results)
        elif point == "out_bwd":
            self.from_sibling[self.MLP] = (l, results)
        elif point == "ssd_bwd":
            self.from_chips[self.IO] = results
        else:
            self.from_chips[self.MLP] = results

    def reductions(self, point):
        group = self.IO if point == "in_wgrad_0" else self.MLP if point == "in_wgrad_1" else None
        if group not in self.from_chips:
            return ()
        layer, sums = self.sums[group]
        return [(s[0], r, layer, self.nl, self.bufs[k]) for k, s, r in zip(group, sums, self.from_chips[group])]

    def reductions_done(self, point, bufs):
        group = self.IO if point == "in_wgrad_0" else self.MLP
        if bufs:
            del self.sums[group], self.from_chips[group]
            self.bufs.update(zip(group, bufs))

    def finish(self):
        self._summed(self.IO, 0, run_rider(self._to_sibling(self.IO, 0), "grads_to_sibling"))
        self._reduced(self.IO, run_rider(self._to_chips(self.IO), "grads_to_chips"))
        mine = [self.bufs[k] for k in BIG]
        theirs = run_rider(grads_rider_exchange(mine), "grads_exchange")
        return dict(zip(BIG, zip(mine, theirs)))


def kernel(x, norm_mix_w, w_in, short_conv_w, ssd_conv_w, ssd_conv_b, dt_bias, a_log, d_skip, ssd_norm_w, w_out, norm_mlp_w, w_up, w_down, final_norm_w, loss_target, m_norm_mix_w, m_w_in, m_short_conv_w, m_ssd_conv_w, m_ssd_conv_b, m_dt_bias, m_a_log, m_d_skip, m_ssd_norm_w, m_w_out, m_norm_mlp_w, m_w_up, m_w_down, m_final_norm_w, v_norm_mix_w, v_w_in, v_short_conv_w, v_ssd_conv_w, v_ssd_conv_b, v_dt_bias, v_a_log, v_d_skip, v_ssd_norm_w, v_w_out, v_norm_mlp_w, v_w_up, v_w_down, v_final_norm_w):
    w = dict(norm_mix_w=norm_mix_w, w_in=w_in, short_conv_w=short_conv_w, ssd_conv_w=ssd_conv_w,
             ssd_conv_b=ssd_conv_b, dt_bias=dt_bias, a_log=a_log, d_skip=d_skip, ssd_norm_w=ssd_norm_w, w_out=w_out,
             norm_mlp_w=norm_mlp_w, w_up=w_up, w_down=w_down, final_norm_w=final_norm_w)
    m = dict(norm_mix_w=m_norm_mix_w, w_in=m_w_in, short_conv_w=m_short_conv_w, ssd_conv_w=m_ssd_conv_w,
             ssd_conv_b=m_ssd_conv_b, dt_bias=m_dt_bias, a_log=m_a_log, d_skip=m_d_skip, ssd_norm_w=m_ssd_norm_w,
             w_out=m_w_out, norm_mlp_w=m_norm_mlp_w, w_up=m_w_up, w_down=m_w_down, final_norm_w=m_final_norm_w)
    v = dict(norm_mix_w=v_norm_mix_w, w_in=v_w_in, short_conv_w=v_short_conv_w, ssd_conv_w=v_ssd_conv_w,
             ssd_conv_b=v_ssd_conv_b, dt_bias=v_dt_bias, a_log=v_a_log, d_skip=v_d_skip, ssd_norm_w=v_ssd_norm_w,
             w_out=v_w_out, norm_mlp_w=v_norm_mlp_w, w_up=v_w_up, w_down=v_w_down, final_norm_w=v_final_norm_w)
    xi, yi, ci = lax.axis_index("x"), lax.axis_index("y"), lax.axis_index("c")
    chip = 2 * xi + yi
    nl = w_up.shape[0]
    cols = N_CHIPS * w_in.shape[2]
    npad = cols // LANES * LANES + LANES

    full = dict(w)
    small_gathered = gather_weights([w[k] for k in SMALL_SHARDED], ["lead"] * len(SMALL_SHARDED), "gather_small")
    for k, g4 in zip(SMALL_SHARDED, small_gathered):
        full[k] = jnp.concatenate([g4[j] for j in range(N_CHIPS)], axis=2)
    comm = ChipComm({k: w[k].astype(BF16) for k in BIG}, nl, npad)

    loss, grad_x, grads = local_step(x[0], loss_target[0], full, comm, cols)
    loss = lax.psum(loss, ("x", "y", "c"))
    halves = comm.finish()
    g_shard = {}

    small_names = SMALL_REPL + SMALL_SHARDED
    small_sum = allreduce_small(_pack_small([grads[k] for k in small_names]), "allreduce_small")
    for k, g in zip(small_names, _unpack_small(small_sum, [grads[k] for k in small_names])):
        if k in SMALL_SHARDED:
            width = w[k].shape[2]
            g = lax.dynamic_slice_in_dim(g, chip * width, width, axis=2)
        g_shard[k] = g

    delta, new_m, new_v = {}, {}, {}
    for k in BIG:
        g_shard[k], delta[k], new_m[k], new_v[k] = adamw_halves(w[k], *halves[k], m[k], v[k], "adamw_%s" % k)
    packed = [_pack_small([d_[k] for k in small_names]) for d_ in (w, g_shard, m, v)]
    outs = adamw(*packed, "adamw_small")
    for d_, buf in zip((delta, new_m, new_v), outs):
        for k, val in zip(small_names, _unpack_small(buf, [w[k] for k in small_names])):
            d_[k] = val

    return (loss, grad_x[None], *[g_shard[k] for k in WEIGHTS], *[delta[k] for k in WEIGHTS],
            *[new_m[k] for k in WEIGHTS], *[new_v[k] for k in WEIGHTS])
```

```python
import functools

import jax
import jax.numpy as jnp
from jax import lax
from jax.experimental import pallas as pl
from jax.experimental.pallas import tpu as pltpu

F32 = jnp.float32
BF16 = jnp.bfloat16

EPS = 1e-5
HEAD_DIM = 64
STATE = 128
GROUPS = 2
SHORT_K = 3
SSD_K = 4
LANES = 128
PAIR = LANES // HEAD_DIM
SCAN_CHUNK = 256
HALO = 16
N_CHIPS = 4
VMEM_LIMIT = 56 * 1024 * 1024

ADAM_LR = 0.001
ADAM_B1 = 0.9
ADAM_B2 = 0.999
ADAM_EPS = 1e-08
ADAM_WD = 0.01
ADAM_STEP = 10

MESH = pl.DeviceIdType.MESH


def _params(sem):
    return pltpu.CompilerParams(dimension_semantics=sem, vmem_limit_bytes=VMEM_LIMIT)


def _tile(n, cap, quantum):
    if n <= cap:
        return n
    best = None
    for t in range(quantum, cap + 1, quantum):
        if n % t == 0:
            best = t
    assert best is not None, (n, cap, quantum)
    return best


def _dot(a, b):
    return jnp.dot(a, b, preferred_element_type=F32)


def _dot_nt(a, b):
    return lax.dot_general(a, b, (((1,), (1,)), ((), ())), preferred_element_type=F32)


def _dot_tn(a, b):
    return lax.dot_general(a, b, (((0,), (0,)), ((), ())), preferred_element_type=F32)


def _dot_exact(a, b):
    return jnp.dot(a, b, precision=lax.Precision.HIGHEST, preferred_element_type=F32)


def _sigmoid(x):
    return pl.reciprocal(1.0 + jnp.exp(-x), approx=True)


def _softplus(x):
    return jnp.maximum(x, 0.0) + jnp.log(1.0 + jnp.exp(-jnp.abs(x)))


def _relu2(v):
    return jnp.square(jnp.maximum(v, 0.0))


class Rider:
    def __init__(self, ins, out_shapes, aliases, n_sems, copies):
        self.ins, self.out_shapes, self.aliases, self.n_sems, self.copies = ins, out_shapes, aliases, n_sems, copies


def _any_specs(n):
    return [pl.BlockSpec(memory_space=pl.ANY)] * n


def _ride(body, grid, in_specs, out_specs, out_shape, scratch, args, aliases, rider, sem, name):
    n_in, n_out, n_scr = len(in_specs), len(out_specs), len(scratch)
    if rider is None:
        outs = pl.pallas_call(
            body, name=name, grid=grid, in_specs=in_specs, out_specs=out_specs, out_shape=out_shape,
            scratch_shapes=scratch, input_output_aliases=aliases, compiler_params=_params(sem))(*args)
        return list(outs), []
    ri, ro = len(rider.ins), len(rider.out_shapes)
    last = tuple(g - 1 for g in grid)

    def wrapped(*refs):
        ins = refs[:n_in]
        r_ins = refs[n_in:n_in + ri]
        outs = refs[n_in + ri:n_in + ri + n_out]
        r_outs = refs[n_in + ri + n_out:n_in + ri + n_out + ro]
        scr = refs[n_in + ri + n_out + ro:n_in + ri + n_out + ro + n_scr]
        send_sems, recv_sems = refs[-2:]
        ids = [pl.program_id(a) for a in range(len(grid))]
        at_first = functools.reduce(jnp.logical_and, [i == 0 for i in ids])
        at_last = functools.reduce(jnp.logical_and, [i == e for i, e in zip(ids, last)])

        @pl.when(at_first)
        def _():
            for cp, _ in rider.copies(r_ins, r_outs, send_sems, recv_sems):
                cp.start()

        body(*ins, *outs, *scr)

        @pl.when(at_last)
        def _():
            for cp, landed in rider.copies(r_ins, r_outs, send_sems, recv_sems):
                cp.wait_send()
                landed.wait_recv()

    all_aliases = dict(aliases)
    all_aliases.update({n_in + a: n_out + b for a, b in rider.aliases.items()})
    outs = pl.pallas_call(
        wrapped, name=name, grid=grid, in_specs=list(in_specs) + _any_specs(ri),
        out_specs=list(out_specs) + _any_specs(ro), out_shape=list(out_shape) + list(rider.out_shapes),
        scratch_shapes=list(scratch) + [pltpu.SemaphoreType.DMA((rider.n_sems,)),
                                        pltpu.SemaphoreType.DMA((rider.n_sems,))],
        input_output_aliases=all_aliases, compiler_params=_params(sem))(*args, *rider.ins)
    return list(outs[:n_out]), list(outs[n_out:])


def run_rider(rider, name):
    ri, ro = len(rider.ins), len(rider.out_shapes)

    def body(*refs):
        send_sems, recv_sems = refs[-2:]
        pairs = rider.copies(refs[:ri], refs[ri:ri + ro], send_sems, recv_sems)
        for cp, _ in pairs:
            cp.start()
        for cp, landed in pairs:
            cp.wait_send()
            landed.wait_recv()

    return list(pl.pallas_call(
        body, name=name, in_specs=_any_specs(ri), out_specs=_any_specs(ro), out_shape=list(rider.out_shapes),
        scratch_shapes=[pltpu.SemaphoreType.DMA((rider.n_sems,)), pltpu.SemaphoreType.DMA((rider.n_sems,))],
        input_output_aliases=dict(rider.aliases),
        compiler_params=pltpu.CompilerParams(has_side_effects=True))(*rider.ins))


def norm_matmul(x, nw, w, layer, n, out_dtype, name, tail_block=None, rider=None):
    t, d = x.shape
    mxu_cols = 2 * LANES
    tn = _tile(n, 1536, mxu_cols if n % mxu_cols == 0 else LANES)
    if n % mxu_cols == 0 and tn < 1024 <= n:
        tn = _tile(n, 3072, mxu_cols)
    tm = _tile(t, 512 if tn > 1536 else 1024, 8)
    nj = n // tn

    def body(x_ref, nw_ref, w_ref, *rest):
        if tail_block is None:
            o_ref, h_ref = rest
        else:
            wt_ref, o_ref, h_ref, tail_ref = rest

        @pl.when(pl.program_id(1) == 0)
        def _():
            xf = x_ref[...]
            r = lax.rsqrt(jnp.mean(xf * xf, axis=-1, keepdims=True) + EPS)
            h_ref[...] = (xf * r * nw_ref[...]).astype(BF16)

        o_ref[...] = _dot(h_ref[...], w_ref[...]).astype(out_dtype)
        if tail_block is not None:
            @pl.when(pl.program_id(1) == nj - 1)
            def _():
                tail_ref[...] = _dot(h_ref[...], wt_ref[...])

    in_specs = [pl.BlockSpec((tm, d), lambda i, j: (i, 0)), pl.BlockSpec((1, d), lambda i, j: (0, 0)),
                pl.BlockSpec((None, d, tn), lambda i, j: (layer, 0, j))]
    out_specs = [pl.BlockSpec((tm, tn), lambda i, j: (i, j)), pl.BlockSpec((tm, d), lambda i, j: (i, 0))]
    out_shape = [jax.ShapeDtypeStruct((t, n), out_dtype), jax.ShapeDtypeStruct((t, d), BF16)]
    args = [x, nw, w]
    if tail_block is not None:
        in_specs.append(pl.BlockSpec((None, d, LANES), lambda i, j: (layer, 0, tail_block)))
        out_specs.append(pl.BlockSpec((tm, LANES), lambda i, j: (i, 0)))
        out_shape.append(jax.ShapeDtypeStruct((t, LANES), F32))
        args.append(w)
    return _ride(body, (t // tm, nj), in_specs, out_specs, out_shape, [], args, {}, rider,
                 ("parallel", "arbitrary"), name)


def matmul(lhs, w, layer, transposed, n, out_dtype, name, *, lhs_fn=None, residual=None, relu_gate=None,
           rider=None):
    t, k = lhs.shape
    tm = _tile(t, 512 if k > 2048 else 1024, 8)
    tn = _tile(n, 1024, LANES)
    staged = lhs.dtype != BF16 or lhs_fn is not None
    fn = lhs_fn if lhs_fn is not None else (lambda v: v)
    has_extra = residual is not None or relu_gate is not None
    dot = _dot_nt if transposed else _dot

    def body(*refs):
        a_ref, w_ref = refs[:2]
        extra = refs[2] if has_extra else None
        o_ref = refs[3] if has_extra else refs[2]
        if staged:
            s_ref = refs[-1]

            @pl.when(pl.program_id(1) == 0)
            def _():
                s_ref[...] = fn(a_ref[...].astype(F32)).astype(BF16)

            a_ref = s_ref
        acc = dot(a_ref[...], w_ref[...])
        if residual is not None:
            acc = acc + extra[...]
        if relu_gate is not None:
            acc = acc * (2.0 * jnp.maximum(extra[...].astype(F32), 0.0))
        o_ref[...] = acc.astype(out_dtype)

    if transposed:
        w_spec = pl.BlockSpec((None, tn, k), lambda i, j: (layer, j, 0))
    else:
        w_spec = pl.BlockSpec((None, k, tn), lambda i, j: (layer, 0, j))
    in_specs = [pl.BlockSpec((tm, k), lambda i, j: (i, 0)), w_spec]
    args = [lhs, w]
    if has_extra:
        in_specs.append(pl.BlockSpec((tm, tn), lambda i, j: (i, j)))
        args.append(residual if residual is not None else relu_gate)
    outs, extra = _ride(
        body, (t // tm, n // tn), in_specs, [pl.BlockSpec((tm, tn), lambda i, j: (i, j))],
        [jax.ShapeDtypeStruct((t, n), out_dtype)], [pltpu.VMEM((tm, k), BF16)] if staged else [], args, {},
        rider, ("parallel", "arbitrary"), name)
    return outs[0], extra


def matmul_normbwd(lhs, pieces, w, layer, x, nw, dres, name, rider=None):
    t, d = x.shape
    nl = len(lhs)
    tm = _tile(t, 512, 8)
    for off, width in pieces:
        assert off % width == 0

    def body(*refs):
        lrefs = refs[:nl]
        wrefs = refs[nl:2 * nl]
        x_ref, nw_ref, dres_ref, dx_ref, dnw_ref = refs[2 * nl:]
        dh = _dot_nt(lrefs[0][...].astype(BF16), wrefs[0][...])
        for a_ref, w_ref in zip(lrefs[1:], wrefs[1:]):
            dh = dh + _dot_nt(a_ref[...].astype(BF16), w_ref[...])
        xf = x_ref[...]
        r = lax.rsqrt(jnp.mean(xf * xf, axis=-1, keepdims=True) + EPS)
        nx = xf * r
        dn = dh * nw_ref[...]
        dx = r * (dn - nx * jnp.mean(dn * nx, axis=-1, keepdims=True))
        dx_ref[...] = dres_ref[...] + dx

        @pl.when(pl.program_id(0) == 0)
        def _():
            dnw_ref[...] = jnp.zeros_like(dnw_ref)

        dnw_ref[...] += jnp.sum(dh * nx, axis=0, keepdims=True)

    in_specs = [pl.BlockSpec((tm, width), (lambda blk: (lambda i: (i, blk)))(blk))
                for (_, blk), (_, width) in zip(lhs, pieces)]
    in_specs += [pl.BlockSpec((None, d, width), (lambda blk: (lambda i: (layer, 0, blk)))(off // width),
                              pipeline_mode=pl.Buffered(1))
                 for off, width in pieces]
    in_specs += [pl.BlockSpec((tm, d), lambda i: (i, 0)), pl.BlockSpec((1, d), lambda i: (0, 0)),
                 pl.BlockSpec((tm, d), lambda i: (i, 0))]
    return _ride(
        body, (t // tm,), in_specs,
        [pl.BlockSpec((tm, d), lambda i: (i, 0)), pl.BlockSpec((1, d), lambda i: (0, 0))],
        [jax.ShapeDtypeStruct((t, d), F32), jax.ShapeDtypeStruct((1, d), F32)], [],
        [*[a for a, _ in lhs], *([w] * nl), x, nw, dres], {}, rider, ("arbitrary",), name)


def matmul_tn(a, b, name, *, a_fn=None, by_chip=False, chip_sums=None, grad_sums=None):
    t, k = a.shape
    n = b.shape[1]
    tk = _tile(k, 1024, LANES)
    nn = n // N_CHIPS if by_chip else n
    tn = _tile(nn, 1536, 2 * LANES if nn % (2 * LANES) == 0 else LANES)
    tt = _tile(t, 1024, 8)
    nt = t // tt
    gn_ = n // tn
    steps = (k // tk) * gn_ * nt
    fn = a_fn if a_fn is not None else (lambda v: v)
    with_sums, chip_sums = chip_sums, list(chip_sums or ())
    for g, _ in chip_sums:
        if (g.shape[0] * g.shape[1] // 2) % (steps * HALO) or steps % g.shape[0]:
            return (matmul_tn(a, b, name, a_fn=a_fn, by_chip=by_chip),
                    [tuple(chip_sum(g_, r_, "chip_sum")) for g_, r_ in chip_sums])
    with_grads, grad_sums = grad_sums, list(grad_sums or ())
    assert not (chip_sums and grad_sums)
    for p32, *_ in grad_sums:
        if p32.shape[1] % (steps * HALO):
            return (matmul_tn(a, b, name, a_fn=a_fn, by_chip=by_chip),
                    [grad_sum(*item, "grad_sum") for item in grad_sums])
    ns = len(chip_sums)
    ng = len(grad_sums)
    n_gin = [N_CHIPS + 3 + (item[4] is not None) for item in grad_sums]
    step = lambda i, j, s: (i * gn_ + j) * nt + s

    def body(*refs):
        a_ref, b_ref = refs[:2]
        side_in = refs[2:2 + 3 * ns]
        grad_in = refs[2 + 3 * ns:2 + 3 * ns + sum(n_gin)]
        o_ref = refs[2 + 3 * ns + sum(n_gin)]
        side_out = refs[3 + 3 * ns + sum(n_gin):3 + 5 * ns + sum(n_gin)]
        grad_out = refs[3 + 5 * ns + sum(n_gin):3 + 5 * ns + sum(n_gin) + ng]
        acc_ref = refs[-1]

        @pl.when(pl.program_id(2) == 0)
        def _():
            acc_ref[...] = jnp.zeros_like(acc_ref)

        av = a_ref[...]
        if a_fn is not None:
            av = fn(av.astype(F32))
        acc_ref[...] += _dot_tn(av.astype(BF16), b_ref[...].astype(BF16))
        for q in range(ns):
            g0_ref, g1_ref, r_ref = side_in[3 * q:3 * q + 3]
            tot = jnp.where(lax.axis_index("c") == 0, g0_ref[...], g1_ref[...]) + r_ref[...]
            side_out[2 * q][...] = tot
            side_out[2 * q + 1][...] = tot.astype(BF16)
        pos = 0
        for q in range(ng):
            p = grad_in[pos:pos + N_CHIPS]
            r = grad_in[pos + N_CHIPS:pos + N_CHIPS + 3]
            pos += n_gin[q]
            x_, y_, _ = _coords()
            chip = 2 * x_ + y_
            own = jnp.where(chip == 0, p[0][...], jnp.where(chip == 1, p[1][...],
                                                           jnp.where(chip == 2, p[2][...], p[3][...])))
            grad_out[q][...] = (own + r[0][...].astype(F32) + r[1][...].astype(F32)
                                + r[2][...].astype(F32))[None]

        @pl.when(pl.program_id(2) == nt - 1)
        def _():
            o_ref[...] = acc_ref[...]

    if by_chip:
        per = n // N_CHIPS // tn
        out_specs = [pl.BlockSpec((None, tk, tn), lambda i, j, s: (j // per, i, j % per))]
        out_shape = [jax.ShapeDtypeStruct((N_CHIPS, k, n // N_CHIPS), F32)]
    else:
        out_specs = [pl.BlockSpec((tk, tn), lambda i, j, s: (i, j))]
        out_shape = [jax.ShapeDtypeStruct((k, n), F32)]
    in_specs = [pl.BlockSpec((tt, tk), lambda i, j, s: (s, i)), pl.BlockSpec((tt, tn), lambda i, j, s: (s, j))]
    args = [a, b]
    for g, recv in chip_sums:
        nch, r, c = g.shape
        r2 = r // 2
        rows = nch * r2 // steps
        per_chip = r2 // rows
        assert rows % HALO == 0 and r2 % rows == 0
        for half in range(2):
            in_specs.append(pl.BlockSpec(
                (rows, c), (lambda h: (lambda i, j, s: ((step(i, j, s) // per_chip) * 2 * per_chip + h * per_chip
                                                        + step(i, j, s) % per_chip, 0)))(half)))
        flat = pl.BlockSpec((rows, c), lambda i, j, s: (step(i, j, s), 0))
        in_specs.append(flat)
        args += [g.reshape(nch * r, c), g.reshape(nch * r, c), recv.reshape(nch * r2, c)]
        out_specs += [flat, flat]
        out_shape += [jax.ShapeDtypeStruct((nch * r2, c), F32), jax.ShapeDtypeStruct((nch * r2, c), BF16)]
    aliases = {}
    for p32, recv, layer, nl, buf in grad_sums:
        _, r2, c = p32.shape
        rows = r2 // steps
        slab = lambda kk: pl.BlockSpec((rows, c), lambda i, j, s: (kk * steps + step(i, j, s), 0))
        in_specs += [slab(kk) for kk in range(N_CHIPS)] + [slab(kk) for kk in range(3)]
        args += [p32.reshape(N_CHIPS * r2, c)] * N_CHIPS + [recv.reshape(3 * r2, c)] * 3
        if buf is not None:
            aliases[len(args)] = len(out_specs)
            in_specs.append(pl.BlockSpec(memory_space=pl.ANY))
            args.append(buf)
        out_specs.append(pl.BlockSpec((1, rows, c), (lambda ly: (lambda i, j, s: (ly, step(i, j, s), 0)))(layer)))
        out_shape.append(jax.ShapeDtypeStruct((nl, r2, c), F32))
    outs = pl.pallas_call(
        body, name=name, grid=(k // tk, gn_, nt), in_specs=in_specs, out_specs=out_specs, out_shape=out_shape,
        scratch_shapes=[pltpu.VMEM((tk, tn), F32)], input_output_aliases=aliases,
        compiler_params=_params(("parallel", "parallel", "arbitrary")),
    )(*args)
    if with_grads is not None:
        return outs[0], list(outs[1:])
    if with_sums is None:
        return outs[0]
    sums = [(outs[1 + 2 * q].reshape(g.shape[0], g.shape[1] // 2, g.shape[2]),
             outs[2 + 2 * q].reshape(g.shape[0], g.shape[1] // 2, g.shape[2])) for q, (g, _) in enumerate(chip_sums)]
    return outs[0], sums


def split_to_chips(pieces, cols, name):
    d = pieces[0].shape[0]
    widths = [p.shape[1] for p in pieces]
    w = cols // N_CHIPS
    tr = _tile(d, 256, 8)
    npc = len(pieces)

    def body(*refs):
        o_ref, row = refs[npc], refs[npc + 1]
        off = 0
        for r, n in zip(refs[:npc], widths):
            row[:, off:off + n] = r[...]
            off += n
        for j in range(N_CHIPS):
            o_ref[j] = row[:, j * w:(j + 1) * w]

    return pl.pallas_call(
        body, name=name, grid=(d // tr,),
        in_specs=[pl.BlockSpec((tr, n), lambda i: (i, 0)) for n in widths],
        out_specs=pl.BlockSpec((N_CHIPS, tr, w), lambda i: (0, i, 0)),
        out_shape=jax.ShapeDtypeStruct((N_CHIPS, d, w), F32),
        scratch_shapes=[pltpu.VMEM((tr, sum(widths)), F32)],
        compiler_params=_params(("parallel",)),
    )(*pieces)


def join_from_chips(g4, npad, name):
    _, nl, d, w = g4.shape
    tr = _tile(d, 256, HALO)

    def body(g_ref, o_ref):
        for j in range(N_CHIPS):
            o_ref[:, j * w:(j + 1) * w] = g_ref[j]
        o_ref[:, N_CHIPS * w:] = jnp.zeros((tr, npad - N_CHIPS * w), o_ref.dtype)

    return pl.pallas_call(
        body, name=name, grid=(nl, d // tr),
        in_specs=[pl.BlockSpec((N_CHIPS, None, tr, w), lambda l, i: (0, l, i, 0))],
        out_specs=pl.BlockSpec((None, tr, npad), lambda l, i: (l, i, 0)),
        out_shape=jax.ShapeDtypeStruct((nl, d, npad), g4.dtype),
        compiler_params=_params(("parallel", "parallel")),
    )(g4)


def conv_mixer_fwd(proj, kw, cw, out_cols, name):
    t = proj.shape[0]
    tm = _tile(t, 1024, HALO)
    tc = _tile(cw, 1024, LANES)
    nj = cw // tc
    hb = tm // HALO

    def body(ub_ref, uc_ref, uh_ref, ucp_ref, uhp_ref, kw_ref, y_ref):
        i = pl.program_id(0)
        taps = [kw_ref[pl.ds(k, 1), :] for k in range(SHORT_K)]
        row = lax.broadcasted_iota(jnp.int32, (8, tc), 0)
        vp = ucp_ref[...].astype(F32) * uhp_ref[...].astype(F32)

        def conv(block, before):
            acc = taps[SHORT_K - 1] * block
            for k in range(SHORT_K - 1):
                s = SHORT_K - 1 - k
                acc = acc + taps[k] * jnp.where(row >= s, pltpu.roll(block, s, 0), pltpu.roll(before, s, 0))
            return acc

        def strip(s, before):
            rows = pl.ds(pl.multiple_of(s * HALO, HALO), HALO)
            v = uc_ref[rows, :].astype(F32) * uh_ref[rows, :].astype(F32)
            top, bottom = v[0:8], v[8:HALO]
            cv = jnp.concatenate([conv(top, before), conv(bottom, top)], axis=0)
            y_ref[rows, :] = (ub_ref[rows, :].astype(F32) * cv).astype(BF16)
            return bottom

        lax.fori_loop(0, tm // HALO, strip, jnp.where(i > 0, vp[8:HALO], 0.0))

    prev = lambda off: (lambda i, j: (jnp.maximum(i * hb - 1, 0), off + j))
    return pl.pallas_call(
        body, name=name, grid=(t // tm, nj),
        in_specs=[pl.BlockSpec((tm, tc), lambda i, j: (i, j)),
                  pl.BlockSpec((tm, tc), lambda i, j: (i, nj + j)),
                  pl.BlockSpec((tm, tc), lambda i, j: (i, 2 * nj + j)),
                  pl.BlockSpec((HALO, tc), prev(nj)),
                  pl.BlockSpec((HALO, tc), prev(2 * nj)),
                  pl.BlockSpec((SHORT_K, tc), lambda i, j: (0, j))],
        out_specs=pl.BlockSpec((tm, tc), lambda i, j: (i, j)),
        out_shape=jax.ShapeDtypeStruct((t, out_cols), BF16),
        compiler_params=_params(("parallel", "parallel")),
    )(proj, proj, proj, proj, proj, kw)


def conv_mixer_bwd(proj, dy, kw, cw, name):
    t = proj.shape[0]
    tm = _tile(t, 1024, HALO)
    tc = cw
    nj = cw // tc
    hb = tm // HALO
    ni = t // tm
    last_hb = t // HALO - 1

    def body(ub_ref, uc_ref, uh_ref, dy_ref, ucp_ref, uhp_ref, ubn_ref, dyn_ref, kw_ref,
             du_ref, dkw_ref):
        i = pl.program_id(1)
        nstrips = tm // HALO
        taps = [kw_ref[pl.ds(k, 1), :] for k in range(SHORT_K)]
        row = lax.broadcasted_iota(jnp.int32, (8, tc), 0)
        vp = ucp_ref[...].astype(F32) * uhp_ref[...].astype(F32)
        dcvn = dyn_ref[...].astype(F32) * ubn_ref[...].astype(F32)

        def shifted(block, before, s):
            return jnp.where(row >= s, pltpu.roll(block, s, 0), pltpu.roll(before, s, 0))

        def lifted(block, after, s):
            return jnp.where(row < 8 - s, pltpu.roll(block, 8 - s, 0), pltpu.roll(after, 8 - s, 0))

        def down(s, carry):
            before, sums = carry
            rows = pl.ds(pl.multiple_of(s * HALO, HALO), HALO)
            v = uc_ref[rows, :].astype(F32) * uh_ref[rows, :].astype(F32)
            dyv = dy_ref[rows, :].astype(F32)
            dcv = dyv * ub_ref[rows, :].astype(F32)
            cvs = []
            sums = list(sums)
            for block, above, dcb in ((v[0:8], before, dcv[0:8]), (v[8:HALO], v[0:8], dcv[8:HALO])):
                moved = [shifted(block, above, SHORT_K - 1 - k) for k in range(SHORT_K - 1)] + [block]
                cvs.append(sum(taps[k] * moved[k] for k in range(SHORT_K)))
                sums = [sums[k] + dcb * moved[k] for k in range(SHORT_K)]
            du_ref[rows, 0:cw] = (dyv * jnp.concatenate(cvs, axis=0)).astype(BF16)
            return v[8:HALO], tuple(sums)

        zero = jnp.zeros((8, tc), F32)
        _, sums = lax.fori_loop(0, nstrips, down, (jnp.where(i > 0, vp[8:HALO], 0.0), (zero,) * SHORT_K))

        def up(n, after):
            rows = pl.ds(pl.multiple_of((nstrips - 1 - n) * HALO, HALO), HALO)
            uc = uc_ref[rows, :].astype(F32)
            uh = uh_ref[rows, :].astype(F32)
            dcv = dy_ref[rows, :].astype(F32) * ub_ref[rows, :].astype(F32)
            dvs = []
            for block, below in ((dcv[0:8], dcv[8:HALO]), (dcv[8:HALO], after)):
                dvs.append(taps[SHORT_K - 1] * block
                           + sum(taps[k] * lifted(block, below, SHORT_K - 1 - k) for k in range(SHORT_K - 1)))
            dv = jnp.concatenate(dvs, axis=0)
            du_ref[rows, cw:2 * cw] = (dv * uh).astype(BF16)
            du_ref[rows, 2 * cw:3 * cw] = (dv * uc).astype(BF16)
            return dcv[0:8]

        lax.fori_loop(0, nstrips, up, jnp.where(i < ni - 1, dcvn[0:8], 0.0))

        @pl.when(i == 0)
        def _():
            dkw_ref[...] = jnp.zeros_like(dkw_ref)

        for k in range(SHORT_K):
            dkw_ref[pl.ds(k, 1), :] += jnp.sum(sums[k], axis=0, keepdims=True)

    prev = lambda off: (lambda j, i: (jnp.maximum(i * hb - 1, 0), off + j))
    nxt = lambda off: (lambda j, i: (jnp.minimum((i + 1) * hb, last_hb), off + j))
    cur = lambda off: (lambda j, i: (i, off + j))
    return pl.pallas_call(
        body, name=name, grid=(nj, ni),
        in_specs=[pl.BlockSpec((tm, tc), cur(0)), pl.BlockSpec((tm, tc), cur(nj)),
                  pl.BlockSpec((tm, tc), cur(2 * nj)), pl.BlockSpec((tm, tc), cur(0)),
                  pl.BlockSpec((HALO, tc), prev(nj)), pl.BlockSpec((HALO, tc), prev(2 * nj)),
                  pl.BlockSpec((HALO, tc), nxt(0)), pl.BlockSpec((HALO, tc), nxt(0)),
                  pl.BlockSpec((SHORT_K, tc), lambda j, i: (0, j))],
        out_specs=[pl.BlockSpec((tm, 3 * cw), lambda j, i: (i, 0)),
                   pl.BlockSpec((SHORT_K, tc), lambda j, i: (0, j))],
        out_shape=[jax.ShapeDtypeStruct((t, 3 * cw), BF16), jax.ShapeDtypeStruct((SHORT_K, cw), F32)],
        compiler_params=_params(("parallel", "arbitrary")),
    )(proj, proj, proj, dy, proj, proj, proj, dy, kw)


def _head_column(mat, lane, h):
    return jnp.sum(jnp.where(lane == h, mat, 0.0), axis=-1, keepdims=True)


def _ssd_common(dt_raw_ref, dtb_ref, aneg_ref, cum_s, cumt_s, chunk):
    dt = _softplus(dt_raw_ref[...] + dtb_ref[...])
    al = dt * aneg_ref[...]
    ri = lax.broadcasted_iota(jnp.int32, (chunk, chunk), 0)
    ci = lax.broadcasted_iota(jnp.int32, (chunk, chunk), 1)
    cum = _dot_exact((ri >= ci).astype(F32), al)
    cum_s[...] = cum
    cumt_s[...] = cum.T
    return dt, cum, ri >= ci


EDGE = 16


def _shift_matrices(shift_s, chunk, kk, up):
    ri = lax.broadcasted_iota(jnp.int32, (chunk, chunk), 0)
    ci = lax.broadcasted_iota(jnp.int32, (chunk, chunk), 1)
    for k in range(kk - 1):
        s = kk - 1 - k
        shift_s[k] = ((ci - ri if up else ri - ci) == s).astype(BF16)


def _causal_conv(cur, head, kw_ref, b_ref, shift_s, kk):
    acc = b_ref[...] + kw_ref[pl.ds(kk - 1, 1), :] * cur.astype(F32)
    top = b_ref[...] + kw_ref[pl.ds(kk - 1, 1), :] * head[pl.ds(8, EDGE), :]
    for k in range(kk - 1):
        acc = acc + kw_ref[pl.ds(k, 1), :] * _dot(shift_s[k], cur)
        top = top + kw_ref[pl.ds(k, 1), :] * head[pl.ds(8 - (kk - 1) + k, EDGE), :]
    return acc, top


def ssd_fwd(proj, dt_raw, y_mix, kw_xs, kw_bc, b_xs, b_bc, dtb, aneg, dskip, normw, cw, si, name, rider=None):
    t = proj.shape[0]
    ch = min(SCAN_CHUNK, t)
    nc = t // ch
    npair = si // LANES
    ppg = npair // GROUPS
    gn = GROUPS * STATE
    gw = si // GROUPS
    assert cw == si and (3 * cw + 2 * si) % (2 * gn) == 0
    zblk = 3 * cw // si
    xsblk = zblk + 1
    bcblk = (3 * cw + 2 * si) // (2 * gn)

    def body(z_ref, xs_ref, bc_ref, dtr_ref, ymix_ref, kwx_ref, kwb_ref, bx_ref, bb_ref, dtb_ref, aneg_ref, dsk_ref,
             nw_ref, yb_ref, ys_ref, hs_ref, xcx_ref, xcb_ref,
             headx, headb, shift_s, xs_s, bc_s, h_s, gated_s, s_s, cum_s, cumt_s):
        del ymix_ref
        c = pl.program_id(0)

        @pl.when(c == 0)
        def _():
            h_s[...] = jnp.zeros_like(h_s)
            headx[0:8, :] = jnp.zeros((8, si), F32)
            headb[0:8, :] = jnp.zeros((8, 2 * gn), F32)
            _shift_matrices(shift_s, ch, SSD_K, up=False)

        for raw_ref, head, kw_ref, b_ref, pre_ref, act_s in ((xs_ref, headx, kwx_ref, bx_ref, xcx_ref, xs_s),
                                                           (bc_ref, headb, kwb_ref, bb_ref, xcb_ref, bc_s)):
            head[8:8 + EDGE, :] = raw_ref[0:EDGE, :].astype(F32)
            pre, top = _causal_conv(raw_ref[...], head, kw_ref, b_ref, shift_s, SSD_K)
            head[0:8, :] = raw_ref[ch - EDGE:ch, :].astype(F32)[EDGE - 8:EDGE]
            pre_ref[...] = pre.astype(BF16)
            pre_ref[0:EDGE, :] = top.astype(BF16)
            act_s[...] = (pre * _sigmoid(pre)).astype(act_s.dtype)
            act_s[0:EDGE, :] = (top * _sigmoid(top)).astype(act_s.dtype)

        dt, cum, tril = _ssd_common(dtr_ref, dtb_ref, aneg_ref, cum_s, cumt_s, ch)
        lane = lax.broadcasted_iota(jnp.int32, (ch, LANES), 1)
        lane1 = lax.broadcasted_iota(jnp.int32, (1, LANES), 1)
        low = lane < HEAD_DIM
        clast = cum_s[pl.ds(ch - 1, 1), :]

        for p in range(npair):
            g = p // ppg
            col = slice(p * LANES, (p + 1) * LANES)
            bg = bc_s[:, g * STATE:(g + 1) * STATE]
            cg = bc_s[:, gn + g * STATE:gn + (g + 1) * STATE]
            if p % ppg == 0:
                s_s[...] = _dot_nt(cg, bg)
            heads = (PAIR * p, PAIR * p + 1)
            ccol = [_head_column(cum, lane, h) for h in heads]
            dcol = [_head_column(dt, lane, h) for h in heads]
            cl = [jnp.sum(jnp.where(lane1 == h, clast, 0.0), axis=-1, keepdims=True) for h in heads]
            cum_px = jnp.where(low, ccol[0], ccol[1])
            dt_px = jnp.where(low, dcol[0], dcol[1])
            cl_px = jnp.where(lane1 < HEAD_DIM, cl[0], cl[1])
            xs_p = xs_s[:, col]
            xdt = xs_p * dt_px
            y = dsk_ref[:, col] * xs_p
            for hi, h in enumerate(heads):
                dec = jnp.exp(jnp.where(tril, ccol[hi] - cumt_s[pl.ds(h, 1), :], -jnp.inf))
                wm = (s_s[...] * dec).astype(BF16)
                xm = jnp.where(low if hi == 0 else jnp.logical_not(low), xdt, 0.0).astype(BF16)
                y = y + _dot(wm, xm)
            hp = h_s[p]
            hs_ref[0, p] = hp
            y = y + _dot(cg, hp.astype(BF16)) * jnp.exp(cum_px)
            st = _dot_tn(bg, (xdt * jnp.exp(cl_px - cum_px)).astype(BF16))
            h_s[p] = jnp.exp(cl_px) * hp + st
            ys_ref[:, col] = y.astype(BF16)
            zp = z_ref[:, col].astype(F32)
            gated_s[:, col] = y * zp * _sigmoid(zp)

        for g in range(GROUPS):
            col = slice(g * gw, (g + 1) * gw)
            gg = gated_s[:, col]
            r = lax.rsqrt(jnp.mean(gg * gg, axis=-1, keepdims=True) + EPS)
            yb_ref[:, col] = (gg * r * nw_ref[:, col]).astype(BF16)

    full = lambda shape: pl.BlockSpec(shape, lambda c: tuple(0 for _ in shape))
    return _ride(
        body, (nc,),
        [pl.BlockSpec((ch, si), lambda c: (c, zblk)),
         pl.BlockSpec((ch, si), lambda c: (c, xsblk)),
         pl.BlockSpec((ch, 2 * gn), lambda c: (c, bcblk)),
         pl.BlockSpec((ch, LANES), lambda c: (c, 0)),
         pl.BlockSpec(memory_space=pl.ANY),
         full((SSD_K, si)), full((SSD_K, 2 * gn)), full((1, si)), full((1, 2 * gn)),
         full((1, LANES)), full((1, LANES)), full((1, si)), full((1, si))],
        [pl.BlockSpec((ch, si), lambda c: (c, cw // si)),
         pl.BlockSpec((ch, si), lambda c: (c, 0)),
         pl.BlockSpec((1, npair, STATE, LANES), lambda c: (c, 0, 0, 0)),
         pl.BlockSpec((ch, si), lambda c: (c, 0)), pl.BlockSpec((ch, 2 * gn), lambda c: (c, 0))],
        [jax.ShapeDtypeStruct(y_mix.shape, BF16), jax.ShapeDtypeStruct((t, si), BF16),
         jax.ShapeDtypeStruct((nc, npair, STATE, LANES), F32),
         jax.ShapeDtypeStruct((t, si), BF16), jax.ShapeDtypeStruct((t, 2 * gn), BF16)],
        [pltpu.VMEM((8 + EDGE, si), F32), pltpu.VMEM((8 + EDGE, 2 * gn), F32),
         pltpu.VMEM((SSD_K - 1, ch, ch), BF16),
         pltpu.VMEM((ch, si), F32), pltpu.VMEM((ch, 2 * gn), BF16),
         pltpu.VMEM((npair, STATE, LANES), F32), pltpu.VMEM((ch, si), F32),
         pltpu.VMEM((ch, ch), F32), pltpu.VMEM((ch, LANES), F32), pltpu.VMEM((LANES, ch), F32)],
        [proj, proj, proj, dt_raw, y_mix, kw_xs, kw_bc, b_xs, b_bc, dtb, aneg, dskip, normw], {4: 0}, rider,
        ("arbitrary",), name)


def ssd_bwd(proj, dt_raw, ys, hsave, pre_xs, pre_bc, dy, kw_xs, kw_bc, dtb, aneg, dskip, normw, cw, si, name,
            rider=None):
    t = proj.shape[0]
    ch = min(SCAN_CHUNK, t)
    nc = t // ch
    npair = si // LANES
    ppg = npair // GROUPS
    gn = GROUPS * STATE
    gw = si // GROUPS
    zblk = 3 * cw // si
    xsblk = zblk + 1
    bcblk = (3 * cw + 2 * si) // (2 * gn)

    def body(z_ref, xs_ref, bc_ref, xcx_ref, xcb_ref, dtr_ref, ys_ref, hs_ref, dyb_ref,
             kwx_ref, kwb_ref, dtb_ref, aneg_ref, dsk_ref, nw_ref,
             dp_ref, ddt_ref, dkwx_ref, dkwb_ref, dbx_ref, dbb_ref, ddtb_ref, da_ref, ddsk_ref,
             dnw_ref,
             tailx, tailb, shift_s, xs_s, bc_s, dsx_s, dsb_s, dy_s, dxs_s, dbc_s, dh_s, s_s, ds_s,
             cum_s, cumt_s, dccol_s, dcrow_s, ddtcol_s, dcl_s):
        i = pl.program_id(0)

        @pl.when(i == 0)
        def _():
            dh_s[...] = jnp.zeros_like(dh_s)
            tailx[EDGE:EDGE + 8, :] = jnp.zeros((8, si), F32)
            tailb[EDGE:EDGE + 8, :] = jnp.zeros((8, 2 * gn), F32)
            _shift_matrices(shift_s, ch, SSD_K, up=True)
            for r in (dkwx_ref, dkwb_ref, dbx_ref, dbb_ref, ddtb_ref, da_ref, ddsk_ref, dnw_ref):
                r[...] = jnp.zeros_like(r)

        xc = xcx_ref[...].astype(F32)
        sg = _sigmoid(xc)
        xs_s[...] = xc * sg
        dsx_s[...] = sg * (1.0 + xc * (1.0 - sg))
        bcc = xcb_ref[...].astype(F32)
        sgb = _sigmoid(bcc)
        bc_s[...] = (bcc * sgb).astype(BF16)
        dsb_s[...] = sgb * (1.0 + bcc * (1.0 - sgb))

        dt, cum, tril = _ssd_common(dtr_ref, dtb_ref, aneg_ref, cum_s, cumt_s, ch)
        lane = lax.broadcasted_iota(jnp.int32, (ch, LANES), 1)
        lane1 = lax.broadcasted_iota(jnp.int32, (1, LANES), 1)
        low = lane < HEAD_DIM
        low1 = lane1 < HEAD_DIM
        clast = cum_s[pl.ds(ch - 1, 1), :]

        for g in range(GROUPS):
            col = slice(g * gw, (g + 1) * gw)
            ysf = ys_ref[:, col].astype(F32)
            zf = z_ref[:, col].astype(F32)
            sz = _sigmoid(zf)
            silz = zf * sz
            gg = ysf * silz
            r = lax.rsqrt(jnp.mean(gg * gg, axis=-1, keepdims=True) + EPS)
            nrm = gg * r
            dyb = dyb_ref[:, col].astype(F32)
            dnw_ref[:, col] += jnp.sum(dyb * nrm, axis=0, keepdims=True)
            dn = dyb * nw_ref[:, col]
            dgg = r * (dn - nrm * jnp.mean(dn * nrm, axis=-1, keepdims=True))
            dy_s[:, col] = dgg * silz
            dp_ref[:, col] = (dgg * ysf * (sz * (1.0 + zf * (1.0 - sz)))).astype(BF16)

        dccol_s[...] = jnp.zeros_like(dccol_s)
        dcrow_s[...] = jnp.zeros_like(dcrow_s)
        ddtcol_s[...] = jnp.zeros_like(ddtcol_s)
        dcl_s[...] = jnp.zeros_like(dcl_s)
        dbc_s[...] = jnp.zeros_like(dbc_s)

        for p in range(npair):
            g = p // ppg
            col = slice(p * LANES, (p + 1) * LANES)
            bcol = slice(g * STATE, (g + 1) * STATE)
            ccolg = slice(gn + g * STATE, gn + (g + 1) * STATE)
            bg = bc_s[:, bcol]
            cg = bc_s[:, ccolg]
            if p % ppg == 0:
                s_s[...] = _dot_nt(cg, bg)
                ds_s[...] = jnp.zeros_like(ds_s)
            heads = (PAIR * p, PAIR * p + 1)
            masks = (low, jnp.logical_not(low))
            masks1 = (low1, jnp.logical_not(low1))
            ccol = [_head_column(cum, lane, h) for h in heads]
            dcol = [_head_column(dt, lane, h) for h in heads]
            cl = [jnp.sum(jnp.where(lane1 == h, clast, 0.0), axis=-1, keepdims=True) for h in heads]
            cum_px = jnp.where(low, ccol[0], ccol[1])
            dt_px = jnp.where(low, dcol[0], dcol[1])
            cl_px = jnp.where(low1, cl[0], cl[1])
            e_px = jnp.exp(cum_px)
            dec_end = jnp.exp(cl_px - cum_px)
            gdec = jnp.exp(cl_px)
            xs_p = xs_s[:, col]
            xdt = xs_p * dt_px
            dyp = dy_s[:, col]
            hc = hs_ref[0, p]
            hcb = hc.astype(BF16)
            dhn = dh_s[p]
            dhnb = dhn.astype(BF16)

            ddsk_ref[:, col] += jnp.sum(dyp * xs_p, axis=0, keepdims=True)
            dxs_acc = dsk_ref[:, col] * dyp
            dye = dyp * e_px
            dyeb = dye.astype(BF16)
            dbc_s[:, ccolg] += _dot_nt(dyeb, hcb)
            dcum_lane = dye * _dot(cg, hcb)
            dh_from_y = _dot_tn(cg, dyeb)
            xd = xdt * dec_end
            dxd = _dot(bg, dhnb)
            dbc_s[:, bcol] += _dot_nt(xd.astype(BF16), dhnb)
            dxdt = dxd * dec_end
            t1 = dxd * xd
            dcum_lane = dcum_lane - t1
            dcl_lane = jnp.sum(t1, axis=0, keepdims=True) + jnp.sum(dhn * hc, axis=0, keepdims=True) * gdec
            dh_s[p] = gdec * dhn + dh_from_y
            xdtb = xdt.astype(BF16)
            for hi, h in enumerate(heads):
                dym = jnp.where(masks[hi], dyp, 0.0).astype(BF16)
                dw = _dot_nt(dym, xdtb)
                dec = jnp.exp(jnp.where(tril, ccol[hi] - cumt_s[pl.ds(h, 1), :], -jnp.inf))
                wm = s_s[...] * dec
                dxdt = dxdt + _dot_tn(wm.astype(BF16), dym)
                ds_s[...] += dw * dec
                gm = dw * wm
                rowsum = jnp.sum(gm, axis=-1, keepdims=True)
                lanesum = jnp.sum(jnp.where(masks[hi], dcum_lane, 0.0), axis=-1, keepdims=True)
                dccol_s[...] += jnp.where(lane == h, rowsum + lanesum, 0.0)
                dcrow_s[pl.ds(h, 1), :] = jnp.sum(gm, axis=0, keepdims=True)
                dcl_h = jnp.sum(jnp.where(masks1[hi], dcl_lane, 0.0), axis=-1, keepdims=True)
                dcl_s[...] += jnp.where(lane1 == h, dcl_h, 0.0)
            ddt_lane = dxdt * xs_p
            for hi, h in enumerate(heads):
                s = jnp.sum(jnp.where(masks[hi], ddt_lane, 0.0), axis=-1, keepdims=True)
                ddtcol_s[...] += jnp.where(lane == h, s, 0.0)
            dxs_s[:, col] = dxs_acc + dxdt * dt_px
            if p % ppg == ppg - 1:
                dsb = ds_s[...].astype(BF16)
                dbc_s[:, ccolg] += _dot(dsb, bg)
                dbc_s[:, bcol] += _dot_tn(dsb, cg)

        rowi = lax.broadcasted_iota(jnp.int32, (ch, LANES), 0)
        dcum = dccol_s[...] - dcrow_s[...].T + jnp.where(rowi == ch - 1, dcl_s[...], 0.0)
        ri = lax.broadcasted_iota(jnp.int32, (ch, ch), 0)
        ci = lax.broadcasted_iota(jnp.int32, (ch, ch), 1)
        dal = _dot_exact((ri <= ci).astype(F32), dcum)
        ddt = dal * aneg_ref[...] + ddtcol_s[...]
        da_ref[...] += jnp.sum(dal * dt, axis=0, keepdims=True)
        ddtr = ddt * _sigmoid(dtr_ref[...] + dtb_ref[...])
        ddt_ref[...] = ddtr
        ddtb_ref[...] += jnp.sum(ddtr, axis=0, keepdims=True)

        for (dpost, dsl, tail, raw_ref, kw_ref, dkw_ref, db_ref, out) in (
                (dxs_s, dsx_s, tailx, xs_ref, kwx_ref, dkwx_ref, dbx_ref, slice(si, 2 * si)),
                (dbc_s, dsb_s, tailb, bc_ref, kwb_ref, dkwb_ref, dbb_ref, slice(2 * si, 2 * si + 2 * gn))):
            dxc = dpost[...] * dsl[...]
            dxcb = dxc.astype(BF16)
            raw = raw_ref[...].astype(F32)
            raw_end = raw_ref[ch - EDGE:ch, :].astype(F32)
            tail[0:EDGE, :] = dxcb[ch - EDGE:ch].astype(F32)
            db_ref[...] += jnp.sum(dxc, axis=0, keepdims=True)
            draw = kw_ref[pl.ds(SSD_K - 1, 1), :] * dxc
            dkw_ref[pl.ds(SSD_K - 1, 1), :] += jnp.sum(dxc * raw, axis=0, keepdims=True)
            fix = jnp.zeros((EDGE, dxc.shape[1]), F32)
            for k in range(SSD_K - 1):
                moved = _dot(shift_s[k], dxcb)
                miss = tail[pl.ds(SSD_K - 1 - k, EDGE), :] - moved[ch - EDGE:ch]
                draw = draw + kw_ref[pl.ds(k, 1), :] * moved
                fix = fix + kw_ref[pl.ds(k, 1), :] * miss
                dkw_ref[pl.ds(k, 1), :] += (jnp.sum(moved * raw, axis=0, keepdims=True)
                                            + jnp.sum(miss * raw_end, axis=0, keepdims=True))
            dp_ref[:, out] = draw.astype(BF16)
            dp_ref[ch - EDGE:ch, out] = (draw[ch - EDGE:ch] + fix).astype(BF16)
            tail[EDGE:EDGE + 8, :] = dxcb[0:EDGE].astype(F32)[0:8]

    full = lambda shape: pl.BlockSpec(shape, lambda i: tuple(0 for _ in shape))
    rev = lambda blk: (lambda i: (nc - 1 - i, blk))
    small_in = [(SSD_K, si), (SSD_K, 2 * gn), (1, LANES), (1, LANES), (1, si), (1, si)]
    small = [(SSD_K, si), (SSD_K, 2 * gn), (1, si), (1, 2 * gn), (1, LANES), (1, LANES), (1, si), (1, si)]
    return _ride(
        body, (nc,),
        [pl.BlockSpec((ch, si), rev(zblk)), pl.BlockSpec((ch, si), rev(xsblk)),
         pl.BlockSpec((ch, 2 * gn), rev(bcblk)),
         pl.BlockSpec((ch, si), rev(0)), pl.BlockSpec((ch, 2 * gn), rev(0)),
         pl.BlockSpec((ch, LANES), rev(0)), pl.BlockSpec((ch, si), rev(0)),
         pl.BlockSpec((1, npair, STATE, LANES), lambda i: (nc - 1 - i, 0, 0, 0)),
         pl.BlockSpec((ch, si), rev(cw // si))] + [full(s) for s in small_in],
        [pl.BlockSpec((ch, 2 * si + 2 * gn), rev(0)), pl.BlockSpec((ch, LANES), rev(0))] + [full(s) for s in small],
        [jax.ShapeDtypeStruct((t, 2 * si + 2 * gn), BF16), jax.ShapeDtypeStruct((t, LANES), F32)]
        + [jax.ShapeDtypeStruct(s, F32) for s in small],
        [pltpu.VMEM((EDGE + 8, si), F32), pltpu.VMEM((EDGE + 8, 2 * gn), F32),
         pltpu.VMEM((SSD_K - 1, ch, ch), BF16),
         pltpu.VMEM((ch, si), F32), pltpu.VMEM((ch, 2 * gn), BF16),
         pltpu.VMEM((ch, si), F32), pltpu.VMEM((ch, 2 * gn), F32),
         pltpu.VMEM((ch, si), F32), pltpu.VMEM((ch, si), F32), pltpu.VMEM((ch, 2 * gn), F32),
         pltpu.VMEM((npair, STATE, LANES), F32),
         pltpu.VMEM((ch, ch), F32), pltpu.VMEM((ch, ch), F32),
         pltpu.VMEM((ch, LANES), F32), pltpu.VMEM((LANES, ch), F32),
         pltpu.VMEM((ch, LANES), F32), pltpu.VMEM((LANES, ch), F32),
         pltpu.VMEM((ch, LANES), F32), pltpu.VMEM((1, LANES), F32)],
        [proj, proj, proj, pre_xs, pre_bc, dt_raw, ys, hsave, dy, kw_xs, kw_bc, dtb, aneg, dskip, normw],
        {}, rider, ("arbitrary",), name)


def final_loss(x, nw, tgt, name):
    t, d = x.shape
    tm = _tile(t, 512, 8)

    def body(x_ref, nw_ref, t_ref, dx_ref, dnw_ref, ls_ref):
        xf = x_ref[...]
        r = lax.rsqrt(jnp.mean(xf * xf, axis=-1, keepdims=True) + EPS)
        nx = xf * r
        e = nx * nw_ref[...] - t_ref[...]
        dyv = e * (1.0 / d)
        dn = dyv * nw_ref[...]
        dx_ref[...] = r * (dn - nx * jnp.mean(dn * nx, axis=-1, keepdims=True))

        @pl.when(pl.program_id(0) == 0)
        def _():
            dnw_ref[...] = jnp.zeros_like(dnw_ref)
            ls_ref[...] = jnp.zeros_like(ls_ref)

        dnw_ref[...] += jnp.sum(dyv * nx, axis=0, keepdims=True)
        ls_ref[...] += jnp.sum(e * e, axis=0, keepdims=True) * (0.5 / d)

    return pl.pallas_call(
        body, name=name, grid=(t // tm,),
        in_specs=[pl.BlockSpec((tm, d), lambda i: (i, 0)), pl.BlockSpec((1, d), lambda i: (0, 0)),
                  pl.BlockSpec((tm, d), lambda i: (i, 0))],
        out_specs=[pl.BlockSpec((tm, d), lambda i: (i, 0)), pl.BlockSpec((1, d), lambda i: (0, 0)),
                   pl.BlockSpec((1, d), lambda i: (0, 0))],
        out_shape=[jax.ShapeDtypeStruct((t, d), F32), jax.ShapeDtypeStruct((1, d), F32),
                   jax.ShapeDtypeStruct((1, d), F32)],
        compiler_params=_params(("arbitrary",)),
    )(x, nw, tgt)


def _rows3(a):
    if a.ndim == 1:
        return a.reshape(1, 1, a.shape[0])
    if a.ndim == 2:
        return a.reshape(1, *a.shape)
    return a.reshape(-1, a.shape[-2], a.shape[-1])


def adamw(w, g, m, v, name):
    shape = w.shape
    views = [_rows3(a) for a in (w, g, m, v)]
    b, r, c = views[0].shape
    tr = _tile(r, 256, 16) if r % 16 == 0 else r

    def body(w_ref, g_ref, m_ref, v_ref, d_ref, nm_ref, nv_ref):
        g = g_ref[...]
        m = ADAM_B1 * m_ref[...] + (1.0 - ADAM_B1) * g
        v = ADAM_B2 * v_ref[...] + (1.0 - ADAM_B2) * (g * g)
        m_hat = m / (1.0 - ADAM_B1 ** ADAM_STEP)
        v_hat = v / (1.0 - ADAM_B2 ** ADAM_STEP)
        d_ref[...] = -ADAM_LR * (m_hat / (jnp.sqrt(v_hat) + ADAM_EPS) + ADAM_WD * w_ref[...])
        nm_ref[...] = m
        nv_ref[...] = v

    spec = pl.BlockSpec((1, tr, c), lambda i, j: (i, j, 0))
    outs = pl.pallas_call(
        body, name=name, grid=(b, r // tr), in_specs=[spec] * 4, out_specs=[spec] * 3,
        out_shape=[jax.ShapeDtypeStruct((b, r, c), F32)] * 3,
        compiler_params=_params(("parallel", "parallel")),
    )(*views)
    return [o.reshape(shape) for o in outs]


def adamw_halves(w, g_mine, g_theirs, m, v, name):
    nl, r, c = w.shape
    r2 = r // 2
    tr = _tile(r2, 256, 16)
    nb = r2 // tr

    def body(w_ref, gm_ref, gt_ref, m_ref, v_ref, g_ref, d_ref, nm_ref, nv_ref):
        mine = (pl.program_id(1) // nb) == lax.axis_index("c")
        g = jnp.where(mine, gm_ref[...], gt_ref[...])
        m = ADAM_B1 * m_ref[...] + (1.0 - ADAM_B1) * g
        v = ADAM_B2 * v_ref[...] + (1.0 - ADAM_B2) * (g * g)
        m_hat = m / (1.0 - ADAM_B1 ** ADAM_STEP)
        v_hat = v / (1.0 - ADAM_B2 ** ADAM_STEP)
        g_ref[...] = g
        d_ref[...] = -ADAM_LR * (m_hat / (jnp.sqrt(v_hat) + ADAM_EPS) + ADAM_WD * w_ref[...])
        nm_ref[...] = m
        nv_ref[...] = v

    whole = pl.BlockSpec((1, tr, c), lambda l, i: (l, i, 0))
    half = pl.BlockSpec((1, tr, c), lambda l, i: (l, i % nb, 0))
    return pl.pallas_call(
        body, name=name, grid=(nl, 2 * nb), in_specs=[whole, half, half, whole, whole], out_specs=[whole] * 4,
        out_shape=[jax.ShapeDtypeStruct((nl, r, c), F32)] * 4,
        compiler_params=_params(("parallel", "parallel")),
    )(w, g_mine, g_theirs, m, v)


def _coords():
    return lax.axis_index("x"), lax.axis_index("y"), lax.axis_index("c")


def _ici_peers(x, y):
    chips = [(1 - x, y), (x, 1 - y), (1 - x, 1 - y)]
    return chips, [2 * cx + cy for cx, cy in chips]


def _place(ref, how, chip, layers, per):
    if how == "lead":
        return ref.at[chip, layers]
    start = pl.multiple_of(chip * per, per)
    if how == "rows":
        return ref.at[layers, pl.ds(start, per), :]
    return ref.at[layers, :, pl.ds(start, per)]


def gather_weights(shards, hows, name):
    na = len(shards)
    out_shape = []
    for s, how in zip(shards, hows):
        assert s.shape[0] % 2 == 0
        if how == "lead":
            shp = (N_CHIPS, *s.shape)
        elif how == "rows":
            shp = (s.shape[0], N_CHIPS * s.shape[1], s.shape[2])
        else:
            shp = (s.shape[0], s.shape[1], N_CHIPS * s.shape[2])
        out_shape.append(jax.ShapeDtypeStruct(shp, s.dtype))

    def body(*refs):
        ins = refs[:na]
        outs = refs[na:2 * na]
        send_sems, recv_sems = refs[2 * na:]
        x, y, c = _coords()
        me = 2 * x + y
        chips, chip_ids = _ici_peers(x, y)
        sibling = (x, y, 1 - c)

        def dst(a, chip, layers):
            per = {"lead": 0, "rows": ins[a].shape[1], "cols": ins[a].shape[-1]}[hows[a]]
            return _place(outs[a], hows[a], chip, layers, per)

        def copy(a, k, src, dst_ref, to):
            return pltpu.make_async_remote_copy(
                src_ref=src, dst_ref=dst_ref, send_sem=send_sems.at[7 * a + k], recv_sem=recv_sems.at[7 * a + k],
                device_id=to, device_id_type=MESH)

        started = []
        halves = []
        for a in range(na):
            nl = ins[a].shape[0]
            hl = nl // 2
            mine = pl.ds(c * hl, hl)
            theirs = pl.ds((1 - c) * hl, hl)
            halves.append((mine, theirs))
            for k in range(3):
                cp = copy(a, k, ins[a].at[mine], dst(a, me, mine), (*chips[k], c))
                cp.start()
                started.append(cp)
            own = copy(a, 6, ins[a], dst(a, me, pl.ds(0, nl)), sibling)
            own.start()
            started.append(own)
        for a in range(na):
            mine, _ = halves[a]
            for k in range(3):
                landed = dst(a, chip_ids[k], mine)
                copy(a, k, landed, landed, (*chips[k], c)).wait_recv()
                fw = copy(a, 3 + k, landed, landed, sibling)
                fw.start()
                started.append(fw)
        for a in range(na):
            _, theirs = halves[a]
            for k in range(3):
                got = dst(a, chip_ids[k], theirs)
                copy(a, 3 + k, got, got, sibling).wait_recv()
            whole = dst(a, me, pl.ds(0, ins[a].shape[0]))
            copy(a, 6, whole, whole, sibling).wait_recv()
        for cp in started:
            cp.wait_send()

    return pl.pallas_call(
        body, name=name, in_specs=_any_specs(na), out_specs=_any_specs(na), out_shape=out_shape,
        scratch_shapes=[pltpu.SemaphoreType.DMA((7 * na,)), pltpu.SemaphoreType.DMA((7 * na,))],
        compiler_params=pltpu.CompilerParams(has_side_effects=True),
    )(*shards)


def _remote(src, dst, send_sems, recv_sems, k, to):
    return pltpu.make_async_remote_copy(src_ref=src, dst_ref=dst, send_sem=send_sems.at[k], recv_sem=recv_sems.at[k],
                                        device_id=to, device_id_type=MESH)


LAYER_HOW = ("lead", "rows", "cols", "rows")


def _layer_place(ref, how, chip, shard_shape, start, size):
    r, c = shard_shape
    if how == "lead":
        return ref.at[chip, :, pl.ds(start, size), :]
    if how == "rows":
        return ref.at[:, pl.ds(pl.multiple_of(chip * r + start, HALO), size), :]
    return ref.at[:, pl.ds(start, size), pl.ds(pl.multiple_of(chip * c, LANES), c)]


def weight_rider_ici(shards, hows, layer):
    shapes = [tuple(s.shape[1:]) for s in shards]
    out_shapes = []
    for (r, c), how, s in zip(shapes, hows, shards):
        shp = {"lead": (N_CHIPS, 1, r, c), "rows": (1, N_CHIPS * r, c), "cols": (1, r, N_CHIPS * c)}[how]
        out_shapes.append(jax.ShapeDtypeStruct(shp, s.dtype))

    def copies(ins, outs, send_sems, recv_sems):
        x, y, c = _coords()
        me = 2 * x + y
        chips, chip_ids = _ici_peers(x, y)
        sibling = (x, y, 1 - c)
        pairs = []
        for a, (shape, how) in enumerate(zip(shapes, hows)):
            half = shape[0] // 2
            mine = pl.multiple_of(c * half, HALO)
            src = ins[a].at[pl.ds(layer, 1)]
            for k in range(3):
                to = (*chips[k], c)
                land = _layer_place(outs[a], how, chip_ids[k], shape, mine, half)
                pairs.append((_remote(src.at[:, pl.ds(mine, half), :], _layer_place(outs[a], how, me, shape, mine, half),
                                      send_sems, recv_sems, 4 * a + k, to),
                              _remote(land, land, send_sems, recv_sems, 4 * a + k, to)))
            whole = _layer_place(outs[a], how, me, shape, 0, shape[0])
            pairs.append((_remote(src, whole, send_sems, recv_sems, 4 * a + 3, sibling),
                          _remote(whole, whole, send_sems, recv_sems, 4 * a + 3, sibling)))
        return pairs

    return Rider(list(shards), out_shapes, {}, 4 * len(shards), copies)


def weight_rider_d2d(bufs, shapes, hows):
    def copies(ins, outs, send_sems, recv_sems):
        x, y, c = _coords()
        _, chip_ids = _ici_peers(x, y)
        sibling = (x, y, 1 - c)
        pairs = []
        for a, (shape, how) in enumerate(zip(shapes, hows)):
            half = shape[0] // 2
            mine = pl.multiple_of(c * half, HALO)
            theirs = pl.multiple_of((1 - c) * half, HALO)
            for k in range(3):
                land = _layer_place(outs[a], how, chip_ids[k], shape, theirs, half)
                pairs.append((_remote(_layer_place(ins[a], how, chip_ids[k], shape, mine, half),
                                      _layer_place(outs[a], how, chip_ids[k], shape, mine, half),
                                      send_sems, recv_sems, 3 * a + k, sibling),
                              _remote(land, land, send_sems, recv_sems, 3 * a + k, sibling)))
        return pairs

    return Rider(list(bufs), [jax.ShapeDtypeStruct(b.shape, b.dtype) for b in bufs],
                 {a: a for a in range(len(bufs))}, 3 * len(bufs), copies)


def grads_rider_sibling(arrs):
    def copies(ins, outs, send_sems, recv_sems):
        x, y, c = _coords()
        sibling = (x, y, 1 - c)
        pairs = []
        for a in range(len(arrs)):
            r2 = ins[a].shape[1] // 2
            src = ins[a].at[:, pl.ds(pl.multiple_of((1 - c) * r2, 8), r2), :]
            pairs.append((_remote(src, outs[a], send_sems, recv_sems, a, sibling),
                          _remote(outs[a], outs[a], send_sems, recv_sems, a, sibling)))
        return pairs

    return Rider(list(arrs), [jax.ShapeDtypeStruct((a.shape[0], a.shape[1] // 2, a.shape[2]), a.dtype) for a in arrs],
                 {}, len(arrs), copies)


def chip_sum(g, recv, name):
    nch, r, c = g.shape
    r2 = r // 2
    tr = _tile(r2, 256, 16)
    nb = r2 // tr

    def body(g0_ref, g1_ref, r_ref, o32_ref, o16_ref):
        s = jnp.where(lax.axis_index("c") == 0, g0_ref[...], g1_ref[...]) + r_ref[...]
        o32_ref[...] = s
        o16_ref[...] = s.astype(BF16)

    here = pl.BlockSpec((1, tr, c), lambda i, j: (i, j, 0))
    return pl.pallas_call(
        body, name=name, grid=(nch, nb),
        in_specs=[here, pl.BlockSpec((1, tr, c), lambda i, j: (i, nb + j, 0)), here],
        out_specs=[here, here],
        out_shape=[jax.ShapeDtypeStruct((nch, r2, c), F32), jax.ShapeDtypeStruct((nch, r2, c), BF16)],
        compiler_params=_params(("parallel", "parallel")),
    )(g, g, recv)


def grads_rider_chips(arrs):
    def copies(ins, outs, send_sems, recv_sems):
        x, y, c = _coords()
        chips, chip_ids = _ici_peers(x, y)
        pairs = []
        for a in range(len(arrs)):
            for k in range(3):
                to = (*chips[k], c)
                pairs.append((_remote(ins[a].at[chip_ids[k]], outs[a].at[k], send_sems, recv_sems, 3 * a + k, to),
                              _remote(outs[a].at[k], outs[a].at[k], send_sems, recv_sems, 3 * a + k, to)))
        return pairs

    return Rider(list(arrs), [jax.ShapeDtypeStruct((3, *a.shape[1:]), a.dtype) for a in arrs], {}, 3 * len(arrs),
                 copies)


def grad_sum(p32, recv, layer, nl, buf, name):
    _, r2, c = p32.shape
    tr = _tile(r2, 256, 16)
    nb = r2 // tr

    def body(p0_ref, p1_ref, p2_ref, p3_ref, r0_ref, r1_ref, r2_ref, *rest):
        o_ref = rest[-1]
        x, y, _ = _coords()
        chip = 2 * x + y
        own = jnp.where(chip == 0, p0_ref[...], jnp.where(chip == 1, p1_ref[...],
                                                        jnp.where(chip == 2, p2_ref[...], p3_ref[...])))
        o_ref[...] = own + r0_ref[...].astype(F32) + r1_ref[...].astype(F32) + r2_ref[...].astype(F32)

    slot = lambda k: pl.BlockSpec((1, tr, c), lambda j: (k, j, 0))
    in_specs = [slot(k) for k in range(N_CHIPS)] + [slot(k) for k in range(3)]
    args = [p32] * N_CHIPS + [recv] * 3
    aliases = {}
    if buf is not None:
        in_specs.append(pl.BlockSpec(memory_space=pl.ANY))
        args.append(buf)
        aliases = {len(args) - 1: 0}
    return pl.pallas_call(
        body, name=name, grid=(nb,), in_specs=in_specs,
        out_specs=pl.BlockSpec((1, tr, c), lambda j: (layer, j, 0)),
        out_shape=jax.ShapeDtypeStruct((nl, r2, c), F32),
        input_output_aliases=aliases,
        compiler_params=_params(("parallel",)),
    )(*args)


def grads_rider_exchange(bufs):
    def copies(ins, outs, send_sems, recv_sems):
        x, y, c = _coords()
        sibling = (x, y, 1 - c)
        return [(_remote(ins[a], outs[a], send_sems, recv_sems, a, sibling),
                 _remote(outs[a], outs[a], send_sems, recv_sems, a, sibling)) for a in range(len(bufs))]

    return Rider(list(bufs), [jax.ShapeDtypeStruct(b.shape, b.dtype) for b in bufs], {}, len(bufs), copies)


def allreduce_small(buf, name):
    r, cdim = buf.shape

    def body(x_ref, o_ref, gath, send_sems, recv_sems):
        x, y, c = _coords()
        me, sibling = (x, y, c), (x, y, 1 - c)
        chips, _ = _ici_peers(x, y)

        def slot(px, py, pc):
            return gath.at[4 * px + 2 * py + pc]

        def copy(k, block, to, src=None):
            return pltpu.make_async_remote_copy(
                src_ref=slot(*block) if src is None else src, dst_ref=slot(*block),
                send_sem=send_sems.at[k], recv_sem=recv_sems.at[k], device_id=to, device_id_type=MESH)

        gath[4 * x + 2 * y + c] = x_ref[...]
        first = [copy(0, me, sibling, src=x_ref)]
        first += [copy(1 + j, me, (*chip, c), src=x_ref) for j, chip in enumerate(chips)]
        for cp in first:
            cp.start()
        passed = [copy(4 + j, (*chip, c), sibling) for j, chip in enumerate(chips)]
        for j, chip in enumerate(chips):
            copy(1 + j, (*chip, c), me).wait_recv()
            passed[j].start()
        copy(0, sibling, me).wait_recv()
        for j, chip in enumerate(chips):
            copy(4 + j, (*chip, 1 - c), me).wait_recv()
        for cp in first + passed:
            cp.wait_send()
        acc = gath[0]
        for d in range(1, 8):
            acc = acc + gath[d]
        o_ref[...] = acc

    return pl.pallas_call(
        body, name=name,
        in_specs=[pl.BlockSpec(memory_space=pltpu.VMEM)], out_specs=pl.BlockSpec(memory_space=pltpu.VMEM),
        out_shape=jax.ShapeDtypeStruct((r, cdim), F32),
        scratch_shapes=[pltpu.VMEM((8, r, cdim), F32), pltpu.SemaphoreType.DMA((7,)), pltpu.SemaphoreType.DMA((7,))],
        compiler_params=pltpu.CompilerParams(has_side_effects=True),
    )(buf)


def _expand_heads(v):
    return jnp.repeat(v.astype(F32), HEAD_DIM).reshape(1, -1)


def _pad_lanes(v):
    return jnp.pad(v.astype(F32), (0, LANES - v.shape[0])).reshape(1, LANES)


def local_step(x, tgt, p, comm, cols):
    nl = p["norm_mix_w"].shape[0]
    d = x.shape[1]
    cw = p["short_conv_w"].shape[2]
    si = p["ssd_norm_w"].shape[1]
    ff, npad = comm.ff, comm.npad
    nh = si // HEAD_DIM
    gn = GROUPS * STATE
    dt_off = 3 * cw + si + si + 2 * gn
    assert cols == dt_off + nh and nh <= LANES and dt_off % LANES == 0 and npad == dt_off + LANES
    pieces = [(0, cw), (cw, cw), (2 * cw, cw), (3 * cw, si), (3 * cw + si, si), (3 * cw + 2 * si, 2 * gn),
              (dt_off, LANES)]

    saved = []
    for l in range(nl):
        nw1 = p["norm_mix_w"][l].reshape(1, d)
        nw2 = p["norm_mlp_w"][l].reshape(1, d)
        kw3 = p["short_conv_w"][l]
        kwx, kwb = p["ssd_conv_w"][l][:, :si], p["ssd_conv_w"][l][:, si:]
        bx, bb = p["ssd_conv_b"][l][:si].reshape(1, si), p["ssd_conv_b"][l][si:].reshape(1, 2 * gn)
        dtb = _pad_lanes(p["dt_bias"][l])
        aneg = _pad_lanes(-jnp.exp(p["a_log"][l]))
        dsk = _expand_heads(p["d_skip"][l])
        snw = p["ssd_norm_w"][l].reshape(1, si)
        ssd_args = (kwx, kwb, bx, bb, dtb, aneg, dsk, snw)

        w_in = comm.weight(l, "w_in")
        (proj, h, dt_raw), sent = norm_matmul(x, nw1, w_in, 0, dt_off, BF16, "in_proj", tail_block=dt_off // LANES,
                                              rider=comm.rider("in_proj", l))
        comm.done("in_proj", l, sent)
        y_mix = conv_mixer_fwd(proj, kw3, cw, cw + si, "conv_mixer_fwd")
        (y_mix, *ssd_saved), sent = ssd_fwd(proj, dt_raw, y_mix, *ssd_args, cw, si, "ssd_fwd",
                                            rider=comm.rider("ssd_fwd", l))
        comm.done("ssd_fwd", l, sent)
        x2, _ = matmul(y_mix, comm.weight(l, "w_out"), 0, False, d, F32, "out_proj", residual=x)
        (up, h2), sent = norm_matmul(x2, nw2, comm.weight(l, "w_up"), 0, ff, BF16, "up_proj",
                                     rider=comm.rider("up_proj", l))
        comm.done("up_proj", l, sent)
        x3, sent = matmul(up, comm.weight(l, "w_down"), 0, False, d, F32, "down_proj", lhs_fn=_relu2, residual=x2,
                          rider=comm.rider("down_proj", l))
        comm.done("down_proj", l, sent)
        saved.append((x, h, proj, dt_raw, y_mix, ssd_saved, x2, h2, up, nw1, nw2, kw3, ssd_args))
        x = x3

    dx, dwf, lvec = final_loss(x, p["final_norm_w"].reshape(1, d), tgt, "final_loss")
    loss = jnp.sum(lvec)

    names = ("norm_mix_w", "short_conv_w", "ssd_conv_w", "ssd_conv_b", "dt_bias", "a_log", "d_skip",
             "ssd_norm_w", "norm_mlp_w")
    grads = {k: [None] * nl for k in names}
    for l in reversed(range(nl)):
        x0, h, proj, dt_raw, y_mix, ssd_saved, x2, h2, up, nw1, nw2, kw3, ssd_args = saved[l]
        kwx, kwb, _, _, dtb, aneg, dsk, snw = ssd_args
        dup, sent = matmul(dx, comm.weight(l, "w_down"), 0, True, ff, BF16, "down_bwd", relu_gate=up,
                           rider=comm.rider("down_bwd", l))
        comm.done("down_bwd", l, sent)
        g_down, sums = matmul_tn(up, dx, "down_wgrad", a_fn=_relu2, chip_sums=comm.side("down_wgrad", l))
        comm.side_done("down_wgrad", l, sums)
        comm.take_gradient(l, "w_down", g_down)
        (dx2, dnw2), _ = matmul_normbwd([(dup, 0)], [(0, ff)], comm.weight(l, "w_up"), 0, x2, nw2, dx, "up_bwd")
        comm.take_gradient(l, "w_up", matmul_tn(h2, dup, "up_wgrad", by_chip=True))
        grads["norm_mlp_w"][l] = dnw2.reshape(d)
        dy, sent = matmul(dx2, comm.weight(l, "w_out"), 0, True, cw + si, BF16, "out_bwd",
                          rider=comm.rider("out_bwd", l))
        comm.done("out_bwd", l, sent)
        g_out, sums = matmul_tn(y_mix, dx2, "out_wgrad", chip_sums=comm.side("out_wgrad", l))
        comm.side_done("out_wgrad", l, sums)
        comm.take_gradient(l, "w_out", g_out)
        du, dkw3 = conv_mixer_bwd(proj, dy, kw3, cw, "conv_mixer_bwd")
        (dssd, ddt, dkwx, dkwb, dbx, dbb, ddtb, da, ddsk, dsnw), sent = ssd_bwd(
            proj, dt_raw, *ssd_saved, dy, kwx, kwb, dtb, aneg, dsk, snw, cw, si, "ssd_bwd",
            rider=comm.rider("ssd_bwd", l))
        comm.done("ssd_bwd", l, sent)
        views = [(du, 0), (du, 1), (du, 2), (dssd, 0), (dssd, 1), (dssd, 2 * si // (2 * gn)), (ddt, 0)]
        (dxl, dnw1), sent = matmul_normbwd(views, pieces, comm.weight(l, "w_in"), 0, x0, nw1, dx2, "in_bwd",
                                           rider=comm.rider("in_bwd", l))
        comm.done("in_bwd", l, sent)
        parts = []
        for i, dp in enumerate((du, dssd, ddt)):
            part, bufs = matmul_tn(h, dp, "in_wgrad_%d" % i, grad_sums=comm.reductions("in_wgrad_%d" % i))
            comm.reductions_done("in_wgrad_%d" % i, bufs)
            parts.append(part)
        comm.take_gradient(l, "w_in", split_to_chips(parts, cols, "in_wgrad_split"))
        grads["norm_mix_w"][l] = dnw1.reshape(d)
        grads["short_conv_w"][l] = dkw3
        grads["ssd_conv_w"][l] = jnp.concatenate([dkwx, dkwb], axis=1)
        grads["ssd_conv_b"][l] = jnp.concatenate([dbx, dbb], axis=1).reshape(-1)
        grads["dt_bias"][l] = ddtb[0, :nh]
        grads["a_log"][l] = da[0, :nh] * aneg[0, :nh]
        grads["d_skip"][l] = jnp.sum(ddsk.reshape(nh, HEAD_DIM), axis=1)
        grads["ssd_norm_w"][l] = dsnw.reshape(si)
        dx = dxl

    grads = {k: jnp.stack(v) for k, v in grads.items()}
    grads["final_norm_w"] = dwf.reshape(d)
    return loss, dx, grads


BIG = ("w_in", "w_out", "w_up", "w_down")
SMALL_SHARDED = ("short_conv_w", "ssd_conv_w")
SMALL_REPL = ("norm_mix_w", "ssd_conv_b", "dt_bias", "a_log", "d_skip", "ssd_norm_w", "norm_mlp_w", "final_norm_w")
WEIGHTS = ("norm_mix_w", "w_in", "short_conv_w", "ssd_conv_w", "ssd_conv_b", "dt_bias", "a_log", "d_skip",
           "ssd_norm_w", "w_out", "norm_mlp_w", "w_up", "w_down", "final_norm_w")
SMALL_COLS = 1024


def _pack_small(named):
    flat = jnp.concatenate([v.reshape(-1).astype(F32) for v in named])
    n = flat.shape[0]
    rows = -(-n // SMALL_COLS)
    rows = -(-rows // 8) * 8
    return jnp.pad(flat, (0, rows * SMALL_COLS - n)).reshape(rows, SMALL_COLS)


def _unpack_small(buf, like):
    flat = buf.reshape(-1)
    out, off = [], 0
    for v in like:
        out.append(flat[off:off + v.size].reshape(v.shape))
        off += v.size
    return out


class ChipComm:
    IO = ("w_in", "w_out")
    MLP = ("w_up", "w_down")
    FIRST = ("w_in",)
    EARLY = ("w_up", "w_down", "w_out")

    def __init__(self, shards, nl, npad):
        self.shards, self.nl, self.npad = shards, nl, npad
        self.how = dict(zip(BIG, LAYER_HOW))
        self.ff = N_CHIPS * shards["w_up"].shape[2]
        self.w = {}
        self.landed = {}
        self.grad = {}
        self.sums = {}
        self.from_sibling = {}
        self.from_chips = {}
        self.bufs = {k: None for k in BIG}
        first = run_rider(self._ici(self.FIRST, 0), "gather_first_ici")
        self._gathered(self.FIRST, 0, run_rider(self._d2d(self.FIRST, first), "gather_first_d2d"))

    def _ici(self, group, l):
        return weight_rider_ici([self.shards[k] for k in group], [self.how[k] for k in group], l)

    def _d2d(self, group, landed):
        return weight_rider_d2d(landed, [tuple(self.shards[k].shape[1:]) for k in group],
                                [self.how[k] for k in group])

    def _gathered(self, group, l, arrays):
        for k, g in zip(group, arrays):
            self.w[(l, k)] = join_from_chips(g, self.npad, "w_in_join") if k == "w_in" else g

    def _to_sibling(self, group, l):
        return grads_rider_sibling([self.grad[(l, k)] for k in group])

    def _summed(self, group, l, from_sibling):
        self.sums[group] = (l, [chip_sum(self.grad.pop((l, k)), r, "chip_sum") for k, r in zip(group, from_sibling)])

    def side(self, point, l):
        group = self.IO if point == "down_wgrad" else self.MLP
        if group not in self.from_sibling:
            return ()
        layer, parts = self.from_sibling[group]
        return [(self.grad[(layer, k)], r) for k, r in zip(group, parts)]

    def side_done(self, point, l, sums):
        group = self.IO if point == "down_wgrad" else self.MLP
        if sums:
            layer, _ = self.from_sibling.pop(group)
            for k in group:
                del self.grad[(layer, k)]
            self.sums[group] = (layer, sums)

    def _to_chips(self, groups):
        return grads_rider_chips([s[1] for group in groups for s in self.sums[group][1]])

    def _reduced(self, group, from_chips):
        l, sums = self.sums.pop(group)
        for k, s, r in zip(group, sums, from_chips):
            self.bufs[k] = grad_sum(s[0], r, l, self.nl, self.bufs[k], "grad_sum")

    def weight(self, l, name):
        return self.w[(l, name)]

    def take_gradient(self, l, name, g):
        self.grad[(l, name)] = g if g.ndim == 3 else g.reshape(N_CHIPS, g.shape[0] // N_CHIPS, g.shape[1])

    def rider(self, point, l):
        more = l + 1 < self.nl
        if point == "in_proj":
            return self._ici(self.EARLY if l == 0 else self.MLP, l)
        if point == "ssd_fwd":
            return self._d2d(*self.landed["own layer"])
        if point == "up_proj":
            return self._ici(self.IO, l + 1) if more else None
        if point == "down_proj":
            return self._d2d(self.IO, self.landed[self.IO]) if more else None
        if point == "down_bwd":
            return self._to_sibling(self.IO, l + 1) if more else None
        if point == "out_bwd":
            return self._to_sibling(self.MLP, l)
        if point == "ssd_bwd":
            return self._to_chips([self.IO]) if more else None
        return self._to_chips([self.MLP])

    def done(self, point, l, results):
        if not results:
            return
        if point == "in_proj":
            self.landed["own layer"] = (self.EARLY if l == 0 else self.MLP, results)
        elif point == "up_proj":
            self.landed[self.IO] = results
        elif point == "ssd_fwd":
            self._gathered(self.landed.pop("own layer")[0], l, results)
        elif point == "down_proj":
            self._gathered(self.IO, l + 1, results)
        elif point == "down_bwd":
            self.from_sibling[self.IO] = (l + 1, results)
        elif point == "out_bwd":
            self.from_sibling[self.MLP] = (l, results)
        elif point == "ssd_bwd":
            self.from_chips[self.IO] = results
        else:
            self.from_chips[self.MLP] = results

    def _pending_groups(self, point):
        groups = [self.IO] if point == "in_wgrad_0" else [self.MLP] if point == "in_wgrad_1" else []
        return [g for g in groups if g in self.from_chips]

    def reductions(self, point):
        items = []
        for group in self._pending_groups(point):
            layer, sums = self.sums[group]
            items += [(s[0], r, layer, self.nl, self.bufs[k]) for k, s, r in zip(group, sums, self.from_chips[group])]
        return items

    def reductions_done(self, point, bufs):
        if bufs:
            names = [k for group in self._pending_groups(point) for k in group]
            for group in self._pending_groups(point):
                del self.sums[group], self.from_chips[group]
            self.bufs.update(zip(names, bufs))

    def finish(self):
        self._summed(self.IO, 0, run_rider(self._to_sibling(self.IO, 0), "grads_to_sibling"))
        self._reduced(self.IO, run_rider(self._to_chips([self.IO]), "grads_to_chips"))
        mine = [self.bufs[k] for k in BIG]
        theirs = run_rider(grads_rider_exchange(mine), "grads_exchange")
        return dict(zip(BIG, zip(mine, theirs)))


def kernel(x, norm_mix_w, w_in, short_conv_w, ssd_conv_w, ssd_conv_b, dt_bias, a_log, d_skip, ssd_norm_w, w_out, norm_mlp_w, w_up, w_down, final_norm_w, loss_target, m_norm_mix_w, m_w_in, m_short_conv_w, m_ssd_conv_w, m_ssd_conv_b, m_dt_bias, m_a_log, m_d_skip, m_ssd_norm_w, m_w_out, m_norm_mlp_w, m_w_up, m_w_down, m_final_norm_w, v_norm_mix_w, v_w_in, v_short_conv_w, v_ssd_conv_w, v_ssd_conv_b, v_dt_bias, v_a_log, v_d_skip, v_ssd_norm_w, v_w_out, v_norm_mlp_w, v_w_up, v_w_down, v_final_norm_w):
    w = dict(norm_mix_w=norm_mix_w, w_in=w_in, short_conv_w=short_conv_w, ssd_conv_w=ssd_conv_w,
             ssd_conv_b=ssd_conv_b, dt_bias=dt_bias, a_log=a_log, d_skip=d_skip, ssd_norm_w=ssd_norm_w, w_out=w_out,
             norm_mlp_w=norm_mlp_w, w_up=w_up, w_down=w_down, final_norm_w=final_norm_w)
    m = dict(norm_mix_w=m_norm_mix_w, w_in=m_w_in, short_conv_w=m_short_conv_w, ssd_conv_w=m_ssd_conv_w,
             ssd_conv_b=m_ssd_conv_b, dt_bias=m_dt_bias, a_log=m_a_log, d_skip=m_d_skip, ssd_norm_w=m_ssd_norm_w,
             w_out=m_w_out, norm_mlp_w=m_norm_mlp_w, w_up=m_w_up, w_down=m_w_down, final_norm_w=m_final_norm_w)
    v = dict(norm_mix_w=v_norm_mix_w, w_in=v_w_in, short_conv_w=v_short_conv_w, ssd_conv_w=v_ssd_conv_w,
             ssd_conv_b=v_ssd_conv_b, dt_bias=v_dt_bias, a_log=v_a_log, d_skip=v_d_skip, ssd_norm_w=v_ssd_norm_w,
             w_out=v_w_out, norm_mlp_w=v_norm_mlp_w, w_up=v_w_up, w_down=v_w_down, final_norm_w=v_final_norm_w)
    xi, yi, ci = lax.axis_index("x"), lax.axis_index("y"), lax.axis_index("c")
    chip = 2 * xi + yi
    nl = w_up.shape[0]
    cols = N_CHIPS * w_in.shape[2]
    npad = cols // LANES * LANES + LANES

    full = dict(w)
    small_gathered = gather_weights([w[k] for k in SMALL_SHARDED], ["lead"] * len(SMALL_SHARDED), "gather_small")
    for k, g4 in zip(SMALL_SHARDED, small_gathered):
        full[k] = jnp.concatenate([g4[j] for j in range(N_CHIPS)], axis=2)
    comm = ChipComm({k: w[k].astype(BF16) for k in BIG}, nl, npad)

    loss, grad_x, grads = local_step(x[0], loss_target[0], full, comm, cols)
    loss = lax.psum(loss, ("x", "y", "c"))
    halves = comm.finish()
    g_shard = {}

    small_names = SMALL_REPL + SMALL_SHARDED
    small_sum = allreduce_small(_pack_small([grads[k] for k in small_names]), "allreduce_small")
    for k, g in zip(small_names, _unpack_small(small_sum, [grads[k] for k in small_names])):
        if k in SMALL_SHARDED:
            width = w[k].shape[2]
            g = lax.dynamic_slice_in_dim(g, chip * width, width, axis=2)
        g_shard[k] = g

    delta, new_m, new_v = {}, {}, {}
    for k in BIG:
        g_shard[k], delta[k], new_m[k], new_v[k] = adamw_halves(w[k], *halves[k], m[k], v[k], "adamw_%s" % k)
    packed = [_pack_small([d_[k] for k in small_names]) for d_ in (w, g_shard, m, v)]
    outs = adamw(*packed, "adamw_small")
    for d_, buf in zip((delta, new_m, new_v), outs):
        for k, val in zip(small_names, _unpack_small(buf, [w[k] for k in small_names])):
            d_[k] = val

    return (loss, grad_x[None], *[g_shard[k] for k in WEIGHTS], *[delta[k] for k in WEIGHTS],
            *[new_m[k] for k in WEIGHTS], *[new_v[k] for k in WEIGHTS])
```

```python
import functools

import jax
import jax.numpy as jnp
from jax import lax
from jax.experimental import pallas as pl
from jax.experimental.pallas import tpu as pltpu

F32 = jnp.float32
BF16 = jnp.bfloat16

EPS = 1e-5
HEAD_DIM = 64
STATE = 128
GROUPS = 2
SHORT_K = 3
SSD_K = 4
LANES = 128
PAIR = LANES // HEAD_DIM
SCAN_CHUNK = 256
HALO = 16
N_CHIPS = 4
VMEM_LIMIT = 56 * 1024 * 1024

ADAM_LR = 0.001
ADAM_B1 = 0.9
ADAM_B2 = 0.999
ADAM_EPS = 1e-08
ADAM_WD = 0.01
ADAM_STEP = 10

MESH = pl.DeviceIdType.MESH


def _params(sem):
    return pltpu.CompilerParams(dimension_semantics=sem, vmem_limit_bytes=VMEM_LIMIT)


def _tile(n, cap, quantum):
    if n <= cap:
        return n
    best = None
    for t in range(quantum, cap + 1, quantum):
        if n % t == 0:
            best = t
    assert best is not None, (n, cap, quantum)
    return best


def _dot(a, b):
    return jnp.dot(a, b, preferred_element_type=F32)


def _dot_nt(a, b):
    return lax.dot_general(a, b, (((1,), (1,)), ((), ())), preferred_element_type=F32)


def _dot_tn(a, b):
    return lax.dot_general(a, b, (((0,), (0,)), ((), ())), preferred_element_type=F32)


def _dot_exact(a, b):
    return jnp.dot(a, b, precision=lax.Precision.HIGHEST, preferred_element_type=F32)


def _sigmoid(x):
    return pl.reciprocal(1.0 + jnp.exp(-x), approx=True)


def _softplus(x):
    return jnp.maximum(x, 0.0) + jnp.log(1.0 + jnp.exp(-jnp.abs(x)))


def _relu2(v):
    return jnp.square(jnp.maximum(v, 0.0))


class Rider:
    def __init__(self, ins, out_shapes, aliases, n_sems, copies):
        self.ins, self.out_shapes, self.aliases, self.n_sems, self.copies = ins, out_shapes, aliases, n_sems, copies


def _any_specs(n):
    return [pl.BlockSpec(memory_space=pl.ANY)] * n


def _ride(body, grid, in_specs, out_specs, out_shape, scratch, args, aliases, rider, sem, name):
    n_in, n_out, n_scr = len(in_specs), len(out_specs), len(scratch)
    if rider is None:
        outs = pl.pallas_call(
            body, name=name, grid=grid, in_specs=in_specs, out_specs=out_specs, out_shape=out_shape,
            scratch_shapes=scratch, input_output_aliases=aliases, compiler_params=_params(sem))(*args)
        return list(outs), []
    ri, ro = len(rider.ins), len(rider.out_shapes)
    last = tuple(g - 1 for g in grid)

    def wrapped(*refs):
        ins = refs[:n_in]
        r_ins = refs[n_in:n_in + ri]
        outs = refs[n_in + ri:n_in + ri + n_out]
        r_outs = refs[n_in + ri + n_out:n_in + ri + n_out + ro]
        scr = refs[n_in + ri + n_out + ro:n_in + ri + n_out + ro + n_scr]
        send_sems, recv_sems = refs[-2:]
        ids = [pl.program_id(a) for a in range(len(grid))]
        at_first = functools.reduce(jnp.logical_and, [i == 0 for i in ids])
        at_last = functools.reduce(jnp.logical_and, [i == e for i, e in zip(ids, last)])

        @pl.when(at_first)
        def _():
            for cp, _ in rider.copies(r_ins, r_outs, send_sems, recv_sems):
                cp.start()

        body(*ins, *outs, *scr)

        @pl.when(at_last)
        def _():
            for cp, landed in rider.copies(r_ins, r_outs, send_sems, recv_sems):
                cp.wait_send()
                landed.wait_recv()

    all_aliases = dict(aliases)
    all_aliases.update({n_in + a: n_out + b for a, b in rider.aliases.items()})
    outs = pl.pallas_call(
        wrapped, name=name, grid=grid, in_specs=list(in_specs) + _any_specs(ri),
        out_specs=list(out_specs) + _any_specs(ro), out_shape=list(out_shape) + list(rider.out_shapes),
        scratch_shapes=list(scratch) + [pltpu.SemaphoreType.DMA((rider.n_sems,)),
                                        pltpu.SemaphoreType.DMA((rider.n_sems,))],
        input_output_aliases=all_aliases, compiler_params=_params(sem))(*args, *rider.ins)
    return list(outs[:n_out]), list(outs[n_out:])


class _SemsFrom:
    def __init__(self, sems, first):
        self.sems, self.first = sems, first

    @property
    def at(self):
        return self

    def __getitem__(self, k):
        return self.sems.at[self.first + k]


def join_riders(a, b):
    ia, oa = len(a.ins), len(a.out_shapes)

    def copies(ins, outs, send_sems, recv_sems):
        return (a.copies(ins[:ia], outs[:oa], send_sems, recv_sems)
                + b.copies(ins[ia:], outs[oa:], _SemsFrom(send_sems, a.n_sems), _SemsFrom(recv_sems, a.n_sems)))

    aliases = dict(a.aliases)
    aliases.update({ia + i: oa + o for i, o in b.aliases.items()})
    return Rider(a.ins + b.ins, a.out_shapes + b.out_shapes, aliases, a.n_sems + b.n_sems, copies)


def run_rider(rider, name):
    ri, ro = len(rider.ins), len(rider.out_shapes)

    def body(*refs):
        send_sems, recv_sems = refs[-2:]
        pairs = rider.copies(refs[:ri], refs[ri:ri + ro], send_sems, recv_sems)
        for cp, _ in pairs:
            cp.start()
        for cp, landed in pairs:
            cp.wait_send()
            landed.wait_recv()

    return list(pl.pallas_call(
        body, name=name, in_specs=_any_specs(ri), out_specs=_any_specs(ro), out_shape=list(rider.out_shapes),
        scratch_shapes=[pltpu.SemaphoreType.DMA((rider.n_sems,)), pltpu.SemaphoreType.DMA((rider.n_sems,))],
        input_output_aliases=dict(rider.aliases),
        compiler_params=pltpu.CompilerParams(has_side_effects=True))(*rider.ins))


def norm_matmul(x, nw, w, layer, n, out_dtype, name, tail_block=None, rider=None):
    t, d = x.shape
    mxu_cols = 2 * LANES
    tn = _tile(n, 1536, mxu_cols if n % mxu_cols == 0 else LANES)
    if n % mxu_cols == 0 and tn < 1024 <= n:
        tn = _tile(n, 3072, mxu_cols)
    tm = _tile(t, 512 if tn > 1536 else 1024, 8)
    nj = n // tn

    def body(x_ref, nw_ref, w_ref, *rest):
        if tail_block is None:
            o_ref, h_ref = rest
        else:
            wt_ref, o_ref, h_ref, tail_ref = rest

        @pl.when(pl.program_id(1) == 0)
        def _():
            xf = x_ref[...]
            r = lax.rsqrt(jnp.mean(xf * xf, axis=-1, keepdims=True) + EPS)
            h_ref[...] = (xf * r * nw_ref[...]).astype(BF16)

        o_ref[...] = _dot(h_ref[...], w_ref[...]).astype(out_dtype)
        if tail_block is not None:
            @pl.when(pl.program_id(1) == nj - 1)
            def _():
                tail_ref[...] = _dot(h_ref[...], wt_ref[...])

    in_specs = [pl.BlockSpec((tm, d), lambda i, j: (i, 0)), pl.BlockSpec((1, d), lambda i, j: (0, 0)),
                pl.BlockSpec((None, d, tn), lambda i, j: (layer, 0, j))]
    out_specs = [pl.BlockSpec((tm, tn), lambda i, j: (i, j)), pl.BlockSpec((tm, d), lambda i, j: (i, 0))]
    out_shape = [jax.ShapeDtypeStruct((t, n), out_dtype), jax.ShapeDtypeStruct((t, d), BF16)]
    args = [x, nw, w]
    if tail_block is not None:
        in_specs.append(pl.BlockSpec((None, d, LANES), lambda i, j: (layer, 0, tail_block)))
        out_specs.append(pl.BlockSpec((tm, LANES), lambda i, j: (i, 0)))
        out_shape.append(jax.ShapeDtypeStruct((t, LANES), F32))
        args.append(w)
    return _ride(body, (t // tm, nj), in_specs, out_specs, out_shape, [], args, {}, rider,
                 ("parallel", "arbitrary"), name)


def matmul(lhs, w, layer, transposed, n, out_dtype, name, *, lhs_fn=None, residual=None, relu_gate=None,
           rider=None):
    t, k = lhs.shape
    tm = _tile(t, 512 if k > 2048 else 1024, 8)
    tn = _tile(n, 1024, LANES)
    staged = lhs.dtype != BF16 or lhs_fn is not None
    fn = lhs_fn if lhs_fn is not None else (lambda v: v)
    has_extra = residual is not None or relu_gate is not None
    dot = _dot_nt if transposed else _dot

    def body(*refs):
        a_ref, w_ref = refs[:2]
        extra = refs[2] if has_extra else None
        o_ref = refs[3] if has_extra else refs[2]
        if staged:
            s_ref = refs[-1]

            @pl.when(pl.program_id(1) == 0)
            def _():
                s_ref[...] = fn(a_ref[...].astype(F32)).astype(BF16)

            a_ref = s_ref
        acc = dot(a_ref[...], w_ref[...])
        if residual is not None:
            acc = acc + extra[...]
        if relu_gate is not None:
            acc = acc * (2.0 * jnp.maximum(extra[...].astype(F32), 0.0))
        o_ref[...] = acc.astype(out_dtype)

    if transposed:
        w_spec = pl.BlockSpec((None, tn, k), lambda i, j: (layer, j, 0))
    else:
        w_spec = pl.BlockSpec((None, k, tn), lambda i, j: (layer, 0, j))
    in_specs = [pl.BlockSpec((tm, k), lambda i, j: (i, 0)), w_spec]
    args = [lhs, w]
    if has_extra:
        in_specs.append(pl.BlockSpec((tm, tn), lambda i, j: (i, j)))
        args.append(residual if residual is not None else relu_gate)
    outs, extra = _ride(
        body, (t // tm, n // tn), in_specs, [pl.BlockSpec((tm, tn), lambda i, j: (i, j))],
        [jax.ShapeDtypeStruct((t, n), out_dtype)], [pltpu.VMEM((tm, k), BF16)] if staged else [], args, {},
        rider, ("parallel", "arbitrary"), name)
    return outs[0], extra


def matmul_normbwd(lhs, pieces, w, layer, x, nw, dres, name, rider=None):
    t, d = x.shape
    nl = len(lhs)
    tm = _tile(t, 512, 8)
    for off, width in pieces:
        assert off % width == 0

    def body(*refs):
        lrefs = refs[:nl]
        wrefs = refs[nl:2 * nl]
        x_ref, nw_ref, dres_ref, dx_ref, dnw_ref = refs[2 * nl:]
        dh = _dot_nt(lrefs[0][...].astype(BF16), wrefs[0][...])
        for a_ref, w_ref in zip(lrefs[1:], wrefs[1:]):
            dh = dh + _dot_nt(a_ref[...].astype(BF16), w_ref[...])
        xf = x_ref[...]
        r = lax.rsqrt(jnp.mean(xf * xf, axis=-1, keepdims=True) + EPS)
        nx = xf * r
        dn = dh * nw_ref[...]
        dx = r * (dn - nx * jnp.mean(dn * nx, axis=-1, keepdims=True))
        dx_ref[...] = dres_ref[...] + dx

        @pl.when(pl.program_id(0) == 0)
        def _():
            dnw_ref[...] = jnp.zeros_like(dnw_ref)

        dnw_ref[...] += jnp.sum(dh * nx, axis=0, keepdims=True)

    in_specs = [pl.BlockSpec((tm, width), (lambda blk: (lambda i: (i, blk)))(blk))
                for (_, blk), (_, width) in zip(lhs, pieces)]
    in_specs += [pl.BlockSpec((None, d, width), (lambda blk: (lambda i: (layer, 0, blk)))(off // width),
                              pipeline_mode=pl.Buffered(1))
                 for off, width in pieces]
    in_specs += [pl.BlockSpec((tm, d), lambda i: (i, 0)), pl.BlockSpec((1, d), lambda i: (0, 0)),
                 pl.BlockSpec((tm, d), lambda i: (i, 0))]
    return _ride(
        body, (t // tm,), in_specs,
        [pl.BlockSpec((tm, d), lambda i: (i, 0)), pl.BlockSpec((1, d), lambda i: (0, 0))],
        [jax.ShapeDtypeStruct((t, d), F32), jax.ShapeDtypeStruct((1, d), F32)], [],
        [*[a for a, _ in lhs], *([w] * nl), x, nw, dres], {}, rider, ("arbitrary",), name)


def matmul_tn(a, b, name, *, a_fn=None, by_chip=False, chip_sums=None, grad_sums=None):
    t, k = a.shape
    n = b.shape[1]
    tk = _tile(k, 1024, LANES)
    nn = n // N_CHIPS if by_chip else n
    tn = _tile(nn, 1536, 2 * LANES if nn % (2 * LANES) == 0 else LANES)
    tt = _tile(t, 1024, 8)
    nt = t // tt
    gn_ = n // tn
    steps = (k // tk) * gn_ * nt
    fn = a_fn if a_fn is not None else (lambda v: v)
    with_sums, chip_sums = chip_sums, list(chip_sums or ())
    for g, _ in chip_sums:
        if (g.shape[0] * g.shape[1] // 2) % (steps * HALO) or steps % g.shape[0]:
            return (matmul_tn(a, b, name, a_fn=a_fn, by_chip=by_chip),
                    [tuple(chip_sum(g_, r_, "chip_sum")) for g_, r_ in chip_sums])
    with_grads, grad_sums = grad_sums, list(grad_sums or ())
    assert not (chip_sums and grad_sums)
    for p32, *_ in grad_sums:
        if p32.shape[1] % (steps * HALO):
            return (matmul_tn(a, b, name, a_fn=a_fn, by_chip=by_chip),
                    [grad_sum(*item, "grad_sum") for item in grad_sums])
    ns = len(chip_sums)
    ng = len(grad_sums)
    n_gin = [N_CHIPS + 3 + (item[4] is not None) for item in grad_sums]
    step = lambda i, j, s: (i * gn_ + j) * nt + s

    def body(*refs):
        a_ref, b_ref = refs[:2]
        side_in = refs[2:2 + 3 * ns]
        grad_in = refs[2 + 3 * ns:2 + 3 * ns + sum(n_gin)]
        o_ref = refs[2 + 3 * ns + sum(n_gin)]
        side_out = refs[3 + 3 * ns + sum(n_gin):3 + 5 * ns + sum(n_gin)]
        grad_out = refs[3 + 5 * ns + sum(n_gin):3 + 5 * ns + sum(n_gin) + ng]
        acc_ref = refs[-1]

        @pl.when(pl.program_id(2) == 0)
        def _():
            acc_ref[...] = jnp.zeros_like(acc_ref)

        av = a_ref[...]
        if a_fn is not None:
            av = fn(av.astype(F32))
        acc_ref[...] += _dot_tn(av.astype(BF16), b_ref[...].astype(BF16))
        for q in range(ns):
            g0_ref, g1_ref, r_ref = side_in[3 * q:3 * q + 3]
            tot = jnp.where(lax.axis_index("c") == 0, g0_ref[...], g1_ref[...]) + r_ref[...]
            side_out[2 * q][...] = tot
            side_out[2 * q + 1][...] = tot.astype(BF16)
        pos = 0
        for q in range(ng):
            p = grad_in[pos:pos + N_CHIPS]
            r = grad_in[pos + N_CHIPS:pos + N_CHIPS + 3]
            pos += n_gin[q]
            x_, y_, _ = _coords()
            chip = 2 * x_ + y_
            own = jnp.where(chip == 0, p[0][...], jnp.where(chip == 1, p[1][...],
                                                           jnp.where(chip == 2, p[2][...], p[3][...])))
            grad_out[q][...] = (own + r[0][...].astype(F32) + r[1][...].astype(F32)
                                + r[2][...].astype(F32))[None]

        @pl.when(pl.program_id(2) == nt - 1)
        def _():
            o_ref[...] = acc_ref[...]

    if by_chip:
        per = n // N_CHIPS // tn
        out_specs = [pl.BlockSpec((None, tk, tn), lambda i, j, s: (j // per, i, j % per))]
        out_shape = [jax.ShapeDtypeStruct((N_CHIPS, k, n // N_CHIPS), F32)]
    else:
        out_specs = [pl.BlockSpec((tk, tn), lambda i, j, s: (i, j))]
        out_shape = [jax.ShapeDtypeStruct((k, n), F32)]
    in_specs = [pl.BlockSpec((tt, tk), lambda i, j, s: (s, i)), pl.BlockSpec((tt, tn), lambda i, j, s: (s, j))]
    args = [a, b]
    for g, recv in chip_sums:
        nch, r, c = g.shape
        r2 = r // 2
        rows = nch * r2 // steps
        per_chip = r2 // rows
        assert rows % HALO == 0 and r2 % rows == 0
        for half in range(2):
            in_specs.append(pl.BlockSpec(
                (rows, c), (lambda h: (lambda i, j, s: ((step(i, j, s) // per_chip) * 2 * per_chip + h * per_chip
                                                        + step(i, j, s) % per_chip, 0)))(half)))
        flat = pl.BlockSpec((rows, c), lambda i, j, s: (step(i, j, s), 0))
        in_specs.append(flat)
        args += [g.reshape(nch * r, c), g.reshape(nch * r, c), recv.reshape(nch * r2, c)]
        out_specs += [flat, flat]
        out_shape += [jax.ShapeDtypeStruct((nch * r2, c), F32), jax.ShapeDtypeStruct((nch * r2, c), BF16)]
    aliases = {}
    for p32, recv, layer, nl, buf in grad_sums:
        _, r2, c = p32.shape
        rows = r2 // steps
        slab = lambda kk: pl.BlockSpec((rows, c), lambda i, j, s: (kk * steps + step(i, j, s), 0))
        in_specs += [slab(kk) for kk in range(N_CHIPS)] + [slab(kk) for kk in range(3)]
        args += [p32.reshape(N_CHIPS * r2, c)] * N_CHIPS + [recv.reshape(3 * r2, c)] * 3
        if buf is not None:
            aliases[len(args)] = len(out_specs)
            in_specs.append(pl.BlockSpec(memory_space=pl.ANY))
            args.append(buf)
        out_specs.append(pl.BlockSpec((1, rows, c), (lambda ly: (lambda i, j, s: (ly, step(i, j, s), 0)))(layer)))
        out_shape.append(jax.ShapeDtypeStruct((nl, r2, c), F32))
    outs = pl.pallas_call(
        body, name=name, grid=(k // tk, gn_, nt), in_specs=in_specs, out_specs=out_specs, out_shape=out_shape,
        scratch_shapes=[pltpu.VMEM((tk, tn), F32)], input_output_aliases=aliases,
        compiler_params=_params(("parallel", "parallel", "arbitrary")),
    )(*args)
    if with_grads is not None:
        return outs[0], list(outs[1:])
    if with_sums is None:
        return outs[0]
    sums = [(outs[1 + 2 * q].reshape(g.shape[0], g.shape[1] // 2, g.shape[2]),
             outs[2 + 2 * q].reshape(g.shape[0], g.shape[1] // 2, g.shape[2])) for q, (g, _) in enumerate(chip_sums)]
    return outs[0], sums


def split_to_chips(pieces, cols, name):
    d = pieces[0].shape[0]
    widths = [p.shape[1] for p in pieces]
    w = cols // N_CHIPS
    tr = _tile(d, 256, 8)
    npc = len(pieces)

    def body(*refs):
        o_ref, row = refs[npc], refs[npc + 1]
        off = 0
        for r, n in zip(refs[:npc], widths):
            row[:, off:off + n] = r[...]
            off += n
        for j in range(N_CHIPS):
            o_ref[j] = row[:, j * w:(j + 1) * w]

    return pl.pallas_call(
        body, name=name, grid=(d // tr,),
        in_specs=[pl.BlockSpec((tr, n), lambda i: (i, 0)) for n in widths],
        out_specs=pl.BlockSpec((N_CHIPS, tr, w), lambda i: (0, i, 0)),
        out_shape=jax.ShapeDtypeStruct((N_CHIPS, d, w), F32),
        scratch_shapes=[pltpu.VMEM((tr, sum(widths)), F32)],
        compiler_params=_params(("parallel",)),
    )(*pieces)


def join_from_chips(g4, npad, name):
    _, nl, d, w = g4.shape
    tr = _tile(d, 256, HALO)

    def body(g_ref, o_ref):
        for j in range(N_CHIPS):
            o_ref[:, j * w:(j + 1) * w] = g_ref[j]
        o_ref[:, N_CHIPS * w:] = jnp.zeros((tr, npad - N_CHIPS * w), o_ref.dtype)

    return pl.pallas_call(
        body, name=name, grid=(nl, d // tr),
        in_specs=[pl.BlockSpec((N_CHIPS, None, tr, w), lambda l, i: (0, l, i, 0))],
        out_specs=pl.BlockSpec((None, tr, npad), lambda l, i: (l, i, 0)),
        out_shape=jax.ShapeDtypeStruct((nl, d, npad), g4.dtype),
        compiler_params=_params(("parallel", "parallel")),
    )(g4)


def conv_mixer_fwd(proj, kw, cw, out_cols, name):
    t = proj.shape[0]
    tm = _tile(t, 1024, HALO)
    tc = _tile(cw, 1024, LANES)
    nj = cw // tc
    hb = tm // HALO

    def body(ub_ref, uc_ref, uh_ref, ucp_ref, uhp_ref, kw_ref, y_ref):
        i = pl.program_id(0)
        taps = [kw_ref[pl.ds(k, 1), :] for k in range(SHORT_K)]
        row = lax.broadcasted_iota(jnp.int32, (8, tc), 0)
        vp = ucp_ref[...].astype(F32) * uhp_ref[...].astype(F32)

        def conv(block, before):
            acc = taps[SHORT_K - 1] * block
            for k in range(SHORT_K - 1):
                s = SHORT_K - 1 - k
                acc = acc + taps[k] * jnp.where(row >= s, pltpu.roll(block, s, 0), pltpu.roll(before, s, 0))
            return acc

        def strip(s, before):
            rows = pl.ds(pl.multiple_of(s * HALO, HALO), HALO)
            v = uc_ref[rows, :].astype(F32) * uh_ref[rows, :].astype(F32)
            top, bottom = v[0:8], v[8:HALO]
            cv = jnp.concatenate([conv(top, before), conv(bottom, top)], axis=0)
            y_ref[rows, :] = (ub_ref[rows, :].astype(F32) * cv).astype(BF16)
            return bottom

        lax.fori_loop(0, tm // HALO, strip, jnp.where(i > 0, vp[8:HALO], 0.0))

    prev = lambda off: (lambda i, j: (jnp.maximum(i * hb - 1, 0), off + j))
    return pl.pallas_call(
        body, name=name, grid=(t // tm, nj),
        in_specs=[pl.BlockSpec((tm, tc), lambda i, j: (i, j)),
                  pl.BlockSpec((tm, tc), lambda i, j: (i, nj + j)),
                  pl.BlockSpec((tm, tc), lambda i, j: (i, 2 * nj + j)),
                  pl.BlockSpec((HALO, tc), prev(nj)),
                  pl.BlockSpec((HALO, tc), prev(2 * nj)),
                  pl.BlockSpec((SHORT_K, tc), lambda i, j: (0, j))],
        out_specs=pl.BlockSpec((tm, tc), lambda i, j: (i, j)),
        out_shape=jax.ShapeDtypeStruct((t, out_cols), BF16),
        compiler_params=_params(("parallel", "parallel")),
    )(proj, proj, proj, proj, proj, kw)


def conv_mixer_bwd(proj, dy, kw, cw, name):
    t = proj.shape[0]
    tm = _tile(t, 1024, HALO)
    tc = cw
    nj = cw // tc
    hb = tm // HALO
    ni = t // tm
    last_hb = t // HALO - 1

    def body(ub_ref, uc_ref, uh_ref, dy_ref, ucp_ref, uhp_ref, ubn_ref, dyn_ref, kw_ref,
             du_ref, dkw_ref):
        i = pl.program_id(1)
        nstrips = tm // HALO
        taps = [kw_ref[pl.ds(k, 1), :] for k in range(SHORT_K)]
        row = lax.broadcasted_iota(jnp.int32, (8, tc), 0)
        vp = ucp_ref[...].astype(F32) * uhp_ref[...].astype(F32)
        dcvn = dyn_ref[...].astype(F32) * ubn_ref[...].astype(F32)

        def shifted(block, before, s):
            return jnp.where(row >= s, pltpu.roll(block, s, 0), pltpu.roll(before, s, 0))

        def lifted(block, after, s):
            return jnp.where(row < 8 - s, pltpu.roll(block, 8 - s, 0), pltpu.roll(after, 8 - s, 0))

        def down(s, carry):
            before, sums = carry
            rows = pl.ds(pl.multiple_of(s * HALO, HALO), HALO)
            v = uc_ref[rows, :].astype(F32) * uh_ref[rows, :].astype(F32)
            dyv = dy_ref[rows, :].astype(F32)
            dcv = dyv * ub_ref[rows, :].astype(F32)
            cvs = []
            sums = list(sums)
            for block, above, dcb in ((v[0:8], before, dcv[0:8]), (v[8:HALO], v[0:8], dcv[8:HALO])):
                moved = [shifted(block, above, SHORT_K - 1 - k) for k in range(SHORT_K - 1)] + [block]
                cvs.append(sum(taps[k] * moved[k] for k in range(SHORT_K)))
                sums = [sums[k] + dcb * moved[k] for k in range(SHORT_K)]
            du_ref[rows, 0:cw] = (dyv * jnp.concatenate(cvs, axis=0)).astype(BF16)
            return v[8:HALO], tuple(sums)

        zero = jnp.zeros((8, tc), F32)
        _, sums = lax.fori_loop(0, nstrips, down, (jnp.where(i > 0, vp[8:HALO], 0.0), (zero,) * SHORT_K))

        def up(n, after):
            rows = pl.ds(pl.multiple_of((nstrips - 1 - n) * HALO, HALO), HALO)
            uc = uc_ref[rows, :].astype(F32)
            uh = uh_ref[rows, :].astype(F32)
            dcv = dy_ref[rows, :].astype(F32) * ub_ref[rows, :].astype(F32)
            dvs = []
            for block, below in ((dcv[0:8], dcv[8:HALO]), (dcv[8:HALO], after)):
                dvs.append(taps[SHORT_K - 1] * block
                           + sum(taps[k] * lifted(block, below, SHORT_K - 1 - k) for k in range(SHORT_K - 1)))
            dv = jnp.concatenate(dvs, axis=0)
            du_ref[rows, cw:2 * cw] = (dv * uh).astype(BF16)
            du_ref[rows, 2 * cw:3 * cw] = (dv * uc).astype(BF16)
            return dcv[0:8]

        lax.fori_loop(0, nstrips, up, jnp.where(i < ni - 1, dcvn[0:8], 0.0))

        @pl.when(i == 0)
        def _():
            dkw_ref[...] = jnp.zeros_like(dkw_ref)

        for k in range(SHORT_K):
            dkw_ref[pl.ds(k, 1), :] += jnp.sum(sums[k], axis=0, keepdims=True)

    prev = lambda off: (lambda j, i: (jnp.maximum(i * hb - 1, 0), off + j))
    nxt = lambda off: (lambda j, i: (jnp.minimum((i + 1) * hb, last_hb), off + j))
    cur = lambda off: (lambda j, i: (i, off + j))
    return pl.pallas_call(
        body, name=name, grid=(nj, ni),
        in_specs=[pl.BlockSpec((tm, tc), cur(0)), pl.BlockSpec((tm, tc), cur(nj)),
                  pl.BlockSpec((tm, tc), cur(2 * nj)), pl.BlockSpec((tm, tc), cur(0)),
                  pl.BlockSpec((HALO, tc), prev(nj)), pl.BlockSpec((HALO, tc), prev(2 * nj)),
                  pl.BlockSpec((HALO, tc), nxt(0)), pl.BlockSpec((HALO, tc), nxt(0)),
                  pl.BlockSpec((SHORT_K, tc), lambda j, i: (0, j))],
        out_specs=[pl.BlockSpec((tm, 3 * cw), lambda j, i: (i, 0)),
                   pl.BlockSpec((SHORT_K, tc), lambda j, i: (0, j))],
        out_shape=[jax.ShapeDtypeStruct((t, 3 * cw), BF16), jax.ShapeDtypeStruct((SHORT_K, cw), F32)],
        compiler_params=_params(("parallel", "arbitrary")),
    )(proj, proj, proj, dy, proj, proj, proj, dy, kw)


def _head_column(mat, lane, h):
    return jnp.sum(jnp.where(lane == h, mat, 0.0), axis=-1, keepdims=True)


def _ssd_common(dt_raw_ref, dtb_ref, aneg_ref, cum_s, cumt_s, chunk):
    dt = _softplus(dt_raw_ref[...] + dtb_ref[...])
    al = dt * aneg_ref[...]
    ri = lax.broadcasted_iota(jnp.int32, (chunk, chunk), 0)
    ci = lax.broadcasted_iota(jnp.int32, (chunk, chunk), 1)
    cum = _dot_exact((ri >= ci).astype(F32), al)
    cum_s[...] = cum
    cumt_s[...] = cum.T
    return dt, cum, ri >= ci


EDGE = 16


def _shift_matrices(shift_s, chunk, kk, up):
    ri = lax.broadcasted_iota(jnp.int32, (chunk, chunk), 0)
    ci = lax.broadcasted_iota(jnp.int32, (chunk, chunk), 1)
    for k in range(kk - 1):
        s = kk - 1 - k
        shift_s[k] = ((ci - ri if up else ri - ci) == s).astype(BF16)


def _causal_conv(cur, head, kw_ref, b_ref, shift_s, kk):
    acc = b_ref[...] + kw_ref[pl.ds(kk - 1, 1), :] * cur.astype(F32)
    top = b_ref[...] + kw_ref[pl.ds(kk - 1, 1), :] * head[pl.ds(8, EDGE), :]
    for k in range(kk - 1):
        acc = acc + kw_ref[pl.ds(k, 1), :] * _dot(shift_s[k], cur)
        top = top + kw_ref[pl.ds(k, 1), :] * head[pl.ds(8 - (kk - 1) + k, EDGE), :]
    return acc, top


def ssd_fwd(proj, dt_raw, y_mix, kw_xs, kw_bc, b_xs, b_bc, dtb, aneg, dskip, normw, cw, si, name, rider=None):
    t = proj.shape[0]
    ch = min(SCAN_CHUNK, t)
    nc = t // ch
    npair = si // LANES
    ppg = npair // GROUPS
    gn = GROUPS * STATE
    gw = si // GROUPS
    assert cw == si and (3 * cw + 2 * si) % (2 * gn) == 0
    zblk = 3 * cw // si
    xsblk = zblk + 1
    bcblk = (3 * cw + 2 * si) // (2 * gn)

    def body(z_ref, xs_ref, bc_ref, dtr_ref, ymix_ref, kwx_ref, kwb_ref, bx_ref, bb_ref, dtb_ref, aneg_ref, dsk_ref,
             nw_ref, yb_ref, ys_ref, hs_ref, xcx_ref, xcb_ref,
             headx, headb, shift_s, xs_s, bc_s, h_s, gated_s, s_s, cum_s, cumt_s):
        del ymix_ref
        c = pl.program_id(0)

        @pl.when(c == 0)
        def _():
            h_s[...] = jnp.zeros_like(h_s)
            headx[0:8, :] = jnp.zeros((8, si), F32)
            headb[0:8, :] = jnp.zeros((8, 2 * gn), F32)
            _shift_matrices(shift_s, ch, SSD_K, up=False)

        for raw_ref, head, kw_ref, b_ref, pre_ref, act_s in ((xs_ref, headx, kwx_ref, bx_ref, xcx_ref, xs_s),
                                                           (bc_ref, headb, kwb_ref, bb_ref, xcb_ref, bc_s)):
            head[8:8 + EDGE, :] = raw_ref[0:EDGE, :].astype(F32)
            pre, top = _causal_conv(raw_ref[...], head, kw_ref, b_ref, shift_s, SSD_K)
            head[0:8, :] = raw_ref[ch - EDGE:ch, :].astype(F32)[EDGE - 8:EDGE]
            pre_ref[...] = pre.astype(BF16)
            pre_ref[0:EDGE, :] = top.astype(BF16)
            act_s[...] = (pre * _sigmoid(pre)).astype(act_s.dtype)
            act_s[0:EDGE, :] = (top * _sigmoid(top)).astype(act_s.dtype)

        dt, cum, tril = _ssd_common(dtr_ref, dtb_ref, aneg_ref, cum_s, cumt_s, ch)
        lane = lax.broadcasted_iota(jnp.int32, (ch, LANES), 1)
        lane1 = lax.broadcasted_iota(jnp.int32, (1, LANES), 1)
        low = lane < HEAD_DIM
        clast = cum_s[pl.ds(ch - 1, 1), :]

        for p in range(npair):
            g = p // ppg
            col = slice(p * LANES, (p + 1) * LANES)
            bg = bc_s[:, g * STATE:(g + 1) * STATE]
            cg = bc_s[:, gn + g * STATE:gn + (g + 1) * STATE]
            if p % ppg == 0:
                s_s[...] = _dot_nt(cg, bg)
            heads = (PAIR * p, PAIR * p + 1)
            ccol = [_head_column(cum, lane, h) for h in heads]
            dcol = [_head_column(dt, lane, h) for h in heads]
            cl = [jnp.sum(jnp.where(lane1 == h, clast, 0.0), axis=-1, keepdims=True) for h in heads]
            cum_px = jnp.where(low, ccol[0], ccol[1])
            dt_px = jnp.where(low, dcol[0], dcol[1])
            cl_px = jnp.where(lane1 < HEAD_DIM, cl[0], cl[1])
            xs_p = xs_s[:, col]
            xdt = xs_p * dt_px
            y = dsk_ref[:, col] * xs_p
            for hi, h in enumerate(heads):
                dec = jnp.exp(jnp.where(tril, ccol[hi] - cumt_s[pl.ds(h, 1), :], -jnp.inf))
                wm = (s_s[...] * dec).astype(BF16)
                xm = jnp.where(low if hi == 0 else jnp.logical_not(low), xdt, 0.0).astype(BF16)
                y = y + _dot(wm, xm)
            hp = h_s[p]
            hs_ref[0, p] = hp
            y = y + _dot(cg, hp.astype(BF16)) * jnp.exp(cum_px)
            st = _dot_tn(bg, (xdt * jnp.exp(cl_px - cum_px)).astype(BF16))
            h_s[p] = jnp.exp(cl_px) * hp + st
            ys_ref[:, col] = y.astype(BF16)
            zp = z_ref[:, col].astype(F32)
            gated_s[:, col] = y * zp * _sigmoid(zp)

        for g in range(GROUPS):
            col = slice(g * gw, (g + 1) * gw)
            gg = gated_s[:, col]
            r = lax.rsqrt(jnp.mean(gg * gg, axis=-1, keepdims=True) + EPS)
            yb_ref[:, col] = (gg * r * nw_ref[:, col]).astype(BF16)

    full = lambda shape: pl.BlockSpec(shape, lambda c: tuple(0 for _ in shape))
    return _ride(
        body, (nc,),
        [pl.BlockSpec((ch, si), lambda c: (c, zblk)),
         pl.BlockSpec((ch, si), lambda c: (c, xsblk)),
         pl.BlockSpec((ch, 2 * gn), lambda c: (c, bcblk)),
         pl.BlockSpec((ch, LANES), lambda c: (c, 0)),
         pl.BlockSpec(memory_space=pl.ANY),
         full((SSD_K, si)), full((SSD_K, 2 * gn)), full((1, si)), full((1, 2 * gn)),
         full((1, LANES)), full((1, LANES)), full((1, si)), full((1, si))],
        [pl.BlockSpec((ch, si), lambda c: (c, cw // si)),
         pl.BlockSpec((ch, si), lambda c: (c, 0)),
         pl.BlockSpec((1, npair, STATE, LANES), lambda c: (c, 0, 0, 0)),
         pl.BlockSpec((ch, si), lambda c: (c, 0)), pl.BlockSpec((ch, 2 * gn), lambda c: (c, 0))],
        [jax.ShapeDtypeStruct(y_mix.shape, BF16), jax.ShapeDtypeStruct((t, si), BF16),
         jax.ShapeDtypeStruct((nc, npair, STATE, LANES), F32),
         jax.ShapeDtypeStruct((t, si), BF16), jax.ShapeDtypeStruct((t, 2 * gn), BF16)],
        [pltpu.VMEM((8 + EDGE, si), F32), pltpu.VMEM((8 + EDGE, 2 * gn), F32),
         pltpu.VMEM((SSD_K - 1, ch, ch), BF16),
         pltpu.VMEM((ch, si), F32), pltpu.VMEM((ch, 2 * gn), BF16),
         pltpu.VMEM((npair, STATE, LANES), F32), pltpu.VMEM((ch, si), F32),
         pltpu.VMEM((ch, ch), F32), pltpu.VMEM((ch, LANES), F32), pltpu.VMEM((LANES, ch), F32)],
        [proj, proj, proj, dt_raw, y_mix, kw_xs, kw_bc, b_xs, b_bc, dtb, aneg, dskip, normw], {4: 0}, rider,
        ("arbitrary",), name)


def ssd_bwd(proj, dt_raw, ys, hsave, pre_xs, pre_bc, dy, kw_xs, kw_bc, dtb, aneg, dskip, normw, cw, si, name,
            rider=None):
    t = proj.shape[0]
    ch = min(SCAN_CHUNK, t)
    nc = t // ch
    npair = si // LANES
    ppg = npair // GROUPS
    gn = GROUPS * STATE
    gw = si // GROUPS
    zblk = 3 * cw // si
    xsblk = zblk + 1
    bcblk = (3 * cw + 2 * si) // (2 * gn)

    def body(z_ref, xs_ref, bc_ref, xcx_ref, xcb_ref, dtr_ref, ys_ref, hs_ref, dyb_ref,
             kwx_ref, kwb_ref, dtb_ref, aneg_ref, dsk_ref, nw_ref,
             dp_ref, ddt_ref, dkwx_ref, dkwb_ref, dbx_ref, dbb_ref, ddtb_ref, da_ref, ddsk_ref,
             dnw_ref,
             tailx, tailb, shift_s, xs_s, bc_s, dsx_s, dsb_s, dy_s, dxs_s, dbc_s, dh_s, s_s, ds_s,
             cum_s, cumt_s, dccol_s, dcrow_s, ddtcol_s, dcl_s):
        i = pl.program_id(0)

        @pl.when(i == 0)
        def _():
            dh_s[...] = jnp.zeros_like(dh_s)
            tailx[EDGE:EDGE + 8, :] = jnp.zeros((8, si), F32)
            tailb[EDGE:EDGE + 8, :] = jnp.zeros((8, 2 * gn), F32)
            _shift_matrices(shift_s, ch, SSD_K, up=True)
            for r in (dkwx_ref, dkwb_ref, dbx_ref, dbb_ref, ddtb_ref, da_ref, ddsk_ref, dnw_ref):
                r[...] = jnp.zeros_like(r)

        xc = xcx_ref[...].astype(F32)
        sg = _sigmoid(xc)
        xs_s[...] = xc * sg
        dsx_s[...] = sg * (1.0 + xc * (1.0 - sg))
        bcc = xcb_ref[...].astype(F32)
        sgb = _sigmoid(bcc)
        bc_s[...] = (bcc * sgb).astype(BF16)
        dsb_s[...] = sgb * (1.0 + bcc * (1.0 - sgb))

        dt, cum, tril = _ssd_common(dtr_ref, dtb_ref, aneg_ref, cum_s, cumt_s, ch)
        lane = lax.broadcasted_iota(jnp.int32, (ch, LANES), 1)
        lane1 = lax.broadcasted_iota(jnp.int32, (1, LANES), 1)
        low = lane < HEAD_DIM
        low1 = lane1 < HEAD_DIM
        clast = cum_s[pl.ds(ch - 1, 1), :]

        for g in range(GROUPS):
            col = slice(g * gw, (g + 1) * gw)
            ysf = ys_ref[:, col].astype(F32)
            zf = z_ref[:, col].astype(F32)
            sz = _sigmoid(zf)
            silz = zf * sz
            gg = ysf * silz
            r = lax.rsqrt(jnp.mean(gg * gg, axis=-1, keepdims=True) + EPS)
            nrm = gg * r
            dyb = dyb_ref[:, col].astype(F32)
            dnw_ref[:, col] += jnp.sum(dyb * nrm, axis=0, keepdims=True)
            dn = dyb * nw_ref[:, col]
            dgg = r * (dn - nrm * jnp.mean(dn * nrm, axis=-1, keepdims=True))
            dy_s[:, col] = dgg * silz
            dp_ref[:, col] = (dgg * ysf * (sz * (1.0 + zf * (1.0 - sz)))).astype(BF16)

        dccol_s[...] = jnp.zeros_like(dccol_s)
        dcrow_s[...] = jnp.zeros_like(dcrow_s)
        ddtcol_s[...] = jnp.zeros_like(ddtcol_s)
        dcl_s[...] = jnp.zeros_like(dcl_s)
        dbc_s[...] = jnp.zeros_like(dbc_s)

        for p in range(npair):
            g = p // ppg
            col = slice(p * LANES, (p + 1) * LANES)
            bcol = slice(g * STATE, (g + 1) * STATE)
            ccolg = slice(gn + g * STATE, gn + (g + 1) * STATE)
            bg = bc_s[:, bcol]
            cg = bc_s[:, ccolg]
            if p % ppg == 0:
                s_s[...] = _dot_nt(cg, bg)
                ds_s[...] = jnp.zeros_like(ds_s)
            heads = (PAIR * p, PAIR * p + 1)
            masks = (low, jnp.logical_not(low))
            masks1 = (low1, jnp.logical_not(low1))
            ccol = [_head_column(cum, lane, h) for h in heads]
            dcol = [_head_column(dt, lane, h) for h in heads]
            cl = [jnp.sum(jnp.where(lane1 == h, clast, 0.0), axis=-1, keepdims=True) for h in heads]
            cum_px = jnp.where(low, ccol[0], ccol[1])
            dt_px = jnp.where(low, dcol[0], dcol[1])
            cl_px = jnp.where(low1, cl[0], cl[1])
            e_px = jnp.exp(cum_px)
            dec_end = jnp.exp(cl_px - cum_px)
            gdec = jnp.exp(cl_px)
            xs_p = xs_s[:, col]
            xdt = xs_p * dt_px
            dyp = dy_s[:, col]
            hc = hs_ref[0, p]
            hcb = hc.astype(BF16)
            dhn = dh_s[p]
            dhnb = dhn.astype(BF16)

            ddsk_ref[:, col] += jnp.sum(dyp * xs_p, axis=0, keepdims=True)
            dxs_acc = dsk_ref[:, col] * dyp
            dye = dyp * e_px
            dyeb = dye.astype(BF16)
            dbc_s[:, ccolg] += _dot_nt(dyeb, hcb)
            dcum_lane = dye * _dot(cg, hcb)
            dh_from_y = _dot_tn(cg, dyeb)
            xd = xdt * dec_end
            dxd = _dot(bg, dhnb)
            dbc_s[:, bcol] += _dot_nt(xd.astype(BF16), dhnb)
            dxdt = dxd * dec_end
            t1 = dxd * xd
            dcum_lane = dcum_lane - t1
            dcl_lane = jnp.sum(t1, axis=0, keepdims=True) + jnp.sum(dhn * hc, axis=0, keepdims=True) * gdec
            dh_s[p] = gdec * dhn + dh_from_y
            xdtb = xdt.astype(BF16)
            for hi, h in enumerate(heads):
                dym = jnp.where(masks[hi], dyp, 0.0).astype(BF16)
                dw = _dot_nt(dym, xdtb)
                dec = jnp.exp(jnp.where(tril, ccol[hi] - cumt_s[pl.ds(h, 1), :], -jnp.inf))
                wm = s_s[...] * dec
                dxdt = dxdt + _dot_tn(wm.astype(BF16), dym)
                ds_s[...] += dw * dec
                gm = dw * wm
                rowsum = jnp.sum(gm, axis=-1, keepdims=True)
                lanesum = jnp.sum(jnp.where(masks[hi], dcum_lane, 0.0), axis=-1, keepdims=True)
                dccol_s[...] += jnp.where(lane == h, rowsum + lanesum, 0.0)
                dcrow_s[pl.ds(h, 1), :] = jnp.sum(gm, axis=0, keepdims=True)
                dcl_h = jnp.sum(jnp.where(masks1[hi], dcl_lane, 0.0), axis=-1, keepdims=True)
                dcl_s[...] += jnp.where(lane1 == h, dcl_h, 0.0)
            ddt_lane = dxdt * xs_p
            for hi, h in enumerate(heads):
                s = jnp.sum(jnp.where(masks[hi], ddt_lane, 0.0), axis=-1, keepdims=True)
                ddtcol_s[...] += jnp.where(lane == h, s, 0.0)
            dxs_s[:, col] = dxs_acc + dxdt * dt_px
            if p % ppg == ppg - 1:
                dsb = ds_s[...].astype(BF16)
                dbc_s[:, ccolg] += _dot(dsb, bg)
                dbc_s[:, bcol] += _dot_tn(dsb, cg)

        rowi = lax.broadcasted_iota(jnp.int32, (ch, LANES), 0)
        dcum = dccol_s[...] - dcrow_s[...].T + jnp.where(rowi == ch - 1, dcl_s[...], 0.0)
        ri = lax.broadcasted_iota(jnp.int32, (ch, ch), 0)
        ci = lax.broadcasted_iota(jnp.int32, (ch, ch), 1)
        dal = _dot_exact((ri <= ci).astype(F32), dcum)
        ddt = dal * aneg_ref[...] + ddtcol_s[...]
        da_ref[...] += jnp.sum(dal * dt, axis=0, keepdims=True)
        ddtr = ddt * _sigmoid(dtr_ref[...] + dtb_ref[...])
        ddt_ref[...] = ddtr
        ddtb_ref[...] += jnp.sum(ddtr, axis=0, keepdims=True)

        for (dpost, dsl, tail, raw_ref, kw_ref, dkw_ref, db_ref, out) in (
                (dxs_s, dsx_s, tailx, xs_ref, kwx_ref, dkwx_ref, dbx_ref, slice(si, 2 * si)),
                (dbc_s, dsb_s, tailb, bc_ref, kwb_ref, dkwb_ref, dbb_ref, slice(2 * si, 2 * si + 2 * gn))):
            dxc = dpost[...] * dsl[...]
            dxcb = dxc.astype(BF16)
            raw = raw_ref[...].astype(F32)
            raw_end = raw_ref[ch - EDGE:ch, :].astype(F32)
            tail[0:EDGE, :] = dxcb[ch - EDGE:ch].astype(F32)
            db_ref[...] += jnp.sum(dxc, axis=0, keepdims=True)
            draw = kw_ref[pl.ds(SSD_K - 1, 1), :] * dxc
            dkw_ref[pl.ds(SSD_K - 1, 1), :] += jnp.sum(dxc * raw, axis=0, keepdims=True)
            fix = jnp.zeros((EDGE, dxc.shape[1]), F32)
            for k in range(SSD_K - 1):
                moved = _dot(shift_s[k], dxcb)
                miss = tail[pl.ds(SSD_K - 1 - k, EDGE), :] - moved[ch - EDGE:ch]
                draw = draw + kw_ref[pl.ds(k, 1), :] * moved
                fix = fix + kw_ref[pl.ds(k, 1), :] * miss
                dkw_ref[pl.ds(k, 1), :] += (jnp.sum(moved * raw, axis=0, keepdims=True)
                                            + jnp.sum(miss * raw_end, axis=0, keepdims=True))
            dp_ref[:, out] = draw.astype(BF16)
            dp_ref[ch - EDGE:ch, out] = (draw[ch - EDGE:ch] + fix).astype(BF16)
            tail[EDGE:EDGE + 8, :] = dxcb[0:EDGE].astype(F32)[0:8]

    full = lambda shape: pl.BlockSpec(shape, lambda i: tuple(0 for _ in shape))
    rev = lambda blk: (lambda i: (nc - 1 - i, blk))
    small_in = [(SSD_K, si), (SSD_K, 2 * gn), (1, LANES), (1, LANES), (1, si), (1, si)]
    small = [(SSD_K, si), (SSD_K, 2 * gn), (1, si), (1, 2 * gn), (1, LANES), (1, LANES), (1, si), (1, si)]
    return _ride(
        body, (nc,),
        [pl.BlockSpec((ch, si), rev(zblk)), pl.BlockSpec((ch, si), rev(xsblk)),
         pl.BlockSpec((ch, 2 * gn), rev(bcblk)),
         pl.BlockSpec((ch, si), rev(0)), pl.BlockSpec((ch, 2 * gn), rev(0)),
         pl.BlockSpec((ch, LANES), rev(0)), pl.BlockSpec((ch, si), rev(0)),
         pl.BlockSpec((1, npair, STATE, LANES), lambda i: (nc - 1 - i, 0, 0, 0)),
         pl.BlockSpec((ch, si), rev(cw // si))] + [full(s) for s in small_in],
        [pl.BlockSpec((ch, 2 * si + 2 * gn), rev(0)), pl.BlockSpec((ch, LANES), rev(0))] + [full(s) for s in small],
        [jax.ShapeDtypeStruct((t, 2 * si + 2 * gn), BF16), jax.ShapeDtypeStruct((t, LANES), F32)]
        + [jax.ShapeDtypeStruct(s, F32) for s in small],
        [pltpu.VMEM((EDGE + 8, si), F32), pltpu.VMEM((EDGE + 8, 2 * gn), F32),
         pltpu.VMEM((SSD_K - 1, ch, ch), BF16),
         pltpu.VMEM((ch, si), F32), pltpu.VMEM((ch, 2 * gn), BF16),
         pltpu.VMEM((ch, si), F32), pltpu.VMEM((ch, 2 * gn), F32),
         pltpu.VMEM((ch, si), F32), pltpu.VMEM((ch, si), F32), pltpu.VMEM((ch, 2 * gn), F32),
         pltpu.VMEM((npair, STATE, LANES), F32),
         pltpu.VMEM((ch, ch), F32), pltpu.VMEM((ch, ch), F32),
         pltpu.VMEM((ch, LANES), F32), pltpu.VMEM((LANES, ch), F32),
         pltpu.VMEM((ch, LANES), F32), pltpu.VMEM((LANES, ch), F32),
         pltpu.VMEM((ch, LANES), F32), pltpu.VMEM((1, LANES), F32)],
        [proj, proj, proj, pre_xs, pre_bc, dt_raw, ys, hsave, dy, kw_xs, kw_bc, dtb, aneg, dskip, normw],
        {}, rider, ("arbitrary",), name)


def final_loss(x, nw, tgt, name):
    t, d = x.shape
    tm = _tile(t, 512, 8)

    def body(x_ref, nw_ref, t_ref, dx_ref, dnw_ref, ls_ref):
        xf = x_ref[...]
        r = lax.rsqrt(jnp.mean(xf * xf, axis=-1, keepdims=True) + EPS)
        nx = xf * r
        e = nx * nw_ref[...] - t_ref[...]
        dyv = e * (1.0 / d)
        dn = dyv * nw_ref[...]
        dx_ref[...] = r * (dn - nx * jnp.mean(dn * nx, axis=-1, keepdims=True))

        @pl.when(pl.program_id(0) == 0)
        def _():
            dnw_ref[...] = jnp.zeros_like(dnw_ref)
            ls_ref[...] = jnp.zeros_like(ls_ref)

        dnw_ref[...] += jnp.sum(dyv * nx, axis=0, keepdims=True)
        ls_ref[...] += jnp.sum(e * e, axis=0, keepdims=True) * (0.5 / d)

    return pl.pallas_call(
        body, name=name, grid=(t // tm,),
        in_specs=[pl.BlockSpec((tm, d), lambda i: (i, 0)), pl.BlockSpec((1, d), lambda i: (0, 0)),
                  pl.BlockSpec((tm, d), lambda i: (i, 0))],
        out_specs=[pl.BlockSpec((tm, d), lambda i: (i, 0)), pl.BlockSpec((1, d), lambda i: (0, 0)),
                   pl.BlockSpec((1, d), lambda i: (0, 0))],
        out_shape=[jax.ShapeDtypeStruct((t, d), F32), jax.ShapeDtypeStruct((1, d), F32),
                   jax.ShapeDtypeStruct((1, d), F32)],
        compiler_params=_params(("arbitrary",)),
    )(x, nw, tgt)


def _rows3(a):
    if a.ndim == 1:
        return a.reshape(1, 1, a.shape[0])
    if a.ndim == 2:
        return a.reshape(1, *a.shape)
    return a.reshape(-1, a.shape[-2], a.shape[-1])


def adamw(w, g, m, v, name):
    shape = w.shape
    views = [_rows3(a) for a in (w, g, m, v)]
    b, r, c = views[0].shape
    tr = _tile(r, 256, 16) if r % 16 == 0 else r

    def body(w_ref, g_ref, m_ref, v_ref, d_ref, nm_ref, nv_ref):
        g = g_ref[...]
        m = ADAM_B1 * m_ref[...] + (1.0 - ADAM_B1) * g
        v = ADAM_B2 * v_ref[...] + (1.0 - ADAM_B2) * (g * g)
        m_hat = m / (1.0 - ADAM_B1 ** ADAM_STEP)
        v_hat = v / (1.0 - ADAM_B2 ** ADAM_STEP)
        d_ref[...] = -ADAM_LR * (m_hat / (jnp.sqrt(v_hat) + ADAM_EPS) + ADAM_WD * w_ref[...])
        nm_ref[...] = m
        nv_ref[...] = v

    spec = pl.BlockSpec((1, tr, c), lambda i, j: (i, j, 0))
    outs = pl.pallas_call(
        body, name=name, grid=(b, r // tr), in_specs=[spec] * 4, out_specs=[spec] * 3,
        out_shape=[jax.ShapeDtypeStruct((b, r, c), F32)] * 3,
        compiler_params=_params(("parallel", "parallel")),
    )(*views)
    return [o.reshape(shape) for o in outs]


def adamw_halves(w, g_mine, g_theirs, m, v, name):
    nl, r, c = w.shape
    r2 = r // 2
    tr = _tile(r2, 256, 16)
    nb = r2 // tr

    def body(w_ref, gm_ref, gt_ref, m_ref, v_ref, g_ref, d_ref, nm_ref, nv_ref):
        mine = (pl.program_id(1) // nb) == lax.axis_index("c")
        g = jnp.where(mine, gm_ref[...], gt_ref[...])
        m = ADAM_B1 * m_ref[...] + (1.0 - ADAM_B1) * g
        v = ADAM_B2 * v_ref[...] + (1.0 - ADAM_B2) * (g * g)
        m_hat = m / (1.0 - ADAM_B1 ** ADAM_STEP)
        v_hat = v / (1.0 - ADAM_B2 ** ADAM_STEP)
        g_ref[...] = g
        d_ref[...] = -ADAM_LR * (m_hat / (jnp.sqrt(v_hat) + ADAM_EPS) + ADAM_WD * w_ref[...])
        nm_ref[...] = m
        nv_ref[...] = v

    whole = pl.BlockSpec((1, tr, c), lambda l, i: (l, i, 0))
    half = pl.BlockSpec((1, tr, c), lambda l, i: (l, i % nb, 0))
    return pl.pallas_call(
        body, name=name, grid=(nl, 2 * nb), in_specs=[whole, half, half, whole, whole], out_specs=[whole] * 4,
        out_shape=[jax.ShapeDtypeStruct((nl, r, c), F32)] * 4,
        compiler_params=_params(("parallel", "parallel")),
    )(w, g_mine, g_theirs, m, v)


def _coords():
    return lax.axis_index("x"), lax.axis_index("y"), lax.axis_index("c")


def _ici_peers(x, y):
    chips = [(1 - x, y), (x, 1 - y), (1 - x, 1 - y)]
    return chips, [2 * cx + cy for cx, cy in chips]


def _place(ref, how, chip, layers, per):
    if how == "lead":
        return ref.at[chip, layers]
    start = pl.multiple_of(chip * per, per)
    if how == "rows":
        return ref.at[layers, pl.ds(start, per), :]
    return ref.at[layers, :, pl.ds(start, per)]


def gather_weights(shards, hows, name):
    na = len(shards)
    out_shape = []
    for s, how in zip(shards, hows):
        assert s.shape[0] % 2 == 0
        if how == "lead":
            shp = (N_CHIPS, *s.shape)
        elif how == "rows":
            shp = (s.shape[0], N_CHIPS * s.shape[1], s.shape[2])
        else:
            shp = (s.shape[0], s.shape[1], N_CHIPS * s.shape[2])
        out_shape.append(jax.ShapeDtypeStruct(shp, s.dtype))

    def body(*refs):
        ins = refs[:na]
        outs = refs[na:2 * na]
        send_sems, recv_sems = refs[2 * na:]
        x, y, c = _coords()
        me = 2 * x + y
        chips, chip_ids = _ici_peers(x, y)
        sibling = (x, y, 1 - c)

        def dst(a, chip, layers):
            per = {"lead": 0, "rows": ins[a].shape[1], "cols": ins[a].shape[-1]}[hows[a]]
            return _place(outs[a], hows[a], chip, layers, per)

        def copy(a, k, src, dst_ref, to):
            return pltpu.make_async_remote_copy(
                src_ref=src, dst_ref=dst_ref, send_sem=send_sems.at[7 * a + k], recv_sem=recv_sems.at[7 * a + k],
                device_id=to, device_id_type=MESH)

        started = []
        halves = []
        for a in range(na):
            nl = ins[a].shape[0]
            hl = nl // 2
            mine = pl.ds(c * hl, hl)
            theirs = pl.ds((1 - c) * hl, hl)
            halves.append((mine, theirs))
            for k in range(3):
                cp = copy(a, k, ins[a].at[mine], dst(a, me, mine), (*chips[k], c))
                cp.start()
                started.append(cp)
            own = copy(a, 6, ins[a], dst(a, me, pl.ds(0, nl)), sibling)
            own.start()
            started.append(own)
        for a in range(na):
            mine, _ = halves[a]
            for k in range(3):
                landed = dst(a, chip_ids[k], mine)
                copy(a, k, landed, landed, (*chips[k], c)).wait_recv()
                fw = copy(a, 3 + k, landed, landed, sibling)
                fw.start()
                started.append(fw)
        for a in range(na):
            _, theirs = halves[a]
            for k in range(3):
                got = dst(a, chip_ids[k], theirs)
                copy(a, 3 + k, got, got, sibling).wait_recv()
            whole = dst(a, me, pl.ds(0, ins[a].shape[0]))
            copy(a, 6, whole, whole, sibling).wait_recv()
        for cp in started:
            cp.wait_send()

    return pl.pallas_call(
        body, name=name, in_specs=_any_specs(na), out_specs=_any_specs(na), out_shape=out_shape,
        scratch_shapes=[pltpu.SemaphoreType.DMA((7 * na,)), pltpu.SemaphoreType.DMA((7 * na,))],
        compiler_params=pltpu.CompilerParams(has_side_effects=True),
    )(*shards)


def _remote(src, dst, send_sems, recv_sems, k, to):
    return pltpu.make_async_remote_copy(src_ref=src, dst_ref=dst, send_sem=send_sems.at[k], recv_sem=recv_sems.at[k],
                                        device_id=to, device_id_type=MESH)


LAYER_HOW = ("lead", "rows", "cols", "rows")


def _layer_place(ref, how, chip, shard_shape, start, size):
    r, c = shard_shape
    if how == "lead":
        return ref.at[chip, :, pl.ds(start, size), :]
    if how == "rows":
        return ref.at[:, pl.ds(pl.multiple_of(chip * r + start, HALO), size), :]
    return ref.at[:, pl.ds(start, size), pl.ds(pl.multiple_of(chip * c, LANES), c)]


def weight_rider_ici(shards, hows, layer):
    shapes = [tuple(s.shape[1:]) for s in shards]
    out_shapes = []
    for (r, c), how, s in zip(shapes, hows, shards):
        shp = {"lead": (N_CHIPS, 1, r, c), "rows": (1, N_CHIPS * r, c), "cols": (1, r, N_CHIPS * c)}[how]
        out_shapes.append(jax.ShapeDtypeStruct(shp, s.dtype))

    def copies(ins, outs, send_sems, recv_sems):
        x, y, c = _coords()
        me = 2 * x + y
        chips, chip_ids = _ici_peers(x, y)
        sibling = (x, y, 1 - c)
        pairs = []
        for a, (shape, how) in enumerate(zip(shapes, hows)):
            half = shape[0] // 2
            mine = pl.multiple_of(c * half, HALO)
            src = ins[a].at[pl.ds(layer, 1)]
            for k in range(3):
                to = (*chips[k], c)
                land = _layer_place(outs[a], how, chip_ids[k], shape, mine, half)
                pairs.append((_remote(src.at[:, pl.ds(mine, half), :], _layer_place(outs[a], how, me, shape, mine, half),
                                      send_sems, recv_sems, 4 * a + k, to),
                              _remote(land, land, send_sems, recv_sems, 4 * a + k, to)))
            whole = _layer_place(outs[a], how, me, shape, 0, shape[0])
            pairs.append((_remote(src, whole, send_sems, recv_sems, 4 * a + 3, sibling),
                          _remote(whole, whole, send_sems, recv_sems, 4 * a + 3, sibling)))
        return pairs

    return Rider(list(shards), out_shapes, {}, 4 * len(shards), copies)


def weight_rider_d2d(bufs, shapes, hows):
    def copies(ins, outs, send_sems, recv_sems):
        x, y, c = _coords()
        _, chip_ids = _ici_peers(x, y)
        sibling = (x, y, 1 - c)
        pairs = []
        for a, (shape, how) in enumerate(zip(shapes, hows)):
            half = shape[0] // 2
            mine = pl.multiple_of(c * half, HALO)
            theirs = pl.multiple_of((1 - c) * half, HALO)
            for k in range(3):
                land = _layer_place(outs[a], how, chip_ids[k], shape, theirs, half)
                pairs.append((_remote(_layer_place(ins[a], how, chip_ids[k], shape, mine, half),
                                      _layer_place(outs[a], how, chip_ids[k], shape, mine, half),
                                      send_sems, recv_sems, 3 * a + k, sibling),
                              _remote(land, land, send_sems, recv_sems, 3 * a + k, sibling)))
        return pairs

    return Rider(list(bufs), [jax.ShapeDtypeStruct(b.shape, b.dtype) for b in bufs],
                 {a: a for a in range(len(bufs))}, 3 * len(bufs), copies)


def grads_rider_sibling(arrs):
    def copies(ins, outs, send_sems, recv_sems):
        x, y, c = _coords()
        sibling = (x, y, 1 - c)
        pairs = []
        for a in range(len(arrs)):
            r2 = ins[a].shape[1] // 2
            src = ins[a].at[:, pl.ds(pl.multiple_of((1 - c) * r2, 8), r2), :]
            pairs.append((_remote(src, outs[a], send_sems, recv_sems, a, sibling),
                          _remote(outs[a], outs[a], send_sems, recv_sems, a, sibling)))
        return pairs

    return Rider(list(arrs), [jax.ShapeDtypeStruct((a.shape[0], a.shape[1] // 2, a.shape[2]), a.dtype) for a in arrs],
                 {}, len(arrs), copies)


def chip_sum(g, recv, name):
    nch, r, c = g.shape
    r2 = r // 2
    tr = _tile(r2, 256, 16)
    nb = r2 // tr

    def body(g0_ref, g1_ref, r_ref, o32_ref, o16_ref):
        s = jnp.where(lax.axis_index("c") == 0, g0_ref[...], g1_ref[...]) + r_ref[...]
        o32_ref[...] = s
        o16_ref[...] = s.astype(BF16)

    here = pl.BlockSpec((1, tr, c), lambda i, j: (i, j, 0))
    return pl.pallas_call(
        body, name=name, grid=(nch, nb),
        in_specs=[here, pl.BlockSpec((1, tr, c), lambda i, j: (i, nb + j, 0)), here],
        out_specs=[here, here],
        out_shape=[jax.ShapeDtypeStruct((nch, r2, c), F32), jax.ShapeDtypeStruct((nch, r2, c), BF16)],
        compiler_params=_params(("parallel", "parallel")),
    )(g, g, recv)


def grads_rider_chips(arrs):
    def copies(ins, outs, send_sems, recv_sems):
        x, y, c = _coords()
        chips, chip_ids = _ici_peers(x, y)
        pairs = []
        for a in range(len(arrs)):
            for k in range(3):
                to = (*chips[k], c)
                pairs.append((_remote(ins[a].at[chip_ids[k]], outs[a].at[k], send_sems, recv_sems, 3 * a + k, to),
                              _remote(outs[a].at[k], outs[a].at[k], send_sems, recv_sems, 3 * a + k, to)))
        return pairs

    return Rider(list(arrs), [jax.ShapeDtypeStruct((3, *a.shape[1:]), a.dtype) for a in arrs], {}, 3 * len(arrs),
                 copies)


def grad_sum(p32, recv, layer, nl, buf, name):
    _, r2, c = p32.shape
    tr = _tile(r2, 256, 16)
    nb = r2 // tr

    def body(p0_ref, p1_ref, p2_ref, p3_ref, r0_ref, r1_ref, r2_ref, *rest):
        o_ref = rest[-1]
        x, y, _ = _coords()
        chip = 2 * x + y
        own = jnp.where(chip == 0, p0_ref[...], jnp.where(chip == 1, p1_ref[...],
                                                        jnp.where(chip == 2, p2_ref[...], p3_ref[...])))
        o_ref[...] = own + r0_ref[...].astype(F32) + r1_ref[...].astype(F32) + r2_ref[...].astype(F32)

    slot = lambda k: pl.BlockSpec((1, tr, c), lambda j: (k, j, 0))
    in_specs = [slot(k) for k in range(N_CHIPS)] + [slot(k) for k in range(3)]
    args = [p32] * N_CHIPS + [recv] * 3
    aliases = {}
    if buf is not None:
        in_specs.append(pl.BlockSpec(memory_space=pl.ANY))
        args.append(buf)
        aliases = {len(args) - 1: 0}
    return pl.pallas_call(
        body, name=name, grid=(nb,), in_specs=in_specs,
        out_specs=pl.BlockSpec((1, tr, c), lambda j: (layer, j, 0)),
        out_shape=jax.ShapeDtypeStruct((nl, r2, c), F32),
        input_output_aliases=aliases,
        compiler_params=_params(("parallel",)),
    )(*args)


def grads_rider_exchange(bufs):
    def copies(ins, outs, send_sems, recv_sems):
        x, y, c = _coords()
        sibling = (x, y, 1 - c)
        return [(_remote(ins[a], outs[a], send_sems, recv_sems, a, sibling),
                 _remote(outs[a], outs[a], send_sems, recv_sems, a, sibling)) for a in range(len(bufs))]

    return Rider(list(bufs), [jax.ShapeDtypeStruct(b.shape, b.dtype) for b in bufs], {}, len(bufs), copies)


def allreduce_small(buf, name):
    r, cdim = buf.shape

    def body(x_ref, o_ref, gath, send_sems, recv_sems):
        x, y, c = _coords()
        me, sibling = (x, y, c), (x, y, 1 - c)
        chips, _ = _ici_peers(x, y)

        def slot(px, py, pc):
            return gath.at[4 * px + 2 * py + pc]

        def copy(k, block, to, src=None):
            return pltpu.make_async_remote_copy(
                src_ref=slot(*block) if src is None else src, dst_ref=slot(*block),
                send_sem=send_sems.at[k], recv_sem=recv_sems.at[k], device_id=to, device_id_type=MESH)

        gath[4 * x + 2 * y + c] = x_ref[...]
        first = [copy(0, me, sibling, src=x_ref)]
        first += [copy(1 + j, me, (*chip, c), src=x_ref) for j, chip in enumerate(chips)]
        for cp in first:
            cp.start()
        passed = [copy(4 + j, (*chip, c), sibling) for j, chip in enumerate(chips)]
        for j, chip in enumerate(chips):
            copy(1 + j, (*chip, c), me).wait_recv()
            passed[j].start()
        copy(0, sibling, me).wait_recv()
        for j, chip in enumerate(chips):
            copy(4 + j, (*chip, 1 - c), me).wait_recv()
        for cp in first + passed:
            cp.wait_send()
        acc = gath[0]
        for d in range(1, 8):
            acc = acc + gath[d]
        o_ref[...] = acc

    return pl.pallas_call(
        body, name=name,
        in_specs=[pl.BlockSpec(memory_space=pltpu.VMEM)], out_specs=pl.BlockSpec(memory_space=pltpu.VMEM),
        out_shape=jax.ShapeDtypeStruct((r, cdim), F32),
        scratch_shapes=[pltpu.VMEM((8, r, cdim), F32), pltpu.SemaphoreType.DMA((7,)), pltpu.SemaphoreType.DMA((7,))],
        compiler_params=pltpu.CompilerParams(has_side_effects=True),
    )(buf)


def _expand_heads(v):
    return jnp.repeat(v.astype(F32), HEAD_DIM).reshape(1, -1)


def _pad_lanes(v):
    return jnp.pad(v.astype(F32), (0, LANES - v.shape[0])).reshape(1, LANES)


def local_step(x, tgt, p, comm, cols):
    nl = p["norm_mix_w"].shape[0]
    d = x.shape[1]
    cw = p["short_conv_w"].shape[2]
    si = p["ssd_norm_w"].shape[1]
    ff, npad = comm.ff, comm.npad
    nh = si // HEAD_DIM
    gn = GROUPS * STATE
    dt_off = 3 * cw + si + si + 2 * gn
    assert cols == dt_off + nh and nh <= LANES and dt_off % LANES == 0 and npad == dt_off + LANES
    pieces = [(0, cw), (cw, cw), (2 * cw, cw), (3 * cw, si), (3 * cw + si, si), (3 * cw + 2 * si, 2 * gn),
              (dt_off, LANES)]

    saved = []
    for l in range(nl):
        nw1 = p["norm_mix_w"][l].reshape(1, d)
        nw2 = p["norm_mlp_w"][l].reshape(1, d)
        kw3 = p["short_conv_w"][l]
        kwx, kwb = p["ssd_conv_w"][l][:, :si], p["ssd_conv_w"][l][:, si:]
        bx, bb = p["ssd_conv_b"][l][:si].reshape(1, si), p["ssd_conv_b"][l][si:].reshape(1, 2 * gn)
        dtb = _pad_lanes(p["dt_bias"][l])
        aneg = _pad_lanes(-jnp.exp(p["a_log"][l]))
        dsk = _expand_heads(p["d_skip"][l])
        snw = p["ssd_norm_w"][l].reshape(1, si)
        ssd_args = (kwx, kwb, bx, bb, dtb, aneg, dsk, snw)

        w_in = comm.weight(l, "w_in")
        (proj, h, dt_raw), sent = norm_matmul(x, nw1, w_in, 0, dt_off, BF16, "in_proj", tail_block=dt_off // LANES,
                                              rider=comm.rider("in_proj", l))
        comm.done("in_proj", l, sent)
        y_mix = conv_mixer_fwd(proj, kw3, cw, cw + si, "conv_mixer_fwd")
        (y_mix, *ssd_saved), sent = ssd_fwd(proj, dt_raw, y_mix, *ssd_args, cw, si, "ssd_fwd",
                                            rider=comm.rider("ssd_fwd", l))
        comm.done("ssd_fwd", l, sent)
        x2, _ = matmul(y_mix, comm.weight(l, "w_out"), 0, False, d, F32, "out_proj", residual=x)
        (up, h2), sent = norm_matmul(x2, nw2, comm.weight(l, "w_up"), 0, ff, BF16, "up_proj",
                                     rider=comm.rider("up_proj", l))
        comm.done("up_proj", l, sent)
        x3, sent = matmul(up, comm.weight(l, "w_down"), 0, False, d, F32, "down_proj", lhs_fn=_relu2, residual=x2,
                          rider=comm.rider("down_proj", l))
        comm.done("down_proj", l, sent)
        saved.append((x, h, proj, dt_raw, y_mix, ssd_saved, x2, h2, up, nw1, nw2, kw3, ssd_args))
        x = x3

    dx, dwf, lvec = final_loss(x, p["final_norm_w"].reshape(1, d), tgt, "final_loss")
    loss = jnp.sum(lvec)

    names = ("norm_mix_w", "short_conv_w", "ssd_conv_w", "ssd_conv_b", "dt_bias", "a_log", "d_skip",
             "ssd_norm_w", "norm_mlp_w")
    grads = {k: [None] * nl for k in names}
    for l in reversed(range(nl)):
        x0, h, proj, dt_raw, y_mix, ssd_saved, x2, h2, up, nw1, nw2, kw3, ssd_args = saved[l]
        kwx, kwb, _, _, dtb, aneg, dsk, snw = ssd_args
        dup, sent = matmul(dx, comm.weight(l, "w_down"), 0, True, ff, BF16, "down_bwd", relu_gate=up,
                           rider=comm.rider("down_bwd", l))
        comm.done("down_bwd", l, sent)
        g_down, sums = matmul_tn(up, dx, "down_wgrad", a_fn=_relu2, chip_sums=comm.side("down_wgrad", l))
        comm.side_done("down_wgrad", l, sums)
        comm.take_gradient(l, "w_down", g_down)
        (dx2, dnw2), _ = matmul_normbwd([(dup, 0)], [(0, ff)], comm.weight(l, "w_up"), 0, x2, nw2, dx, "up_bwd")
        comm.take_gradient(l, "w_up", matmul_tn(h2, dup, "up_wgrad", by_chip=True))
        grads["norm_mlp_w"][l] = dnw2.reshape(d)
        dy, sent = matmul(dx2, comm.weight(l, "w_out"), 0, True, cw + si, BF16, "out_bwd",
                          rider=comm.rider("out_bwd", l))
        comm.done("out_bwd", l, sent)
        g_out, sums = matmul_tn(y_mix, dx2, "out_wgrad", chip_sums=comm.side("out_wgrad", l))
        comm.side_done("out_wgrad", l, sums)
        comm.take_gradient(l, "w_out", g_out)
        du, dkw3 = conv_mixer_bwd(proj, dy, kw3, cw, "conv_mixer_bwd")
        (dssd, ddt, dkwx, dkwb, dbx, dbb, ddtb, da, ddsk, dsnw), sent = ssd_bwd(
            proj, dt_raw, *ssd_saved, dy, kwx, kwb, dtb, aneg, dsk, snw, cw, si, "ssd_bwd",
            rider=comm.rider("ssd_bwd", l))
        comm.done("ssd_bwd", l, sent)
        views = [(du, 0), (du, 1), (du, 2), (dssd, 0), (dssd, 1), (dssd, 2 * si // (2 * gn)), (ddt, 0)]
        (dxl, dnw1), sent = matmul_normbwd(views, pieces, comm.weight(l, "w_in"), 0, x0, nw1, dx2, "in_bwd",
                                           rider=comm.rider("in_bwd", l))
        comm.done("in_bwd", l, sent)
        parts = []
        for i, dp in enumerate((du, dssd, ddt)):
            part, bufs = matmul_tn(h, dp, "in_wgrad_%d" % i, grad_sums=comm.reductions("in_wgrad_%d" % i))
            comm.reductions_done("in_wgrad_%d" % i, bufs)
            parts.append(part)
        comm.take_gradient(l, "w_in", split_to_chips(parts, cols, "in_wgrad_split"))
        grads["norm_mix_w"][l] = dnw1.reshape(d)
        grads["short_conv_w"][l] = dkw3
        grads["ssd_conv_w"][l] = jnp.concatenate([dkwx, dkwb], axis=1)
        grads["ssd_conv_b"][l] = jnp.concatenate([dbx, dbb], axis=1).reshape(-1)
        grads["dt_bias"][l] = ddtb[0, :nh]
        grads["a_log"][l] = da[0, :nh] * aneg[0, :nh]
        grads["d_skip"][l] = jnp.sum(ddsk.reshape(nh, HEAD_DIM), axis=1)
        grads["ssd_norm_w"][l] = dsnw.reshape(si)
        dx = dxl

    grads = {k: jnp.stack(v) for k, v in grads.items()}
    grads["final_norm_w"] = dwf.reshape(d)
    return loss, dx, grads


BIG = ("w_in", "w_out", "w_up", "w_down")
SMALL_SHARDED = ("short_conv_w", "ssd_conv_w")
SMALL_REPL = ("norm_mix_w", "ssd_conv_b", "dt_bias", "a_log", "d_skip", "ssd_norm_w", "norm_mlp_w", "final_norm_w")
WEIGHTS = ("norm_mix_w", "w_in", "short_conv_w", "ssd_conv_w", "ssd_conv_b", "dt_bias", "a_log", "d_skip",
           "ssd_norm_w", "w_out", "norm_mlp_w", "w_up", "w_down", "final_norm_w")
SMALL_COLS = 1024


def _pack_small(named):
    flat = jnp.concatenate([v.reshape(-1).astype(F32) for v in named])
    n = flat.shape[0]
    rows = -(-n // SMALL_COLS)
    rows = -(-rows // 8) * 8
    return jnp.pad(flat, (0, rows * SMALL_COLS - n)).reshape(rows, SMALL_COLS)


def _unpack_small(buf, like):
    flat = buf.reshape(-1)
    out, off = [], 0
    for v in like:
        out.append(flat[off:off + v.size].reshape(v.shape))
        off += v.size
    return out


class ChipComm:
    IO = ("w_in", "w_out")
    MLP = ("w_up", "w_down")
    FIRST = ("w_in",)
    EARLY = ("w_up", "w_down", "w_out")

    def __init__(self, shards, nl, npad):
        self.shards, self.nl, self.npad = shards, nl, npad
        self.how = dict(zip(BIG, LAYER_HOW))
        self.ff = N_CHIPS * shards["w_up"].shape[2]
        self.w = {}
        self.landed = {}
        self.grad = {}
        self.sums = {}
        self.from_sibling = {}
        self.from_chips = {}
        self.bufs = {k: None for k in BIG}
        first = run_rider(self._ici(self.FIRST, 0), "gather_first_ici")
        self._gathered(self.FIRST, 0, run_rider(self._d2d(self.FIRST, first), "gather_first_d2d"))

    def _ici(self, group, l):
        return weight_rider_ici([self.shards[k] for k in group], [self.how[k] for k in group], l)

    def _d2d(self, group, landed):
        return weight_rider_d2d(landed, [tuple(self.shards[k].shape[1:]) for k in group],
                                [self.how[k] for k in group])

    def _gathered(self, group, l, arrays):
        for k, g in zip(group, arrays):
            self.w[(l, k)] = join_from_chips(g, self.npad, "w_in_join") if k == "w_in" else g

    def _to_sibling(self, group, l):
        return grads_rider_sibling([self.grad[(l, k)] for k in group])

    def _summed(self, group, l, from_sibling):
        self.sums[group] = (l, [chip_sum(self.grad.pop((l, k)), r, "chip_sum") for k, r in zip(group, from_sibling)])

    def side(self, point, l):
        group = self.IO if point == "down_wgrad" else self.MLP
        if group not in self.from_sibling:
            return ()
        layer, parts = self.from_sibling[group]
        return [(self.grad[(layer, k)], r) for k, r in zip(group, parts)]

    def side_done(self, point, l, sums):
        group = self.IO if point == "down_wgrad" else self.MLP
        if sums:
            layer, _ = self.from_sibling.pop(group)
            for k in group:
                del self.grad[(layer, k)]
            self.sums[group] = (layer, sums)

    def _to_chips(self, groups):
        return grads_rider_chips([s[1] for group in groups for s in self.sums[group][1]])

    def _reduced(self, group, from_chips):
        l, sums = self.sums.pop(group)
        for k, s, r in zip(group, sums, from_chips):
            self.bufs[k] = grad_sum(s[0], r, l, self.nl, self.bufs[k], "grad_sum")

    def weight(self, l, name):
        return self.w[(l, name)]

    def take_gradient(self, l, name, g):
        self.grad[(l, name)] = g if g.ndim == 3 else g.reshape(N_CHIPS, g.shape[0] // N_CHIPS, g.shape[1])

    def rider(self, point, l):
        more = l + 1 < self.nl
        if point == "in_proj":
            return self._ici(self.EARLY if l == 0 else self.MLP, l)
        if point == "ssd_fwd":
            return self._d2d(*self.landed["own layer"])
        if point == "up_proj":
            return self._ici(self.IO, l + 1) if more else None
        if point == "down_proj":
            return self._d2d(self.IO, self.landed[self.IO]) if more else None
        if point == "down_bwd":
            return self._to_sibling(self.IO, l + 1) if more else None
        if point == "out_bwd":
            return self._to_sibling(self.MLP, l)
        if point == "ssd_bwd":
            return self._to_chips([self.IO]) if more else None
        return self._to_chips([self.MLP])

    def done(self, point, l, results):
        if not results:
            return
        if point == "in_proj":
            self.landed["own layer"] = (self.EARLY if l == 0 else self.MLP, results)
        elif point == "up_proj":
            self.landed[self.IO] = results
        elif point == "ssd_fwd":
            self._gathered(self.landed.pop("own layer")[0], l, results)
        elif point == "down_proj":
            self._gathered(self.IO, l + 1, results)
        elif point == "down_bwd":
            self.from_sibling[self.IO] = (l + 1, results)
        elif point == "out_bwd":
            self.from_sibling[self.MLP] = (l, results)
        elif point == "ssd_bwd":
            self.from_chips[self.IO] = results
        else:
            self.from_chips[self.MLP] = results

    def _pending_groups(self, point):
        groups = [self.IO] if point == "in_wgrad_0" else [self.MLP] if point == "in_wgrad_1" else []
        return [g for g in groups if g in self.from_chips]

    def reductions(self, point):
        items = []
        for group in self._pending_groups(point):
            layer, sums = self.sums[group]
            items += [(s[0], r, layer, self.nl, self.bufs[k]) for k, s, r in zip(group, sums, self.from_chips[group])]
        return items

    def reductions_done(self, point, bufs):
        if bufs:
            names = [k for group in self._pending_groups(point) for k in group]
            for group in self._pending_groups(point):
                del self.sums[group], self.from_chips[group]
            self.bufs.update(zip(names, bufs))

    def finish(self):
        self._summed(self.IO, 0, run_rider(self._to_sibling(self.IO, 0), "grads_to_sibling"))
        done = [self.bufs[k] for k in self.MLP]
        arrived = run_rider(join_riders(self._to_chips([self.IO]), grads_rider_exchange(done)), "grads_to_chips")
        self._reduced(self.IO, arrived[:len(self.IO)])
        last = [self.bufs[k] for k in self.IO]
        theirs = dict(zip(self.MLP + self.IO, arrived[len(self.IO):] + run_rider(grads_rider_exchange(last),
                                                                                 "grads_exchange")))
        return {k: (self.bufs[k], theirs[k]) for k in BIG}


def kernel(x, norm_mix_w, w_in, short_conv_w, ssd_conv_w, ssd_conv_b, dt_bias, a_log, d_skip, ssd_norm_w, w_out, norm_mlp_w, w_up, w_down, final_norm_w, loss_target, m_norm_mix_w, m_w_in, m_short_conv_w, m_ssd_conv_w, m_ssd_conv_b, m_dt_bias, m_a_log, m_d_skip, m_ssd_norm_w, m_w_out, m_norm_mlp_w, m_w_up, m_w_down, m_final_norm_w, v_norm_mix_w, v_w_in, v_short_conv_w, v_ssd_conv_w, v_ssd_conv_b, v_dt_bias, v_a_log, v_d_skip, v_ssd_norm_w, v_w_out, v_norm_mlp_w, v_w_up, v_w_down, v_final_norm_w):
    w = dict(norm_mix_w=norm_mix_w, w_in=w_in, short_conv_w=short_conv_w, ssd_conv_w=ssd_conv_w,
             ssd_conv_b=ssd_conv_b, dt_bias=dt_bias, a_log=a_log, d_skip=d_skip, ssd_norm_w=ssd_norm_w, w_out=w_out,
             norm_mlp_w=norm_mlp_w, w_up=w_up, w_down=w_down, final_norm_w=final_norm_w)
    m = dict(norm_mix_w=m_norm_mix_w, w_in=m_w_in, short_conv_w=m_short_conv_w, ssd_conv_w=m_ssd_conv_w,
             ssd_conv_b=m_ssd_conv_b, dt_bias=m_dt_bias, a_log=m_a_log, d_skip=m_d_skip, ssd_norm_w=m_ssd_norm_w,
             w_out=m_w_out, norm_mlp_w=m_norm_mlp_w, w_up=m_w_up, w_down=m_w_down, final_norm_w=m_final_norm_w)
    v = dict(norm_mix_w=v_norm_mix_w, w_in=v_w_in, short_conv_w=v_short_conv_w, ssd_conv_w=v_ssd_conv_w,
             ssd_conv_b=v_ssd_conv_b, dt_bias=v_dt_bias, a_log=v_a_log, d_skip=v_d_skip, ssd_norm_w=v_ssd_norm_w,
             w_out=v_w_out, norm_mlp_w=v_norm_mlp_w, w_up=v_w_up, w_down=v_w_down, final_norm_w=v_final_norm_w)
    xi, yi, ci = lax.axis_index("x"), lax.axis_index("y"), lax.axis_index("c")
    chip = 2 * xi + yi
    nl = w_up.shape[0]
    cols = N_CHIPS * w_in.shape[2]
    npad = cols // LANES * LANES + LANES

    full = dict(w)
    small_gathered = gather_weights([w[k] for k in SMALL_SHARDED], ["lead"] * len(SMALL_SHARDED), "gather_small")
    for k, g4 in zip(SMALL_SHARDED, small_gathered):
        full[k] = jnp.concatenate([g4[j] for j in range(N_CHIPS)], axis=2)
    comm = ChipComm({k: w[k].astype(BF16) for k in BIG}, nl, npad)

    loss, grad_x, grads = local_step(x[0], loss_target[0], full, comm, cols)
    loss = lax.psum(loss, ("x", "y", "c"))
    halves = comm.finish()
    g_shard = {}

    small_names = SMALL_REPL + SMALL_SHARDED
    small_sum = allreduce_small(_pack_small([grads[k] for k in small_names]), "allreduce_small")
    for k, g in zip(small_names, _unpack_small(small_sum, [grads[k] for k in small_names])):
        if k in SMALL_SHARDED:
            width = w[k].shape[2]
            g = lax.dynamic_slice_in_dim(g, chip * width, width, axis=2)
        g_shard[k] = g

    delta, new_m, new_v = {}, {}, {}
    for k in BIG:
        g_shard[k], delta[k], new_m[k], new_v[k] = adamw_halves(w[k], *halves[k], m[k], v[k], "adamw_%s" % k)
    packed = [_pack_small([d_[k] for k in small_names]) for d_ in (w, g_shard, m, v)]
    outs = adamw(*packed, "adamw_small")
    for d_, buf in zip((delta, new_m, new_v), outs):
        for k, val in zip(small_names, _unpack_small(buf, [w[k] for k in small_names])):
            d_[k] = val

    return (loss, grad_x[None], *[g_shard[k] for k in WEIGHTS], *[delta[k] for k in WEIGHTS],
            *[new_m[k] for k in WEIGHTS], *[new_v[k] for k in WEIGHTS])
```

```python
import functools

import jax
import jax.numpy as jnp
from jax import lax
from jax.experimental import pallas as pl
from jax.experimental.pallas import tpu as pltpu

F32 = jnp.float32
BF16 = jnp.bfloat16

EPS = 1e-5
HEAD_DIM = 64
STATE = 128
GROUPS = 2
SHORT_K = 3
SSD_K = 4
LANES = 128
PAIR = LANES // HEAD_DIM
SCAN_CHUNK = 256
HALO = 16
N_CHIPS = 4
VMEM_LIMIT = 56 * 1024 * 1024

ADAM_LR = 0.001
ADAM_B1 = 0.9
ADAM_B2 = 0.999
ADAM_EPS = 1e-08
ADAM_WD = 0.01
ADAM_STEP = 10

MESH = pl.DeviceIdType.MESH


def _params(sem):
    return pltpu.CompilerParams(dimension_semantics=sem, vmem_limit_bytes=VMEM_LIMIT)


def _tile(n, cap, quantum):
    if n <= cap:
        return n
    best = None
    for t in range(quantum, cap + 1, quantum):
        if n % t == 0:
            best = t
    assert best is not None, (n, cap, quantum)
    return best


def _dot(a, b):
    return jnp.dot(a, b, preferred_element_type=F32)


def _dot_nt(a, b):
    return lax.dot_general(a, b, (((1,), (1,)), ((), ())), preferred_element_type=F32)


def _dot_tn(a, b):
    return lax.dot_general(a, b, (((0,), (0,)), ((), ())), preferred_element_type=F32)


def _dot_exact(a, b):
    return jnp.dot(a, b, precision=lax.Precision.HIGHEST, preferred_element_type=F32)


def _sigmoid(x):
    return pl.reciprocal(1.0 + jnp.exp(-x), approx=True)


def _softplus(x):
    return jnp.maximum(x, 0.0) + jnp.log(1.0 + jnp.exp(-jnp.abs(x)))


def _relu2(v):
    return jnp.square(jnp.maximum(v, 0.0))


class Rider:
    def __init__(self, ins, out_shapes, aliases, n_sems, copies):
        self.ins, self.out_shapes, self.aliases, self.n_sems, self.copies = ins, out_shapes, aliases, n_sems, copies


def _any_specs(n):
    return [pl.BlockSpec(memory_space=pl.ANY)] * n


def _ride(body, grid, in_specs, out_specs, out_shape, scratch, args, aliases, rider, sem, name):
    n_in, n_out, n_scr = len(in_specs), len(out_specs), len(scratch)
    if rider is None:
        outs = pl.pallas_call(
            body, name=name, grid=grid, in_specs=in_specs, out_specs=out_specs, out_shape=out_shape,
            scratch_shapes=scratch, input_output_aliases=aliases, compiler_params=_params(sem))(*args)
        return list(outs), []
    ri, ro = len(rider.ins), len(rider.out_shapes)
    last = tuple(g - 1 for g in grid)

    def wrapped(*refs):
        ins = refs[:n_in]
        r_ins = refs[n_in:n_in + ri]
        outs = refs[n_in + ri:n_in + ri + n_out]
        r_outs = refs[n_in + ri + n_out:n_in + ri + n_out + ro]
        scr = refs[n_in + ri + n_out + ro:n_in + ri + n_out + ro + n_scr]
        send_sems, recv_sems = refs[-2:]
        ids = [pl.program_id(a) for a in range(len(grid))]
        at_first = functools.reduce(jnp.logical_and, [i == 0 for i in ids])
        at_last = functools.reduce(jnp.logical_and, [i == e for i, e in zip(ids, last)])

        @pl.when(at_first)
        def _():
            for cp, _ in rider.copies(r_ins, r_outs, send_sems, recv_sems):
                cp.start()

        body(*ins, *outs, *scr)

        @pl.when(at_last)
        def _():
            for cp, landed in rider.copies(r_ins, r_outs, send_sems, recv_sems):
                cp.wait_send()
                landed.wait_recv()

    all_aliases = dict(aliases)
    all_aliases.update({n_in + a: n_out + b for a, b in rider.aliases.items()})
    outs = pl.pallas_call(
        wrapped, name=name, grid=grid, in_specs=list(in_specs) + _any_specs(ri),
        out_specs=list(out_specs) + _any_specs(ro), out_shape=list(out_shape) + list(rider.out_shapes),
        scratch_shapes=list(scratch) + [pltpu.SemaphoreType.DMA((rider.n_sems,)),
                                        pltpu.SemaphoreType.DMA((rider.n_sems,))],
        input_output_aliases=all_aliases, compiler_params=_params(sem))(*args, *rider.ins)
    return list(outs[:n_out]), list(outs[n_out:])


class _SemsFrom:
    def __init__(self, sems, first):
        self.sems, self.first = sems, first

    @property
    def at(self):
        return self

    def __getitem__(self, k):
        return self.sems.at[self.first + k]


def join_riders(a, b):
    ia, oa = len(a.ins), len(a.out_shapes)

    def copies(ins, outs, send_sems, recv_sems):
        return (a.copies(ins[:ia], outs[:oa], send_sems, recv_sems)
                + b.copies(ins[ia:], outs[oa:], _SemsFrom(send_sems, a.n_sems), _SemsFrom(recv_sems, a.n_sems)))

    aliases = dict(a.aliases)
    aliases.update({ia + i: oa + o for i, o in b.aliases.items()})
    return Rider(a.ins + b.ins, a.out_shapes + b.out_shapes, aliases, a.n_sems + b.n_sems, copies)


def run_rider(rider, name):
    ri, ro = len(rider.ins), len(rider.out_shapes)

    def body(*refs):
        send_sems, recv_sems = refs[-2:]
        pairs = rider.copies(refs[:ri], refs[ri:ri + ro], send_sems, recv_sems)
        for cp, _ in pairs:
            cp.start()
        for cp, landed in pairs:
            cp.wait_send()
            landed.wait_recv()

    return list(pl.pallas_call(
        body, name=name, in_specs=_any_specs(ri), out_specs=_any_specs(ro), out_shape=list(rider.out_shapes),
        scratch_shapes=[pltpu.SemaphoreType.DMA((rider.n_sems,)), pltpu.SemaphoreType.DMA((rider.n_sems,))],
        input_output_aliases=dict(rider.aliases),
        compiler_params=pltpu.CompilerParams(has_side_effects=True))(*rider.ins))


def norm_matmul(x, nw, w, layer, n, out_dtype, name, tail_block=None, rider=None):
    t, d = x.shape
    mxu_cols = 2 * LANES
    tn = _tile(n, 1536, mxu_cols if n % mxu_cols == 0 else LANES)
    if n % mxu_cols == 0 and tn < 1024 <= n:
        tn = _tile(n, 3072, mxu_cols)
    tm = _tile(t, 512 if tn > 1536 else 1024, 8)
    nj = n // tn

    def body(x_ref, nw_ref, w_ref, *rest):
        if tail_block is None:
            o_ref, h_ref = rest
        else:
            wt_ref, o_ref, h_ref, tail_ref = rest

        @pl.when(pl.program_id(1) == 0)
        def _():
            xf = x_ref[...]
            r = lax.rsqrt(jnp.mean(xf * xf, axis=-1, keepdims=True) + EPS)
            h_ref[...] = (xf * r * nw_ref[...]).astype(BF16)

        o_ref[...] = _dot(h_ref[...], w_ref[...]).astype(out_dtype)
        if tail_block is not None:
            @pl.when(pl.program_id(1) == nj - 1)
            def _():
                tail_ref[...] = _dot(h_ref[...], wt_ref[...])

    in_specs = [pl.BlockSpec((tm, d), lambda i, j: (i, 0)), pl.BlockSpec((1, d), lambda i, j: (0, 0)),
                pl.BlockSpec((None, d, tn), lambda i, j: (layer, 0, j))]
    out_specs = [pl.BlockSpec((tm, tn), lambda i, j: (i, j)), pl.BlockSpec((tm, d), lambda i, j: (i, 0))]
    out_shape = [jax.ShapeDtypeStruct((t, n), out_dtype), jax.ShapeDtypeStruct((t, d), BF16)]
    args = [x, nw, w]
    if tail_block is not None:
        in_specs.append(pl.BlockSpec((None, d, LANES), lambda i, j: (layer, 0, tail_block)))
        out_specs.append(pl.BlockSpec((tm, LANES), lambda i, j: (i, 0)))
        out_shape.append(jax.ShapeDtypeStruct((t, LANES), F32))
        args.append(w)
    return _ride(body, (t // tm, nj), in_specs, out_specs, out_shape, [], args, {}, rider,
                 ("parallel", "arbitrary"), name)


def matmul(lhs, w, layer, transposed, n, out_dtype, name, *, lhs_fn=None, residual=None, relu_gate=None,
           rider=None):
    t, k = lhs.shape
    tm = _tile(t, 512 if k > 2048 else 1024, 8)
    tn = _tile(n, 1024, LANES)
    staged = lhs.dtype != BF16 or lhs_fn is not None
    fn = lhs_fn if lhs_fn is not None else (lambda v: v)
    has_extra = residual is not None or relu_gate is not None
    dot = _dot_nt if transposed else _dot

    def body(*refs):
        a_ref, w_ref = refs[:2]
        extra = refs[2] if has_extra else None
        o_ref = refs[3] if has_extra else refs[2]
        if staged:
            s_ref = refs[-1]

            @pl.when(pl.program_id(1) == 0)
            def _():
                s_ref[...] = fn(a_ref[...].astype(F32)).astype(BF16)

            a_ref = s_ref
        acc = dot(a_ref[...], w_ref[...])
        if residual is not None:
            acc = acc + extra[...]
        if relu_gate is not None:
            acc = acc * (2.0 * jnp.maximum(extra[...].astype(F32), 0.0))
        o_ref[...] = acc.astype(out_dtype)

    if transposed:
        w_spec = pl.BlockSpec((None, tn, k), lambda i, j: (layer, j, 0))
    else:
        w_spec = pl.BlockSpec((None, k, tn), lambda i, j: (layer, 0, j))
    in_specs = [pl.BlockSpec((tm, k), lambda i, j: (i, 0)), w_spec]
    args = [lhs, w]
    if has_extra:
        in_specs.append(pl.BlockSpec((tm, tn), lambda i, j: (i, j)))
        args.append(residual if residual is not None else relu_gate)
    outs, extra = _ride(
        body, (t // tm, n // tn), in_specs, [pl.BlockSpec((tm, tn), lambda i, j: (i, j))],
        [jax.ShapeDtypeStruct((t, n), out_dtype)], [pltpu.VMEM((tm, k), BF16)] if staged else [], args, {},
        rider, ("parallel", "arbitrary"), name)
    return outs[0], extra


def matmul_normbwd(lhs, pieces, w, layer, x, nw, dres, name, rider=None):
    t, d = x.shape
    nl = len(lhs)
    tm = _tile(t, 512, 8)
    for off, width in pieces:
        assert off % width == 0

    def body(*refs):
        lrefs = refs[:nl]
        wrefs = refs[nl:2 * nl]
        x_ref, nw_ref, dres_ref, dx_ref, dnw_ref = refs[2 * nl:]
        dh = _dot_nt(lrefs[0][...].astype(BF16), wrefs[0][...])
        for a_ref, w_ref in zip(lrefs[1:], wrefs[1:]):
            dh = dh + _dot_nt(a_ref[...].astype(BF16), w_ref[...])
        xf = x_ref[...]
        r = lax.rsqrt(jnp.mean(xf * xf, axis=-1, keepdims=True) + EPS)
        nx = xf * r
        dn = dh * nw_ref[...]
        dx = r * (dn - nx * jnp.mean(dn * nx, axis=-1, keepdims=True))
        dx_ref[...] = dres_ref[...] + dx

        @pl.when(pl.program_id(0) == 0)
        def _():
            dnw_ref[...] = jnp.zeros_like(dnw_ref)

        dnw_ref[...] += jnp.sum(dh * nx, axis=0, keepdims=True)

    in_specs = [pl.BlockSpec((tm, width), (lambda blk: (lambda i: (i, blk)))(blk))
                for (_, blk), (_, width) in zip(lhs, pieces)]
    in_specs += [pl.BlockSpec((None, d, width), (lambda blk: (lambda i: (layer, 0, blk)))(off // width),
                              pipeline_mode=pl.Buffered(1))
                 for off, width in pieces]
    in_specs += [pl.BlockSpec((tm, d), lambda i: (i, 0)), pl.BlockSpec((1, d), lambda i: (0, 0)),
                 pl.BlockSpec((tm, d), lambda i: (i, 0))]
    return _ride(
        body, (t // tm,), in_specs,
        [pl.BlockSpec((tm, d), lambda i: (i, 0)), pl.BlockSpec((1, d), lambda i: (0, 0))],
        [jax.ShapeDtypeStruct((t, d), F32), jax.ShapeDtypeStruct((1, d), F32)], [],
        [*[a for a, _ in lhs], *([w] * nl), x, nw, dres], {}, rider, ("arbitrary",), name)


def matmul_tn(a, b, name, *, a_fn=None, by_chip=False, chip_sums=None, grad_sums=None):
    t, k = a.shape
    n = b.shape[1]
    tk = _tile(k, 1024, LANES)
    nn = n // N_CHIPS if by_chip else n
    tn = _tile(nn, 1536, 2 * LANES if nn % (2 * LANES) == 0 else LANES)
    tt = _tile(t, 1024, 8)
    nt = t // tt
    gn_ = n // tn
    steps = (k // tk) * gn_ * nt
    fn = a_fn if a_fn is not None else (lambda v: v)
    with_sums, chip_sums = chip_sums, list(chip_sums or ())
    for g, _ in chip_sums:
        if (g.shape[0] * g.shape[1] // 2) % (steps * HALO) or steps % g.shape[0]:
            return (matmul_tn(a, b, name, a_fn=a_fn, by_chip=by_chip),
                    [tuple(chip_sum(g_, r_, "chip_sum")) for g_, r_ in chip_sums])
    with_grads, grad_sums = grad_sums, list(grad_sums or ())
    assert not (chip_sums and grad_sums)
    for p32, *_ in grad_sums:
        if p32.shape[1] % (steps * HALO):
            return (matmul_tn(a, b, name, a_fn=a_fn, by_chip=by_chip),
                    [grad_sum(*item, "grad_sum") for item in grad_sums])
    ns = len(chip_sums)
    ng = len(grad_sums)
    n_gin = [N_CHIPS + 3 + (item[4] is not None) for item in grad_sums]
    step = lambda i, j, s: (i * gn_ + j) * nt + s

    def body(*refs):
        a_ref, b_ref = refs[:2]
        side_in = refs[2:2 + 3 * ns]
        grad_in = refs[2 + 3 * ns:2 + 3 * ns + sum(n_gin)]
        o_ref = refs[2 + 3 * ns + sum(n_gin)]
        side_out = refs[3 + 3 * ns + sum(n_gin):3 + 5 * ns + sum(n_gin)]
        grad_out = refs[3 + 5 * ns + sum(n_gin):3 + 5 * ns + sum(n_gin) + ng]
        acc_ref = refs[-1]

        @pl.when(pl.program_id(2) == 0)
        def _():
            acc_ref[...] = jnp.zeros_like(acc_ref)

        av = a_ref[...]
        if a_fn is not None:
            av = fn(av.astype(F32))
        acc_ref[...] += _dot_tn(av.astype(BF16), b_ref[...].astype(BF16))
        for q in range(ns):
            g0_ref, g1_ref, r_ref = side_in[3 * q:3 * q + 3]
            tot = jnp.where(lax.axis_index("c") == 0, g0_ref[...], g1_ref[...]) + r_ref[...]
            side_out[2 * q][...] = tot
            side_out[2 * q + 1][...] = tot.astype(BF16)
        pos = 0
        for q in range(ng):
            p = grad_in[pos:pos + N_CHIPS]
            r = grad_in[pos + N_CHIPS:pos + N_CHIPS + 3]
            pos += n_gin[q]
            x_, y_, _ = _coords()
            chip = 2 * x_ + y_
            own = jnp.where(chip == 0, p[0][...], jnp.where(chip == 1, p[1][...],
                                                           jnp.where(chip == 2, p[2][...], p[3][...])))
            grad_out[q][...] = (own + r[0][...].astype(F32) + r[1][...].astype(F32)
                                + r[2][...].astype(F32))[None]

        @pl.when(pl.program_id(2) == nt - 1)
        def _():
            o_ref[...] = acc_ref[...]

    if by_chip:
        per = n // N_CHIPS // tn
        out_specs = [pl.BlockSpec((None, tk, tn), lambda i, j, s: (j // per, i, j % per))]
        out_shape = [jax.ShapeDtypeStruct((N_CHIPS, k, n // N_CHIPS), F32)]
    else:
        out_specs = [pl.BlockSpec((tk, tn), lambda i, j, s: (i, j))]
        out_shape = [jax.ShapeDtypeStruct((k, n), F32)]
    in_specs = [pl.BlockSpec((tt, tk), lambda i, j, s: (s, i)), pl.BlockSpec((tt, tn), lambda i, j, s: (s, j))]
    args = [a, b]
    for g, recv in chip_sums:
        nch, r, c = g.shape
        r2 = r // 2
        rows = nch * r2 // steps
        per_chip = r2 // rows
        assert rows % HALO == 0 and r2 % rows == 0
        for half in range(2):
            in_specs.append(pl.BlockSpec(
                (rows, c), (lambda h: (lambda i, j, s: ((step(i, j, s) // per_chip) * 2 * per_chip + h * per_chip
                                                        + step(i, j, s) % per_chip, 0)))(half)))
        flat = pl.BlockSpec((rows, c), lambda i, j, s: (step(i, j, s), 0))
        in_specs.append(flat)
        args += [g.reshape(nch * r, c), g.reshape(nch * r, c), recv.reshape(nch * r2, c)]
        out_specs += [flat, flat]
        out_shape += [jax.ShapeDtypeStruct((nch * r2, c), F32), jax.ShapeDtypeStruct((nch * r2, c), BF16)]
    aliases = {}
    for p32, recv, layer, nl, buf in grad_sums:
        _, r2, c = p32.shape
        rows = r2 // steps
        slab = lambda kk: pl.BlockSpec((rows, c), lambda i, j, s: (kk * steps + step(i, j, s), 0))
        in_specs += [slab(kk) for kk in range(N_CHIPS)] + [slab(kk) for kk in range(3)]
        args += [p32.reshape(N_CHIPS * r2, c)] * N_CHIPS + [recv.reshape(3 * r2, c)] * 3
        if buf is not None:
            aliases[len(args)] = len(out_specs)
            in_specs.append(pl.BlockSpec(memory_space=pl.ANY))
            args.append(buf)
        out_specs.append(pl.BlockSpec((1, rows, c), (lambda ly: (lambda i, j, s: (ly, step(i, j, s), 0)))(layer)))
        out_shape.append(jax.ShapeDtypeStruct((nl, r2, c), F32))
    outs = pl.pallas_call(
        body, name=name, grid=(k // tk, gn_, nt), in_specs=in_specs, out_specs=out_specs, out_shape=out_shape,
        scratch_shapes=[pltpu.VMEM((tk, tn), F32)], input_output_aliases=aliases,
        compiler_params=_params(("parallel", "parallel", "arbitrary")),
    )(*args)
    if with_grads is not None:
        return outs[0], list(outs[1:])
    if with_sums is None:
        return outs[0]
    sums = [(outs[1 + 2 * q].reshape(g.shape[0], g.shape[1] // 2, g.shape[2]),
             outs[2 + 2 * q].reshape(g.shape[0], g.shape[1] // 2, g.shape[2])) for q, (g, _) in enumerate(chip_sums)]
    return outs[0], sums


def split_to_chips(pieces, cols, name):
    d = pieces[0].shape[0]
    widths = [p.shape[1] for p in pieces]
    w = cols // N_CHIPS
    tr = _tile(d, 256, 8)
    npc = len(pieces)

    def body(*refs):
        o_ref, row = refs[npc], refs[npc + 1]
        off = 0
        for r, n in zip(refs[:npc], widths):
            row[:, off:off + n] = r[...]
            off += n
        for j in range(N_CHIPS):
            o_ref[j] = row[:, j * w:(j + 1) * w]

    return pl.pallas_call(
        body, name=name, grid=(d // tr,),
        in_specs=[pl.BlockSpec((tr, n), lambda i: (i, 0)) for n in widths],
        out_specs=pl.BlockSpec((N_CHIPS, tr, w), lambda i: (0, i, 0)),
        out_shape=jax.ShapeDtypeStruct((N_CHIPS, d, w), F32),
        scratch_shapes=[pltpu.VMEM((tr, sum(widths)), F32)],
        compiler_params=_params(("parallel",)),
    )(*pieces)


def join_from_chips(g4, npad, name):
    _, nl, d, w = g4.shape
    tr = _tile(d, 256, HALO)

    def body(g_ref, o_ref):
        for j in range(N_CHIPS):
            o_ref[:, j * w:(j + 1) * w] = g_ref[j]
        o_ref[:, N_CHIPS * w:] = jnp.zeros((tr, npad - N_CHIPS * w), o_ref.dtype)

    return pl.pallas_call(
        body, name=name, grid=(nl, d // tr),
        in_specs=[pl.BlockSpec((N_CHIPS, None, tr, w), lambda l, i: (0, l, i, 0))],
        out_specs=pl.BlockSpec((None, tr, npad), lambda l, i: (l, i, 0)),
        out_shape=jax.ShapeDtypeStruct((nl, d, npad), g4.dtype),
        compiler_params=_params(("parallel", "parallel")),
    )(g4)


def conv_mixer_fwd(proj, kw, cw, out_cols, name):
    t = proj.shape[0]
    tm = _tile(t, 1024, HALO)
    tc = _tile(cw, 1024, LANES)
    nj = cw // tc
    hb = tm // HALO

    def body(ub_ref, uc_ref, uh_ref, ucp_ref, uhp_ref, kw_ref, y_ref):
        i = pl.program_id(0)
        taps = [kw_ref[pl.ds(k, 1), :] for k in range(SHORT_K)]
        row = lax.broadcasted_iota(jnp.int32, (8, tc), 0)
        vp = ucp_ref[...].astype(F32) * uhp_ref[...].astype(F32)

        def conv(block, before):
            acc = taps[SHORT_K - 1] * block
            for k in range(SHORT_K - 1):
                s = SHORT_K - 1 - k
                acc = acc + taps[k] * jnp.where(row >= s, pltpu.roll(block, s, 0), pltpu.roll(before, s, 0))
            return acc

        def strip(s, before):
            rows = pl.ds(pl.multiple_of(s * HALO, HALO), HALO)
            v = uc_ref[rows, :].astype(F32) * uh_ref[rows, :].astype(F32)
            top, bottom = v[0:8], v[8:HALO]
            cv = jnp.concatenate([conv(top, before), conv(bottom, top)], axis=0)
            y_ref[rows, :] = (ub_ref[rows, :].astype(F32) * cv).astype(BF16)
            return bottom

        lax.fori_loop(0, tm // HALO, strip, jnp.where(i > 0, vp[8:HALO], 0.0))

    prev = lambda off: (lambda i, j: (jnp.maximum(i * hb - 1, 0), off + j))
    return pl.pallas_call(
        body, name=name, grid=(t // tm, nj),
        in_specs=[pl.BlockSpec((tm, tc), lambda i, j: (i, j)),
                  pl.BlockSpec((tm, tc), lambda i, j: (i, nj + j)),
                  pl.BlockSpec((tm, tc), lambda i, j: (i, 2 * nj + j)),
                  pl.BlockSpec((HALO, tc), prev(nj)),
                  pl.BlockSpec((HALO, tc), prev(2 * nj)),
                  pl.BlockSpec((SHORT_K, tc), lambda i, j: (0, j))],
        out_specs=pl.BlockSpec((tm, tc), lambda i, j: (i, j)),
        out_shape=jax.ShapeDtypeStruct((t, out_cols), BF16),
        compiler_params=_params(("parallel", "parallel")),
    )(proj, proj, proj, proj, proj, kw)


def conv_mixer_bwd(proj, dy, kw, cw, name):
    t = proj.shape[0]
    tm = _tile(t, 1024, HALO)
    tc = cw
    nj = cw // tc
    hb = tm // HALO
    ni = t // tm
    last_hb = t // HALO - 1

    def body(ub_ref, uc_ref, uh_ref, dy_ref, ucp_ref, uhp_ref, ubn_ref, dyn_ref, kw_ref,
             du_ref, dkw_ref):
        i = pl.program_id(1)
        nstrips = tm // HALO
        taps = [kw_ref[pl.ds(k, 1), :] for k in range(SHORT_K)]
        row = lax.broadcasted_iota(jnp.int32, (8, tc), 0)
        vp = ucp_ref[...].astype(F32) * uhp_ref[...].astype(F32)
        dcvn = dyn_ref[...].astype(F32) * ubn_ref[...].astype(F32)

        def shifted(block, before, s):
            return jnp.where(row >= s, pltpu.roll(block, s, 0), pltpu.roll(before, s, 0))

        def lifted(block, after, s):
            return jnp.where(row < 8 - s, pltpu.roll(block, 8 - s, 0), pltpu.roll(after, 8 - s, 0))

        def down(s, carry):
            before, sums = carry
            rows = pl.ds(pl.multiple_of(s * HALO, HALO), HALO)
            v = uc_ref[rows, :].astype(F32) * uh_ref[rows, :].astype(F32)
            dyv = dy_ref[rows, :].astype(F32)
            dcv = dyv * ub_ref[rows, :].astype(F32)
            cvs = []
            sums = list(sums)
            for block, above, dcb in ((v[0:8], before, dcv[0:8]), (v[8:HALO], v[0:8], dcv[8:HALO])):
                moved = [shifted(block, above, SHORT_K - 1 - k) for k in range(SHORT_K - 1)] + [block]
                cvs.append(sum(taps[k] * moved[k] for k in range(SHORT_K)))
                sums = [sums[k] + dcb * moved[k] for k in range(SHORT_K)]
            du_ref[rows, 0:cw] = (dyv * jnp.concatenate(cvs, axis=0)).astype(BF16)
            return v[8:HALO], tuple(sums)

        zero = jnp.zeros((8, tc), F32)
        _, sums = lax.fori_loop(0, nstrips, down, (jnp.where(i > 0, vp[8:HALO], 0.0), (zero,) * SHORT_K))

        def up(n, after):
            rows = pl.ds(pl.multiple_of((nstrips - 1 - n) * HALO, HALO), HALO)
            uc = uc_ref[rows, :].astype(F32)
            uh = uh_ref[rows, :].astype(F32)
            dcv = dy_ref[rows, :].astype(F32) * ub_ref[rows, :].astype(F32)
            dvs = []
            for block, below in ((dcv[0:8], dcv[8:HALO]), (dcv[8:HALO], after)):
                dvs.append(taps[SHORT_K - 1] * block
                           + sum(taps[k] * lifted(block, below, SHORT_K - 1 - k) for k in range(SHORT_K - 1)))
            dv = jnp.concatenate(dvs, axis=0)
            du_ref[rows, cw:2 * cw] = (dv * uh).astype(BF16)
            du_ref[rows, 2 * cw:3 * cw] = (dv * uc).astype(BF16)
            return dcv[0:8]

        lax.fori_loop(0, nstrips, up, jnp.where(i < ni - 1, dcvn[0:8], 0.0))

        @pl.when(i == 0)
        def _():
            dkw_ref[...] = jnp.zeros_like(dkw_ref)

        for k in range(SHORT_K):
            dkw_ref[pl.ds(k, 1), :] += jnp.sum(sums[k], axis=0, keepdims=True)

    prev = lambda off: (lambda j, i: (jnp.maximum(i * hb - 1, 0), off + j))
    nxt = lambda off: (lambda j, i: (jnp.minimum((i + 1) * hb, last_hb), off + j))
    cur = lambda off: (lambda j, i: (i, off + j))
    return pl.pallas_call(
        body, name=name, grid=(nj, ni),
        in_specs=[pl.BlockSpec((tm, tc), cur(0)), pl.BlockSpec((tm, tc), cur(nj)),
                  pl.BlockSpec((tm, tc), cur(2 * nj)), pl.BlockSpec((tm, tc), cur(0)),
                  pl.BlockSpec((HALO, tc), prev(nj)), pl.BlockSpec((HALO, tc), prev(2 * nj)),
                  pl.BlockSpec((HALO, tc), nxt(0)), pl.BlockSpec((HALO, tc), nxt(0)),
                  pl.BlockSpec((SHORT_K, tc), lambda j, i: (0, j))],
        out_specs=[pl.BlockSpec((tm, 3 * cw), lambda j, i: (i, 0)),
                   pl.BlockSpec((SHORT_K, tc), lambda j, i: (0, j))],
        out_shape=[jax.ShapeDtypeStruct((t, 3 * cw), BF16), jax.ShapeDtypeStruct((SHORT_K, cw), F32)],
        compiler_params=_params(("parallel", "arbitrary")),
    )(proj, proj, proj, dy, proj, proj, proj, dy, kw)


def _head_column(mat, lane, h):
    return jnp.sum(jnp.where(lane == h, mat, 0.0), axis=-1, keepdims=True)


def _ssd_common(dt_raw_ref, dtb_ref, aneg_ref, cum_s, cumt_s, chunk):
    dt = _softplus(dt_raw_ref[...] + dtb_ref[...])
    al = dt * aneg_ref[...]
    ri = lax.broadcasted_iota(jnp.int32, (chunk, chunk), 0)
    ci = lax.broadcasted_iota(jnp.int32, (chunk, chunk), 1)
    cum = _dot_exact((ri >= ci).astype(F32), al)
    cum_s[...] = cum
    cumt_s[...] = cum.T
    return dt, cum, ri >= ci


EDGE = 16


def _shift_matrices(shift_s, chunk, kk, up):
    ri = lax.broadcasted_iota(jnp.int32, (chunk, chunk), 0)
    ci = lax.broadcasted_iota(jnp.int32, (chunk, chunk), 1)
    for k in range(kk - 1):
        s = kk - 1 - k
        shift_s[k] = ((ci - ri if up else ri - ci) == s).astype(BF16)


def _causal_conv(cur, head, kw_ref, b_ref, shift_s, kk):
    acc = b_ref[...] + kw_ref[pl.ds(kk - 1, 1), :] * cur.astype(F32)
    top = b_ref[...] + kw_ref[pl.ds(kk - 1, 1), :] * head[pl.ds(8, EDGE), :]
    for k in range(kk - 1):
        acc = acc + kw_ref[pl.ds(k, 1), :] * _dot(shift_s[k], cur)
        top = top + kw_ref[pl.ds(k, 1), :] * head[pl.ds(8 - (kk - 1) + k, EDGE), :]
    return acc, top


def ssd_fwd(proj, dt_raw, y_mix, kw_xs, kw_bc, b_xs, b_bc, dtb, aneg, dskip, normw, cw, si, name, rider=None):
    t = proj.shape[0]
    ch = min(SCAN_CHUNK, t)
    nc = t // ch
    npair = si // LANES
    ppg = npair // GROUPS
    gn = GROUPS * STATE
    gw = si // GROUPS
    assert cw == si and (3 * cw + 2 * si) % (2 * gn) == 0
    zblk = 3 * cw // si
    xsblk = zblk + 1
    bcblk = (3 * cw + 2 * si) // (2 * gn)

    def body(z_ref, xs_ref, bc_ref, dtr_ref, ymix_ref, kwx_ref, kwb_ref, bx_ref, bb_ref, dtb_ref, aneg_ref, dsk_ref,
             nw_ref, yb_ref, ys_ref, hs_ref, xcx_ref, xcb_ref,
             headx, headb, shift_s, xs_s, bc_s, h_s, gated_s, s_s, cum_s, cumt_s):
        del ymix_ref
        c = pl.program_id(0)

        @pl.when(c == 0)
        def _():
            h_s[...] = jnp.zeros_like(h_s)
            headx[0:8, :] = jnp.zeros((8, si), F32)
            headb[0:8, :] = jnp.zeros((8, 2 * gn), F32)
            _shift_matrices(shift_s, ch, SSD_K, up=False)

        for raw_ref, head, kw_ref, b_ref, pre_ref, act_s in ((xs_ref, headx, kwx_ref, bx_ref, xcx_ref, xs_s),
                                                           (bc_ref, headb, kwb_ref, bb_ref, xcb_ref, bc_s)):
            head[8:8 + EDGE, :] = raw_ref[0:EDGE, :].astype(F32)
            pre, top = _causal_conv(raw_ref[...], head, kw_ref, b_ref, shift_s, SSD_K)
            head[0:8, :] = raw_ref[ch - EDGE:ch, :].astype(F32)[EDGE - 8:EDGE]
            pre_ref[...] = pre.astype(BF16)
            pre_ref[0:EDGE, :] = top.astype(BF16)
            act_s[...] = (pre * _sigmoid(pre)).astype(act_s.dtype)
            act_s[0:EDGE, :] = (top * _sigmoid(top)).astype(act_s.dtype)

        dt, cum, tril = _ssd_common(dtr_ref, dtb_ref, aneg_ref, cum_s, cumt_s, ch)
        lane = lax.broadcasted_iota(jnp.int32, (ch, LANES), 1)
        lane1 = lax.broadcasted_iota(jnp.int32, (1, LANES), 1)
        low = lane < HEAD_DIM
        clast = cum_s[pl.ds(ch - 1, 1), :]

        for p in range(npair):
            g = p // ppg
            col = slice(p * LANES, (p + 1) * LANES)
            bg = bc_s[:, g * STATE:(g + 1) * STATE]
            cg = bc_s[:, gn + g * STATE:gn + (g + 1) * STATE]
            if p % ppg == 0:
                s_s[...] = _dot_nt(cg, bg)
            heads = (PAIR * p, PAIR * p + 1)
            ccol = [_head_column(cum, lane, h) for h in heads]
            dcol = [_head_column(dt, lane, h) for h in heads]
            cl = [jnp.sum(jnp.where(lane1 == h, clast, 0.0), axis=-1, keepdims=True) for h in heads]
            cum_px = jnp.where(low, ccol[0], ccol[1])
            dt_px = jnp.where(low, dcol[0], dcol[1])
            cl_px = jnp.where(lane1 < HEAD_DIM, cl[0], cl[1])
            xs_p = xs_s[:, col]
            xdt = xs_p * dt_px
            y = dsk_ref[:, col] * xs_p
            for hi, h in enumerate(heads):
                dec = jnp.exp(jnp.where(tril, ccol[hi] - cumt_s[pl.ds(h, 1), :], -jnp.inf))
                wm = (s_s[...] * dec).astype(BF16)
                xm = jnp.where(low if hi == 0 else jnp.logical_not(low), xdt, 0.0).astype(BF16)
                y = y + _dot(wm, xm)
            hp = h_s[p]
            hs_ref[0, p] = hp
            y = y + _dot(cg, hp.astype(BF16)) * jnp.exp(cum_px)
            st = _dot_tn(bg, (xdt * jnp.exp(cl_px - cum_px)).astype(BF16))
            h_s[p] = jnp.exp(cl_px) * hp + st
            ys_ref[:, col] = y.astype(BF16)
            zp = z_ref[:, col].astype(F32)
            gated_s[:, col] = y * zp * _sigmoid(zp)

        for g in range(GROUPS):
            col = slice(g * gw, (g + 1) * gw)
            gg = gated_s[:, col]
            r = lax.rsqrt(jnp.mean(gg * gg, axis=-1, keepdims=True) + EPS)
            yb_ref[:, col] = (gg * r * nw_ref[:, col]).astype(BF16)

    full = lambda shape: pl.BlockSpec(shape, lambda c: tuple(0 for _ in shape))
    return _ride(
        body, (nc,),
        [pl.BlockSpec((ch, si), lambda c: (c, zblk)),
         pl.BlockSpec((ch, si), lambda c: (c, xsblk)),
         pl.BlockSpec((ch, 2 * gn), lambda c: (c, bcblk)),
         pl.BlockSpec((ch, LANES), lambda c: (c, 0)),
         pl.BlockSpec(memory_space=pl.ANY),
         full((SSD_K, si)), full((SSD_K, 2 * gn)), full((1, si)), full((1, 2 * gn)),
         full((1, LANES)), full((1, LANES)), full((1, si)), full((1, si))],
        [pl.BlockSpec((ch, si), lambda c: (c, cw // si)),
         pl.BlockSpec((ch, si), lambda c: (c, 0)),
         pl.BlockSpec((1, npair, STATE, LANES), lambda c: (c, 0, 0, 0)),
         pl.BlockSpec((ch, si), lambda c: (c, 0)), pl.BlockSpec((ch, 2 * gn), lambda c: (c, 0))],
        [jax.ShapeDtypeStruct(y_mix.shape, BF16), jax.ShapeDtypeStruct((t, si), BF16),
         jax.ShapeDtypeStruct((nc, npair, STATE, LANES), F32),
         jax.ShapeDtypeStruct((t, si), BF16), jax.ShapeDtypeStruct((t, 2 * gn), BF16)],
        [pltpu.VMEM((8 + EDGE, si), F32), pltpu.VMEM((8 + EDGE, 2 * gn), F32),
         pltpu.VMEM((SSD_K - 1, ch, ch), BF16),
         pltpu.VMEM((ch, si), F32), pltpu.VMEM((ch, 2 * gn), BF16),
         pltpu.VMEM((npair, STATE, LANES), F32), pltpu.VMEM((ch, si), F32),
         pltpu.VMEM((ch, ch), F32), pltpu.VMEM((ch, LANES), F32), pltpu.VMEM((LANES, ch), F32)],
        [proj, proj, proj, dt_raw, y_mix, kw_xs, kw_bc, b_xs, b_bc, dtb, aneg, dskip, normw], {4: 0}, rider,
        ("arbitrary",), name)


def ssd_bwd(proj, dt_raw, ys, hsave, pre_xs, pre_bc, dy, kw_xs, kw_bc, dtb, aneg, dskip, normw, cw, si, name,
            rider=None):
    t = proj.shape[0]
    ch = min(SCAN_CHUNK, t)
    nc = t // ch
    npair = si // LANES
    ppg = npair // GROUPS
    gn = GROUPS * STATE
    gw = si // GROUPS
    zblk = 3 * cw // si
    xsblk = zblk + 1
    bcblk = (3 * cw + 2 * si) // (2 * gn)

    def body(z_ref, xs_ref, bc_ref, xcx_ref, xcb_ref, dtr_ref, ys_ref, hs_ref, dyb_ref,
             kwx_ref, kwb_ref, dtb_ref, aneg_ref, dsk_ref, nw_ref,
             dp_ref, ddt_ref, dkwx_ref, dkwb_ref, dbx_ref, dbb_ref, ddtb_ref, da_ref, ddsk_ref,
             dnw_ref,
             tailx, tailb, shift_s, xs_s, bc_s, dsx_s, dsb_s, dy_s, dxs_s, dbc_s, dh_s, s_s, ds_s,
             cum_s, cumt_s, dccol_s, dcrow_s, ddtcol_s, dcl_s):
        i = pl.program_id(0)

        @pl.when(i == 0)
        def _():
            dh_s[...] = jnp.zeros_like(dh_s)
            tailx[EDGE:EDGE + 8, :] = jnp.zeros((8, si), F32)
            tailb[EDGE:EDGE + 8, :] = jnp.zeros((8, 2 * gn), F32)
            _shift_matrices(shift_s, ch, SSD_K, up=True)
            for r in (dkwx_ref, dkwb_ref, dbx_ref, dbb_ref, ddtb_ref, da_ref, ddsk_ref, dnw_ref):
                r[...] = jnp.zeros_like(r)

        xc = xcx_ref[...].astype(F32)
        sg = _sigmoid(xc)
        xs_s[...] = xc * sg
        dsx_s[...] = sg * (1.0 + xc * (1.0 - sg))
        bcc = xcb_ref[...].astype(F32)
        sgb = _sigmoid(bcc)
        bc_s[...] = (bcc * sgb).astype(BF16)
        dsb_s[...] = sgb * (1.0 + bcc * (1.0 - sgb))

        dt, cum, tril = _ssd_common(dtr_ref, dtb_ref, aneg_ref, cum_s, cumt_s, ch)
        lane = lax.broadcasted_iota(jnp.int32, (ch, LANES), 1)
        lane1 = lax.broadcasted_iota(jnp.int32, (1, LANES), 1)
        low = lane < HEAD_DIM
        low1 = lane1 < HEAD_DIM
        clast = cum_s[pl.ds(ch - 1, 1), :]

        for g in range(GROUPS):
            col = slice(g * gw, (g + 1) * gw)
            ysf = ys_ref[:, col].astype(F32)
            zf = z_ref[:, col].astype(F32)
            sz = _sigmoid(zf)
            silz = zf * sz
            gg = ysf * silz
            r = lax.rsqrt(jnp.mean(gg * gg, axis=-1, keepdims=True) + EPS)
            nrm = gg * r
            dyb = dyb_ref[:, col].astype(F32)
            dnw_ref[:, col] += jnp.sum(dyb * nrm, axis=0, keepdims=True)
            dn = dyb * nw_ref[:, col]
            dgg = r * (dn - nrm * jnp.mean(dn * nrm, axis=-1, keepdims=True))
            dy_s[:, col] = dgg * silz
            dp_ref[:, col] = (dgg * ysf * (sz * (1.0 + zf * (1.0 - sz)))).astype(BF16)

        dccol_s[...] = jnp.zeros_like(dccol_s)
        dcrow_s[...] = jnp.zeros_like(dcrow_s)
        ddtcol_s[...] = jnp.zeros_like(ddtcol_s)
        dcl_s[...] = jnp.zeros_like(dcl_s)
        dbc_s[...] = jnp.zeros_like(dbc_s)

        for p in range(npair):
            g = p // ppg
            col = slice(p * LANES, (p + 1) * LANES)
            bcol = slice(g * STATE, (g + 1) * STATE)
            ccolg = slice(gn + g * STATE, gn + (g + 1) * STATE)
            bg = bc_s[:, bcol]
            cg = bc_s[:, ccolg]
            if p % ppg == 0:
                s_s[...] = _dot_nt(cg, bg)
                ds_s[...] = jnp.zeros_like(ds_s)
            heads = (PAIR * p, PAIR * p + 1)
            masks = (low, jnp.logical_not(low))
            masks1 = (low1, jnp.logical_not(low1))
            ccol = [_head_column(cum, lane, h) for h in heads]
            dcol = [_head_column(dt, lane, h) for h in heads]
            cl = [jnp.sum(jnp.where(lane1 == h, clast, 0.0), axis=-1, keepdims=True) for h in heads]
            cum_px = jnp.where(low, ccol[0], ccol[1])
            dt_px = jnp.where(low, dcol[0], dcol[1])
            cl_px = jnp.where(low1, cl[0], cl[1])
            e_px = jnp.exp(cum_px)
            dec_end = jnp.exp(cl_px - cum_px)
            gdec = jnp.exp(cl_px)
            xs_p = xs_s[:, col]
            xdt = xs_p * dt_px
            dyp = dy_s[:, col]
            hc = hs_ref[0, p]
            hcb = hc.astype(BF16)
            dhn = dh_s[p]
            dhnb = dhn.astype(BF16)

            ddsk_ref[:, col] += jnp.sum(dyp * xs_p, axis=0, keepdims=True)
            dxs_acc = dsk_ref[:, col] * dyp
            dye = dyp * e_px
            dyeb = dye.astype(BF16)
            dbc_s[:, ccolg] += _dot_nt(dyeb, hcb)
            dcum_lane = dye * _dot(cg, hcb)
            dh_from_y = _dot_tn(cg, dyeb)
            xd = xdt * dec_end
            dxd = _dot(bg, dhnb)
            dbc_s[:, bcol] += _dot_nt(xd.astype(BF16), dhnb)
            dxdt = dxd * dec_end
            t1 = dxd * xd
            dcum_lane = dcum_lane - t1
            dcl_lane = jnp.sum(t1, axis=0, keepdims=True) + jnp.sum(dhn * hc, axis=0, keepdims=True) * gdec
            dh_s[p] = gdec * dhn + dh_from_y
            xdtb = xdt.astype(BF16)
            for hi, h in enumerate(heads):
                dym = jnp.where(masks[hi], dyp, 0.0).astype(BF16)
                dw = _dot_nt(dym, xdtb)
                dec = jnp.exp(jnp.where(tril, ccol[hi] - cumt_s[pl.ds(h, 1), :], -jnp.inf))
                wm = s_s[...] * dec
                dxdt = dxdt + _dot_tn(wm.astype(BF16), dym)
                ds_s[...] += dw * dec
                gm = dw * wm
                rowsum = jnp.sum(gm, axis=-1, keepdims=True)
                lanesum = jnp.sum(jnp.where(masks[hi], dcum_lane, 0.0), axis=-1, keepdims=True)
                dccol_s[...] += jnp.where(lane == h, rowsum + lanesum, 0.0)
                dcrow_s[pl.ds(h, 1), :] = jnp.sum(gm, axis=0, keepdims=True)
                dcl_h = jnp.sum(jnp.where(masks1[hi], dcl_lane, 0.0), axis=-1, keepdims=True)
                dcl_s[...] += jnp.where(lane1 == h, dcl_h, 0.0)
            ddt_lane = dxdt * xs_p
            for hi, h in enumerate(heads):
                s = jnp.sum(jnp.where(masks[hi], ddt_lane, 0.0), axis=-1, keepdims=True)
                ddtcol_s[...] += jnp.where(lane == h, s, 0.0)
            dxs_s[:, col] = dxs_acc + dxdt * dt_px
            if p % ppg == ppg - 1:
                dsb = ds_s[...].astype(BF16)
                dbc_s[:, ccolg] += _dot(dsb, bg)
                dbc_s[:, bcol] += _dot_tn(dsb, cg)

        rowi = lax.broadcasted_iota(jnp.int32, (ch, LANES), 0)
        dcum = dccol_s[...] - dcrow_s[...].T + jnp.where(rowi == ch - 1, dcl_s[...], 0.0)
        ri = lax.broadcasted_iota(jnp.int32, (ch, ch), 0)
        ci = lax.broadcasted_iota(jnp.int32, (ch, ch), 1)
        dal = _dot_exact((ri <= ci).astype(F32), dcum)
        ddt = dal * aneg_ref[...] + ddtcol_s[...]
        da_ref[...] += jnp.sum(dal * dt, axis=0, keepdims=True)
        ddtr = ddt * _sigmoid(dtr_ref[...] + dtb_ref[...])
        ddt_ref[...] = ddtr
        ddtb_ref[...] += jnp.sum(ddtr, axis=0, keepdims=True)

        for (dpost, dsl, tail, raw_ref, kw_ref, dkw_ref, db_ref, out) in (
                (dxs_s, dsx_s, tailx, xs_ref, kwx_ref, dkwx_ref, dbx_ref, slice(si, 2 * si)),
                (dbc_s, dsb_s, tailb, bc_ref, kwb_ref, dkwb_ref, dbb_ref, slice(2 * si, 2 * si + 2 * gn))):
            dxc = dpost[...] * dsl[...]
            dxcb = dxc.astype(BF16)
            raw = raw_ref[...].astype(F32)
            raw_end = raw_ref[ch - EDGE:ch, :].astype(F32)
            tail[0:EDGE, :] = dxcb[ch - EDGE:ch].astype(F32)
            db_ref[...] += jnp.sum(dxc, axis=0, keepdims=True)
            draw = kw_ref[pl.ds(SSD_K - 1, 1), :] * dxc
            dkw_ref[pl.ds(SSD_K - 1, 1), :] += jnp.sum(dxc * raw, axis=0, keepdims=True)
            fix = jnp.zeros((EDGE, dxc.shape[1]), F32)
            for k in range(SSD_K - 1):
                moved = _dot(shift_s[k], dxcb)
                miss = tail[pl.ds(SSD_K - 1 - k, EDGE), :] - moved[ch - EDGE:ch]
                draw = draw + kw_ref[pl.ds(k, 1), :] * moved
                fix = fix + kw_ref[pl.ds(k, 1), :] * miss
                dkw_ref[pl.ds(k, 1), :] += (jnp.sum(moved * raw, axis=0, keepdims=True)
                                            + jnp.sum(miss * raw_end, axis=0, keepdims=True))
            dp_ref[:, out] = draw.astype(BF16)
            dp_ref[ch - EDGE:ch, out] = (draw[ch - EDGE:ch] + fix).astype(BF16)
            tail[EDGE:EDGE + 8, :] = dxcb[0:EDGE].astype(F32)[0:8]

    full = lambda shape: pl.BlockSpec(shape, lambda i: tuple(0 for _ in shape))
    rev = lambda blk: (lambda i: (nc - 1 - i, blk))
    small_in = [(SSD_K, si), (SSD_K, 2 * gn), (1, LANES), (1, LANES), (1, si), (1, si)]
    small = [(SSD_K, si), (SSD_K, 2 * gn), (1, si), (1, 2 * gn), (1, LANES), (1, LANES), (1, si), (1, si)]
    return _ride(
        body, (nc,),
        [pl.BlockSpec((ch, si), rev(zblk)), pl.BlockSpec((ch, si), rev(xsblk)),
         pl.BlockSpec((ch, 2 * gn), rev(bcblk)),
         pl.BlockSpec((ch, si), rev(0)), pl.BlockSpec((ch, 2 * gn), rev(0)),
         pl.BlockSpec((ch, LANES), rev(0)), pl.BlockSpec((ch, si), rev(0)),
         pl.BlockSpec((1, npair, STATE, LANES), lambda i: (nc - 1 - i, 0, 0, 0)),
         pl.BlockSpec((ch, si), rev(cw // si))] + [full(s) for s in small_in],
        [pl.BlockSpec((ch, 2 * si + 2 * gn), rev(0)), pl.BlockSpec((ch, LANES), rev(0))] + [full(s) for s in small],
        [jax.ShapeDtypeStruct((t, 2 * si + 2 * gn), BF16), jax.ShapeDtypeStruct((t, LANES), F32)]
        + [jax.ShapeDtypeStruct(s, F32) for s in small],
        [pltpu.VMEM((EDGE + 8, si), F32), pltpu.VMEM((EDGE + 8, 2 * gn), F32),
         pltpu.VMEM((SSD_K - 1, ch, ch), BF16),
         pltpu.VMEM((ch, si), F32), pltpu.VMEM((ch, 2 * gn), BF16),
         pltpu.VMEM((ch, si), F32), pltpu.VMEM((ch, 2 * gn), F32),
         pltpu.VMEM((ch, si), F32), pltpu.VMEM((ch, si), F32), pltpu.VMEM((ch, 2 * gn), F32),
         pltpu.VMEM((npair, STATE, LANES), F32),
         pltpu.VMEM((ch, ch), F32), pltpu.VMEM((ch, ch), F32),
         pltpu.VMEM((ch, LANES), F32), pltpu.VMEM((LANES, ch), F32),
         pltpu.VMEM((ch, LANES), F32), pltpu.VMEM((LANES, ch), F32),
         pltpu.VMEM((ch, LANES), F32), pltpu.VMEM((1, LANES), F32)],
        [proj, proj, proj, pre_xs, pre_bc, dt_raw, ys, hsave, dy, kw_xs, kw_bc, dtb, aneg, dskip, normw],
        {}, rider, ("arbitrary",), name)


def final_loss(x, nw, tgt, name):
    t, d = x.shape
    tm = _tile(t, 512, 8)

    def body(x_ref, nw_ref, t_ref, dx_ref, dnw_ref, ls_ref):
        xf = x_ref[...]
        r = lax.rsqrt(jnp.mean(xf * xf, axis=-1, keepdims=True) + EPS)
        nx = xf * r
        e = nx * nw_ref[...] - t_ref[...]
        dyv = e * (1.0 / d)
        dn = dyv * nw_ref[...]
        dx_ref[...] = r * (dn - nx * jnp.mean(dn * nx, axis=-1, keepdims=True))

        @pl.when(pl.program_id(0) == 0)
        def _():
            dnw_ref[...] = jnp.zeros_like(dnw_ref)
            ls_ref[...] = jnp.zeros_like(ls_ref)

        dnw_ref[...] += jnp.sum(dyv * nx, axis=0, keepdims=True)
        ls_ref[...] += jnp.sum(e * e, axis=0, keepdims=True) * (0.5 / d)

    return pl.pallas_call(
        body, name=name, grid=(t // tm,),
        in_specs=[pl.BlockSpec((tm, d), lambda i: (i, 0)), pl.BlockSpec((1, d), lambda i: (0, 0)),
                  pl.BlockSpec((tm, d), lambda i: (i, 0))],
        out_specs=[pl.BlockSpec((tm, d), lambda i: (i, 0)), pl.BlockSpec((1, d), lambda i: (0, 0)),
                   pl.BlockSpec((1, d), lambda i: (0, 0))],
        out_shape=[jax.ShapeDtypeStruct((t, d), F32), jax.ShapeDtypeStruct((1, d), F32),
                   jax.ShapeDtypeStruct((1, d), F32)],
        compiler_params=_params(("arbitrary",)),
    )(x, nw, tgt)


def _rows3(a):
    if a.ndim == 1:
        return a.reshape(1, 1, a.shape[0])
    if a.ndim == 2:
        return a.reshape(1, *a.shape)
    return a.reshape(-1, a.shape[-2], a.shape[-1])


def adamw(w, g, m, v, name):
    shape = w.shape
    views = [_rows3(a) for a in (w, g, m, v)]
    b, r, c = views[0].shape
    tr = _tile(r, 256, 16) if r % 16 == 0 else r

    def body(w_ref, g_ref, m_ref, v_ref, d_ref, nm_ref, nv_ref):
        g = g_ref[...]
        m = ADAM_B1 * m_ref[...] + (1.0 - ADAM_B1) * g
        v = ADAM_B2 * v_ref[...] + (1.0 - ADAM_B2) * (g * g)
        m_hat = m / (1.0 - ADAM_B1 ** ADAM_STEP)
        v_hat = v / (1.0 - ADAM_B2 ** ADAM_STEP)
        d_ref[...] = -ADAM_LR * (m_hat / (jnp.sqrt(v_hat) + ADAM_EPS) + ADAM_WD * w_ref[...])
        nm_ref[...] = m
        nv_ref[...] = v

    spec = pl.BlockSpec((1, tr, c), lambda i, j: (i, j, 0))
    outs = pl.pallas_call(
        body, name=name, grid=(b, r // tr), in_specs=[spec] * 4, out_specs=[spec] * 3,
        out_shape=[jax.ShapeDtypeStruct((b, r, c), F32)] * 3,
        compiler_params=_params(("parallel", "parallel")),
    )(*views)
    return [o.reshape(shape) for o in outs]


def adamw_halves(w, g_mine, g_theirs, m, v, name):
    nl, r, c = w.shape
    r2 = r // 2
    tr = _tile(r2, 256, 16)
    nb = r2 // tr

    def body(w_ref, gm_ref, gt_ref, m_ref, v_ref, g_ref, d_ref, nm_ref, nv_ref):
        mine = (pl.program_id(1) // nb) == lax.axis_index("c")
        g = jnp.where(mine, gm_ref[...], gt_ref[...])
        m = ADAM_B1 * m_ref[...] + (1.0 - ADAM_B1) * g
        v = ADAM_B2 * v_ref[...] + (1.0 - ADAM_B2) * (g * g)
        m_hat = m / (1.0 - ADAM_B1 ** ADAM_STEP)
        v_hat = v / (1.0 - ADAM_B2 ** ADAM_STEP)
        g_ref[...] = g
        d_ref[...] = -ADAM_LR * (m_hat / (jnp.sqrt(v_hat) + ADAM_EPS) + ADAM_WD * w_ref[...])
        nm_ref[...] = m
        nv_ref[...] = v

    whole = pl.BlockSpec((1, tr, c), lambda l, i: (l, i, 0))
    half = pl.BlockSpec((1, tr, c), lambda l, i: (l, i % nb, 0))
    return pl.pallas_call(
        body, name=name, grid=(nl, 2 * nb), in_specs=[whole, half, half, whole, whole], out_specs=[whole] * 4,
        out_shape=[jax.ShapeDtypeStruct((nl, r, c), F32)] * 4,
        compiler_params=_params(("parallel", "parallel")),
    )(w, g_mine, g_theirs, m, v)


def _coords():
    return lax.axis_index("x"), lax.axis_index("y"), lax.axis_index("c")


def _ici_peers(x, y):
    chips = [(1 - x, y), (x, 1 - y), (1 - x, 1 - y)]
    return chips, [2 * cx + cy for cx, cy in chips]


def _place(ref, how, chip, layers, per):
    if how == "lead":
        return ref.at[chip, layers]
    start = pl.multiple_of(chip * per, per)
    if how == "rows":
        return ref.at[layers, pl.ds(start, per), :]
    return ref.at[layers, :, pl.ds(start, per)]


def gather_weights(shards, hows, name):
    na = len(shards)
    out_shape = []
    for s, how in zip(shards, hows):
        assert s.shape[0] % 2 == 0
        if how == "lead":
            shp = (N_CHIPS, *s.shape)
        elif how == "rows":
            shp = (s.shape[0], N_CHIPS * s.shape[1], s.shape[2])
        else:
            shp = (s.shape[0], s.shape[1], N_CHIPS * s.shape[2])
        out_shape.append(jax.ShapeDtypeStruct(shp, s.dtype))

    def body(*refs):
        ins = refs[:na]
        outs = refs[na:2 * na]
        send_sems, recv_sems = refs[2 * na:]
        x, y, c = _coords()
        me = 2 * x + y
        chips, chip_ids = _ici_peers(x, y)
        sibling = (x, y, 1 - c)

        def dst(a, chip, layers):
            per = {"lead": 0, "rows": ins[a].shape[1], "cols": ins[a].shape[-1]}[hows[a]]
            return _place(outs[a], hows[a], chip, layers, per)

        def copy(a, k, src, dst_ref, to):
            return pltpu.make_async_remote_copy(
                src_ref=src, dst_ref=dst_ref, send_sem=send_sems.at[7 * a + k], recv_sem=recv_sems.at[7 * a + k],
                device_id=to, device_id_type=MESH)

        started = []
        halves = []
        for a in range(na):
            nl = ins[a].shape[0]
            hl = nl // 2
            mine = pl.ds(c * hl, hl)
            theirs = pl.ds((1 - c) * hl, hl)
            halves.append((mine, theirs))
            for k in range(3):
                cp = copy(a, k, ins[a].at[mine], dst(a, me, mine), (*chips[k], c))
                cp.start()
                started.append(cp)
            own = copy(a, 6, ins[a], dst(a, me, pl.ds(0, nl)), sibling)
            own.start()
            started.append(own)
        for a in range(na):
            mine, _ = halves[a]
            for k in range(3):
                landed = dst(a, chip_ids[k], mine)
                copy(a, k, landed, landed, (*chips[k], c)).wait_recv()
                fw = copy(a, 3 + k, landed, landed, sibling)
                fw.start()
                started.append(fw)
        for a in range(na):
            _, theirs = halves[a]
            for k in range(3):
                got = dst(a, chip_ids[k], theirs)
                copy(a, 3 + k, got, got, sibling).wait_recv()
            whole = dst(a, me, pl.ds(0, ins[a].shape[0]))
            copy(a, 6, whole, whole, sibling).wait_recv()
        for cp in started:
            cp.wait_send()

    return pl.pallas_call(
        body, name=name, in_specs=_any_specs(na), out_specs=_any_specs(na), out_shape=out_shape,
        scratch_shapes=[pltpu.SemaphoreType.DMA((7 * na,)), pltpu.SemaphoreType.DMA((7 * na,))],
        compiler_params=pltpu.CompilerParams(has_side_effects=True),
    )(*shards)


def _remote(src, dst, send_sems, recv_sems, k, to):
    return pltpu.make_async_remote_copy(src_ref=src, dst_ref=dst, send_sem=send_sems.at[k], recv_sem=recv_sems.at[k],
                                        device_id=to, device_id_type=MESH)


LAYER_HOW = ("lead", "rows", "cols", "rows")


def _layer_place(ref, how, chip, shard_shape, start, size):
    r, c = shard_shape
    if how == "lead":
        return ref.at[chip, :, pl.ds(start, size), :]
    if how == "rows":
        return ref.at[:, pl.ds(pl.multiple_of(chip * r + start, HALO), size), :]
    return ref.at[:, pl.ds(start, size), pl.ds(pl.multiple_of(chip * c, LANES), c)]


def weight_rider_ici(shards, hows, layer):
    shapes = [tuple(s.shape[1:]) for s in shards]
    out_shapes = []
    for (r, c), how, s in zip(shapes, hows, shards):
        shp = {"lead": (N_CHIPS, 1, r, c), "rows": (1, N_CHIPS * r, c), "cols": (1, r, N_CHIPS * c)}[how]
        out_shapes.append(jax.ShapeDtypeStruct(shp, s.dtype))

    def copies(ins, outs, send_sems, recv_sems):
        x, y, c = _coords()
        me = 2 * x + y
        chips, chip_ids = _ici_peers(x, y)
        sibling = (x, y, 1 - c)
        pairs = []
        for a, (shape, how) in enumerate(zip(shapes, hows)):
            half = shape[0] // 2
            mine = pl.multiple_of(c * half, HALO)
            src = ins[a].at[pl.ds(layer, 1)]
            for k in range(3):
                to = (*chips[k], c)
                land = _layer_place(outs[a], how, chip_ids[k], shape, mine, half)
                pairs.append((_remote(src.at[:, pl.ds(mine, half), :], _layer_place(outs[a], how, me, shape, mine, half),
                                      send_sems, recv_sems, 4 * a + k, to),
                              _remote(land, land, send_sems, recv_sems, 4 * a + k, to)))
            whole = _layer_place(outs[a], how, me, shape, 0, shape[0])
            pairs.append((_remote(src, whole, send_sems, recv_sems, 4 * a + 3, sibling),
                          _remote(whole, whole, send_sems, recv_sems, 4 * a + 3, sibling)))
        return pairs

    return Rider(list(shards), out_shapes, {}, 4 * len(shards), copies)


def weight_rider_d2d(bufs, shapes, hows):
    def copies(ins, outs, send_sems, recv_sems):
        x, y, c = _coords()
        _, chip_ids = _ici_peers(x, y)
        sibling = (x, y, 1 - c)
        pairs = []
        for a, (shape, how) in enumerate(zip(shapes, hows)):
            half = shape[0] // 2
            mine = pl.multiple_of(c * half, HALO)
            theirs = pl.multiple_of((1 - c) * half, HALO)
            for k in range(3):
                land = _layer_place(outs[a], how, chip_ids[k], shape, theirs, half)
                pairs.append((_remote(_layer_place(ins[a], how, chip_ids[k], shape, mine, half),
                                      _layer_place(outs[a], how, chip_ids[k], shape, mine, half),
                                      send_sems, recv_sems, 3 * a + k, sibling),
                              _remote(land, land, send_sems, recv_sems, 3 * a + k, sibling)))
        return pairs

    return Rider(list(bufs), [jax.ShapeDtypeStruct(b.shape, b.dtype) for b in bufs],
                 {a: a for a in range(len(bufs))}, 3 * len(bufs), copies)


def grads_rider_sibling(arrs):
    def copies(ins, outs, send_sems, recv_sems):
        x, y, c = _coords()
        sibling = (x, y, 1 - c)
        pairs = []
        for a in range(len(arrs)):
            r2 = ins[a].shape[1] // 2
            src = ins[a].at[:, pl.ds(pl.multiple_of((1 - c) * r2, 8), r2), :]
            pairs.append((_remote(src, outs[a], send_sems, recv_sems, a, sibling),
                          _remote(outs[a], outs[a], send_sems, recv_sems, a, sibling)))
        return pairs

    return Rider(list(arrs), [jax.ShapeDtypeStruct((a.shape[0], a.shape[1] // 2, a.shape[2]), a.dtype) for a in arrs],
                 {}, len(arrs), copies)


def chip_sum(g, recv, name):
    nch, r, c = g.shape
    r2 = r // 2
    tr = _tile(r2, 256, 16)
    nb = r2 // tr

    def body(g0_ref, g1_ref, r_ref, o32_ref, o16_ref):
        s = jnp.where(lax.axis_index("c") == 0, g0_ref[...], g1_ref[...]) + r_ref[...]
        o32_ref[...] = s
        o16_ref[...] = s.astype(BF16)

    here = pl.BlockSpec((1, tr, c), lambda i, j: (i, j, 0))
    return pl.pallas_call(
        body, name=name, grid=(nch, nb),
        in_specs=[here, pl.BlockSpec((1, tr, c), lambda i, j: (i, nb + j, 0)), here],
        out_specs=[here, here],
        out_shape=[jax.ShapeDtypeStruct((nch, r2, c), F32), jax.ShapeDtypeStruct((nch, r2, c), BF16)],
        compiler_params=_params(("parallel", "parallel")),
    )(g, g, recv)


def grads_rider_chips(arrs):
    def copies(ins, outs, send_sems, recv_sems):
        x, y, c = _coords()
        chips, chip_ids = _ici_peers(x, y)
        pairs = []
        for a in range(len(arrs)):
            for k in range(3):
                to = (*chips[k], c)
                pairs.append((_remote(ins[a].at[chip_ids[k]], outs[a].at[k], send_sems, recv_sems, 3 * a + k, to),
                              _remote(outs[a].at[k], outs[a].at[k], send_sems, recv_sems, 3 * a + k, to)))
        return pairs

    return Rider(list(arrs), [jax.ShapeDtypeStruct((3, *a.shape[1:]), a.dtype) for a in arrs], {}, 3 * len(arrs),
                 copies)


def grad_sum(p32, recv, layer, nl, buf, name):
    _, r2, c = p32.shape
    tr = _tile(r2, 256, 16)
    nb = r2 // tr

    def body(p0_ref, p1_ref, p2_ref, p3_ref, r0_ref, r1_ref, r2_ref, *rest):
        o_ref = rest[-1]
        x, y, _ = _coords()
        chip = 2 * x + y
        own = jnp.where(chip == 0, p0_ref[...], jnp.where(chip == 1, p1_ref[...],
                                                        jnp.where(chip == 2, p2_ref[...], p3_ref[...])))
        o_ref[...] = own + r0_ref[...].astype(F32) + r1_ref[...].astype(F32) + r2_ref[...].astype(F32)

    slot = lambda k: pl.BlockSpec((1, tr, c), lambda j: (k, j, 0))
    in_specs = [slot(k) for k in range(N_CHIPS)] + [slot(k) for k in range(3)]
    args = [p32] * N_CHIPS + [recv] * 3
    aliases = {}
    if buf is not None:
        in_specs.append(pl.BlockSpec(memory_space=pl.ANY))
        args.append(buf)
        aliases = {len(args) - 1: 0}
    return pl.pallas_call(
        body, name=name, grid=(nb,), in_specs=in_specs,
        out_specs=pl.BlockSpec((1, tr, c), lambda j: (layer, j, 0)),
        out_shape=jax.ShapeDtypeStruct((nl, r2, c), F32),
        input_output_aliases=aliases,
        compiler_params=_params(("parallel",)),
    )(*args)


def grads_rider_exchange(bufs):
    def copies(ins, outs, send_sems, recv_sems):
        x, y, c = _coords()
        sibling = (x, y, 1 - c)
        return [(_remote(ins[a], outs[a], send_sems, recv_sems, a, sibling),
                 _remote(outs[a], outs[a], send_sems, recv_sems, a, sibling)) for a in range(len(bufs))]

    return Rider(list(bufs), [jax.ShapeDtypeStruct(b.shape, b.dtype) for b in bufs], {}, len(bufs), copies)


def allreduce_small(buf, name):
    r, cdim = buf.shape

    def body(x_ref, o_ref, gath, send_sems, recv_sems):
        x, y, c = _coords()
        me, sibling = (x, y, c), (x, y, 1 - c)
        chips, _ = _ici_peers(x, y)

        def slot(px, py, pc):
            return gath.at[4 * px + 2 * py + pc]

        def copy(k, block, to, src=None):
            return pltpu.make_async_remote_copy(
                src_ref=slot(*block) if src is None else src, dst_ref=slot(*block),
                send_sem=send_sems.at[k], recv_sem=recv_sems.at[k], device_id=to, device_id_type=MESH)

        gath[4 * x + 2 * y + c] = x_ref[...]
        first = [copy(0, me, sibling, src=x_ref)]
        first += [copy(1 + j, me, (*chip, c), src=x_ref) for j, chip in enumerate(chips)]
        for cp in first:
            cp.start()
        passed = [copy(4 + j, (*chip, c), sibling) for j, chip in enumerate(chips)]
        for j, chip in enumerate(chips):
            copy(1 + j, (*chip, c), me).wait_recv()
            passed[j].start()
        copy(0, sibling, me).wait_recv()
        for j, chip in enumerate(chips):
            copy(4 + j, (*chip, 1 - c), me).wait_recv()
        for cp in first + passed:
            cp.wait_send()
        acc = gath[0]
        for d in range(1, 8):
            acc = acc + gath[d]
        o_ref[...] = acc

    return pl.pallas_call(
        body, name=name,
        in_specs=[pl.BlockSpec(memory_space=pltpu.VMEM)], out_specs=pl.BlockSpec(memory_space=pltpu.VMEM),
        out_shape=jax.ShapeDtypeStruct((r, cdim), F32),
        scratch_shapes=[pltpu.VMEM((8, r, cdim), F32), pltpu.SemaphoreType.DMA((7,)), pltpu.SemaphoreType.DMA((7,))],
        compiler_params=pltpu.CompilerParams(has_side_effects=True),
    )(buf)


def _expand_heads(v):
    return jnp.repeat(v.astype(F32), HEAD_DIM).reshape(1, -1)


def _pad_lanes(v):
    return jnp.pad(v.astype(F32), (0, LANES - v.shape[0])).reshape(1, LANES)


def local_step(x, tgt, p, comm, cols):
    nl = p["norm_mix_w"].shape[0]
    d = x.shape[1]
    cw = p["short_conv_w"].shape[2]
    si = p["ssd_norm_w"].shape[1]
    ff, npad = comm.ff, comm.npad
    nh = si // HEAD_DIM
    gn = GROUPS * STATE
    dt_off = 3 * cw + si + si + 2 * gn
    assert cols == dt_off + nh and nh <= LANES and dt_off % LANES == 0 and npad == dt_off + LANES
    pieces = [(0, cw), (cw, cw), (2 * cw, cw), (3 * cw, si), (3 * cw + si, si), (3 * cw + 2 * si, 2 * gn),
              (dt_off, LANES)]

    saved = []
    for l in range(nl):
        nw1 = p["norm_mix_w"][l].reshape(1, d)
        nw2 = p["norm_mlp_w"][l].reshape(1, d)
        kw3 = p["short_conv_w"][l]
        kwx, kwb = p["ssd_conv_w"][l][:, :si], p["ssd_conv_w"][l][:, si:]
        bx, bb = p["ssd_conv_b"][l][:si].reshape(1, si), p["ssd_conv_b"][l][si:].reshape(1, 2 * gn)
        dtb = _pad_lanes(p["dt_bias"][l])
        aneg = _pad_lanes(-jnp.exp(p["a_log"][l]))
        dsk = _expand_heads(p["d_skip"][l])
        snw = p["ssd_norm_w"][l].reshape(1, si)
        ssd_args = (kwx, kwb, bx, bb, dtb, aneg, dsk, snw)

        w_in = comm.weight(l, "w_in")
        (proj, h, dt_raw), sent = norm_matmul(x, nw1, w_in, 0, dt_off, BF16, "in_proj", tail_block=dt_off // LANES,
                                              rider=comm.rider("in_proj", l))
        comm.done("in_proj", l, sent)
        y_mix = conv_mixer_fwd(proj, kw3, cw, cw + si, "conv_mixer_fwd")
        (y_mix, *ssd_saved), sent = ssd_fwd(proj, dt_raw, y_mix, *ssd_args, cw, si, "ssd_fwd",
                                            rider=comm.rider("ssd_fwd", l))
        comm.done("ssd_fwd", l, sent)
        x2, _ = matmul(y_mix, comm.weight(l, "w_out"), 0, False, d, F32, "out_proj", residual=x)
        (up, h2), sent = norm_matmul(x2, nw2, comm.weight(l, "w_up"), 0, ff, BF16, "up_proj",
                                     rider=comm.rider("up_proj", l))
        comm.done("up_proj", l, sent)
        x3, sent = matmul(up, comm.weight(l, "w_down"), 0, False, d, F32, "down_proj", lhs_fn=_relu2, residual=x2,
                          rider=comm.rider("down_proj", l))
        comm.done("down_proj", l, sent)
        saved.append((x, h, proj, dt_raw, y_mix, ssd_saved, x2, h2, up, nw1, nw2, kw3, ssd_args))
        x = x3

    dx, dwf, lvec = final_loss(x, p["final_norm_w"].reshape(1, d), tgt, "final_loss")
    loss = jnp.sum(lvec)

    names = ("norm_mix_w", "short_conv_w", "ssd_conv_w", "ssd_conv_b", "dt_bias", "a_log", "d_skip",
             "ssd_norm_w", "norm_mlp_w")
    grads = {k: [None] * nl for k in names}
    for l in reversed(range(nl)):
        x0, h, proj, dt_raw, y_mix, ssd_saved, x2, h2, up, nw1, nw2, kw3, ssd_args = saved[l]
        kwx, kwb, _, _, dtb, aneg, dsk, snw = ssd_args
        dup, sent = matmul(dx, comm.weight(l, "w_down"), 0, True, ff, BF16, "down_bwd", relu_gate=up,
                           rider=comm.rider("down_bwd", l))
        comm.done("down_bwd", l, sent)
        g_down, sums = matmul_tn(up, dx, "down_wgrad", a_fn=_relu2, chip_sums=comm.side("down_wgrad", l))
        comm.side_done("down_wgrad", l, sums)
        comm.take_gradient(l, "w_down", g_down)
        (dx2, dnw2), _ = matmul_normbwd([(dup, 0)], [(0, ff)], comm.weight(l, "w_up"), 0, x2, nw2, dx, "up_bwd")
        comm.take_gradient(l, "w_up", matmul_tn(h2, dup, "up_wgrad", by_chip=True))
        grads["norm_mlp_w"][l] = dnw2.reshape(d)
        dy, sent = matmul(dx2, comm.weight(l, "w_out"), 0, True, cw + si, BF16, "out_bwd",
                          rider=comm.rider("out_bwd", l))
        comm.done("out_bwd", l, sent)
        g_out, sums = matmul_tn(y_mix, dx2, "out_wgrad", chip_sums=comm.side("out_wgrad", l))
        comm.side_done("out_wgrad", l, sums)
        comm.take_gradient(l, "w_out", g_out)
        du, dkw3 = conv_mixer_bwd(proj, dy, kw3, cw, "conv_mixer_bwd")
        (dssd, ddt, dkwx, dkwb, dbx, dbb, ddtb, da, ddsk, dsnw), sent = ssd_bwd(
            proj, dt_raw, *ssd_saved, dy, kwx, kwb, dtb, aneg, dsk, snw, cw, si, "ssd_bwd",
            rider=comm.rider("ssd_bwd", l))
        comm.done("ssd_bwd", l, sent)
        views = [(du, 0), (du, 1), (du, 2), (dssd, 0), (dssd, 1), (dssd, 2 * si // (2 * gn)), (ddt, 0)]
        (dxl, dnw1), sent = matmul_normbwd(views, pieces, comm.weight(l, "w_in"), 0, x0, nw1, dx2, "in_bwd",
                                           rider=comm.rider("in_bwd", l))
        comm.done("in_bwd", l, sent)
        parts = []
        for i, dp in enumerate((du, dssd, ddt)):
            part, bufs = matmul_tn(h, dp, "in_wgrad_%d" % i, grad_sums=comm.reductions("in_wgrad_%d" % i))
            comm.reductions_done("in_wgrad_%d" % i, bufs)
            parts.append(part)
        comm.take_gradient(l, "w_in", split_to_chips(parts, cols, "in_wgrad_split"))
        grads["norm_mix_w"][l] = dnw1.reshape(d)
        grads["short_conv_w"][l] = dkw3
        grads["ssd_conv_w"][l] = jnp.concatenate([dkwx, dkwb], axis=1)
        grads["ssd_conv_b"][l] = jnp.concatenate([dbx, dbb], axis=1).reshape(-1)
        grads["dt_bias"][l] = ddtb[0, :nh]
        grads["a_log"][l] = da[0, :nh] * aneg[0, :nh]
        grads["d_skip"][l] = jnp.sum(ddsk.reshape(nh, HEAD_DIM), axis=1)
        grads["ssd_norm_w"][l] = dsnw.reshape(si)
        dx = dxl

    grads = {k: jnp.stack(v) for k, v in grads.items()}
    grads["final_norm_w"] = dwf.reshape(d)
    return loss, dx, grads


BIG = ("w_in", "w_out", "w_up", "w_down")
SMALL_SHARDED = ("short_conv_w", "ssd_conv_w")
SMALL_REPL = ("norm_mix_w", "ssd_conv_b", "dt_bias", "a_log", "d_skip", "ssd_norm_w", "norm_mlp_w", "final_norm_w")
WEIGHTS = ("norm_mix_w", "w_in", "short_conv_w", "ssd_conv_w", "ssd_conv_b", "dt_bias", "a_log", "d_skip",
           "ssd_norm_w", "w_out", "norm_mlp_w", "w_up", "w_down", "final_norm_w")
SMALL_COLS = 1024


def _pack_small(named):
    flat = jnp.concatenate([v.reshape(-1).astype(F32) for v in named])
    n = flat.shape[0]
    rows = -(-n // SMALL_COLS)
    rows = -(-rows // 8) * 8
    return jnp.pad(flat, (0, rows * SMALL_COLS - n)).reshape(rows, SMALL_COLS)


def _unpack_small(buf, like):
    flat = buf.reshape(-1)
    out, off = [], 0
    for v in like:
        out.append(flat[off:off + v.size].reshape(v.shape))
        off += v.size
    return out


class ChipComm:
    IO = ("w_in", "w_out")
    MLP = ("w_up", "w_down")
    FIRST = ("w_in",)
    EARLY = ("w_up", "w_down", "w_out")
    LATE = ("w_out",)

    def __init__(self, shards, nl, npad):
        self.shards, self.nl, self.npad = shards, nl, npad
        self.how = dict(zip(BIG, LAYER_HOW))
        self.ff = N_CHIPS * shards["w_up"].shape[2]
        self.w = {}
        self.landed = {}
        self.grad = {}
        self.sums = {}
        self.from_sibling = {}
        self.from_chips = {}
        self.bufs = {k: None for k in BIG}
        first = run_rider(self._ici(self.FIRST, 0), "gather_first_ici")
        self._gathered(self.FIRST, 0, run_rider(self._d2d(self.FIRST, first), "gather_first_d2d"))

    def _ici(self, group, l):
        return weight_rider_ici([self.shards[k] for k in group], [self.how[k] for k in group], l)

    def _d2d(self, group, landed):
        return weight_rider_d2d(landed, [tuple(self.shards[k].shape[1:]) for k in group],
                                [self.how[k] for k in group])

    def _gathered(self, group, l, arrays):
        for k, g in zip(group, arrays):
            self.w[(l, k)] = join_from_chips(g, self.npad, "w_in_join") if k == "w_in" else g

    def _to_sibling(self, group, l):
        return grads_rider_sibling([self.grad[(l, k)] for k in group])

    def _summed(self, group, l, from_sibling):
        self.sums[group] = (l, [chip_sum(self.grad.pop((l, k)), r, "chip_sum") for k, r in zip(group, from_sibling)])

    def side(self, point, l):
        group = self.IO if point == "down_wgrad" else self.MLP
        if group not in self.from_sibling:
            return ()
        layer, parts = self.from_sibling[group]
        return [(self.grad[(layer, k)], r) for k, r in zip(group, parts)]

    def side_done(self, point, l, sums):
        group = self.IO if point == "down_wgrad" else self.MLP
        if sums:
            layer, _ = self.from_sibling.pop(group)
            for k in group:
                del self.grad[(layer, k)]
            self.sums[group] = (layer, sums)

    def _to_chips(self, groups):
        return grads_rider_chips([s[1] for group in groups for s in self.sums[group][1]])

    def _reduced(self, group, from_chips):
        l, sums = self.sums.pop(group)
        for k, s, r in zip(group, sums, from_chips):
            self.bufs[k] = grad_sum(s[0], r, l, self.nl, self.bufs[k], "grad_sum")

    def weight(self, l, name):
        return self.w[(l, name)]

    def take_gradient(self, l, name, g):
        self.grad[(l, name)] = g if g.ndim == 3 else g.reshape(N_CHIPS, g.shape[0] // N_CHIPS, g.shape[1])

    def rider(self, point, l):
        more = l + 1 < self.nl
        if point == "in_proj":
            return self._ici(self.EARLY if l == 0 else self.MLP, l)
        if point == "ssd_fwd":
            return self._d2d(*self.landed["own layer"])
        if point == "up_proj":
            return self._ici(self.IO, l + 1) if more else None
        if point == "down_proj":
            return self._d2d(self.IO, self.landed[self.IO]) if more else None
        if point == "down_bwd":
            return self._to_sibling(self.IO, l + 1) if more else None
        if point == "out_bwd":
            return self._to_sibling(self.MLP, l)
        if point == "ssd_bwd":
            ahead = self._to_chips([self.IO]) if more else None
            if l > 0:
                return ahead
            own = self._to_sibling(self.LATE, 0)
            return join_riders(ahead, own) if ahead else own
        return self._to_chips([self.MLP] + ([] if l else [self.LATE]))

    def done(self, point, l, results):
        if not results:
            return
        if point == "in_proj":
            self.landed["own layer"] = (self.EARLY if l == 0 else self.MLP, results)
        elif point == "up_proj":
            self.landed[self.IO] = results
        elif point == "ssd_fwd":
            self._gathered(self.landed.pop("own layer")[0], l, results)
        elif point == "down_proj":
            self._gathered(self.IO, l + 1, results)
        elif point == "down_bwd":
            self.from_sibling[self.IO] = (l + 1, results)
        elif point == "out_bwd":
            self.from_sibling[self.MLP] = (l, results)
        elif point == "ssd_bwd":
            if l > 0:
                self.from_chips[self.IO] = results
            else:
                if len(results) > len(self.LATE):
                    self.from_chips[self.IO] = results[:len(self.IO)]
                self._summed(self.LATE, 0, results[-len(self.LATE):])
        else:
            self.from_chips[self.MLP] = results[:len(self.MLP)]
            if l == 0:
                self.from_chips[self.LATE] = results[len(self.MLP):]

    def _pending_groups(self, point):
        groups = [self.IO] if point == "in_wgrad_0" else [self.MLP, self.LATE] if point == "in_wgrad_1" else []
        return [g for g in groups if g in self.from_chips]

    def reductions(self, point):
        items = []
        for group in self._pending_groups(point):
            layer, sums = self.sums[group]
            items += [(s[0], r, layer, self.nl, self.bufs[k]) for k, s, r in zip(group, sums, self.from_chips[group])]
        return items

    def reductions_done(self, point, bufs):
        if bufs:
            names = [k for group in self._pending_groups(point) for k in group]
            for group in self._pending_groups(point):
                del self.sums[group], self.from_chips[group]
            self.bufs.update(zip(names, bufs))

    def finish(self):
        self._summed(self.FIRST, 0, run_rider(self._to_sibling(self.FIRST, 0), "grads_to_sibling"))
        ready = self.MLP + self.LATE
        done = [self.bufs[k] for k in ready]
        arrived = run_rider(join_riders(self._to_chips([self.FIRST]), grads_rider_exchange(done)), "grads_to_chips")
        self._reduced(self.FIRST, arrived[:len(self.FIRST)])
        last = [self.bufs[k] for k in self.FIRST]
        theirs = dict(zip(ready + self.FIRST, arrived[len(self.FIRST):] + run_rider(grads_rider_exchange(last),
                                                                                    "grads_exchange")))
        return {k: (self.bufs[k], theirs[k]) for k in BIG}


def kernel(x, norm_mix_w, w_in, short_conv_w, ssd_conv_w, ssd_conv_b, dt_bias, a_log, d_skip, ssd_norm_w, w_out, norm_mlp_w, w_up, w_down, final_norm_w, loss_target, m_norm_mix_w, m_w_in, m_short_conv_w, m_ssd_conv_w, m_ssd_conv_b, m_dt_bias, m_a_log, m_d_skip, m_ssd_norm_w, m_w_out, m_norm_mlp_w, m_w_up, m_w_down, m_final_norm_w, v_norm_mix_w, v_w_in, v_short_conv_w, v_ssd_conv_w, v_ssd_conv_b, v_dt_bias, v_a_log, v_d_skip, v_ssd_norm_w, v_w_out, v_norm_mlp_w, v_w_up, v_w_down, v_final_norm_w):
    w = dict(norm_mix_w=norm_mix_w, w_in=w_in, short_conv_w=short_conv_w, ssd_conv_w=ssd_conv_w,
             ssd_conv_b=ssd_conv_b, dt_bias=dt_bias, a_log=a_log, d_skip=d_skip, ssd_norm_w=ssd_norm_w, w_out=w_out,
             norm_mlp_w=norm_mlp_w, w_up=w_up, w_down=w_down, final_norm_w=final_norm_w)
    m = dict(norm_mix_w=m_norm_mix_w, w_in=m_w_in, short_conv_w=m_short_conv_w, ssd_conv_w=m_ssd_conv_w,
             ssd_conv_b=m_ssd_conv_b, dt_bias=m_dt_bias, a_log=m_a_log, d_skip=m_d_skip, ssd_norm_w=m_ssd_norm_w,
             w_out=m_w_out, norm_mlp_w=m_norm_mlp_w, w_up=m_w_up, w_down=m_w_down, final_norm_w=m_final_norm_w)
    v = dict(norm_mix_w=v_norm_mix_w, w_in=v_w_in, short_conv_w=v_short_conv_w, ssd_conv_w=v_ssd_conv_w,
             ssd_conv_b=v_ssd_conv_b, dt_bias=v_dt_bias, a_log=v_a_log, d_skip=v_d_skip, ssd_norm_w=v_ssd_norm_w,
             w_out=v_w_out, norm_mlp_w=v_norm_mlp_w, w_up=v_w_up, w_down=v_w_down, final_norm_w=v_final_norm_w)
    xi, yi, ci = lax.axis_index("x"), lax.axis_index("y"), lax.axis_index("c")
    chip = 2 * xi + yi
    nl = w_up.shape[0]
    cols = N_CHIPS * w_in.shape[2]
    npad = cols // LANES * LANES + LANES

    full = dict(w)
    small_gathered = gather_weights([w[k] for k in SMALL_SHARDED], ["lead"] * len(SMALL_SHARDED), "gather_small")
    for k, g4 in zip(SMALL_SHARDED, small_gathered):
        full[k] = jnp.concatenate([g4[j] for j in range(N_CHIPS)], axis=2)
    comm = ChipComm({k: w[k].astype(BF16) for k in BIG}, nl, npad)

    loss, grad_x, grads = local_step(x[0], loss_target[0], full, comm, cols)
    loss = lax.psum(loss, ("x", "y", "c"))
    halves = comm.finish()
    g_shard = {}

    small_names = SMALL_REPL + SMALL_SHARDED
    small_sum = allreduce_small(_pack_small([grads[k] for k in small_names]), "allreduce_small")
    for k, g in zip(small_names, _unpack_small(small_sum, [grads[k] for k in small_names])):
        if k in SMALL_SHARDED:
            width = w[k].shape[2]
            g = lax.dynamic_slice_in_dim(g, chip * width, width, axis=2)
        g_shard[k] = g

    delta, new_m, new_v = {}, {}, {}
    for k in BIG:
        g_shard[k], delta[k], new_m[k], new_v[k] = adamw_halves(w[k], *halves[k], m[k], v[k], "adamw_%s" % k)
    packed = [_pack_small([d_[k] for k in small_names]) for d_ in (w, g_shard, m, v)]
    outs = adamw(*packed, "adamw_small")
    for d_, buf in zip((delta, new_m, new_v), outs):
        for k, val in zip(small_names, _unpack_small(buf, [w[k] for k in small_names])):
            d_[k] = val

    return (loss, grad_x[None], *[g_shard[k] for k in WEIGHTS], *[delta[k] for k in WEIGHTS],
            *[new_m[k] for k in WEIGHTS], *[new_v[k] for k in WEIGHTS])
```

```python
import functools

import jax
import jax.numpy as jnp
from jax import lax
from jax.experimental import pallas as pl
from jax.experimental.pallas import tpu as pltpu

F32 = jnp.float32
BF16 = jnp.bfloat16

EPS = 1e-5
HEAD_DIM = 64
STATE = 128
GROUPS = 2
SHORT_K = 3
SSD_K = 4
LANES = 128
PAIR = LANES // HEAD_DIM
SCAN_CHUNK = 256
HALO = 16
N_CHIPS = 4
VMEM_LIMIT = 56 * 1024 * 1024

ADAM_LR = 0.001
ADAM_B1 = 0.9
ADAM_B2 = 0.999
ADAM_EPS = 1e-08
ADAM_WD = 0.01
ADAM_STEP = 10

MESH = pl.DeviceIdType.MESH


def _params(sem):
    return pltpu.CompilerParams(dimension_semantics=sem, vmem_limit_bytes=VMEM_LIMIT)


def _tile(n, cap, quantum):
    if n <= cap:
        return n
    best = None
    for t in range(quantum, cap + 1, quantum):
        if n % t == 0:
            best = t
    assert best is not None, (n, cap, quantum)
    return best


def _dot(a, b):
    return jnp.dot(a, b, preferred_element_type=F32)


def _dot_nt(a, b):
    return lax.dot_general(a, b, (((1,), (1,)), ((), ())), preferred_element_type=F32)


def _dot_tn(a, b):
    return lax.dot_general(a, b, (((0,), (0,)), ((), ())), preferred_element_type=F32)


def _dot_exact(a, b):
    return jnp.dot(a, b, precision=lax.Precision.HIGHEST, preferred_element_type=F32)


def _sigmoid(x):
    return pl.reciprocal(1.0 + jnp.exp(-x), approx=True)


def _softplus(x):
    return jnp.maximum(x, 0.0) + jnp.log(1.0 + jnp.exp(-jnp.abs(x)))


def _relu2(v):
    return jnp.square(jnp.maximum(v, 0.0))


class Rider:
    def __init__(self, ins, out_shapes, aliases, n_sems, copies):
        self.ins, self.out_shapes, self.aliases, self.n_sems, self.copies = ins, out_shapes, aliases, n_sems, copies


def _any_specs(n):
    return [pl.BlockSpec(memory_space=pl.ANY)] * n


def _ride(body, grid, in_specs, out_specs, out_shape, scratch, args, aliases, rider, sem, name):
    n_in, n_out, n_scr = len(in_specs), len(out_specs), len(scratch)
    if rider is None:
        outs = pl.pallas_call(
            body, name=name, grid=grid, in_specs=in_specs, out_specs=out_specs, out_shape=out_shape,
            scratch_shapes=scratch, input_output_aliases=aliases, compiler_params=_params(sem))(*args)
        return list(outs), []
    ri, ro = len(rider.ins), len(rider.out_shapes)
    last = tuple(g - 1 for g in grid)

    def wrapped(*refs):
        ins = refs[:n_in]
        r_ins = refs[n_in:n_in + ri]
        outs = refs[n_in + ri:n_in + ri + n_out]
        r_outs = refs[n_in + ri + n_out:n_in + ri + n_out + ro]
        scr = refs[n_in + ri + n_out + ro:n_in + ri + n_out + ro + n_scr]
        send_sems, recv_sems = refs[-2:]
        ids = [pl.program_id(a) for a in range(len(grid))]
        at_first = functools.reduce(jnp.logical_and, [i == 0 for i in ids])
        at_last = functools.reduce(jnp.logical_and, [i == e for i, e in zip(ids, last)])

        @pl.when(at_first)
        def _():
            for cp, _ in rider.copies(r_ins, r_outs, send_sems, recv_sems):
                cp.start()

        body(*ins, *outs, *scr)

        @pl.when(at_last)
        def _():
            for cp, landed in rider.copies(r_ins, r_outs, send_sems, recv_sems):
                cp.wait_send()
                landed.wait_recv()

    all_aliases = dict(aliases)
    all_aliases.update({n_in + a: n_out + b for a, b in rider.aliases.items()})
    outs = pl.pallas_call(
        wrapped, name=name, grid=grid, in_specs=list(in_specs) + _any_specs(ri),
        out_specs=list(out_specs) + _any_specs(ro), out_shape=list(out_shape) + list(rider.out_shapes),
        scratch_shapes=list(scratch) + [pltpu.SemaphoreType.DMA((rider.n_sems,)),
                                        pltpu.SemaphoreType.DMA((rider.n_sems,))],
        input_output_aliases=all_aliases, compiler_params=_params(sem))(*args, *rider.ins)
    return list(outs[:n_out]), list(outs[n_out:])


class _SemsFrom:
    def __init__(self, sems, first):
        self.sems, self.first = sems, first

    @property
    def at(self):
        return self

    def __getitem__(self, k):
        return self.sems.at[self.first + k]


def join_riders(a, b):
    ia, oa = len(a.ins), len(a.out_shapes)

    def copies(ins, outs, send_sems, recv_sems):
        return (a.copies(ins[:ia], outs[:oa], send_sems, recv_sems)
                + b.copies(ins[ia:], outs[oa:], _SemsFrom(send_sems, a.n_sems), _SemsFrom(recv_sems, a.n_sems)))

    aliases = dict(a.aliases)
    aliases.update({ia + i: oa + o for i, o in b.aliases.items()})
    return Rider(a.ins + b.ins, a.out_shapes + b.out_shapes, aliases, a.n_sems + b.n_sems, copies)


def run_rider(rider, name):
    ri, ro = len(rider.ins), len(rider.out_shapes)

    def body(*refs):
        send_sems, recv_sems = refs[-2:]
        pairs = rider.copies(refs[:ri], refs[ri:ri + ro], send_sems, recv_sems)
        for cp, _ in pairs:
            cp.start()
        for cp, landed in pairs:
            cp.wait_send()
            landed.wait_recv()

    return list(pl.pallas_call(
        body, name=name, in_specs=_any_specs(ri), out_specs=_any_specs(ro), out_shape=list(rider.out_shapes),
        scratch_shapes=[pltpu.SemaphoreType.DMA((rider.n_sems,)), pltpu.SemaphoreType.DMA((rider.n_sems,))],
        input_output_aliases=dict(rider.aliases),
        compiler_params=pltpu.CompilerParams(has_side_effects=True))(*rider.ins))


def norm_matmul(x, nw, w, layer, n, out_dtype, name, tail_block=None, rider=None):
    t, d = x.shape
    mxu_cols = 2 * LANES
    tn = _tile(n, 1536, mxu_cols if n % mxu_cols == 0 else LANES)
    if n % mxu_cols == 0 and tn < 1024 <= n:
        tn = _tile(n, 3072, mxu_cols)
    tm = _tile(t, 512 if tn > 1536 else 1024, 8)
    nj = n // tn

    def body(x_ref, nw_ref, w_ref, *rest):
        if tail_block is None:
            o_ref, h_ref = rest
        else:
            wt_ref, o_ref, h_ref, tail_ref = rest

        @pl.when(pl.program_id(1) == 0)
        def _():
            xf = x_ref[...]
            r = lax.rsqrt(jnp.mean(xf * xf, axis=-1, keepdims=True) + EPS)
            h_ref[...] = (xf * r * nw_ref[...]).astype(BF16)

        o_ref[...] = _dot(h_ref[...], w_ref[...]).astype(out_dtype)
        if tail_block is not None:
            @pl.when(pl.program_id(1) == nj - 1)
            def _():
                tail_ref[...] = _dot(h_ref[...], wt_ref[...])

    in_specs = [pl.BlockSpec((tm, d), lambda i, j: (i, 0)), pl.BlockSpec((1, d), lambda i, j: (0, 0)),
                pl.BlockSpec((None, d, tn), lambda i, j: (layer, 0, j))]
    out_specs = [pl.BlockSpec((tm, tn), lambda i, j: (i, j)), pl.BlockSpec((tm, d), lambda i, j: (i, 0))]
    out_shape = [jax.ShapeDtypeStruct((t, n), out_dtype), jax.ShapeDtypeStruct((t, d), BF16)]
    args = [x, nw, w]
    if tail_block is not None:
        in_specs.append(pl.BlockSpec((None, d, LANES), lambda i, j: (layer, 0, tail_block)))
        out_specs.append(pl.BlockSpec((tm, LANES), lambda i, j: (i, 0)))
        out_shape.append(jax.ShapeDtypeStruct((t, LANES), F32))
        args.append(w)
    return _ride(body, (t // tm, nj), in_specs, out_specs, out_shape, [], args, {}, rider,
                 ("parallel", "arbitrary"), name)


def matmul(lhs, w, layer, transposed, n, out_dtype, name, *, lhs_fn=None, residual=None, relu_gate=None,
           rider=None):
    t, k = lhs.shape
    tm = _tile(t, 512 if k > 2048 else 1024, 8)
    tn = _tile(n, 1024, LANES)
    staged = lhs.dtype != BF16 or lhs_fn is not None
    fn = lhs_fn if lhs_fn is not None else (lambda v: v)
    has_extra = residual is not None or relu_gate is not None
    dot = _dot_nt if transposed else _dot

    def body(*refs):
        a_ref, w_ref = refs[:2]
        extra = refs[2] if has_extra else None
        o_ref = refs[3] if has_extra else refs[2]
        if staged:
            s_ref = refs[-1]

            @pl.when(pl.program_id(1) == 0)
            def _():
                s_ref[...] = fn(a_ref[...].astype(F32)).astype(BF16)

            a_ref = s_ref
        acc = dot(a_ref[...], w_ref[...])
        if residual is not None:
            acc = acc + extra[...]
        if relu_gate is not None:
            acc = acc * (2.0 * jnp.maximum(extra[...].astype(F32), 0.0))
        o_ref[...] = acc.astype(out_dtype)

    if transposed:
        w_spec = pl.BlockSpec((None, tn, k), lambda i, j: (layer, j, 0))
    else:
        w_spec = pl.BlockSpec((None, k, tn), lambda i, j: (layer, 0, j))
    in_specs = [pl.BlockSpec((tm, k), lambda i, j: (i, 0)), w_spec]
    args = [lhs, w]
    if has_extra:
        in_specs.append(pl.BlockSpec((tm, tn), lambda i, j: (i, j)))
        args.append(residual if residual is not None else relu_gate)
    outs, extra = _ride(
        body, (t // tm, n // tn), in_specs, [pl.BlockSpec((tm, tn), lambda i, j: (i, j))],
        [jax.ShapeDtypeStruct((t, n), out_dtype)], [pltpu.VMEM((tm, k), BF16)] if staged else [], args, {},
        rider, ("parallel", "arbitrary"), name)
    return outs[0], extra


def matmul_normbwd(lhs, pieces, w, layer, x, nw, dres, name, rider=None):
    t, d = x.shape
    nl = len(lhs)
    tm = _tile(t, 512, 8)
    for off, width in pieces:
        assert off % width == 0

    def body(*refs):
        lrefs = refs[:nl]
        wrefs = refs[nl:2 * nl]
        x_ref, nw_ref, dres_ref, dx_ref, dnw_ref, dxb_ref = refs[2 * nl:]
        dh = _dot_nt(lrefs[0][...].astype(BF16), wrefs[0][...])
        for a_ref, w_ref in zip(lrefs[1:], wrefs[1:]):
            dh = dh + _dot_nt(a_ref[...].astype(BF16), w_ref[...])
        xf = x_ref[...]
        r = lax.rsqrt(jnp.mean(xf * xf, axis=-1, keepdims=True) + EPS)
        nx = xf * r
        dn = dh * nw_ref[...]
        dx = r * (dn - nx * jnp.mean(dn * nx, axis=-1, keepdims=True))
        dx_ref[...] = dres_ref[...] + dx
        dxb_ref[...] = (dres_ref[...] + dx).astype(BF16)

        @pl.when(pl.program_id(0) == 0)
        def _():
            dnw_ref[...] = jnp.zeros_like(dnw_ref)

        dnw_ref[...] += jnp.sum(dh * nx, axis=0, keepdims=True)

    in_specs = [pl.BlockSpec((tm, width), (lambda blk: (lambda i: (i, blk)))(blk))
                for (_, blk), (_, width) in zip(lhs, pieces)]
    in_specs += [pl.BlockSpec((None, d, width), (lambda blk: (lambda i: (layer, 0, blk)))(off // width),
                              pipeline_mode=pl.Buffered(1))
                 for off, width in pieces]
    in_specs += [pl.BlockSpec((tm, d), lambda i: (i, 0)), pl.BlockSpec((1, d), lambda i: (0, 0)),
                 pl.BlockSpec((tm, d), lambda i: (i, 0))]
    return _ride(
        body, (t // tm,), in_specs,
        [pl.BlockSpec((tm, d), lambda i: (i, 0)), pl.BlockSpec((1, d), lambda i: (0, 0)),
         pl.BlockSpec((tm, d), lambda i: (i, 0))],
        [jax.ShapeDtypeStruct((t, d), F32), jax.ShapeDtypeStruct((1, d), F32), jax.ShapeDtypeStruct((t, d), BF16)], [],
        [*[a for a, _ in lhs], *([w] * nl), x, nw, dres], {}, rider, ("arbitrary",), name)


def matmul_tn(a, b, name, *, a_fn=None, by_chip=False, chip_sums=None, grad_sums=None):
    t, k = a.shape
    n = b.shape[1]
    tk = _tile(k, 1024, LANES)
    nn = n // N_CHIPS if by_chip else n
    tn = _tile(nn, 1536, 2 * LANES if nn % (2 * LANES) == 0 else LANES)
    tt = _tile(t, 1024, 8)
    nt = t // tt
    gn_ = n // tn
    steps = (k // tk) * gn_ * nt
    fn = a_fn if a_fn is not None else (lambda v: v)
    with_sums, chip_sums = chip_sums, list(chip_sums or ())
    for g, _ in chip_sums:
        if (g.shape[0] * g.shape[1] // 2) % (steps * HALO) or steps % g.shape[0]:
            return (matmul_tn(a, b, name, a_fn=a_fn, by_chip=by_chip),
                    [tuple(chip_sum(g_, r_, "chip_sum")) for g_, r_ in chip_sums])
    with_grads, grad_sums = grad_sums, list(grad_sums or ())
    assert not (chip_sums and grad_sums)
    for p32, *_ in grad_sums:
        if p32.shape[1] % (steps * HALO):
            return (matmul_tn(a, b, name, a_fn=a_fn, by_chip=by_chip),
                    [grad_sum(*item, "grad_sum") for item in grad_sums])
    ns = len(chip_sums)
    ng = len(grad_sums)
    n_gin = [N_CHIPS + 3 + (item[4] is not None) for item in grad_sums]
    step = lambda i, j, s: (i * gn_ + j) * nt + s

    def body(*refs):
        a_ref, b_ref = refs[:2]
        side_in = refs[2:2 + 3 * ns]
        grad_in = refs[2 + 3 * ns:2 + 3 * ns + sum(n_gin)]
        o_ref = refs[2 + 3 * ns + sum(n_gin)]
        side_out = refs[3 + 3 * ns + sum(n_gin):3 + 5 * ns + sum(n_gin)]
        grad_out = refs[3 + 5 * ns + sum(n_gin):3 + 5 * ns + sum(n_gin) + ng]
        acc_ref = refs[-1]

        @pl.when(pl.program_id(2) == 0)
        def _():
            acc_ref[...] = jnp.zeros_like(acc_ref)

        av = a_ref[...]
        if a_fn is not None:
            av = fn(av.astype(F32))
        acc_ref[...] += _dot_tn(av.astype(BF16), b_ref[...].astype(BF16))
        for q in range(ns):
            g0_ref, g1_ref, r_ref = side_in[3 * q:3 * q + 3]
            tot = jnp.where(lax.axis_index("c") == 0, g0_ref[...], g1_ref[...]) + r_ref[...]
            side_out[2 * q][...] = tot
            side_out[2 * q + 1][...] = tot.astype(BF16)
        pos = 0
        for q in range(ng):
            p = grad_in[pos:pos + N_CHIPS]
            r = grad_in[pos + N_CHIPS:pos + N_CHIPS + 3]
            pos += n_gin[q]
            x_, y_, _ = _coords()
            chip = 2 * x_ + y_
            own = jnp.where(chip == 0, p[0][...], jnp.where(chip == 1, p[1][...],
                                                           jnp.where(chip == 2, p[2][...], p[3][...])))
            grad_out[q][...] = (own + r[0][...].astype(F32) + r[1][...].astype(F32)
                                + r[2][...].astype(F32))[None]

        @pl.when(pl.program_id(2) == nt - 1)
        def _():
            o_ref[...] = acc_ref[...]

    if by_chip:
        per = n // N_CHIPS // tn
        out_specs = [pl.BlockSpec((None, tk, tn), lambda i, j, s: (j // per, i, j % per))]
        out_shape = [jax.ShapeDtypeStruct((N_CHIPS, k, n // N_CHIPS), F32)]
    else:
        out_specs = [pl.BlockSpec((tk, tn), lambda i, j, s: (i, j))]
        out_shape = [jax.ShapeDtypeStruct((k, n), F32)]
    in_specs = [pl.BlockSpec((tt, tk), lambda i, j, s: (s, i)), pl.BlockSpec((tt, tn), lambda i, j, s: (s, j))]
    args = [a, b]
    for g, recv in chip_sums:
        nch, r, c = g.shape
        r2 = r // 2
        rows = nch * r2 // steps
        per_chip = r2 // rows
        assert rows % HALO == 0 and r2 % rows == 0
        for half in range(2):
            in_specs.append(pl.BlockSpec(
                (rows, c), (lambda h: (lambda i, j, s: ((step(i, j, s) // per_chip) * 2 * per_chip + h * per_chip
                                                        + step(i, j, s) % per_chip, 0)))(half)))
        flat = pl.BlockSpec((rows, c), lambda i, j, s: (step(i, j, s), 0))
        in_specs.append(flat)
        args += [g.reshape(nch * r, c), g.reshape(nch * r, c), recv.reshape(nch * r2, c)]
        out_specs += [flat, flat]
        out_shape += [jax.ShapeDtypeStruct((nch * r2, c), F32), jax.ShapeDtypeStruct((nch * r2, c), BF16)]
    aliases = {}
    for p32, recv, layer, nl, buf in grad_sums:
        _, r2, c = p32.shape
        rows = r2 // steps
        slab = lambda kk: pl.BlockSpec((rows, c), lambda i, j, s: (kk * steps + step(i, j, s), 0))
        in_specs += [slab(kk) for kk in range(N_CHIPS)] + [slab(kk) for kk in range(3)]
        args += [p32.reshape(N_CHIPS * r2, c)] * N_CHIPS + [recv.reshape(3 * r2, c)] * 3
        if buf is not None:
            aliases[len(args)] = len(out_specs)
            in_specs.append(pl.BlockSpec(memory_space=pl.ANY))
            args.append(buf)
        out_specs.append(pl.BlockSpec((1, rows, c), (lambda ly: (lambda i, j, s: (ly, step(i, j, s), 0)))(layer)))
        out_shape.append(jax.ShapeDtypeStruct((nl, r2, c), F32))
    outs = pl.pallas_call(
        body, name=name, grid=(k // tk, gn_, nt), in_specs=in_specs, out_specs=out_specs, out_shape=out_shape,
        scratch_shapes=[pltpu.VMEM((tk, tn), F32)], input_output_aliases=aliases,
        compiler_params=_params(("parallel", "parallel", "arbitrary")),
    )(*args)
    if with_grads is not None:
        return outs[0], list(outs[1:])
    if with_sums is None:
        return outs[0]
    sums = [(outs[1 + 2 * q].reshape(g.shape[0], g.shape[1] // 2, g.shape[2]),
             outs[2 + 2 * q].reshape(g.shape[0], g.shape[1] // 2, g.shape[2])) for q, (g, _) in enumerate(chip_sums)]
    return outs[0], sums


def split_to_chips(pieces, cols, name):
    d = pieces[0].shape[0]
    widths = [p.shape[1] for p in pieces]
    w = cols // N_CHIPS
    tr = _tile(d, 256, 8)
    npc = len(pieces)

    def body(*refs):
        o_ref, row = refs[npc], refs[npc + 1]
        off = 0
        for r, n in zip(refs[:npc], widths):
            row[:, off:off + n] = r[...]
            off += n
        for j in range(N_CHIPS):
            o_ref[j] = row[:, j * w:(j + 1) * w]

    return pl.pallas_call(
        body, name=name, grid=(d // tr,),
        in_specs=[pl.BlockSpec((tr, n), lambda i: (i, 0)) for n in widths],
        out_specs=pl.BlockSpec((N_CHIPS, tr, w), lambda i: (0, i, 0)),
        out_shape=jax.ShapeDtypeStruct((N_CHIPS, d, w), F32),
        scratch_shapes=[pltpu.VMEM((tr, sum(widths)), F32)],
        compiler_params=_params(("parallel",)),
    )(*pieces)


def join_from_chips(g4, npad, name):
    _, nl, d, w = g4.shape
    tr = _tile(d, 256, HALO)

    def body(g_ref, o_ref):
        for j in range(N_CHIPS):
            o_ref[:, j * w:(j + 1) * w] = g_ref[j]
        o_ref[:, N_CHIPS * w:] = jnp.zeros((tr, npad - N_CHIPS * w), o_ref.dtype)

    return pl.pallas_call(
        body, name=name, grid=(nl, d // tr),
        in_specs=[pl.BlockSpec((N_CHIPS, None, tr, w), lambda l, i: (0, l, i, 0))],
        out_specs=pl.BlockSpec((None, tr, npad), lambda l, i: (l, i, 0)),
        out_shape=jax.ShapeDtypeStruct((nl, d, npad), g4.dtype),
        compiler_params=_params(("parallel", "parallel")),
    )(g4)


def conv_mixer_fwd(proj, kw, cw, out_cols, name):
    t = proj.shape[0]
    tm = _tile(t, 1024, HALO)
    tc = _tile(cw, 1024, LANES)
    nj = cw // tc
    hb = tm // HALO

    def body(ub_ref, uc_ref, uh_ref, ucp_ref, uhp_ref, kw_ref, y_ref):
        i = pl.program_id(0)
        taps = [kw_ref[pl.ds(k, 1), :] for k in range(SHORT_K)]
        row = lax.broadcasted_iota(jnp.int32, (8, tc), 0)
        vp = ucp_ref[...].astype(F32) * uhp_ref[...].astype(F32)

        def conv(block, before):
            acc = taps[SHORT_K - 1] * block
            for k in range(SHORT_K - 1):
                s = SHORT_K - 1 - k
                acc = acc + taps[k] * jnp.where(row >= s, pltpu.roll(block, s, 0), pltpu.roll(before, s, 0))
            return acc

        def strip(s, before):
            rows = pl.ds(pl.multiple_of(s * HALO, HALO), HALO)
            v = uc_ref[rows, :].astype(F32) * uh_ref[rows, :].astype(F32)
            top, bottom = v[0:8], v[8:HALO]
            cv = jnp.concatenate([conv(top, before), conv(bottom, top)], axis=0)
            y_ref[rows, :] = (ub_ref[rows, :].astype(F32) * cv).astype(BF16)
            return bottom

        lax.fori_loop(0, tm // HALO, strip, jnp.where(i > 0, vp[8:HALO], 0.0))

    prev = lambda off: (lambda i, j: (jnp.maximum(i * hb - 1, 0), off + j))
    return pl.pallas_call(
        body, name=name, grid=(t // tm, nj),
        in_specs=[pl.BlockSpec((tm, tc), lambda i, j: (i, j)),
                  pl.BlockSpec((tm, tc), lambda i, j: (i, nj + j)),
                  pl.BlockSpec((tm, tc), lambda i, j: (i, 2 * nj + j)),
                  pl.BlockSpec((HALO, tc), prev(nj)),
                  pl.BlockSpec((HALO, tc), prev(2 * nj)),
                  pl.BlockSpec((SHORT_K, tc), lambda i, j: (0, j))],
        out_specs=pl.BlockSpec((tm, tc), lambda i, j: (i, j)),
        out_shape=jax.ShapeDtypeStruct((t, out_cols), BF16),
        compiler_params=_params(("parallel", "parallel")),
    )(proj, proj, proj, proj, proj, kw)


def conv_mixer_bwd(proj, dy, kw, cw, name):
    t = proj.shape[0]
    tm = _tile(t, 1024, HALO)
    tc = cw
    nj = cw // tc
    hb = tm // HALO
    ni = t // tm
    last_hb = t // HALO - 1

    def body(ub_ref, uc_ref, uh_ref, dy_ref, ucp_ref, uhp_ref, ubn_ref, dyn_ref, kw_ref,
             du_ref, dkw_ref):
        i = pl.program_id(1)
        nstrips = tm // HALO
        taps = [kw_ref[pl.ds(k, 1), :] for k in range(SHORT_K)]
        row = lax.broadcasted_iota(jnp.int32, (8, tc), 0)
        vp = ucp_ref[...].astype(F32) * uhp_ref[...].astype(F32)
        dcvn = dyn_ref[...].astype(F32) * ubn_ref[...].astype(F32)

        def shifted(block, before, s):
            return jnp.where(row >= s, pltpu.roll(block, s, 0), pltpu.roll(before, s, 0))

        def lifted(block, after, s):
            return jnp.where(row < 8 - s, pltpu.roll(block, 8 - s, 0), pltpu.roll(after, 8 - s, 0))

        def down(s, carry):
            before, sums = carry
            rows = pl.ds(pl.multiple_of(s * HALO, HALO), HALO)
            v = uc_ref[rows, :].astype(F32) * uh_ref[rows, :].astype(F32)
            dyv = dy_ref[rows, :].astype(F32)
            dcv = dyv * ub_ref[rows, :].astype(F32)
            cvs = []
            sums = list(sums)
            for block, above, dcb in ((v[0:8], before, dcv[0:8]), (v[8:HALO], v[0:8], dcv[8:HALO])):
                moved = [shifted(block, above, SHORT_K - 1 - k) for k in range(SHORT_K - 1)] + [block]
                cvs.append(sum(taps[k] * moved[k] for k in range(SHORT_K)))
                sums = [sums[k] + dcb * moved[k] for k in range(SHORT_K)]
            du_ref[rows, 0:cw] = (dyv * jnp.concatenate(cvs, axis=0)).astype(BF16)
            return v[8:HALO], tuple(sums)

        zero = jnp.zeros((8, tc), F32)
        _, sums = lax.fori_loop(0, nstrips, down, (jnp.where(i > 0, vp[8:HALO], 0.0), (zero,) * SHORT_K))

        def up(n, after):
            rows = pl.ds(pl.multiple_of((nstrips - 1 - n) * HALO, HALO), HALO)
            uc = uc_ref[rows, :].astype(F32)
            uh = uh_ref[rows, :].astype(F32)
            dcv = dy_ref[rows, :].astype(F32) * ub_ref[rows, :].astype(F32)
            dvs = []
            for block, below in ((dcv[0:8], dcv[8:HALO]), (dcv[8:HALO], after)):
                dvs.append(taps[SHORT_K - 1] * block
                           + sum(taps[k] * lifted(block, below, SHORT_K - 1 - k) for k in range(SHORT_K - 1)))
            dv = jnp.concatenate(dvs, axis=0)
            du_ref[rows, cw:2 * cw] = (dv * uh).astype(BF16)
            du_ref[rows, 2 * cw:3 * cw] = (dv * uc).astype(BF16)
            return dcv[0:8]

        lax.fori_loop(0, nstrips, up, jnp.where(i < ni - 1, dcvn[0:8], 0.0))

        @pl.when(i == 0)
        def _():
            dkw_ref[...] = jnp.zeros_like(dkw_ref)

        for k in range(SHORT_K):
            dkw_ref[pl.ds(k, 1), :] += jnp.sum(sums[k], axis=0, keepdims=True)

    prev = lambda off: (lambda j, i: (jnp.maximum(i * hb - 1, 0), off + j))
    nxt = lambda off: (lambda j, i: (jnp.minimum((i + 1) * hb, last_hb), off + j))
    cur = lambda off: (lambda j, i: (i, off + j))
    return pl.pallas_call(
        body, name=name, grid=(nj, ni),
        in_specs=[pl.BlockSpec((tm, tc), cur(0)), pl.BlockSpec((tm, tc), cur(nj)),
                  pl.BlockSpec((tm, tc), cur(2 * nj)), pl.BlockSpec((tm, tc), cur(0)),
                  pl.BlockSpec((HALO, tc), prev(nj)), pl.BlockSpec((HALO, tc), prev(2 * nj)),
                  pl.BlockSpec((HALO, tc), nxt(0)), pl.BlockSpec((HALO, tc), nxt(0)),
                  pl.BlockSpec((SHORT_K, tc), lambda j, i: (0, j))],
        out_specs=[pl.BlockSpec((tm, 3 * cw), lambda j, i: (i, 0)),
                   pl.BlockSpec((SHORT_K, tc), lambda j, i: (0, j))],
        out_shape=[jax.ShapeDtypeStruct((t, 3 * cw), BF16), jax.ShapeDtypeStruct((SHORT_K, cw), F32)],
        compiler_params=_params(("parallel", "arbitrary")),
    )(proj, proj, proj, dy, proj, proj, proj, dy, kw)


def _head_column(mat, lane, h):
    return jnp.sum(jnp.where(lane == h, mat, 0.0), axis=-1, keepdims=True)


def _ssd_common(dt_raw_ref, dtb_ref, aneg_ref, cum_s, cumt_s, chunk):
    dt = _softplus(dt_raw_ref[...] + dtb_ref[...])
    al = dt * aneg_ref[...]
    ri = lax.broadcasted_iota(jnp.int32, (chunk, chunk), 0)
    ci = lax.broadcasted_iota(jnp.int32, (chunk, chunk), 1)
    cum = _dot_exact((ri >= ci).astype(F32), al)
    cum_s[...] = cum
    cumt_s[...] = cum.T
    return dt, cum, ri >= ci


EDGE = 16


def _shift_matrices(shift_s, chunk, kk, up):
    ri = lax.broadcasted_iota(jnp.int32, (chunk, chunk), 0)
    ci = lax.broadcasted_iota(jnp.int32, (chunk, chunk), 1)
    for k in range(kk - 1):
        s = kk - 1 - k
        shift_s[k] = ((ci - ri if up else ri - ci) == s).astype(BF16)


def _causal_conv(cur, head, kw_ref, b_ref, shift_s, kk):
    acc = b_ref[...] + kw_ref[pl.ds(kk - 1, 1), :] * cur.astype(F32)
    top = b_ref[...] + kw_ref[pl.ds(kk - 1, 1), :] * head[pl.ds(8, EDGE), :]
    for k in range(kk - 1):
        acc = acc + kw_ref[pl.ds(k, 1), :] * _dot(shift_s[k], cur)
        top = top + kw_ref[pl.ds(k, 1), :] * head[pl.ds(8 - (kk - 1) + k, EDGE), :]
    return acc, top


def ssd_fwd(proj, dt_raw, y_mix, kw_xs, kw_bc, b_xs, b_bc, dtb, aneg, dskip, normw, cw, si, name, rider=None):
    t = proj.shape[0]
    ch = min(SCAN_CHUNK, t)
    nc = t // ch
    npair = si // LANES
    ppg = npair // GROUPS
    gn = GROUPS * STATE
    gw = si // GROUPS
    assert cw == si and (3 * cw + 2 * si) % (2 * gn) == 0
    zblk = 3 * cw // si
    xsblk = zblk + 1
    bcblk = (3 * cw + 2 * si) // (2 * gn)

    def body(z_ref, xs_ref, bc_ref, dtr_ref, ymix_ref, kwx_ref, kwb_ref, bx_ref, bb_ref, dtb_ref, aneg_ref, dsk_ref,
             nw_ref, yb_ref, ys_ref, hs_ref, xcx_ref, xcb_ref,
             headx, headb, shift_s, xs_s, bc_s, h_s, gated_s, s_s, cum_s, cumt_s):
        del ymix_ref
        c = pl.program_id(0)

        @pl.when(c == 0)
        def _():
            h_s[...] = jnp.zeros_like(h_s)
            headx[0:8, :] = jnp.zeros((8, si), F32)
            headb[0:8, :] = jnp.zeros((8, 2 * gn), F32)
            _shift_matrices(shift_s, ch, SSD_K, up=False)

        for raw_ref, head, kw_ref, b_ref, pre_ref, act_s in ((xs_ref, headx, kwx_ref, bx_ref, xcx_ref, xs_s),
                                                           (bc_ref, headb, kwb_ref, bb_ref, xcb_ref, bc_s)):
            head[8:8 + EDGE, :] = raw_ref[0:EDGE, :].astype(F32)
            pre, top = _causal_conv(raw_ref[...], head, kw_ref, b_ref, shift_s, SSD_K)
            head[0:8, :] = raw_ref[ch - EDGE:ch, :].astype(F32)[EDGE - 8:EDGE]
            pre_ref[...] = pre.astype(BF16)
            pre_ref[0:EDGE, :] = top.astype(BF16)
            act_s[...] = (pre * _sigmoid(pre)).astype(act_s.dtype)
            act_s[0:EDGE, :] = (top * _sigmoid(top)).astype(act_s.dtype)

        dt, cum, tril = _ssd_common(dtr_ref, dtb_ref, aneg_ref, cum_s, cumt_s, ch)
        lane = lax.broadcasted_iota(jnp.int32, (ch, LANES), 1)
        lane1 = lax.broadcasted_iota(jnp.int32, (1, LANES), 1)
        low = lane < HEAD_DIM
        clast = cum_s[pl.ds(ch - 1, 1), :]

        for p in range(npair):
            g = p // ppg
            col = slice(p * LANES, (p + 1) * LANES)
            bg = bc_s[:, g * STATE:(g + 1) * STATE]
            cg = bc_s[:, gn + g * STATE:gn + (g + 1) * STATE]
            if p % ppg == 0:
                s_s[...] = _dot_nt(cg, bg)
            heads = (PAIR * p, PAIR * p + 1)
            ccol = [_head_column(cum, lane, h) for h in heads]
            dcol = [_head_column(dt, lane, h) for h in heads]
            cl = [jnp.sum(jnp.where(lane1 == h, clast, 0.0), axis=-1, keepdims=True) for h in heads]
            cum_px = jnp.where(low, ccol[0], ccol[1])
            dt_px = jnp.where(low, dcol[0], dcol[1])
            cl_px = jnp.where(lane1 < HEAD_DIM, cl[0], cl[1])
            xs_p = xs_s[:, col]
            xdt = xs_p * dt_px
            y = dsk_ref[:, col] * xs_p
            for hi, h in enumerate(heads):
                dec = jnp.exp(jnp.where(tril, ccol[hi] - cumt_s[pl.ds(h, 1), :], -jnp.inf))
                wm = (s_s[...] * dec).astype(BF16)
                xm = jnp.where(low if hi == 0 else jnp.logical_not(low), xdt, 0.0).astype(BF16)
                y = y + _dot(wm, xm)
            hp = h_s[p]
            hs_ref[0, p] = hp
            y = y + _dot(cg, hp.astype(BF16)) * jnp.exp(cum_px)
            st = _dot_tn(bg, (xdt * jnp.exp(cl_px - cum_px)).astype(BF16))
            h_s[p] = jnp.exp(cl_px) * hp + st
            ys_ref[:, col] = y.astype(BF16)
            zp = z_ref[:, col].astype(F32)
            gated_s[:, col] = y * zp * _sigmoid(zp)

        for g in range(GROUPS):
            col = slice(g * gw, (g + 1) * gw)
            gg = gated_s[:, col]
            r = lax.rsqrt(jnp.mean(gg * gg, axis=-1, keepdims=True) + EPS)
            yb_ref[:, col] = (gg * r * nw_ref[:, col]).astype(BF16)

    full = lambda shape: pl.BlockSpec(shape, lambda c: tuple(0 for _ in shape))
    return _ride(
        body, (nc,),
        [pl.BlockSpec((ch, si), lambda c: (c, zblk)),
         pl.BlockSpec((ch, si), lambda c: (c, xsblk)),
         pl.BlockSpec((ch, 2 * gn), lambda c: (c, bcblk)),
         pl.BlockSpec((ch, LANES), lambda c: (c, 0)),
         pl.BlockSpec(memory_space=pl.ANY),
         full((SSD_K, si)), full((SSD_K, 2 * gn)), full((1, si)), full((1, 2 * gn)),
         full((1, LANES)), full((1, LANES)), full((1, si)), full((1, si))],
        [pl.BlockSpec((ch, si), lambda c: (c, cw // si)),
         pl.BlockSpec((ch, si), lambda c: (c, 0)),
         pl.BlockSpec((1, npair, STATE, LANES), lambda c: (c, 0, 0, 0)),
         pl.BlockSpec((ch, si), lambda c: (c, 0)), pl.BlockSpec((ch, 2 * gn), lambda c: (c, 0))],
        [jax.ShapeDtypeStruct(y_mix.shape, BF16), jax.ShapeDtypeStruct((t, si), BF16),
         jax.ShapeDtypeStruct((nc, npair, STATE, LANES), F32),
         jax.ShapeDtypeStruct((t, si), BF16), jax.ShapeDtypeStruct((t, 2 * gn), BF16)],
        [pltpu.VMEM((8 + EDGE, si), F32), pltpu.VMEM((8 + EDGE, 2 * gn), F32),
         pltpu.VMEM((SSD_K - 1, ch, ch), BF16),
         pltpu.VMEM((ch, si), F32), pltpu.VMEM((ch, 2 * gn), BF16),
         pltpu.VMEM((npair, STATE, LANES), F32), pltpu.VMEM((ch, si), F32),
         pltpu.VMEM((ch, ch), F32), pltpu.VMEM((ch, LANES), F32), pltpu.VMEM((LANES, ch), F32)],
        [proj, proj, proj, dt_raw, y_mix, kw_xs, kw_bc, b_xs, b_bc, dtb, aneg, dskip, normw], {4: 0}, rider,
        ("arbitrary",), name)


def ssd_bwd(proj, dt_raw, ys, hsave, pre_xs, pre_bc, dy, kw_xs, kw_bc, dtb, aneg, dskip, normw, cw, si, name,
            rider=None):
    t = proj.shape[0]
    ch = min(SCAN_CHUNK, t)
    nc = t // ch
    npair = si // LANES
    ppg = npair // GROUPS
    gn = GROUPS * STATE
    gw = si // GROUPS
    zblk = 3 * cw // si
    xsblk = zblk + 1
    bcblk = (3 * cw + 2 * si) // (2 * gn)

    def body(z_ref, xs_ref, bc_ref, xcx_ref, xcb_ref, dtr_ref, ys_ref, hs_ref, dyb_ref,
             kwx_ref, kwb_ref, dtb_ref, aneg_ref, dsk_ref, nw_ref,
             dp_ref, ddt_ref, dkwx_ref, dkwb_ref, dbx_ref, dbb_ref, ddtb_ref, da_ref, ddsk_ref,
             dnw_ref,
             tailx, tailb, shift_s, xs_s, bc_s, dsx_s, dsb_s, dy_s, dxs_s, dbc_s, dh_s, s_s, ds_s,
             cum_s, cumt_s, dccol_s, dcrow_s, ddtcol_s, dcl_s):
        i = pl.program_id(0)

        @pl.when(i == 0)
        def _():
            dh_s[...] = jnp.zeros_like(dh_s)
            tailx[EDGE:EDGE + 8, :] = jnp.zeros((8, si), F32)
            tailb[EDGE:EDGE + 8, :] = jnp.zeros((8, 2 * gn), F32)
            _shift_matrices(shift_s, ch, SSD_K, up=True)
            for r in (dkwx_ref, dkwb_ref, dbx_ref, dbb_ref, ddtb_ref, da_ref, ddsk_ref, dnw_ref):
                r[...] = jnp.zeros_like(r)

        xc = xcx_ref[...].astype(F32)
        sg = _sigmoid(xc)
        xs_s[...] = xc * sg
        dsx_s[...] = sg * (1.0 + xc * (1.0 - sg))
        bcc = xcb_ref[...].astype(F32)
        sgb = _sigmoid(bcc)
        bc_s[...] = (bcc * sgb).astype(BF16)
        dsb_s[...] = sgb * (1.0 + bcc * (1.0 - sgb))

        dt, cum, tril = _ssd_common(dtr_ref, dtb_ref, aneg_ref, cum_s, cumt_s, ch)
        lane = lax.broadcasted_iota(jnp.int32, (ch, LANES), 1)
        lane1 = lax.broadcasted_iota(jnp.int32, (1, LANES), 1)
        low = lane < HEAD_DIM
        low1 = lane1 < HEAD_DIM
        clast = cum_s[pl.ds(ch - 1, 1), :]

        for g in range(GROUPS):
            col = slice(g * gw, (g + 1) * gw)
            ysf = ys_ref[:, col].astype(F32)
            zf = z_ref[:, col].astype(F32)
            sz = _sigmoid(zf)
            silz = zf * sz
            gg = ysf * silz
            r = lax.rsqrt(jnp.mean(gg * gg, axis=-1, keepdims=True) + EPS)
            nrm = gg * r
            dyb = dyb_ref[:, col].astype(F32)
            dnw_ref[:, col] += jnp.sum(dyb * nrm, axis=0, keepdims=True)
            dn = dyb * nw_ref[:, col]
            dgg = r * (dn - nrm * jnp.mean(dn * nrm, axis=-1, keepdims=True))
            dy_s[:, col] = dgg * silz
            dp_ref[:, col] = (dgg * ysf * (sz * (1.0 + zf * (1.0 - sz)))).astype(BF16)

        dccol_s[...] = jnp.zeros_like(dccol_s)
        dcrow_s[...] = jnp.zeros_like(dcrow_s)
        ddtcol_s[...] = jnp.zeros_like(ddtcol_s)
        dcl_s[...] = jnp.zeros_like(dcl_s)
        dbc_s[...] = jnp.zeros_like(dbc_s)

        for p in range(npair):
            g = p // ppg
            col = slice(p * LANES, (p + 1) * LANES)
            bcol = slice(g * STATE, (g + 1) * STATE)
            ccolg = slice(gn + g * STATE, gn + (g + 1) * STATE)
            bg = bc_s[:, bcol]
            cg = bc_s[:, ccolg]
            if p % ppg == 0:
                s_s[...] = _dot_nt(cg, bg)
                ds_s[...] = jnp.zeros_like(ds_s)
            heads = (PAIR * p, PAIR * p + 1)
            masks = (low, jnp.logical_not(low))
            masks1 = (low1, jnp.logical_not(low1))
            ccol = [_head_column(cum, lane, h) for h in heads]
            dcol = [_head_column(dt, lane, h) for h in heads]
            cl = [jnp.sum(jnp.where(lane1 == h, clast, 0.0), axis=-1, keepdims=True) for h in heads]
            cum_px = jnp.where(low, ccol[0], ccol[1])
            dt_px = jnp.where(low, dcol[0], dcol[1])
            cl_px = jnp.where(low1, cl[0], cl[1])
            e_px = jnp.exp(cum_px)
            dec_end = jnp.exp(cl_px - cum_px)
            gdec = jnp.exp(cl_px)
            xs_p = xs_s[:, col]
            xdt = xs_p * dt_px
            dyp = dy_s[:, col]
            hc = hs_ref[0, p]
            hcb = hc.astype(BF16)
            dhn = dh_s[p]
            dhnb = dhn.astype(BF16)

            ddsk_ref[:, col] += jnp.sum(dyp * xs_p, axis=0, keepdims=True)
            dxs_acc = dsk_ref[:, col] * dyp
            dye = dyp * e_px
            dyeb = dye.astype(BF16)
            dbc_s[:, ccolg] += _dot_nt(dyeb, hcb)
            dcum_lane = dye * _dot(cg, hcb)
            dh_from_y = _dot_tn(cg, dyeb)
            xd = xdt * dec_end
            dxd = _dot(bg, dhnb)
            dbc_s[:, bcol] += _dot_nt(xd.astype(BF16), dhnb)
            dxdt = dxd * dec_end
            t1 = dxd * xd
            dcum_lane = dcum_lane - t1
            dcl_lane = jnp.sum(t1, axis=0, keepdims=True) + jnp.sum(dhn * hc, axis=0, keepdims=True) * gdec
            dh_s[p] = gdec * dhn + dh_from_y
            xdtb = xdt.astype(BF16)
            for hi, h in enumerate(heads):
                dym = jnp.where(masks[hi], dyp, 0.0).astype(BF16)
                dw = _dot_nt(dym, xdtb)
                dec = jnp.exp(jnp.where(tril, ccol[hi] - cumt_s[pl.ds(h, 1), :], -jnp.inf))
                wm = s_s[...] * dec
                dxdt = dxdt + _dot_tn(wm.astype(BF16), dym)
                ds_s[...] += dw * dec
                gm = dw * wm
                rowsum = jnp.sum(gm, axis=-1, keepdims=True)
                lanesum = jnp.sum(jnp.where(masks[hi], dcum_lane, 0.0), axis=-1, keepdims=True)
                dccol_s[...] += jnp.where(lane == h, rowsum + lanesum, 0.0)
                dcrow_s[pl.ds(h, 1), :] = jnp.sum(gm, axis=0, keepdims=True)
                dcl_h = jnp.sum(jnp.where(masks1[hi], dcl_lane, 0.0), axis=-1, keepdims=True)
                dcl_s[...] += jnp.where(lane1 == h, dcl_h, 0.0)
            ddt_lane = dxdt * xs_p
            for hi, h in enumerate(heads):
                s = jnp.sum(jnp.where(masks[hi], ddt_lane, 0.0), axis=-1, keepdims=True)
                ddtcol_s[...] += jnp.where(lane == h, s, 0.0)
            dxs_s[:, col] = dxs_acc + dxdt * dt_px
            if p % ppg == ppg - 1:
                dsb = ds_s[...].astype(BF16)
                dbc_s[:, ccolg] += _dot(dsb, bg)
                dbc_s[:, bcol] += _dot_tn(dsb, cg)

        rowi = lax.broadcasted_iota(jnp.int32, (ch, LANES), 0)
        dcum = dccol_s[...] - dcrow_s[...].T + jnp.where(rowi == ch - 1, dcl_s[...], 0.0)
        ri = lax.broadcasted_iota(jnp.int32, (ch, ch), 0)
        ci = lax.broadcasted_iota(jnp.int32, (ch, ch), 1)
        dal = _dot_exact((ri <= ci).astype(F32), dcum)
        ddt = dal * aneg_ref[...] + ddtcol_s[...]
        da_ref[...] += jnp.sum(dal * dt, axis=0, keepdims=True)
        ddtr = ddt * _sigmoid(dtr_ref[...] + dtb_ref[...])
        ddt_ref[...] = ddtr
        ddtb_ref[...] += jnp.sum(ddtr, axis=0, keepdims=True)

        for (dpost, dsl, tail, raw_ref, kw_ref, dkw_ref, db_ref, out) in (
                (dxs_s, dsx_s, tailx, xs_ref, kwx_ref, dkwx_ref, dbx_ref, slice(si, 2 * si)),
                (dbc_s, dsb_s, tailb, bc_ref, kwb_ref, dkwb_ref, dbb_ref, slice(2 * si, 2 * si + 2 * gn))):
            dxc = dpost[...] * dsl[...]
            dxcb = dxc.astype(BF16)
            raw = raw_ref[...].astype(F32)
            raw_end = raw_ref[ch - EDGE:ch, :].astype(F32)
            tail[0:EDGE, :] = dxcb[ch - EDGE:ch].astype(F32)
            db_ref[...] += jnp.sum(dxc, axis=0, keepdims=True)
            draw = kw_ref[pl.ds(SSD_K - 1, 1), :] * dxc
            dkw_ref[pl.ds(SSD_K - 1, 1), :] += jnp.sum(dxc * raw, axis=0, keepdims=True)
            fix = jnp.zeros((EDGE, dxc.shape[1]), F32)
            for k in range(SSD_K - 1):
                moved = _dot(shift_s[k], dxcb)
                miss = tail[pl.ds(SSD_K - 1 - k, EDGE), :] - moved[ch - EDGE:ch]
                draw = draw + kw_ref[pl.ds(k, 1), :] * moved
                fix = fix + kw_ref[pl.ds(k, 1), :] * miss
                dkw_ref[pl.ds(k, 1), :] += (jnp.sum(moved * raw, axis=0, keepdims=True)
                                            + jnp.sum(miss * raw_end, axis=0, keepdims=True))
            dp_ref[:, out] = draw.astype(BF16)
            dp_ref[ch - EDGE:ch, out] = (draw[ch - EDGE:ch] + fix).astype(BF16)
            tail[EDGE:EDGE + 8, :] = dxcb[0:EDGE].astype(F32)[0:8]

    full = lambda shape: pl.BlockSpec(shape, lambda i: tuple(0 for _ in shape))
    rev = lambda blk: (lambda i: (nc - 1 - i, blk))
    small_in = [(SSD_K, si), (SSD_K, 2 * gn), (1, LANES), (1, LANES), (1, si), (1, si)]
    small = [(SSD_K, si), (SSD_K, 2 * gn), (1, si), (1, 2 * gn), (1, LANES), (1, LANES), (1, si), (1, si)]
    return _ride(
        body, (nc,),
        [pl.BlockSpec((ch, si), rev(zblk)), pl.BlockSpec((ch, si), rev(xsblk)),
         pl.BlockSpec((ch, 2 * gn), rev(bcblk)),
         pl.BlockSpec((ch, si), rev(0)), pl.BlockSpec((ch, 2 * gn), rev(0)),
         pl.BlockSpec((ch, LANES), rev(0)), pl.BlockSpec((ch, si), rev(0)),
         pl.BlockSpec((1, npair, STATE, LANES), lambda i: (nc - 1 - i, 0, 0, 0)),
         pl.BlockSpec((ch, si), rev(cw // si))] + [full(s) for s in small_in],
        [pl.BlockSpec((ch, 2 * si + 2 * gn), rev(0)), pl.BlockSpec((ch, LANES), rev(0))] + [full(s) for s in small],
        [jax.ShapeDtypeStruct((t, 2 * si + 2 * gn), BF16), jax.ShapeDtypeStruct((t, LANES), F32)]
        + [jax.ShapeDtypeStruct(s, F32) for s in small],
        [pltpu.VMEM((EDGE + 8, si), F32), pltpu.VMEM((EDGE + 8, 2 * gn), F32),
         pltpu.VMEM((SSD_K - 1, ch, ch), BF16),
         pltpu.VMEM((ch, si), F32), pltpu.VMEM((ch, 2 * gn), BF16),
         pltpu.VMEM((ch, si), F32), pltpu.VMEM((ch, 2 * gn), F32),
         pltpu.VMEM((ch, si), F32), pltpu.VMEM((ch, si), F32), pltpu.VMEM((ch, 2 * gn), F32),
         pltpu.VMEM((npair, STATE, LANES), F32),
         pltpu.VMEM((ch, ch), F32), pltpu.VMEM((ch, ch), F32),
         pltpu.VMEM((ch, LANES), F32), pltpu.VMEM((LANES, ch), F32),
         pltpu.VMEM((ch, LANES), F32), pltpu.VMEM((LANES, ch), F32),
         pltpu.VMEM((ch, LANES), F32), pltpu.VMEM((1, LANES), F32)],
        [proj, proj, proj, pre_xs, pre_bc, dt_raw, ys, hsave, dy, kw_xs, kw_bc, dtb, aneg, dskip, normw],
        {}, rider, ("arbitrary",), name)


def final_loss(x, nw, tgt, name):
    t, d = x.shape
    tm = _tile(t, 512, 8)

    def body(x_ref, nw_ref, t_ref, dx_ref, dnw_ref, ls_ref, dxb_ref):
        xf = x_ref[...]
        r = lax.rsqrt(jnp.mean(xf * xf, axis=-1, keepdims=True) + EPS)
        nx = xf * r
        e = nx * nw_ref[...] - t_ref[...]
        dyv = e * (1.0 / d)
        dn = dyv * nw_ref[...]
        dxv = r * (dn - nx * jnp.mean(dn * nx, axis=-1, keepdims=True))
        dx_ref[...] = dxv
        dxb_ref[...] = dxv.astype(BF16)

        @pl.when(pl.program_id(0) == 0)
        def _():
            dnw_ref[...] = jnp.zeros_like(dnw_ref)
            ls_ref[...] = jnp.zeros_like(ls_ref)

        dnw_ref[...] += jnp.sum(dyv * nx, axis=0, keepdims=True)
        ls_ref[...] += jnp.sum(e * e, axis=0, keepdims=True) * (0.5 / d)

    return pl.pallas_call(
        body, name=name, grid=(t // tm,),
        in_specs=[pl.BlockSpec((tm, d), lambda i: (i, 0)), pl.BlockSpec((1, d), lambda i: (0, 0)),
                  pl.BlockSpec((tm, d), lambda i: (i, 0))],
        out_specs=[pl.BlockSpec((tm, d), lambda i: (i, 0)), pl.BlockSpec((1, d), lambda i: (0, 0)),
                   pl.BlockSpec((1, d), lambda i: (0, 0)), pl.BlockSpec((tm, d), lambda i: (i, 0))],
        out_shape=[jax.ShapeDtypeStruct((t, d), F32), jax.ShapeDtypeStruct((1, d), F32),
                   jax.ShapeDtypeStruct((1, d), F32), jax.ShapeDtypeStruct((t, d), BF16)],
        compiler_params=_params(("arbitrary",)),
    )(x, nw, tgt)


def _rows3(a):
    if a.ndim == 1:
        return a.reshape(1, 1, a.shape[0])
    if a.ndim == 2:
        return a.reshape(1, *a.shape)
    return a.reshape(-1, a.shape[-2], a.shape[-1])


def adamw(w, g, m, v, name):
    shape = w.shape
    views = [_rows3(a) for a in (w, g, m, v)]
    b, r, c = views[0].shape
    tr = _tile(r, 256, 16) if r % 16 == 0 else r

    def body(w_ref, g_ref, m_ref, v_ref, d_ref, nm_ref, nv_ref):
        g = g_ref[...]
        m = ADAM_B1 * m_ref[...] + (1.0 - ADAM_B1) * g
        v = ADAM_B2 * v_ref[...] + (1.0 - ADAM_B2) * (g * g)
        m_hat = m / (1.0 - ADAM_B1 ** ADAM_STEP)
        v_hat = v / (1.0 - ADAM_B2 ** ADAM_STEP)
        d_ref[...] = -ADAM_LR * (m_hat / (jnp.sqrt(v_hat) + ADAM_EPS) + ADAM_WD * w_ref[...])
        nm_ref[...] = m
        nv_ref[...] = v

    spec = pl.BlockSpec((1, tr, c), lambda i, j: (i, j, 0))
    outs = pl.pallas_call(
        body, name=name, grid=(b, r // tr), in_specs=[spec] * 4, out_specs=[spec] * 3,
        out_shape=[jax.ShapeDtypeStruct((b, r, c), F32)] * 3,
        compiler_params=_params(("parallel", "parallel")),
    )(*views)
    return [o.reshape(shape) for o in outs]


def adamw_halves(w, g_mine, g_theirs, m, v, name):
    nl, r, c = w.shape
    r2 = r // 2
    tr = _tile(r2, 256, 16)
    nb = r2 // tr

    def body(w_ref, gm_ref, gt_ref, m_ref, v_ref, g_ref, d_ref, nm_ref, nv_ref):
        mine = (pl.program_id(1) // nb) == lax.axis_index("c")
        g = jnp.where(mine, gm_ref[...], gt_ref[...])
        m = ADAM_B1 * m_ref[...] + (1.0 - ADAM_B1) * g
        v = ADAM_B2 * v_ref[...] + (1.0 - ADAM_B2) * (g * g)
        m_hat = m / (1.0 - ADAM_B1 ** ADAM_STEP)
        v_hat = v / (1.0 - ADAM_B2 ** ADAM_STEP)
        g_ref[...] = g
        d_ref[...] = -ADAM_LR * (m_hat / (jnp.sqrt(v_hat) + ADAM_EPS) + ADAM_WD * w_ref[...])
        nm_ref[...] = m
        nv_ref[...] = v

    whole = pl.BlockSpec((1, tr, c), lambda l, i: (l, i, 0))
    half = pl.BlockSpec((1, tr, c), lambda l, i: (l, i % nb, 0))
    return pl.pallas_call(
        body, name=name, grid=(nl, 2 * nb), in_specs=[whole, half, half, whole, whole], out_specs=[whole] * 4,
        out_shape=[jax.ShapeDtypeStruct((nl, r, c), F32)] * 4,
        compiler_params=_params(("parallel", "parallel")),
    )(w, g_mine, g_theirs, m, v)


def _coords():
    return lax.axis_index("x"), lax.axis_index("y"), lax.axis_index("c")


def _ici_peers(x, y):
    chips = [(1 - x, y), (x, 1 - y), (1 - x, 1 - y)]
    return chips, [2 * cx + cy for cx, cy in chips]


def _place(ref, how, chip, layers, per):
    if how == "lead":
        return ref.at[chip, layers]
    start = pl.multiple_of(chip * per, per)
    if how == "rows":
        return ref.at[layers, pl.ds(start, per), :]
    return ref.at[layers, :, pl.ds(start, per)]


def gather_weights(shards, hows, name):
    na = len(shards)
    out_shape = []
    for s, how in zip(shards, hows):
        assert s.shape[0] % 2 == 0
        if how == "lead":
            shp = (N_CHIPS, *s.shape)
        elif how == "rows":
            shp = (s.shape[0], N_CHIPS * s.shape[1], s.shape[2])
        else:
            shp = (s.shape[0], s.shape[1], N_CHIPS * s.shape[2])
        out_shape.append(jax.ShapeDtypeStruct(shp, s.dtype))

    def body(*refs):
        ins = refs[:na]
        outs = refs[na:2 * na]
        send_sems, recv_sems = refs[2 * na:]
        x, y, c = _coords()
        me = 2 * x + y
        chips, chip_ids = _ici_peers(x, y)
        sibling = (x, y, 1 - c)

        def dst(a, chip, layers):
            per = {"lead": 0, "rows": ins[a].shape[1], "cols": ins[a].shape[-1]}[hows[a]]
            return _place(outs[a], hows[a], chip, layers, per)

        def copy(a, k, src, dst_ref, to):
            return pltpu.make_async_remote_copy(
                src_ref=src, dst_ref=dst_ref, send_sem=send_sems.at[7 * a + k], recv_sem=recv_sems.at[7 * a + k],
                device_id=to, device_id_type=MESH)

        started = []
        halves = []
        for a in range(na):
            nl = ins[a].shape[0]
            hl = nl // 2
            mine = pl.ds(c * hl, hl)
            theirs = pl.ds((1 - c) * hl, hl)
            halves.append((mine, theirs))
            for k in range(3):
                cp = copy(a, k, ins[a].at[mine], dst(a, me, mine), (*chips[k], c))
                cp.start()
                started.append(cp)
            own = copy(a, 6, ins[a], dst(a, me, pl.ds(0, nl)), sibling)
            own.start()
            started.append(own)
        for a in range(na):
            mine, _ = halves[a]
            for k in range(3):
                landed = dst(a, chip_ids[k], mine)
                copy(a, k, landed, landed, (*chips[k], c)).wait_recv()
                fw = copy(a, 3 + k, landed, landed, sibling)
                fw.start()
                started.append(fw)
        for a in range(na):
            _, theirs = halves[a]
            for k in range(3):
                got = dst(a, chip_ids[k], theirs)
                copy(a, 3 + k, got, got, sibling).wait_recv()
            whole = dst(a, me, pl.ds(0, ins[a].shape[0]))
            copy(a, 6, whole, whole, sibling).wait_recv()
        for cp in started:
            cp.wait_send()

    return pl.pallas_call(
        body, name=name, in_specs=_any_specs(na), out_specs=_any_specs(na), out_shape=out_shape,
        scratch_shapes=[pltpu.SemaphoreType.DMA((7 * na,)), pltpu.SemaphoreType.DMA((7 * na,))],
        compiler_params=pltpu.CompilerParams(has_side_effects=True),
    )(*shards)


def _remote(src, dst, send_sems, recv_sems, k, to):
    return pltpu.make_async_remote_copy(src_ref=src, dst_ref=dst, send_sem=send_sems.at[k], recv_sem=recv_sems.at[k],
                                        device_id=to, device_id_type=MESH)


LAYER_HOW = ("lead", "rows", "cols", "rows")


def _layer_place(ref, how, chip, shard_shape, start, size):
    r, c = shard_shape
    if how == "lead":
        return ref.at[chip, :, pl.ds(start, size), :]
    if how == "rows":
        return ref.at[:, pl.ds(pl.multiple_of(chip * r + start, HALO), size), :]
    return ref.at[:, pl.ds(start, size), pl.ds(pl.multiple_of(chip * c, LANES), c)]


def weight_rider_ici(shards, hows, layer):
    shapes = [tuple(s.shape[1:]) for s in shards]
    out_shapes = []
    for (r, c), how, s in zip(shapes, hows, shards):
        shp = {"lead": (N_CHIPS, 1, r, c), "rows": (1, N_CHIPS * r, c), "cols": (1, r, N_CHIPS * c)}[how]
        out_shapes.append(jax.ShapeDtypeStruct(shp, s.dtype))

    def copies(ins, outs, send_sems, recv_sems):
        x, y, c = _coords()
        me = 2 * x + y
        chips, chip_ids = _ici_peers(x, y)
        sibling = (x, y, 1 - c)
        pairs = []
        for a, (shape, how) in enumerate(zip(shapes, hows)):
            half = shape[0] // 2
            mine = pl.multiple_of(c * half, HALO)
            src = ins[a].at[pl.ds(layer, 1)]
            for k in range(3):
                to = (*chips[k], c)
                land = _layer_place(outs[a], how, chip_ids[k], shape, mine, half)
                pairs.append((_remote(src.at[:, pl.ds(mine, half), :], _layer_place(outs[a], how, me, shape, mine, half),
                                      send_sems, recv_sems, 4 * a + k, to),
                              _remote(land, land, send_sems, recv_sems, 4 * a + k, to)))
            whole = _layer_place(outs[a], how, me, shape, 0, shape[0])
            pairs.append((_remote(src, whole, send_sems, recv_sems, 4 * a + 3, sibling),
                          _remote(whole, whole, send_sems, recv_sems, 4 * a + 3, sibling)))
        return pairs

    return Rider(list(shards), out_shapes, {}, 4 * len(shards), copies)


def weight_rider_d2d(bufs, shapes, hows):
    def copies(ins, outs, send_sems, recv_sems):
        x, y, c = _coords()
        _, chip_ids = _ici_peers(x, y)
        sibling = (x, y, 1 - c)
        pairs = []
        for a, (shape, how) in enumerate(zip(shapes, hows)):
            half = shape[0] // 2
            mine = pl.multiple_of(c * half, HALO)
            theirs = pl.multiple_of((1 - c) * half, HALO)
            for k in range(3):
                land = _layer_place(outs[a], how, chip_ids[k], shape, theirs, half)
                pairs.append((_remote(_layer_place(ins[a], how, chip_ids[k], shape, mine, half),
                                      _layer_place(outs[a], how, chip_ids[k], shape, mine, half),
                                      send_sems, recv_sems, 3 * a + k, sibling),
                              _remote(land, land, send_sems, recv_sems, 3 * a + k, sibling)))
        return pairs

    return Rider(list(bufs), [jax.ShapeDtypeStruct(b.shape, b.dtype) for b in bufs],
                 {a: a for a in range(len(bufs))}, 3 * len(bufs), copies)


def grads_rider_sibling(arrs):
    def copies(ins, outs, send_sems, recv_sems):
        x, y, c = _coords()
        sibling = (x, y, 1 - c)
        pairs = []
        for a in range(len(arrs)):
            r2 = ins[a].shape[1] // 2
            src = ins[a].at[:, pl.ds(pl.multiple_of((1 - c) * r2, 8), r2), :]
            pairs.append((_remote(src, outs[a], send_sems, recv_sems, a, sibling),
                          _remote(outs[a], outs[a], send_sems, recv_sems, a, sibling)))
        return pairs

    return Rider(list(arrs), [jax.ShapeDtypeStruct((a.shape[0], a.shape[1] // 2, a.shape[2]), a.dtype) for a in arrs],
                 {}, len(arrs), copies)


def chip_sum(g, recv, name):
    nch, r, c = g.shape
    r2 = r // 2
    tr = _tile(r2, 256, 16)
    nb = r2 // tr

    def body(g0_ref, g1_ref, r_ref, o32_ref, o16_ref):
        s = jnp.where(lax.axis_index("c") == 0, g0_ref[...], g1_ref[...]) + r_ref[...]
        o32_ref[...] = s
        o16_ref[...] = s.astype(BF16)

    here = pl.BlockSpec((1, tr, c), lambda i, j: (i, j, 0))
    return pl.pallas_call(
        body, name=name, grid=(nch, nb),
        in_specs=[here, pl.BlockSpec((1, tr, c), lambda i, j: (i, nb + j, 0)), here],
        out_specs=[here, here],
        out_shape=[jax.ShapeDtypeStruct((nch, r2, c), F32), jax.ShapeDtypeStruct((nch, r2, c), BF16)],
        compiler_params=_params(("parallel", "parallel")),
    )(g, g, recv)


def grads_rider_chips(arrs):
    def copies(ins, outs, send_sems, recv_sems):
        x, y, c = _coords()
        chips, chip_ids = _ici_peers(x, y)
        pairs = []
        for a in range(len(arrs)):
            for k in range(3):
                to = (*chips[k], c)
                pairs.append((_remote(ins[a].at[chip_ids[k]], outs[a].at[k], send_sems, recv_sems, 3 * a + k, to),
                              _remote(outs[a].at[k], outs[a].at[k], send_sems, recv_sems, 3 * a + k, to)))
        return pairs

    return Rider(list(arrs), [jax.ShapeDtypeStruct((3, *a.shape[1:]), a.dtype) for a in arrs], {}, 3 * len(arrs),
                 copies)


def grad_sum(p32, recv, layer, nl, buf, name):
    _, r2, c = p32.shape
    tr = _tile(r2, 256, 16)
    nb = r2 // tr

    def body(p0_ref, p1_ref, p2_ref, p3_ref, r0_ref, r1_ref, r2_ref, *rest):
        o_ref = rest[-1]
        x, y, _ = _coords()
        chip = 2 * x + y
        own = jnp.where(chip == 0, p0_ref[...], jnp.where(chip == 1, p1_ref[...],
                                                        jnp.where(chip == 2, p2_ref[...], p3_ref[...])))
        o_ref[...] = own + r0_ref[...].astype(F32) + r1_ref[...].astype(F32) + r2_ref[...].astype(F32)

    slot = lambda k: pl.BlockSpec((1, tr, c), lambda j: (k, j, 0))
    in_specs = [slot(k) for k in range(N_CHIPS)] + [slot(k) for k in range(3)]
    args = [p32] * N_CHIPS + [recv] * 3
    aliases = {}
    if buf is not None:
        in_specs.append(pl.BlockSpec(memory_space=pl.ANY))
        args.append(buf)
        aliases = {len(args) - 1: 0}
    return pl.pallas_call(
        body, name=name, grid=(nb,), in_specs=in_specs,
        out_specs=pl.BlockSpec((1, tr, c), lambda j: (layer, j, 0)),
        out_shape=jax.ShapeDtypeStruct((nl, r2, c), F32),
        input_output_aliases=aliases,
        compiler_params=_params(("parallel",)),
    )(*args)


def grads_rider_exchange(bufs):
    def copies(ins, outs, send_sems, recv_sems):
        x, y, c = _coords()
        sibling = (x, y, 1 - c)
        return [(_remote(ins[a], outs[a], send_sems, recv_sems, a, sibling),
                 _remote(outs[a], outs[a], send_sems, recv_sems, a, sibling)) for a in range(len(bufs))]

    return Rider(list(bufs), [jax.ShapeDtypeStruct(b.shape, b.dtype) for b in bufs], {}, len(bufs), copies)


def allreduce_small(buf, name):
    r, cdim = buf.shape

    def body(x_ref, o_ref, gath, send_sems, recv_sems):
        x, y, c = _coords()
        me, sibling = (x, y, c), (x, y, 1 - c)
        chips, _ = _ici_peers(x, y)

        def slot(px, py, pc):
            return gath.at[4 * px + 2 * py + pc]

        def copy(k, block, to, src=None):
            return pltpu.make_async_remote_copy(
                src_ref=slot(*block) if src is None else src, dst_ref=slot(*block),
                send_sem=send_sems.at[k], recv_sem=recv_sems.at[k], device_id=to, device_id_type=MESH)

        gath[4 * x + 2 * y + c] = x_ref[...]
        first = [copy(0, me, sibling, src=x_ref)]
        first += [copy(1 + j, me, (*chip, c), src=x_ref) for j, chip in enumerate(chips)]
        for cp in first:
            cp.start()
        passed = [copy(4 + j, (*chip, c), sibling) for j, chip in enumerate(chips)]
        for j, chip in enumerate(chips):
            copy(1 + j, (*chip, c), me).wait_recv()
            passed[j].start()
        copy(0, sibling, me).wait_recv()
        for j, chip in enumerate(chips):
            copy(4 + j, (*chip, 1 - c), me).wait_recv()
        for cp in first + passed:
            cp.wait_send()
        acc = gath[0]
        for d in range(1, 8):
            acc = acc + gath[d]
        o_ref[...] = acc

    return pl.pallas_call(
        body, name=name,
        in_specs=[pl.BlockSpec(memory_space=pltpu.VMEM)], out_specs=pl.BlockSpec(memory_space=pltpu.VMEM),
        out_shape=jax.ShapeDtypeStruct((r, cdim), F32),
        scratch_shapes=[pltpu.VMEM((8, r, cdim), F32), pltpu.SemaphoreType.DMA((7,)), pltpu.SemaphoreType.DMA((7,))],
        compiler_params=pltpu.CompilerParams(has_side_effects=True),
    )(buf)


def _expand_heads(v):
    return jnp.repeat(v.astype(F32), HEAD_DIM).reshape(1, -1)


def _pad_lanes(v):
    return jnp.pad(v.astype(F32), (0, LANES - v.shape[0])).reshape(1, LANES)


def local_step(x, tgt, p, comm, cols):
    nl = p["norm_mix_w"].shape[0]
    d = x.shape[1]
    cw = p["short_conv_w"].shape[2]
    si = p["ssd_norm_w"].shape[1]
    ff, npad = comm.ff, comm.npad
    nh = si // HEAD_DIM
    gn = GROUPS * STATE
    dt_off = 3 * cw + si + si + 2 * gn
    assert cols == dt_off + nh and nh <= LANES and dt_off % LANES == 0 and npad == dt_off + LANES
    pieces = [(0, cw), (cw, cw), (2 * cw, cw), (3 * cw, si), (3 * cw + si, si), (3 * cw + 2 * si, 2 * gn),
              (dt_off, LANES)]

    saved = []
    for l in range(nl):
        nw1 = p["norm_mix_w"][l].reshape(1, d)
        nw2 = p["norm_mlp_w"][l].reshape(1, d)
        kw3 = p["short_conv_w"][l]
        kwx, kwb = p["ssd_conv_w"][l][:, :si], p["ssd_conv_w"][l][:, si:]
        bx, bb = p["ssd_conv_b"][l][:si].reshape(1, si), p["ssd_conv_b"][l][si:].reshape(1, 2 * gn)
        dtb = _pad_lanes(p["dt_bias"][l])
        aneg = _pad_lanes(-jnp.exp(p["a_log"][l]))
        dsk = _expand_heads(p["d_skip"][l])
        snw = p["ssd_norm_w"][l].reshape(1, si)
        ssd_args = (kwx, kwb, bx, bb, dtb, aneg, dsk, snw)

        w_in = comm.weight(l, "w_in")
        (proj, h, dt_raw), sent = norm_matmul(x, nw1, w_in, 0, dt_off, BF16, "in_proj", tail_block=dt_off // LANES,
                                              rider=comm.rider("in_proj", l))
        comm.done("in_proj", l, sent)
        y_mix = conv_mixer_fwd(proj, kw3, cw, cw + si, "conv_mixer_fwd")
        (y_mix, *ssd_saved), sent = ssd_fwd(proj, dt_raw, y_mix, *ssd_args, cw, si, "ssd_fwd",
                                            rider=comm.rider("ssd_fwd", l))
        comm.done("ssd_fwd", l, sent)
        x2, _ = matmul(y_mix, comm.weight(l, "w_out"), 0, False, d, F32, "out_proj", residual=x)
        (up, h2), sent = norm_matmul(x2, nw2, comm.weight(l, "w_up"), 0, ff, BF16, "up_proj",
                                     rider=comm.rider("up_proj", l))
        comm.done("up_proj", l, sent)
        x3, sent = matmul(up, comm.weight(l, "w_down"), 0, False, d, F32, "down_proj", lhs_fn=_relu2, residual=x2,
                          rider=comm.rider("down_proj", l))
        comm.done("down_proj", l, sent)
        saved.append((x, h, proj, dt_raw, y_mix, ssd_saved, x2, h2, up, nw1, nw2, kw3, ssd_args))
        x = x3

    dx, dwf, lvec, dxb = final_loss(x, p["final_norm_w"].reshape(1, d), tgt, "final_loss")
    loss = jnp.sum(lvec)

    names = ("norm_mix_w", "short_conv_w", "ssd_conv_w", "ssd_conv_b", "dt_bias", "a_log", "d_skip",
             "ssd_norm_w", "norm_mlp_w")
    grads = {k: [None] * nl for k in names}
    for l in reversed(range(nl)):
        x0, h, proj, dt_raw, y_mix, ssd_saved, x2, h2, up, nw1, nw2, kw3, ssd_args = saved[l]
        kwx, kwb, _, _, dtb, aneg, dsk, snw = ssd_args
        dup, sent = matmul(dxb, comm.weight(l, "w_down"), 0, True, ff, BF16, "down_bwd", relu_gate=up,
                           rider=comm.rider("down_bwd", l))
        comm.done("down_bwd", l, sent)
        g_down, sums = matmul_tn(up, dxb, "down_wgrad", a_fn=_relu2, chip_sums=comm.side("down_wgrad", l))
        comm.side_done("down_wgrad", l, sums)
        comm.take_gradient(l, "w_down", g_down)
        (dx2, dnw2, dx2b), _ = matmul_normbwd([(dup, 0)], [(0, ff)], comm.weight(l, "w_up"), 0, x2, nw2, dx, "up_bwd")
        comm.take_gradient(l, "w_up", matmul_tn(h2, dup, "up_wgrad", by_chip=True))
        grads["norm_mlp_w"][l] = dnw2.reshape(d)
        dy, sent = matmul(dx2b, comm.weight(l, "w_out"), 0, True, cw + si, BF16, "out_bwd",
                          rider=comm.rider("out_bwd", l))
        comm.done("out_bwd", l, sent)
        g_out, sums = matmul_tn(y_mix, dx2b, "out_wgrad", chip_sums=comm.side("out_wgrad", l))
        comm.side_done("out_wgrad", l, sums)
        comm.take_gradient(l, "w_out", g_out)
        du, dkw3 = conv_mixer_bwd(proj, dy, kw3, cw, "conv_mixer_bwd")
        (dssd, ddt, dkwx, dkwb, dbx, dbb, ddtb, da, ddsk, dsnw), sent = ssd_bwd(
            proj, dt_raw, *ssd_saved, dy, kwx, kwb, dtb, aneg, dsk, snw, cw, si, "ssd_bwd",
            rider=comm.rider("ssd_bwd", l))
        comm.done("ssd_bwd", l, sent)
        views = [(du, 0), (du, 1), (du, 2), (dssd, 0), (dssd, 1), (dssd, 2 * si // (2 * gn)), (ddt, 0)]
        (dxl, dnw1, dxb), sent = matmul_normbwd(views, pieces, comm.weight(l, "w_in"), 0, x0, nw1, dx2, "in_bwd",
                                           rider=comm.rider("in_bwd", l))
        comm.done("in_bwd", l, sent)
        parts = []
        for i, dp in enumerate((du, dssd, ddt)):
            part, bufs = matmul_tn(h, dp, "in_wgrad_%d" % i, grad_sums=comm.reductions("in_wgrad_%d" % i))
            comm.reductions_done("in_wgrad_%d" % i, bufs)
            parts.append(part)
        comm.take_gradient(l, "w_in", split_to_chips(parts, cols, "in_wgrad_split"))
        grads["norm_mix_w"][l] = dnw1.reshape(d)
        grads["short_conv_w"][l] = dkw3
        grads["ssd_conv_w"][l] = jnp.concatenate([dkwx, dkwb], axis=1)
        grads["ssd_conv_b"][l] = jnp.concatenate([dbx, dbb], axis=1).reshape(-1)
        grads["dt_bias"][l] = ddtb[0, :nh]
        grads["a_log"][l] = da[0, :nh] * aneg[0, :nh]
        grads["d_skip"][l] = jnp.sum(ddsk.reshape(nh, HEAD_DIM), axis=1)
        grads["ssd_norm_w"][l] = dsnw.reshape(si)
        dx = dxl

    grads = {k: jnp.stack(v) for k, v in grads.items()}
    grads["final_norm_w"] = dwf.reshape(d)
    return loss, dx, grads


BIG = ("w_in", "w_out", "w_up", "w_down")
SMALL_SHARDED = ("short_conv_w", "ssd_conv_w")
SMALL_REPL = ("norm_mix_w", "ssd_conv_b", "dt_bias", "a_log", "d_skip", "ssd_norm_w", "norm_mlp_w", "final_norm_w")
WEIGHTS = ("norm_mix_w", "w_in", "short_conv_w", "ssd_conv_w", "ssd_conv_b", "dt_bias", "a_log", "d_skip",
           "ssd_norm_w", "w_out", "norm_mlp_w", "w_up", "w_down", "final_norm_w")
SMALL_COLS = 1024


def _pack_small(named):
    flat = jnp.concatenate([v.reshape(-1).astype(F32) for v in named])
    n = flat.shape[0]
    rows = -(-n // SMALL_COLS)
    rows = -(-rows // 8) * 8
    return jnp.pad(flat, (0, rows * SMALL_COLS - n)).reshape(rows, SMALL_COLS)


def _unpack_small(buf, like):
    flat = buf.reshape(-1)
    out, off = [], 0
    for v in like:
        out.append(flat[off:off + v.size].reshape(v.shape))
        off += v.size
    return out


class ChipComm:
    IO = ("w_in", "w_out")
    MLP = ("w_up", "w_down")
    FIRST = ("w_in",)
    EARLY = ("w_up", "w_down", "w_out")
    LATE = ("w_out",)

    def __init__(self, shards, nl, npad):
        self.shards, self.nl, self.npad = shards, nl, npad
        self.how = dict(zip(BIG, LAYER_HOW))
        self.ff = N_CHIPS * shards["w_up"].shape[2]
        self.w = {}
        self.landed = {}
        self.grad = {}
        self.sums = {}
        self.from_sibling = {}
        self.from_chips = {}
        self.bufs = {k: None for k in BIG}
        first = run_rider(self._ici(self.FIRST, 0), "gather_first_ici")
        self._gathered(self.FIRST, 0, run_rider(self._d2d(self.FIRST, first), "gather_first_d2d"))

    def _ici(self, group, l):
        return weight_rider_ici([self.shards[k] for k in group], [self.how[k] for k in group], l)

    def _d2d(self, group, landed):
        return weight_rider_d2d(landed, [tuple(self.shards[k].shape[1:]) for k in group],
                                [self.how[k] for k in group])

    def _gathered(self, group, l, arrays):
        for k, g in zip(group, arrays):
            self.w[(l, k)] = join_from_chips(g, self.npad, "w_in_join") if k == "w_in" else g

    def _to_sibling(self, group, l):
        return grads_rider_sibling([self.grad[(l, k)] for k in group])

    def _summed(self, group, l, from_sibling):
        self.sums[group] = (l, [chip_sum(self.grad.pop((l, k)), r, "chip_sum") for k, r in zip(group, from_sibling)])

    def side(self, point, l):
        group = self.IO if point == "down_wgrad" else self.MLP
        if group not in self.from_sibling:
            return ()
        layer, parts = self.from_sibling[group]
        return [(self.grad[(layer, k)], r) for k, r in zip(group, parts)]

    def side_done(self, point, l, sums):
        group = self.IO if point == "down_wgrad" else self.MLP
        if sums:
            layer, _ = self.from_sibling.pop(group)
            for k in group:
                del self.grad[(layer, k)]
            self.sums[group] = (layer, sums)

    def _to_chips(self, groups):
        return grads_rider_chips([s[1] for group in groups for s in self.sums[group][1]])

    def _reduced(self, group, from_chips):
        l, sums = self.sums.pop(group)
        for k, s, r in zip(group, sums, from_chips):
            self.bufs[k] = grad_sum(s[0], r, l, self.nl, self.bufs[k], "grad_sum")

    def weight(self, l, name):
        return self.w[(l, name)]

    def take_gradient(self, l, name, g):
        self.grad[(l, name)] = g if g.ndim == 3 else g.reshape(N_CHIPS, g.shape[0] // N_CHIPS, g.shape[1])

    def rider(self, point, l):
        more = l + 1 < self.nl
        if point == "in_proj":
            return self._ici(self.EARLY if l == 0 else self.MLP, l)
        if point == "ssd_fwd":
            return self._d2d(*self.landed["own layer"])
        if point == "up_proj":
            return self._ici(self.IO, l + 1) if more else None
        if point == "down_proj":
            return self._d2d(self.IO, self.landed[self.IO]) if more else None
        if point == "down_bwd":
            return self._to_sibling(self.IO, l + 1) if more else None
        if point == "out_bwd":
            return self._to_sibling(self.MLP, l)
        if point == "ssd_bwd":
            ahead = self._to_chips([self.IO]) if more else None
            if l > 0:
                return ahead
            own = self._to_sibling(self.LATE, 0)
            return join_riders(ahead, own) if ahead else own
        return self._to_chips([self.MLP] + ([] if l else [self.LATE]))

    def done(self, point, l, results):
        if not results:
            return
        if point == "in_proj":
            self.landed["own layer"] = (self.EARLY if l == 0 else self.MLP, results)
        elif point == "up_proj":
            self.landed[self.IO] = results
        elif point == "ssd_fwd":
            self._gathered(self.landed.pop("own layer")[0], l, results)
        elif point == "down_proj":
            self._gathered(self.IO, l + 1, results)
        elif point == "down_bwd":
            self.from_sibling[self.IO] = (l + 1, results)
        elif point == "out_bwd":
            self.from_sibling[self.MLP] = (l, results)
        elif point == "ssd_bwd":
            if l > 0:
                self.from_chips[self.IO] = results
            else:
                if len(results) > len(self.LATE):
                    self.from_chips[self.IO] = results[:len(self.IO)]
                self._summed(self.LATE, 0, results[-len(self.LATE):])
        else:
            self.from_chips[self.MLP] = results[:len(self.MLP)]
            if l == 0:
                self.from_chips[self.LATE] = results[len(self.MLP):]

    def _pending_groups(self, point):
        groups = [self.IO] if point == "in_wgrad_0" else [self.MLP, self.LATE] if point == "in_wgrad_1" else []
        return [g for g in groups if g in self.from_chips]

    def reductions(self, point):
        items = []
        for group in self._pending_groups(point):
            layer, sums = self.sums[group]
            items += [(s[0], r, layer, self.nl, self.bufs[k]) for k, s, r in zip(group, sums, self.from_chips[group])]
        return items

    def reductions_done(self, point, bufs):
        if bufs:
            names = [k for group in self._pending_groups(point) for k in group]
            for group in self._pending_groups(point):
                del self.sums[group], self.from_chips[group]
            self.bufs.update(zip(names, bufs))

    def finish(self):
        self._summed(self.FIRST, 0, run_rider(self._to_sibling(self.FIRST, 0), "grads_to_sibling"))
        ready = self.MLP + self.LATE
        done = [self.bufs[k] for k in ready]
        arrived = run_rider(join_riders(self._to_chips([self.FIRST]), grads_rider_exchange(done)), "grads_to_chips")
        self._reduced(self.FIRST, arrived[:len(self.FIRST)])
        last = [self.bufs[k] for k in self.FIRST]
        theirs = dict(zip(ready + self.FIRST, arrived[len(self.FIRST):] + run_rider(grads_rider_exchange(last),
                                                                                    "grads_exchange")))
        return {k: (self.bufs[k], theirs[k]) for k in BIG}


def kernel(x, norm_mix_w, w_in, short_conv_w, ssd_conv_w, ssd_conv_b, dt_bias, a_log, d_skip, ssd_norm_w, w_out, norm_mlp_w, w_up, w_down, final_norm_w, loss_target, m_norm_mix_w, m_w_in, m_short_conv_w, m_ssd_conv_w, m_ssd_conv_b, m_dt_bias, m_a_log, m_d_skip, m_ssd_norm_w, m_w_out, m_norm_mlp_w, m_w_up, m_w_down, m_final_norm_w, v_norm_mix_w, v_w_in, v_short_conv_w, v_ssd_conv_w, v_ssd_conv_b, v_dt_bias, v_a_log, v_d_skip, v_ssd_norm_w, v_w_out, v_norm_mlp_w, v_w_up, v_w_down, v_final_norm_w):
    w = dict(norm_mix_w=norm_mix_w, w_in=w_in, short_conv_w=short_conv_w, ssd_conv_w=ssd_conv_w,
             ssd_conv_b=ssd_conv_b, dt_bias=dt_bias, a_log=a_log, d_skip=d_skip, ssd_norm_w=ssd_norm_w, w_out=w_out,
             norm_mlp_w=norm_mlp_w, w_up=w_up, w_down=w_down, final_norm_w=final_norm_w)
    m = dict(norm_mix_w=m_norm_mix_w, w_in=m_w_in, short_conv_w=m_short_conv_w, ssd_conv_w=m_ssd_conv_w,
             ssd_conv_b=m_ssd_conv_b, dt_bias=m_dt_bias, a_log=m_a_log, d_skip=m_d_skip, ssd_norm_w=m_ssd_norm_w,
             w_out=m_w_out, norm_mlp_w=m_norm_mlp_w, w_up=m_w_up, w_down=m_w_down, final_norm_w=m_final_norm_w)
    v = dict(norm_mix_w=v_norm_mix_w, w_in=v_w_in, short_conv_w=v_short_conv_w, ssd_conv_w=v_ssd_conv_w,
             ssd_conv_b=v_ssd_conv_b, dt_bias=v_dt_bias, a_log=v_a_log, d_skip=v_d_skip, ssd_norm_w=v_ssd_norm_w,
             w_out=v_w_out, norm_mlp_w=v_norm_mlp_w, w_up=v_w_up, w_down=v_w_down, final_norm_w=v_final_norm_w)
    xi, yi, ci = lax.axis_index("x"), lax.axis_index("y"), lax.axis_index("c")
    chip = 2 * xi + yi
    nl = w_up.shape[0]
    cols = N_CHIPS * w_in.shape[2]
    npad = cols // LANES * LANES + LANES

    full = dict(w)
    small_gathered = gather_weights([w[k] for k in SMALL_SHARDED], ["lead"] * len(SMALL_SHARDED), "gather_small")
    for k, g4 in zip(SMALL_SHARDED, small_gathered):
        full[k] = jnp.concatenate([g4[j] for j in range(N_CHIPS)], axis=2)
    comm = ChipComm({k: w[k].astype(BF16) for k in BIG}, nl, npad)

    loss, grad_x, grads = local_step(x[0], loss_target[0], full, comm, cols)
    loss = lax.psum(loss, ("x", "y", "c"))
    halves = comm.finish()
    g_shard = {}

    small_names = SMALL_REPL + SMALL_SHARDED
    small_sum = allreduce_small(_pack_small([grads[k] for k in small_names]), "allreduce_small")
    for k, g in zip(small_names, _unpack_small(small_sum, [grads[k] for k in small_names])):
        if k in SMALL_SHARDED:
            width = w[k].shape[2]
            g = lax.dynamic_slice_in_dim(g, chip * width, width, axis=2)
        g_shard[k] = g

    delta, new_m, new_v = {}, {}, {}
    for k in BIG:
        g_shard[k], delta[k], new_m[k], new_v[k] = adamw_halves(w[k], *halves[k], m[k], v[k], "adamw_%s" % k)
    packed = [_pack_small([d_[k] for k in small_names]) for d_ in (w, g_shard, m, v)]
    outs = adamw(*packed, "adamw_small")
    for d_, buf in zip((delta, new_m, new_v), outs):
        for k, val in zip(small_names, _unpack_small(buf, [w[k] for k in small_names])):
            d_[k] = val

    return (loss, grad_x[None], *[g_shard[k] for k in WEIGHTS], *[delta[k] for k in WEIGHTS],
            *[new_m[k] for k in WEIGHTS], *[new_v[k] for k in WEIGHTS])
```
